```python
import math
import jax, jax.numpy as jnp
from jax import lax
import numpy as np

D_MODEL = 1024
BATCH = 8
SEQ = 2048
DEPTH = 2

GRID_W = 64
CTX_LEN = 256
EPS = 1e-6
NEG_INF = -1e30
f32 = jnp.float32

SSD_HEADS = 16
SSD_HEAD_DIM = 64
SSD_GROUPS = 2
SSD_STATE = 128
SSD_CONV = 5
SSD_CHUNK = 128
SSD_INNER = SSD_HEADS * SSD_HEAD_DIM
SSD_BC = SSD_GROUPS * SSD_STATE
SSD_XBC = SSD_INNER + 2 * SSD_BC
ATT_HEADS = 16
ATT_KV_HEADS = 4
ATT_HEAD_DIM = 64
WINDOW = 128
ATT_BLOCK = 128
ROPE_THETA = 10000.0
ATT_Q = ATT_HEADS * ATT_HEAD_DIM
ATT_KV = 2 * ATT_KV_HEADS * ATT_HEAD_DIM
EVEN_SPLITS = [SSD_INNER,
               SSD_INNER + SSD_XBC,
               SSD_INNER + SSD_XBC + 2 * SSD_HEADS,
               SSD_INNER + SSD_XBC + 2 * SSD_HEADS + ATT_Q,
               SSD_INNER + SSD_XBC + 2 * SSD_HEADS + ATT_Q + ATT_KV]
EVEN_IN = SSD_INNER + SSD_XBC + 2 * SSD_HEADS + ATT_Q + ATT_KV + ATT_Q
EVEN_OUT = SSD_INNER + ATT_Q
S5_WIDTH = 1024
S5_GROUP_CH = 16
S5_GROUPS = S5_WIDTH // S5_GROUP_CH
S5_STATE = 64

N_EVEN = (DEPTH + 1) // 2
N_ODD = DEPTH // 2

kernel_name = "hybrid_ssd_swa_s5_prefix_dit"


def rmsnorm(x, w):
    xf = x.astype(f32)
    y = xf * lax.rsqrt(jnp.mean(xf * xf, axis=-1, keepdims=True) + EPS)
    return (y * w.astype(f32)).astype(x.dtype)


def adaln(cvec, w, b):
    mod = (jax.nn.silu(cvec) @ w + b)[..., None, :]
    shift, scale, gate = jnp.split(mod, 3, axis=-1)
    return shift, scale, gate


def norm_mod(x, w, shift, scale):
    return rmsnorm(x, w) * (1 + scale) + shift


def sink_softmax(scores, sink_b):
    sizes = [s.shape[-1] for s in scores]
    sink_col = jnp.broadcast_to(sink_b, scores[0].shape[:-1] + (1,))
    p = jax.nn.softmax(jnp.concatenate(scores + [sink_col], axis=-1), axis=-1)
    return jnp.split(p[..., :-1], np.cumsum(sizes)[:-1].tolist(), axis=-1)


def dwconv(x, w, b):
    ch = x.shape[-1]
    y = lax.conv_general_dilated(x, w[:, None, :].astype(x.dtype), window_strides=(1,),
                                 padding=[(SSD_CONV // 2, SSD_CONV // 2)],
                                 dimension_numbers=('NWC', 'WIO', 'NWC'),
                                 feature_group_count=ch)
    return y + b


def ssd_inputs(xbc_raw, dt_raw, conv_w, conv_b, dt_bias):
    bsz, L, _ = xbc_raw.shape
    xbc = jax.nn.silu(dwconv(xbc_raw, conv_w, conv_b))
    xs, bm, cm = jnp.split(xbc, [SSD_INNER, SSD_INNER + SSD_BC], axis=-1)
    xs = xs.reshape(bsz, L, SSD_HEADS, SSD_HEAD_DIM)
    bm = bm.reshape(bsz, L, SSD_GROUPS, SSD_STATE)
    cm = cm.reshape(bsz, L, SSD_GROUPS, SSD_STATE)
    dt = jax.nn.softplus((dt_raw.reshape(bsz, L, 2, SSD_HEADS) + dt_bias).astype(f32))
    return xs, bm, cm, dt


def ssd_scan(x, dt, a, bm, cm, h0):
    bsz, L, H, P = x.shape
    G, N = bm.shape[2], bm.shape[3]
    R = H // G
    Q = SSD_CHUNK
    nc = L // Q
    xc = (x * dt[..., None]).reshape(bsz, nc, Q, G, R, P)
    acum = jnp.cumsum((dt * a).reshape(bsz, nc, Q, G, R), axis=2)
    bc = bm.reshape(bsz, nc, Q, G, N)
    cc = cm.reshape(bsz, nc, Q, G, N)
    lower = jnp.tril(jnp.ones((Q, Q), dtype=bool))[:, :, None, None]
    seg = acum[:, :, :, None] - acum[:, :, None, :]
    decay = jnp.exp(jnp.where(lower, seg, NEG_INF))
    cb = jnp.einsum('bclgn,bcsgn->bclsg', cc, bc)
    y_diag = jnp.einsum('bclsg,bclsgr,bcsgrp->bclgrp', cb, decay, xc)
    decay_end = jnp.exp(acum[:, :, -1:] - acum)
    states = jnp.einsum('bclgn,bclgr,bclgrp->bcgrpn', bc, decay_end, xc)
    chunk_decay = jnp.exp(acum[:, :, -1])

    def step(h, inp):
        s, d = inp
        return h * d[..., None, None] + s, h

    h_last, h_prev = lax.scan(step, h0.reshape(bsz, G, R, P, N).astype(states.dtype),
                              (jnp.moveaxis(states, 1, 0), jnp.moveaxis(chunk_decay, 1, 0)))
    h_prev = jnp.moveaxis(h_prev, 0, 1)
    y_off = jnp.einsum('bclgn,bcgrpn,bclgr->bclgrp', cc, h_prev, jnp.exp(acum))
    y = (y_diag + y_off).reshape(bsz, L, H, P)
    return y, h_last.reshape(bsz, H, P, N)


def ssd_bidir(xs, bm, cm, dt, a, h0_f, h0_b):
    flip = lambda t: jnp.flip(t, axis=1)
    y_f, h_f = ssd_scan(xs, dt[:, :, 0], a[0], bm, cm, h0_f)
    y_b, h_b = ssd_scan(flip(xs), flip(dt[:, :, 1]), a[1], flip(bm), flip(cm), h0_b)
    return y_f + flip(y_b), h_f, h_b


def ssd_output(y, xs, z, d_skip, norm_w):
    bsz, L = y.shape[:2]
    y = (y + xs * d_skip[:, None]).reshape(bsz, L, SSD_INNER)
    return rmsnorm(y * jax.nn.silu(z), norm_w)


def axial_rope(L):
    rows = L // GRID_W
    row = jnp.repeat(jnp.arange(rows, dtype=f32), GRID_W)
    col = jnp.tile(jnp.arange(GRID_W, dtype=f32), rows)
    n_freq = ATT_HEAD_DIM // 4
    inv = ROPE_THETA ** (-jnp.arange(n_freq, dtype=f32) / n_freq)
    ang = jnp.concatenate([row[:, None] * inv, col[:, None] * inv], axis=-1)
    return jnp.cos(ang), jnp.sin(ang)


def apply_rope(t, cos, sin):
    half = t.shape[-1] // 2
    t1, t2 = t[..., :half], t[..., half:]
    cos = cos[None, :, None, :].astype(t.dtype)
    sin = sin[None, :, None, :].astype(t.dtype)
    return jnp.concatenate([t1 * cos - t2 * sin, t1 * sin + t2 * cos], axis=-1)


def window_attention(q, k, v, kc, vc, sink):
    bsz, L, Hq, Dh = q.shape
    Hk = k.shape[2]
    R = Hq // Hk
    T = ATT_BLOCK
    nb = L // T
    scale = Dh ** -0.5
    qb = jnp.moveaxis(q.reshape(bsz, nb, T, Hk, R, Dh), 1, 0)

    def band(t):
        tp = jnp.pad(t, ((0, 0), (T, T), (0, 0), (0, 0))).reshape(bsz, nb + 2, T, Hk, Dh)
        w = jnp.concatenate([tp[:, :-2], tp[:, 1:-1], tp[:, 2:]], axis=2)
        return jnp.moveaxis(w, 1, 0)

    kb, vb = band(k), band(v)
    qpos = jnp.arange(nb)[:, None] * T + jnp.arange(T)[None, :]
    kpos = jnp.arange(nb)[:, None] * T - T + jnp.arange(3 * T)[None, :]
    valid = ((jnp.abs(qpos[:, :, None] - kpos[:, None, :]) <= WINDOW)
             & (kpos[:, None, :] >= 0) & (kpos[:, None, :] < L))
    sink_b = sink.reshape(Hk, R)[None, :, :, None, None].astype(f32)

    def block(args):
        qi, ki, vi, mi = args
        s_loc = jnp.einsum('bqhrd,bkhd->bhrqk', qi, ki).astype(f32) * scale
        s_loc = jnp.where(mi[None, None, None], s_loc, NEG_INF)
        s_ctx = jnp.einsum('bqhrd,bmhd->bhrqm', qi, kc).astype(f32) * scale
        p_loc, p_ctx = sink_softmax([s_loc, s_ctx], sink_b)
        return (jnp.einsum('bhrqk,bkhd->bqhrd', p_loc.astype(vi.dtype), vi)
                + jnp.einsum('bhrqm,bmhd->bqhrd', p_ctx.astype(vc.dtype), vc))

    o = lax.map(block, (qb, kb, vb, valid))
    return jnp.moveaxis(o, 0, 1).reshape(bsz, L, Hq * Dh)


def ctx_attention(qc, kc, vc, sink):
    bsz, C, Hq, Dh = qc.shape
    Hk = kc.shape[2]
    R = Hq // Hk
    qg = qc.reshape(bsz, C, Hk, R, Dh)
    s = jnp.einsum('bqhrd,bkhd->bhrqk', qg, kc).astype(f32) * (Dh ** -0.5)
    (p,) = sink_softmax([s], sink.reshape(Hk, R)[None, :, :, None, None].astype(f32))
    o = jnp.einsum('bhrqk,bkhd->bqhrd', p.astype(vc.dtype), vc)
    return o.reshape(bsz, C, Hq * Dh)


def s5_discretise(lam_re, lam_im, log_step, b_re, b_im):
    lam_re, lam_im = lam_re.astype(f32), lam_im.astype(f32)
    dt = jnp.exp(log_step.astype(f32))[:, None]
    mag = jnp.exp(lam_re * dt)
    ab_re, ab_im = mag * jnp.cos(lam_im * dt), mag * jnp.sin(lam_im * dt)
    num_re, num_im = ab_re - 1.0, ab_im
    den = lam_re * lam_re + lam_im * lam_im
    coef_re = (num_re * lam_re + num_im * lam_im) / den
    coef_im = (num_im * lam_re - num_re * lam_im) / den
    b_re, b_im = b_re.astype(f32), b_im.astype(f32)
    bb_re = coef_re[..., None] * b_re - coef_im[..., None] * b_im
    bb_im = coef_re[..., None] * b_im + coef_im[..., None] * b_re
    return ab_re, ab_im, bb_re, bb_im


def s5_scan(u, ab_re, ab_im, bb_re, bb_im, h0_re, h0_im):
    L = u.shape[1]
    bu_re = jnp.einsum('blgc,gnc->blgn', u, bb_re)
    bu_im = jnp.einsum('blgc,gnc->blgn', u, bb_im)
    bu_re = bu_re.at[:, 0].add(ab_re * h0_re - ab_im * h0_im)
    bu_im = bu_im.at[:, 0].add(ab_re * h0_im + ab_im * h0_re)
    a_re = jnp.broadcast_to(ab_re, (1, L) + ab_re.shape)
    a_im = jnp.broadcast_to(ab_im, (1, L) + ab_im.shape)

    def combine(e1, e2):
        a1r, a1i, b1r, b1i = e1
        a2r, a2i, b2r, b2i = e2
        return (a2r * a1r - a2i * a1i, a2r * a1i + a2i * a1r,
                a2r * b1r - a2i * b1i + b2r, a2r * b1i + a2i * b1r + b2i)

    _, _, h_re, h_im = lax.associative_scan(combine, (a_re, a_im, bu_re, bu_im), axis=1)
    return h_re, h_im


def s5_bidir(u, disc, init):
    hf_re, hf_im = s5_scan(u, *disc[0], *init[0])
    hb_re, hb_im = s5_scan(jnp.flip(u, axis=1), *disc[1], *init[1])
    final = [(hf_re[:, -1], hf_im[:, -1]), (hb_re[:, -1], hb_im[:, -1])]
    return hf_re + jnp.flip(hb_re, axis=1), hf_im + jnp.flip(hb_im, axis=1), final


def s5_output(s_re, s_im, u, c_re, c_im, d_skip, glu_w, glu_b):
    bsz, L = u.shape[:2]
    y = (jnp.einsum('blgn,gcn->blgc', s_re, c_re.astype(f32))
         - jnp.einsum('blgn,gcn->blgc', s_im, c_im.astype(f32)))
    y = y.reshape(bsz, L, S5_WIDTH) + d_skip * u.reshape(bsz, L, S5_WIDTH)
    y = jax.nn.gelu(y)
    return y * jax.nn.sigmoid(y @ glu_w + glu_b)


def even_layer(x, xc, c, c_ctx, norm_w, ada_w, ada_b, w_in, conv_w, conv_b, dt_bias, a_log,
               d_skip, ssd_norm_w, sink, w_out, ctx_out):
    bsz, L, _ = x.shape
    n_ctx = xc.shape[1]
    shift, scale, gate = adaln(c, ada_w, ada_b)
    shift_c, scale_c, gate_c = adaln(c_ctx, ada_w, ada_b)
    proj = norm_mod(x, norm_w, shift, scale) @ w_in
    proj_c = norm_mod(xc, norm_w, shift_c, scale_c) @ w_in
    z, xbc, dt_raw, q, kv, g = jnp.split(proj, EVEN_SPLITS, axis=-1)
    z_c, xbc_c, dt_raw_c, q_c, kv_c, g_c = jnp.split(proj_c, EVEN_SPLITS, axis=-1)
    a = -jnp.exp(a_log.astype(f32))

    xs_c, bm_c, cm_c, dt_c = ssd_inputs(xbc_c, dt_raw_c, conv_w, conv_b, dt_bias)
    h0 = jnp.zeros((bsz, SSD_HEADS, SSD_HEAD_DIM, SSD_STATE), f32)
    y_c, h_f, h_b = ssd_bidir(xs_c, bm_c, cm_c, dt_c, a, h0, h0)
    xs, bm, cm, dt = ssd_inputs(xbc, dt_raw, conv_w, conv_b, dt_bias)
    y, _, _ = ssd_bidir(xs, bm, cm, dt, a, h_f, h_b)
    ssd_out = ssd_output(y, xs, z, d_skip, ssd_norm_w)

    k_c, v_c = [t.reshape(bsz, n_ctx, ATT_KV_HEADS, ATT_HEAD_DIM) for t in jnp.split(kv_c, 2, axis=-1)]
    k, v = [t.reshape(bsz, L, ATT_KV_HEADS, ATT_HEAD_DIM) for t in jnp.split(kv, 2, axis=-1)]
    cos, sin = axial_rope(L)
    q = apply_rope(q.reshape(bsz, L, ATT_HEADS, ATT_HEAD_DIM), cos, sin)
    k = apply_rope(k, cos, sin)
    att = window_attention(q, k, v, k_c, v_c, sink) * jax.nn.silu(g)
    x = x + gate * (jnp.concatenate([ssd_out, att], axis=-1) @ w_out)

    if ctx_out:
        att_c = ctx_attention(q_c.reshape(bsz, n_ctx, ATT_HEADS, ATT_HEAD_DIM), k_c, v_c, sink) * jax.nn.silu(g_c)
        ssd_c = ssd_output(y_c, xs_c, z_c, d_skip, ssd_norm_w)
        xc = xc + gate_c * (jnp.concatenate([ssd_c, att_c], axis=-1) @ w_out)
    return x, xc


def odd_layer(x, xc, c, c_ctx, norm_w, ada_w, ada_b, w_in, lam_re, lam_im, log_step, b_re, b_im,
              c_re, c_im, d_skip, glu_w, glu_b, w_out, ctx_out):
    bsz, L, _ = x.shape
    n_ctx = xc.shape[1]
    shift, scale, gate = adaln(c, ada_w, ada_b)
    shift_c, scale_c, gate_c = adaln(c_ctx, ada_w, ada_b)
    u, g = jnp.split(norm_mod(x, norm_w, shift, scale) @ w_in, 2, axis=-1)
    hc = norm_mod(xc, norm_w, shift_c, scale_c)
    if ctx_out:
        u_c, g_c = jnp.split(hc @ w_in, 2, axis=-1)
    else:
        u_c = hc @ w_in[:, :S5_WIDTH]
    disc = [s5_discretise(lam_re[d], lam_im[d], log_step[d], b_re, b_im) for d in range(2)]

    ug_c = u_c.reshape(bsz, n_ctx, S5_GROUPS, S5_GROUP_CH).astype(f32)
    zero = jnp.zeros((bsz, S5_GROUPS, S5_STATE), f32)
    sc_re, sc_im, final_c = s5_bidir(ug_c, disc, [(zero, zero), (zero, zero)])
    ug = u.reshape(bsz, L, S5_GROUPS, S5_GROUP_CH).astype(f32)
    s_re, s_im, _ = s5_bidir(ug, disc, final_c)
    y = s5_output(s_re, s_im, ug, c_re, c_im, d_skip, glu_w, glu_b) * jax.nn.silu(g)
    x = x + gate * (y @ w_out)

    if ctx_out:
        y_c = s5_output(sc_re, sc_im, ug_c, c_re, c_im, d_skip, glu_w, glu_b) * jax.nn.silu(g_c)
        xc = xc + gate_c * (y_c @ w_out)
    return x, xc


def setup_inputs(seed: int = 0) -> dict:
    key = jax.random.key(seed)
    ks = iter(jax.random.split(key, 48))
    nrm = lambda shape, s: jax.random.normal(next(ks), shape, f32) * s
    D = D_MODEL
    inp = {}
    inp['x'] = nrm((BATCH, SEQ, D), 1.0)
    inp['c'] = nrm((BATCH, D), 1.0)
    inp['ctx'] = nrm((BATCH, CTX_LEN, D), 1.0)
    inp['c_ctx'] = nrm((D,), 1.0)
    NE = N_EVEN
    inp['e_norm_w'] = 1.0 + nrm((NE, D), 0.02)
    inp['e_ada_w'] = nrm((NE, D, 3 * D), D ** -0.5)
    inp['e_ada_b'] = nrm((NE, 3 * D), 0.02)
    inp['e_w_in'] = nrm((NE, D, EVEN_IN), D ** -0.5)
    inp['e_conv_w'] = nrm((NE, SSD_CONV, SSD_XBC), SSD_CONV ** -0.5)
    inp['e_conv_b'] = nrm((NE, SSD_XBC), 0.02)
    u = jax.random.uniform(next(ks), (NE, 2, SSD_HEADS), f32)
    dt0 = jnp.exp(u * (math.log(0.1) - math.log(0.001)) + math.log(0.001))
    inp['e_dt_bias'] = dt0 + jnp.log(-jnp.expm1(-dt0))
    inp['e_a_log'] = jnp.log(jax.random.uniform(next(ks), (NE, 2, SSD_HEADS), f32, 1.0, 16.0))
    inp['e_d_skip'] = 1.0 + nrm((NE, SSD_HEADS), 0.02)
    inp['e_ssd_norm_w'] = 1.0 + nrm((NE, SSD_INNER), 0.02)
    inp['e_sink'] = nrm((NE, ATT_HEADS), 0.5)
    inp['e_w_out'] = nrm((NE, EVEN_OUT, D), EVEN_OUT ** -0.5)
    NO = N_ODD
    inp['o_norm_w'] = 1.0 + nrm((NO, D), 0.02)
    inp['o_ada_w'] = nrm((NO, D, 3 * D), D ** -0.5)
    inp['o_ada_b'] = nrm((NO, 3 * D), 0.02)
    inp['o_w_in'] = nrm((NO, D, 2 * S5_WIDTH), D ** -0.5)
    inp['o_lam_re'] = -0.5 + nrm((NO, 2, S5_GROUPS, S5_STATE), 0.01)
    inp['o_lam_im'] = math.pi * jnp.arange(S5_STATE, dtype=f32) + nrm((NO, 2, S5_GROUPS, S5_STATE), 0.01)
    us = jax.random.uniform(next(ks), (NO, 2, S5_GROUPS), f32)
    inp['o_log_step'] = us * (math.log(0.1) - math.log(0.001)) + math.log(0.001)
    inp['o_b_re'] = nrm((NO, S5_GROUPS, S5_STATE, S5_GROUP_CH), (2 * S5_GROUP_CH) ** -0.5)
    inp['o_b_im'] = nrm((NO, S5_GROUPS, S5_STATE, S5_GROUP_CH), (2 * S5_GROUP_CH) ** -0.5)
    inp['o_c_re'] = nrm((NO, S5_GROUPS, S5_GROUP_CH, S5_STATE), S5_STATE ** -0.5)
    inp['o_c_im'] = nrm((NO, S5_GROUPS, S5_GROUP_CH, S5_STATE), S5_STATE ** -0.5)
    inp['o_d_skip'] = nrm((NO, S5_WIDTH), 1.0)
    inp['o_glu_w'] = nrm((NO, S5_WIDTH, S5_WIDTH), S5_WIDTH ** -0.5)
    inp['o_glu_b'] = nrm((NO, S5_WIDTH), 0.02)
    inp['o_w_out'] = nrm((NO, S5_WIDTH, D), S5_WIDTH ** -0.5)
    inp['final_norm_w'] = 1.0 + nrm((D,), 0.02)
    return inp


def reference(x, c, ctx, c_ctx, e_norm_w, e_ada_w, e_ada_b, e_w_in, e_conv_w, e_conv_b, e_dt_bias,
              e_a_log, e_d_skip, e_ssd_norm_w, e_sink, e_w_out, o_norm_w, o_ada_w, o_ada_b, o_w_in,
              o_lam_re, o_lam_im, o_log_step, o_b_re, o_b_im, o_c_re, o_c_im, o_d_skip, o_glu_w,
              o_glu_b, o_w_out, final_norm_w):
    xc = ctx
    for i in range(DEPTH):
        ctx_out = i < DEPTH - 1
        j = i // 2
        if i % 2 == 0:
            x, xc = even_layer(x, xc, c, c_ctx, e_norm_w[j], e_ada_w[j], e_ada_b[j], e_w_in[j],
                               e_conv_w[j], e_conv_b[j], e_dt_bias[j], e_a_log[j], e_d_skip[j],
                               e_ssd_norm_w[j], e_sink[j], e_w_out[j], ctx_out)
        else:
            x, xc = odd_layer(x, xc, c, c_ctx, o_norm_w[j], o_ada_w[j], o_ada_b[j], o_w_in[j],
                              o_lam_re[j], o_lam_im[j], o_log_step[j], o_b_re[j], o_b_im[j],
                              o_c_re[j], o_c_im[j], o_d_skip[j], o_glu_w[j], o_glu_b[j],
                              o_w_out[j], ctx_out)
    return rmsnorm(x, final_norm_w)
```

```python
import functools
import math

import jax
import jax.numpy as jnp
import numpy as np
from jax import lax
from jax.experimental import pallas as pl
from jax.experimental.pallas import tpu as pltpu

f32 = jnp.float32
bf16 = jnp.bfloat16

D_MODEL = 1024
GRID_W = 64
EPS = 1e-6
NEG_INF = -1e30

SSD_HEADS = 16
SSD_HEAD_DIM = 64
SSD_GROUPS = 2
SSD_STATE = 128
SSD_CONV = 5
SSD_CHUNK = 128
SSD_INNER = SSD_HEADS * SSD_HEAD_DIM
SSD_BC = SSD_GROUPS * SSD_STATE
SSD_XBC = SSD_INNER + 2 * SSD_BC
ATT_HEADS = 16
ATT_KV_HEADS = 4
ATT_HEAD_DIM = 64
ATT_BLOCK = 128
ROPE_THETA = 10000.0
ATT_Q = ATT_HEADS * ATT_HEAD_DIM
ATT_KVW = ATT_KV_HEADS * ATT_HEAD_DIM
S5_WIDTH = 1024
S5_GROUP_CH = 16
S5_GROUPS = S5_WIDTH // S5_GROUP_CH
S5_STATE = 64

LANES = 128
ROW_TILE = 512
S5_TIME_BLOCK = 32
S5_GBLK = 8
VMEM_LIMIT = 56 * 1024 * 1024


def _cparams(sem):
    return pltpu.CompilerParams(dimension_semantics=sem, vmem_limit_bytes=VMEM_LIMIT)


def _silu(x):
    return x * jax.nn.sigmoid(x)


def _adaln_kernel(c_ref, w_ref, b_ref, o_ref):
    c = c_ref[...]
    s = _silu(c).astype(bf16)
    o_ref[...] = jnp.dot(s, w_ref[...], preferred_element_type=f32) + b_ref[...]


def _adaln(cvecs, w, b):
    r, d = cvecs.shape
    n = w.shape[1]
    tn = 1024
    return pl.pallas_call(
        _adaln_kernel,
        out_shape=jax.ShapeDtypeStruct((r, n), f32),
        grid=(n // tn,),
        in_specs=[pl.BlockSpec((r, d), lambda j: (0, 0)),
                  pl.BlockSpec((d, tn), lambda j: (0, j)),
                  pl.BlockSpec((1, tn), lambda j: (0, j))],
        out_specs=pl.BlockSpec((r, tn), lambda j: (0, j)),
        compiler_params=_cparams(("arbitrary",)),
        name="adaln",
    )(cvecs, w, b)


def _inproj_kernel(segs, has_rope, x_ref, shift_ref, scale_ref, nw_ref, w_ref, *rest):
    if has_rope:
        cos_ref, sina_ref, sinb_ref = rest[:3]
        outs = rest[3:]
    else:
        outs = rest
    x = x_ref[...]
    ms = jnp.mean(x * x, axis=-1, keepdims=True)
    h = (x * lax.rsqrt(ms + EPS)) * nw_ref[...]
    h = h * (1.0 + scale_ref[...]) + shift_ref[...]
    hb = h.astype(bf16)
    for (start, width, rope, _), o_ref in zip(segs, outs):
        acc = jnp.dot(hb, w_ref[:, start:start + width], preferred_element_type=f32)
        if rope is not None and not has_rope:
            acc = acc * rope
        elif rope is not None:
            rep = width // LANES
            cos = jnp.concatenate([cos_ref[...]] * rep, axis=1) * rope
            sina = jnp.concatenate([sina_ref[...]] * rep, axis=1) * rope
            sinb = jnp.concatenate([sinb_ref[...]] * rep, axis=1) * rope
            half = ATT_HEAD_DIM // 2
            up = pltpu.roll(acc, width - half, axis=1)
            dn = pltpu.roll(acc, half, axis=1)
            acc = acc * cos + up * sina + dn * sinb
        o_ref[...] = acc.astype(o_ref.dtype)


def _inproj(x, shift, scale, nw, w, segs, rows_per_mod, rope_tabs=None):
    m, d = x.shape
    tm = min(ROW_TILE, rows_per_mod)
    n = w.shape[1]
    per = rows_per_mod // tm
    nmod = shift.shape[0]
    mod_idx = (lambda i: (i // per, 0, 0)) if nmod > 1 else (lambda i: (0, 0, 0))
    in_specs = [pl.BlockSpec((tm, d), lambda i: (i, 0)),
                pl.BlockSpec((None, 1, d), mod_idx),
                pl.BlockSpec((None, 1, d), mod_idx),
                pl.BlockSpec((1, d), lambda i: (0, 0)),
                pl.BlockSpec((d, n), lambda i: (0, 0))]
    args = [x, shift, scale, nw, w]
    if rope_tabs is not None:
        for t in rope_tabs:
            in_specs.append(pl.BlockSpec((tm, LANES), lambda i: (i % per, 0)))
            args.append(t)
    out_shape = [jax.ShapeDtypeStruct((m, sg[1]), sg[3]) for sg in segs]
    out_specs = [pl.BlockSpec((tm, sg[1]), lambda i: (i, 0)) for sg in segs]
    return pl.pallas_call(
        functools.partial(_inproj_kernel, tuple(segs), rope_tabs is not None),
        out_shape=out_shape,
        grid=(m // tm,),
        in_specs=in_specs,
        out_specs=out_specs,
        compiler_params=_cparams(("parallel",)),
        name="inproj",
    )(*args)


def _rope_tables(seq_len):
    rows = seq_len // GRID_W
    row = jnp.repeat(jnp.arange(rows, dtype=f32), GRID_W)
    col = jnp.tile(jnp.arange(GRID_W, dtype=f32), rows)
    n_freq = ATT_HEAD_DIM // 4
    inv = ROPE_THETA ** (-jnp.arange(n_freq, dtype=f32) / n_freq)
    ang = jnp.concatenate([row[:, None] * inv, col[:, None] * inv], axis=-1)
    cos, sin = jnp.cos(ang), jnp.sin(ang)
    zero = jnp.zeros_like(sin)
    cos_h = jnp.concatenate([cos, cos], axis=-1)
    sina_h = jnp.concatenate([-sin, zero], axis=-1)
    sinb_h = jnp.concatenate([zero, sin], axis=-1)
    two = lambda t: jnp.concatenate([t, t], axis=-1)
    return two(cos_h), two(sina_h), two(sinb_h)


def _pair_expand(vals, base, q):
    lane = lax.broadcasted_iota(jnp.int32, (q, LANES), 1)
    lo = lane < SSD_HEAD_DIM
    parts = []
    for k in range(SSD_HEADS // 2):
        a = jnp.broadcast_to(vals[:, base + 2 * k:base + 2 * k + 1], (q, LANES))
        b = jnp.broadcast_to(vals[:, base + 2 * k + 1:base + 2 * k + 2], (q, LANES))
        parts.append(jnp.where(lo, a, b))
    return jnp.concatenate(parts, axis=1)


def _ssd_kernel(seq_len, xbc_ref, dt_ref, z_ref, cw_ref, cb_ref, dtb_ref, alog_ref, dskip_ref, nw_ref,
                h0_ref, out_ref, hfin_ref, xs_s, bc_s, dt_s, y_s, hf_s, hb_s):
    q = SSD_CHUNK
    nc = seq_len // q
    halo = 16
    H, P, N = SSD_HEADS, SSD_HEAD_DIM, SSD_STATE
    a_row = -jnp.exp(alog_ref[...])

    def conv_body(c, carry):
        r0 = pl.multiple_of(c * q, q)
        cur = xbc_ref[pl.ds(r0, q), :].astype(f32)
        pstart = pl.multiple_of(jnp.maximum(r0 - halo, 0), halo)
        nstart = pl.multiple_of(jnp.minimum(r0 + q, seq_len - halo), halo)
        prev = xbc_ref[pl.ds(pstart, halo), :].astype(f32)
        nxt = xbc_ref[pl.ds(nstart, halo), :].astype(f32)
        prev = jnp.where(c > 0, prev, 0.0)
        nxt = jnp.where(c < nc - 1, nxt, 0.0)
        win = jnp.concatenate([prev, cur, nxt], axis=0)
        wn = q + 2 * halo
        acc = jnp.zeros((q, SSD_XBC), f32) + cb_ref[...]
        for k in range(SSD_CONV):
            d = k - SSD_CONV // 2
            sh = win if d == 0 else pltpu.roll(win, (-d) % wn, axis=0)
            acc = acc + sh[halo:halo + q, :] * cw_ref[k:k + 1, :]
        act = _silu(acc)
        xs_s[pl.ds(r0, q), :] = act[:, :SSD_INNER]
        bc_s[pl.ds(r0, q), :] = act[:, SSD_INNER:].astype(bf16)
        dt_s[pl.ds(r0, q), :] = jax.nn.softplus(dt_ref[pl.ds(r0, q), :] + dtb_ref[...])
        return carry

    lax.fori_loop(0, nc, conv_body, 0)

    hf_s[...] = h0_ref[0]
    hb_s[...] = h0_ref[1]

    ri = lax.broadcasted_iota(jnp.int32, (q, q), 0)
    ci = lax.broadcasted_iota(jnp.int32, (q, q), 1)
    lower = ri >= ci
    upper = ci >= ri
    tri_lo = lower.astype(f32)
    tri_up = upper.astype(f32)
    lane = lax.broadcasted_iota(jnp.int32, (q, LANES), 1)
    lo_half = lane < P
    hi = lax.Precision.HIGHEST

    def state_update(h_s, bmat, xw, dec_row):
        for g in range(SSD_GROUPS):
            gs = slice(g * (H // SSD_GROUPS) * P, (g + 1) * (H // SSD_GROUPS) * P)
            bt = jnp.transpose(bmat[g].astype(f32)).astype(bf16)
            upd = jnp.dot(bt, xw[:, gs].astype(bf16), preferred_element_type=f32)
            h_s[:, gs] = h_s[:, gs] * dec_row[:, gs] + upd

    def fwd_body(c, carry):
        r0 = pl.multiple_of(c * q, q)
        dt = dt_s[pl.ds(r0, q), :]
        dta = dt * a_row
        cf = jnp.dot(tri_lo, dta, precision=hi, preferred_element_type=f32)
        rb = jnp.dot(tri_up, dta, precision=hi, preferred_element_type=f32)
        cft = jnp.transpose(cf)
        rbt = jnp.transpose(rb)
        dtt = jnp.transpose(dt)
        xs = xs_s[pl.ds(r0, q), :]
        bcv = bc_s[pl.ds(r0, q), :]
        bmat = [bcv[:, g * N:(g + 1) * N] for g in range(SSD_GROUPS)]
        cmat = [bcv[:, SSD_BC + g * N:SSD_BC + (g + 1) * N] for g in range(SSD_GROUPS)]
        cbm = [lax.dot_general(cmat[g], bmat[g], (((1,), (1,)), ((), ())), preferred_element_type=f32)
               for g in range(SSD_GROUPS)]
        xsb = xs.astype(bf16)
        ypairs = []
        for k in range(H // 2):
            res = []
            for h in (2 * k, 2 * k + 1):
                g = h // (H // SSD_GROUPS)
                segf = cf[:, h:h + 1] - cft[h:h + 1, :]
                segb = rb[:, H + h:H + h + 1] - rbt[H + h:H + h + 1, :]
                df = jnp.exp(jnp.where(lower, segf, NEG_INF)) * dtt[h:h + 1, :]
                db = jnp.exp(jnp.where(upper, segb, NEG_INF)) * dtt[H + h:H + h + 1, :]
                mh = (cbm[g] * (df + db)).astype(bf16)
                res.append(jnp.dot(mh, xsb[:, k * LANES:(k + 1) * LANES], preferred_element_type=f32))
            ypairs.append(jnp.where(lo_half, res[0], res[1]))
        y = jnp.concatenate(ypairs, axis=1)
        ecf = jnp.exp(cf)
        ef = _pair_expand(ecf, 0, q)
        hfb = hf_s[...].astype(bf16)
        yoff = jnp.concatenate(
            [jnp.dot(cmat[g], hfb[:, g * 512:(g + 1) * 512], preferred_element_type=f32)
             for g in range(SSD_GROUPS)], axis=1)
        y_s[pl.ds(r0, q), :] = y + yoff * ef
        wfa = jnp.exp(cf[q - 1:q, :] - cf) * dt
        xw = xs * _pair_expand(wfa, 0, q)
        state_update(hf_s, bmat, xw, ef[q - 1:q, :])
        return carry

    lax.fori_loop(0, nc, fwd_body, 0)

    def bwd_body(i, carry):
        c = nc - 1 - i
        r0 = pl.multiple_of(c * q, q)
        dt = dt_s[pl.ds(r0, q), :]
        dta = dt * a_row
        rb = jnp.dot(tri_up, dta, precision=hi, preferred_element_type=f32)
        xs = xs_s[pl.ds(r0, q), :]
        bcv = bc_s[pl.ds(r0, q), :]
        bmat = [bcv[:, g * N:(g + 1) * N] for g in range(SSD_GROUPS)]
        cmat = [bcv[:, SSD_BC + g * N:SSD_BC + (g + 1) * N] for g in range(SSD_GROUPS)]
        erb = jnp.exp(rb)
        eb = _pair_expand(erb, H, q)
        hbb = hb_s[...].astype(bf16)
        yoff = jnp.concatenate(
            [jnp.dot(cmat[g], hbb[:, g * 512:(g + 1) * 512], preferred_element_type=f32)
             for g in range(SSD_GROUPS)], axis=1)
        y = y_s[pl.ds(r0, q), :] + yoff * eb
        wba = jnp.exp(rb[0:1, :] - rb) * dt
        xw = xs * _pair_expand(wba, H, q)
        state_update(hb_s, bmat, xw, eb[0:1, :])
        yy = y + xs * dskip_ref[...]
        zz = z_ref[pl.ds(r0, q), :].astype(f32)
        gated = yy * _silu(zz)
        ms = jnp.mean(gated * gated, axis=-1, keepdims=True)
        out_ref[pl.ds(r0, q), :] = (gated * lax.rsqrt(ms + EPS) * nw_ref[...]).astype(out_ref.dtype)
        return carry

    lax.fori_loop(0, nc, bwd_body, 0)
    hfin_ref[0] = hf_s[...]
    hfin_ref[1] = hb_s[...]


def _ssd(xbc, dt, z, conv_w, conv_b, dtb, alog, dskip, nw, h0, bsz, seq_len):
    one = pl.Buffered(1)
    seq = lambda w: pl.BlockSpec((seq_len, w), lambda b: (b, 0), pipeline_mode=one)
    const = lambda r, w: pl.BlockSpec((r, w), lambda b: (0, 0))
    st = pl.BlockSpec((None, 2, SSD_STATE, SSD_INNER), lambda b: (b, 0, 0, 0))
    return pl.pallas_call(
        functools.partial(_ssd_kernel, seq_len),
        out_shape=[jax.ShapeDtypeStruct((bsz * seq_len, SSD_INNER), bf16),
                   jax.ShapeDtypeStruct((bsz, 2, SSD_STATE, SSD_INNER), f32)],
        grid=(bsz,),
        in_specs=[seq(SSD_XBC), seq(LANES), seq(SSD_INNER),
                  const(8, SSD_XBC), const(1, SSD_XBC), const(1, LANES), const(1, LANES),
                  const(1, SSD_INNER), const(1, SSD_INNER), st],
        out_specs=[seq(SSD_INNER), st],
        scratch_shapes=[pltpu.VMEM((seq_len, SSD_INNER), f32),
                        pltpu.VMEM((seq_len, 2 * SSD_BC), bf16),
                        pltpu.VMEM((seq_len, LANES), f32),
                        pltpu.VMEM((seq_len, SSD_INNER), f32),
                        pltpu.VMEM((SSD_STATE, SSD_INNER), f32),
                        pltpu.VMEM((SSD_STATE, SSD_INNER), f32)],
        compiler_params=_cparams(("parallel",)),
        name="ssd",
    )(xbc, dt, z, conv_w, conv_b, dtb, alog, dskip, nw, h0)


def _attn_kernel(n_blocks, local, q_ref, g_ref, k_ref, v_ref, kc_ref, vc_ref, sink_ref, o_ref):
    t = ATT_BLOCK
    i = pl.program_id(1)
    n_ctx = kc_ref.shape[0]
    rpk = ATT_HEADS // ATT_KV_HEADS
    lane = lax.broadcasted_iota(jnp.int32, (t, LANES), 1)
    lo_half = lane < ATT_HEAD_DIM
    qv = q_ref[...]
    gv = g_ref[...].astype(f32)
    if local:
        p0 = pl.multiple_of(jnp.maximum(i - 1, 0) * t, t)
        c0 = pl.multiple_of(i * t, t)
        n0 = pl.multiple_of(jnp.minimum(i + 1, n_blocks - 1) * t, t)
        ql = lax.broadcasted_iota(jnp.int32, (t, t), 0)
        kl = lax.broadcasted_iota(jnp.int32, (t, t), 1)
        bias_prev = jnp.where((kl >= ql) & (i > 0), 0.0, NEG_INF)
        bias_next = jnp.where((kl <= ql) & (i < n_blocks - 1), 0.0, NEG_INF)
        bias = jnp.concatenate([bias_prev, jnp.zeros((t, t), f32), bias_next,
                                jnp.zeros((t, n_ctx), f32)], axis=1)
        bias = jnp.concatenate([bias] * rpk, axis=0)
    zero_b = jnp.zeros((), bf16)
    outs = []
    for j in range(ATT_KV_HEADS):
        ls = slice(j * LANES, (j + 1) * LANES)
        if local:
            kk = jnp.concatenate([k_ref[pl.ds(p0, t), ls], k_ref[pl.ds(c0, t), ls],
                                  k_ref[pl.ds(n0, t), ls], kc_ref[:, ls]], axis=0)
            vv = jnp.concatenate([v_ref[pl.ds(p0, t), ls], v_ref[pl.ds(c0, t), ls],
                                  v_ref[pl.ds(n0, t), ls], vc_ref[:, ls]], axis=0)
        else:
            kk = kc_ref[:, ls]
            vv = vc_ref[:, ls]
        pieces = []
        sinks = []
        for r in range(rpk):
            hq = j * rpk + r
            qp = qv[:, (hq // 2) * LANES:(hq // 2 + 1) * LANES]
            keep = lo_half if hq % 2 == 0 else jnp.logical_not(lo_half)
            pieces.append(jnp.where(keep, qp, zero_b))
            sinks.append(jnp.full((t, 1), sink_ref[hq], f32))
        q4 = jnp.concatenate(pieces, axis=0)
        sk = jnp.concatenate(sinks, axis=0)
        s = lax.dot_general(q4, kk, (((1,), (1,)), ((), ())), preferred_element_type=f32)
        if local:
            s = s + bias
        m = jnp.maximum(jnp.max(s, axis=1, keepdims=True), sk)
        p = jnp.exp(s - m)
        den = jnp.sum(p, axis=1, keepdims=True) + jnp.exp(sk - m)
        o4 = jnp.dot(p.astype(bf16), vv, preferred_element_type=f32) / den
        outs.append(jnp.where(lo_half, o4[0:t], o4[t:2 * t]))
        outs.append(jnp.where(lo_half, o4[2 * t:3 * t], o4[3 * t:4 * t]))
    o = jnp.concatenate(outs, axis=1)
    o_ref[...] = (o * _silu(gv)).astype(o_ref.dtype)


def _attention(q, g, k, v, kc, vc, sink, bsz, seq_len, local):
    t = ATT_BLOCK
    nb = seq_len // t
    n_ctx = kc.shape[1]
    kw = ATT_KV_HEADS * LANES
    blk = pl.BlockSpec((t, ATT_Q), lambda b, i: (b * nb + i, 0))
    full = lambda n: pl.BlockSpec((None, n, kw), lambda b, i: (b, 0, 0))
    return pl.pallas_call(
        functools.partial(_attn_kernel, nb, local),
        out_shape=jax.ShapeDtypeStruct((bsz * seq_len, ATT_Q), bf16),
        grid=(bsz, nb),
        in_specs=[blk, blk, full(k.shape[1]), full(v.shape[1]), full(n_ctx), full(n_ctx),
                  pl.BlockSpec(memory_space=pltpu.SMEM)],
        out_specs=blk,
        compiler_params=_cparams(("parallel", "arbitrary")),
        name="attention",
    )(q, g, k, v, kc, vc, sink)


def _outproj_kernel(a_ref, b_ref, w_ref, x_ref, gate_ref, o_ref):
    ka = a_ref.shape[1]
    acc = jnp.dot(a_ref[...], w_ref[:ka, :], preferred_element_type=f32)
    acc = acc + jnp.dot(b_ref[...], w_ref[ka:, :], preferred_element_type=f32)
    o_ref[...] = x_ref[...] + gate_ref[...] * acc


def _outproj(a, b, w, x, gate, rows_per_mod):
    m, d = x.shape
    tm = min(ROW_TILE, rows_per_mod)
    per = rows_per_mod // tm
    nmod = gate.shape[0]
    mod_idx = (lambda i: (i // per, 0, 0)) if nmod > 1 else (lambda i: (0, 0, 0))
    row = lambda w_: pl.BlockSpec((tm, w_), lambda i: (i, 0))
    return pl.pallas_call(
        _outproj_kernel,
        out_shape=jax.ShapeDtypeStruct((m, d), f32),
        grid=(m // tm,),
        in_specs=[row(a.shape[1]), row(b.shape[1]),
                  pl.BlockSpec(w.shape, lambda i: (0, 0)), row(d),
                  pl.BlockSpec((None, 1, d), mod_idx)],
        out_specs=row(d),
        compiler_params=_cparams(("parallel",)),
        name="outproj",
    )(a, b, w, x, gate)


def _s5_disc_kernel(lre_ref, lim_ref, ls_ref, bre_ref, bim_ref, abre_ref, abim_ref, bbre_ref, bbim_ref):
    lam_re = lre_ref[...]
    lam_im = lim_ref[...]
    dt = jnp.exp(ls_ref[...])
    mag = jnp.exp(lam_re * dt)
    ab_re = mag * jnp.cos(lam_im * dt)
    ab_im = mag * jnp.sin(lam_im * dt)
    num_re, num_im = ab_re - 1.0, ab_im
    den = lam_re * lam_re + lam_im * lam_im
    coef_re = (num_re * lam_re + num_im * lam_im) / den
    coef_im = (num_im * lam_re - num_re * lam_im) / den
    b_re, b_im = bre_ref[...], bim_ref[...]
    abre_ref[...] = ab_re
    abim_ref[...] = ab_im
    bbre_ref[...] = coef_re * b_re - coef_im * b_im
    bbim_ref[...] = coef_re * b_im + coef_im * b_re


def _s5_discretise(lam_re, lam_im, log_step, b_re, b_im):
    g, n, cg = b_re.shape
    exp = lambda t: jnp.repeat(t.reshape(2 * g, n), cg, axis=1)
    ls = jnp.broadcast_to(log_step.reshape(2 * g, 1), (2 * g, n * cg))
    bb = lambda t: jnp.tile(t.reshape(g, n * cg), (2, 1))
    shp = jax.ShapeDtypeStruct((2 * g, n * cg), f32)
    ab_re, ab_im, bb_re, bb_im = pl.pallas_call(
        _s5_disc_kernel, out_shape=[shp] * 4, name="s5_disc",
    )(exp(lam_re), exp(lam_im), ls, bb(b_re), bb(b_im))
    first = lambda t: t.reshape(2, g, n, cg)[..., 0]
    full = lambda t: t.reshape(2, g, n, cg)
    return first(ab_re), first(ab_im), full(bb_re), full(bb_im)


def _s5_kernel(tblk, uf_ref, ub_ref, wbf_ref, wbb_ref, wc_ref, af_ref, ab_ref, h0_ref,
               yf_ref, yb_ref, hfin_ref, buf_s, bub_s, hst_s):
    i = pl.program_id(0)
    nblk = S5_GROUPS // S5_GBLK
    sw = S5_GBLK * S5_STATE
    rows = tblk * 8

    @pl.when(i == 0)
    def _():
        hst_s[...] = h0_ref[...]

    uf = uf_ref[...]
    ub = ub_ref[...]
    for j in range(nblk):
        cs = slice(j * LANES, (j + 1) * LANES)
        ws = slice(j * 2 * sw, (j + 1) * 2 * sw)
        buf_s[:, ws] = jnp.dot(uf[:, cs], wbf_ref[j], preferred_element_type=f32)
        bub_s[:, ws] = jnp.dot(ub[:, cs], wbb_ref[j], preferred_element_type=f32)

    for j in range(nblk):
        re = slice(j * 2 * sw, j * 2 * sw + sw)
        im = slice(j * 2 * sw + sw, (j + 1) * 2 * sw)
        bc = lambda r, s: jnp.broadcast_to(r[:, s], (8, sw))
        afr, afi = bc(af_ref, re), bc(af_ref, im)
        abr, abi = bc(ab_ref, re), bc(ab_ref, im)

        def step(tt, carry):
            hfr, hfi, hbr, hbi = carry
            rf = pl.multiple_of(tt * 8, 8)
            rbk = pl.multiple_of((tblk - 1 - tt) * 8, 8)
            nfr = afr * hfr - afi * hfi + buf_s[pl.ds(rf, 8), re]
            nfi = afr * hfi + afi * hfr + buf_s[pl.ds(rf, 8), im]
            nbr = abr * hbr - abi * hbi + bub_s[pl.ds(rbk, 8), re]
            nbi = abr * hbi + abi * hbr + bub_s[pl.ds(rbk, 8), im]
            buf_s[pl.ds(rf, 8), re] = nfr
            buf_s[pl.ds(rf, 8), im] = nfi
            bub_s[pl.ds(rbk, 8), re] = nbr
            bub_s[pl.ds(rbk, 8), im] = nbi
            return nfr, nfi, nbr, nbi

        init = (hst_s[0, :, re], hst_s[0, :, im], hst_s[1, :, re], hst_s[1, :, im])
        hfr, hfi, hbr, hbi = lax.fori_loop(0, tblk, step, init, unroll=4)
        hst_s[0, :, re] = hfr
        hst_s[0, :, im] = hfi
        hst_s[1, :, re] = hbr
        hst_s[1, :, im] = hbi

    for j in range(nblk):
        cs = slice(j * LANES, (j + 1) * LANES)
        ws = slice(j * 2 * sw, (j + 1) * 2 * sw)
        yf_ref[:, cs] = jnp.dot(buf_s[:, ws].astype(bf16), wc_ref[j], preferred_element_type=f32)
        yb_ref[:, cs] = jnp.dot(bub_s[:, ws].astype(bf16), wc_ref[j], preferred_element_type=f32)

    hfin_ref[...] = hst_s[...]


def _s5_scan(u_t, wbf, wbb, wc, a_f, a_b, h0, seq_len):
    tblk = min(S5_TIME_BLOCK, seq_len)
    nt = seq_len // tblk
    rows = tblk * 8
    nstate = 2 * S5_GROUPS * S5_STATE
    const3 = lambda a: pl.BlockSpec(a.shape, lambda i: (0, 0, 0))
    const2 = lambda a: pl.BlockSpec(a.shape, lambda i: (0, 0))
    return pl.pallas_call(
        functools.partial(_s5_kernel, tblk),
        out_shape=[jax.ShapeDtypeStruct((seq_len * 8, S5_WIDTH), f32),
                   jax.ShapeDtypeStruct((seq_len * 8, S5_WIDTH), f32),
                   jax.ShapeDtypeStruct((2, 8, nstate), f32)],
        grid=(nt,),
        in_specs=[pl.BlockSpec((rows, S5_WIDTH), lambda i: (i, 0)),
                  pl.BlockSpec((rows, S5_WIDTH), lambda i: (nt - 1 - i, 0)),
                  const3(wbf), const3(wbb), const3(wc), const2(a_f), const2(a_b), const3(h0)],
        out_specs=[pl.BlockSpec((rows, S5_WIDTH), lambda i: (i, 0)),
                   pl.BlockSpec((rows, S5_WIDTH), lambda i: (nt - 1 - i, 0)),
                   pl.BlockSpec((2, 8, nstate), lambda i: (0, 0, 0))],
        scratch_shapes=[pltpu.VMEM((rows, nstate), f32), pltpu.VMEM((rows, nstate), f32),
                        pltpu.VMEM((2, 8, nstate), f32)],
        compiler_params=_cparams(("arbitrary",)),
        name="s5_scan",
    )(u_t, u_t, wbf, wbb, wc, a_f, a_b, h0)


def _s5_weights(ab_re, ab_im, bb_re, bb_im, c_re, c_im):
    g, n, cg = S5_GROUPS, S5_STATE, S5_GROUP_CH
    nblk = g // S5_GBLK
    eye = jnp.eye(S5_GBLK, dtype=f32)

    def wb_dir(bre, bim):
        def bd(t):
            t = t.reshape(nblk, S5_GBLK, n, cg)
            return jnp.einsum('jgnc,gh->jgchn', t, eye).reshape(nblk, S5_GBLK * cg, S5_GBLK * n)
        return jnp.concatenate([bd(bre), bd(bim)], axis=-1).astype(bf16)

    def a_dir(are, aim):
        are = are.reshape(nblk, S5_GBLK * n)
        aim = aim.reshape(nblk, S5_GBLK * n)
        return jnp.concatenate([are, aim], axis=-1).reshape(1, 2 * g * n)

    def bdc(t):
        t = t.reshape(nblk, S5_GBLK, cg, n)
        return jnp.einsum('jgcn,gh->jgnhc', t, eye).reshape(nblk, S5_GBLK * n, S5_GBLK * cg)

    wc = jnp.concatenate([bdc(c_re), -bdc(c_im)], axis=1).astype(bf16)
    return (wb_dir(bb_re[0], bb_im[0]), wb_dir(bb_re[1], bb_im[1]), wc,
            a_dir(ab_re[0], ab_im[0]), a_dir(ab_re[1], ab_im[1]))


def _s5_out_kernel(yf_ref, yb_ref, u_ref, g_ref, x_ref, dskip_ref, gw_ref, gb_ref, w_ref, gate_ref,
                   fw_ref, o_ref):
    y = yf_ref[...] + yb_ref[...] + dskip_ref[...] * u_ref[...].astype(f32)
    y = jax.nn.gelu(y)
    glu = jnp.dot(y.astype(bf16), gw_ref[...], preferred_element_type=f32) + gb_ref[...]
    y = y * jax.nn.sigmoid(glu)
    y = y * _silu(g_ref[...].astype(f32))
    x = x_ref[...] + gate_ref[...] * jnp.dot(y.astype(bf16), w_ref[...], preferred_element_type=f32)
    ms = jnp.mean(x * x, axis=-1, keepdims=True)
    o_ref[...] = x * lax.rsqrt(ms + EPS) * fw_ref[...]


def _s5_out(yf, yb, u, g, x, dskip, gw, gb, w, gate, fw, rows_per_mod):
    m, d = x.shape
    tm = min(ROW_TILE, rows_per_mod)
    per = rows_per_mod // tm
    row = lambda: pl.BlockSpec((tm, d), lambda i: (i, 0))
    vec = lambda: pl.BlockSpec((1, d), lambda i: (0, 0))
    mat = lambda: pl.BlockSpec((d, d), lambda i: (0, 0))
    return pl.pallas_call(
        _s5_out_kernel,
        out_shape=jax.ShapeDtypeStruct((m, d), f32),
        grid=(m // tm,),
        in_specs=[row(), row(), row(), row(), row(), vec(), mat(), vec(), mat(),
                  pl.BlockSpec((None, 1, d), lambda i: (i // per, 0, 0)), vec()],
        out_specs=row(),
        compiler_params=_cparams(("parallel",)),
        name="s5_out",
    )(yf, yb, u, g, x, dskip, gw, gb, w, gate, fw)


def _even_weights(w_in):
    o = 0
    z = w_in[:, o:o + SSD_INNER]; o += SSD_INNER
    xbc = w_in[:, o:o + SSD_XBC]; o += SSD_XBC
    dt = w_in[:, o:o + 2 * SSD_HEADS]; o += 2 * SSD_HEADS
    q = w_in[:, o:o + ATT_Q]; o += ATT_Q
    k = w_in[:, o:o + ATT_KVW]; o += ATT_KVW
    v = w_in[:, o:o + ATT_KVW]; o += ATT_KVW
    g = w_in[:, o:o + ATT_Q]
    d = w_in.shape[0]
    dup = lambda t: jnp.concatenate([t.reshape(d, ATT_KV_HEADS, 1, ATT_HEAD_DIM)] * 2, axis=2).reshape(d, -1)
    dtp = jnp.pad(dt, ((0, 0), (0, LANES - 2 * SSD_HEADS)))
    return jnp.concatenate([z, xbc, q, dup(k), dup(v), g, dtp], axis=1).astype(bf16)


def _even_segs(rope):
    scale = ATT_HEAD_DIM ** -0.5
    widths = [(SSD_INNER, None, bf16), (SSD_XBC, None, bf16),
              (ATT_Q, scale, bf16),
              (ATT_KV_HEADS * LANES, 1.0 if rope else None, bf16),
              (ATT_KV_HEADS * LANES, None, bf16), (ATT_Q, None, bf16), (LANES, None, f32)]
    segs, o = [], 0
    for w, r, dtp in widths:
        segs.append((o, w, r, dtp))
        o += w
    return segs


def _pad_lanes(v, n=LANES):
    v = v.reshape(1, -1)
    return jnp.pad(v, ((0, 0), (0, n - v.shape[1])))


def kernel(x, c, ctx, c_ctx, e_norm_w, e_ada_w, e_ada_b, e_w_in, e_conv_w, e_conv_b, e_dt_bias,
           e_a_log, e_d_skip, e_ssd_norm_w, e_sink, e_w_out, o_norm_w, o_ada_w, o_ada_b, o_w_in,
           o_lam_re, o_lam_im, o_log_step, o_b_re, o_b_im, o_c_re, o_c_im, o_d_skip, o_glu_w,
           o_glu_b, o_w_out, final_norm_w):
    bsz, seq_len, d = x.shape
    n_ctx = ctx.shape[1]
    xf = x.reshape(bsz * seq_len, d)
    xcf = ctx.reshape(bsz * n_ctx, d)

    cvecs = jnp.concatenate([c, c_ctx[None, :], jnp.zeros((16 - bsz - 1, d), f32)], axis=0)

    def modulation(ada_w, ada_b):
        mod = _adaln(cvecs, ada_w.astype(bf16), ada_b.reshape(1, -1))
        parts = [mod[:, k * d:(k + 1) * d] for k in range(3)]
        lat = [p[:bsz].reshape(bsz, 1, d) for p in parts]
        cx = [p[bsz:bsz + 1].reshape(1, 1, d) for p in parts]
        return lat, cx

    (shift, scale, gate), (shift_c, scale_c, gate_c) = modulation(e_ada_w[0], e_ada_b[0])
    w_in = _even_weights(e_w_in[0])
    nw = e_norm_w[0].reshape(1, d)
    tabs = _rope_tables(seq_len)
    z, xbc, q, k, v, g, dt = _inproj(xf, shift, scale, nw, w_in, _even_segs(True), seq_len, tabs)
    z_c, xbc_c, q_c, k_c, v_c, g_c, dt_c = _inproj(xcf, shift_c, scale_c, nw, w_in, _even_segs(False), n_ctx)

    conv_w = jnp.pad(e_conv_w[0], ((0, 8 - SSD_CONV), (0, 0)))
    conv_b = e_conv_b[0].reshape(1, -1)
    dtb = _pad_lanes(e_dt_bias[0])
    alog = _pad_lanes(e_a_log[0])
    dskip = jnp.repeat(e_d_skip[0], SSD_HEAD_DIM).reshape(1, -1)
    snw = e_ssd_norm_w[0].reshape(1, -1)
    h0 = jnp.zeros((bsz, 2, SSD_STATE, SSD_INNER), f32)
    ssd_c, hfin = _ssd(xbc_c, dt_c, z_c, conv_w, conv_b, dtb, alog, dskip, snw, h0, bsz, n_ctx)
    ssd_o, _ = _ssd(xbc, dt, z, conv_w, conv_b, dtb, alog, dskip, snw, hfin, bsz, seq_len)

    kw = ATT_KV_HEADS * LANES
    k3, v3 = k.reshape(bsz, seq_len, kw), v.reshape(bsz, seq_len, kw)
    kc3, vc3 = k_c.reshape(bsz, n_ctx, kw), v_c.reshape(bsz, n_ctx, kw)
    sink = e_sink[0]
    att = _attention(q, g, k3, v3, kc3, vc3, sink, bsz, seq_len, True)
    att_c = _attention(q_c, g_c, kc3, vc3, kc3, vc3, sink, bsz, n_ctx, False)
    w_out = e_w_out[0].astype(bf16)
    x1 = _outproj(ssd_o, att, w_out, xf, gate, seq_len)
    xc1 = _outproj(ssd_c, att_c, w_out, xcf, gate_c, n_ctx)

    (shift, scale, gate), (shift_c, scale_c, _) = modulation(o_ada_w[0], o_ada_b[0])
    w_in = o_w_in[0].astype(bf16)
    nw = o_norm_w[0].reshape(1, d)
    u, g2 = _inproj(x1, shift, scale, nw, w_in,
                    [(0, S5_WIDTH, None, bf16), (S5_WIDTH, S5_WIDTH, None, bf16)], seq_len)
    (u_c,) = _inproj(xc1, shift_c, scale_c, nw, w_in[:, :S5_WIDTH], [(0, S5_WIDTH, None, bf16)], n_ctx)

    ab_re, ab_im, bb_re, bb_im = _s5_discretise(o_lam_re[0], o_lam_im[0], o_log_step[0], o_b_re[0], o_b_im[0])
    wbf, wbb, wc, a_f, a_b = _s5_weights(ab_re, ab_im, bb_re, bb_im, o_c_re[0], o_c_im[0])
    tmajor = lambda t, n: jnp.swapaxes(t.reshape(bsz, n, S5_WIDTH), 0, 1).reshape(n * bsz, S5_WIDTH)
    bmajor = lambda t, n: jnp.swapaxes(t.reshape(n, bsz, S5_WIDTH), 0, 1).reshape(n * bsz, S5_WIDTH)
    hz = jnp.zeros((2, bsz, 2 * S5_GROUPS * S5_STATE), f32)
    _, _, hfin_c = _s5_scan(tmajor(u_c, n_ctx), wbf, wbb, wc, a_f, a_b, hz, n_ctx)
    yf, yb, _ = _s5_scan(tmajor(u, seq_len), wbf, wbb, wc, a_f, a_b, hfin_c, seq_len)
    out = _s5_out(bmajor(yf, seq_len), bmajor(yb, seq_len), u, g2, x1, o_d_skip[0].reshape(1, -1),
                  o_glu_w[0].astype(bf16), o_glu_b[0].reshape(1, -1), o_w_out[0].astype(bf16), gate,
                  final_norm_w.reshape(1, -1), seq_len)
    return out.reshape(bsz, seq_len, d)
```

```python
import functools
import math

import jax
import jax.numpy as jnp
import numpy as np
from jax import lax
from jax.experimental import pallas as pl
from jax.experimental.pallas import tpu as pltpu

f32 = jnp.float32
bf16 = jnp.bfloat16

D_MODEL = 1024
GRID_W = 64
EPS = 1e-6
NEG_INF = -1e30

SSD_HEADS = 16
SSD_HEAD_DIM = 64
SSD_GROUPS = 2
SSD_STATE = 128
SSD_CONV = 5
SSD_CHUNK = 128
SSD_INNER = SSD_HEADS * SSD_HEAD_DIM
SSD_BC = SSD_GROUPS * SSD_STATE
SSD_XBC = SSD_INNER + 2 * SSD_BC
ATT_HEADS = 16
ATT_KV_HEADS = 4
ATT_HEAD_DIM = 64
ATT_BLOCK = 128
ROPE_THETA = 10000.0
ATT_Q = ATT_HEADS * ATT_HEAD_DIM
ATT_KVW = ATT_KV_HEADS * ATT_HEAD_DIM
S5_WIDTH = 1024
S5_GROUP_CH = 16
S5_GROUPS = S5_WIDTH // S5_GROUP_CH
S5_STATE = 64

LOG2E = math.log2(math.e)
LANES = 128
ROW_TILE = 512
S5_TIME_BLOCK = 32
S5_GBLK = 8
VMEM_LIMIT = 56 * 1024 * 1024


def _cparams(sem, flags=None):
    return pltpu.CompilerParams(dimension_semantics=sem, vmem_limit_bytes=VMEM_LIMIT, flags=flags)


def _silu(x):
    h = 0.5 * x
    return h + h * jnp.tanh(h)


def _adaln_kernel(c_ref, w_ref, b_ref, o_ref):
    c = c_ref[...]
    s = _silu(c).astype(bf16)
    o_ref[...] = jnp.dot(s, w_ref[...], preferred_element_type=f32) + b_ref[...]


def _adaln(cvecs, w, b):
    r, d = cvecs.shape
    n = w.shape[1]
    tn = 1024
    return pl.pallas_call(
        _adaln_kernel,
        out_shape=jax.ShapeDtypeStruct((r, n), f32),
        grid=(n // tn,),
        in_specs=[pl.BlockSpec((r, d), lambda j: (0, 0)),
                  pl.BlockSpec((d, tn), lambda j: (0, j)),
                  pl.BlockSpec((1, tn), lambda j: (0, j))],
        out_specs=pl.BlockSpec((r, tn), lambda j: (0, j)),
        compiler_params=_cparams(("arbitrary",)),
        name="adaln",
    )(cvecs, w, b)


def _inproj_kernel(segs, has_rope, x_ref, shift_ref, scale_ref, nw_ref, w_ref, *rest):
    if has_rope:
        cos_ref, sina_ref, sinb_ref = rest[:3]
        outs = rest[3:]
    else:
        outs = rest
    x = x_ref[...]
    ms = jnp.mean(x * x, axis=-1, keepdims=True)
    h = (x * lax.rsqrt(ms + EPS)) * nw_ref[...]
    h = h * (1.0 + scale_ref[...]) + shift_ref[...]
    hb = h.astype(bf16)
    for (start, width, rope, _), o_ref in zip(segs, outs):
        acc = jnp.dot(hb, w_ref[:, start:start + width], preferred_element_type=f32)
        if rope is not None and not has_rope:
            acc = acc * rope
        elif rope is not None:
            rep = width // LANES
            cos = jnp.concatenate([cos_ref[...]] * rep, axis=1) * rope
            sina = jnp.concatenate([sina_ref[...]] * rep, axis=1) * rope
            sinb = jnp.concatenate([sinb_ref[...]] * rep, axis=1) * rope
            half = ATT_HEAD_DIM // 2
            up = pltpu.roll(acc, width - half, axis=1)
            dn = pltpu.roll(acc, half, axis=1)
            acc = acc * cos + up * sina + dn * sinb
        o_ref[...] = acc.astype(o_ref.dtype)


def _inproj(x, shift, scale, nw, w, segs, rows_per_mod, rope_tabs=None):
    m, d = x.shape
    tm = min(ROW_TILE, rows_per_mod)
    n = w.shape[1]
    per = rows_per_mod // tm
    nmod = shift.shape[0]
    mod_idx = (lambda i: (i // per, 0, 0)) if nmod > 1 else (lambda i: (0, 0, 0))
    in_specs = [pl.BlockSpec((tm, d), lambda i: (i, 0)),
                pl.BlockSpec((None, 1, d), mod_idx),
                pl.BlockSpec((None, 1, d), mod_idx),
                pl.BlockSpec((1, d), lambda i: (0, 0)),
                pl.BlockSpec((d, n), lambda i: (0, 0))]
    args = [x, shift, scale, nw, w]
    if rope_tabs is not None:
        for t in rope_tabs:
            in_specs.append(pl.BlockSpec((tm, LANES), lambda i: (i % per, 0)))
            args.append(t)
    out_shape = [jax.ShapeDtypeStruct((m, sg[1]), sg[3]) for sg in segs]
    out_specs = [pl.BlockSpec((tm, sg[1]), lambda i: (i, 0)) for sg in segs]
    return pl.pallas_call(
        functools.partial(_inproj_kernel, tuple(segs), rope_tabs is not None),
        out_shape=out_shape,
        grid=(m // tm,),
        in_specs=in_specs,
        out_specs=out_specs,
        compiler_params=_cparams(("parallel",)),
        name="inproj",
    )(*args)


def _rope_tables(seq_len):
    rows = seq_len // GRID_W
    row = jnp.repeat(jnp.arange(rows, dtype=f32), GRID_W)
    col = jnp.tile(jnp.arange(GRID_W, dtype=f32), rows)
    n_freq = ATT_HEAD_DIM // 4
    inv = ROPE_THETA ** (-jnp.arange(n_freq, dtype=f32) / n_freq)
    ang = jnp.concatenate([row[:, None] * inv, col[:, None] * inv], axis=-1)
    cos, sin = jnp.cos(ang), jnp.sin(ang)
    zero = jnp.zeros_like(sin)
    cos_h = jnp.concatenate([cos, cos], axis=-1)
    sina_h = jnp.concatenate([-sin, zero], axis=-1)
    sinb_h = jnp.concatenate([zero, sin], axis=-1)
    two = lambda t: jnp.concatenate([t, t], axis=-1)
    return two(cos_h), two(sina_h), two(sinb_h)


SSD_PACK = 32


def _split3(x):
    hi = x.astype(bf16)
    r1 = x - hi.astype(f32)
    mid = r1.astype(bf16)
    lo = (r1 - mid.astype(f32)).astype(bf16)
    return hi, mid, lo


def _pack3(x):
    hi, mid, lo = _split3(x)
    lane = lax.broadcasted_iota(jnp.int32, x.shape, 1)
    mid_r = pltpu.roll(mid.astype(f32), SSD_PACK, axis=1)
    lo_r = pltpu.roll(lo.astype(f32), 2 * SSD_PACK, axis=1)
    packed = jnp.where(lane < SSD_PACK, hi.astype(f32),
                       jnp.where(lane < 2 * SSD_PACK, mid_r,
                                 jnp.where(lane < 3 * SSD_PACK, lo_r, 0.0)))
    return packed.astype(bf16)


def _ssd_selectors():
    k = np.arange(LANES)
    src = np.where(k < 3 * SSD_PACK, k % SSD_PACK, -1)
    col_blk = np.arange(SSD_PACK * SSD_CHUNK) // SSD_CHUNK
    sel_bc = (src[:, None] == col_blk[None, :])
    head = np.arange(SSD_INNER) // SSD_HEAD_DIM
    sel_f = (src[:, None] == head[None, :])
    sel_b = (src[:, None] == (head + SSD_HEADS)[None, :])
    tri3 = np.tile(np.tril(np.ones((SSD_CHUNK, SSD_CHUNK))), (1, 3))
    rows = np.arange(SSD_CHUNK)[:, None]
    cols = np.arange(SSD_CHUNK + 32)[None, :]
    half = SSD_CONV // 2
    shift = np.concatenate([cols == rows + 16 + d for d in range(-half, half + 1) if d != 0], axis=0)
    as_bf = lambda a: jnp.asarray(a.astype(np.float32), dtype=bf16)
    return as_bf(sel_bc), as_bf(sel_f), as_bf(sel_b), as_bf(tri3), as_bf(shift)


def _ssd_kernel(seq_len, xbc_ref, dt_ref, z_ref, cw_ref, cb_ref, dtb_ref, alog_ref, dskip_ref, nw_ref,
                selbc_ref, self_ref, selb_ref, tri3_ref, shift_ref, h0_ref, out_ref, hfin_ref,
                xs_s, bc_s, dt_s, y_s, hf_s, hb_s, win_s):
    q = SSD_CHUNK
    nc = seq_len // q
    halo = 16
    H, P, N = SSD_HEADS, SSD_HEAD_DIM, SSD_STATE
    gw = (H // SSD_GROUPS) * P
    a2_row = -jnp.exp(alog_ref[...]) * math.log2(math.e)

    def conv_chunk(c):
        r0 = pl.multiple_of(c * q, q)
        pstart = pl.multiple_of(jnp.maximum(r0 - halo, 0), halo)
        nstart = pl.multiple_of(jnp.minimum(r0 + q, seq_len - halo), halo)
        zero = jnp.zeros((), bf16)
        win_s[0:halo, :] = jnp.where(c > 0, xbc_ref[pl.ds(pstart, halo), :], zero)
        win_s[halo:halo + q, :] = xbc_ref[pl.ds(r0, q), :]
        win_s[halo + q:, :] = jnp.where(c < nc - 1, xbc_ref[pl.ds(nstart, halo), :], zero)
        taps = [k for k in range(SSD_CONV) if k != SSD_CONV // 2]
        cw = 2 * LANES
        for j in range(SSD_XBC // cw):
            cs = slice(j * cw, (j + 1) * cw)
            sh = jnp.dot(shift_ref[...], win_s[:, cs], preferred_element_type=f32)
            acc = cb_ref[:, cs] + win_s[halo:halo + q, cs].astype(f32) * cw_ref[SSD_CONV // 2:SSD_CONV // 2 + 1, cs]
            for n, k in enumerate(taps):
                acc = acc + sh[n * q:(n + 1) * q, :] * cw_ref[k:k + 1, cs]
            act = _silu(acc)
            if j < SSD_INNER // cw:
                xs_s[pl.ds(r0, q), cs] = act
            else:
                bc_s[pl.ds(r0, q), j * cw - SSD_INNER:(j + 1) * cw - SSD_INNER] = act.astype(bf16)
        dt_s[pl.ds(r0, q), :] = jax.nn.softplus(dt_ref[pl.ds(r0, q), :] + dtb_ref[...])

    conv_chunk(0)
    hf_s[...] = h0_ref[0]
    hb_s[...] = h0_ref[1]

    ri = lax.broadcasted_iota(jnp.int32, (q, q), 0)
    ci = lax.broadcasted_iota(jnp.int32, (q, q), 1)
    lower = ri >= ci
    upper = ci >= ri
    lane = lax.broadcasted_iota(jnp.int32, (q, LANES), 1)
    lo_half = lane < P

    def cumsums(dt):
        dta = dt * a2_row
        cf = jnp.dot(tri3_ref[...], jnp.concatenate(_split3(dta), axis=0), preferred_element_type=f32)
        rb = cf[q - 1:q, :] - cf + dta
        return cf, rb

    def load_chunk(r0):
        dt = dt_s[pl.ds(r0, q), :]
        xs = xs_s[pl.ds(r0, q), :]
        bcv = bc_s[pl.ds(r0, q), :]
        bmat = [bcv[:, g * N:(g + 1) * N] for g in range(SSD_GROUPS)]
        cmat = [bcv[:, SSD_BC + g * N:SSD_BC + (g + 1) * N] for g in range(SSD_GROUPS)]
        return dt, xs, bmat, cmat

    def inter_chunk(h_s, sel_ref, decay, weight, xs, bmat, cmat, dec_idx):
        ew = jnp.dot(jnp.concatenate([_pack3(decay), _pack3(weight)], axis=0), sel_ref[...],
                     preferred_element_type=f32)
        e_x, w_x = ew[:q], ew[q:]
        hb_ = h_s[...].astype(bf16)
        yoff = jnp.concatenate(
            [jnp.dot(cmat[g], hb_[:, g * gw:(g + 1) * gw], preferred_element_type=f32)
             for g in range(SSD_GROUPS)], axis=1)
        xw = (xs * w_x).astype(bf16)
        dec_row = e_x[dec_idx:dec_idx + 1, :]
        for g in range(SSD_GROUPS):
            gs = slice(g * gw, (g + 1) * gw)
            bt = jnp.transpose(bmat[g].astype(f32)).astype(bf16)
            upd = jnp.dot(bt, xw[:, gs], preferred_element_type=f32)
            h_s[:, gs] = h_s[:, gs] * dec_row[:, gs] + upd
        return yoff * e_x

    def fwd_body(c, carry):
        r0 = pl.multiple_of(c * q, q)
        dt, xs, bmat, cmat = load_chunk(r0)
        cf, rb = cumsums(dt)
        pcol = jnp.where(lane < H, cf, rb)
        bcast = jnp.dot(_pack3(pcol), selbc_ref[...], preferred_element_type=f32)
        prow = jnp.transpose(pcol - jnp.log2(dt))
        cbm = [lax.dot_general(cmat[g], bmat[g], (((1,), (1,)), ((), ())), preferred_element_type=f32)
               for g in range(SSD_GROUPS)]
        xsb = xs.astype(bf16)
        ypairs = []
        for k in range(H // 2):
            res = []
            for h in (2 * k, 2 * k + 1):
                g = h // (H // SSD_GROUPS)
                hb_ = H + h
                segf = bcast[:, h * q:(h + 1) * q] - prow[h:h + 1, :]
                segb = bcast[:, hb_ * q:(hb_ + 1) * q] - prow[hb_:hb_ + 1, :]
                df = jnp.exp2(jnp.where(lower, segf, NEG_INF))
                db = jnp.exp2(jnp.where(upper, segb, NEG_INF))
                mh = (cbm[g] * (df + db)).astype(bf16)
                res.append(jnp.dot(mh, xsb[:, k * LANES:(k + 1) * LANES], preferred_element_type=f32))
            ypairs.append(jnp.where(lo_half, res[0], res[1]))
        y = jnp.concatenate(ypairs, axis=1)
        wfa = jnp.exp2(cf[q - 1:q, :] - cf) * dt
        y_s[pl.ds(r0, q), :] = y + inter_chunk(hf_s, self_ref, jnp.exp2(cf), wfa, xs, bmat, cmat, q - 1)
        conv_chunk(jnp.minimum(c + 1, nc - 1))
        return carry

    lax.fori_loop(0, nc, fwd_body, 0)

    def bwd_body(i, carry):
        c = nc - 1 - i
        r0 = pl.multiple_of(c * q, q)
        dt, xs, bmat, cmat = load_chunk(r0)
        _, rb = cumsums(dt)
        wba = jnp.exp2(rb[0:1, :] - rb) * dt
        y = y_s[pl.ds(r0, q), :] + inter_chunk(hb_s, selb_ref, jnp.exp2(rb), wba, xs, bmat, cmat, 0)
        yy = y + xs * dskip_ref[...]
        zz = z_ref[pl.ds(r0, q), :].astype(f32)
        gated = yy * _silu(zz)
        ms = jnp.mean(gated * gated, axis=-1, keepdims=True)
        out_ref[pl.ds(r0, q), :] = (gated * lax.rsqrt(ms + EPS) * nw_ref[...]).astype(out_ref.dtype)
        return carry

    lax.fori_loop(0, nc, bwd_body, 0)
    hfin_ref[0] = hf_s[...]
    hfin_ref[1] = hb_s[...]


def _ssd(xbc, dt, z, conv_w, conv_b, dtb, alog, dskip, nw, h0, bsz, seq_len):
    one = pl.Buffered(1)
    seq = lambda w: pl.BlockSpec((seq_len, w), lambda b: (b, 0), pipeline_mode=one)
    const = lambda r, w: pl.BlockSpec((r, w), lambda b: (0, 0))
    st = pl.BlockSpec((None, 2, SSD_STATE, SSD_INNER), lambda b: (b, 0, 0, 0))
    sels = _ssd_selectors()
    return pl.pallas_call(
        functools.partial(_ssd_kernel, seq_len),
        out_shape=[jax.ShapeDtypeStruct((bsz * seq_len, SSD_INNER), bf16),
                   jax.ShapeDtypeStruct((bsz, 2, SSD_STATE, SSD_INNER), f32)],
        grid=(bsz,),
        in_specs=[seq(SSD_XBC), seq(LANES), seq(SSD_INNER),
                  const(8, SSD_XBC), const(1, SSD_XBC), const(1, LANES), const(1, LANES),
                  const(1, SSD_INNER), const(1, SSD_INNER)]
                 + [const(*s.shape) for s in sels] + [st],
        out_specs=[seq(SSD_INNER), st],
        scratch_shapes=[pltpu.VMEM((seq_len, SSD_INNER), f32),
                        pltpu.VMEM((seq_len, 2 * SSD_BC), bf16),
                        pltpu.VMEM((seq_len, LANES), f32),
                        pltpu.VMEM((seq_len, SSD_INNER), f32),
                        pltpu.VMEM((SSD_STATE, SSD_INNER), f32),
                        pltpu.VMEM((SSD_STATE, SSD_INNER), f32),
                        pltpu.VMEM((SSD_CHUNK + 32, SSD_XBC), bf16)],
        compiler_params=_cparams(("parallel",)),
        name="ssd",
    )(xbc, dt, z, conv_w, conv_b, dtb, alog, dskip, nw, *sels, h0)


def _attn_kernel(n_blocks, local, q_ref, g_ref, k_ref, v_ref, kc_ref, vc_ref, sink_ref, o_ref, s_s):
    t = ATT_BLOCK
    i = pl.program_id(1)
    n_ctx = kc_ref.shape[0]
    rpk = ATT_HEADS // ATT_KV_HEADS
    lane = lax.broadcasted_iota(jnp.int32, (t, LANES), 1)
    lo_half = lane < ATT_HEAD_DIM
    qv = q_ref[...]
    gv = g_ref[...].astype(f32)
    if local:
        p0 = pl.multiple_of(jnp.maximum(i - 1, 0) * t, t)
        c0 = pl.multiple_of(i * t, t)
        n0 = pl.multiple_of(jnp.minimum(i + 1, n_blocks - 1) * t, t)
        ql = lax.broadcasted_iota(jnp.int32, (t, t), 0)
        kl = lax.broadcasted_iota(jnp.int32, (t, t), 1)
        bias_prev = jnp.where((kl >= ql) & (i > 0), 0.0, NEG_INF)
        bias_next = jnp.where((kl <= ql) & (i < n_blocks - 1), 0.0, NEG_INF)
        bias_prev = jnp.concatenate([bias_prev] * rpk, axis=0)
        bias_next = jnp.concatenate([bias_next] * rpk, axis=0)
    zero_b = jnp.zeros((), bf16)
    for j in range(ATT_KV_HEADS):
        ls = slice(j * LANES, (j + 1) * LANES)
        if local:
            kk = jnp.concatenate([k_ref[pl.ds(p0, t), ls], k_ref[pl.ds(c0, t), ls],
                                  k_ref[pl.ds(n0, t), ls], kc_ref[:, ls]], axis=0)
        else:
            kk = kc_ref[:, ls]
        pieces = []
        for r in range(rpk):
            hq = j * rpk + r
            qp = qv[:, (hq // 2) * LANES:(hq // 2 + 1) * LANES]
            keep = lo_half if hq % 2 == 0 else jnp.logical_not(lo_half)
            pieces.append(jnp.where(keep, qp, zero_b))
        q4 = jnp.concatenate(pieces, axis=0)
        s = lax.dot_general(q4, kk, (((1,), (1,)), ((), ())), preferred_element_type=f32)
        if local:
            s = jnp.concatenate([s[:, :t] + bias_prev, s[:, t:2 * t], s[:, 2 * t:3 * t] + bias_next,
                                 s[:, 3 * t:]], axis=1)
        s_s[j] = s
    outs = []
    for j in range(ATT_KV_HEADS):
        ls = slice(j * LANES, (j + 1) * LANES)
        if local:
            vv = jnp.concatenate([v_ref[pl.ds(p0, t), ls], v_ref[pl.ds(c0, t), ls],
                                  v_ref[pl.ds(n0, t), ls], vc_ref[:, ls]], axis=0)
        else:
            vv = vc_ref[:, ls]
        sk = jnp.concatenate([jnp.full((t, 1), sink_ref[j * rpk + r] * LOG2E, f32) for r in range(rpk)],
                             axis=0)
        s = s_s[j]
        m = jnp.maximum(jnp.max(s, axis=1, keepdims=True), sk)
        p = jnp.exp2(s - m)
        den = jnp.sum(p, axis=1, keepdims=True) + jnp.exp2(sk - m)
        o4 = jnp.dot(p.astype(bf16), vv, preferred_element_type=f32) / den
        outs.append(jnp.where(lo_half, o4[0:t], o4[t:2 * t]))
        outs.append(jnp.where(lo_half, o4[2 * t:3 * t], o4[3 * t:4 * t]))
    o = jnp.concatenate(outs, axis=1)
    o_ref[...] = (o * _silu(gv)).astype(o_ref.dtype)


def _attention(q, g, k, v, kc, vc, sink, bsz, seq_len, local):
    t = ATT_BLOCK
    nb = seq_len // t
    n_ctx = kc.shape[1]
    kw = ATT_KV_HEADS * LANES
    blk = pl.BlockSpec((t, ATT_Q), lambda b, i: (b * nb + i, 0))
    full = lambda n: pl.BlockSpec((None, n, kw), lambda b, i: (b, 0, 0))
    return pl.pallas_call(
        functools.partial(_attn_kernel, nb, local),
        out_shape=jax.ShapeDtypeStruct((bsz * seq_len, ATT_Q), bf16),
        grid=(bsz, nb),
        in_specs=[blk, blk, full(k.shape[1]), full(v.shape[1]), full(n_ctx), full(n_ctx),
                  pl.BlockSpec(memory_space=pltpu.SMEM)],
        out_specs=blk,
        scratch_shapes=[pltpu.VMEM((ATT_KV_HEADS, (ATT_HEADS // ATT_KV_HEADS) * t,
                                    (3 * t if local else 0) + n_ctx), f32)],
        compiler_params=_cparams(("parallel", "arbitrary")),
        name="attention",
    )(q, g, k, v, kc, vc, sink)


def _outproj_kernel(a_ref, b_ref, w_ref, x_ref, gate_ref, o_ref):
    ka = a_ref.shape[1]
    acc = jnp.dot(a_ref[...], w_ref[:ka, :], preferred_element_type=f32)
    acc = acc + jnp.dot(b_ref[...], w_ref[ka:, :], preferred_element_type=f32)
    o_ref[...] = x_ref[...] + gate_ref[...] * acc


def _outproj(a, b, w, x, gate, rows_per_mod):
    m, d = x.shape
    tm = min(ROW_TILE, rows_per_mod)
    per = rows_per_mod // tm
    nmod = gate.shape[0]
    mod_idx = (lambda i: (i // per, 0, 0)) if nmod > 1 else (lambda i: (0, 0, 0))
    row = lambda w_: pl.BlockSpec((tm, w_), lambda i: (i, 0))
    return pl.pallas_call(
        _outproj_kernel,
        out_shape=jax.ShapeDtypeStruct((m, d), f32),
        grid=(m // tm,),
        in_specs=[row(a.shape[1]), row(b.shape[1]),
                  pl.BlockSpec(w.shape, lambda i: (0, 0)), row(d),
                  pl.BlockSpec((None, 1, d), mod_idx)],
        out_specs=row(d),
        compiler_params=_cparams(("parallel",)),
        name="outproj",
    )(a, b, w, x, gate)


def _s5_disc_kernel(lre_ref, lim_ref, ls_ref, bre_ref, bim_ref, abre_ref, abim_ref, bbre_ref, bbim_ref):
    lam_re = lre_ref[...]
    lam_im = lim_ref[...]
    dt = jnp.exp(ls_ref[...])
    mag = jnp.exp(lam_re * dt)
    ab_re = mag * jnp.cos(lam_im * dt)
    ab_im = mag * jnp.sin(lam_im * dt)
    num_re, num_im = ab_re - 1.0, ab_im
    den = lam_re * lam_re + lam_im * lam_im
    coef_re = (num_re * lam_re + num_im * lam_im) / den
    coef_im = (num_im * lam_re - num_re * lam_im) / den
    b_re, b_im = bre_ref[...], bim_ref[...]
    abre_ref[...] = ab_re
    abim_ref[...] = ab_im
    bbre_ref[...] = coef_re * b_re - coef_im * b_im
    bbim_ref[...] = coef_re * b_im + coef_im * b_re


def _s5_discretise(lam_re, lam_im, log_step, b_re, b_im):
    g, n, cg = b_re.shape
    exp = lambda t: jnp.repeat(t.reshape(2 * g, n), cg, axis=1)
    ls = jnp.broadcast_to(log_step.reshape(2 * g, 1), (2 * g, n * cg))
    bb = lambda t: jnp.tile(t.reshape(g, n * cg), (2, 1))
    shp = jax.ShapeDtypeStruct((2 * g, n * cg), f32)
    ab_re, ab_im, bb_re, bb_im = pl.pallas_call(
        _s5_disc_kernel, out_shape=[shp] * 4, name="s5_disc",
    )(exp(lam_re), exp(lam_im), ls, bb(b_re), bb(b_im))
    first = lambda t: t.reshape(2, g, n, cg)[..., 0]
    full = lambda t: t.reshape(2, g, n, cg)
    return first(ab_re), first(ab_im), full(bb_re), full(bb_im)


def _s5_kernel(tblk, uf_ref, ub_ref, wbf_ref, wbb_ref, wc_ref, af_ref, ab_ref, h0_ref,
               yf_ref, yb_ref, hfin_ref, buf_s, bub_s, hst_s):
    i = pl.program_id(0)
    nblk = S5_GROUPS // S5_GBLK
    sw = S5_GBLK * S5_STATE
    rows = tblk * 8

    @pl.when(i == 0)
    def _():
        hst_s[...] = h0_ref[...]

    uf = uf_ref[...]
    ub = ub_ref[...]
    for j in range(nblk):
        cs = slice(j * LANES, (j + 1) * LANES)
        ws = slice(j * 2 * sw, (j + 1) * 2 * sw)
        buf_s[:, ws] = jnp.dot(uf[:, cs], wbf_ref[j], preferred_element_type=f32)
        bub_s[:, ws] = jnp.dot(ub[:, cs], wbb_ref[j], preferred_element_type=f32)

    for j in range(nblk):
        re = slice(j * 2 * sw, j * 2 * sw + sw)
        im = slice(j * 2 * sw + sw, (j + 1) * 2 * sw)
        bc = lambda r, s: jnp.broadcast_to(r[:, s], (8, sw))
        afr, afi = bc(af_ref, re), bc(af_ref, im)
        abr, abi = bc(ab_ref, re), bc(ab_ref, im)

        def step(tt, carry):
            hfr, hfi, hbr, hbi = carry
            rf = pl.multiple_of(tt * 8, 8)
            rbk = pl.multiple_of((tblk - 1 - tt) * 8, 8)
            nfr = afr * hfr - afi * hfi + buf_s[pl.ds(rf, 8), re]
            nfi = afr * hfi + afi * hfr + buf_s[pl.ds(rf, 8), im]
            nbr = abr * hbr - abi * hbi + bub_s[pl.ds(rbk, 8), re]
            nbi = abr * hbi + abi * hbr + bub_s[pl.ds(rbk, 8), im]
            buf_s[pl.ds(rf, 8), re] = nfr
            buf_s[pl.ds(rf, 8), im] = nfi
            bub_s[pl.ds(rbk, 8), re] = nbr
            bub_s[pl.ds(rbk, 8), im] = nbi
            return nfr, nfi, nbr, nbi

        init = (hst_s[0, :, re], hst_s[0, :, im], hst_s[1, :, re], hst_s[1, :, im])
        hfr, hfi, hbr, hbi = lax.fori_loop(0, tblk, step, init, unroll=4)
        hst_s[0, :, re] = hfr
        hst_s[0, :, im] = hfi
        hst_s[1, :, re] = hbr
        hst_s[1, :, im] = hbi

    for j in range(nblk):
        cs = slice(j * LANES, (j + 1) * LANES)
        ws = slice(j * 2 * sw, (j + 1) * 2 * sw)
        yf_ref[:, cs] = jnp.dot(buf_s[:, ws].astype(bf16), wc_ref[j], preferred_element_type=f32)
        yb_ref[:, cs] = jnp.dot(bub_s[:, ws].astype(bf16), wc_ref[j], preferred_element_type=f32)

    hfin_ref[...] = hst_s[...]


def _s5_scan(u_t, wbf, wbb, wc, a_f, a_b, h0, seq_len):
    tblk = min(S5_TIME_BLOCK, seq_len)
    nt = seq_len // tblk
    rows = tblk * 8
    nstate = 2 * S5_GROUPS * S5_STATE
    const3 = lambda a: pl.BlockSpec(a.shape, lambda i: (0, 0, 0))
    const2 = lambda a: pl.BlockSpec(a.shape, lambda i: (0, 0))
    return pl.pallas_call(
        functools.partial(_s5_kernel, tblk),
        out_shape=[jax.ShapeDtypeStruct((seq_len * 8, S5_WIDTH), f32),
                   jax.ShapeDtypeStruct((seq_len * 8, S5_WIDTH), f32),
                   jax.ShapeDtypeStruct((2, 8, nstate), f32)],
        grid=(nt,),
        in_specs=[pl.BlockSpec((rows, S5_WIDTH), lambda i: (i, 0)),
                  pl.BlockSpec((rows, S5_WIDTH), lambda i: (nt - 1 - i, 0)),
                  const3(wbf), const3(wbb), const3(wc), const2(a_f), const2(a_b), const3(h0)],
        out_specs=[pl.BlockSpec((rows, S5_WIDTH), lambda i: (i, 0)),
                   pl.BlockSpec((rows, S5_WIDTH), lambda i: (nt - 1 - i, 0)),
                   pl.BlockSpec((2, 8, nstate), lambda i: (0, 0, 0))],
        scratch_shapes=[pltpu.VMEM((rows, nstate), f32), pltpu.VMEM((rows, nstate), f32),
                        pltpu.VMEM((2, 8, nstate), f32)],
        compiler_params=_cparams(("arbitrary",)),
        name="s5_scan",
    )(u_t, u_t, wbf, wbb, wc, a_f, a_b, h0)


def _s5_weights(ab_re, ab_im, bb_re, bb_im, c_re, c_im):
    g, n, cg = S5_GROUPS, S5_STATE, S5_GROUP_CH
    nblk = g // S5_GBLK
    eye = jnp.eye(S5_GBLK, dtype=f32)

    def wb_dir(bre, bim):
        def bd(t):
            t = t.reshape(nblk, S5_GBLK, n, cg)
            return jnp.einsum('jgnc,gh->jgchn', t, eye).reshape(nblk, S5_GBLK * cg, S5_GBLK * n)
        return jnp.concatenate([bd(bre), bd(bim)], axis=-1).astype(bf16)

    def a_dir(are, aim):
        are = are.reshape(nblk, S5_GBLK * n)
        aim = aim.reshape(nblk, S5_GBLK * n)
        return jnp.concatenate([are, aim], axis=-1).reshape(1, 2 * g * n)

    def bdc(t):
        t = t.reshape(nblk, S5_GBLK, cg, n)
        return jnp.einsum('jgcn,gh->jgnhc', t, eye).reshape(nblk, S5_GBLK * n, S5_GBLK * cg)

    wc = jnp.concatenate([bdc(c_re), -bdc(c_im)], axis=1).astype(bf16)
    return (wb_dir(bb_re[0], bb_im[0]), wb_dir(bb_re[1], bb_im[1]), wc,
            a_dir(ab_re[0], ab_im[0]), a_dir(ab_re[1], ab_im[1]))


def _s5_out_kernel(yf_ref, yb_ref, u_ref, g_ref, x_ref, dskip_ref, gw_ref, gb_ref, w_ref, gate_ref,
                   fw_ref, o_ref):
    y = yf_ref[...] + yb_ref[...] + dskip_ref[...] * u_ref[...].astype(f32)
    y = jax.nn.gelu(y)
    glu = jnp.dot(y.astype(bf16), gw_ref[...], preferred_element_type=f32) + gb_ref[...]
    y = y * jax.nn.sigmoid(glu)
    y = y * _silu(g_ref[...].astype(f32))
    x = x_ref[...] + gate_ref[...] * jnp.dot(y.astype(bf16), w_ref[...], preferred_element_type=f32)
    ms = jnp.mean(x * x, axis=-1, keepdims=True)
    o_ref[...] = x * lax.rsqrt(ms + EPS) * fw_ref[...]


def _s5_out(yf, yb, u, g, x, dskip, gw, gb, w, gate, fw, rows_per_mod):
    m, d = x.shape
    tm = min(ROW_TILE, rows_per_mod)
    per = rows_per_mod // tm
    row = lambda: pl.BlockSpec((tm, d), lambda i: (i, 0))
    vec = lambda: pl.BlockSpec((1, d), lambda i: (0, 0))
    mat = lambda: pl.BlockSpec((d, d), lambda i: (0, 0))
    return pl.pallas_call(
        _s5_out_kernel,
        out_shape=jax.ShapeDtypeStruct((m, d), f32),
        grid=(m // tm,),
        in_specs=[row(), row(), row(), row(), row(), vec(), mat(), vec(), mat(),
                  pl.BlockSpec((None, 1, d), lambda i: (i // per, 0, 0)), vec()],
        out_specs=row(),
        compiler_params=_cparams(("parallel",)),
        name="s5_out",
    )(yf, yb, u, g, x, dskip, gw, gb, w, gate, fw)


def _even_weights(w_in):
    o = 0
    z = w_in[:, o:o + SSD_INNER]; o += SSD_INNER
    xbc = w_in[:, o:o + SSD_XBC]; o += SSD_XBC
    dt = w_in[:, o:o + 2 * SSD_HEADS]; o += 2 * SSD_HEADS
    q = w_in[:, o:o + ATT_Q]; o += ATT_Q
    k = w_in[:, o:o + ATT_KVW]; o += ATT_KVW
    v = w_in[:, o:o + ATT_KVW]; o += ATT_KVW
    g = w_in[:, o:o + ATT_Q]
    d = w_in.shape[0]
    dup = lambda t: jnp.concatenate([t.reshape(d, ATT_KV_HEADS, 1, ATT_HEAD_DIM)] * 2, axis=2).reshape(d, -1)
    dtp = jnp.pad(dt, ((0, 0), (0, LANES - 2 * SSD_HEADS)))
    return jnp.concatenate([z, xbc, q, dup(k), dup(v), g, dtp], axis=1).astype(bf16)


def _even_segs(rope):
    scale = ATT_HEAD_DIM ** -0.5 * LOG2E
    widths = [(SSD_INNER, None, bf16), (SSD_XBC, None, bf16),
              (ATT_Q, scale, bf16),
              (ATT_KV_HEADS * LANES, 1.0 if rope else None, bf16),
              (ATT_KV_HEADS * LANES, None, bf16), (ATT_Q, None, bf16), (LANES, None, f32)]
    segs, o = [], 0
    for w, r, dtp in widths:
        segs.append((o, w, r, dtp))
        o += w
    return segs


def _pad_lanes(v, n=LANES):
    v = v.reshape(1, -1)
    return jnp.pad(v, ((0, 0), (0, n - v.shape[1])))


def kernel(x, c, ctx, c_ctx, e_norm_w, e_ada_w, e_ada_b, e_w_in, e_conv_w, e_conv_b, e_dt_bias,
           e_a_log, e_d_skip, e_ssd_norm_w, e_sink, e_w_out, o_norm_w, o_ada_w, o_ada_b, o_w_in,
           o_lam_re, o_lam_im, o_log_step, o_b_re, o_b_im, o_c_re, o_c_im, o_d_skip, o_glu_w,
           o_glu_b, o_w_out, final_norm_w):
    bsz, seq_len, d = x.shape
    n_ctx = ctx.shape[1]
    xf = x.reshape(bsz * seq_len, d)
    xcf = ctx.reshape(bsz * n_ctx, d)

    cvecs = jnp.concatenate([c, c_ctx[None, :], jnp.zeros((16 - bsz - 1, d), f32)], axis=0)

    def modulation(ada_w, ada_b):
        mod = _adaln(cvecs, ada_w.astype(bf16), ada_b.reshape(1, -1))
        parts = [mod[:, k * d:(k + 1) * d] for k in range(3)]
        lat = [p[:bsz].reshape(bsz, 1, d) for p in parts]
        cx = [p[bsz:bsz + 1].reshape(1, 1, d) for p in parts]
        return lat, cx

    (shift, scale, gate), (shift_c, scale_c, gate_c) = modulation(e_ada_w[0], e_ada_b[0])
    w_in = _even_weights(e_w_in[0])
    nw = e_norm_w[0].reshape(1, d)
    tabs = _rope_tables(seq_len)
    z, xbc, q, k, v, g, dt = _inproj(xf, shift, scale, nw, w_in, _even_segs(True), seq_len, tabs)
    z_c, xbc_c, q_c, k_c, v_c, g_c, dt_c = _inproj(xcf, shift_c, scale_c, nw, w_in, _even_segs(False), n_ctx)

    conv_w = jnp.pad(e_conv_w[0], ((0, 8 - SSD_CONV), (0, 0)))
    conv_b = e_conv_b[0].reshape(1, -1)
    dtb = _pad_lanes(e_dt_bias[0])
    alog = _pad_lanes(e_a_log[0])
    dskip = jnp.repeat(e_d_skip[0], SSD_HEAD_DIM).reshape(1, -1)
    snw = e_ssd_norm_w[0].reshape(1, -1)
    h0 = jnp.zeros((bsz, 2, SSD_STATE, SSD_INNER), f32)
    ssd_c, hfin = _ssd(xbc_c, dt_c, z_c, conv_w, conv_b, dtb, alog, dskip, snw, h0, bsz, n_ctx)
    ssd_o, _ = _ssd(xbc, dt, z, conv_w, conv_b, dtb, alog, dskip, snw, hfin, bsz, seq_len)

    kw = ATT_KV_HEADS * LANES
    k3, v3 = k.reshape(bsz, seq_len, kw), v.reshape(bsz, seq_len, kw)
    kc3, vc3 = k_c.reshape(bsz, n_ctx, kw), v_c.reshape(bsz, n_ctx, kw)
    sink = e_sink[0]
    att = _attention(q, g, k3, v3, kc3, vc3, sink, bsz, seq_len, True)
    att_c = _attention(q_c, g_c, kc3, vc3, kc3, vc3, sink, bsz, n_ctx, False)
    w_out = e_w_out[0].astype(bf16)
    x1 = _outproj(ssd_o, att, w_out, xf, gate, seq_len)
    xc1 = _outproj(ssd_c, att_c, w_out, xcf, gate_c, n_ctx)

    (shift, scale, gate), (shift_c, scale_c, _) = modulation(o_ada_w[0], o_ada_b[0])
    w_in = o_w_in[0].astype(bf16)
    nw = o_norm_w[0].reshape(1, d)
    u, g2 = _inproj(x1, shift, scale, nw, w_in,
                    [(0, S5_WIDTH, None, bf16), (S5_WIDTH, S5_WIDTH, None, bf16)], seq_len)
    (u_c,) = _inproj(xc1, shift_c, scale_c, nw, w_in[:, :S5_WIDTH], [(0, S5_WIDTH, None, bf16)], n_ctx)

    ab_re, ab_im, bb_re, bb_im = _s5_discretise(o_lam_re[0], o_lam_im[0], o_log_step[0], o_b_re[0], o_b_im[0])
    wbf, wbb, wc, a_f, a_b = _s5_weights(ab_re, ab_im, bb_re, bb_im, o_c_re[0], o_c_im[0])
    tmajor = lambda t, n: jnp.swapaxes(t.reshape(bsz, n, S5_WIDTH), 0, 1).reshape(n * bsz, S5_WIDTH)
    bmajor = lambda t, n: jnp.swapaxes(t.reshape(n, bsz, S5_WIDTH), 0, 1).reshape(n * bsz, S5_WIDTH)
    hz = jnp.zeros((2, bsz, 2 * S5_GROUPS * S5_STATE), f32)
    _, _, hfin_c = _s5_scan(tmajor(u_c, n_ctx), wbf, wbb, wc, a_f, a_b, hz, n_ctx)
    yf, yb, _ = _s5_scan(tmajor(u, seq_len), wbf, wbb, wc, a_f, a_b, hfin_c, seq_len)
    out = _s5_out(bmajor(yf, seq_len), bmajor(yb, seq_len), u, g2, x1, o_d_skip[0].reshape(1, -1),
                  o_glu_w[0].astype(bf16), o_glu_b[0].reshape(1, -1), o_w_out[0].astype(bf16), gate,
                  final_norm_w.reshape(1, -1), seq_len)
    return out.reshape(bsz, seq_len, d)
```

```python
import functools
import math

import jax
import jax.numpy as jnp
import numpy as np
from jax import lax
from jax.experimental import pallas as pl
from jax.experimental.pallas import tpu as pltpu

f32 = jnp.float32
bf16 = jnp.bfloat16

D_MODEL = 1024
GRID_W = 64
EPS = 1e-6
NEG_INF = -1e30

SSD_HEADS = 16
SSD_HEAD_DIM = 64
SSD_GROUPS = 2
SSD_STATE = 128
SSD_CONV = 5
SSD_CHUNK = 128
SSD_INNER = SSD_HEADS * SSD_HEAD_DIM
SSD_BC = SSD_GROUPS * SSD_STATE
SSD_XBC = SSD_INNER + 2 * SSD_BC
ATT_HEADS = 16
ATT_KV_HEADS = 4
ATT_HEAD_DIM = 64
ATT_BLOCK = 128
ROPE_THETA = 10000.0
ATT_Q = ATT_HEADS * ATT_HEAD_DIM
ATT_KVW = ATT_KV_HEADS * ATT_HEAD_DIM
S5_WIDTH = 1024
S5_GROUP_CH = 16
S5_GROUPS = S5_WIDTH // S5_GROUP_CH
S5_STATE = 64

LOG2E = math.log2(math.e)
LANES = 128
ROW_TILE = 512
S5_GBLK = 8
VMEM_LIMIT = 56 * 1024 * 1024


def _cparams(sem, flags=None):
    return pltpu.CompilerParams(dimension_semantics=sem, vmem_limit_bytes=VMEM_LIMIT, flags=flags)


def _silu(x):
    h = 0.5 * x
    return h + h * jnp.tanh(h)


def _adaln_kernel(c_ref, w_ref, b_ref, o_ref):
    c = c_ref[...]
    s = _silu(c).astype(bf16)
    o_ref[...] = jnp.dot(s, w_ref[...], preferred_element_type=f32) + b_ref[...]


def _adaln(cvecs, w, b):
    r, d = cvecs.shape
    n = w.shape[1]
    tn = 1024
    return pl.pallas_call(
        _adaln_kernel,
        out_shape=jax.ShapeDtypeStruct((r, n), f32),
        grid=(n // tn,),
        in_specs=[pl.BlockSpec((r, d), lambda j: (0, 0)),
                  pl.BlockSpec((d, tn), lambda j: (0, j)),
                  pl.BlockSpec((1, tn), lambda j: (0, j))],
        out_specs=pl.BlockSpec((r, tn), lambda j: (0, j)),
        compiler_params=_cparams(("arbitrary",)),
        name="adaln",
    )(cvecs, w, b)


def _inproj_kernel(segs, has_rope, x_ref, shift_ref, scale_ref, nw_ref, w_ref, *rest):
    if has_rope:
        cos_ref, sina_ref, sinb_ref = rest[:3]
        outs = rest[3:]
    else:
        outs = rest
    x = x_ref[...]
    ms = jnp.mean(x * x, axis=-1, keepdims=True)
    h = (x * lax.rsqrt(ms + EPS)) * nw_ref[...]
    h = h * (1.0 + scale_ref[...]) + shift_ref[...]
    hb = h.astype(bf16)
    for (start, width, rope, _), o_ref in zip(segs, outs):
        acc = jnp.dot(hb, w_ref[:, start:start + width], preferred_element_type=f32)
        if rope is not None and not has_rope:
            acc = acc * rope
        elif rope is not None:
            rep = width // LANES
            cos = jnp.concatenate([cos_ref[...]] * rep, axis=1) * rope
            sina = jnp.concatenate([sina_ref[...]] * rep, axis=1) * rope
            sinb = jnp.concatenate([sinb_ref[...]] * rep, axis=1) * rope
            half = ATT_HEAD_DIM // 2
            up = pltpu.roll(acc, width - half, axis=1)
            dn = pltpu.roll(acc, half, axis=1)
            acc = acc * cos + up * sina + dn * sinb
        o_ref[...] = acc.astype(o_ref.dtype)


def _inproj(x, shift, scale, nw, w, segs, rows_per_mod, rope_tabs=None):
    m, d = x.shape
    tm = min(ROW_TILE, rows_per_mod)
    n = w.shape[1]
    per = rows_per_mod // tm
    nmod = shift.shape[0]
    mod_idx = (lambda i: (i // per, 0, 0)) if nmod > 1 else (lambda i: (0, 0, 0))
    in_specs = [pl.BlockSpec((tm, d), lambda i: (i, 0)),
                pl.BlockSpec((None, 1, d), mod_idx),
                pl.BlockSpec((None, 1, d), mod_idx),
                pl.BlockSpec((1, d), lambda i: (0, 0)),
                pl.BlockSpec((d, n), lambda i: (0, 0))]
    args = [x, shift, scale, nw, w]
    if rope_tabs is not None:
        for t in rope_tabs:
            in_specs.append(pl.BlockSpec((tm, LANES), lambda i: (i % per, 0)))
            args.append(t)
    out_shape = [jax.ShapeDtypeStruct((m, sg[1]), sg[3]) for sg in segs]
    out_specs = [pl.BlockSpec((tm, sg[1]), lambda i: (i, 0)) for sg in segs]
    return pl.pallas_call(
        functools.partial(_inproj_kernel, tuple(segs), rope_tabs is not None),
        out_shape=out_shape,
        grid=(m // tm,),
        in_specs=in_specs,
        out_specs=out_specs,
        compiler_params=_cparams(("parallel",)),
        name="inproj",
    )(*args)


def _rope_tables(seq_len):
    rows = seq_len // GRID_W
    row = jnp.repeat(jnp.arange(rows, dtype=f32), GRID_W)
    col = jnp.tile(jnp.arange(GRID_W, dtype=f32), rows)
    n_freq = ATT_HEAD_DIM // 4
    inv = ROPE_THETA ** (-jnp.arange(n_freq, dtype=f32) / n_freq)
    ang = jnp.concatenate([row[:, None] * inv, col[:, None] * inv], axis=-1)
    cos, sin = jnp.cos(ang), jnp.sin(ang)
    zero = jnp.zeros_like(sin)
    cos_h = jnp.concatenate([cos, cos], axis=-1)
    sina_h = jnp.concatenate([-sin, zero], axis=-1)
    sinb_h = jnp.concatenate([zero, sin], axis=-1)
    two = lambda t: jnp.concatenate([t, t], axis=-1)
    return two(cos_h), two(sina_h), two(sinb_h)


SSD_PACK = 32


def _split3(x):
    hi = x.astype(bf16)
    r1 = x - hi.astype(f32)
    mid = r1.astype(bf16)
    lo = (r1 - mid.astype(f32)).astype(bf16)
    return hi, mid, lo


def _pack3(x):
    hi, mid, lo = _split3(x)
    lane = lax.broadcasted_iota(jnp.int32, x.shape, 1)
    mid_r = pltpu.roll(mid.astype(f32), SSD_PACK, axis=1)
    lo_r = pltpu.roll(lo.astype(f32), 2 * SSD_PACK, axis=1)
    packed = jnp.where(lane < SSD_PACK, hi.astype(f32),
                       jnp.where(lane < 2 * SSD_PACK, mid_r,
                                 jnp.where(lane < 3 * SSD_PACK, lo_r, 0.0)))
    return packed.astype(bf16)


def _ssd_selectors():
    k = np.arange(LANES)
    src = np.where(k < 3 * SSD_PACK, k % SSD_PACK, -1)
    col_blk = np.arange(SSD_PACK * SSD_CHUNK) // SSD_CHUNK
    sel_bc = (src[:, None] == col_blk[None, :])
    head = np.arange(SSD_INNER) // SSD_HEAD_DIM
    sel_f = (src[:, None] == head[None, :])
    sel_b = (src[:, None] == (head + SSD_HEADS)[None, :])
    tri3 = np.tile(np.tril(np.ones((SSD_CHUNK, SSD_CHUNK))), (1, 3))
    rows = np.arange(SSD_CHUNK)[:, None]
    cols = np.arange(SSD_CHUNK + 32)[None, :]
    half = SSD_CONV // 2
    shift = np.concatenate([cols == rows + 16 + d for d in range(-half, half + 1) if d != 0], axis=0)
    as_bf = lambda a: jnp.asarray(a.astype(np.float32), dtype=bf16)
    return as_bf(sel_bc), as_bf(sel_f), as_bf(sel_b), as_bf(tri3), as_bf(shift)


def _ssd_kernel(seq_len, xbc_ref, dt_ref, z_ref, cw_ref, cb_ref, dtb_ref, alog_ref, dskip_ref, nw_ref,
                selbc_ref, self_ref, selb_ref, tri3_ref, shift_ref, h0_ref, out_ref, hfin_ref,
                xs_s, bc_s, dt_s, y_s, hf_s, hb_s, win_s):
    q = SSD_CHUNK
    nc = seq_len // q
    halo = 16
    H, P, N = SSD_HEADS, SSD_HEAD_DIM, SSD_STATE
    gw = (H // SSD_GROUPS) * P
    a2_row = -jnp.exp(alog_ref[...]) * math.log2(math.e)

    def conv_chunk(c):
        r0 = pl.multiple_of(c * q, q)
        pstart = pl.multiple_of(jnp.maximum(r0 - halo, 0), halo)
        nstart = pl.multiple_of(jnp.minimum(r0 + q, seq_len - halo), halo)
        zero = jnp.zeros((), bf16)
        win_s[0:halo, :] = jnp.where(c > 0, xbc_ref[pl.ds(pstart, halo), :], zero)
        win_s[halo:halo + q, :] = xbc_ref[pl.ds(r0, q), :]
        win_s[halo + q:, :] = jnp.where(c < nc - 1, xbc_ref[pl.ds(nstart, halo), :], zero)
        taps = [k for k in range(SSD_CONV) if k != SSD_CONV // 2]
        cw = 2 * LANES
        for j in range(SSD_XBC // cw):
            cs = slice(j * cw, (j + 1) * cw)
            sh = jnp.dot(shift_ref[...], win_s[:, cs], preferred_element_type=f32)
            acc = cb_ref[:, cs] + win_s[halo:halo + q, cs].astype(f32) * cw_ref[SSD_CONV // 2:SSD_CONV // 2 + 1, cs]
            for n, k in enumerate(taps):
                acc = acc + sh[n * q:(n + 1) * q, :] * cw_ref[k:k + 1, cs]
            act = _silu(acc)
            if j < SSD_INNER // cw:
                xs_s[pl.ds(r0, q), cs] = act
            else:
                bc_s[pl.ds(r0, q), j * cw - SSD_INNER:(j + 1) * cw - SSD_INNER] = act.astype(bf16)
        dt_s[pl.ds(r0, q), :] = jax.nn.softplus(dt_ref[pl.ds(r0, q), :] + dtb_ref[...])

    conv_chunk(0)
    hf_s[...] = h0_ref[0]
    hb_s[...] = h0_ref[1]

    ri = lax.broadcasted_iota(jnp.int32, (q, q), 0)
    ci = lax.broadcasted_iota(jnp.int32, (q, q), 1)
    lower = ri >= ci
    upper = ci >= ri
    lane = lax.broadcasted_iota(jnp.int32, (q, LANES), 1)
    lo_half = lane < P

    def cumsums(dt):
        dta = dt * a2_row
        cf = jnp.dot(tri3_ref[...], jnp.concatenate(_split3(dta), axis=0), preferred_element_type=f32)
        rb = cf[q - 1:q, :] - cf + dta
        return cf, rb

    def load_chunk(r0):
        dt = dt_s[pl.ds(r0, q), :]
        xs = xs_s[pl.ds(r0, q), :]
        bcv = bc_s[pl.ds(r0, q), :]
        bmat = [bcv[:, g * N:(g + 1) * N] for g in range(SSD_GROUPS)]
        cmat = [bcv[:, SSD_BC + g * N:SSD_BC + (g + 1) * N] for g in range(SSD_GROUPS)]
        return dt, xs, bmat, cmat

    def inter_chunk(h_s, sel_ref, decay, weight, xs, bmat, cmat, dec_idx):
        ew = jnp.dot(jnp.concatenate([_pack3(decay), _pack3(weight)], axis=0), sel_ref[...],
                     preferred_element_type=f32)
        e_x, w_x = ew[:q], ew[q:]
        hb_ = h_s[...].astype(bf16)
        yoff = jnp.concatenate(
            [jnp.dot(cmat[g], hb_[:, g * gw:(g + 1) * gw], preferred_element_type=f32)
             for g in range(SSD_GROUPS)], axis=1)
        xw = (xs * w_x).astype(bf16)
        dec_row = e_x[dec_idx:dec_idx + 1, :]
        for g in range(SSD_GROUPS):
            gs = slice(g * gw, (g + 1) * gw)
            bt = jnp.transpose(bmat[g].astype(f32)).astype(bf16)
            upd = jnp.dot(bt, xw[:, gs], preferred_element_type=f32)
            h_s[:, gs] = h_s[:, gs] * dec_row[:, gs] + upd
        return yoff * e_x

    def fwd_body(c, carry):
        r0 = pl.multiple_of(c * q, q)
        dt, xs, bmat, cmat = load_chunk(r0)
        cf, rb = cumsums(dt)
        pcol = jnp.where(lane < H, cf, rb)
        bcast = jnp.dot(_pack3(pcol), selbc_ref[...], preferred_element_type=f32)
        prow = jnp.transpose(pcol - jnp.log2(dt))
        cbm = [lax.dot_general(cmat[g], bmat[g], (((1,), (1,)), ((), ())), preferred_element_type=f32)
               for g in range(SSD_GROUPS)]
        xsb = xs.astype(bf16)
        ypairs = []
        for k in range(H // 2):
            res = []
            for h in (2 * k, 2 * k + 1):
                g = h // (H // SSD_GROUPS)
                hb_ = H + h
                segf = bcast[:, h * q:(h + 1) * q] - prow[h:h + 1, :]
                segb = bcast[:, hb_ * q:(hb_ + 1) * q] - prow[hb_:hb_ + 1, :]
                df = jnp.exp2(jnp.where(lower, segf, NEG_INF))
                db = jnp.exp2(jnp.where(upper, segb, NEG_INF))
                mh = (cbm[g] * (df + db)).astype(bf16)
                res.append(jnp.dot(mh, xsb[:, k * LANES:(k + 1) * LANES], preferred_element_type=f32))
            ypairs.append(jnp.where(lo_half, res[0], res[1]))
        y = jnp.concatenate(ypairs, axis=1)
        wfa = jnp.exp2(cf[q - 1:q, :] - cf) * dt
        y_s[pl.ds(r0, q), :] = y + inter_chunk(hf_s, self_ref, jnp.exp2(cf), wfa, xs, bmat, cmat, q - 1)
        conv_chunk(jnp.minimum(c + 1, nc - 1))
        return carry

    lax.fori_loop(0, nc, fwd_body, 0)

    def bwd_body(i, carry):
        c = nc - 1 - i
        r0 = pl.multiple_of(c * q, q)
        dt, xs, bmat, cmat = load_chunk(r0)
        _, rb = cumsums(dt)
        wba = jnp.exp2(rb[0:1, :] - rb) * dt
        y = y_s[pl.ds(r0, q), :] + inter_chunk(hb_s, selb_ref, jnp.exp2(rb), wba, xs, bmat, cmat, 0)
        yy = y + xs * dskip_ref[...]
        zz = z_ref[pl.ds(r0, q), :].astype(f32)
        gated = yy * _silu(zz)
        ms = jnp.mean(gated * gated, axis=-1, keepdims=True)
        out_ref[pl.ds(r0, q), :] = (gated * lax.rsqrt(ms + EPS) * nw_ref[...]).astype(out_ref.dtype)
        return carry

    lax.fori_loop(0, nc, bwd_body, 0)
    hfin_ref[0] = hf_s[...]
    hfin_ref[1] = hb_s[...]


def _ssd(xbc, dt, z, conv_w, conv_b, dtb, alog, dskip, nw, h0, bsz, seq_len):
    one = pl.Buffered(1)
    seq = lambda w: pl.BlockSpec((seq_len, w), lambda b: (b, 0), pipeline_mode=one)
    const = lambda r, w: pl.BlockSpec((r, w), lambda b: (0, 0))
    st = pl.BlockSpec((None, 2, SSD_STATE, SSD_INNER), lambda b: (b, 0, 0, 0))
    sels = _ssd_selectors()
    return pl.pallas_call(
        functools.partial(_ssd_kernel, seq_len),
        out_shape=[jax.ShapeDtypeStruct((bsz * seq_len, SSD_INNER), bf16),
                   jax.ShapeDtypeStruct((bsz, 2, SSD_STATE, SSD_INNER), f32)],
        grid=(bsz,),
        in_specs=[seq(SSD_XBC), seq(LANES), seq(SSD_INNER),
                  const(8, SSD_XBC), const(1, SSD_XBC), const(1, LANES), const(1, LANES),
                  const(1, SSD_INNER), const(1, SSD_INNER)]
                 + [const(*s.shape) for s in sels] + [st],
        out_specs=[seq(SSD_INNER), st],
        scratch_shapes=[pltpu.VMEM((seq_len, SSD_INNER), f32),
                        pltpu.VMEM((seq_len, 2 * SSD_BC), bf16),
                        pltpu.VMEM((seq_len, LANES), f32),
                        pltpu.VMEM((seq_len, SSD_INNER), f32),
                        pltpu.VMEM((SSD_STATE, SSD_INNER), f32),
                        pltpu.VMEM((SSD_STATE, SSD_INNER), f32),
                        pltpu.VMEM((SSD_CHUNK + 32, SSD_XBC), bf16)],
        compiler_params=_cparams(("parallel",)),
        name="ssd",
    )(xbc, dt, z, conv_w, conv_b, dtb, alog, dskip, nw, *sels, h0)


def _attn_kernel(n_blocks, local, q_ref, g_ref, k_ref, v_ref, kc_ref, vc_ref, sink_ref, o_ref, s_s):
    t = ATT_BLOCK
    i = pl.program_id(1)
    n_ctx = kc_ref.shape[0]
    rpk = ATT_HEADS // ATT_KV_HEADS
    lane = lax.broadcasted_iota(jnp.int32, (t, LANES), 1)
    lo_half = lane < ATT_HEAD_DIM
    qv = q_ref[...]
    gv = g_ref[...].astype(f32)
    if local:
        p0 = pl.multiple_of(jnp.maximum(i - 1, 0) * t, t)
        c0 = pl.multiple_of(i * t, t)
        n0 = pl.multiple_of(jnp.minimum(i + 1, n_blocks - 1) * t, t)
        ql = lax.broadcasted_iota(jnp.int32, (t, t), 0)
        kl = lax.broadcasted_iota(jnp.int32, (t, t), 1)
        bias_prev = jnp.where((kl >= ql) & (i > 0), 0.0, NEG_INF)
        bias_next = jnp.where((kl <= ql) & (i < n_blocks - 1), 0.0, NEG_INF)
        bias_prev = jnp.concatenate([bias_prev] * rpk, axis=0)
        bias_next = jnp.concatenate([bias_next] * rpk, axis=0)
    zero_b = jnp.zeros((), bf16)
    for j in range(ATT_KV_HEADS):
        ls = slice(j * LANES, (j + 1) * LANES)
        if local:
            kk = jnp.concatenate([k_ref[pl.ds(p0, t), ls], k_ref[pl.ds(c0, t), ls],
                                  k_ref[pl.ds(n0, t), ls], kc_ref[:, ls]], axis=0)
        else:
            kk = kc_ref[:, ls]
        pieces = []
        for r in range(rpk):
            hq = j * rpk + r
            qp = qv[:, (hq // 2) * LANES:(hq // 2 + 1) * LANES]
            keep = lo_half if hq % 2 == 0 else jnp.logical_not(lo_half)
            pieces.append(jnp.where(keep, qp, zero_b))
        q4 = jnp.concatenate(pieces, axis=0)
        s = lax.dot_general(q4, kk, (((1,), (1,)), ((), ())), preferred_element_type=f32)
        if local:
            s = jnp.concatenate([s[:, :t] + bias_prev, s[:, t:2 * t], s[:, 2 * t:3 * t] + bias_next,
                                 s[:, 3 * t:]], axis=1)
        s_s[j] = s
    outs = []
    for j in range(ATT_KV_HEADS):
        ls = slice(j * LANES, (j + 1) * LANES)
        if local:
            vv = jnp.concatenate([v_ref[pl.ds(p0, t), ls], v_ref[pl.ds(c0, t), ls],
                                  v_ref[pl.ds(n0, t), ls], vc_ref[:, ls]], axis=0)
        else:
            vv = vc_ref[:, ls]
        sk = jnp.concatenate([jnp.full((t, 1), sink_ref[j * rpk + r] * LOG2E, f32) for r in range(rpk)],
                             axis=0)
        s = s_s[j]
        m = jnp.maximum(jnp.max(s, axis=1, keepdims=True), sk)
        p = jnp.exp2(s - m)
        den = jnp.sum(p, axis=1, keepdims=True) + jnp.exp2(sk - m)
        o4 = jnp.dot(p.astype(bf16), vv, preferred_element_type=f32) / den
        outs.append(jnp.where(lo_half, o4[0:t], o4[t:2 * t]))
        outs.append(jnp.where(lo_half, o4[2 * t:3 * t], o4[3 * t:4 * t]))
    o = jnp.concatenate(outs, axis=1)
    o_ref[...] = (o * _silu(gv)).astype(o_ref.dtype)


def _attention(q, g, k, v, kc, vc, sink, bsz, seq_len, local):
    t = ATT_BLOCK
    nb = seq_len // t
    n_ctx = kc.shape[1]
    kw = ATT_KV_HEADS * LANES
    blk = pl.BlockSpec((t, ATT_Q), lambda b, i: (b * nb + i, 0))
    full = lambda n: pl.BlockSpec((None, n, kw), lambda b, i: (b, 0, 0))
    return pl.pallas_call(
        functools.partial(_attn_kernel, nb, local),
        out_shape=jax.ShapeDtypeStruct((bsz * seq_len, ATT_Q), bf16),
        grid=(bsz, nb),
        in_specs=[blk, blk, full(k.shape[1]), full(v.shape[1]), full(n_ctx), full(n_ctx),
                  pl.BlockSpec(memory_space=pltpu.SMEM)],
        out_specs=blk,
        scratch_shapes=[pltpu.VMEM((ATT_KV_HEADS, (ATT_HEADS // ATT_KV_HEADS) * t,
                                    (3 * t if local else 0) + n_ctx), f32)],
        compiler_params=_cparams(("parallel", "arbitrary")),
        name="attention",
    )(q, g, k, v, kc, vc, sink)


def _outproj_kernel(a_ref, b_ref, w_ref, x_ref, gate_ref, o_ref):
    ka = a_ref.shape[1]
    acc = jnp.dot(a_ref[...], w_ref[:ka, :], preferred_element_type=f32)
    acc = acc + jnp.dot(b_ref[...], w_ref[ka:, :], preferred_element_type=f32)
    o_ref[...] = x_ref[...] + gate_ref[...] * acc


def _outproj(a, b, w, x, gate, rows_per_mod):
    m, d = x.shape
    tm = min(ROW_TILE, rows_per_mod)
    per = rows_per_mod // tm
    nmod = gate.shape[0]
    mod_idx = (lambda i: (i // per, 0, 0)) if nmod > 1 else (lambda i: (0, 0, 0))
    row = lambda w_: pl.BlockSpec((tm, w_), lambda i: (i, 0))
    return pl.pallas_call(
        _outproj_kernel,
        out_shape=jax.ShapeDtypeStruct((m, d), f32),
        grid=(m // tm,),
        in_specs=[row(a.shape[1]), row(b.shape[1]),
                  pl.BlockSpec(w.shape, lambda i: (0, 0)), row(d),
                  pl.BlockSpec((None, 1, d), mod_idx)],
        out_specs=row(d),
        compiler_params=_cparams(("parallel",)),
        name="outproj",
    )(a, b, w, x, gate)


def _s5_disc_kernel(lre_ref, lim_ref, ls_ref, bre_ref, bim_ref, abre_ref, abim_ref, bbre_ref, bbim_ref):
    lam_re = lre_ref[...]
    lam_im = lim_ref[...]
    dt = jnp.exp(ls_ref[...])
    mag = jnp.exp(lam_re * dt)
    ab_re = mag * jnp.cos(lam_im * dt)
    ab_im = mag * jnp.sin(lam_im * dt)
    num_re, num_im = ab_re - 1.0, ab_im
    den = lam_re * lam_re + lam_im * lam_im
    coef_re = (num_re * lam_re + num_im * lam_im) / den
    coef_im = (num_im * lam_re - num_re * lam_im) / den
    b_re, b_im = bre_ref[...], bim_ref[...]
    abre_ref[...] = ab_re
    abim_ref[...] = ab_im
    bbre_ref[...] = coef_re * b_re - coef_im * b_im
    bbim_ref[...] = coef_re * b_im + coef_im * b_re


def _s5_discretise(lam_re, lam_im, log_step, b_re, b_im):
    g, n, cg = b_re.shape
    exp = lambda t: jnp.repeat(t.reshape(2 * g, n), cg, axis=1)
    ls = jnp.broadcast_to(log_step.reshape(2 * g, 1), (2 * g, n * cg))
    bb = lambda t: jnp.tile(t.reshape(g, n * cg), (2, 1))
    shp = jax.ShapeDtypeStruct((2 * g, n * cg), f32)
    ab_re, ab_im, bb_re, bb_im = pl.pallas_call(
        _s5_disc_kernel, out_shape=[shp] * 4, name="s5_disc",
    )(exp(lam_re), exp(lam_im), ls, bb(b_re), bb(b_im))
    first = lambda t: t.reshape(2, g, n, cg)[..., 0]
    full = lambda t: t.reshape(2, g, n, cg)
    return first(ab_re), first(ab_im), full(bb_re), full(bb_im)


S5_TC = 8
S5_PAD = 8


def _cmul(ar, ai, br, bi):
    return ar * br - ai * bi, ar * bi + ai * br


def _s5_kernel(n_lat, n_ctx, *refs):
    tc = S5_TC
    xl, xc = refs[:tc], refs[tc:2 * tc]
    arow_ref, acol_ref, bbd_ref, cbd_ref, y_ref, wyz_s, ws_s, sl_s, sc_s = refs[2 * tc:]
    sw = S5_GBLK * S5_STATE
    nsl = sw // LANES
    nb = 8

    def powers(re, im, n):
        out = [(jnp.ones_like(re), jnp.zeros_like(im))]
        for _ in range(n):
            out.append(_cmul(out[-1][0], out[-1][1], re, im))
        return out

    prow = [powers(arow_ref[2 * d:2 * d + 1, :], arow_ref[2 * d + 1:2 * d + 2, :], tc) for d in range(2)]

    for s in range(tc):
        for d, k in ((0, tc - 1 - s), (1, s)):
            wr, wi = _cmul(bbd_ref[2 * d], bbd_ref[2 * d + 1], *prow[d][k])
            ws_s[s * LANES:(s + 1) * LANES, d * 2 * sw:d * 2 * sw + sw] = wr.astype(bf16)
            ws_s[s * LANES:(s + 1) * LANES, d * 2 * sw + sw:(d + 1) * 2 * sw] = wi.astype(bf16)
    crhs = jnp.concatenate([cbd_ref[0], -cbd_ref[1]], axis=0).astype(bf16)
    kall = [jnp.dot(ws_s[:, d * 2 * sw:(d + 1) * 2 * sw], crhs, preferred_element_type=f32) for d in range(2)]
    kf = [kall[0][(tc - 1 - k) * LANES:(tc - k) * LANES] for k in range(tc)]
    kb = [kall[1][k * LANES:(k + 1) * LANES] for k in range(tc)]
    for s in range(tc):
        for t in range(tc):
            blk = kf[t - s] if t > s else (kb[s - t] if t < s else kf[0] + kb[0])
            wyz_s[s * LANES:(s + 1) * LANES, t * LANES:(t + 1) * LANES] = blk.astype(bf16)
    base = tc * LANES
    for d in range(2):
        a1 = (acol_ref[2 * d], acol_ref[2 * d + 1])
        ak = a1
        for k in range(1, tc + 1):
            t = k - 1 if d == 0 else tc - k
            dre, dim_ = _cmul(cbd_ref[0], cbd_ref[1], *ak)
            r0 = base + d * 2 * sw
            wyz_s[r0:r0 + sw, t * LANES:(t + 1) * LANES] = dre.astype(bf16)
            wyz_s[r0 + sw:r0 + 2 * sw, t * LANES:(t + 1) * LANES] = (-dim_).astype(bf16)
            if k < tc:
                ak = _cmul(*ak, *a1)

    def rows_of(x_refs, b, n):
        return jnp.concatenate([r[b * n:(b + 1) * n, :] for r in x_refs], axis=1)

    def inject(x_refs, s_ref, n):
        for b in range(nb):
            sb = jnp.dot(rows_of(x_refs, b, n), ws_s[...], preferred_element_type=f32)
            for k in range(4 * nsl):
                s_ref[k, b * (n + S5_PAD):b * (n + S5_PAD) + n, :] = sb[:, k * LANES:(k + 1) * LANES]

    at = [[tuple(jnp.broadcast_to(p[:, k * LANES:(k + 1) * LANES], (nb, LANES)) for p in prow[d][tc])
           for k in range(nsl)] for d in range(2)]

    def scan(s_ref, n, init):
        def step(i, carry):
            new = []
            for d in range(2):
                idx = pl.ds(i if d == 0 else n - 1 - i, nb, stride=n + S5_PAD)
                for k in range(nsl):
                    hr, hi = carry[2 * (d * nsl + k)], carry[2 * (d * nsl + k) + 1]
                    kr, ki = d * 2 * nsl + k, d * 2 * nsl + nsl + k
                    sr, si = s_ref[kr, idx, :], s_ref[ki, idx, :]
                    s_ref[kr, idx, :] = hr
                    s_ref[ki, idx, :] = hi
                    ar, ai = at[d][k]
                    new += [ar * hr - ai * hi + sr, ar * hi + ai * hr + si]
            return tuple(new)
        return lax.fori_loop(0, n, step, init, unroll=2)

    inject(xc, sc_s, n_ctx)
    h_ctx = scan(sc_s, n_ctx, tuple(jnp.zeros((nb, LANES), f32) for _ in range(4 * nsl)))
    inject(xl, sl_s, n_lat)
    scan(sl_s, n_lat, h_ctx)

    for b in range(nb):
        r0 = b * (n_lat + S5_PAD)
        hin = jnp.concatenate([sl_s[k, r0:r0 + n_lat, :] for k in range(4 * nsl)], axis=1).astype(bf16)
        yb = jnp.dot(jnp.concatenate([rows_of(xl, b, n_lat), hin], axis=1), wyz_s[...],
                     preferred_element_type=f32)
        for t in range(tc):
            y_ref[pl.ds(b * n_lat * tc + t, n_lat, stride=tc), :] = yb[:, t * LANES:(t + 1) * LANES]


def _s5_mix(u, u_c, arow, acol, bbd, cbd, bsz, seq_len, n_ctx_tok):
    assert bsz == 8, "the chunk recurrence puts the batch on the 8 sublanes"
    tc = S5_TC
    n_lat, n_ctx = seq_len // tc, n_ctx_tok // tc
    nblk = S5_GROUPS // S5_GBLK
    sw = S5_GBLK * S5_STATE
    nsl = sw // LANES
    one = pl.Buffered(1)
    ul = u.reshape(bsz * n_lat, tc * S5_WIDTH)
    uc = u_c.reshape(bsz * n_ctx, tc * S5_WIDTH)
    xspec = lambda rows, s: pl.BlockSpec((rows, LANES), lambda g: (0, s * nblk + g))
    par = lambda *shape: pl.BlockSpec((None,) + shape, lambda g: (g,) + (0,) * len(shape), pipeline_mode=one)
    return pl.pallas_call(
        functools.partial(_s5_kernel, n_lat, n_ctx),
        out_shape=jax.ShapeDtypeStruct((bsz * seq_len, S5_WIDTH), f32),
        grid=(nblk,),
        in_specs=[xspec(bsz * n_lat, s) for s in range(tc)] + [xspec(bsz * n_ctx, s) for s in range(tc)]
                 + [par(4, sw), par(4, sw, LANES), par(4, LANES, sw), par(2, sw, LANES)],
        out_specs=pl.BlockSpec((bsz * seq_len, LANES), lambda g: (0, g), pipeline_mode=one),
        scratch_shapes=[pltpu.VMEM((tc * LANES + 4 * sw, tc * LANES), bf16),
                        pltpu.VMEM((tc * LANES, 4 * sw), bf16),
                        pltpu.VMEM((4 * nsl, bsz * (n_lat + S5_PAD), LANES), f32),
                        pltpu.VMEM((4 * nsl, bsz * (n_ctx + S5_PAD), LANES), f32)],
        compiler_params=_cparams(("arbitrary",)),
        name="s5_mix",
    )(*([ul] * tc), *([uc] * tc), arow, acol, bbd, cbd)


def _s5_block_params(ab_re, ab_im, bb_re, bb_im, c_re, c_im):
    g, n, cg = S5_GROUPS, S5_STATE, S5_GROUP_CH
    nblk = g // S5_GBLK
    sw = S5_GBLK * n
    eye = jnp.eye(S5_GBLK, dtype=f32)
    arow = jnp.stack([t[d].reshape(nblk, sw) for d in range(2) for t in (ab_re, ab_im)], axis=1)
    acol = jnp.broadcast_to(arow[..., None], (nblk, 4, sw, LANES))

    def bd(t):
        t = t.reshape(nblk, S5_GBLK, n, cg)
        return jnp.einsum('jgnc,gh->jgchn', t, eye).reshape(nblk, S5_GBLK * cg, sw)

    def bdc(t):
        t = t.reshape(nblk, S5_GBLK, cg, n)
        return jnp.einsum('jgcn,gh->jgnhc', t, eye).reshape(nblk, sw, S5_GBLK * cg)

    bbd = jnp.stack([bd(t[d]) for d in range(2) for t in (bb_re, bb_im)], axis=1)
    cbd = jnp.stack([bdc(c_re), bdc(c_im)], axis=1)
    return arow, acol, bbd, cbd


def _s5_out_kernel(y_ref, u_ref, g_ref, x_ref, dskip_ref, gw_ref, gb_ref, w_ref, gate_ref,
                   fw_ref, o_ref):
    y = y_ref[...] + dskip_ref[...] * u_ref[...].astype(f32)
    y = jax.nn.gelu(y)
    glu = jnp.dot(y.astype(bf16), gw_ref[...], preferred_element_type=f32) + gb_ref[...]
    y = y * jax.nn.sigmoid(glu)
    y = y * _silu(g_ref[...].astype(f32))
    x = x_ref[...] + gate_ref[...] * jnp.dot(y.astype(bf16), w_ref[...], preferred_element_type=f32)
    ms = jnp.mean(x * x, axis=-1, keepdims=True)
    o_ref[...] = x * lax.rsqrt(ms + EPS) * fw_ref[...]


def _s5_out(y, u, g, x, dskip, gw, gb, w, gate, fw, rows_per_mod):
    m, d = x.shape
    tm = min(ROW_TILE, rows_per_mod)
    per = rows_per_mod // tm
    row = lambda: pl.BlockSpec((tm, d), lambda i: (i, 0))
    vec = lambda: pl.BlockSpec((1, d), lambda i: (0, 0))
    mat = lambda: pl.BlockSpec((d, d), lambda i: (0, 0))
    return pl.pallas_call(
        _s5_out_kernel,
        out_shape=jax.ShapeDtypeStruct((m, d), f32),
        grid=(m // tm,),
        in_specs=[row(), row(), row(), row(), vec(), mat(), vec(), mat(),
                  pl.BlockSpec((None, 1, d), lambda i: (i // per, 0, 0)), vec()],
        out_specs=row(),
        compiler_params=_cparams(("parallel",)),
        name="s5_out",
    )(y, u, g, x, dskip, gw, gb, w, gate, fw)


def _even_weights(w_in):
    o = 0
    z = w_in[:, o:o + SSD_INNER]; o += SSD_INNER
    xbc = w_in[:, o:o + SSD_XBC]; o += SSD_XBC
    dt = w_in[:, o:o + 2 * SSD_HEADS]; o += 2 * SSD_HEADS
    q = w_in[:, o:o + ATT_Q]; o += ATT_Q
    k = w_in[:, o:o + ATT_KVW]; o += ATT_KVW
    v = w_in[:, o:o + ATT_KVW]; o += ATT_KVW
    g = w_in[:, o:o + ATT_Q]
    d = w_in.shape[0]
    dup = lambda t: jnp.concatenate([t.reshape(d, ATT_KV_HEADS, 1, ATT_HEAD_DIM)] * 2, axis=2).reshape(d, -1)
    dtp = jnp.pad(dt, ((0, 0), (0, LANES - 2 * SSD_HEADS)))
    return jnp.concatenate([z, xbc, q, dup(k), dup(v), g, dtp], axis=1).astype(bf16)


def _even_segs(rope):
    scale = ATT_HEAD_DIM ** -0.5 * LOG2E
    widths = [(SSD_INNER, None, bf16), (SSD_XBC, None, bf16),
              (ATT_Q, scale, bf16),
              (ATT_KV_HEADS * LANES, 1.0 if rope else None, bf16),
              (ATT_KV_HEADS * LANES, None, bf16), (ATT_Q, None, bf16), (LANES, None, f32)]
    segs, o = [], 0
    for w, r, dtp in widths:
        segs.append((o, w, r, dtp))
        o += w
    return segs


def _pad_lanes(v, n=LANES):
    v = v.reshape(1, -1)
    return jnp.pad(v, ((0, 0), (0, n - v.shape[1])))


def kernel(x, c, ctx, c_ctx, e_norm_w, e_ada_w, e_ada_b, e_w_in, e_conv_w, e_conv_b, e_dt_bias,
           e_a_log, e_d_skip, e_ssd_norm_w, e_sink, e_w_out, o_norm_w, o_ada_w, o_ada_b, o_w_in,
           o_lam_re, o_lam_im, o_log_step, o_b_re, o_b_im, o_c_re, o_c_im, o_d_skip, o_glu_w,
           o_glu_b, o_w_out, final_norm_w):
    bsz, seq_len, d = x.shape
    n_ctx = ctx.shape[1]
    xf = x.reshape(bsz * seq_len, d)
    xcf = ctx.reshape(bsz * n_ctx, d)

    cvecs = jnp.concatenate([c, c_ctx[None, :], jnp.zeros((16 - bsz - 1, d), f32)], axis=0)

    def modulation(ada_w, ada_b):
        mod = _adaln(cvecs, ada_w.astype(bf16), ada_b.reshape(1, -1))
        parts = [mod[:, k * d:(k + 1) * d] for k in range(3)]
        lat = [p[:bsz].reshape(bsz, 1, d) for p in parts]
        cx = [p[bsz:bsz + 1].reshape(1, 1, d) for p in parts]
        return lat, cx

    (shift, scale, gate), (shift_c, scale_c, gate_c) = modulation(e_ada_w[0], e_ada_b[0])
    w_in = _even_weights(e_w_in[0])
    nw = e_norm_w[0].reshape(1, d)
    tabs = _rope_tables(seq_len)
    z, xbc, q, k, v, g, dt = _inproj(xf, shift, scale, nw, w_in, _even_segs(True), seq_len, tabs)
    z_c, xbc_c, q_c, k_c, v_c, g_c, dt_c = _inproj(xcf, shift_c, scale_c, nw, w_in, _even_segs(False), n_ctx)

    conv_w = jnp.pad(e_conv_w[0], ((0, 8 - SSD_CONV), (0, 0)))
    conv_b = e_conv_b[0].reshape(1, -1)
    dtb = _pad_lanes(e_dt_bias[0])
    alog = _pad_lanes(e_a_log[0])
    dskip = jnp.repeat(e_d_skip[0], SSD_HEAD_DIM).reshape(1, -1)
    snw = e_ssd_norm_w[0].reshape(1, -1)
    h0 = jnp.zeros((bsz, 2, SSD_STATE, SSD_INNER), f32)
    ssd_c, hfin = _ssd(xbc_c, dt_c, z_c, conv_w, conv_b, dtb, alog, dskip, snw, h0, bsz, n_ctx)
    ssd_o, _ = _ssd(xbc, dt, z, conv_w, conv_b, dtb, alog, dskip, snw, hfin, bsz, seq_len)

    kw = ATT_KV_HEADS * LANES
    k3, v3 = k.reshape(bsz, seq_len, kw), v.reshape(bsz, seq_len, kw)
    kc3, vc3 = k_c.reshape(bsz, n_ctx, kw), v_c.reshape(bsz, n_ctx, kw)
    sink = e_sink[0]
    att = _attention(q, g, k3, v3, kc3, vc3, sink, bsz, seq_len, True)
    att_c = _attention(q_c, g_c, kc3, vc3, kc3, vc3, sink, bsz, n_ctx, False)
    w_out = e_w_out[0].astype(bf16)
    x1 = _outproj(ssd_o, att, w_out, xf, gate, seq_len)
    xc1 = _outproj(ssd_c, att_c, w_out, xcf, gate_c, n_ctx)

    (shift, scale, gate), (shift_c, scale_c, _) = modulation(o_ada_w[0], o_ada_b[0])
    w_in = o_w_in[0].astype(bf16)
    nw = o_norm_w[0].reshape(1, d)
    u, g2 = _inproj(x1, shift, scale, nw, w_in,
                    [(0, S5_WIDTH, None, bf16), (S5_WIDTH, S5_WIDTH, None, bf16)], seq_len)
    (u_c,) = _inproj(xc1, shift_c, scale_c, nw, w_in[:, :S5_WIDTH], [(0, S5_WIDTH, None, bf16)], n_ctx)

    ab_re, ab_im, bb_re, bb_im = _s5_discretise(o_lam_re[0], o_lam_im[0], o_log_step[0], o_b_re[0], o_b_im[0])
    arow, acol, bbd, cbd = _s5_block_params(ab_re, ab_im, bb_re, bb_im, o_c_re[0], o_c_im[0])
    y = _s5_mix(u, u_c, arow, acol, bbd, cbd, bsz, seq_len, n_ctx)
    out = _s5_out(y, u, g2, x1, o_d_skip[0].reshape(1, -1),
                  o_glu_w[0].astype(bf16), o_glu_b[0].reshape(1, -1), o_w_out[0].astype(bf16), gate,
                  final_norm_w.reshape(1, -1), seq_len)
    return out.reshape(bsz, seq_len, d)
```

```python
import functools
import math

import jax
import jax.numpy as jnp
import numpy as np
from jax import lax
from jax.experimental import pallas as pl
from jax.experimental.pallas import tpu as pltpu

f32 = jnp.float32
bf16 = jnp.bfloat16

D_MODEL = 1024
GRID_W = 64
EPS = 1e-6
NEG_INF = -1e30

SSD_HEADS = 16
SSD_HEAD_DIM = 64
SSD_GROUPS = 2
SSD_STATE = 128
SSD_CONV = 5
SSD_CHUNK = 128
SSD_INNER = SSD_HEADS * SSD_HEAD_DIM
SSD_BC = SSD_GROUPS * SSD_STATE
SSD_XBC = SSD_INNER + 2 * SSD_BC
ATT_HEADS = 16
ATT_KV_HEADS = 4
ATT_HEAD_DIM = 64
ATT_BLOCK = 128
ROPE_THETA = 10000.0
ATT_Q = ATT_HEADS * ATT_HEAD_DIM
ATT_KVW = ATT_KV_HEADS * ATT_HEAD_DIM
S5_WIDTH = 1024
S5_GROUP_CH = 16
S5_GROUPS = S5_WIDTH // S5_GROUP_CH
S5_STATE = 64

LOG2E = math.log2(math.e)
LANES = 128
ROW_TILE = 512
S5_GBLK = 8
S5_TC = 8
VMEM_LIMIT = 56 * 1024 * 1024


def _cparams(sem, flags=None):
    return pltpu.CompilerParams(dimension_semantics=sem, vmem_limit_bytes=VMEM_LIMIT, flags=flags)


def _silu(x):
    h = 0.5 * x
    return h + h * jnp.tanh(h)


def _adaln_kernel(c_ref, w_ref, b_ref, o_ref):
    c = c_ref[...]
    s = _silu(c).astype(bf16)
    o_ref[...] = jnp.dot(s, w_ref[...], preferred_element_type=f32) + b_ref[...]


def _adaln(cvecs, w, b):
    r, d = cvecs.shape
    n = w.shape[1]
    tn = 1024
    return pl.pallas_call(
        _adaln_kernel,
        out_shape=jax.ShapeDtypeStruct((r, n), f32),
        grid=(n // tn,),
        in_specs=[pl.BlockSpec((r, d), lambda j: (0, 0)),
                  pl.BlockSpec((d, tn), lambda j: (0, j)),
                  pl.BlockSpec((1, tn), lambda j: (0, j))],
        out_specs=pl.BlockSpec((r, tn), lambda j: (0, j)),
        compiler_params=_cparams(("arbitrary",)),
        name="adaln",
    )(cvecs, w, b)


def _inproj_kernel(segs, has_rope, x_ref, shift_ref, scale_ref, nw_ref, w_ref, *rest):
    if has_rope:
        cos_ref, sina_ref, sinb_ref = rest[:3]
        rest = rest[3:]
    outs = rest[:len(segs)]
    slab_s = rest[len(segs)] if len(rest) > len(segs) else None
    x = x_ref[...]
    ms = jnp.mean(x * x, axis=-1, keepdims=True)
    h = (x * lax.rsqrt(ms + EPS)) * nw_ref[...]
    h = h * (1.0 + scale_ref[...]) + shift_ref[...]
    hb = h.astype(bf16)
    products = {}
    for (start, width, rope, _, chunked), o_ref in zip(segs, outs):
        if (start, width) not in products:
            products[(start, width)] = jnp.dot(hb, w_ref[:, start:start + width], preferred_element_type=f32)
        acc = products[(start, width)]
        if chunked:
            tm = acc.shape[0]
            for j in range(width // LANES):
                slab_s[j] = acc[:, j * LANES:(j + 1) * LANES]
            for s in range(S5_TC):
                for j in range(width // LANES):
                    o_ref[:, s * width + j * LANES:s * width + (j + 1) * LANES] = (
                        slab_s[j, pl.ds(s, tm // S5_TC, stride=S5_TC), :].astype(o_ref.dtype))
            continue
        if rope is not None and not has_rope:
            acc = acc * rope
        elif rope is not None:
            rep = width // LANES
            cos = jnp.concatenate([cos_ref[...]] * rep, axis=1) * rope
            sina = jnp.concatenate([sina_ref[...]] * rep, axis=1) * rope
            sinb = jnp.concatenate([sinb_ref[...]] * rep, axis=1) * rope
            half = ATT_HEAD_DIM // 2
            up = pltpu.roll(acc, width - half, axis=1)
            dn = pltpu.roll(acc, half, axis=1)
            acc = acc * cos + up * sina + dn * sinb
        o_ref[...] = acc.astype(o_ref.dtype)


def _inproj(x, shift, scale, nw, w, segs, rows_per_mod, rope_tabs=None):
    m, d = x.shape
    tm = min(ROW_TILE, rows_per_mod)
    n = w.shape[1]
    per = rows_per_mod // tm
    nmod = shift.shape[0]
    mod_idx = (lambda i: (i // per, 0, 0)) if nmod > 1 else (lambda i: (0, 0, 0))
    in_specs = [pl.BlockSpec((tm, d), lambda i: (i, 0)),
                pl.BlockSpec((None, 1, d), mod_idx),
                pl.BlockSpec((None, 1, d), mod_idx),
                pl.BlockSpec((1, d), lambda i: (0, 0)),
                pl.BlockSpec((d, n), lambda i: (0, 0))]
    args = [x, shift, scale, nw, w]
    if rope_tabs is not None:
        for t in rope_tabs:
            in_specs.append(pl.BlockSpec((tm, LANES), lambda i: (i % per, 0)))
            args.append(t)
    fold = lambda sg: S5_TC if sg[4] else 1
    out_shape = [jax.ShapeDtypeStruct((m // fold(sg), sg[1] * fold(sg)), sg[3]) for sg in segs]
    out_specs = [pl.BlockSpec((tm // fold(sg), sg[1] * fold(sg)), lambda i: (i, 0)) for sg in segs]
    chunked_w = [sg[1] for sg in segs if sg[4]]
    scratch = [pltpu.VMEM((max(chunked_w) // LANES, tm, LANES), f32)] if chunked_w else []
    return pl.pallas_call(
        functools.partial(_inproj_kernel, tuple(segs), rope_tabs is not None),
        out_shape=out_shape,
        grid=(m // tm,),
        in_specs=in_specs,
        out_specs=out_specs,
        scratch_shapes=scratch,
        compiler_params=_cparams(("parallel",)),
        name="inproj",
    )(*args)


def _rope_tables(seq_len):
    rows = seq_len // GRID_W
    row = jnp.repeat(jnp.arange(rows, dtype=f32), GRID_W)
    col = jnp.tile(jnp.arange(GRID_W, dtype=f32), rows)
    n_freq = ATT_HEAD_DIM // 4
    inv = ROPE_THETA ** (-jnp.arange(n_freq, dtype=f32) / n_freq)
    ang = jnp.concatenate([row[:, None] * inv, col[:, None] * inv], axis=-1)
    cos, sin = jnp.cos(ang), jnp.sin(ang)
    zero = jnp.zeros_like(sin)
    cos_h = jnp.concatenate([cos, cos], axis=-1)
    sina_h = jnp.concatenate([-sin, zero], axis=-1)
    sinb_h = jnp.concatenate([zero, sin], axis=-1)
    two = lambda t: jnp.concatenate([t, t], axis=-1)
    return two(cos_h), two(sina_h), two(sinb_h)


SSD_PACK = 32


def _split3(x):
    hi = x.astype(bf16)
    r1 = x - hi.astype(f32)
    mid = r1.astype(bf16)
    lo = (r1 - mid.astype(f32)).astype(bf16)
    return hi, mid, lo


def _pack3(x):
    hi, mid, lo = _split3(x)
    lane = lax.broadcasted_iota(jnp.int32, x.shape, 1)
    mid_r = pltpu.roll(mid.astype(f32), SSD_PACK, axis=1)
    lo_r = pltpu.roll(lo.astype(f32), 2 * SSD_PACK, axis=1)
    packed = jnp.where(lane < SSD_PACK, hi.astype(f32),
                       jnp.where(lane < 2 * SSD_PACK, mid_r,
                                 jnp.where(lane < 3 * SSD_PACK, lo_r, 0.0)))
    return packed.astype(bf16)


def _ssd_selectors():
    k = np.arange(LANES)
    src = np.where(k < 3 * SSD_PACK, k % SSD_PACK, -1)
    col_blk = np.arange(SSD_PACK * SSD_CHUNK) // SSD_CHUNK
    sel_bc = (src[:, None] == col_blk[None, :])
    head = np.arange(SSD_INNER) // SSD_HEAD_DIM
    sel_f = (src[:, None] == head[None, :])
    sel_b = (src[:, None] == (head + SSD_HEADS)[None, :])
    tri3 = np.tile(np.tril(np.ones((SSD_CHUNK, SSD_CHUNK))), (1, 3))
    rows = np.arange(SSD_CHUNK)[:, None]
    cols = np.arange(SSD_CHUNK + 32)[None, :]
    half = SSD_CONV // 2
    shift = np.concatenate([cols == rows + 16 + d for d in range(-half, half + 1) if d != 0], axis=0)
    as_bf = lambda a: jnp.asarray(a.astype(np.float32), dtype=bf16)
    return as_bf(sel_bc), as_bf(sel_f), as_bf(sel_b), as_bf(tri3), as_bf(shift)


def _ssd_kernel(seq_len, xbc_ref, dt_ref, z_ref, cw_ref, cb_ref, dtb_ref, alog_ref, dskip_ref, nw_ref,
                selbc_ref, self_ref, selb_ref, tri3_ref, shift_ref, h0_ref, out_ref, hfin_ref,
                xs_s, bc_s, dt_s, y_s, hf_s, hb_s, win_s):
    q = SSD_CHUNK
    nc = seq_len // q
    halo = 16
    H, P, N = SSD_HEADS, SSD_HEAD_DIM, SSD_STATE
    gw = (H // SSD_GROUPS) * P
    a2_row = -jnp.exp(alog_ref[...]) * math.log2(math.e)

    def conv_chunk(c):
        r0 = pl.multiple_of(c * q, q)
        pstart = pl.multiple_of(jnp.maximum(r0 - halo, 0), halo)
        nstart = pl.multiple_of(jnp.minimum(r0 + q, seq_len - halo), halo)
        zero = jnp.zeros((), bf16)
        win_s[0:halo, :] = jnp.where(c > 0, xbc_ref[pl.ds(pstart, halo), :], zero)
        win_s[halo:halo + q, :] = xbc_ref[pl.ds(r0, q), :]
        win_s[halo + q:, :] = jnp.where(c < nc - 1, xbc_ref[pl.ds(nstart, halo), :], zero)
        taps = [k for k in range(SSD_CONV) if k != SSD_CONV // 2]
        cw = 2 * LANES
        for j in range(SSD_XBC // cw):
            cs = slice(j * cw, (j + 1) * cw)
            sh = jnp.dot(shift_ref[...], win_s[:, cs], preferred_element_type=f32)
            acc = cb_ref[:, cs] + win_s[halo:halo + q, cs].astype(f32) * cw_ref[SSD_CONV // 2:SSD_CONV // 2 + 1, cs]
            for n, k in enumerate(taps):
                acc = acc + sh[n * q:(n + 1) * q, :] * cw_ref[k:k + 1, cs]
            act = _silu(acc)
            if j < SSD_INNER // cw:
                xs_s[pl.ds(r0, q), cs] = act
            else:
                bc_s[pl.ds(r0, q), j * cw - SSD_INNER:(j + 1) * cw - SSD_INNER] = act.astype(bf16)
        dt_s[pl.ds(r0, q), :] = jax.nn.softplus(dt_ref[pl.ds(r0, q), :] + dtb_ref[...])

    conv_chunk(0)
    hf_s[...] = h0_ref[0]
    hb_s[...] = h0_ref[1]

    ri = lax.broadcasted_iota(jnp.int32, (q, q), 0)
    ci = lax.broadcasted_iota(jnp.int32, (q, q), 1)
    lower = ri >= ci
    upper = ci >= ri
    lane = lax.broadcasted_iota(jnp.int32, (q, LANES), 1)
    lo_half = lane < P

    def cumsums(dt):
        dta = dt * a2_row
        cf = jnp.dot(tri3_ref[...], jnp.concatenate(_split3(dta), axis=0), preferred_element_type=f32)
        rb = cf[q - 1:q, :] - cf + dta
        return cf, rb

    def load_chunk(r0):
        dt = dt_s[pl.ds(r0, q), :]
        xs = xs_s[pl.ds(r0, q), :]
        bcv = bc_s[pl.ds(r0, q), :]
        bmat = [bcv[:, g * N:(g + 1) * N] for g in range(SSD_GROUPS)]
        cmat = [bcv[:, SSD_BC + g * N:SSD_BC + (g + 1) * N] for g in range(SSD_GROUPS)]
        return dt, xs, bmat, cmat

    def inter_chunk(h_s, sel_ref, decay, weight, xs, bmat, cmat, dec_idx):
        ew = jnp.dot(jnp.concatenate([_pack3(decay), _pack3(weight)], axis=0), sel_ref[...],
                     preferred_element_type=f32)
        e_x, w_x = ew[:q], ew[q:]
        hb_ = h_s[...].astype(bf16)
        yoff = jnp.concatenate(
            [jnp.dot(cmat[g], hb_[:, g * gw:(g + 1) * gw], preferred_element_type=f32)
             for g in range(SSD_GROUPS)], axis=1)
        xw = (xs * w_x).astype(bf16)
        dec_row = e_x[dec_idx:dec_idx + 1, :]
        for g in range(SSD_GROUPS):
            gs = slice(g * gw, (g + 1) * gw)
            bt = jnp.transpose(bmat[g].astype(f32)).astype(bf16)
            upd = jnp.dot(bt, xw[:, gs], preferred_element_type=f32)
            h_s[:, gs] = h_s[:, gs] * dec_row[:, gs] + upd
        return yoff * e_x

    def fwd_body(c, carry):
        r0 = pl.multiple_of(c * q, q)
        dt, xs, bmat, cmat = load_chunk(r0)
        cf, rb = cumsums(dt)
        pcol = jnp.where(lane < H, cf, rb)
        bcast = jnp.dot(_pack3(pcol), selbc_ref[...], preferred_element_type=f32)
        prow = jnp.transpose(pcol - jnp.log2(dt))
        cbm = [lax.dot_general(cmat[g], bmat[g], (((1,), (1,)), ((), ())), preferred_element_type=f32)
               for g in range(SSD_GROUPS)]
        xsb = xs.astype(bf16)
        ypairs = []
        for k in range(H // 2):
            res = []
            for h in (2 * k, 2 * k + 1):
                g = h // (H // SSD_GROUPS)
                hb_ = H + h
                segf = bcast[:, h * q:(h + 1) * q] - prow[h:h + 1, :]
                segb = bcast[:, hb_ * q:(hb_ + 1) * q] - prow[hb_:hb_ + 1, :]
                df = jnp.exp2(jnp.where(lower, segf, NEG_INF))
                db = jnp.exp2(jnp.where(upper, segb, NEG_INF))
                mh = (cbm[g] * (df + db)).astype(bf16)
                res.append(jnp.dot(mh, xsb[:, k * LANES:(k + 1) * LANES], preferred_element_type=f32))
            ypairs.append(jnp.where(lo_half, res[0], res[1]))
        y = jnp.concatenate(ypairs, axis=1)
        wfa = jnp.exp2(cf[q - 1:q, :] - cf) * dt
        y_s[pl.ds(r0, q), :] = y + inter_chunk(hf_s, self_ref, jnp.exp2(cf), wfa, xs, bmat, cmat, q - 1)
        conv_chunk(jnp.minimum(c + 1, nc - 1))
        return carry

    lax.fori_loop(0, nc, fwd_body, 0)

    def bwd_body(i, carry):
        c = nc - 1 - i
        r0 = pl.multiple_of(c * q, q)
        dt, xs, bmat, cmat = load_chunk(r0)
        _, rb = cumsums(dt)
        wba = jnp.exp2(rb[0:1, :] - rb) * dt
        y = y_s[pl.ds(r0, q), :] + inter_chunk(hb_s, selb_ref, jnp.exp2(rb), wba, xs, bmat, cmat, 0)
        yy = y + xs * dskip_ref[...]
        zz = z_ref[pl.ds(r0, q), :].astype(f32)
        gated = yy * _silu(zz)
        ms = jnp.mean(gated * gated, axis=-1, keepdims=True)
        out_ref[pl.ds(r0, q), :] = (gated * lax.rsqrt(ms + EPS) * nw_ref[...]).astype(out_ref.dtype)
        return carry

    lax.fori_loop(0, nc, bwd_body, 0)
    hfin_ref[0] = hf_s[...]
    hfin_ref[1] = hb_s[...]


def _ssd(xbc, dt, z, conv_w, conv_b, dtb, alog, dskip, nw, h0, bsz, seq_len):
    one = pl.Buffered(1)
    seq = lambda w: pl.BlockSpec((seq_len, w), lambda b: (b, 0), pipeline_mode=one)
    const = lambda r, w: pl.BlockSpec((r, w), lambda b: (0, 0))
    st = pl.BlockSpec((None, 2, SSD_STATE, SSD_INNER), lambda b: (b, 0, 0, 0))
    sels = _ssd_selectors()
    return pl.pallas_call(
        functools.partial(_ssd_kernel, seq_len),
        out_shape=[jax.ShapeDtypeStruct((bsz * seq_len, SSD_INNER), bf16),
                   jax.ShapeDtypeStruct((bsz, 2, SSD_STATE, SSD_INNER), f32)],
        grid=(bsz,),
        in_specs=[seq(SSD_XBC), seq(LANES), seq(SSD_INNER),
                  const(8, SSD_XBC), const(1, SSD_XBC), const(1, LANES), const(1, LANES),
                  const(1, SSD_INNER), const(1, SSD_INNER)]
                 + [const(*s.shape) for s in sels] + [st],
        out_specs=[seq(SSD_INNER), st],
        scratch_shapes=[pltpu.VMEM((seq_len, SSD_INNER), f32),
                        pltpu.VMEM((seq_len, 2 * SSD_BC), bf16),
                        pltpu.VMEM((seq_len, LANES), f32),
                        pltpu.VMEM((seq_len, SSD_INNER), f32),
                        pltpu.VMEM((SSD_STATE, SSD_INNER), f32),
                        pltpu.VMEM((SSD_STATE, SSD_INNER), f32),
                        pltpu.VMEM((SSD_CHUNK + 32, SSD_XBC), bf16)],
        compiler_params=_cparams(("parallel",)),
        name="ssd",
    )(xbc, dt, z, conv_w, conv_b, dtb, alog, dskip, nw, *sels, h0)


def _attn_kernel(n_blocks, local, q_ref, g_ref, k_ref, v_ref, kc_ref, vc_ref, sink_ref, o_ref, s_s):
    t = ATT_BLOCK
    i = pl.program_id(1)
    n_ctx = kc_ref.shape[0]
    rpk = ATT_HEADS // ATT_KV_HEADS
    lane = lax.broadcasted_iota(jnp.int32, (t, LANES), 1)
    lo_half = lane < ATT_HEAD_DIM
    qv = q_ref[...]
    gv = g_ref[...].astype(f32)
    if local:
        p0 = pl.multiple_of(jnp.maximum(i - 1, 0) * t, t)
        c0 = pl.multiple_of(i * t, t)
        n0 = pl.multiple_of(jnp.minimum(i + 1, n_blocks - 1) * t, t)
        ql = lax.broadcasted_iota(jnp.int32, (t, t), 0)
        kl = lax.broadcasted_iota(jnp.int32, (t, t), 1)
        bias_prev = jnp.where((kl >= ql) & (i > 0), 0.0, NEG_INF)
        bias_next = jnp.where((kl <= ql) & (i < n_blocks - 1), 0.0, NEG_INF)
        bias_prev = jnp.concatenate([bias_prev] * rpk, axis=0)
        bias_next = jnp.concatenate([bias_next] * rpk, axis=0)
    zero_b = jnp.zeros((), bf16)
    for j in range(ATT_KV_HEADS):
        ls = slice(j * LANES, (j + 1) * LANES)
        if local:
            kk = jnp.concatenate([k_ref[pl.ds(p0, t), ls], k_ref[pl.ds(c0, t), ls],
                                  k_ref[pl.ds(n0, t), ls], kc_ref[:, ls]], axis=0)
        else:
            kk = kc_ref[:, ls]
        pieces = []
        for r in range(rpk):
            hq = j * rpk + r
            qp = qv[:, (hq // 2) * LANES:(hq // 2 + 1) * LANES]
            keep = lo_half if hq % 2 == 0 else jnp.logical_not(lo_half)
            pieces.append(jnp.where(keep, qp, zero_b))
        q4 = jnp.concatenate(pieces, axis=0)
        s = lax.dot_general(q4, kk, (((1,), (1,)), ((), ())), preferred_element_type=f32)
        if local:
            s = jnp.concatenate([s[:, :t] + bias_prev, s[:, t:2 * t], s[:, 2 * t:3 * t] + bias_next,
                                 s[:, 3 * t:]], axis=1)
        s_s[j] = s
    outs = []
    for j in range(ATT_KV_HEADS):
        ls = slice(j * LANES, (j + 1) * LANES)
        if local:
            vv = jnp.concatenate([v_ref[pl.ds(p0, t), ls], v_ref[pl.ds(c0, t), ls],
                                  v_ref[pl.ds(n0, t), ls], vc_ref[:, ls]], axis=0)
        else:
            vv = vc_ref[:, ls]
        sk = jnp.concatenate([jnp.full((t, 1), sink_ref[j * rpk + r] * LOG2E, f32) for r in range(rpk)],
                             axis=0)
        s = s_s[j]
        m = jnp.maximum(jnp.max(s, axis=1, keepdims=True), sk)
        p = jnp.exp2(s - m)
        den = jnp.sum(p, axis=1, keepdims=True) + jnp.exp2(sk - m)
        o4 = jnp.dot(p.astype(bf16), vv, preferred_element_type=f32) / den
        outs.append(jnp.where(lo_half, o4[0:t], o4[t:2 * t]))
        outs.append(jnp.where(lo_half, o4[2 * t:3 * t], o4[3 * t:4 * t]))
    o = jnp.concatenate(outs, axis=1)
    o_ref[...] = (o * _silu(gv)).astype(o_ref.dtype)


def _attention(q, g, k, v, kc, vc, sink, bsz, seq_len, local):
    t = ATT_BLOCK
    nb = seq_len // t
    n_ctx = kc.shape[1]
    kw = ATT_KV_HEADS * LANES
    blk = pl.BlockSpec((t, ATT_Q), lambda b, i: (b * nb + i, 0))
    full = lambda n: pl.BlockSpec((None, n, kw), lambda b, i: (b, 0, 0))
    return pl.pallas_call(
        functools.partial(_attn_kernel, nb, local),
        out_shape=jax.ShapeDtypeStruct((bsz * seq_len, ATT_Q), bf16),
        grid=(bsz, nb),
        in_specs=[blk, blk, full(k.shape[1]), full(v.shape[1]), full(n_ctx), full(n_ctx),
                  pl.BlockSpec(memory_space=pltpu.SMEM)],
        out_specs=blk,
        scratch_shapes=[pltpu.VMEM((ATT_KV_HEADS, (ATT_HEADS // ATT_KV_HEADS) * t,
                                    (3 * t if local else 0) + n_ctx), f32)],
        compiler_params=_cparams(("parallel", "arbitrary")),
        name="attention",
    )(q, g, k, v, kc, vc, sink)


def _outproj_kernel(a_ref, b_ref, w_ref, x_ref, gate_ref, o_ref):
    ka = a_ref.shape[1]
    acc = jnp.dot(a_ref[...], w_ref[:ka, :], preferred_element_type=f32)
    acc = acc + jnp.dot(b_ref[...], w_ref[ka:, :], preferred_element_type=f32)
    o_ref[...] = x_ref[...] + gate_ref[...] * acc


def _outproj(a, b, w, x, gate, rows_per_mod):
    m, d = x.shape
    tm = min(ROW_TILE, rows_per_mod)
    per = rows_per_mod // tm
    nmod = gate.shape[0]
    mod_idx = (lambda i: (i // per, 0, 0)) if nmod > 1 else (lambda i: (0, 0, 0))
    row = lambda w_: pl.BlockSpec((tm, w_), lambda i: (i, 0))
    return pl.pallas_call(
        _outproj_kernel,
        out_shape=jax.ShapeDtypeStruct((m, d), f32),
        grid=(m // tm,),
        in_specs=[row(a.shape[1]), row(b.shape[1]),
                  pl.BlockSpec(w.shape, lambda i: (0, 0)), row(d),
                  pl.BlockSpec((None, 1, d), mod_idx)],
        out_specs=row(d),
        compiler_params=_cparams(("parallel",)),
        name="outproj",
    )(a, b, w, x, gate)


def _s5_disc_kernel(lre_ref, lim_ref, ls_ref, bre_ref, bim_ref, abre_ref, abim_ref, bbre_ref, bbim_ref):
    lam_re = lre_ref[...]
    lam_im = lim_ref[...]
    dt = jnp.exp(ls_ref[...])
    mag = jnp.exp(lam_re * dt)
    ab_re = mag * jnp.cos(lam_im * dt)
    ab_im = mag * jnp.sin(lam_im * dt)
    num_re, num_im = ab_re - 1.0, ab_im
    den = lam_re * lam_re + lam_im * lam_im
    coef_re = (num_re * lam_re + num_im * lam_im) / den
    coef_im = (num_im * lam_re - num_re * lam_im) / den
    b_re, b_im = bre_ref[...], bim_ref[...]
    abre_ref[...] = ab_re
    abim_ref[...] = ab_im
    bbre_ref[...] = coef_re * b_re - coef_im * b_im
    bbim_ref[...] = coef_re * b_im + coef_im * b_re


def _s5_discretise(lam_re, lam_im, log_step, b_re, b_im):
    g, n, cg = b_re.shape
    exp = lambda t: jnp.repeat(t.reshape(2 * g, n), cg, axis=1)
    ls = jnp.broadcast_to(log_step.reshape(2 * g, 1), (2 * g, n * cg))
    bb = lambda t: jnp.tile(t.reshape(g, n * cg), (2, 1))
    shp = jax.ShapeDtypeStruct((2 * g, n * cg), f32)
    ab_re, ab_im, bb_re, bb_im = pl.pallas_call(
        _s5_disc_kernel, out_shape=[shp] * 4, name="s5_disc",
    )(exp(lam_re), exp(lam_im), ls, bb(b_re), bb(b_im))
    first = lambda t: t.reshape(2, g, n, cg)[..., 0]
    full = lambda t: t.reshape(2, g, n, cg)
    return first(ab_re), first(ab_im), full(bb_re), full(bb_im)


def _cmul(ar, ai, br, bi):
    return ar * br - ai * bi, ar * bi + ai * br


def _s5_kernel(n_lat, n_ctx, *refs):
    tc = S5_TC
    xl, xc = refs[:tc], refs[tc:2 * tc]
    arow_ref, acol_ref, bbd_ref, cbd_ref, y_ref, wyz_s, ws_s, sl_s, sc_s = refs[2 * tc:]
    sw = S5_GBLK * S5_STATE
    nsl = sw // LANES
    nb = 8

    def powers(re, im, n):
        out = [(jnp.ones_like(re), jnp.zeros_like(im))]
        for _ in range(n):
            out.append(_cmul(out[-1][0], out[-1][1], re, im))
        return out

    prow = [powers(arow_ref[2 * d:2 * d + 1, :], arow_ref[2 * d + 1:2 * d + 2, :], tc) for d in range(2)]

    for s in range(tc):
        for d, k in ((0, tc - 1 - s), (1, s)):
            wr, wi = _cmul(bbd_ref[2 * d], bbd_ref[2 * d + 1], *prow[d][k])
            ws_s[s * LANES:(s + 1) * LANES, d * 2 * sw:d * 2 * sw + sw] = wr.astype(bf16)
            ws_s[s * LANES:(s + 1) * LANES, d * 2 * sw + sw:(d + 1) * 2 * sw] = wi.astype(bf16)
    crhs = jnp.concatenate([cbd_ref[0], -cbd_ref[1]], axis=0).astype(bf16)
    kall = [jnp.dot(ws_s[:, d * 2 * sw:(d + 1) * 2 * sw], crhs, preferred_element_type=f32) for d in range(2)]
    kf = [kall[0][(tc - 1 - k) * LANES:(tc - k) * LANES] for k in range(tc)]
    kb = [kall[1][k * LANES:(k + 1) * LANES] for k in range(tc)]
    for s in range(tc):
        for t in range(tc):
            blk = kf[t - s] if t > s else (kb[s - t] if t < s else kf[0] + kb[0])
            wyz_s[s * LANES:(s + 1) * LANES, t * LANES:(t + 1) * LANES] = blk.astype(bf16)
    base = tc * LANES
    for d in range(2):
        a1 = (acol_ref[2 * d], acol_ref[2 * d + 1])
        ak = a1
        for k in range(1, tc + 1):
            t = k - 1 if d == 0 else tc - k
            dre, dim_ = _cmul(cbd_ref[0], cbd_ref[1], *ak)
            r0 = base + d * 2 * sw
            wyz_s[r0:r0 + sw, t * LANES:(t + 1) * LANES] = dre.astype(bf16)
            wyz_s[r0 + sw:r0 + 2 * sw, t * LANES:(t + 1) * LANES] = (-dim_).astype(bf16)
            if k < tc:
                ak = _cmul(*ak, *a1)

    def rows_of(x_refs, b, n):
        return jnp.concatenate([r[b * n:(b + 1) * n, :] for r in x_refs], axis=1)

    def inject(x_refs, s_ref, n):
        for b in range(nb):
            sb = jnp.dot(rows_of(x_refs, b, n), ws_s[...], preferred_element_type=f32)
            for k in range(4 * nsl):
                s_ref[k, pl.ds(b, n, stride=nb), :] = sb[:, k * LANES:(k + 1) * LANES]

    at = [[tuple(jnp.broadcast_to(p[:, k * LANES:(k + 1) * LANES], (nb, LANES)) for p in prow[d][tc])
           for k in range(nsl)] for d in range(2)]

    def scan(s_ref, n, init):
        def step(i, carry):
            new = []
            for d in range(2):
                idx = pl.ds(pl.multiple_of((i if d == 0 else n - 1 - i) * nb, nb), nb)
                for k in range(nsl):
                    hr, hi = carry[2 * (d * nsl + k)], carry[2 * (d * nsl + k) + 1]
                    kr, ki = d * 2 * nsl + k, d * 2 * nsl + nsl + k
                    sr, si = s_ref[kr, idx, :], s_ref[ki, idx, :]
                    s_ref[kr, idx, :] = hr
                    s_ref[ki, idx, :] = hi
                    ar, ai = at[d][k]
                    new += [ar * hr - ai * hi + sr, ar * hi + ai * hr + si]
            return tuple(new)
        return lax.fori_loop(0, n, step, init, unroll=4)

    inject(xc, sc_s, n_ctx)
    h_ctx = scan(sc_s, n_ctx, tuple(jnp.zeros((nb, LANES), f32) for _ in range(4 * nsl)))
    inject(xl, sl_s, n_lat)
    scan(sl_s, n_lat, h_ctx)

    for b in range(nb):
        hin = jnp.concatenate([sl_s[k, pl.ds(b, n_lat, stride=nb), :] for k in range(4 * nsl)],
                              axis=1).astype(bf16)
        yb = jnp.dot(jnp.concatenate([rows_of(xl, b, n_lat), hin], axis=1), wyz_s[...],
                     preferred_element_type=f32)
        for t in range(tc):
            y_ref[pl.ds(b * n_lat * tc + t, n_lat, stride=tc), :] = yb[:, t * LANES:(t + 1) * LANES]


def _s5_mix(u, u_c, arow, acol, bbd, cbd, bsz, seq_len, n_ctx_tok):
    assert bsz == 8, "the chunk recurrence puts the batch on the 8 sublanes"
    tc = S5_TC
    n_lat, n_ctx = seq_len // tc, n_ctx_tok // tc
    nblk = S5_GROUPS // S5_GBLK
    sw = S5_GBLK * S5_STATE
    nsl = sw // LANES
    one = pl.Buffered(1)
    ul, uc = u, u_c
    xspec = lambda rows, s: pl.BlockSpec((rows, LANES), lambda g: (0, s * nblk + g))
    par = lambda *shape: pl.BlockSpec((None,) + shape, lambda g: (g,) + (0,) * len(shape), pipeline_mode=one)
    return pl.pallas_call(
        functools.partial(_s5_kernel, n_lat, n_ctx),
        out_shape=jax.ShapeDtypeStruct((bsz * seq_len, S5_WIDTH), f32),
        grid=(nblk,),
        in_specs=[xspec(bsz * n_lat, s) for s in range(tc)] + [xspec(bsz * n_ctx, s) for s in range(tc)]
                 + [par(4, sw), par(4, sw, LANES), par(4, LANES, sw), par(2, sw, LANES)],
        out_specs=pl.BlockSpec((bsz * seq_len, LANES), lambda g: (0, g), pipeline_mode=one),
        scratch_shapes=[pltpu.VMEM((tc * LANES + 4 * sw, tc * LANES), bf16),
                        pltpu.VMEM((tc * LANES, 4 * sw), bf16),
                        pltpu.VMEM((4 * nsl, bsz * n_lat, LANES), f32),
                        pltpu.VMEM((4 * nsl, bsz * n_ctx, LANES), f32)],
        compiler_params=_cparams(("arbitrary",)),
        name="s5_mix",
    )(*([ul] * tc), *([uc] * tc), arow, acol, bbd, cbd)


def _s5_block_params(ab_re, ab_im, bb_re, bb_im, c_re, c_im):
    g, n, cg = S5_GROUPS, S5_STATE, S5_GROUP_CH
    nblk = g // S5_GBLK
    sw = S5_GBLK * n
    eye = jnp.eye(S5_GBLK, dtype=f32)
    arow = jnp.stack([t[d].reshape(nblk, sw) for d in range(2) for t in (ab_re, ab_im)], axis=1)
    acol = jnp.broadcast_to(arow[..., None], (nblk, 4, sw, LANES))

    def bd(t):
        t = t.reshape(nblk, S5_GBLK, n, cg)
        return jnp.einsum('jgnc,gh->jgchn', t, eye).reshape(nblk, S5_GBLK * cg, sw)

    def bdc(t):
        t = t.reshape(nblk, S5_GBLK, cg, n)
        return jnp.einsum('jgcn,gh->jgnhc', t, eye).reshape(nblk, sw, S5_GBLK * cg)

    bbd = jnp.stack([bd(t[d]) for d in range(2) for t in (bb_re, bb_im)], axis=1)
    cbd = jnp.stack([bdc(c_re), bdc(c_im)], axis=1)
    return arow, acol, bbd, cbd


def _s5_out_kernel(y_ref, u_ref, g_ref, x_ref, dskip_ref, gw_ref, gb_ref, w_ref, gate_ref,
                   fw_ref, o_ref):
    y = y_ref[...] + dskip_ref[...] * u_ref[...].astype(f32)
    y = jax.nn.gelu(y)
    glu = jnp.dot(y.astype(bf16), gw_ref[...], preferred_element_type=f32) + gb_ref[...]
    y = y * jax.nn.sigmoid(glu)
    y = y * _silu(g_ref[...].astype(f32))
    x = x_ref[...] + gate_ref[...] * jnp.dot(y.astype(bf16), w_ref[...], preferred_element_type=f32)
    ms = jnp.mean(x * x, axis=-1, keepdims=True)
    o_ref[...] = x * lax.rsqrt(ms + EPS) * fw_ref[...]


def _s5_out(y, u, g, x, dskip, gw, gb, w, gate, fw, rows_per_mod):
    m, d = x.shape
    tm = min(ROW_TILE, rows_per_mod)
    per = rows_per_mod // tm
    row = lambda: pl.BlockSpec((tm, d), lambda i: (i, 0))
    vec = lambda: pl.BlockSpec((1, d), lambda i: (0, 0))
    mat = lambda: pl.BlockSpec((d, d), lambda i: (0, 0))
    return pl.pallas_call(
        _s5_out_kernel,
        out_shape=jax.ShapeDtypeStruct((m, d), f32),
        grid=(m // tm,),
        in_specs=[row(), row(), row(), row(), vec(), mat(), vec(), mat(),
                  pl.BlockSpec((None, 1, d), lambda i: (i // per, 0, 0)), vec()],
        out_specs=row(),
        compiler_params=_cparams(("parallel",)),
        name="s5_out",
    )(y, u, g, x, dskip, gw, gb, w, gate, fw)


def _even_weights(w_in):
    o = 0
    z = w_in[:, o:o + SSD_INNER]; o += SSD_INNER
    xbc = w_in[:, o:o + SSD_XBC]; o += SSD_XBC
    dt = w_in[:, o:o + 2 * SSD_HEADS]; o += 2 * SSD_HEADS
    q = w_in[:, o:o + ATT_Q]; o += ATT_Q
    k = w_in[:, o:o + ATT_KVW]; o += ATT_KVW
    v = w_in[:, o:o + ATT_KVW]; o += ATT_KVW
    g = w_in[:, o:o + ATT_Q]
    d = w_in.shape[0]
    dup = lambda t: jnp.concatenate([t.reshape(d, ATT_KV_HEADS, 1, ATT_HEAD_DIM)] * 2, axis=2).reshape(d, -1)
    dtp = jnp.pad(dt, ((0, 0), (0, LANES - 2 * SSD_HEADS)))
    return jnp.concatenate([z, xbc, q, dup(k), dup(v), g, dtp], axis=1).astype(bf16)


def _even_segs(rope):
    scale = ATT_HEAD_DIM ** -0.5 * LOG2E
    widths = [(SSD_INNER, None, bf16), (SSD_XBC, None, bf16),
              (ATT_Q, scale, bf16),
              (ATT_KV_HEADS * LANES, 1.0 if rope else None, bf16),
              (ATT_KV_HEADS * LANES, None, bf16), (ATT_Q, None, bf16), (LANES, None, f32)]
    segs, o = [], 0
    for w, r, dtp in widths:
        segs.append((o, w, r, dtp, False))
        o += w
    return segs


def _pad_lanes(v, n=LANES):
    v = v.reshape(1, -1)
    return jnp.pad(v, ((0, 0), (0, n - v.shape[1])))


def kernel(x, c, ctx, c_ctx, e_norm_w, e_ada_w, e_ada_b, e_w_in, e_conv_w, e_conv_b, e_dt_bias,
           e_a_log, e_d_skip, e_ssd_norm_w, e_sink, e_w_out, o_norm_w, o_ada_w, o_ada_b, o_w_in,
           o_lam_re, o_lam_im, o_log_step, o_b_re, o_b_im, o_c_re, o_c_im, o_d_skip, o_glu_w,
           o_glu_b, o_w_out, final_norm_w):
    bsz, seq_len, d = x.shape
    n_ctx = ctx.shape[1]
    xf = x.reshape(bsz * seq_len, d)
    xcf = ctx.reshape(bsz * n_ctx, d)

    cvecs = jnp.concatenate([c, c_ctx[None, :], jnp.zeros((16 - bsz - 1, d), f32)], axis=0)

    def modulation(ada_w, ada_b):
        mod = _adaln(cvecs, ada_w.astype(bf16), ada_b.reshape(1, -1))
        parts = [mod[:, k * d:(k + 1) * d] for k in range(3)]
        lat = [p[:bsz].reshape(bsz, 1, d) for p in parts]
        cx = [p[bsz:bsz + 1].reshape(1, 1, d) for p in parts]
        return lat, cx

    (shift, scale, gate), (shift_c, scale_c, gate_c) = modulation(e_ada_w[0], e_ada_b[0])
    w_in = _even_weights(e_w_in[0])
    nw = e_norm_w[0].reshape(1, d)
    tabs = _rope_tables(seq_len)
    z, xbc, q, k, v, g, dt = _inproj(xf, shift, scale, nw, w_in, _even_segs(True), seq_len, tabs)
    z_c, xbc_c, q_c, k_c, v_c, g_c, dt_c = _inproj(xcf, shift_c, scale_c, nw, w_in, _even_segs(False), n_ctx)

    conv_w = jnp.pad(e_conv_w[0], ((0, 8 - SSD_CONV), (0, 0)))
    conv_b = e_conv_b[0].reshape(1, -1)
    dtb = _pad_lanes(e_dt_bias[0])
    alog = _pad_lanes(e_a_log[0])
    dskip = jnp.repeat(e_d_skip[0], SSD_HEAD_DIM).reshape(1, -1)
    snw = e_ssd_norm_w[0].reshape(1, -1)
    h0 = jnp.zeros((bsz, 2, SSD_STATE, SSD_INNER), f32)
    ssd_c, hfin = _ssd(xbc_c, dt_c, z_c, conv_w, conv_b, dtb, alog, dskip, snw, h0, bsz, n_ctx)
    ssd_o, _ = _ssd(xbc, dt, z, conv_w, conv_b, dtb, alog, dskip, snw, hfin, bsz, seq_len)

    kw = ATT_KV_HEADS * LANES
    k3, v3 = k.reshape(bsz, seq_len, kw), v.reshape(bsz, seq_len, kw)
    kc3, vc3 = k_c.reshape(bsz, n_ctx, kw), v_c.reshape(bsz, n_ctx, kw)
    sink = e_sink[0]
    att = _attention(q, g, k3, v3, kc3, vc3, sink, bsz, seq_len, True)
    att_c = _attention(q_c, g_c, kc3, vc3, kc3, vc3, sink, bsz, n_ctx, False)
    w_out = e_w_out[0].astype(bf16)
    x1 = _outproj(ssd_o, att, w_out, xf, gate, seq_len)
    xc1 = _outproj(ssd_c, att_c, w_out, xcf, gate_c, n_ctx)

    (shift, scale, gate), (shift_c, scale_c, _) = modulation(o_ada_w[0], o_ada_b[0])
    w_in = o_w_in[0].astype(bf16)
    nw = o_norm_w[0].reshape(1, d)
    u, u_ch, g2 = _inproj(x1, shift, scale, nw, w_in,
                          [(0, S5_WIDTH, None, bf16, False), (0, S5_WIDTH, None, bf16, True),
                           (S5_WIDTH, S5_WIDTH, None, bf16, False)], seq_len)
    (uc_ch,) = _inproj(xc1, shift_c, scale_c, nw, w_in[:, :S5_WIDTH], [(0, S5_WIDTH, None, bf16, True)], n_ctx)

    ab_re, ab_im, bb_re, bb_im = _s5_discretise(o_lam_re[0], o_lam_im[0], o_log_step[0], o_b_re[0], o_b_im[0])
    arow, acol, bbd, cbd = _s5_block_params(ab_re, ab_im, bb_re, bb_im, o_c_re[0], o_c_im[0])
    y = _s5_mix(u_ch, uc_ch, arow, acol, bbd, cbd, bsz, seq_len, n_ctx)
    out = _s5_out(y, u, g2, x1, o_d_skip[0].reshape(1, -1),
                  o_glu_w[0].astype(bf16), o_glu_b[0].reshape(1, -1), o_w_out[0].astype(bf16), gate,
                  final_norm_w.reshape(1, -1), seq_len)
    return out.reshape(bsz, seq_len, d)
```

```python
import functools
import math

import jax
import jax.numpy as jnp
import numpy as np
from jax import lax
from jax.experimental import pallas as pl
from jax.experimental.pallas import tpu as pltpu

f32 = jnp.float32
bf16 = jnp.bfloat16

D_MODEL = 1024
GRID_W = 64
EPS = 1e-6
NEG_INF = -1e30

SSD_HEADS = 16
SSD_HEAD_DIM = 64
SSD_GROUPS = 2
SSD_STATE = 128
SSD_CONV = 5
SSD_CHUNK = 128
SSD_INNER = SSD_HEADS * SSD_HEAD_DIM
SSD_BC = SSD_GROUPS * SSD_STATE
SSD_XBC = SSD_INNER + 2 * SSD_BC
ATT_HEADS = 16
ATT_KV_HEADS = 4
ATT_HEAD_DIM = 64
ATT_BLOCK = 128
ROPE_THETA = 10000.0
ATT_Q = ATT_HEADS * ATT_HEAD_DIM
ATT_KVW = ATT_KV_HEADS * ATT_HEAD_DIM
S5_WIDTH = 1024
S5_GROUP_CH = 16
S5_GROUPS = S5_WIDTH // S5_GROUP_CH
S5_STATE = 64

LOG2E = math.log2(math.e)
LANES = 128
ROW_TILE = 512
S5_GBLK = 8
S5_TC = 8
VMEM_LIMIT = 56 * 1024 * 1024


def _cparams(sem, flags=None):
    return pltpu.CompilerParams(dimension_semantics=sem, vmem_limit_bytes=VMEM_LIMIT, flags=flags)


def _silu(x):
    h = 0.5 * x
    return h + h * jnp.tanh(h)


def _adaln_kernel(c_ref, w_ref, b_ref, o_ref):
    c = c_ref[...]
    s = _silu(c).astype(bf16)
    o_ref[...] = jnp.dot(s, w_ref[...], preferred_element_type=f32) + b_ref[...]


def _adaln(cvecs, w, b):
    r, d = cvecs.shape
    n = w.shape[1]
    tn = 1024
    return pl.pallas_call(
        _adaln_kernel,
        out_shape=jax.ShapeDtypeStruct((r, n), f32),
        grid=(n // tn,),
        in_specs=[pl.BlockSpec((r, d), lambda j: (0, 0)),
                  pl.BlockSpec((d, tn), lambda j: (0, j)),
                  pl.BlockSpec((1, tn), lambda j: (0, j))],
        out_specs=pl.BlockSpec((r, tn), lambda j: (0, j)),
        compiler_params=_cparams(("arbitrary",)),
        name="adaln",
    )(cvecs, w, b)


def _inproj_kernel(segs, tsegs, has_rope, x_ref, shift_ref, scale_ref, nw_ref, w_ref, *rest):
    if tsegs:
        wt_ref, rest = rest[0], rest[1:]
    if has_rope:
        cos_ref, sina_ref, sinb_ref = rest[:3]
        rest = rest[3:]
    outs = rest[:len(segs)]
    touts = rest[len(segs):len(segs) + len(tsegs)]
    rest = rest[len(segs) + len(tsegs):]
    slab_s = rest[0] if rest else None
    x = x_ref[...]
    ms = jnp.mean(x * x, axis=-1, keepdims=True)
    h = (x * lax.rsqrt(ms + EPS)) * nw_ref[...]
    h = h * (1.0 + scale_ref[...]) + shift_ref[...]
    hb = h.astype(bf16)
    for (start, width, _), o_ref in zip(tsegs, touts):
        acc_t = lax.dot_general(wt_ref[start:start + width, :], hb, (((1,), (1,)), ((), ())),
                                preferred_element_type=f32)
        o_ref[...] = acc_t.astype(o_ref.dtype)
    products = {}
    for (start, width, rope, _, chunked), o_ref in zip(segs, outs):
        if (start, width) not in products:
            products[(start, width)] = jnp.dot(hb, w_ref[:, start:start + width], preferred_element_type=f32)
        acc = products[(start, width)]
        if chunked:
            tm = acc.shape[0]
            for j in range(width // LANES):
                slab_s[j] = acc[:, j * LANES:(j + 1) * LANES]
            for s in range(S5_TC):
                for j in range(width // LANES):
                    o_ref[j, :, s * LANES:(s + 1) * LANES] = (
                        slab_s[j, pl.ds(s, tm // S5_TC, stride=S5_TC), :].astype(o_ref.dtype))
            continue
        if rope is not None and not has_rope:
            acc = acc * rope
        elif rope is not None:
            rep = width // LANES
            cos = jnp.concatenate([cos_ref[...]] * rep, axis=1) * rope
            sina = jnp.concatenate([sina_ref[...]] * rep, axis=1) * rope
            sinb = jnp.concatenate([sinb_ref[...]] * rep, axis=1) * rope
            half = ATT_HEAD_DIM // 2
            up = pltpu.roll(acc, width - half, axis=1)
            dn = pltpu.roll(acc, half, axis=1)
            acc = acc * cos + up * sina + dn * sinb
        o_ref[...] = acc.astype(o_ref.dtype)


def _inproj(x, shift, scale, nw, w, segs, rows_per_mod, rope_tabs=None, wt=None, tsegs=()):
    m, d = x.shape
    tm = min(ROW_TILE, rows_per_mod)
    n = w.shape[1]
    per = rows_per_mod // tm
    nmod = shift.shape[0]
    mod_idx = (lambda i: (i // per, 0, 0)) if nmod > 1 else (lambda i: (0, 0, 0))
    in_specs = [pl.BlockSpec((tm, d), lambda i: (i, 0)),
                pl.BlockSpec((None, 1, d), mod_idx),
                pl.BlockSpec((None, 1, d), mod_idx),
                pl.BlockSpec((1, d), lambda i: (0, 0)),
                pl.BlockSpec((d, n), lambda i: (0, 0))]
    args = [x, shift, scale, nw, w]
    if tsegs:
        in_specs.append(pl.BlockSpec(wt.shape, lambda i: (0, 0)))
        args.append(wt)
    if rope_tabs is not None:
        for t in rope_tabs:
            in_specs.append(pl.BlockSpec((tm, LANES), lambda i: (i % per, 0)))
            args.append(t)
    out_shape, out_specs = [], []
    for sg in segs:
        if sg[4]:
            out_shape.append(jax.ShapeDtypeStruct((sg[1] // LANES, m // S5_TC, S5_TC * LANES), sg[3]))
            out_specs.append(pl.BlockSpec((sg[1] // LANES, tm // S5_TC, S5_TC * LANES), lambda i: (0, i, 0)))
        else:
            out_shape.append(jax.ShapeDtypeStruct((m, sg[1]), sg[3]))
            out_specs.append(pl.BlockSpec((tm, sg[1]), lambda i: (i, 0)))
    for _, width, dtp in tsegs:
        out_shape.append(jax.ShapeDtypeStruct((width, m), dtp))
        out_specs.append(pl.BlockSpec((width, tm), lambda i: (0, i)))
    chunked_w = [sg[1] for sg in segs if sg[4]]
    scratch = [pltpu.VMEM((max(chunked_w) // LANES, tm, LANES), f32)] if chunked_w else []
    return pl.pallas_call(
        functools.partial(_inproj_kernel, tuple(segs), tuple(tsegs), rope_tabs is not None),
        out_shape=out_shape,
        grid=(m // tm,),
        in_specs=in_specs,
        out_specs=out_specs,
        scratch_shapes=scratch,
        compiler_params=_cparams(("parallel",)),
        name="inproj",
    )(*args)


def _rope_tables(seq_len):
    rows = seq_len // GRID_W
    row = jnp.repeat(jnp.arange(rows, dtype=f32), GRID_W)
    col = jnp.tile(jnp.arange(GRID_W, dtype=f32), rows)
    n_freq = ATT_HEAD_DIM // 4
    inv = ROPE_THETA ** (-jnp.arange(n_freq, dtype=f32) / n_freq)
    ang = jnp.concatenate([row[:, None] * inv, col[:, None] * inv], axis=-1)
    cos, sin = jnp.cos(ang), jnp.sin(ang)
    zero = jnp.zeros_like(sin)
    cos_h = jnp.concatenate([cos, cos], axis=-1)
    sina_h = jnp.concatenate([-sin, zero], axis=-1)
    sinb_h = jnp.concatenate([zero, sin], axis=-1)
    two = lambda t: jnp.concatenate([t, t], axis=-1)
    return two(cos_h), two(sina_h), two(sinb_h)


SSD_PACK = 32


def _split3(x):
    hi = x.astype(bf16)
    r1 = x - hi.astype(f32)
    mid = r1.astype(bf16)
    lo = (r1 - mid.astype(f32)).astype(bf16)
    return hi, mid, lo


def _pack3(x):
    hi, mid, lo = _split3(x)
    lane = lax.broadcasted_iota(jnp.int32, x.shape, 1)
    mid_r = pltpu.roll(mid.astype(f32), SSD_PACK, axis=1)
    lo_r = pltpu.roll(lo.astype(f32), 2 * SSD_PACK, axis=1)
    packed = jnp.where(lane < SSD_PACK, hi.astype(f32),
                       jnp.where(lane < 2 * SSD_PACK, mid_r,
                                 jnp.where(lane < 3 * SSD_PACK, lo_r, 0.0)))
    return packed.astype(bf16)


def _ssd_selectors():
    k = np.arange(LANES)
    src = np.where(k < 3 * SSD_PACK, k % SSD_PACK, -1)
    col_blk = np.arange(SSD_PACK * SSD_CHUNK) // SSD_CHUNK
    sel_bc = (src[:, None] == col_blk[None, :])
    head = np.arange(SSD_INNER) // SSD_HEAD_DIM
    sel_f = (src[:, None] == head[None, :])
    sel_b = (src[:, None] == (head + SSD_HEADS)[None, :])
    tri3 = np.tile(np.tril(np.ones((SSD_CHUNK, SSD_CHUNK))), (1, 3))
    rows = np.arange(SSD_CHUNK)[:, None]
    cols = np.arange(SSD_CHUNK + 32)[None, :]
    half = SSD_CONV // 2
    shift = np.concatenate([cols == rows + 16 + d for d in range(-half, half + 1) if d != 0], axis=0)
    as_bf = lambda a: jnp.asarray(a.astype(np.float32), dtype=bf16)
    return as_bf(sel_bc), as_bf(sel_f), as_bf(sel_b), as_bf(tri3), as_bf(shift)


def _ssd_kernel(seq_len, xbc_ref, dt_ref, z_ref, cw_ref, cb_ref, dtb_ref, alog_ref, dskip_ref, nw_ref,
                selbc_ref, self_ref, selb_ref, tri3_ref, shift_ref, h0_ref, out_ref, hfin_ref,
                xs_s, bc_s, dt_s, y_s, hf_s, hb_s, win_s):
    q = SSD_CHUNK
    nc = seq_len // q
    halo = 16
    H, P, N = SSD_HEADS, SSD_HEAD_DIM, SSD_STATE
    gw = (H // SSD_GROUPS) * P
    a2_row = -jnp.exp(alog_ref[...]) * math.log2(math.e)

    def conv_chunk(c):
        r0 = pl.multiple_of(c * q, q)
        pstart = pl.multiple_of(jnp.maximum(r0 - halo, 0), halo)
        nstart = pl.multiple_of(jnp.minimum(r0 + q, seq_len - halo), halo)
        zero = jnp.zeros((), bf16)
        win_s[0:halo, :] = jnp.where(c > 0, xbc_ref[pl.ds(pstart, halo), :], zero)
        win_s[halo:halo + q, :] = xbc_ref[pl.ds(r0, q), :]
        win_s[halo + q:, :] = jnp.where(c < nc - 1, xbc_ref[pl.ds(nstart, halo), :], zero)
        taps = [k for k in range(SSD_CONV) if k != SSD_CONV // 2]
        cw = 2 * LANES
        for j in range(SSD_XBC // cw):
            cs = slice(j * cw, (j + 1) * cw)
            sh = jnp.dot(shift_ref[...], win_s[:, cs], preferred_element_type=f32)
            acc = cb_ref[:, cs] + win_s[halo:halo + q, cs].astype(f32) * cw_ref[SSD_CONV // 2:SSD_CONV // 2 + 1, cs]
            for n, k in enumerate(taps):
                acc = acc + sh[n * q:(n + 1) * q, :] * cw_ref[k:k + 1, cs]
            act = _silu(acc)
            if j < SSD_INNER // cw:
                xs_s[pl.ds(r0, q), cs] = act
            else:
                bc_s[pl.ds(r0, q), j * cw - SSD_INNER:(j + 1) * cw - SSD_INNER] = act.astype(bf16)
        dt_s[pl.ds(r0, q), :] = jax.nn.softplus(dt_ref[pl.ds(r0, q), :] + dtb_ref[...])

    conv_chunk(0)
    hf_s[...] = h0_ref[0]
    hb_s[...] = h0_ref[1]

    ri = lax.broadcasted_iota(jnp.int32, (q, q), 0)
    ci = lax.broadcasted_iota(jnp.int32, (q, q), 1)
    lower = ri >= ci
    upper = ci >= ri
    lane = lax.broadcasted_iota(jnp.int32, (q, LANES), 1)
    lo_half = lane < P

    def cumsums(dt):
        dta = dt * a2_row
        cf = jnp.dot(tri3_ref[...], jnp.concatenate(_split3(dta), axis=0), preferred_element_type=f32)
        rb = cf[q - 1:q, :] - cf + dta
        return cf, rb

    def load_chunk(r0):
        dt = dt_s[pl.ds(r0, q), :]
        xs = xs_s[pl.ds(r0, q), :]
        bcv = bc_s[pl.ds(r0, q), :]
        bmat = [bcv[:, g * N:(g + 1) * N] for g in range(SSD_GROUPS)]
        cmat = [bcv[:, SSD_BC + g * N:SSD_BC + (g + 1) * N] for g in range(SSD_GROUPS)]
        return dt, xs, bmat, cmat

    def inter_chunk(h_s, sel_ref, decay, weight, xs, bmat, cmat, dec_idx):
        ew = jnp.dot(jnp.concatenate([_pack3(decay), _pack3(weight)], axis=0), sel_ref[...],
                     preferred_element_type=f32)
        e_x, w_x = ew[:q], ew[q:]
        hb_ = h_s[...].astype(bf16)
        yoff = jnp.concatenate(
            [jnp.dot(cmat[g], hb_[:, g * gw:(g + 1) * gw], preferred_element_type=f32)
             for g in range(SSD_GROUPS)], axis=1)
        xw = (xs * w_x).astype(bf16)
        dec_row = e_x[dec_idx:dec_idx + 1, :]
        for g in range(SSD_GROUPS):
            gs = slice(g * gw, (g + 1) * gw)
            bt = jnp.transpose(bmat[g].astype(f32)).astype(bf16)
            upd = jnp.dot(bt, xw[:, gs], preferred_element_type=f32)
            h_s[:, gs] = h_s[:, gs] * dec_row[:, gs] + upd
        return yoff * e_x

    def fwd_body(c, carry):
        r0 = pl.multiple_of(c * q, q)
        dt, xs, bmat, cmat = load_chunk(r0)
        cf, rb = cumsums(dt)
        pcol = jnp.where(lane < H, cf, rb)
        bcast = jnp.dot(_pack3(pcol), selbc_ref[...], preferred_element_type=f32)
        prow = jnp.transpose(pcol - jnp.log2(dt))
        cbm = [lax.dot_general(cmat[g], bmat[g], (((1,), (1,)), ((), ())), preferred_element_type=f32)
               for g in range(SSD_GROUPS)]
        xsb = xs.astype(bf16)
        ypairs = []
        for k in range(H // 2):
            res = []
            for h in (2 * k, 2 * k + 1):
                g = h // (H // SSD_GROUPS)
                hb_ = H + h
                segf = bcast[:, h * q:(h + 1) * q] - prow[h:h + 1, :]
                segb = bcast[:, hb_ * q:(hb_ + 1) * q] - prow[hb_:hb_ + 1, :]
                df = jnp.exp2(jnp.where(lower, segf, NEG_INF))
                db = jnp.exp2(jnp.where(upper, segb, NEG_INF))
                mh = (cbm[g] * (df + db)).astype(bf16)
                res.append(jnp.dot(mh, xsb[:, k * LANES:(k + 1) * LANES], preferred_element_type=f32))
            ypairs.append(jnp.where(lo_half, res[0], res[1]))
        y = jnp.concatenate(ypairs, axis=1)
        wfa = jnp.exp2(cf[q - 1:q, :] - cf) * dt
        y_s[pl.ds(r0, q), :] = y + inter_chunk(hf_s, self_ref, jnp.exp2(cf), wfa, xs, bmat, cmat, q - 1)
        conv_chunk(jnp.minimum(c + 1, nc - 1))
        return carry

    lax.fori_loop(0, nc, fwd_body, 0)

    def bwd_body(i, carry):
        c = nc - 1 - i
        r0 = pl.multiple_of(c * q, q)
        dt, xs, bmat, cmat = load_chunk(r0)
        _, rb = cumsums(dt)
        wba = jnp.exp2(rb[0:1, :] - rb) * dt
        y = y_s[pl.ds(r0, q), :] + inter_chunk(hb_s, selb_ref, jnp.exp2(rb), wba, xs, bmat, cmat, 0)
        yy = y + xs * dskip_ref[...]
        zz = z_ref[pl.ds(r0, q), :].astype(f32)
        gated = yy * _silu(zz)
        ms = jnp.mean(gated * gated, axis=-1, keepdims=True)
        out_ref[pl.ds(r0, q), :] = (gated * lax.rsqrt(ms + EPS) * nw_ref[...]).astype(out_ref.dtype)
        return carry

    lax.fori_loop(0, nc, bwd_body, 0)
    hfin_ref[0] = hf_s[...]
    hfin_ref[1] = hb_s[...]


def _ssd(xbc, dt, z, conv_w, conv_b, dtb, alog, dskip, nw, h0, bsz, seq_len):
    one = pl.Buffered(1)
    seq = lambda w: pl.BlockSpec((seq_len, w), lambda b: (b, 0), pipeline_mode=one)
    const = lambda r, w: pl.BlockSpec((r, w), lambda b: (0, 0))
    st = pl.BlockSpec((None, 2, SSD_STATE, SSD_INNER), lambda b: (b, 0, 0, 0))
    sels = _ssd_selectors()
    return pl.pallas_call(
        functools.partial(_ssd_kernel, seq_len),
        out_shape=[jax.ShapeDtypeStruct((bsz * seq_len, SSD_INNER), bf16),
                   jax.ShapeDtypeStruct((bsz, 2, SSD_STATE, SSD_INNER), f32)],
        grid=(bsz,),
        in_specs=[seq(SSD_XBC), seq(LANES), seq(SSD_INNER),
                  const(8, SSD_XBC), const(1, SSD_XBC), const(1, LANES), const(1, LANES),
                  const(1, SSD_INNER), const(1, SSD_INNER)]
                 + [const(*s.shape) for s in sels] + [st],
        out_specs=[seq(SSD_INNER), st],
        scratch_shapes=[pltpu.VMEM((seq_len, SSD_INNER), f32),
                        pltpu.VMEM((seq_len, 2 * SSD_BC), bf16),
                        pltpu.VMEM((seq_len, LANES), f32),
                        pltpu.VMEM((seq_len, SSD_INNER), f32),
                        pltpu.VMEM((SSD_STATE, SSD_INNER), f32),
                        pltpu.VMEM((SSD_STATE, SSD_INNER), f32),
                        pltpu.VMEM((SSD_CHUNK + 32, SSD_XBC), bf16)],
        compiler_params=_cparams(("parallel",)),
        name="ssd",
    )(xbc, dt, z, conv_w, conv_b, dtb, alog, dskip, nw, *sels, h0)


def _attn_kernel(n_blocks, local, q_ref, g_ref, k_ref, kc_ref, *rest):
    if local:
        vp_ref, vcur_ref, vn_ref, vc_ref, sink_ref, o_ref, s_s = rest
    else:
        vc_ref, sink_ref, o_ref, s_s = rest
    t = ATT_BLOCK
    i = pl.program_id(1)
    rpk = ATT_HEADS // ATT_KV_HEADS
    lane = lax.broadcasted_iota(jnp.int32, (t, LANES), 1)
    lo_half = lane < ATT_HEAD_DIM
    qv = q_ref[...]
    gv = g_ref[...].astype(f32)
    if local:
        p0 = pl.multiple_of(jnp.maximum(i - 1, 0) * t, t)
        c0 = pl.multiple_of(i * t, t)
        n0 = pl.multiple_of(jnp.minimum(i + 1, n_blocks - 1) * t, t)
        kl = lax.broadcasted_iota(jnp.int32, (t, t), 0)
        ql = lax.broadcasted_iota(jnp.int32, (t, t), 1)
        bias_prev = jnp.where((kl >= ql) & (i > 0), 0.0, NEG_INF)
        bias_next = jnp.where((kl <= ql) & (i < n_blocks - 1), 0.0, NEG_INF)
        bias_prev = jnp.concatenate([bias_prev] * rpk, axis=1)
        bias_next = jnp.concatenate([bias_next] * rpk, axis=1)
    zero_b = jnp.zeros((), bf16)
    for j in range(ATT_KV_HEADS):
        ls = slice(j * LANES, (j + 1) * LANES)
        if local:
            kk = jnp.concatenate([k_ref[pl.ds(p0, t), ls], k_ref[pl.ds(c0, t), ls],
                                  k_ref[pl.ds(n0, t), ls], kc_ref[:, ls]], axis=0)
        else:
            kk = kc_ref[:, ls]
        pieces = []
        for r in range(rpk):
            hq = j * rpk + r
            qp = qv[:, (hq // 2) * LANES:(hq // 2 + 1) * LANES]
            keep = lo_half if hq % 2 == 0 else jnp.logical_not(lo_half)
            pieces.append(jnp.where(keep, qp, zero_b))
        q4 = jnp.concatenate(pieces, axis=0)
        s = lax.dot_general(kk, q4, (((1,), (1,)), ((), ())), preferred_element_type=f32)
        if local:
            s = jnp.concatenate([s[:t] + bias_prev, s[t:2 * t], s[2 * t:3 * t] + bias_next, s[3 * t:]],
                                axis=0)
        s_s[j] = s
    outs = []
    for j in range(ATT_KV_HEADS):
        ls = slice(j * LANES, (j + 1) * LANES)
        if local:
            vvt = jnp.concatenate([vp_ref[ls, :], vcur_ref[ls, :], vn_ref[ls, :], vc_ref[ls, :]], axis=1)
        else:
            vvt = vc_ref[ls, :]
        sk = jnp.concatenate([jnp.full((1, t), sink_ref[j * rpk + r] * LOG2E, f32) for r in range(rpk)],
                             axis=1)
        s = s_s[j]
        m = jnp.maximum(jnp.max(s, axis=0, keepdims=True), sk)
        p = jnp.exp2(s - m)
        den = jnp.sum(p, axis=0, keepdims=True) + jnp.exp2(sk - m)
        ot = jnp.dot(vvt, p.astype(bf16), preferred_element_type=f32) / den
        o4 = [jnp.transpose(ot[:, r * t:(r + 1) * t]) for r in range(rpk)]
        outs.append(jnp.where(lo_half, o4[0], o4[1]))
        outs.append(jnp.where(lo_half, o4[2], o4[3]))
    o = jnp.concatenate(outs, axis=1)
    o_ref[...] = (o * _silu(gv)).astype(o_ref.dtype)


def _attention(q, g, k, vt, kc, vct, sink, bsz, seq_len, local):
    t = ATT_BLOCK
    nb = seq_len // t
    n_ctx = kc.shape[1]
    kw = ATT_KV_HEADS * LANES
    blk = pl.BlockSpec((t, ATT_Q), lambda b, i: (b * nb + i, 0))
    full = lambda n: pl.BlockSpec((None, n, kw), lambda b, i: (b, 0, 0))
    vblk = lambda off: pl.BlockSpec((kw, t), lambda b, i: (0, b * nb + jnp.clip(i + off, 0, nb - 1)))
    vspecs = [vblk(-1), vblk(0), vblk(1)] if local else []
    vargs = [vt, vt, vt] if local else []
    return pl.pallas_call(
        functools.partial(_attn_kernel, nb, local),
        out_shape=jax.ShapeDtypeStruct((bsz * seq_len, ATT_Q), bf16),
        grid=(bsz, nb),
        in_specs=[blk, blk, full(k.shape[1]), full(n_ctx)] + vspecs
                 + [pl.BlockSpec((kw, n_ctx), lambda b, i: (0, b)), pl.BlockSpec(memory_space=pltpu.SMEM)],
        out_specs=blk,
        scratch_shapes=[pltpu.VMEM((ATT_KV_HEADS, (3 * t if local else 0) + n_ctx,
                                    (ATT_HEADS // ATT_KV_HEADS) * t), f32)],
        compiler_params=_cparams(("parallel", "arbitrary")),
        name="attention",
    )(q, g, k, kc, *vargs, vct, sink)


def _outproj_kernel(a_ref, b_ref, w_ref, x_ref, gate_ref, o_ref):
    ka = a_ref.shape[1]
    acc = jnp.dot(a_ref[...], w_ref[:ka, :], preferred_element_type=f32)
    acc = acc + jnp.dot(b_ref[...], w_ref[ka:, :], preferred_element_type=f32)
    o_ref[...] = x_ref[...] + gate_ref[...] * acc


def _outproj(a, b, w, x, gate, rows_per_mod):
    m, d = x.shape
    tm = min(ROW_TILE, rows_per_mod)
    per = rows_per_mod // tm
    nmod = gate.shape[0]
    mod_idx = (lambda i: (i // per, 0, 0)) if nmod > 1 else (lambda i: (0, 0, 0))
    row = lambda w_: pl.BlockSpec((tm, w_), lambda i: (i, 0))
    return pl.pallas_call(
        _outproj_kernel,
        out_shape=jax.ShapeDtypeStruct((m, d), f32),
        grid=(m // tm,),
        in_specs=[row(a.shape[1]), row(b.shape[1]),
                  pl.BlockSpec(w.shape, lambda i: (0, 0)), row(d),
                  pl.BlockSpec((None, 1, d), mod_idx)],
        out_specs=row(d),
        compiler_params=_cparams(("parallel",)),
        name="outproj",
    )(a, b, w, x, gate)


def _s5_disc_kernel(lre_ref, lim_ref, ls_ref, bre_ref, bim_ref, abre_ref, abim_ref, bbre_ref, bbim_ref):
    lam_re = lre_ref[...]
    lam_im = lim_ref[...]
    dt = jnp.exp(ls_ref[...])
    mag = jnp.exp(lam_re * dt)
    ab_re = mag * jnp.cos(lam_im * dt)
    ab_im = mag * jnp.sin(lam_im * dt)
    num_re, num_im = ab_re - 1.0, ab_im
    den = lam_re * lam_re + lam_im * lam_im
    coef_re = (num_re * lam_re + num_im * lam_im) / den
    coef_im = (num_im * lam_re - num_re * lam_im) / den
    b_re, b_im = bre_ref[...], bim_ref[...]
    abre_ref[...] = ab_re
    abim_ref[...] = ab_im
    bbre_ref[...] = coef_re * b_re - coef_im * b_im
    bbim_ref[...] = coef_re * b_im + coef_im * b_re


def _s5_discretise(lam_re, lam_im, log_step, b_re, b_im):
    g, n, cg = b_re.shape
    exp = lambda t: jnp.repeat(t.reshape(2 * g, n), cg, axis=1)
    ls = jnp.broadcast_to(log_step.reshape(2 * g, 1), (2 * g, n * cg))
    bb = lambda t: jnp.tile(t.reshape(g, n * cg), (2, 1))
    shp = jax.ShapeDtypeStruct((2 * g, n * cg), f32)
    ab_re, ab_im, bb_re, bb_im = pl.pallas_call(
        _s5_disc_kernel, out_shape=[shp] * 4, name="s5_disc",
    )(exp(lam_re), exp(lam_im), ls, bb(b_re), bb(b_im))
    first = lambda t: t.reshape(2, g, n, cg)[..., 0]
    full = lambda t: t.reshape(2, g, n, cg)
    return first(ab_re), first(ab_im), full(bb_re), full(bb_im)


def _cmul(ar, ai, br, bi):
    return ar * br - ai * bi, ar * bi + ai * br


def _s5_kernel(n_lat, n_ctx, *refs):
    tc = S5_TC
    xl, xc, arow_ref, acol_ref, bbd_ref, cbd_ref, y_ref, wyz_s, ws_s, sl_s, sc_s = refs
    sw = S5_GBLK * S5_STATE
    nsl = sw // LANES
    nb = 8

    def powers(re, im, n):
        out = [(jnp.ones_like(re), jnp.zeros_like(im))]
        for _ in range(n):
            out.append(_cmul(out[-1][0], out[-1][1], re, im))
        return out

    prow = [powers(arow_ref[2 * d:2 * d + 1, :], arow_ref[2 * d + 1:2 * d + 2, :], tc) for d in range(2)]

    for s in range(tc):
        for d, k in ((0, tc - 1 - s), (1, s)):
            wr, wi = _cmul(bbd_ref[2 * d], bbd_ref[2 * d + 1], *prow[d][k])
            ws_s[s * LANES:(s + 1) * LANES, d * 2 * sw:d * 2 * sw + sw] = wr.astype(bf16)
            ws_s[s * LANES:(s + 1) * LANES, d * 2 * sw + sw:(d + 1) * 2 * sw] = wi.astype(bf16)
    crhs = jnp.concatenate([cbd_ref[0], -cbd_ref[1]], axis=0).astype(bf16)
    kall = [jnp.dot(ws_s[:, d * 2 * sw:(d + 1) * 2 * sw], crhs, preferred_element_type=f32) for d in range(2)]
    kf = [kall[0][(tc - 1 - k) * LANES:(tc - k) * LANES] for k in range(tc)]
    kb = [kall[1][k * LANES:(k + 1) * LANES] for k in range(tc)]
    for s in range(tc):
        for t in range(tc):
            blk = kf[t - s] if t > s else (kb[s - t] if t < s else kf[0] + kb[0])
            wyz_s[s * LANES:(s + 1) * LANES, t * LANES:(t + 1) * LANES] = blk.astype(bf16)
    base = tc * LANES
    for d in range(2):
        a1 = (acol_ref[2 * d], acol_ref[2 * d + 1])
        ak = a1
        for k in range(1, tc + 1):
            t = k - 1 if d == 0 else tc - k
            dre, dim_ = _cmul(cbd_ref[0], cbd_ref[1], *ak)
            r0 = base + d * 2 * sw
            wyz_s[r0:r0 + sw, t * LANES:(t + 1) * LANES] = dre.astype(bf16)
            wyz_s[r0 + sw:r0 + 2 * sw, t * LANES:(t + 1) * LANES] = (-dim_).astype(bf16)
            if k < tc:
                ak = _cmul(*ak, *a1)

    def rows_of(x_ref, b, n):
        return x_ref[b * n:(b + 1) * n, :]

    def inject(x_refs, s_ref, n):
        for b in range(nb):
            sb = jnp.dot(rows_of(x_refs, b, n), ws_s[...], preferred_element_type=f32)
            for k in range(4 * nsl):
                s_ref[k, pl.ds(b, n, stride=nb), :] = sb[:, k * LANES:(k + 1) * LANES]

    at = [[tuple(jnp.broadcast_to(p[:, k * LANES:(k + 1) * LANES], (nb, LANES)) for p in prow[d][tc])
           for k in range(nsl)] for d in range(2)]

    def scan(s_ref, n, init):
        def step(i, carry):
            new = []
            for d in range(2):
                idx = pl.ds(pl.multiple_of((i if d == 0 else n - 1 - i) * nb, nb), nb)
                for k in range(nsl):
                    hr, hi = carry[2 * (d * nsl + k)], carry[2 * (d * nsl + k) + 1]
                    kr, ki = d * 2 * nsl + k, d * 2 * nsl + nsl + k
                    sr, si = s_ref[kr, idx, :], s_ref[ki, idx, :]
                    s_ref[kr, idx, :] = hr
                    s_ref[ki, idx, :] = hi
                    ar, ai = at[d][k]
                    new += [ar * hr - ai * hi + sr, ar * hi + ai * hr + si]
            return tuple(new)
        return lax.fori_loop(0, n, step, init, unroll=4)

    inject(xc, sc_s, n_ctx)
    h_ctx = scan(sc_s, n_ctx, tuple(jnp.zeros((nb, LANES), f32) for _ in range(4 * nsl)))
    inject(xl, sl_s, n_lat)
    scan(sl_s, n_lat, h_ctx)

    for b in range(nb):
        hin = jnp.concatenate([sl_s[k, pl.ds(b, n_lat, stride=nb), :] for k in range(4 * nsl)],
                              axis=1).astype(bf16)
        yb = jnp.dot(jnp.concatenate([rows_of(xl, b, n_lat), hin], axis=1), wyz_s[...],
                     preferred_element_type=f32)
        for t in range(tc):
            y_ref[pl.ds(b * n_lat * tc + t, n_lat, stride=tc), :] = yb[:, t * LANES:(t + 1) * LANES]


def _s5_mix(u, u_c, arow, acol, bbd, cbd, bsz, seq_len, n_ctx_tok):
    assert bsz == 8, "the chunk recurrence puts the batch on the 8 sublanes"
    tc = S5_TC
    n_lat, n_ctx = seq_len // tc, n_ctx_tok // tc
    nblk = S5_GROUPS // S5_GBLK
    sw = S5_GBLK * S5_STATE
    nsl = sw // LANES
    one = pl.Buffered(1)
    xspec = lambda rows: pl.BlockSpec((None, rows, tc * LANES), lambda g: (g, 0, 0))
    par = lambda *shape: pl.BlockSpec((None,) + shape, lambda g: (g,) + (0,) * len(shape), pipeline_mode=one)
    return pl.pallas_call(
        functools.partial(_s5_kernel, n_lat, n_ctx),
        out_shape=jax.ShapeDtypeStruct((nblk, bsz * seq_len, LANES), f32),
        grid=(nblk,),
        in_specs=[xspec(bsz * n_lat), xspec(bsz * n_ctx)]
                 + [par(4, sw), par(4, sw, LANES), par(4, LANES, sw), par(2, sw, LANES)],
        out_specs=pl.BlockSpec((None, bsz * seq_len, LANES), lambda g: (g, 0, 0), pipeline_mode=one),
        scratch_shapes=[pltpu.VMEM((tc * LANES + 4 * sw, tc * LANES), bf16),
                        pltpu.VMEM((tc * LANES, 4 * sw), bf16),
                        pltpu.VMEM((4 * nsl, bsz * n_lat, LANES), f32),
                        pltpu.VMEM((4 * nsl, bsz * n_ctx, LANES), f32)],
        compiler_params=_cparams(("arbitrary",)),
        name="s5_mix",
    )(u, u_c, arow, acol, bbd, cbd)


def _s5_block_params(ab_re, ab_im, bb_re, bb_im, c_re, c_im):
    g, n, cg = S5_GROUPS, S5_STATE, S5_GROUP_CH
    nblk = g // S5_GBLK
    sw = S5_GBLK * n
    eye = jnp.eye(S5_GBLK, dtype=f32)
    arow = jnp.stack([t[d].reshape(nblk, sw) for d in range(2) for t in (ab_re, ab_im)], axis=1)
    acol = jnp.broadcast_to(arow[..., None], (nblk, 4, sw, LANES))

    def bd(t):
        t = t.reshape(nblk, S5_GBLK, n, cg)
        return jnp.einsum('jgnc,gh->jgchn', t, eye).reshape(nblk, S5_GBLK * cg, sw)

    def bdc(t):
        t = t.reshape(nblk, S5_GBLK, cg, n)
        return jnp.einsum('jgcn,gh->jgnhc', t, eye).reshape(nblk, sw, S5_GBLK * cg)

    bbd = jnp.stack([bd(t[d]) for d in range(2) for t in (bb_re, bb_im)], axis=1)
    cbd = jnp.stack([bdc(c_re), bdc(c_im)], axis=1)
    return arow, acol, bbd, cbd


def _s5_out_kernel(y_ref, u_ref, g_ref, x_ref, dskip_ref, gw_ref, gb_ref, w_ref, gate_ref,
                   fw_ref, o_ref):
    y = jnp.concatenate([y_ref[j] for j in range(y_ref.shape[0])], axis=1)
    y = y + dskip_ref[...] * u_ref[...].astype(f32)
    y = jax.nn.gelu(y)
    glu = jnp.dot(y.astype(bf16), gw_ref[...], preferred_element_type=f32) + gb_ref[...]
    y = y * jax.nn.sigmoid(glu)
    y = y * _silu(g_ref[...].astype(f32))
    x = x_ref[...] + gate_ref[...] * jnp.dot(y.astype(bf16), w_ref[...], preferred_element_type=f32)
    ms = jnp.mean(x * x, axis=-1, keepdims=True)
    o_ref[...] = x * lax.rsqrt(ms + EPS) * fw_ref[...]


def _s5_out(y, u, g, x, dskip, gw, gb, w, gate, fw, rows_per_mod):
    m, d = x.shape
    tm = min(ROW_TILE, rows_per_mod)
    per = rows_per_mod // tm
    row = lambda: pl.BlockSpec((tm, d), lambda i: (i, 0))
    vec = lambda: pl.BlockSpec((1, d), lambda i: (0, 0))
    mat = lambda: pl.BlockSpec((d, d), lambda i: (0, 0))
    return pl.pallas_call(
        _s5_out_kernel,
        out_shape=jax.ShapeDtypeStruct((m, d), f32),
        grid=(m // tm,),
        in_specs=[pl.BlockSpec((y.shape[0], tm, LANES), lambda i: (0, i, 0)),
                  row(), row(), row(), vec(), mat(), vec(), mat(),
                  pl.BlockSpec((None, 1, d), lambda i: (i // per, 0, 0)), vec()],
        out_specs=row(),
        compiler_params=_cparams(("parallel",)),
        name="s5_out",
    )(y, u, g, x, dskip, gw, gb, w, gate, fw)


def _even_weights(w_in):
    o = 0
    z = w_in[:, o:o + SSD_INNER]; o += SSD_INNER
    xbc = w_in[:, o:o + SSD_XBC]; o += SSD_XBC
    dt = w_in[:, o:o + 2 * SSD_HEADS]; o += 2 * SSD_HEADS
    q = w_in[:, o:o + ATT_Q]; o += ATT_Q
    k = w_in[:, o:o + ATT_KVW]; o += ATT_KVW
    v = w_in[:, o:o + ATT_KVW]; o += ATT_KVW
    g = w_in[:, o:o + ATT_Q]
    d = w_in.shape[0]
    dup = lambda t: jnp.concatenate([t.reshape(d, ATT_KV_HEADS, 1, ATT_HEAD_DIM)] * 2, axis=2).reshape(d, -1)
    dtp = jnp.pad(dt, ((0, 0), (0, LANES - 2 * SSD_HEADS)))
    return jnp.concatenate([z, xbc, q, dup(k), g, dtp], axis=1).astype(bf16), dup(v).T.astype(bf16)


def _even_segs(rope):
    scale = ATT_HEAD_DIM ** -0.5 * LOG2E
    widths = [(SSD_INNER, None, bf16), (SSD_XBC, None, bf16),
              (ATT_Q, scale, bf16),
              (ATT_KV_HEADS * LANES, 1.0 if rope else None, bf16),
              (ATT_Q, None, bf16), (LANES, None, f32)]
    segs, o = [], 0
    for w, r, dtp in widths:
        segs.append((o, w, r, dtp, False))
        o += w
    return segs


def _pad_lanes(v, n=LANES):
    v = v.reshape(1, -1)
    return jnp.pad(v, ((0, 0), (0, n - v.shape[1])))


def kernel(x, c, ctx, c_ctx, e_norm_w, e_ada_w, e_ada_b, e_w_in, e_conv_w, e_conv_b, e_dt_bias,
           e_a_log, e_d_skip, e_ssd_norm_w, e_sink, e_w_out, o_norm_w, o_ada_w, o_ada_b, o_w_in,
           o_lam_re, o_lam_im, o_log_step, o_b_re, o_b_im, o_c_re, o_c_im, o_d_skip, o_glu_w,
           o_glu_b, o_w_out, final_norm_w):
    bsz, seq_len, d = x.shape
    n_ctx = ctx.shape[1]
    xf = x.reshape(bsz * seq_len, d)
    xcf = ctx.reshape(bsz * n_ctx, d)

    cvecs = jnp.concatenate([c, c_ctx[None, :], jnp.zeros((16 - bsz - 1, d), f32)], axis=0)

    def modulation(ada_w, ada_b):
        mod = _adaln(cvecs, ada_w.astype(bf16), ada_b.reshape(1, -1))
        parts = [mod[:, k * d:(k + 1) * d] for k in range(3)]
        lat = [p[:bsz].reshape(bsz, 1, d) for p in parts]
        cx = [p[bsz:bsz + 1].reshape(1, 1, d) for p in parts]
        return lat, cx

    (shift, scale, gate), (shift_c, scale_c, gate_c) = modulation(e_ada_w[0], e_ada_b[0])
    w_in, w_vt = _even_weights(e_w_in[0])
    nw = e_norm_w[0].reshape(1, d)
    tabs = _rope_tables(seq_len)
    kw = ATT_KV_HEADS * LANES
    vseg = [(0, kw, bf16)]
    z, xbc, q, k, g, dt, vt = _inproj(xf, shift, scale, nw, w_in, _even_segs(True), seq_len, tabs, w_vt, vseg)
    z_c, xbc_c, q_c, k_c, g_c, dt_c, vt_c = _inproj(xcf, shift_c, scale_c, nw, w_in, _even_segs(False), n_ctx,
                                                    None, w_vt, vseg)

    conv_w = jnp.pad(e_conv_w[0], ((0, 8 - SSD_CONV), (0, 0)))
    conv_b = e_conv_b[0].reshape(1, -1)
    dtb = _pad_lanes(e_dt_bias[0])
    alog = _pad_lanes(e_a_log[0])
    dskip = jnp.repeat(e_d_skip[0], SSD_HEAD_DIM).reshape(1, -1)
    snw = e_ssd_norm_w[0].reshape(1, -1)
    h0 = jnp.zeros((bsz, 2, SSD_STATE, SSD_INNER), f32)
    ssd_c, hfin = _ssd(xbc_c, dt_c, z_c, conv_w, conv_b, dtb, alog, dskip, snw, h0, bsz, n_ctx)
    ssd_o, _ = _ssd(xbc, dt, z, conv_w, conv_b, dtb, alog, dskip, snw, hfin, bsz, seq_len)

    k3 = k.reshape(bsz, seq_len, kw)
    kc3 = k_c.reshape(bsz, n_ctx, kw)
    sink = e_sink[0]
    att = _attention(q, g, k3, vt, kc3, vt_c, sink, bsz, seq_len, True)
    att_c = _attention(q_c, g_c, kc3, vt_c, kc3, vt_c, sink, bsz, n_ctx, False)
    w_out = e_w_out[0].astype(bf16)
    x1 = _outproj(ssd_o, att, w_out, xf, gate, seq_len)
    xc1 = _outproj(ssd_c, att_c, w_out, xcf, gate_c, n_ctx)

    (shift, scale, gate), (shift_c, scale_c, _) = modulation(o_ada_w[0], o_ada_b[0])
    w_in = o_w_in[0].astype(bf16)
    nw = o_norm_w[0].reshape(1, d)
    u, u_ch, g2 = _inproj(x1, shift, scale, nw, w_in,
                          [(0, S5_WIDTH, None, bf16, False), (0, S5_WIDTH, None, bf16, True),
                           (S5_WIDTH, S5_WIDTH, None, bf16, False)], seq_len)
    (uc_ch,) = _inproj(xc1, shift_c, scale_c, nw, w_in[:, :S5_WIDTH], [(0, S5_WIDTH, None, bf16, True)], n_ctx)

    ab_re, ab_im, bb_re, bb_im = _s5_discretise(o_lam_re[0], o_lam_im[0], o_log_step[0], o_b_re[0], o_b_im[0])
    arow, acol, bbd, cbd = _s5_block_params(ab_re, ab_im, bb_re, bb_im, o_c_re[0], o_c_im[0])
    y = _s5_mix(u_ch, uc_ch, arow, acol, bbd, cbd, bsz, seq_len, n_ctx)
    out = _s5_out(y, u, g2, x1, o_d_skip[0].reshape(1, -1),
                  o_glu_w[0].astype(bf16), o_glu_b[0].reshape(1, -1), o_w_out[0].astype(bf16), gate,
                  final_norm_w.reshape(1, -1), seq_len)
    return out.reshape(bsz, seq_len, d)
```

```python
import functools
import math

import jax
import jax.numpy as jnp
import numpy as np
from jax import lax
from jax.experimental import pallas as pl
from jax.experimental.pallas import tpu as pltpu

f32 = jnp.float32
bf16 = jnp.bfloat16

D_MODEL = 1024
GRID_W = 64
EPS = 1e-6
NEG_INF = -1e30

SSD_HEADS = 16
SSD_HEAD_DIM = 64
SSD_GROUPS = 2
SSD_STATE = 128
SSD_CONV = 5
SSD_CHUNK = 128
SSD_INNER = SSD_HEADS * SSD_HEAD_DIM
SSD_BC = SSD_GROUPS * SSD_STATE
SSD_XBC = SSD_INNER + 2 * SSD_BC
ATT_HEADS = 16
ATT_KV_HEADS = 4
ATT_HEAD_DIM = 64
ATT_BLOCK = 128
ROPE_THETA = 10000.0
ATT_Q = ATT_HEADS * ATT_HEAD_DIM
ATT_KVW = ATT_KV_HEADS * ATT_HEAD_DIM
S5_WIDTH = 1024
S5_GROUP_CH = 16
S5_GROUPS = S5_WIDTH // S5_GROUP_CH
S5_STATE = 64

LOG2E = math.log2(math.e)
LANES = 128
ROW_TILE = 512
S5_GBLK = 8
S5_TC = 8
VMEM_LIMIT = 56 * 1024 * 1024


def _cparams(sem, flags=None):
    return pltpu.CompilerParams(dimension_semantics=sem, vmem_limit_bytes=VMEM_LIMIT, flags=flags)


def _silu(x):
    h = 0.5 * x
    return h + h * jnp.tanh(h)


def _adaln_kernel(c_ref, w_ref, b_ref, o_ref):
    c = c_ref[...]
    s = _silu(c).astype(bf16)
    o_ref[...] = jnp.dot(s, w_ref[...], preferred_element_type=f32) + b_ref[...]


def _adaln(cvecs, w, b):
    r, d = cvecs.shape
    n = w.shape[1]
    tn = 1024
    return pl.pallas_call(
        _adaln_kernel,
        out_shape=jax.ShapeDtypeStruct((r, n), f32),
        grid=(n // tn,),
        in_specs=[pl.BlockSpec((r, d), lambda j: (0, 0)),
                  pl.BlockSpec((d, tn), lambda j: (0, j)),
                  pl.BlockSpec((1, tn), lambda j: (0, j))],
        out_specs=pl.BlockSpec((r, tn), lambda j: (0, j)),
        compiler_params=_cparams(("arbitrary",)),
        name="adaln",
    )(cvecs, w, b)


def _inproj_kernel(segs, tsegs, has_rope, x_ref, shift_ref, scale_ref, nw_ref, w_ref, *rest):
    if tsegs:
        wt_ref, rest = rest[0], rest[1:]
    if has_rope:
        cos_ref, sina_ref, sinb_ref = rest[:3]
        rest = rest[3:]
    outs = rest[:len(segs)]
    touts = rest[len(segs):len(segs) + len(tsegs)]
    rest = rest[len(segs) + len(tsegs):]
    slab_s = rest[0] if rest else None
    x = x_ref[...]
    ms = jnp.mean(x * x, axis=-1, keepdims=True)
    h = (x * lax.rsqrt(ms + EPS)) * nw_ref[...]
    h = h * (1.0 + scale_ref[...]) + shift_ref[...]
    hb = h.astype(bf16)
    for (start, width, _), o_ref in zip(tsegs, touts):
        acc_t = lax.dot_general(wt_ref[start:start + width, :], hb, (((1,), (1,)), ((), ())),
                                preferred_element_type=f32)
        o_ref[...] = acc_t.astype(o_ref.dtype)
    products = {}
    for (start, width, rope, _, chunked), o_ref in zip(segs, outs):
        if (start, width) not in products:
            products[(start, width)] = jnp.dot(hb, w_ref[:, start:start + width], preferred_element_type=f32)
        acc = products[(start, width)]
        if chunked:
            tm = acc.shape[0]
            for j in range(width // LANES):
                slab_s[j] = acc[:, j * LANES:(j + 1) * LANES]
            for s in range(S5_TC):
                for j in range(width // LANES):
                    o_ref[j, :, s * LANES:(s + 1) * LANES] = (
                        slab_s[j, pl.ds(s, tm // S5_TC, stride=S5_TC), :].astype(o_ref.dtype))
            continue
        if rope is not None and not has_rope:
            acc = acc * rope
        elif rope is not None:
            rep = width // LANES
            cos = jnp.concatenate([cos_ref[...]] * rep, axis=1) * rope
            sina = jnp.concatenate([sina_ref[...]] * rep, axis=1) * rope
            sinb = jnp.concatenate([sinb_ref[...]] * rep, axis=1) * rope
            half = ATT_HEAD_DIM // 2
            up = pltpu.roll(acc, width - half, axis=1)
            dn = pltpu.roll(acc, half, axis=1)
            acc = acc * cos + up * sina + dn * sinb
        o_ref[...] = acc.astype(o_ref.dtype)


def _inproj(x, shift, scale, nw, w, segs, rows_per_mod, rope_tabs=None, wt=None, tsegs=()):
    m, d = x.shape
    tm = min(ROW_TILE, rows_per_mod)
    n = w.shape[1]
    per = rows_per_mod // tm
    nmod = shift.shape[0]
    mod_idx = (lambda i: (i // per, 0, 0)) if nmod > 1 else (lambda i: (0, 0, 0))
    in_specs = [pl.BlockSpec((tm, d), lambda i: (i, 0)),
                pl.BlockSpec((None, 1, d), mod_idx),
                pl.BlockSpec((None, 1, d), mod_idx),
                pl.BlockSpec((1, d), lambda i: (0, 0)),
                pl.BlockSpec((d, n), lambda i: (0, 0))]
    args = [x, shift, scale, nw, w]
    if tsegs:
        in_specs.append(pl.BlockSpec(wt.shape, lambda i: (0, 0)))
        args.append(wt)
    if rope_tabs is not None:
        for t in rope_tabs:
            in_specs.append(pl.BlockSpec((tm, LANES), lambda i: (i % per, 0)))
            args.append(t)
    out_shape, out_specs = [], []
    for sg in segs:
        if sg[4]:
            out_shape.append(jax.ShapeDtypeStruct((sg[1] // LANES, m // S5_TC, S5_TC * LANES), sg[3]))
            out_specs.append(pl.BlockSpec((sg[1] // LANES, tm // S5_TC, S5_TC * LANES), lambda i: (0, i, 0)))
        else:
            out_shape.append(jax.ShapeDtypeStruct((m, sg[1]), sg[3]))
            out_specs.append(pl.BlockSpec((tm, sg[1]), lambda i: (i, 0)))
    for _, width, dtp in tsegs:
        out_shape.append(jax.ShapeDtypeStruct((width, m), dtp))
        out_specs.append(pl.BlockSpec((width, tm), lambda i: (0, i)))
    chunked_w = [sg[1] for sg in segs if sg[4]]
    scratch = [pltpu.VMEM((max(chunked_w) // LANES, tm, LANES), f32)] if chunked_w else []
    return pl.pallas_call(
        functools.partial(_inproj_kernel, tuple(segs), tuple(tsegs), rope_tabs is not None),
        out_shape=out_shape,
        grid=(m // tm,),
        in_specs=in_specs,
        out_specs=out_specs,
        scratch_shapes=scratch,
        compiler_params=_cparams(("parallel",)),
        name="inproj",
    )(*args)


def _rope_tables(seq_len):
    rows = seq_len // GRID_W
    row = jnp.repeat(jnp.arange(rows, dtype=f32), GRID_W)
    col = jnp.tile(jnp.arange(GRID_W, dtype=f32), rows)
    n_freq = ATT_HEAD_DIM // 4
    inv = ROPE_THETA ** (-jnp.arange(n_freq, dtype=f32) / n_freq)
    ang = jnp.concatenate([row[:, None] * inv, col[:, None] * inv], axis=-1)
    cos, sin = jnp.cos(ang), jnp.sin(ang)
    zero = jnp.zeros_like(sin)
    cos_h = jnp.concatenate([cos, cos], axis=-1)
    sina_h = jnp.concatenate([-sin, zero], axis=-1)
    sinb_h = jnp.concatenate([zero, sin], axis=-1)
    two = lambda t: jnp.concatenate([t, t], axis=-1)
    return two(cos_h), two(sina_h), two(sinb_h)


SSD_PACK = 32


def _split3(x):
    hi = x.astype(bf16)
    r1 = x - hi.astype(f32)
    mid = r1.astype(bf16)
    lo = (r1 - mid.astype(f32)).astype(bf16)
    return hi, mid, lo


def _pack3(x):
    hi, mid, lo = _split3(x)
    lane = lax.broadcasted_iota(jnp.int32, x.shape, 1)
    mid_r = pltpu.roll(mid.astype(f32), SSD_PACK, axis=1)
    lo_r = pltpu.roll(lo.astype(f32), 2 * SSD_PACK, axis=1)
    packed = jnp.where(lane < SSD_PACK, hi.astype(f32),
                       jnp.where(lane < 2 * SSD_PACK, mid_r,
                                 jnp.where(lane < 3 * SSD_PACK, lo_r, 0.0)))
    return packed.astype(bf16)


def _ssd_selectors():
    k = np.arange(LANES)
    src = np.where(k < 3 * SSD_PACK, k % SSD_PACK, -1)
    col_blk = np.arange(SSD_PACK * SSD_CHUNK) // SSD_CHUNK
    sel_bc = (src[:, None] == col_blk[None, :])
    head = np.arange(SSD_INNER) // SSD_HEAD_DIM
    sel_f = (src[:, None] == head[None, :])
    sel_b = (src[:, None] == (head + SSD_HEADS)[None, :])
    tri3 = np.tile(np.tril(np.ones((SSD_CHUNK, SSD_CHUNK))), (1, 3))
    rows = np.arange(SSD_CHUNK)[:, None]
    cols = np.arange(SSD_CHUNK + 32)[None, :]
    half = SSD_CONV // 2
    shift = np.concatenate([cols == rows + 16 + d for d in range(-half, half + 1) if d != 0], axis=0)
    as_bf = lambda a: jnp.asarray(a.astype(np.float32), dtype=bf16)
    return as_bf(sel_bc), as_bf(sel_f), as_bf(sel_b), as_bf(tri3), as_bf(shift)


def _ssd_kernel(seq_len, xbc_ref, dt_ref, z_ref, cw_ref, cb_ref, dtb_ref, alog_ref, dskip_ref, nw_ref,
                selbc_ref, self_ref, selb_ref, tri3_ref, shift_ref, h0_ref, out_ref, hfin_ref,
                xs_s, bc_s, dt_s, y_s, hf_s, hb_s, win_s):
    q = SSD_CHUNK
    nc = seq_len // q
    halo = 16
    H, P, N = SSD_HEADS, SSD_HEAD_DIM, SSD_STATE
    gw = (H // SSD_GROUPS) * P
    a2_row = -jnp.exp(alog_ref[...]) * math.log2(math.e)

    def conv_chunk(c):
        r0 = pl.multiple_of(c * q, q)
        pstart = pl.multiple_of(jnp.maximum(r0 - halo, 0), halo)
        nstart = pl.multiple_of(jnp.minimum(r0 + q, seq_len - halo), halo)
        zero = jnp.zeros((), bf16)
        win_s[0:halo, :] = jnp.where(c > 0, xbc_ref[pl.ds(pstart, halo), :], zero)
        win_s[halo:halo + q, :] = xbc_ref[pl.ds(r0, q), :]
        win_s[halo + q:, :] = jnp.where(c < nc - 1, xbc_ref[pl.ds(nstart, halo), :], zero)
        taps = [k for k in range(SSD_CONV) if k != SSD_CONV // 2]
        cw = 2 * LANES
        for j in range(SSD_XBC // cw):
            cs = slice(j * cw, (j + 1) * cw)
            sh = jnp.dot(shift_ref[...], win_s[:, cs], preferred_element_type=f32)
            acc = cb_ref[:, cs] + win_s[halo:halo + q, cs].astype(f32) * cw_ref[SSD_CONV // 2:SSD_CONV // 2 + 1, cs]
            for n, k in enumerate(taps):
                acc = acc + sh[n * q:(n + 1) * q, :] * cw_ref[k:k + 1, cs]
            act = _silu(acc)
            if j < SSD_INNER // cw:
                xs_s[pl.ds(r0, q), cs] = act
            else:
                bc_s[pl.ds(r0, q), j * cw - SSD_INNER:(j + 1) * cw - SSD_INNER] = act.astype(bf16)
        dt_s[pl.ds(r0, q), :] = jax.nn.softplus(dt_ref[pl.ds(r0, q), :] + dtb_ref[...])

    conv_chunk(0)
    hf_s[...] = h0_ref[0]
    hb_s[...] = h0_ref[1]

    ri = lax.broadcasted_iota(jnp.int32, (q, q), 0)
    ci = lax.broadcasted_iota(jnp.int32, (q, q), 1)
    lower = ri >= ci
    upper = ci >= ri
    lane = lax.broadcasted_iota(jnp.int32, (q, LANES), 1)
    lo_half = lane < P

    def cumsums(dt):
        dta = dt * a2_row
        cf = jnp.dot(tri3_ref[...], jnp.concatenate(_split3(dta), axis=0), preferred_element_type=f32)
        rb = cf[q - 1:q, :] - cf + dta
        return cf, rb

    def load_chunk(r0):
        dt = dt_s[pl.ds(r0, q), :]
        xs = xs_s[pl.ds(r0, q), :]
        bcv = bc_s[pl.ds(r0, q), :]
        bmat = [bcv[:, g * N:(g + 1) * N] for g in range(SSD_GROUPS)]
        cmat = [bcv[:, SSD_BC + g * N:SSD_BC + (g + 1) * N] for g in range(SSD_GROUPS)]
        return dt, xs, bmat, cmat

    def inter_chunk(h_s, sel_ref, decay, weight, xs, bmat, cmat, dec_idx):
        ew = jnp.dot(jnp.concatenate([_pack3(decay), _pack3(weight)], axis=0), sel_ref[...],
                     preferred_element_type=f32)
        e_x, w_x = ew[:q], ew[q:]
        hb_ = h_s[...].astype(bf16)
        yoff = jnp.concatenate(
            [jnp.dot(cmat[g], hb_[:, g * gw:(g + 1) * gw], preferred_element_type=f32)
             for g in range(SSD_GROUPS)], axis=1)
        xw = (xs * w_x).astype(bf16)
        dec_row = e_x[dec_idx:dec_idx + 1, :]
        for g in range(SSD_GROUPS):
            gs = slice(g * gw, (g + 1) * gw)
            bt = jnp.transpose(bmat[g].astype(f32)).astype(bf16)
            upd = jnp.dot(bt, xw[:, gs], preferred_element_type=f32)
            h_s[:, gs] = h_s[:, gs] * dec_row[:, gs] + upd
        return yoff * e_x

    def finish(r0, y, xs):
        yy = y + xs * dskip_ref[...]
        zz = z_ref[pl.ds(r0, q), :].astype(f32)
        gated = yy * _silu(zz)
        ms = jnp.mean(gated * gated, axis=-1, keepdims=True)
        out_ref[pl.ds(r0, q), :] = (gated * lax.rsqrt(ms + EPS) * nw_ref[...]).astype(out_ref.dtype)

    def fwd_chunk(c, second_half):
        r0 = pl.multiple_of(c * q, q)
        dt, xs, bmat, cmat = load_chunk(r0)
        cf, rb = cumsums(dt)
        pcol = jnp.where(lane < H, cf, rb)
        bcast = jnp.dot(_pack3(pcol), selbc_ref[...], preferred_element_type=f32)
        prow = jnp.transpose(pcol - jnp.log2(dt))
        cbm = [lax.dot_general(cmat[g], bmat[g], (((1,), (1,)), ((), ())), preferred_element_type=f32)
               for g in range(SSD_GROUPS)]
        xsb = xs.astype(bf16)
        zero_b = jnp.zeros((), bf16)
        ypairs = []
        for k in range(H // 2):
            mats = []
            for h in (2 * k, 2 * k + 1):
                g = h // (H // SSD_GROUPS)
                hb_ = H + h
                segf = bcast[:, h * q:(h + 1) * q] - prow[h:h + 1, :]
                segb = bcast[:, hb_ * q:(hb_ + 1) * q] - prow[hb_:hb_ + 1, :]
                df = jnp.exp2(jnp.where(lower, segf, NEG_INF))
                db = jnp.exp2(jnp.where(upper, segb, NEG_INF))
                mats.append((cbm[g] * (df + db)).astype(bf16))
            xp = xsb[:, k * LANES:(k + 1) * LANES]
            xbd = jnp.concatenate([jnp.where(lo_half, xp, zero_b), jnp.where(lo_half, zero_b, xp)], axis=0)
            ypairs.append(jnp.dot(jnp.concatenate(mats, axis=1), xbd, preferred_element_type=f32))
        y = jnp.concatenate(ypairs, axis=1)
        wfa = jnp.exp2(cf[q - 1:q, :] - cf) * dt
        y = y + inter_chunk(hf_s, self_ref, jnp.exp2(cf), wfa, xs, bmat, cmat, q - 1)
        if second_half:
            finish(r0, y_s[pl.ds(r0, q), :] + y, xs)
        else:
            y_s[pl.ds(r0, q), :] = y

    def bwd_chunk(c, second_half):
        r0 = pl.multiple_of(c * q, q)
        dt, xs, bmat, cmat = load_chunk(r0)
        _, rb = cumsums(dt)
        wba = jnp.exp2(rb[0:1, :] - rb) * dt
        y = inter_chunk(hb_s, selb_ref, jnp.exp2(rb), wba, xs, bmat, cmat, 0)
        if second_half:
            finish(r0, y_s[pl.ds(r0, q), :] + y, xs)
        else:
            y_s[pl.ds(r0, q), :] = y

    half = nc // 2
    conv_chunk(nc - 1)

    def first_half(i, carry):
        fwd_chunk(i, False)
        bwd_chunk(nc - 1 - i, False)
        conv_chunk(jnp.minimum(i + 1, half))
        conv_chunk(jnp.maximum(nc - 2 - i, half - 1))
        return carry

    def second_half(i, carry):
        fwd_chunk(i, True)
        bwd_chunk(nc - 1 - i, True)
        return carry

    lax.fori_loop(0, half, first_half, 0)
    lax.fori_loop(half, nc, second_half, 0)
    hfin_ref[0] = hf_s[...]
    hfin_ref[1] = hb_s[...]


def _ssd(xbc, dt, z, conv_w, conv_b, dtb, alog, dskip, nw, h0, bsz, seq_len):
    assert seq_len % (2 * SSD_CHUNK) == 0, "the two recurrences meet in the middle: even chunk count"
    one = pl.Buffered(1)
    seq = lambda w: pl.BlockSpec((seq_len, w), lambda b: (b, 0), pipeline_mode=one)
    const = lambda r, w: pl.BlockSpec((r, w), lambda b: (0, 0))
    st = pl.BlockSpec((None, 2, SSD_STATE, SSD_INNER), lambda b: (b, 0, 0, 0))
    sels = _ssd_selectors()
    return pl.pallas_call(
        functools.partial(_ssd_kernel, seq_len),
        out_shape=[jax.ShapeDtypeStruct((bsz * seq_len, SSD_INNER), bf16),
                   jax.ShapeDtypeStruct((bsz, 2, SSD_STATE, SSD_INNER), f32)],
        grid=(bsz,),
        in_specs=[seq(SSD_XBC), seq(LANES), seq(SSD_INNER),
                  const(8, SSD_XBC), const(1, SSD_XBC), const(1, LANES), const(1, LANES),
                  const(1, SSD_INNER), const(1, SSD_INNER)]
                 + [const(*s.shape) for s in sels] + [st],
        out_specs=[seq(SSD_INNER), st],
        scratch_shapes=[pltpu.VMEM((seq_len, SSD_INNER), f32),
                        pltpu.VMEM((seq_len, 2 * SSD_BC), bf16),
                        pltpu.VMEM((seq_len, LANES), f32),
                        pltpu.VMEM((seq_len, SSD_INNER), f32),
                        pltpu.VMEM((SSD_STATE, SSD_INNER), f32),
                        pltpu.VMEM((SSD_STATE, SSD_INNER), f32),
                        pltpu.VMEM((SSD_CHUNK + 32, SSD_XBC), bf16)],
        compiler_params=_cparams(("parallel",)),
        name="ssd",
    )(xbc, dt, z, conv_w, conv_b, dtb, alog, dskip, nw, *sels, h0)


def _attn_kernel(n_blocks, local, q_ref, g_ref, k_ref, kc_ref, *rest):
    if local:
        vp_ref, vcur_ref, vn_ref, vc_ref, sink_ref, o_ref, s_s = rest
    else:
        vc_ref, sink_ref, o_ref, s_s = rest
    t = ATT_BLOCK
    i = pl.program_id(1)
    rpk = ATT_HEADS // ATT_KV_HEADS
    lane = lax.broadcasted_iota(jnp.int32, (t, LANES), 1)
    lo_half = lane < ATT_HEAD_DIM
    qv = q_ref[...]
    gv = g_ref[...].astype(f32)
    if local:
        p0 = pl.multiple_of(jnp.maximum(i - 1, 0) * t, t)
        c0 = pl.multiple_of(i * t, t)
        n0 = pl.multiple_of(jnp.minimum(i + 1, n_blocks - 1) * t, t)
        kl = lax.broadcasted_iota(jnp.int32, (t, t), 0)
        ql = lax.broadcasted_iota(jnp.int32, (t, t), 1)
        bias_prev = jnp.where((kl >= ql) & (i > 0), 0.0, NEG_INF)
        bias_next = jnp.where((kl <= ql) & (i < n_blocks - 1), 0.0, NEG_INF)
        bias_prev = jnp.concatenate([bias_prev] * rpk, axis=1)
        bias_next = jnp.concatenate([bias_next] * rpk, axis=1)
    zero_b = jnp.zeros((), bf16)
    for j in range(ATT_KV_HEADS):
        ls = slice(j * LANES, (j + 1) * LANES)
        if local:
            kk = jnp.concatenate([k_ref[pl.ds(p0, t), ls], k_ref[pl.ds(c0, t), ls],
                                  k_ref[pl.ds(n0, t), ls], kc_ref[:, ls]], axis=0)
        else:
            kk = kc_ref[:, ls]
        pieces = []
        for r in range(rpk):
            hq = j * rpk + r
            qp = qv[:, (hq // 2) * LANES:(hq // 2 + 1) * LANES]
            keep = lo_half if hq % 2 == 0 else jnp.logical_not(lo_half)
            pieces.append(jnp.where(keep, qp, zero_b))
        q4 = jnp.concatenate(pieces, axis=0)
        s = lax.dot_general(kk, q4, (((1,), (1,)), ((), ())), preferred_element_type=f32)
        if local:
            s = jnp.concatenate([s[:t] + bias_prev, s[t:2 * t], s[2 * t:3 * t] + bias_next, s[3 * t:]],
                                axis=0)
        s_s[j] = s
    outs = []
    for j in range(ATT_KV_HEADS):
        ls = slice(j * LANES, (j + 1) * LANES)
        if local:
            vvt = jnp.concatenate([vp_ref[ls, :], vcur_ref[ls, :], vn_ref[ls, :], vc_ref[ls, :]], axis=1)
        else:
            vvt = vc_ref[ls, :]
        sk = jnp.concatenate([jnp.full((1, t), sink_ref[j * rpk + r] * LOG2E, f32) for r in range(rpk)],
                             axis=1)
        s = s_s[j]
        m = jnp.maximum(jnp.max(s, axis=0, keepdims=True), sk)
        p = jnp.exp2(s - m)
        den = jnp.sum(p, axis=0, keepdims=True) + jnp.exp2(sk - m)
        ot = jnp.dot(vvt, p.astype(bf16), preferred_element_type=f32) / den
        o4 = [jnp.transpose(ot[:, r * t:(r + 1) * t]) for r in range(rpk)]
        outs.append(jnp.where(lo_half, o4[0], o4[1]))
        outs.append(jnp.where(lo_half, o4[2], o4[3]))
    o = jnp.concatenate(outs, axis=1)
    o_ref[...] = (o * _silu(gv)).astype(o_ref.dtype)


def _attention(q, g, k, vt, kc, vct, sink, bsz, seq_len, local):
    t = ATT_BLOCK
    nb = seq_len // t
    n_ctx = kc.shape[1]
    kw = ATT_KV_HEADS * LANES
    blk = pl.BlockSpec((t, ATT_Q), lambda b, i: (b * nb + i, 0))
    full = lambda n: pl.BlockSpec((None, n, kw), lambda b, i: (b, 0, 0))
    vblk = lambda off: pl.BlockSpec((kw, t), lambda b, i: (0, b * nb + jnp.clip(i + off, 0, nb - 1)))
    vspecs = [vblk(-1), vblk(0), vblk(1)] if local else []
    vargs = [vt, vt, vt] if local else []
    return pl.pallas_call(
        functools.partial(_attn_kernel, nb, local),
        out_shape=jax.ShapeDtypeStruct((bsz * seq_len, ATT_Q), bf16),
        grid=(bsz, nb),
        in_specs=[blk, blk, full(k.shape[1]), full(n_ctx)] + vspecs
                 + [pl.BlockSpec((kw, n_ctx), lambda b, i: (0, b)), pl.BlockSpec(memory_space=pltpu.SMEM)],
        out_specs=blk,
        scratch_shapes=[pltpu.VMEM((ATT_KV_HEADS, (3 * t if local else 0) + n_ctx,
                                    (ATT_HEADS // ATT_KV_HEADS) * t), f32)],
        compiler_params=_cparams(("parallel", "arbitrary")),
        name="attention",
    )(q, g, k, kc, *vargs, vct, sink)


def _outproj_kernel(a_ref, b_ref, w_ref, x_ref, gate_ref, o_ref):
    ka = a_ref.shape[1]
    acc = jnp.dot(a_ref[...], w_ref[:ka, :], preferred_element_type=f32)
    acc = acc + jnp.dot(b_ref[...], w_ref[ka:, :], preferred_element_type=f32)
    o_ref[...] = x_ref[...] + gate_ref[...] * acc


def _outproj(a, b, w, x, gate, rows_per_mod):
    m, d = x.shape
    tm = min(ROW_TILE, rows_per_mod)
    per = rows_per_mod // tm
    nmod = gate.shape[0]
    mod_idx = (lambda i: (i // per, 0, 0)) if nmod > 1 else (lambda i: (0, 0, 0))
    row = lambda w_: pl.BlockSpec((tm, w_), lambda i: (i, 0))
    return pl.pallas_call(
        _outproj_kernel,
        out_shape=jax.ShapeDtypeStruct((m, d), f32),
        grid=(m // tm,),
        in_specs=[row(a.shape[1]), row(b.shape[1]),
                  pl.BlockSpec(w.shape, lambda i: (0, 0)), row(d),
                  pl.BlockSpec((None, 1, d), mod_idx)],
        out_specs=row(d),
        compiler_params=_cparams(("parallel",)),
        name="outproj",
    )(a, b, w, x, gate)


def _s5_disc_kernel(lre_ref, lim_ref, ls_ref, bre_ref, bim_ref, abre_ref, abim_ref, bbre_ref, bbim_ref):
    lam_re = lre_ref[...]
    lam_im = lim_ref[...]
    dt = jnp.exp(ls_ref[...])
    mag = jnp.exp(lam_re * dt)
    ab_re = mag * jnp.cos(lam_im * dt)
    ab_im = mag * jnp.sin(lam_im * dt)
    num_re, num_im = ab_re - 1.0, ab_im
    den = lam_re * lam_re + lam_im * lam_im
    coef_re = (num_re * lam_re + num_im * lam_im) / den
    coef_im = (num_im * lam_re - num_re * lam_im) / den
    b_re, b_im = bre_ref[...], bim_ref[...]
    abre_ref[...] = ab_re
    abim_ref[...] = ab_im
    bbre_ref[...] = coef_re * b_re - coef_im * b_im
    bbim_ref[...] = coef_re * b_im + coef_im * b_re


def _s5_discretise(lam_re, lam_im, log_step, b_re, b_im):
    g, n, cg = b_re.shape
    exp = lambda t: jnp.repeat(t.reshape(2 * g, n), cg, axis=1)
    ls = jnp.broadcast_to(log_step.reshape(2 * g, 1), (2 * g, n * cg))
    bb = lambda t: jnp.tile(t.reshape(g, n * cg), (2, 1))
    shp = jax.ShapeDtypeStruct((2 * g, n * cg), f32)
    ab_re, ab_im, bb_re, bb_im = pl.pallas_call(
        _s5_disc_kernel, out_shape=[shp] * 4, name="s5_disc",
    )(exp(lam_re), exp(lam_im), ls, bb(b_re), bb(b_im))
    first = lambda t: t.reshape(2, g, n, cg)[..., 0]
    full = lambda t: t.reshape(2, g, n, cg)
    return first(ab_re), first(ab_im), full(bb_re), full(bb_im)


def _cmul(ar, ai, br, bi):
    return ar * br - ai * bi, ar * bi + ai * br


def _s5_kernel(n_lat, n_ctx, *refs):
    tc = S5_TC
    xl, xc, arow_ref, acol_ref, bbd_ref, cbd_ref, y_ref, wyz_s, ws_s, sl_s, sc_s = refs
    sw = S5_GBLK * S5_STATE
    nsl = sw // LANES
    nb = 8

    def powers(re, im, n):
        out = [(jnp.ones_like(re), jnp.zeros_like(im))]
        for _ in range(n):
            out.append(_cmul(out[-1][0], out[-1][1], re, im))
        return out

    prow = [powers(arow_ref[2 * d:2 * d + 1, :], arow_ref[2 * d + 1:2 * d + 2, :], tc) for d in range(2)]

    for s in range(tc):
        for d, k in ((0, tc - 1 - s), (1, s)):
            wr, wi = _cmul(bbd_ref[2 * d], bbd_ref[2 * d + 1], *prow[d][k])
            ws_s[s * LANES:(s + 1) * LANES, d * 2 * sw:d * 2 * sw + sw] = wr.astype(bf16)
            ws_s[s * LANES:(s + 1) * LANES, d * 2 * sw + sw:(d + 1) * 2 * sw] = wi.astype(bf16)
    crhs = jnp.concatenate([cbd_ref[0], -cbd_ref[1]], axis=0).astype(bf16)
    kall = [jnp.dot(ws_s[:, d * 2 * sw:(d + 1) * 2 * sw], crhs, preferred_element_type=f32) for d in range(2)]
    kf = [kall[0][(tc - 1 - k) * LANES:(tc - k) * LANES] for k in range(tc)]
    kb = [kall[1][k * LANES:(k + 1) * LANES] for k in range(tc)]
    for s in range(tc):
        for t in range(tc):
            blk = kf[t - s] if t > s else (kb[s - t] if t < s else kf[0] + kb[0])
            wyz_s[s * LANES:(s + 1) * LANES, t * LANES:(t + 1) * LANES] = blk.astype(bf16)
    base = tc * LANES
    for d in range(2):
        a1 = (acol_ref[2 * d], acol_ref[2 * d + 1])
        ak = a1
        for k in range(1, tc + 1):
            t = k - 1 if d == 0 else tc - k
            dre, dim_ = _cmul(cbd_ref[0], cbd_ref[1], *ak)
            r0 = base + d * 2 * sw
            wyz_s[r0:r0 + sw, t * LANES:(t + 1) * LANES] = dre.astype(bf16)
            wyz_s[r0 + sw:r0 + 2 * sw, t * LANES:(t + 1) * LANES] = (-dim_).astype(bf16)
            if k < tc:
                ak = _cmul(*ak, *a1)

    def rows_of(x_ref, b, n):
        return x_ref[b * n:(b + 1) * n, :]

    def inject(x_refs, s_ref, n):
        for b in range(nb):
            sb = jnp.dot(rows_of(x_refs, b, n), ws_s[...], preferred_element_type=f32)
            for k in range(4 * nsl):
                s_ref[k, pl.ds(b, n, stride=nb), :] = sb[:, k * LANES:(k + 1) * LANES]

    at = [[tuple(jnp.broadcast_to(p[:, k * LANES:(k + 1) * LANES], (nb, LANES)) for p in prow[d][tc])
           for k in range(nsl)] for d in range(2)]

    def scan(s_ref, n, init):
        def step(i, carry):
            new = []
            for d in range(2):
                idx = pl.ds(pl.multiple_of((i if d == 0 else n - 1 - i) * nb, nb), nb)
                for k in range(nsl):
                    hr, hi = carry[2 * (d * nsl + k)], carry[2 * (d * nsl + k) + 1]
                    kr, ki = d * 2 * nsl + k, d * 2 * nsl + nsl + k
                    sr, si = s_ref[kr, idx, :], s_ref[ki, idx, :]
                    s_ref[kr, idx, :] = hr
                    s_ref[ki, idx, :] = hi
                    ar, ai = at[d][k]
                    new += [ar * hr - ai * hi + sr, ar * hi + ai * hr + si]
            return tuple(new)
        return lax.fori_loop(0, n, step, init, unroll=4)

    inject(xc, sc_s, n_ctx)
    h_ctx = scan(sc_s, n_ctx, tuple(jnp.zeros((nb, LANES), f32) for _ in range(4 * nsl)))
    inject(xl, sl_s, n_lat)
    scan(sl_s, n_lat, h_ctx)

    for b in range(nb):
        hin = jnp.concatenate([sl_s[k, pl.ds(b, n_lat, stride=nb), :] for k in range(4 * nsl)],
                              axis=1).astype(bf16)
        yb = jnp.dot(jnp.concatenate([rows_of(xl, b, n_lat), hin], axis=1), wyz_s[...],
                     preferred_element_type=f32)
        for t in range(tc):
            y_ref[pl.ds(b * n_lat * tc + t, n_lat, stride=tc), :] = yb[:, t * LANES:(t + 1) * LANES]


def _s5_mix(u, u_c, arow, acol, bbd, cbd, bsz, seq_len, n_ctx_tok):
    assert bsz == 8, "the chunk recurrence puts the batch on the 8 sublanes"
    tc = S5_TC
    n_lat, n_ctx = seq_len // tc, n_ctx_tok // tc
    nblk = S5_GROUPS // S5_GBLK
    sw = S5_GBLK * S5_STATE
    nsl = sw // LANES
    one = pl.Buffered(1)
    xspec = lambda rows: pl.BlockSpec((None, rows, tc * LANES), lambda g: (g, 0, 0))
    par = lambda *shape: pl.BlockSpec((None,) + shape, lambda g: (g,) + (0,) * len(shape), pipeline_mode=one)
    return pl.pallas_call(
        functools.partial(_s5_kernel, n_lat, n_ctx),
        out_shape=jax.ShapeDtypeStruct((nblk, bsz * seq_len, LANES), f32),
        grid=(nblk,),
        in_specs=[xspec(bsz * n_lat), xspec(bsz * n_ctx)]
                 + [par(4, sw), par(4, sw, LANES), par(4, LANES, sw), par(2, sw, LANES)],
        out_specs=pl.BlockSpec((None, bsz * seq_len, LANES), lambda g: (g, 0, 0), pipeline_mode=one),
        scratch_shapes=[pltpu.VMEM((tc * LANES + 4 * sw, tc * LANES), bf16),
                        pltpu.VMEM((tc * LANES, 4 * sw), bf16),
                        pltpu.VMEM((4 * nsl, bsz * n_lat, LANES), f32),
                        pltpu.VMEM((4 * nsl, bsz * n_ctx, LANES), f32)],
        compiler_params=_cparams(("arbitrary",)),
        name="s5_mix",
    )(u, u_c, arow, acol, bbd, cbd)


def _s5_block_params(ab_re, ab_im, bb_re, bb_im, c_re, c_im):
    g, n, cg = S5_GROUPS, S5_STATE, S5_GROUP_CH
    nblk = g // S5_GBLK
    sw = S5_GBLK * n
    eye = jnp.eye(S5_GBLK, dtype=f32)
    arow = jnp.stack([t[d].reshape(nblk, sw) for d in range(2) for t in (ab_re, ab_im)], axis=1)
    acol = jnp.broadcast_to(arow[..., None], (nblk, 4, sw, LANES))
    blockdiag = lambda t: t[..., :, :, None, :] * eye[:, None, :, None]
    bb = jnp.stack([t[d] for d in range(2) for t in (bb_re, bb_im)], axis=0)
    bb = jnp.swapaxes(bb.reshape(4, nblk, S5_GBLK, n, cg), -1, -2)
    bbd = jnp.swapaxes(blockdiag(bb).reshape(4, nblk, S5_GBLK * cg, sw), 0, 1)
    cc = jnp.swapaxes(jnp.stack([c_re, c_im], axis=0).reshape(2, nblk, S5_GBLK, cg, n), -1, -2)
    cbd = jnp.swapaxes(blockdiag(cc).reshape(2, nblk, sw, S5_GBLK * cg), 0, 1)
    return arow, acol, bbd, cbd


def _s5_out_kernel(y_ref, u_ref, g_ref, x_ref, dskip_ref, gw_ref, gb_ref, w_ref, gate_ref,
                   fw_ref, o_ref):
    y = jnp.concatenate([y_ref[j] for j in range(y_ref.shape[0])], axis=1)
    y = y + dskip_ref[...] * u_ref[...].astype(f32)
    y = jax.nn.gelu(y)
    glu = jnp.dot(y.astype(bf16), gw_ref[...], preferred_element_type=f32) + gb_ref[...]
    y = y * jax.nn.sigmoid(glu)
    y = y * _silu(g_ref[...].astype(f32))
    x = x_ref[...] + gate_ref[...] * jnp.dot(y.astype(bf16), w_ref[...], preferred_element_type=f32)
    ms = jnp.mean(x * x, axis=-1, keepdims=True)
    o_ref[...] = x * lax.rsqrt(ms + EPS) * fw_ref[...]


def _s5_out(y, u, g, x, dskip, gw, gb, w, gate, fw, rows_per_mod):
    m, d = x.shape
    tm = min(ROW_TILE, rows_per_mod)
    per = rows_per_mod // tm
    row = lambda: pl.BlockSpec((tm, d), lambda i: (i, 0))
    vec = lambda: pl.BlockSpec((1, d), lambda i: (0, 0))
    mat = lambda: pl.BlockSpec((d, d), lambda i: (0, 0))
    return pl.pallas_call(
        _s5_out_kernel,
        out_shape=jax.ShapeDtypeStruct((m, d), f32),
        grid=(m // tm,),
        in_specs=[pl.BlockSpec((y.shape[0], tm, LANES), lambda i: (0, i, 0)),
                  row(), row(), row(), vec(), mat(), vec(), mat(),
                  pl.BlockSpec((None, 1, d), lambda i: (i // per, 0, 0)), vec()],
        out_specs=row(),
        compiler_params=_cparams(("parallel",)),
        name="s5_out",
    )(y, u, g, x, dskip, gw, gb, w, gate, fw)


def _even_weights(w_in):
    o = 0
    z = w_in[:, o:o + SSD_INNER]; o += SSD_INNER
    xbc = w_in[:, o:o + SSD_XBC]; o += SSD_XBC
    dt = w_in[:, o:o + 2 * SSD_HEADS]; o += 2 * SSD_HEADS
    q = w_in[:, o:o + ATT_Q]; o += ATT_Q
    k = w_in[:, o:o + ATT_KVW]; o += ATT_KVW
    v = w_in[:, o:o + ATT_KVW]; o += ATT_KVW
    g = w_in[:, o:o + ATT_Q]
    d = w_in.shape[0]
    dup = lambda t: jnp.concatenate([t.reshape(d, ATT_KV_HEADS, 1, ATT_HEAD_DIM)] * 2, axis=2).reshape(d, -1)
    dtp = jnp.pad(dt, ((0, 0), (0, LANES - 2 * SSD_HEADS)))
    cast = lambda t: t.astype(bf16)
    return jnp.concatenate([cast(t) for t in (z, xbc, q, dup(k), g, dtp)], axis=1), cast(dup(v).T)


def _even_segs(rope):
    scale = ATT_HEAD_DIM ** -0.5 * LOG2E
    widths = [(SSD_INNER, None, bf16), (SSD_XBC, None, bf16),
              (ATT_Q, scale, bf16),
              (ATT_KV_HEADS * LANES, 1.0 if rope else None, bf16),
              (ATT_Q, None, bf16), (LANES, None, f32)]
    segs, o = [], 0
    for w, r, dtp in widths:
        segs.append((o, w, r, dtp, False))
        o += w
    return segs


def _pad_lanes(v, n=LANES):
    v = v.reshape(1, -1)
    return jnp.pad(v, ((0, 0), (0, n - v.shape[1])))


def kernel(x, c, ctx, c_ctx, e_norm_w, e_ada_w, e_ada_b, e_w_in, e_conv_w, e_conv_b, e_dt_bias,
           e_a_log, e_d_skip, e_ssd_norm_w, e_sink, e_w_out, o_norm_w, o_ada_w, o_ada_b, o_w_in,
           o_lam_re, o_lam_im, o_log_step, o_b_re, o_b_im, o_c_re, o_c_im, o_d_skip, o_glu_w,
           o_glu_b, o_w_out, final_norm_w):
    bsz, seq_len, d = x.shape
    n_ctx = ctx.shape[1]
    xf = x.reshape(bsz * seq_len, d)
    xcf = ctx.reshape(bsz * n_ctx, d)

    cvecs = jnp.concatenate([c, c_ctx[None, :], jnp.zeros((16 - bsz - 1, d), f32)], axis=0)

    def modulation(ada_w, ada_b):
        mod = _adaln(cvecs, ada_w.astype(bf16), ada_b.reshape(1, -1))
        parts = [mod[:, k * d:(k + 1) * d] for k in range(3)]
        lat = [p[:bsz].reshape(bsz, 1, d) for p in parts]
        cx = [p[bsz:bsz + 1].reshape(1, 1, d) for p in parts]
        return lat, cx

    (shift, scale, gate), (shift_c, scale_c, gate_c) = modulation(e_ada_w[0], e_ada_b[0])
    w_in, w_vt = _even_weights(e_w_in[0])
    nw = e_norm_w[0].reshape(1, d)
    tabs = _rope_tables(seq_len)
    kw = ATT_KV_HEADS * LANES
    vseg = [(0, kw, bf16)]
    z, xbc, q, k, g, dt, vt = _inproj(xf, shift, scale, nw, w_in, _even_segs(True), seq_len, tabs, w_vt, vseg)
    z_c, xbc_c, q_c, k_c, g_c, dt_c, vt_c = _inproj(xcf, shift_c, scale_c, nw, w_in, _even_segs(False), n_ctx,
                                                    None, w_vt, vseg)

    conv_w = jnp.pad(e_conv_w[0], ((0, 8 - SSD_CONV), (0, 0)))
    conv_b = e_conv_b[0].reshape(1, -1)
    dtb = _pad_lanes(e_dt_bias[0])
    alog = _pad_lanes(e_a_log[0])
    dskip = jnp.repeat(e_d_skip[0], SSD_HEAD_DIM).reshape(1, -1)
    snw = e_ssd_norm_w[0].reshape(1, -1)
    h0 = jnp.zeros((bsz, 2, SSD_STATE, SSD_INNER), f32)
    ssd_c, hfin = _ssd(xbc_c, dt_c, z_c, conv_w, conv_b, dtb, alog, dskip, snw, h0, bsz, n_ctx)
    ssd_o, _ = _ssd(xbc, dt, z, conv_w, conv_b, dtb, alog, dskip, snw, hfin, bsz, seq_len)

    k3 = k.reshape(bsz, seq_len, kw)
    kc3 = k_c.reshape(bsz, n_ctx, kw)
    sink = e_sink[0]
    att = _attention(q, g, k3, vt, kc3, vt_c, sink, bsz, seq_len, True)
    att_c = _attention(q_c, g_c, kc3, vt_c, kc3, vt_c, sink, bsz, n_ctx, False)
    w_out = e_w_out[0].astype(bf16)
    x1 = _outproj(ssd_o, att, w_out, xf, gate, seq_len)
    xc1 = _outproj(ssd_c, att_c, w_out, xcf, gate_c, n_ctx)

    (shift, scale, gate), (shift_c, scale_c, _) = modulation(o_ada_w[0], o_ada_b[0])
    w_in = o_w_in[0].astype(bf16)
    nw = o_norm_w[0].reshape(1, d)
    u, u_ch, g2 = _inproj(x1, shift, scale, nw, w_in,
                          [(0, S5_WIDTH, None, bf16, False), (0, S5_WIDTH, None, bf16, True),
                           (S5_WIDTH, S5_WIDTH, None, bf16, False)], seq_len)
    (uc_ch,) = _inproj(xc1, shift_c, scale_c, nw, w_in[:, :S5_WIDTH], [(0, S5_WIDTH, None, bf16, True)], n_ctx)

    ab_re, ab_im, bb_re, bb_im = _s5_discretise(o_lam_re[0], o_lam_im[0], o_log_step[0], o_b_re[0], o_b_im[0])
    arow, acol, bbd, cbd = _s5_block_params(ab_re, ab_im, bb_re, bb_im, o_c_re[0], o_c_im[0])
    y = _s5_mix(u_ch, uc_ch, arow, acol, bbd, cbd, bsz, seq_len, n_ctx)
    out = _s5_out(y, u, g2, x1, o_d_skip[0].reshape(1, -1),
                  o_glu_w[0].astype(bf16), o_glu_b[0].reshape(1, -1), o_w_out[0].astype(bf16), gate,
                  final_norm_w.reshape(1, -1), seq_len)
    return out.reshape(bsz, seq_len, d)
```

```python
import functools
import math

import jax
import jax.numpy as jnp
import numpy as np
from jax import lax
from jax.experimental import pallas as pl
from jax.experimental.pallas import tpu as pltpu

f32 = jnp.float32
bf16 = jnp.bfloat16

D_MODEL = 1024
GRID_W = 64
EPS = 1e-6
NEG_INF = -1e30

SSD_HEADS = 16
SSD_HEAD_DIM = 64
SSD_GROUPS = 2
SSD_STATE = 128
SSD_CONV = 5
SSD_CHUNK = 128
SSD_INNER = SSD_HEADS * SSD_HEAD_DIM
SSD_BC = SSD_GROUPS * SSD_STATE
SSD_XBC = SSD_INNER + 2 * SSD_BC
ATT_HEADS = 16
ATT_KV_HEADS = 4
ATT_HEAD_DIM = 64
ATT_BLOCK = 128
ROPE_THETA = 10000.0
ATT_Q = ATT_HEADS * ATT_HEAD_DIM
ATT_KVW = ATT_KV_HEADS * ATT_HEAD_DIM
S5_WIDTH = 1024
S5_GROUP_CH = 16
S5_GROUPS = S5_WIDTH // S5_GROUP_CH
S5_STATE = 64

LOG2E = math.log2(math.e)
LANES = 128
ROW_TILE = 512
ATT_SUB = 2
S5_GBLK = 8
S5_TC = 8
VMEM_LIMIT = 56 * 1024 * 1024


def _cparams(sem, flags=None):
    return pltpu.CompilerParams(dimension_semantics=sem, vmem_limit_bytes=VMEM_LIMIT, flags=flags)


def _silu(x):
    h = 0.5 * x
    return h + h * jnp.tanh(h)


def _adaln_kernel(c_ref, w_ref, b_ref, o_ref):
    c = c_ref[...]
    s = _silu(c).astype(bf16)
    o_ref[...] = jnp.dot(s, w_ref[...], preferred_element_type=f32) + b_ref[...]


def _adaln(cvecs, w, b):
    r, d = cvecs.shape
    n = w.shape[1]
    tn = 1024
    return pl.pallas_call(
        _adaln_kernel,
        out_shape=jax.ShapeDtypeStruct((r, n), f32),
        grid=(n // tn,),
        in_specs=[pl.BlockSpec((r, d), lambda j: (0, 0)),
                  pl.BlockSpec((d, tn), lambda j: (0, j)),
                  pl.BlockSpec((1, tn), lambda j: (0, j))],
        out_specs=pl.BlockSpec((r, tn), lambda j: (0, j)),
        compiler_params=_cparams(("arbitrary",)),
        name="adaln",
    )(cvecs, w, b)


def _inproj_kernel(segs, tsegs, n_w, has_rope, x_ref, shift_ref, scale_ref, nw_ref, *rest):
    w_refs, rest = rest[:n_w], rest[n_w:]
    if tsegs:
        wt_ref, rest = rest[0], rest[1:]
    if has_rope:
        cos_ref, sina_ref, sinb_ref = rest[:3]
        rest = rest[3:]
    outs = rest[:len(segs)]
    touts = rest[len(segs):len(segs) + len(tsegs)]
    rest = rest[len(segs) + len(tsegs):]
    slab_s = rest[0] if rest else None
    x = x_ref[...]
    ms = jnp.mean(x * x, axis=-1, keepdims=True)
    h = (x * lax.rsqrt(ms + EPS)) * nw_ref[...]
    h = h * (1.0 + scale_ref[...]) + shift_ref[...]
    hb = h.astype(bf16)
    for (start, width, _), o_ref in zip(tsegs, touts):
        acc_t = lax.dot_general(wt_ref[start:start + width, :], hb, (((1,), (1,)), ((), ())),
                                preferred_element_type=f32)
        o_ref[...] = acc_t.astype(o_ref.dtype)
    products = {}
    for (widx, width, rope, _, chunked), o_ref in zip(segs, outs):
        if widx not in products:
            products[widx] = jnp.dot(hb, w_refs[widx][...], preferred_element_type=f32)
        acc = products[widx]
        if chunked:
            tm = acc.shape[0]
            for j in range(width // LANES):
                slab_s[j] = acc[:, j * LANES:(j + 1) * LANES]
            for s in range(S5_TC):
                for j in range(width // LANES):
                    o_ref[j, :, s * LANES:(s + 1) * LANES] = (
                        slab_s[j, pl.ds(s, tm // S5_TC, stride=S5_TC), :].astype(o_ref.dtype))
            continue
        if rope is not None and not has_rope:
            acc = acc * rope
        elif rope is not None:
            rep = width // LANES
            cos = jnp.concatenate([cos_ref[...]] * rep, axis=1) * rope
            sina = jnp.concatenate([sina_ref[...]] * rep, axis=1) * rope
            sinb = jnp.concatenate([sinb_ref[...]] * rep, axis=1) * rope
            half = ATT_HEAD_DIM // 2
            up = pltpu.roll(acc, width - half, axis=1)
            dn = pltpu.roll(acc, half, axis=1)
            acc = acc * cos + up * sina + dn * sinb
        o_ref[...] = acc.astype(o_ref.dtype)


def _inproj(x, shift, scale, nw, w, segs, rows_per_mod, rope_tabs=None, wt=None, tsegs=()):
    m, d = x.shape
    tm = min(ROW_TILE, rows_per_mod)
    per = rows_per_mod // tm
    nmod = shift.shape[0]
    mod_idx = (lambda i: (i // per, 0, 0)) if nmod > 1 else (lambda i: (0, 0, 0))
    in_specs = [pl.BlockSpec((tm, d), lambda i: (i, 0)),
                pl.BlockSpec((None, 1, d), mod_idx),
                pl.BlockSpec((None, 1, d), mod_idx),
                pl.BlockSpec((1, d), lambda i: (0, 0))]
    in_specs += [pl.BlockSpec(wi.shape, lambda i: (0, 0)) for wi in w]
    args = [x, shift, scale, nw, *w]
    if tsegs:
        in_specs.append(pl.BlockSpec(wt.shape, lambda i: (0, 0)))
        args.append(wt)
    if rope_tabs is not None:
        for t in rope_tabs:
            in_specs.append(pl.BlockSpec((tm, LANES), lambda i: (i % per, 0)))
            args.append(t)
    out_shape, out_specs = [], []
    for sg in segs:
        if sg[4]:
            out_shape.append(jax.ShapeDtypeStruct((sg[1] // LANES, m // S5_TC, S5_TC * LANES), sg[3]))
            out_specs.append(pl.BlockSpec((sg[1] // LANES, tm // S5_TC, S5_TC * LANES), lambda i: (0, i, 0)))
        else:
            out_shape.append(jax.ShapeDtypeStruct((m, sg[1]), sg[3]))
            out_specs.append(pl.BlockSpec((tm, sg[1]), lambda i: (i, 0)))
    for _, width, dtp in tsegs:
        out_shape.append(jax.ShapeDtypeStruct((width, m), dtp))
        out_specs.append(pl.BlockSpec((width, tm), lambda i: (0, i)))
    chunked_w = [sg[1] for sg in segs if sg[4]]
    scratch = [pltpu.VMEM((max(chunked_w) // LANES, tm, LANES), f32)] if chunked_w else []
    return pl.pallas_call(
        functools.partial(_inproj_kernel, tuple(segs), tuple(tsegs), len(w), rope_tabs is not None),
        out_shape=out_shape,
        grid=(m // tm,),
        in_specs=in_specs,
        out_specs=out_specs,
        scratch_shapes=scratch,
        compiler_params=_cparams(("parallel",)),
        name="inproj",
    )(*args)


def _rope_tables(seq_len):
    rows = seq_len // GRID_W
    row = jnp.repeat(jnp.arange(rows, dtype=f32), GRID_W)
    col = jnp.tile(jnp.arange(GRID_W, dtype=f32), rows)
    n_freq = ATT_HEAD_DIM // 4
    inv = ROPE_THETA ** (-jnp.arange(n_freq, dtype=f32) / n_freq)
    ang = jnp.concatenate([row[:, None] * inv, col[:, None] * inv], axis=-1)
    cos, sin = jnp.cos(ang), jnp.sin(ang)
    zero = jnp.zeros_like(sin)
    cos_h = jnp.concatenate([cos, cos], axis=-1)
    sina_h = jnp.concatenate([-sin, zero], axis=-1)
    sinb_h = jnp.concatenate([zero, sin], axis=-1)
    two = lambda t: jnp.concatenate([t, t], axis=-1)
    return two(cos_h), two(sina_h), two(sinb_h)


SSD_PACK = 32


def _split3(x):
    hi = x.astype(bf16)
    r1 = x - hi.astype(f32)
    mid = r1.astype(bf16)
    lo = (r1 - mid.astype(f32)).astype(bf16)
    return hi, mid, lo


def _pack3(x):
    hi, mid, lo = _split3(x)
    lane = lax.broadcasted_iota(jnp.int32, x.shape, 1)
    mid_r = pltpu.roll(mid.astype(f32), SSD_PACK, axis=1)
    lo_r = pltpu.roll(lo.astype(f32), 2 * SSD_PACK, axis=1)
    packed = jnp.where(lane < SSD_PACK, hi.astype(f32),
                       jnp.where(lane < 2 * SSD_PACK, mid_r,
                                 jnp.where(lane < 3 * SSD_PACK, lo_r, 0.0)))
    return packed.astype(bf16)


def _ssd_selectors():
    k = np.arange(LANES)
    src = np.where(k < 3 * SSD_PACK, k % SSD_PACK, -1)
    col_blk = np.arange(SSD_PACK * SSD_CHUNK) // SSD_CHUNK
    sel_bc = (src[:, None] == col_blk[None, :])
    head = np.arange(SSD_INNER) // SSD_HEAD_DIM
    sel_f = (src[:, None] == head[None, :])
    sel_b = (src[:, None] == (head + SSD_HEADS)[None, :])
    tri3 = np.tile(np.tril(np.ones((SSD_CHUNK, SSD_CHUNK))), (1, 3))
    rows = np.arange(SSD_CHUNK)[:, None]
    cols = np.arange(SSD_CHUNK + 32)[None, :]
    half = SSD_CONV // 2
    shift = np.concatenate([cols == rows + 16 + d for d in range(-half, half + 1) if d != 0], axis=0)
    as_bf = lambda a: jnp.asarray(a.astype(np.float32), dtype=bf16)
    return as_bf(sel_bc), as_bf(sel_f), as_bf(sel_b), as_bf(tri3), as_bf(shift)


def _ssd_kernel(seq_len, xbc_ref, dt_ref, z_ref, cw_ref, cb_ref, dtb_ref, alog_ref, dskip_ref, nw_ref,
                selbc_ref, self_ref, selb_ref, tri3_ref, shift_ref, h0_ref, out_ref, hfin_ref,
                xs_s, bc_s, dt_s, y_s, hf_s, hb_s, win_s):
    q = SSD_CHUNK
    nc = seq_len // q
    halo = 16
    H, P, N = SSD_HEADS, SSD_HEAD_DIM, SSD_STATE
    gw = (H // SSD_GROUPS) * P
    a2_row = -jnp.exp(alog_ref[...]) * math.log2(math.e)

    def conv_chunk(c):
        r0 = pl.multiple_of(c * q, q)
        pstart = pl.multiple_of(jnp.maximum(r0 - halo, 0), halo)
        nstart = pl.multiple_of(jnp.minimum(r0 + q, seq_len - halo), halo)
        zero = jnp.zeros((), bf16)
        win_s[0:halo, :] = jnp.where(c > 0, xbc_ref[pl.ds(pstart, halo), :], zero)
        win_s[halo:halo + q, :] = xbc_ref[pl.ds(r0, q), :]
        win_s[halo + q:, :] = jnp.where(c < nc - 1, xbc_ref[pl.ds(nstart, halo), :], zero)
        taps = [k for k in range(SSD_CONV) if k != SSD_CONV // 2]
        cw = 2 * LANES
        for j in range(SSD_XBC // cw):
            cs = slice(j * cw, (j + 1) * cw)
            sh = jnp.dot(shift_ref[...], win_s[:, cs], preferred_element_type=f32)
            acc = cb_ref[:, cs] + win_s[halo:halo + q, cs].astype(f32) * cw_ref[SSD_CONV // 2:SSD_CONV // 2 + 1, cs]
            for n, k in enumerate(taps):
                acc = acc + sh[n * q:(n + 1) * q, :] * cw_ref[k:k + 1, cs]
            act = _silu(acc)
            if j < SSD_INNER // cw:
                xs_s[pl.ds(r0, q), cs] = act
            else:
                bc_s[pl.ds(r0, q), j * cw - SSD_INNER:(j + 1) * cw - SSD_INNER] = act.astype(bf16)
        dt_s[pl.ds(r0, q), :] = jax.nn.softplus(dt_ref[pl.ds(r0, q), :] + dtb_ref[...])

    conv_chunk(0)
    hf_s[...] = h0_ref[0]
    hb_s[...] = h0_ref[1]

    ri = lax.broadcasted_iota(jnp.int32, (q, q), 0)
    ci = lax.broadcasted_iota(jnp.int32, (q, q), 1)
    lower = ri >= ci
    upper = ci >= ri
    lane = lax.broadcasted_iota(jnp.int32, (q, LANES), 1)
    lo_half = lane < P

    def cumsums(dt):
        dta = dt * a2_row
        cf = jnp.dot(tri3_ref[...], jnp.concatenate(_split3(dta), axis=0), preferred_element_type=f32)
        rb = cf[q - 1:q, :] - cf + dta
        return cf, rb

    def load_chunk(r0):
        dt = dt_s[pl.ds(r0, q), :]
        xs = xs_s[pl.ds(r0, q), :]
        bcv = bc_s[pl.ds(r0, q), :]
        bmat = [bcv[:, g * N:(g + 1) * N] for g in range(SSD_GROUPS)]
        cmat = [bcv[:, SSD_BC + g * N:SSD_BC + (g + 1) * N] for g in range(SSD_GROUPS)]
        return dt, xs, bmat, cmat

    def inter_chunk(h_s, sel_ref, decay, weight, xs, bmat, cmat, dec_idx):
        ew = jnp.dot(jnp.concatenate([_pack3(decay), _pack3(weight)], axis=0), sel_ref[...],
                     preferred_element_type=f32)
        e_x, w_x = ew[:q], ew[q:]
        hb_ = h_s[...].astype(bf16)
        yoff = jnp.concatenate(
            [jnp.dot(cmat[g], hb_[:, g * gw:(g + 1) * gw], preferred_element_type=f32)
             for g in range(SSD_GROUPS)], axis=1)
        xw = (xs * w_x).astype(bf16)
        dec_row = e_x[dec_idx:dec_idx + 1, :]
        for g in range(SSD_GROUPS):
            gs = slice(g * gw, (g + 1) * gw)
            bt = jnp.transpose(bmat[g].astype(f32)).astype(bf16)
            upd = jnp.dot(bt, xw[:, gs], preferred_element_type=f32)
            h_s[:, gs] = h_s[:, gs] * dec_row[:, gs] + upd
        return yoff * e_x

    def finish(r0, y, xs):
        yy = y + xs * dskip_ref[...]
        zz = z_ref[pl.ds(r0, q), :].astype(f32)
        gated = yy * _silu(zz)
        ms = jnp.mean(gated * gated, axis=-1, keepdims=True)
        out_ref[pl.ds(r0, q), :] = (gated * lax.rsqrt(ms + EPS) * nw_ref[...]).astype(out_ref.dtype)

    def fwd_chunk(c, second_half):
        r0 = pl.multiple_of(c * q, q)
        dt, xs, bmat, cmat = load_chunk(r0)
        cf, rb = cumsums(dt)
        pcol = jnp.where(lane < H, cf, rb)
        bcast = jnp.dot(_pack3(pcol), selbc_ref[...], preferred_element_type=f32)
        prow = jnp.transpose(pcol - jnp.log2(dt))
        cbm = [lax.dot_general(cmat[g], bmat[g], (((1,), (1,)), ((), ())), preferred_element_type=f32)
               for g in range(SSD_GROUPS)]
        xsb = xs.astype(bf16)
        zero_b = jnp.zeros((), bf16)
        ypairs = []
        for k in range(H // 2):
            mats = []
            for h in (2 * k, 2 * k + 1):
                g = h // (H // SSD_GROUPS)
                hb_ = H + h
                segf = bcast[:, h * q:(h + 1) * q] - prow[h:h + 1, :]
                segb = bcast[:, hb_ * q:(hb_ + 1) * q] - prow[hb_:hb_ + 1, :]
                df = jnp.exp2(jnp.where(lower, segf, NEG_INF))
                db = jnp.exp2(jnp.where(upper, segb, NEG_INF))
                mats.append((cbm[g] * (df + db)).astype(bf16))
            xp = xsb[:, k * LANES:(k + 1) * LANES]
            xbd = jnp.concatenate([jnp.where(lo_half, xp, zero_b), jnp.where(lo_half, zero_b, xp)], axis=0)
            ypairs.append(jnp.dot(jnp.concatenate(mats, axis=1), xbd, preferred_element_type=f32))
        y = jnp.concatenate(ypairs, axis=1)
        wfa = jnp.exp2(cf[q - 1:q, :] - cf) * dt
        y = y + inter_chunk(hf_s, self_ref, jnp.exp2(cf), wfa, xs, bmat, cmat, q - 1)
        if second_half:
            finish(r0, y_s[pl.ds(r0, q), :] + y, xs)
        else:
            y_s[pl.ds(r0, q), :] = y

    def bwd_chunk(c, second_half):
        r0 = pl.multiple_of(c * q, q)
        dt, xs, bmat, cmat = load_chunk(r0)
        _, rb = cumsums(dt)
        wba = jnp.exp2(rb[0:1, :] - rb) * dt
        y = inter_chunk(hb_s, selb_ref, jnp.exp2(rb), wba, xs, bmat, cmat, 0)
        if second_half:
            finish(r0, y_s[pl.ds(r0, q), :] + y, xs)
        else:
            y_s[pl.ds(r0, q), :] = y

    half = nc // 2
    conv_chunk(nc - 1)

    def first_half(i, carry):
        fwd_chunk(i, False)
        bwd_chunk(nc - 1 - i, False)
        conv_chunk(i + 1)
        conv_chunk(nc - 2 - i)
        return carry

    def second_half(i, carry):
        fwd_chunk(i, True)
        bwd_chunk(nc - 1 - i, True)
        return carry

    lax.fori_loop(0, half - 1, first_half, 0)
    fwd_chunk(jnp.int32(half - 1), False)
    bwd_chunk(jnp.int32(half), False)
    lax.fori_loop(half, nc, second_half, 0)
    hfin_ref[0] = hf_s[...]
    hfin_ref[1] = hb_s[...]


def _ssd(xbc, dt, z, conv_w, conv_b, dtb, alog, dskip, nw, h0, bsz, seq_len):
    assert seq_len % (2 * SSD_CHUNK) == 0, "the two recurrences meet in the middle: even chunk count"
    one = pl.Buffered(1)
    seq = lambda w: pl.BlockSpec((seq_len, w), lambda b: (b, 0), pipeline_mode=one)
    const = lambda r, w: pl.BlockSpec((r, w), lambda b: (0, 0))
    st = pl.BlockSpec((None, 2, SSD_STATE, SSD_INNER), lambda b: (b, 0, 0, 0))
    sels = _ssd_selectors()
    return pl.pallas_call(
        functools.partial(_ssd_kernel, seq_len),
        out_shape=[jax.ShapeDtypeStruct((bsz * seq_len, SSD_INNER), bf16),
                   jax.ShapeDtypeStruct((bsz, 2, SSD_STATE, SSD_INNER), f32)],
        grid=(bsz,),
        in_specs=[seq(SSD_XBC), seq(LANES), seq(SSD_INNER),
                  const(8, SSD_XBC), const(1, SSD_XBC), const(1, LANES), const(1, LANES),
                  const(1, SSD_INNER), const(1, SSD_INNER)]
                 + [const(*s.shape) for s in sels] + [st],
        out_specs=[seq(SSD_INNER), st],
        scratch_shapes=[pltpu.VMEM((seq_len, SSD_INNER), f32),
                        pltpu.VMEM((seq_len, 2 * SSD_BC), bf16),
                        pltpu.VMEM((seq_len, LANES), f32),
                        pltpu.VMEM((seq_len, SSD_INNER), f32),
                        pltpu.VMEM((SSD_STATE, SSD_INNER), f32),
                        pltpu.VMEM((SSD_STATE, SSD_INNER), f32),
                        pltpu.VMEM((SSD_CHUNK + 32, SSD_XBC), bf16)],
        compiler_params=_cparams(("parallel",)),
        name="ssd",
    )(xbc, dt, z, conv_w, conv_b, dtb, alog, dskip, nw, *sels, h0)


def _attn_kernel(n_blocks, nsub, local, q_ref, g_ref, k_ref, kc_ref, *rest):
    nv = nsub + 2 if local else 0
    v_refs = rest[:nv]
    vc_ref, sink_ref, o_ref, s_s = rest[nv:]
    t = ATT_BLOCK
    i0 = pl.program_id(1) * nsub
    rpk = ATT_HEADS // ATT_KV_HEADS
    lane = lax.broadcasted_iota(jnp.int32, (t, LANES), 1)
    lo_half = lane < ATT_HEAD_DIM
    zero_b = jnp.zeros((), bf16)
    kl = lax.broadcasted_iota(jnp.int32, (t, t), 0)
    ql = lax.broadcasted_iota(jnp.int32, (t, t), 1)
    for sub in range(nsub):
        i = i0 + sub
        qv = q_ref[sub * t:(sub + 1) * t, :]
        if local:
            p0 = pl.multiple_of(jnp.maximum(i - 1, 0) * t, t)
            c0 = pl.multiple_of(i * t, t)
            n0 = pl.multiple_of(jnp.minimum(i + 1, n_blocks - 1) * t, t)
            bias_prev = jnp.where((kl >= ql) & (i > 0), 0.0, NEG_INF)
            bias_next = jnp.where((kl <= ql) & (i < n_blocks - 1), 0.0, NEG_INF)
            bias_prev = jnp.concatenate([bias_prev] * rpk, axis=1)
            bias_next = jnp.concatenate([bias_next] * rpk, axis=1)
        for j in range(ATT_KV_HEADS):
            ls = slice(j * LANES, (j + 1) * LANES)
            if local:
                kk = jnp.concatenate([k_ref[pl.ds(p0, t), ls], k_ref[pl.ds(c0, t), ls],
                                      k_ref[pl.ds(n0, t), ls], kc_ref[:, ls]], axis=0)
            else:
                kk = kc_ref[:, ls]
            pieces = []
            for r in range(rpk):
                hq = j * rpk + r
                qp = qv[:, (hq // 2) * LANES:(hq // 2 + 1) * LANES]
                keep = lo_half if hq % 2 == 0 else jnp.logical_not(lo_half)
                pieces.append(jnp.where(keep, qp, zero_b))
            q4 = jnp.concatenate(pieces, axis=0)
            s = lax.dot_general(kk, q4, (((1,), (1,)), ((), ())), preferred_element_type=f32)
            if local:
                s = jnp.concatenate([s[:t] + bias_prev, s[t:2 * t], s[2 * t:3 * t] + bias_next, s[3 * t:]],
                                    axis=0)
            s_s[sub, j] = s
    for sub in range(nsub):
        outs = []
        for j in range(ATT_KV_HEADS):
            ls = slice(j * LANES, (j + 1) * LANES)
            if local:
                vvt = jnp.concatenate([v_refs[sub + k][ls, :] for k in range(3)] + [vc_ref[ls, :]], axis=1)
            else:
                vvt = vc_ref[ls, :]
            sk = jnp.concatenate([jnp.full((1, t), sink_ref[j * rpk + r] * LOG2E, f32) for r in range(rpk)],
                                 axis=1)
            s = s_s[sub, j]
            m = jnp.maximum(jnp.max(s, axis=0, keepdims=True), sk)
            p = jnp.exp2(s - m)
            den = jnp.sum(p, axis=0, keepdims=True) + jnp.exp2(sk - m)
            ot = jnp.dot(vvt, p.astype(bf16), preferred_element_type=f32) / den
            o4 = [jnp.transpose(ot[:, r * t:(r + 1) * t]) for r in range(rpk)]
            outs.append(jnp.where(lo_half, o4[0], o4[1]))
            outs.append(jnp.where(lo_half, o4[2], o4[3]))
        o = jnp.concatenate(outs, axis=1)
        gv = g_ref[sub * t:(sub + 1) * t, :].astype(f32)
        o_ref[sub * t:(sub + 1) * t, :] = (o * _silu(gv)).astype(o_ref.dtype)


def _attention(q, g, k, vt, kc, vct, sink, bsz, seq_len, local):
    t = ATT_BLOCK
    nb = seq_len // t
    nsub = ATT_SUB if nb % ATT_SUB == 0 else 1
    n_ctx = kc.shape[1]
    kw = ATT_KV_HEADS * LANES
    blk = pl.BlockSpec((nsub * t, ATT_Q), lambda b, i: (b * (nb // nsub) + i, 0))
    full = lambda n: pl.BlockSpec((None, n, kw), lambda b, i: (b, 0, 0))
    vblk = lambda off: pl.BlockSpec((kw, t), lambda b, i: (0, b * nb + jnp.clip(i * nsub + off, 0, nb - 1)))
    vspecs = [vblk(off) for off in range(-1, nsub + 1)] if local else []
    return pl.pallas_call(
        functools.partial(_attn_kernel, nb, nsub, local),
        out_shape=jax.ShapeDtypeStruct((bsz * seq_len, ATT_Q), bf16),
        grid=(bsz, nb // nsub),
        in_specs=[blk, blk, full(k.shape[1]), full(n_ctx)] + vspecs
                 + [pl.BlockSpec((kw, n_ctx), lambda b, i: (0, b)), pl.BlockSpec(memory_space=pltpu.SMEM)],
        out_specs=blk,
        scratch_shapes=[pltpu.VMEM((nsub, ATT_KV_HEADS, (3 * t if local else 0) + n_ctx,
                                    (ATT_HEADS // ATT_KV_HEADS) * t), f32)],
        compiler_params=_cparams(("parallel", "arbitrary")),
        name="attention",
    )(q, g, k, kc, *([vt] * len(vspecs)), vct, sink)


def _outproj_kernel(a_ref, b_ref, w_ref, x_ref, gate_ref, o_ref):
    ka = a_ref.shape[1]
    acc = jnp.dot(a_ref[...], w_ref[:ka, :], preferred_element_type=f32)
    acc = acc + jnp.dot(b_ref[...], w_ref[ka:, :], preferred_element_type=f32)
    o_ref[...] = x_ref[...] + gate_ref[...] * acc


def _outproj(a, b, w, x, gate, rows_per_mod):
    m, d = x.shape
    tm = min(ROW_TILE, rows_per_mod)
    per = rows_per_mod // tm
    nmod = gate.shape[0]
    mod_idx = (lambda i: (i // per, 0, 0)) if nmod > 1 else (lambda i: (0, 0, 0))
    row = lambda w_: pl.BlockSpec((tm, w_), lambda i: (i, 0))
    return pl.pallas_call(
        _outproj_kernel,
        out_shape=jax.ShapeDtypeStruct((m, d), f32),
        grid=(m // tm,),
        in_specs=[row(a.shape[1]), row(b.shape[1]),
                  pl.BlockSpec(w.shape, lambda i: (0, 0)), row(d),
                  pl.BlockSpec((None, 1, d), mod_idx)],
        out_specs=row(d),
        compiler_params=_cparams(("parallel",)),
        name="outproj",
    )(a, b, w, x, gate)


def _s5_disc_kernel(lre_ref, lim_ref, ls_ref, bre_ref, bim_ref, abre_ref, abim_ref, bbre_ref, bbim_ref):
    lam_re = lre_ref[...]
    lam_im = lim_ref[...]
    dt = jnp.exp(ls_ref[...])
    mag = jnp.exp(lam_re * dt)
    ab_re = mag * jnp.cos(lam_im * dt)
    ab_im = mag * jnp.sin(lam_im * dt)
    num_re, num_im = ab_re - 1.0, ab_im
    den = lam_re * lam_re + lam_im * lam_im
    coef_re = (num_re * lam_re + num_im * lam_im) / den
    coef_im = (num_im * lam_re - num_re * lam_im) / den
    b_re, b_im = bre_ref[...], bim_ref[...]
    abre_ref[...] = ab_re
    abim_ref[...] = ab_im
    bbre_ref[...] = coef_re * b_re - coef_im * b_im
    bbim_ref[...] = coef_re * b_im + coef_im * b_re


def _s5_discretise(lam_re, lam_im, log_step, b_re, b_im):
    g, n, cg = b_re.shape
    exp = lambda t: jnp.repeat(t.reshape(2 * g, n), cg, axis=1)
    ls = jnp.broadcast_to(log_step.reshape(2 * g, 1), (2 * g, n * cg))
    bb = lambda t: jnp.tile(t.reshape(g, n * cg), (2, 1))
    shp = jax.ShapeDtypeStruct((2 * g, n * cg), f32)
    ab_re, ab_im, bb_re, bb_im = pl.pallas_call(
        _s5_disc_kernel, out_shape=[shp] * 4, name="s5_disc",
    )(exp(lam_re), exp(lam_im), ls, bb(b_re), bb(b_im))
    first = lambda t: t.reshape(2, g, n, cg)[..., 0]
    full = lambda t: t.reshape(2, g, n, cg)
    return first(ab_re), first(ab_im), full(bb_re), full(bb_im)


def _cmul(ar, ai, br, bi):
    return ar * br - ai * bi, ar * bi + ai * br


def _s5_kernel(n_lat, n_ctx, *refs):
    tc = S5_TC
    xl, xc, arow_ref, acol_ref, bbd_ref, cbd_ref, y_ref, wyz_s, ws_s, sl_s, sc_s = refs
    sw = S5_GBLK * S5_STATE
    nsl = sw // LANES
    nb = 8

    def powers(re, im, n):
        out = [(jnp.ones_like(re), jnp.zeros_like(im))]
        for _ in range(n):
            out.append(_cmul(out[-1][0], out[-1][1], re, im))
        return out

    prow = [powers(arow_ref[2 * d:2 * d + 1, :], arow_ref[2 * d + 1:2 * d + 2, :], tc) for d in range(2)]

    for s in range(tc):
        for d, k in ((0, tc - 1 - s), (1, s)):
            wr, wi = _cmul(bbd_ref[2 * d], bbd_ref[2 * d + 1], *prow[d][k])
            ws_s[s * LANES:(s + 1) * LANES, d * 2 * sw:d * 2 * sw + sw] = wr.astype(bf16)
            ws_s[s * LANES:(s + 1) * LANES, d * 2 * sw + sw:(d + 1) * 2 * sw] = wi.astype(bf16)
    crhs = jnp.concatenate([cbd_ref[0], -cbd_ref[1]], axis=0).astype(bf16)
    kall = [jnp.dot(ws_s[:, d * 2 * sw:(d + 1) * 2 * sw], crhs, preferred_element_type=f32) for d in range(2)]
    kf = [kall[0][(tc - 1 - k) * LANES:(tc - k) * LANES] for k in range(tc)]
    kb = [kall[1][k * LANES:(k + 1) * LANES] for k in range(tc)]
    for s in range(tc):
        for t in range(tc):
            blk = kf[t - s] if t > s else (kb[s - t] if t < s else kf[0] + kb[0])
            wyz_s[s * LANES:(s + 1) * LANES, t * LANES:(t + 1) * LANES] = blk.astype(bf16)
    base = tc * LANES
    for d in range(2):
        a1 = (acol_ref[2 * d], acol_ref[2 * d + 1])
        ak = a1
        for k in range(1, tc + 1):
            t = k - 1 if d == 0 else tc - k
            dre, dim_ = _cmul(cbd_ref[0], cbd_ref[1], *ak)
            r0 = base + d * 2 * sw
            wyz_s[r0:r0 + sw, t * LANES:(t + 1) * LANES] = dre.astype(bf16)
            wyz_s[r0 + sw:r0 + 2 * sw, t * LANES:(t + 1) * LANES] = (-dim_).astype(bf16)
            if k < tc:
                ak = _cmul(*ak, *a1)

    def rows_of(x_ref, b, n):
        return x_ref[b * n:(b + 1) * n, :]

    def inject(x_refs, s_ref, n):
        for b in range(nb):
            sb = jnp.dot(rows_of(x_refs, b, n), ws_s[...], preferred_element_type=f32)
            for k in range(4 * nsl):
                s_ref[k, pl.ds(b, n, stride=nb), :] = sb[:, k * LANES:(k + 1) * LANES]

    at = [[tuple(jnp.broadcast_to(p[:, k * LANES:(k + 1) * LANES], (nb, LANES)) for p in prow[d][tc])
           for k in range(nsl)] for d in range(2)]

    def scan(s_ref, n, init):
        def step(i, carry):
            new = []
            for d in range(2):
                idx = pl.ds(pl.multiple_of((i if d == 0 else n - 1 - i) * nb, nb), nb)
                for k in range(nsl):
                    hr, hi = carry[2 * (d * nsl + k)], carry[2 * (d * nsl + k) + 1]
                    kr, ki = d * 2 * nsl + k, d * 2 * nsl + nsl + k
                    sr, si = s_ref[kr, idx, :], s_ref[ki, idx, :]
                    s_ref[kr, idx, :] = hr
                    s_ref[ki, idx, :] = hi
                    ar, ai = at[d][k]
                    new += [ar * hr - ai * hi + sr, ar * hi + ai * hr + si]
            return tuple(new)
        return lax.fori_loop(0, n, step, init, unroll=4)

    inject(xc, sc_s, n_ctx)
    h_ctx = scan(sc_s, n_ctx, tuple(jnp.zeros((nb, LANES), f32) for _ in range(4 * nsl)))
    inject(xl, sl_s, n_lat)
    scan(sl_s, n_lat, h_ctx)

    for b in range(nb):
        hin = jnp.concatenate([sl_s[k, pl.ds(b, n_lat, stride=nb), :] for k in range(4 * nsl)],
                              axis=1).astype(bf16)
        yb = jnp.dot(jnp.concatenate([rows_of(xl, b, n_lat), hin], axis=1), wyz_s[...],
                     preferred_element_type=f32)
        for t in range(tc):
            y_ref[pl.ds(b * n_lat * tc + t, n_lat, stride=tc), :] = yb[:, t * LANES:(t + 1) * LANES]


def _s5_mix(u, u_c, arow, acol, bbd, cbd, bsz, seq_len, n_ctx_tok):
    assert bsz == 8, "the chunk recurrence puts the batch on the 8 sublanes"
    tc = S5_TC
    n_lat, n_ctx = seq_len // tc, n_ctx_tok // tc
    nblk = S5_GROUPS // S5_GBLK
    sw = S5_GBLK * S5_STATE
    nsl = sw // LANES
    one = pl.Buffered(1)
    xspec = lambda rows: pl.BlockSpec((None, rows, tc * LANES), lambda g: (g, 0, 0))
    par = lambda *shape: pl.BlockSpec((None,) + shape, lambda g: (g,) + (0,) * len(shape), pipeline_mode=one)
    return pl.pallas_call(
        functools.partial(_s5_kernel, n_lat, n_ctx),
        out_shape=jax.ShapeDtypeStruct((nblk, bsz * seq_len, LANES), f32),
        grid=(nblk,),
        in_specs=[xspec(bsz * n_lat), xspec(bsz * n_ctx)]
                 + [par(4, sw), par(4, sw, LANES), par(4, LANES, sw), par(2, sw, LANES)],
        out_specs=pl.BlockSpec((None, bsz * seq_len, LANES), lambda g: (g, 0, 0), pipeline_mode=one),
        scratch_shapes=[pltpu.VMEM((tc * LANES + 4 * sw, tc * LANES), bf16),
                        pltpu.VMEM((tc * LANES, 4 * sw), bf16),
                        pltpu.VMEM((4 * nsl, bsz * n_lat, LANES), f32),
                        pltpu.VMEM((4 * nsl, bsz * n_ctx, LANES), f32)],
        compiler_params=_cparams(("arbitrary",)),
        name="s5_mix",
    )(u, u_c, arow, acol, bbd, cbd)


def _s5_block_params(ab_re, ab_im, bb_re, bb_im, c_re, c_im):
    g, n, cg = S5_GROUPS, S5_STATE, S5_GROUP_CH
    nblk = g // S5_GBLK
    sw = S5_GBLK * n
    eye = jnp.eye(S5_GBLK, dtype=f32)
    arow = jnp.stack([t[d].reshape(nblk, sw) for d in range(2) for t in (ab_re, ab_im)], axis=1)
    acol = jnp.broadcast_to(arow[..., None], (nblk, 4, sw, LANES))
    blockdiag = lambda t: t[..., :, :, None, :] * eye[:, None, :, None]
    bb = jnp.stack([t[d] for d in range(2) for t in (bb_re, bb_im)], axis=0)
    bb = jnp.swapaxes(bb.reshape(4, nblk, S5_GBLK, n, cg), -1, -2)
    bbd = jnp.swapaxes(blockdiag(bb).reshape(4, nblk, S5_GBLK * cg, sw), 0, 1)
    cc = jnp.swapaxes(jnp.stack([c_re, c_im], axis=0).reshape(2, nblk, S5_GBLK, cg, n), -1, -2)
    cbd = jnp.swapaxes(blockdiag(cc).reshape(2, nblk, sw, S5_GBLK * cg), 0, 1)
    return arow, acol, bbd, cbd


def _s5_out_kernel(y_ref, u_ref, g_ref, x_ref, dskip_ref, gw_ref, gb_ref, w_ref, gate_ref,
                   fw_ref, o_ref):
    y = jnp.concatenate([y_ref[j] for j in range(y_ref.shape[0])], axis=1)
    y = y + dskip_ref[...] * u_ref[...].astype(f32)
    y = jax.nn.gelu(y)
    glu = jnp.dot(y.astype(bf16), gw_ref[...], preferred_element_type=f32) + gb_ref[...]
    y = y * jax.nn.sigmoid(glu)
    y = y * _silu(g_ref[...].astype(f32))
    x = x_ref[...] + gate_ref[...] * jnp.dot(y.astype(bf16), w_ref[...], preferred_element_type=f32)
    ms = jnp.mean(x * x, axis=-1, keepdims=True)
    o_ref[...] = x * lax.rsqrt(ms + EPS) * fw_ref[...]


def _s5_out(y, u, g, x, dskip, gw, gb, w, gate, fw, rows_per_mod):
    m, d = x.shape
    tm = min(ROW_TILE, rows_per_mod)
    per = rows_per_mod // tm
    row = lambda: pl.BlockSpec((tm, d), lambda i: (i, 0))
    vec = lambda: pl.BlockSpec((1, d), lambda i: (0, 0))
    mat = lambda: pl.BlockSpec((d, d), lambda i: (0, 0))
    return pl.pallas_call(
        _s5_out_kernel,
        out_shape=jax.ShapeDtypeStruct((m, d), f32),
        grid=(m // tm,),
        in_specs=[pl.BlockSpec((y.shape[0], tm, LANES), lambda i: (0, i, 0)),
                  row(), row(), row(), vec(), mat(), vec(), mat(),
                  pl.BlockSpec((None, 1, d), lambda i: (i // per, 0, 0)), vec()],
        out_specs=row(),
        compiler_params=_cparams(("parallel",)),
        name="s5_out",
    )(y, u, g, x, dskip, gw, gb, w, gate, fw)


def _even_weights(w_in):
    o = 0
    z = w_in[:, o:o + SSD_INNER]; o += SSD_INNER
    xbc = w_in[:, o:o + SSD_XBC]; o += SSD_XBC
    dt = w_in[:, o:o + 2 * SSD_HEADS]; o += 2 * SSD_HEADS
    q = w_in[:, o:o + ATT_Q]; o += ATT_Q
    k = w_in[:, o:o + ATT_KVW]; o += ATT_KVW
    v = w_in[:, o:o + ATT_KVW]; o += ATT_KVW
    g = w_in[:, o:o + ATT_Q]
    d = w_in.shape[0]
    dup = lambda t: jnp.concatenate([t.reshape(d, ATT_KV_HEADS, 1, ATT_HEAD_DIM)] * 2, axis=2).reshape(d, -1)
    dtp = jnp.pad(dt, ((0, 0), (0, LANES - 2 * SSD_HEADS)))
    cast = lambda t: t.astype(bf16)
    return [cast(t) for t in (z, xbc, q, dup(k), g, dtp)], cast(dup(v).T)


def _even_segs(rope):
    scale = ATT_HEAD_DIM ** -0.5 * LOG2E
    widths = [(SSD_INNER, None, bf16), (SSD_XBC, None, bf16),
              (ATT_Q, scale, bf16),
              (ATT_KV_HEADS * LANES, 1.0 if rope else None, bf16),
              (ATT_Q, None, bf16), (LANES, None, f32)]
    return [(i, w, r, dtp, False) for i, (w, r, dtp) in enumerate(widths)]


def _pad_lanes(v, n=LANES):
    v = v.reshape(1, -1)
    return jnp.pad(v, ((0, 0), (0, n - v.shape[1])))


def kernel(x, c, ctx, c_ctx, e_norm_w, e_ada_w, e_ada_b, e_w_in, e_conv_w, e_conv_b, e_dt_bias,
           e_a_log, e_d_skip, e_ssd_norm_w, e_sink, e_w_out, o_norm_w, o_ada_w, o_ada_b, o_w_in,
           o_lam_re, o_lam_im, o_log_step, o_b_re, o_b_im, o_c_re, o_c_im, o_d_skip, o_glu_w,
           o_glu_b, o_w_out, final_norm_w):
    bsz, seq_len, d = x.shape
    n_ctx = ctx.shape[1]
    xf = x.reshape(bsz * seq_len, d)
    xcf = ctx.reshape(bsz * n_ctx, d)

    cvecs = jnp.concatenate([c, c_ctx[None, :], jnp.zeros((16 - bsz - 1, d), f32)], axis=0)

    def modulation(ada_w, ada_b):
        mod = _adaln(cvecs, ada_w.astype(bf16), ada_b.reshape(1, -1))
        parts = [mod[:, k * d:(k + 1) * d] for k in range(3)]
        lat = [p[:bsz].reshape(bsz, 1, d) for p in parts]
        cx = [p[bsz:bsz + 1].reshape(1, 1, d) for p in parts]
        return lat, cx

    (shift, scale, gate), (shift_c, scale_c, gate_c) = modulation(e_ada_w[0], e_ada_b[0])
    w_in, w_vt = _even_weights(e_w_in[0])
    nw = e_norm_w[0].reshape(1, d)
    tabs = _rope_tables(seq_len)
    kw = ATT_KV_HEADS * LANES
    vseg = [(0, kw, bf16)]
    z, xbc, q, k, g, dt, vt = _inproj(xf, shift, scale, nw, w_in, _even_segs(True), seq_len, tabs, w_vt, vseg)
    z_c, xbc_c, q_c, k_c, g_c, dt_c, vt_c = _inproj(xcf, shift_c, scale_c, nw, w_in, _even_segs(False), n_ctx,
                                                    None, w_vt, vseg)

    conv_w = jnp.pad(e_conv_w[0], ((0, 8 - SSD_CONV), (0, 0)))
    conv_b = e_conv_b[0].reshape(1, -1)
    dtb = _pad_lanes(e_dt_bias[0])
    alog = _pad_lanes(e_a_log[0])
    dskip = jnp.repeat(e_d_skip[0], SSD_HEAD_DIM).reshape(1, -1)
    snw = e_ssd_norm_w[0].reshape(1, -1)
    h0 = jnp.zeros((bsz, 2, SSD_STATE, SSD_INNER), f32)
    ssd_c, hfin = _ssd(xbc_c, dt_c, z_c, conv_w, conv_b, dtb, alog, dskip, snw, h0, bsz, n_ctx)
    ssd_o, _ = _ssd(xbc, dt, z, conv_w, conv_b, dtb, alog, dskip, snw, hfin, bsz, seq_len)

    k3 = k.reshape(bsz, seq_len, kw)
    kc3 = k_c.reshape(bsz, n_ctx, kw)
    sink = e_sink[0]
    att = _attention(q, g, k3, vt, kc3, vt_c, sink, bsz, seq_len, True)
    att_c = _attention(q_c, g_c, kc3, vt_c, kc3, vt_c, sink, bsz, n_ctx, False)
    w_out = e_w_out[0].astype(bf16)
    x1 = _outproj(ssd_o, att, w_out, xf, gate, seq_len)
    xc1 = _outproj(ssd_c, att_c, w_out, xcf, gate_c, n_ctx)

    (shift, scale, gate), (shift_c, scale_c, _) = modulation(o_ada_w[0], o_ada_b[0])
    w_u = o_w_in[0][:, :S5_WIDTH].astype(bf16)
    w_g = o_w_in[0][:, S5_WIDTH:].astype(bf16)
    nw = o_norm_w[0].reshape(1, d)
    u, u_ch, g2 = _inproj(x1, shift, scale, nw, [w_u, w_g],
                          [(0, S5_WIDTH, None, bf16, False), (0, S5_WIDTH, None, bf16, True),
                           (1, S5_WIDTH, None, bf16, False)], seq_len)
    (uc_ch,) = _inproj(xc1, shift_c, scale_c, nw, [w_u], [(0, S5_WIDTH, None, bf16, True)], n_ctx)

    ab_re, ab_im, bb_re, bb_im = _s5_discretise(o_lam_re[0], o_lam_im[0], o_log_step[0], o_b_re[0], o_b_im[0])
    arow, acol, bbd, cbd = _s5_block_params(ab_re, ab_im, bb_re, bb_im, o_c_re[0], o_c_im[0])
    y = _s5_mix(u_ch, uc_ch, arow, acol, bbd, cbd, bsz, seq_len, n_ctx)
    out = _s5_out(y, u, g2, x1, o_d_skip[0].reshape(1, -1),
                  o_glu_w[0].astype(bf16), o_glu_b[0].reshape(1, -1), o_w_out[0].astype(bf16), gate,
                  final_norm_w.reshape(1, -1), seq_len)
    return out.reshape(bsz, seq_len, d)
```

```python
import functools
import math

import jax
import jax.numpy as jnp
import numpy as np
from jax import lax
from jax.experimental import pallas as pl
from jax.experimental.pallas import tpu as pltpu

f32 = jnp.float32
bf16 = jnp.bfloat16

D_MODEL = 1024
GRID_W = 64
EPS = 1e-6
NEG_INF = -1e30

SSD_HEADS = 16
SSD_HEAD_DIM = 64
SSD_GROUPS = 2
SSD_STATE = 128
SSD_CONV = 5
SSD_CHUNK = 128
SSD_INNER = SSD_HEADS * SSD_HEAD_DIM
SSD_BC = SSD_GROUPS * SSD_STATE
SSD_XBC = SSD_INNER + 2 * SSD_BC
ATT_HEADS = 16
ATT_KV_HEADS = 4
ATT_HEAD_DIM = 64
ATT_BLOCK = 128
ROPE_THETA = 10000.0
ATT_Q = ATT_HEADS * ATT_HEAD_DIM
ATT_KVW = ATT_KV_HEADS * ATT_HEAD_DIM
S5_WIDTH = 1024
S5_GROUP_CH = 16
S5_GROUPS = S5_WIDTH // S5_GROUP_CH
S5_STATE = 64

LOG2E = math.log2(math.e)
LANES = 128
ROW_TILE = 512
ATT_SUB = 4
S5_GBLK = 8
S5_TC = 8
VMEM_LIMIT = 56 * 1024 * 1024


def _cparams(sem, flags=None):
    return pltpu.CompilerParams(dimension_semantics=sem, vmem_limit_bytes=VMEM_LIMIT, flags=flags)


def _silu(x):
    h = 0.5 * x
    return h + h * jnp.tanh(h)


def _adaln_kernel(c_ref, w_ref, b_ref, o_ref):
    c = c_ref[...]
    s = _silu(c).astype(bf16)
    o_ref[...] = jnp.dot(s, w_ref[...], preferred_element_type=f32) + b_ref[...]


def _adaln(cvecs, w, b):
    r, d = cvecs.shape
    n = w.shape[1]
    tn = 1024
    return pl.pallas_call(
        _adaln_kernel,
        out_shape=jax.ShapeDtypeStruct((r, n), f32),
        grid=(n // tn,),
        in_specs=[pl.BlockSpec((r, d), lambda j: (0, 0)),
                  pl.BlockSpec((d, tn), lambda j: (0, j)),
                  pl.BlockSpec((1, tn), lambda j: (0, j))],
        out_specs=pl.BlockSpec((r, tn), lambda j: (0, j)),
        compiler_params=_cparams(("arbitrary",)),
        name="adaln",
    )(cvecs, w, b)


def _inproj_kernel(segs, tsegs, n_w, has_rope, has_pre, x_ref, shift_ref, scale_ref, nw_ref, *rest):
    if has_pre:
        a_ref, b_ref, wo_ref, gate_ref = rest[:4]
        rest = rest[4:]
    w_refs, rest = rest[:n_w], rest[n_w:]
    if tsegs:
        wt_ref, rest = rest[0], rest[1:]
    if has_rope:
        cos_ref, sina_ref, sinb_ref = rest[:3]
        rest = rest[3:]
    if has_pre:
        x1_ref, rest = rest[0], rest[1:]
    outs = rest[:len(segs)]
    touts = rest[len(segs):len(segs) + len(tsegs)]
    rest = rest[len(segs) + len(tsegs):]
    slab_s = rest[0] if rest else None
    x = x_ref[...]
    if has_pre:
        ka = a_ref.shape[1]
        acc = jnp.dot(a_ref[...], wo_ref[:ka, :], preferred_element_type=f32)
        acc = acc + jnp.dot(b_ref[...], wo_ref[ka:, :], preferred_element_type=f32)
        x = x + gate_ref[...] * acc
        x1_ref[...] = x
    ms = jnp.mean(x * x, axis=-1, keepdims=True)
    h = (x * lax.rsqrt(ms + EPS)) * nw_ref[...]
    h = h * (1.0 + scale_ref[...]) + shift_ref[...]
    hb = h.astype(bf16)
    for (start, width, _), o_ref in zip(tsegs, touts):
        acc_t = lax.dot_general(wt_ref[start:start + width, :], hb, (((1,), (1,)), ((), ())),
                                preferred_element_type=f32)
        o_ref[...] = acc_t.astype(o_ref.dtype)
    products = {}
    for (widx, width, rope, _, chunked), o_ref in zip(segs, outs):
        if widx not in products:
            products[widx] = jnp.dot(hb, w_refs[widx][...], preferred_element_type=f32)
        acc = products[widx]
        if chunked:
            tm = acc.shape[0]
            for j in range(width // LANES):
                slab_s[j] = acc[:, j * LANES:(j + 1) * LANES]
            for s in range(S5_TC):
                for j in range(width // LANES):
                    o_ref[j, :, s * LANES:(s + 1) * LANES] = (
                        slab_s[j, pl.ds(s, tm // S5_TC, stride=S5_TC), :].astype(o_ref.dtype))
            continue
        if rope is not None and not has_rope:
            acc = acc * rope
        elif rope is not None:
            rep = width // LANES
            cos = jnp.concatenate([cos_ref[...]] * rep, axis=1) * rope
            sina = jnp.concatenate([sina_ref[...]] * rep, axis=1) * rope
            sinb = jnp.concatenate([sinb_ref[...]] * rep, axis=1) * rope
            half = ATT_HEAD_DIM // 2
            up = pltpu.roll(acc, width - half, axis=1)
            dn = pltpu.roll(acc, half, axis=1)
            acc = acc * cos + up * sina + dn * sinb
        o_ref[...] = acc.astype(o_ref.dtype)


def _inproj(x, shift, scale, nw, w, segs, rows_per_mod, rope_tabs=None, wt=None, tsegs=(), pre=None):
    m, d = x.shape
    tm = min(ROW_TILE, rows_per_mod)
    per = rows_per_mod // tm
    nmod = shift.shape[0]
    mod_idx = (lambda i: (i // per, 0, 0)) if nmod > 1 else (lambda i: (0, 0, 0))
    in_specs = [pl.BlockSpec((tm, d), lambda i: (i, 0)),
                pl.BlockSpec((None, 1, d), mod_idx),
                pl.BlockSpec((None, 1, d), mod_idx),
                pl.BlockSpec((1, d), lambda i: (0, 0))]
    args = [x, shift, scale, nw]
    if pre is not None:
        a, b, w_out, gate = pre
        gate_idx = (lambda i: (i // per, 0, 0)) if gate.shape[0] > 1 else (lambda i: (0, 0, 0))
        in_specs += [pl.BlockSpec((tm, a.shape[1]), lambda i: (i, 0)),
                     pl.BlockSpec((tm, b.shape[1]), lambda i: (i, 0)),
                     pl.BlockSpec(w_out.shape, lambda i: (0, 0)),
                     pl.BlockSpec((None, 1, d), gate_idx)]
        args += [a, b, w_out, gate]
    in_specs += [pl.BlockSpec(wi.shape, lambda i: (0, 0)) for wi in w]
    args += list(w)
    if tsegs:
        in_specs.append(pl.BlockSpec(wt.shape, lambda i: (0, 0)))
        args.append(wt)
    if rope_tabs is not None:
        for t in rope_tabs:
            in_specs.append(pl.BlockSpec((tm, LANES), lambda i: (i % per, 0)))
            args.append(t)
    out_shape, out_specs = [], []
    if pre is not None:
        out_shape.append(jax.ShapeDtypeStruct((m, d), f32))
        out_specs.append(pl.BlockSpec((tm, d), lambda i: (i, 0)))
    for sg in segs:
        if sg[4]:
            out_shape.append(jax.ShapeDtypeStruct((sg[1] // LANES, m // S5_TC, S5_TC * LANES), sg[3]))
            out_specs.append(pl.BlockSpec((sg[1] // LANES, tm // S5_TC, S5_TC * LANES), lambda i: (0, i, 0)))
        else:
            out_shape.append(jax.ShapeDtypeStruct((m, sg[1]), sg[3]))
            out_specs.append(pl.BlockSpec((tm, sg[1]), lambda i: (i, 0)))
    for _, width, dtp in tsegs:
        out_shape.append(jax.ShapeDtypeStruct((width, m), dtp))
        out_specs.append(pl.BlockSpec((width, tm), lambda i: (0, i)))
    chunked_w = [sg[1] for sg in segs if sg[4]]
    scratch = [pltpu.VMEM((max(chunked_w) // LANES, tm, LANES), f32)] if chunked_w else []
    return pl.pallas_call(
        functools.partial(_inproj_kernel, tuple(segs), tuple(tsegs), len(w), rope_tabs is not None,
                          pre is not None),
        out_shape=out_shape,
        grid=(m // tm,),
        in_specs=in_specs,
        out_specs=out_specs,
        scratch_shapes=scratch,
        compiler_params=_cparams(("parallel",)),
        name="inproj",
    )(*args)


def _rope_tables(seq_len):
    rows = seq_len // GRID_W
    row = jnp.repeat(jnp.arange(rows, dtype=f32), GRID_W)
    col = jnp.tile(jnp.arange(GRID_W, dtype=f32), rows)
    n_freq = ATT_HEAD_DIM // 4
    inv = ROPE_THETA ** (-jnp.arange(n_freq, dtype=f32) / n_freq)
    ang = jnp.concatenate([row[:, None] * inv, col[:, None] * inv], axis=-1)
    cos, sin = jnp.cos(ang), jnp.sin(ang)
    zero = jnp.zeros_like(sin)
    cos_h = jnp.concatenate([cos, cos], axis=-1)
    sina_h = jnp.concatenate([-sin, zero], axis=-1)
    sinb_h = jnp.concatenate([zero, sin], axis=-1)
    two = lambda t: jnp.concatenate([t, t], axis=-1)
    return two(cos_h), two(sina_h), two(sinb_h)


SSD_PACK = 32


def _split3(x):
    hi = x.astype(bf16)
    r1 = x - hi.astype(f32)
    mid = r1.astype(bf16)
    lo = (r1 - mid.astype(f32)).astype(bf16)
    return hi, mid, lo


def _pack3(x):
    hi, mid, lo = _split3(x)
    lane = lax.broadcasted_iota(jnp.int32, x.shape, 1)
    mid_r = pltpu.roll(mid.astype(f32), SSD_PACK, axis=1)
    lo_r = pltpu.roll(lo.astype(f32), 2 * SSD_PACK, axis=1)
    packed = jnp.where(lane < SSD_PACK, hi.astype(f32),
                       jnp.where(lane < 2 * SSD_PACK, mid_r,
                                 jnp.where(lane < 3 * SSD_PACK, lo_r, 0.0)))
    return packed.astype(bf16)


def _ssd_selectors():
    k = np.arange(LANES)
    src = np.where(k < 3 * SSD_PACK, k % SSD_PACK, -1)
    col_blk = np.arange(SSD_PACK * SSD_CHUNK) // SSD_CHUNK
    sel_bc = (src[:, None] == col_blk[None, :])
    head = np.arange(SSD_INNER) // SSD_HEAD_DIM
    sel_f = (src[:, None] == head[None, :])
    sel_b = (src[:, None] == (head + SSD_HEADS)[None, :])
    tri3 = np.tile(np.tril(np.ones((SSD_CHUNK, SSD_CHUNK))), (1, 3))
    rows = np.arange(SSD_CHUNK)[:, None]
    cols = np.arange(SSD_CHUNK + 32)[None, :]
    half = SSD_CONV // 2
    shift = np.concatenate([cols == rows + 16 + d for d in range(-half, half + 1) if d != 0], axis=0)
    as_bf = lambda a: jnp.asarray(a.astype(np.float32), dtype=bf16)
    return as_bf(sel_bc), as_bf(sel_f), as_bf(sel_b), as_bf(tri3), as_bf(shift)


def _ssd_kernel(seq_len, xbc_ref, dt_ref, z_ref, cw_ref, cb_ref, dtb_ref, alog_ref, dskip_ref, nw_ref,
                selbc_ref, self_ref, selb_ref, tri3_ref, shift_ref, h0_ref, out_ref, hfin_ref,
                xs_s, bc_s, dt_s, y_s, hf_s, hb_s, win_s):
    q = SSD_CHUNK
    nc = seq_len // q
    halo = 16
    H, P, N = SSD_HEADS, SSD_HEAD_DIM, SSD_STATE
    gw = (H // SSD_GROUPS) * P
    a2_row = -jnp.exp(alog_ref[...]) * math.log2(math.e)

    def conv_chunk(c):
        r0 = pl.multiple_of(c * q, q)
        pstart = pl.multiple_of(jnp.maximum(r0 - halo, 0), halo)
        nstart = pl.multiple_of(jnp.minimum(r0 + q, seq_len - halo), halo)
        zero = jnp.zeros((), bf16)
        win_s[0:halo, :] = jnp.where(c > 0, xbc_ref[pl.ds(pstart, halo), :], zero)
        win_s[halo:halo + q, :] = xbc_ref[pl.ds(r0, q), :]
        win_s[halo + q:, :] = jnp.where(c < nc - 1, xbc_ref[pl.ds(nstart, halo), :], zero)
        taps = [k for k in range(SSD_CONV) if k != SSD_CONV // 2]
        cw = 2 * LANES
        for j in range(SSD_XBC // cw):
            cs = slice(j * cw, (j + 1) * cw)
            sh = jnp.dot(shift_ref[...], win_s[:, cs], preferred_element_type=f32)
            acc = cb_ref[:, cs] + win_s[halo:halo + q, cs].astype(f32) * cw_ref[SSD_CONV // 2:SSD_CONV // 2 + 1, cs]
            for n, k in enumerate(taps):
                acc = acc + sh[n * q:(n + 1) * q, :] * cw_ref[k:k + 1, cs]
            act = _silu(acc)
            if j < SSD_INNER // cw:
                xs_s[pl.ds(r0, q), cs] = act
            else:
                bc_s[pl.ds(r0, q), j * cw - SSD_INNER:(j + 1) * cw - SSD_INNER] = act.astype(bf16)
        dt_s[pl.ds(r0, q), :] = jax.nn.softplus(dt_ref[pl.ds(r0, q), :] + dtb_ref[...])

    conv_chunk(0)
    hf_s[...] = h0_ref[0]
    hb_s[...] = h0_ref[1]

    ri = lax.broadcasted_iota(jnp.int32, (q, q), 0)
    ci = lax.broadcasted_iota(jnp.int32, (q, q), 1)
    lower = ri >= ci
    upper = ci >= ri
    lane = lax.broadcasted_iota(jnp.int32, (q, LANES), 1)
    lo_half = lane < P

    def cumsums(dt):
        dta = dt * a2_row
        cf = jnp.dot(tri3_ref[...], jnp.concatenate(_split3(dta), axis=0), preferred_element_type=f32)
        rb = cf[q - 1:q, :] - cf + dta
        return cf, rb

    def load_chunk(r0):
        dt = dt_s[pl.ds(r0, q), :]
        xs = xs_s[pl.ds(r0, q), :]
        bcv = bc_s[pl.ds(r0, q), :]
        bmat = [bcv[:, g * N:(g + 1) * N] for g in range(SSD_GROUPS)]
        cmat = [bcv[:, SSD_BC + g * N:SSD_BC + (g + 1) * N] for g in range(SSD_GROUPS)]
        return dt, xs, bmat, cmat

    def inter_chunk(h_s, sel_ref, decay, weight, xs, bmat, cmat, dec_idx):
        ew = jnp.dot(jnp.concatenate([_pack3(decay), _pack3(weight)], axis=0), sel_ref[...],
                     preferred_element_type=f32)
        e_x, w_x = ew[:q], ew[q:]
        hb_ = h_s[...].astype(bf16)
        yoff = jnp.concatenate(
            [jnp.dot(cmat[g], hb_[:, g * gw:(g + 1) * gw], preferred_element_type=f32)
             for g in range(SSD_GROUPS)], axis=1)
        xw = (xs * w_x).astype(bf16)
        dec_row = e_x[dec_idx:dec_idx + 1, :]
        for g in range(SSD_GROUPS):
            gs = slice(g * gw, (g + 1) * gw)
            bt = jnp.transpose(bmat[g].astype(f32)).astype(bf16)
            upd = jnp.dot(bt, xw[:, gs], preferred_element_type=f32)
            h_s[:, gs] = h_s[:, gs] * dec_row[:, gs] + upd
        return yoff * e_x

    def finish(r0, y, xs):
        yy = y + xs * dskip_ref[...]
        zz = z_ref[pl.ds(r0, q), :].astype(f32)
        gated = yy * _silu(zz)
        ms = jnp.mean(gated * gated, axis=-1, keepdims=True)
        out_ref[pl.ds(r0, q), :] = (gated * lax.rsqrt(ms + EPS) * nw_ref[...]).astype(out_ref.dtype)

    def fwd_chunk(c, second_half):
        r0 = pl.multiple_of(c * q, q)
        dt, xs, bmat, cmat = load_chunk(r0)
        cf, rb = cumsums(dt)
        pcol = jnp.where(lane < H, cf, rb)
        bcast = jnp.dot(_pack3(pcol), selbc_ref[...], preferred_element_type=f32)
        prow = jnp.transpose(pcol - jnp.log2(dt))
        cbm = [lax.dot_general(cmat[g], bmat[g], (((1,), (1,)), ((), ())), preferred_element_type=f32)
               for g in range(SSD_GROUPS)]
        xsb = xs.astype(bf16)
        zero_b = jnp.zeros((), bf16)
        ypairs = []
        for k in range(H // 2):
            mats = []
            for h in (2 * k, 2 * k + 1):
                g = h // (H // SSD_GROUPS)
                hb_ = H + h
                segf = bcast[:, h * q:(h + 1) * q] - prow[h:h + 1, :]
                segb = bcast[:, hb_ * q:(hb_ + 1) * q] - prow[hb_:hb_ + 1, :]
                df = jnp.exp2(jnp.where(lower, segf, NEG_INF))
                db = jnp.exp2(jnp.where(upper, segb, NEG_INF))
                mats.append((cbm[g] * (df + db)).astype(bf16))
            xp = xsb[:, k * LANES:(k + 1) * LANES]
            xbd = jnp.concatenate([jnp.where(lo_half, xp, zero_b), jnp.where(lo_half, zero_b, xp)], axis=0)
            ypairs.append(jnp.dot(jnp.concatenate(mats, axis=1), xbd, preferred_element_type=f32))
        y = jnp.concatenate(ypairs, axis=1)
        wfa = jnp.exp2(cf[q - 1:q, :] - cf) * dt
        y = y + inter_chunk(hf_s, self_ref, jnp.exp2(cf), wfa, xs, bmat, cmat, q - 1)
        if second_half:
            finish(r0, y_s[pl.ds(r0, q), :] + y, xs)
        else:
            y_s[pl.ds(r0, q), :] = y

    def bwd_chunk(c, second_half):
        r0 = pl.multiple_of(c * q, q)
        dt, xs, bmat, cmat = load_chunk(r0)
        _, rb = cumsums(dt)
        wba = jnp.exp2(rb[0:1, :] - rb) * dt
        y = inter_chunk(hb_s, selb_ref, jnp.exp2(rb), wba, xs, bmat, cmat, 0)
        if second_half:
            finish(r0, y_s[pl.ds(r0, q), :] + y, xs)
        else:
            y_s[pl.ds(r0, q), :] = y

    half = nc // 2
    conv_chunk(nc - 1)

    def first_half(i, carry):
        fwd_chunk(i, False)
        bwd_chunk(nc - 1 - i, False)
        conv_chunk(i + 1)
        conv_chunk(nc - 2 - i)
        return carry

    def second_half(i, carry):
        fwd_chunk(i, True)
        bwd_chunk(nc - 1 - i, True)
        return carry

    lax.fori_loop(0, half - 1, first_half, 0)
    fwd_chunk(jnp.int32(half - 1), False)
    bwd_chunk(jnp.int32(half), False)
    lax.fori_loop(half, nc, second_half, 0)
    hfin_ref[0] = hf_s[...]
    hfin_ref[1] = hb_s[...]


def _ssd(xbc, dt, z, conv_w, conv_b, dtb, alog, dskip, nw, h0, bsz, seq_len):
    assert seq_len % (2 * SSD_CHUNK) == 0, "the two recurrences meet in the middle: even chunk count"
    one = pl.Buffered(1)
    seq = lambda w: pl.BlockSpec((seq_len, w), lambda b: (b, 0), pipeline_mode=one)
    const = lambda r, w: pl.BlockSpec((r, w), lambda b: (0, 0))
    st = pl.BlockSpec((None, 2, SSD_STATE, SSD_INNER), lambda b: (b, 0, 0, 0))
    sels = _ssd_selectors()
    return pl.pallas_call(
        functools.partial(_ssd_kernel, seq_len),
        out_shape=[jax.ShapeDtypeStruct((bsz * seq_len, SSD_INNER), bf16),
                   jax.ShapeDtypeStruct((bsz, 2, SSD_STATE, SSD_INNER), f32)],
        grid=(bsz,),
        in_specs=[seq(SSD_XBC), seq(LANES), seq(SSD_INNER),
                  const(8, SSD_XBC), const(1, SSD_XBC), const(1, LANES), const(1, LANES),
                  const(1, SSD_INNER), const(1, SSD_INNER)]
                 + [const(*s.shape) for s in sels] + [st],
        out_specs=[seq(SSD_INNER), st],
        scratch_shapes=[pltpu.VMEM((seq_len, SSD_INNER), f32),
                        pltpu.VMEM((seq_len, 2 * SSD_BC), bf16),
                        pltpu.VMEM((seq_len, LANES), f32),
                        pltpu.VMEM((seq_len, SSD_INNER), f32),
                        pltpu.VMEM((SSD_STATE, SSD_INNER), f32),
                        pltpu.VMEM((SSD_STATE, SSD_INNER), f32),
                        pltpu.VMEM((SSD_CHUNK + 32, SSD_XBC), bf16)],
        compiler_params=_cparams(("parallel",)),
        name="ssd",
    )(xbc, dt, z, conv_w, conv_b, dtb, alog, dskip, nw, *sels, h0)


def _attn_kernel(n_blocks, nsub, local, q_ref, g_ref, k_ref, kc_ref, *rest):
    nv = nsub + 2 if local else 0
    v_refs = rest[:nv]
    vc_ref, sink_ref, o_ref, s_s = rest[nv:]
    t = ATT_BLOCK
    i0 = pl.program_id(1) * nsub
    rpk = ATT_HEADS // ATT_KV_HEADS
    lane = lax.broadcasted_iota(jnp.int32, (t, LANES), 1)
    lo_half = lane < ATT_HEAD_DIM
    zero_b = jnp.zeros((), bf16)
    kl = lax.broadcasted_iota(jnp.int32, (t, t), 0)
    ql = lax.broadcasted_iota(jnp.int32, (t, t), 1)
    for sub in range(nsub):
        i = i0 + sub
        qv = q_ref[sub * t:(sub + 1) * t, :]
        if local:
            p0 = pl.multiple_of(jnp.maximum(i - 1, 0) * t, t)
            c0 = pl.multiple_of(i * t, t)
            n0 = pl.multiple_of(jnp.minimum(i + 1, n_blocks - 1) * t, t)
            bias_prev = jnp.where((kl >= ql) & (i > 0), 0.0, NEG_INF)
            bias_next = jnp.where((kl <= ql) & (i < n_blocks - 1), 0.0, NEG_INF)
            bias_prev = jnp.concatenate([bias_prev] * rpk, axis=1)
            bias_next = jnp.concatenate([bias_next] * rpk, axis=1)
        for j in range(ATT_KV_HEADS):
            ls = slice(j * LANES, (j + 1) * LANES)
            if local:
                kk = jnp.concatenate([k_ref[pl.ds(p0, t), ls], k_ref[pl.ds(c0, t), ls],
                                      k_ref[pl.ds(n0, t), ls], kc_ref[:, ls]], axis=0)
            else:
                kk = kc_ref[:, ls]
            pieces = []
            for r in range(rpk):
                hq = j * rpk + r
                qp = qv[:, (hq // 2) * LANES:(hq // 2 + 1) * LANES]
                keep = lo_half if hq % 2 == 0 else jnp.logical_not(lo_half)
                pieces.append(jnp.where(keep, qp, zero_b))
            q4 = jnp.concatenate(pieces, axis=0)
            s = lax.dot_general(kk, q4, (((1,), (1,)), ((), ())), preferred_element_type=f32)
            if local:
                s = jnp.concatenate([s[:t] + bias_prev, s[t:2 * t], s[2 * t:3 * t] + bias_next, s[3 * t:]],
                                    axis=0)
            s_s[sub, j] = s
    for sub in range(nsub):
        outs = []
        for j in range(ATT_KV_HEADS):
            ls = slice(j * LANES, (j + 1) * LANES)
            if local:
                vvt = jnp.concatenate([v_refs[sub + k][ls, :] for k in range(3)] + [vc_ref[ls, :]], axis=1)
            else:
                vvt = vc_ref[ls, :]
            sk = jnp.concatenate([jnp.full((1, t), sink_ref[j * rpk + r] * LOG2E, f32) for r in range(rpk)],
                                 axis=1)
            s = s_s[sub, j]
            m = jnp.maximum(jnp.max(s, axis=0, keepdims=True), sk)
            p = jnp.exp2(s - m)
            den = jnp.sum(p, axis=0, keepdims=True) + jnp.exp2(sk - m)
            ot = jnp.dot(vvt, p.astype(bf16), preferred_element_type=f32) / den
            o4 = [jnp.transpose(ot[:, r * t:(r + 1) * t]) for r in range(rpk)]
            outs.append(jnp.where(lo_half, o4[0], o4[1]))
            outs.append(jnp.where(lo_half, o4[2], o4[3]))
        o = jnp.concatenate(outs, axis=1)
        gv = g_ref[sub * t:(sub + 1) * t, :].astype(f32)
        o_ref[sub * t:(sub + 1) * t, :] = (o * _silu(gv)).astype(o_ref.dtype)


def _attention(q, g, k, vt, kc, vct, sink, bsz, seq_len, local):
    t = ATT_BLOCK
    nb = seq_len // t
    nsub = max(s for s in range(1, ATT_SUB + 1) if nb % s == 0)
    n_ctx = kc.shape[0] // bsz
    kw = ATT_KV_HEADS * LANES
    blk = pl.BlockSpec((nsub * t, ATT_Q), lambda b, i: (b * (nb // nsub) + i, 0))
    full = lambda n: pl.BlockSpec((n, kw), lambda b, i: (b, 0))
    vblk = lambda off: pl.BlockSpec((kw, t), lambda b, i: (0, b * nb + jnp.clip(i * nsub + off, 0, nb - 1)))
    vspecs = [vblk(off) for off in range(-1, nsub + 1)] if local else []
    return pl.pallas_call(
        functools.partial(_attn_kernel, nb, nsub, local),
        out_shape=jax.ShapeDtypeStruct((bsz * seq_len, ATT_Q), bf16),
        grid=(bsz, nb // nsub),
        in_specs=[blk, blk, full(k.shape[0] // bsz), full(n_ctx)] + vspecs
                 + [pl.BlockSpec((kw, n_ctx), lambda b, i: (0, b)), pl.BlockSpec(memory_space=pltpu.SMEM)],
        out_specs=blk,
        scratch_shapes=[pltpu.VMEM((nsub, ATT_KV_HEADS, (3 * t if local else 0) + n_ctx,
                                    (ATT_HEADS // ATT_KV_HEADS) * t), f32)],
        compiler_params=_cparams(("parallel", "arbitrary")),
        name="attention",
    )(q, g, k, kc, *([vt] * len(vspecs)), vct, sink)


def _s5_disc_kernel(lre_ref, lim_ref, ls_ref, bre_ref, bim_ref, abre_ref, abim_ref, bbre_ref, bbim_ref):
    lam_re = lre_ref[...]
    lam_im = lim_ref[...]
    dt = jnp.exp(ls_ref[...])
    mag = jnp.exp(lam_re * dt)
    ab_re = mag * jnp.cos(lam_im * dt)
    ab_im = mag * jnp.sin(lam_im * dt)
    num_re, num_im = ab_re - 1.0, ab_im
    den = lam_re * lam_re + lam_im * lam_im
    coef_re = (num_re * lam_re + num_im * lam_im) / den
    coef_im = (num_im * lam_re - num_re * lam_im) / den
    b_re, b_im = bre_ref[...], bim_ref[...]
    abre_ref[...] = ab_re
    abim_ref[...] = ab_im
    bbre_ref[...] = coef_re * b_re - coef_im * b_im
    bbim_ref[...] = coef_re * b_im + coef_im * b_re


def _s5_discretise(lam_re, lam_im, log_step, b_re, b_im):
    g, n, cg = b_re.shape
    exp = lambda t: jnp.repeat(t.reshape(2 * g, n), cg, axis=1)
    ls = jnp.broadcast_to(log_step.reshape(2 * g, 1), (2 * g, n * cg))
    bb = lambda t: jnp.tile(t.reshape(g, n * cg), (2, 1))
    shp = jax.ShapeDtypeStruct((2 * g, n * cg), f32)
    ab_re, ab_im, bb_re, bb_im = pl.pallas_call(
        _s5_disc_kernel, out_shape=[shp] * 4, name="s5_disc",
    )(exp(lam_re), exp(lam_im), ls, bb(b_re), bb(b_im))
    first = lambda t: t.reshape(2, g, n, cg)[..., 0]
    full = lambda t: t.reshape(2, g, n, cg)
    return first(ab_re), first(ab_im), full(bb_re), full(bb_im)


def _cmul(ar, ai, br, bi):
    return ar * br - ai * bi, ar * bi + ai * br


def _s5_kernel(n_lat, n_ctx, *refs):
    tc = S5_TC
    xl, xc, arow_ref, acol_ref, bbd_ref, cbd_ref, y_ref, wyz_s, ws_s, sl_s, sc_s = refs
    sw = S5_GBLK * S5_STATE
    nsl = sw // LANES
    nb = 8

    def powers(re, im, n):
        out = [(jnp.ones_like(re), jnp.zeros_like(im))]
        for _ in range(n):
            out.append(_cmul(out[-1][0], out[-1][1], re, im))
        return out

    prow = [powers(arow_ref[2 * d:2 * d + 1, :], arow_ref[2 * d + 1:2 * d + 2, :], tc) for d in range(2)]

    for s in range(tc):
        for d, k in ((0, tc - 1 - s), (1, s)):
            wr, wi = _cmul(bbd_ref[2 * d], bbd_ref[2 * d + 1], *prow[d][k])
            ws_s[s * LANES:(s + 1) * LANES, d * 2 * sw:d * 2 * sw + sw] = wr.astype(bf16)
            ws_s[s * LANES:(s + 1) * LANES, d * 2 * sw + sw:(d + 1) * 2 * sw] = wi.astype(bf16)
    crhs = jnp.concatenate([cbd_ref[0], -cbd_ref[1]], axis=0).astype(bf16)
    kall = [jnp.dot(ws_s[:, d * 2 * sw:(d + 1) * 2 * sw], crhs, preferred_element_type=f32) for d in range(2)]
    kf = [kall[0][(tc - 1 - k) * LANES:(tc - k) * LANES] for k in range(tc)]
    kb = [kall[1][k * LANES:(k + 1) * LANES] for k in range(tc)]
    for s in range(tc):
        for t in range(tc):
            blk = kf[t - s] if t > s else (kb[s - t] if t < s else kf[0] + kb[0])
            wyz_s[s * LANES:(s + 1) * LANES, t * LANES:(t + 1) * LANES] = blk.astype(bf16)
    base = tc * LANES
    for d in range(2):
        a1 = (acol_ref[2 * d], acol_ref[2 * d + 1])
        ak = a1
        for k in range(1, tc + 1):
            t = k - 1 if d == 0 else tc - k
            dre, dim_ = _cmul(cbd_ref[0], cbd_ref[1], *ak)
            r0 = base + d * 2 * sw
            wyz_s[r0:r0 + sw, t * LANES:(t + 1) * LANES] = dre.astype(bf16)
            wyz_s[r0 + sw:r0 + 2 * sw, t * LANES:(t + 1) * LANES] = (-dim_).astype(bf16)
            if k < tc:
                ak = _cmul(*ak, *a1)

    def rows_of(x_ref, b, n):
        return x_ref[b * n:(b + 1) * n, :]

    def inject(x_refs, s_ref, n):
        for b in range(nb):
            sb = jnp.dot(rows_of(x_refs, b, n), ws_s[...], preferred_element_type=f32)
            for k in range(4 * nsl):
                s_ref[k, pl.ds(b, n, stride=nb), :] = sb[:, k * LANES:(k + 1) * LANES]

    at = [[tuple(jnp.broadcast_to(p[:, k * LANES:(k + 1) * LANES], (nb, LANES)) for p in prow[d][tc])
           for k in range(nsl)] for d in range(2)]

    def scan(s_ref, n, init):
        def step(i, carry):
            new = []
            for d in range(2):
                idx = pl.ds(pl.multiple_of((i if d == 0 else n - 1 - i) * nb, nb), nb)
                for k in range(nsl):
                    hr, hi = carry[2 * (d * nsl + k)], carry[2 * (d * nsl + k) + 1]
                    kr, ki = d * 2 * nsl + k, d * 2 * nsl + nsl + k
                    sr, si = s_ref[kr, idx, :], s_ref[ki, idx, :]
                    s_ref[kr, idx, :] = hr
                    s_ref[ki, idx, :] = hi
                    ar, ai = at[d][k]
                    new += [ar * hr - ai * hi + sr, ar * hi + ai * hr + si]
            return tuple(new)
        return lax.fori_loop(0, n, step, init, unroll=4)

    inject(xc, sc_s, n_ctx)
    h_ctx = scan(sc_s, n_ctx, tuple(jnp.zeros((nb, LANES), f32) for _ in range(4 * nsl)))
    inject(xl, sl_s, n_lat)
    scan(sl_s, n_lat, h_ctx)

    for b in range(nb):
        hin = jnp.concatenate([sl_s[k, pl.ds(b, n_lat, stride=nb), :] for k in range(4 * nsl)],
                              axis=1).astype(bf16)
        yb = jnp.dot(jnp.concatenate([rows_of(xl, b, n_lat), hin], axis=1), wyz_s[...],
                     preferred_element_type=f32)
        for t in range(tc):
            y_ref[pl.ds(b * n_lat * tc + t, n_lat, stride=tc), :] = yb[:, t * LANES:(t + 1) * LANES]


def _s5_mix(u, u_c, arow, acol, bbd, cbd, bsz, seq_len, n_ctx_tok):
    assert bsz == 8, "the chunk recurrence puts the batch on the 8 sublanes"
    tc = S5_TC
    n_lat, n_ctx = seq_len // tc, n_ctx_tok // tc
    nblk = S5_GROUPS // S5_GBLK
    sw = S5_GBLK * S5_STATE
    nsl = sw // LANES
    one = pl.Buffered(1)
    xspec = lambda rows: pl.BlockSpec((None, rows, tc * LANES), lambda g: (g, 0, 0))
    par = lambda *shape: pl.BlockSpec((None,) + shape, lambda g: (g,) + (0,) * len(shape), pipeline_mode=one)
    return pl.pallas_call(
        functools.partial(_s5_kernel, n_lat, n_ctx),
        out_shape=jax.ShapeDtypeStruct((nblk, bsz * seq_len, LANES), f32),
        grid=(nblk,),
        in_specs=[xspec(bsz * n_lat), xspec(bsz * n_ctx)]
                 + [par(4, sw), par(4, sw, LANES),
                    pl.BlockSpec((4, None, LANES, sw), lambda g: (0, g, 0, 0), pipeline_mode=one),
                    pl.BlockSpec((2, None, sw, LANES), lambda g: (0, g, 0, 0), pipeline_mode=one)],
        out_specs=pl.BlockSpec((None, bsz * seq_len, LANES), lambda g: (g, 0, 0), pipeline_mode=one),
        scratch_shapes=[pltpu.VMEM((tc * LANES + 4 * sw, tc * LANES), bf16),
                        pltpu.VMEM((tc * LANES, 4 * sw), bf16),
                        pltpu.VMEM((4 * nsl, bsz * n_lat, LANES), f32),
                        pltpu.VMEM((4 * nsl, bsz * n_ctx, LANES), f32)],
        compiler_params=_cparams(("arbitrary",)),
        name="s5_mix",
    )(u, u_c, arow, acol, bbd, cbd)


def _s5_block_params(ab_re, ab_im, bb_re, bb_im, c_re, c_im):
    g, n, cg = S5_GROUPS, S5_STATE, S5_GROUP_CH
    nblk = g // S5_GBLK
    sw = S5_GBLK * n
    eye = jnp.eye(S5_GBLK, dtype=f32)
    arow = jnp.stack([t[d].reshape(nblk, sw) for d in range(2) for t in (ab_re, ab_im)], axis=1)
    acol = jnp.broadcast_to(arow[..., None], (nblk, 4, sw, LANES))
    blockdiag = lambda t: t[..., :, :, None, :] * eye[:, None, :, None]
    bb = jnp.stack([t[d] for d in range(2) for t in (bb_re, bb_im)], axis=0)
    bb = jnp.swapaxes(bb.reshape(4, nblk, S5_GBLK, n, cg), -1, -2)
    bbd = blockdiag(bb).reshape(4, nblk, S5_GBLK * cg, sw)
    cc = jnp.swapaxes(jnp.stack([c_re, c_im], axis=0).reshape(2, nblk, S5_GBLK, cg, n), -1, -2)
    cbd = blockdiag(cc).reshape(2, nblk, sw, S5_GBLK * cg)
    return arow, acol, bbd, cbd


def _s5_out_kernel(y_ref, u_ref, g_ref, x_ref, dskip_ref, gw_ref, gb_ref, w_ref, gate_ref,
                   fw_ref, o_ref):
    y = jnp.concatenate([y_ref[j] for j in range(y_ref.shape[0])], axis=1)
    y = y + dskip_ref[...] * u_ref[...].astype(f32)
    y = jax.nn.gelu(y)
    glu = jnp.dot(y.astype(bf16), gw_ref[...], preferred_element_type=f32) + gb_ref[...]
    y = y * jax.nn.sigmoid(glu)
    y = y * _silu(g_ref[...].astype(f32))
    x = x_ref[...] + gate_ref[...] * jnp.dot(y.astype(bf16), w_ref[...], preferred_element_type=f32)
    ms = jnp.mean(x * x, axis=-1, keepdims=True)
    o_ref[...] = x * lax.rsqrt(ms + EPS) * fw_ref[...]


def _s5_out(y, u, g, x, dskip, gw, gb, w, gate, fw, rows_per_mod):
    m, d = x.shape
    tm = min(ROW_TILE, rows_per_mod)
    per = rows_per_mod // tm
    row = lambda: pl.BlockSpec((tm, d), lambda i: (i, 0))
    vec = lambda: pl.BlockSpec((1, d), lambda i: (0, 0))
    mat = lambda: pl.BlockSpec((d, d), lambda i: (0, 0))
    return pl.pallas_call(
        _s5_out_kernel,
        out_shape=jax.ShapeDtypeStruct((m, d), f32),
        grid=(m // tm,),
        in_specs=[pl.BlockSpec((y.shape[0], tm, LANES), lambda i: (0, i, 0)),
                  row(), row(), row(), vec(), mat(), vec(), mat(),
                  pl.BlockSpec((None, 1, d), lambda i: (i // per, 0, 0)), vec()],
        out_specs=row(),
        compiler_params=_cparams(("parallel",)),
        name="s5_out",
    )(y, u, g, x, dskip, gw, gb, w, gate, fw)


def _even_weights(w_in):
    o = 0
    z = w_in[:, o:o + SSD_INNER]; o += SSD_INNER
    xbc = w_in[:, o:o + SSD_XBC]; o += SSD_XBC
    dt = w_in[:, o:o + 2 * SSD_HEADS]; o += 2 * SSD_HEADS
    q = w_in[:, o:o + ATT_Q]; o += ATT_Q
    k = w_in[:, o:o + ATT_KVW]; o += ATT_KVW
    v = w_in[:, o:o + ATT_KVW]; o += ATT_KVW
    g = w_in[:, o:o + ATT_Q]
    d = w_in.shape[0]
    dup = lambda t: jnp.concatenate([t.reshape(d, ATT_KV_HEADS, 1, ATT_HEAD_DIM)] * 2, axis=2).reshape(d, -1)
    dtp = jnp.pad(dt, ((0, 0), (0, LANES - 2 * SSD_HEADS)))
    cast = lambda t: t.astype(bf16)
    return [cast(t) for t in (z, xbc, q, dup(k), g, dtp)], cast(dup(v).T)


def _even_segs(rope):
    scale = ATT_HEAD_DIM ** -0.5 * LOG2E
    widths = [(SSD_INNER, None, bf16), (SSD_XBC, None, bf16),
              (ATT_Q, scale, bf16),
              (ATT_KV_HEADS * LANES, 1.0 if rope else None, bf16),
              (ATT_Q, None, bf16), (LANES, None, f32)]
    return [(i, w, r, dtp, False) for i, (w, r, dtp) in enumerate(widths)]


def _pad_lanes(v, n=LANES):
    v = v.reshape(1, -1)
    return jnp.pad(v, ((0, 0), (0, n - v.shape[1])))


def kernel(x, c, ctx, c_ctx, e_norm_w, e_ada_w, e_ada_b, e_w_in, e_conv_w, e_conv_b, e_dt_bias,
           e_a_log, e_d_skip, e_ssd_norm_w, e_sink, e_w_out, o_norm_w, o_ada_w, o_ada_b, o_w_in,
           o_lam_re, o_lam_im, o_log_step, o_b_re, o_b_im, o_c_re, o_c_im, o_d_skip, o_glu_w,
           o_glu_b, o_w_out, final_norm_w):
    bsz, seq_len, d = x.shape
    n_ctx = ctx.shape[1]
    xf = x.reshape(bsz * seq_len, d)
    xcf = ctx.reshape(bsz * n_ctx, d)

    cvecs = jnp.concatenate([c, c_ctx[None, :], jnp.zeros((16 - bsz - 1, d), f32)], axis=0)

    def modulation(ada_w, ada_b):
        mod = _adaln(cvecs, ada_w.astype(bf16), ada_b.reshape(1, -1))
        parts = [mod[:, k * d:(k + 1) * d] for k in range(3)]
        lat = [p[:bsz].reshape(bsz, 1, d) for p in parts]
        cx = [p[bsz:bsz + 1].reshape(1, 1, d) for p in parts]
        return lat, cx

    (shift, scale, gate), (shift_c, scale_c, gate_c) = modulation(e_ada_w[0], e_ada_b[0])
    w_in, w_vt = _even_weights(e_w_in[0])
    nw = e_norm_w[0].reshape(1, d)
    tabs = _rope_tables(seq_len)
    kw = ATT_KV_HEADS * LANES
    vseg = [(0, kw, bf16)]
    z, xbc, q, k, g, dt, vt = _inproj(xf, shift, scale, nw, w_in, _even_segs(True), seq_len, tabs, w_vt, vseg)
    z_c, xbc_c, q_c, k_c, g_c, dt_c, vt_c = _inproj(xcf, shift_c, scale_c, nw, w_in, _even_segs(False), n_ctx,
                                                    None, w_vt, vseg)

    conv_w = jnp.pad(e_conv_w[0], ((0, 8 - SSD_CONV), (0, 0)))
    conv_b = e_conv_b[0].reshape(1, -1)
    dtb = _pad_lanes(e_dt_bias[0])
    alog = _pad_lanes(e_a_log[0])
    dskip = jnp.repeat(e_d_skip[0], SSD_HEAD_DIM).reshape(1, -1)
    snw = e_ssd_norm_w[0].reshape(1, -1)
    h0 = jnp.zeros((bsz, 2, SSD_STATE, SSD_INNER), f32)
    ssd_c, hfin = _ssd(xbc_c, dt_c, z_c, conv_w, conv_b, dtb, alog, dskip, snw, h0, bsz, n_ctx)
    ssd_o, _ = _ssd(xbc, dt, z, conv_w, conv_b, dtb, alog, dskip, snw, hfin, bsz, seq_len)

    sink = e_sink[0]
    att = _attention(q, g, k, vt, k_c, vt_c, sink, bsz, seq_len, True)
    att_c = _attention(q_c, g_c, k_c, vt_c, k_c, vt_c, sink, bsz, n_ctx, False)
    w_out = e_w_out[0].astype(bf16)
    even_gate, even_gate_c = gate, gate_c

    (shift, scale, gate), (shift_c, scale_c, _) = modulation(o_ada_w[0], o_ada_b[0])
    w_u = o_w_in[0][:, :S5_WIDTH].astype(bf16)
    w_g = o_w_in[0][:, S5_WIDTH:].astype(bf16)
    nw = o_norm_w[0].reshape(1, d)
    x1, u, u_ch, g2 = _inproj(xf, shift, scale, nw, [w_u, w_g],
                              [(0, S5_WIDTH, None, bf16, False), (0, S5_WIDTH, None, bf16, True),
                               (1, S5_WIDTH, None, bf16, False)], seq_len,
                              pre=(ssd_o, att, w_out, even_gate))
    _, uc_ch = _inproj(xcf, shift_c, scale_c, nw, [w_u], [(0, S5_WIDTH, None, bf16, True)], n_ctx,
                       pre=(ssd_c, att_c, w_out, even_gate_c))

    ab_re, ab_im, bb_re, bb_im = _s5_discretise(o_lam_re[0], o_lam_im[0], o_log_step[0], o_b_re[0], o_b_im[0])
    arow, acol, bbd, cbd = _s5_block_params(ab_re, ab_im, bb_re, bb_im, o_c_re[0], o_c_im[0])
    y = _s5_mix(u_ch, uc_ch, arow, acol, bbd, cbd, bsz, seq_len, n_ctx)
    out = _s5_out(y, u, g2, x1, o_d_skip[0].reshape(1, -1),
                  o_glu_w[0].astype(bf16), o_glu_b[0].reshape(1, -1), o_w_out[0].astype(bf16), gate,
                  final_norm_w.reshape(1, -1), seq_len)
    return out.reshape(bsz, seq_len, d)
```

```python
import functools
import math

import jax
import jax.numpy as jnp
import numpy as np
from jax import lax
from jax.experimental import pallas as pl
from jax.experimental.pallas import tpu as pltpu

f32 = jnp.float32
bf16 = jnp.bfloat16

D_MODEL = 1024
GRID_W = 64
EPS = 1e-6
NEG_INF = -1e30

SSD_HEADS = 16
SSD_HEAD_DIM = 64
SSD_GROUPS = 2
SSD_STATE = 128
SSD_CONV = 5
SSD_CHUNK = 128
SSD_INNER = SSD_HEADS * SSD_HEAD_DIM
SSD_BC = SSD_GROUPS * SSD_STATE
SSD_XBC = SSD_INNER + 2 * SSD_BC
ATT_HEADS = 16
ATT_KV_HEADS = 4
ATT_HEAD_DIM = 64
ATT_BLOCK = 128
ROPE_THETA = 10000.0
ATT_Q = ATT_HEADS * ATT_HEAD_DIM
ATT_KVW = ATT_KV_HEADS * ATT_HEAD_DIM
S5_WIDTH = 1024
S5_GROUP_CH = 16
S5_GROUPS = S5_WIDTH // S5_GROUP_CH
S5_STATE = 64

LOG2E = math.log2(math.e)
LANES = 128
ROW_TILE = 512
ATT_SUB = 4
S5_GBLK = 8
S5_TC = 8
S5_HGRP = 4
VMEM_LIMIT = 56 * 1024 * 1024


def _cparams(sem, flags=None):
    return pltpu.CompilerParams(dimension_semantics=sem, vmem_limit_bytes=VMEM_LIMIT, flags=flags)


def _silu(x):
    h = 0.5 * x
    return h + h * jnp.tanh(h)


def _adaln_kernel(c_ref, w_ref, b_ref, o_ref):
    c = c_ref[...]
    s = _silu(c).astype(bf16)
    o_ref[...] = jnp.dot(s, w_ref[...], preferred_element_type=f32) + b_ref[...]


def _adaln(cvecs, w, b):
    r, d = cvecs.shape
    n = w.shape[1]
    tn = 1024
    return pl.pallas_call(
        _adaln_kernel,
        out_shape=jax.ShapeDtypeStruct((r, n), f32),
        grid=(n // tn,),
        in_specs=[pl.BlockSpec((r, d), lambda j: (0, 0)),
                  pl.BlockSpec((d, tn), lambda j: (0, j)),
                  pl.BlockSpec((1, tn), lambda j: (0, j))],
        out_specs=pl.BlockSpec((r, tn), lambda j: (0, j)),
        compiler_params=_cparams(("arbitrary",)),
        name="adaln",
    )(cvecs, w, b)


def _inproj_kernel(segs, tsegs, n_w, has_rope, has_pre, x_ref, shift_ref, scale_ref, nw_ref, *rest):
    if has_pre:
        a_ref, b_ref, wo_ref, gate_ref = rest[:4]
        rest = rest[4:]
    w_refs, rest = rest[:n_w], rest[n_w:]
    if tsegs:
        wt_ref, rest = rest[0], rest[1:]
    if has_rope:
        cos_ref, sina_ref, sinb_ref = rest[:3]
        rest = rest[3:]
    if has_pre:
        x1_ref, rest = rest[0], rest[1:]
    outs = rest[:len(segs)]
    touts = rest[len(segs):len(segs) + len(tsegs)]
    rest = rest[len(segs) + len(tsegs):]
    slab_s = rest[0] if rest else None
    x = x_ref[...]
    if has_pre:
        ka = a_ref.shape[1]
        acc = jnp.dot(a_ref[...], wo_ref[:ka, :], preferred_element_type=f32)
        acc = acc + jnp.dot(b_ref[...], wo_ref[ka:, :], preferred_element_type=f32)
        x = x + gate_ref[...] * acc
        x1_ref[...] = x
    ms = jnp.mean(x * x, axis=-1, keepdims=True)
    h = (x * lax.rsqrt(ms + EPS)) * nw_ref[...]
    h = h * (1.0 + scale_ref[...]) + shift_ref[...]
    hb = h.astype(bf16)
    for (start, width, _), o_ref in zip(tsegs, touts):
        acc_t = lax.dot_general(wt_ref[start:start + width, :], hb, (((1,), (1,)), ((), ())),
                                preferred_element_type=f32)
        o_ref[...] = acc_t.astype(o_ref.dtype)
    products = {}
    for (widx, width, rope, _, chunked), o_ref in zip(segs, outs):
        if widx not in products:
            products[widx] = jnp.dot(hb, w_refs[widx][...], preferred_element_type=f32)
        acc = products[widx]
        if chunked:
            tm = acc.shape[0]
            hl = LANES // 2
            lo = lax.broadcasted_iota(jnp.int32, (1, LANES), 1) < hl
            for j in range(width // LANES):
                slab_s[j] = acc[:, j * LANES:(j + 1) * LANES]
            for j in range(width // LANES):
                rows = [slab_s[j, pl.ds(s, tm // S5_TC, stride=S5_TC), :] for s in range(S5_TC)]
                for p in range(S5_TC // 2):
                    a, b = rows[2 * p], rows[2 * p + 1]
                    h0 = jnp.where(lo, a, pltpu.roll(b, hl, axis=1))
                    h1 = jnp.where(lo, pltpu.roll(a, hl, axis=1), b)
                    o_ref[j, :, p * LANES:(p + 1) * LANES] = h0.astype(o_ref.dtype)
                    o_ref[j, :, S5_TC * hl + p * LANES:S5_TC * hl + (p + 1) * LANES] = h1.astype(o_ref.dtype)
            continue
        if rope is not None and not has_rope:
            acc = acc * rope
        elif rope is not None:
            rep = width // LANES
            cos = jnp.concatenate([cos_ref[...]] * rep, axis=1) * rope
            sina = jnp.concatenate([sina_ref[...]] * rep, axis=1) * rope
            sinb = jnp.concatenate([sinb_ref[...]] * rep, axis=1) * rope
            half = ATT_HEAD_DIM // 2
            up = pltpu.roll(acc, width - half, axis=1)
            dn = pltpu.roll(acc, half, axis=1)
            acc = acc * cos + up * sina + dn * sinb
        o_ref[...] = acc.astype(o_ref.dtype)


def _inproj(x, shift, scale, nw, w, segs, rows_per_mod, rope_tabs=None, wt=None, tsegs=(), pre=None):
    m, d = x.shape
    tm = min(ROW_TILE, rows_per_mod)
    per = rows_per_mod // tm
    nmod = shift.shape[0]
    mod_idx = (lambda i: (i // per, 0, 0)) if nmod > 1 else (lambda i: (0, 0, 0))
    in_specs = [pl.BlockSpec((tm, d), lambda i: (i, 0)),
                pl.BlockSpec((None, 1, d), mod_idx),
                pl.BlockSpec((None, 1, d), mod_idx),
                pl.BlockSpec((1, d), lambda i: (0, 0))]
    args = [x, shift, scale, nw]
    if pre is not None:
        a, b, w_out, gate = pre
        gate_idx = (lambda i: (i // per, 0, 0)) if gate.shape[0] > 1 else (lambda i: (0, 0, 0))
        in_specs += [pl.BlockSpec((tm, a.shape[1]), lambda i: (i, 0)),
                     pl.BlockSpec((tm, b.shape[1]), lambda i: (i, 0)),
                     pl.BlockSpec(w_out.shape, lambda i: (0, 0)),
                     pl.BlockSpec((None, 1, d), gate_idx)]
        args += [a, b, w_out, gate]
    in_specs += [pl.BlockSpec(wi.shape, lambda i: (0, 0)) for wi in w]
    args += list(w)
    if tsegs:
        in_specs.append(pl.BlockSpec(wt.shape, lambda i: (0, 0)))
        args.append(wt)
    if rope_tabs is not None:
        for t in rope_tabs:
            in_specs.append(pl.BlockSpec((tm, LANES), lambda i: (i % per, 0)))
            args.append(t)
    out_shape, out_specs = [], []
    if pre is not None:
        out_shape.append(jax.ShapeDtypeStruct((m, d), f32))
        out_specs.append(pl.BlockSpec((tm, d), lambda i: (i, 0)))
    for sg in segs:
        if sg[4]:
            out_shape.append(jax.ShapeDtypeStruct((sg[1] // LANES, m // S5_TC, S5_TC * LANES), sg[3]))
            out_specs.append(pl.BlockSpec((sg[1] // LANES, tm // S5_TC, S5_TC * LANES), lambda i: (0, i, 0)))
        else:
            out_shape.append(jax.ShapeDtypeStruct((m, sg[1]), sg[3]))
            out_specs.append(pl.BlockSpec((tm, sg[1]), lambda i: (i, 0)))
    for _, width, dtp in tsegs:
        out_shape.append(jax.ShapeDtypeStruct((width, m), dtp))
        out_specs.append(pl.BlockSpec((width, tm), lambda i: (0, i)))
    chunked_w = [sg[1] for sg in segs if sg[4]]
    scratch = [pltpu.VMEM((max(chunked_w) // LANES, tm, LANES), f32)] if chunked_w else []
    return pl.pallas_call(
        functools.partial(_inproj_kernel, tuple(segs), tuple(tsegs), len(w), rope_tabs is not None,
                          pre is not None),
        out_shape=out_shape,
        grid=(m // tm,),
        in_specs=in_specs,
        out_specs=out_specs,
        scratch_shapes=scratch,
        compiler_params=_cparams(("parallel",)),
        name="inproj",
    )(*args)


def _rope_tables(seq_len):
    rows = seq_len // GRID_W
    row = jnp.repeat(jnp.arange(rows, dtype=f32), GRID_W)
    col = jnp.tile(jnp.arange(GRID_W, dtype=f32), rows)
    n_freq = ATT_HEAD_DIM // 4
    inv = ROPE_THETA ** (-jnp.arange(n_freq, dtype=f32) / n_freq)
    ang = jnp.concatenate([row[:, None] * inv, col[:, None] * inv], axis=-1)
    cos, sin = jnp.cos(ang), jnp.sin(ang)
    zero = jnp.zeros_like(sin)
    cos_h = jnp.concatenate([cos, cos], axis=-1)
    sina_h = jnp.concatenate([-sin, zero], axis=-1)
    sinb_h = jnp.concatenate([zero, sin], axis=-1)
    two = lambda t: jnp.concatenate([t, t], axis=-1)
    return two(cos_h), two(sina_h), two(sinb_h)


SSD_PACK = 32


def _split3(x):
    hi = x.astype(bf16)
    r1 = x - hi.astype(f32)
    mid = r1.astype(bf16)
    lo = (r1 - mid.astype(f32)).astype(bf16)
    return hi, mid, lo


def _pack3(x):
    hi, mid, lo = _split3(x)
    lane = lax.broadcasted_iota(jnp.int32, x.shape, 1)
    mid_r = pltpu.roll(mid.astype(f32), SSD_PACK, axis=1)
    lo_r = pltpu.roll(lo.astype(f32), 2 * SSD_PACK, axis=1)
    packed = jnp.where(lane < SSD_PACK, hi.astype(f32),
                       jnp.where(lane < 2 * SSD_PACK, mid_r,
                                 jnp.where(lane < 3 * SSD_PACK, lo_r, 0.0)))
    return packed.astype(bf16)


def _ssd_selectors():
    k = np.arange(LANES)
    src = np.where(k < 3 * SSD_PACK, k % SSD_PACK, -1)
    col_blk = np.arange(SSD_PACK * SSD_CHUNK) // SSD_CHUNK
    sel_bc = (src[:, None] == col_blk[None, :])
    head = np.arange(SSD_INNER) // SSD_HEAD_DIM
    sel_f = (src[:, None] == head[None, :])
    sel_b = (src[:, None] == (head + SSD_HEADS)[None, :])
    tri3 = np.tile(np.tril(np.ones((SSD_CHUNK, SSD_CHUNK))), (1, 3))
    rows = np.arange(SSD_CHUNK)[:, None]
    cols = np.arange(SSD_CHUNK + 32)[None, :]
    half = SSD_CONV // 2
    shift = np.concatenate([cols == rows + 16 + d for d in range(-half, half + 1) if d != 0], axis=0)
    as_bf = lambda a: jnp.asarray(a.astype(np.float32), dtype=bf16)
    return as_bf(sel_bc), as_bf(sel_f), as_bf(sel_b), as_bf(tri3), as_bf(shift)


def _ssd_kernel(seq_len, xbc_ref, dt_ref, z_ref, cw_ref, cb_ref, dtb_ref, alog_ref, dskip_ref, nw_ref,
                selbc_ref, self_ref, selb_ref, tri3_ref, shift_ref, h0_ref, out_ref, hfin_ref,
                xs_s, bc_s, dt_s, y_s, hf_s, hb_s, win_s):
    q = SSD_CHUNK
    nc = seq_len // q
    halo = 16
    H, P, N = SSD_HEADS, SSD_HEAD_DIM, SSD_STATE
    gw = (H // SSD_GROUPS) * P
    a2_row = -jnp.exp(alog_ref[...]) * math.log2(math.e)

    def conv_chunk(c):
        r0 = pl.multiple_of(c * q, q)
        pstart = pl.multiple_of(jnp.maximum(r0 - halo, 0), halo)
        nstart = pl.multiple_of(jnp.minimum(r0 + q, seq_len - halo), halo)
        zero = jnp.zeros((), bf16)
        win_s[0:halo, :] = jnp.where(c > 0, xbc_ref[pl.ds(pstart, halo), :], zero)
        win_s[halo:halo + q, :] = xbc_ref[pl.ds(r0, q), :]
        win_s[halo + q:, :] = jnp.where(c < nc - 1, xbc_ref[pl.ds(nstart, halo), :], zero)
        taps = [k for k in range(SSD_CONV) if k != SSD_CONV // 2]
        cw = 2 * LANES
        for j in range(SSD_XBC // cw):
            cs = slice(j * cw, (j + 1) * cw)
            sh = jnp.dot(shift_ref[...], win_s[:, cs], preferred_element_type=f32)
            acc = cb_ref[:, cs] + win_s[halo:halo + q, cs].astype(f32) * cw_ref[SSD_CONV // 2:SSD_CONV // 2 + 1, cs]
            for n, k in enumerate(taps):
                acc = acc + sh[n * q:(n + 1) * q, :] * cw_ref[k:k + 1, cs]
            act = _silu(acc)
            if j < SSD_INNER // cw:
                xs_s[pl.ds(r0, q), cs] = act
            else:
                bc_s[pl.ds(r0, q), j * cw - SSD_INNER:(j + 1) * cw - SSD_INNER] = act.astype(bf16)
        dt_s[pl.ds(r0, q), :] = jax.nn.softplus(dt_ref[pl.ds(r0, q), :] + dtb_ref[...])

    conv_chunk(0)
    hf_s[...] = h0_ref[0]
    hb_s[...] = h0_ref[1]

    ri = lax.broadcasted_iota(jnp.int32, (q, q), 0)
    ci = lax.broadcasted_iota(jnp.int32, (q, q), 1)
    lower = ri >= ci
    upper = ci >= ri
    lane = lax.broadcasted_iota(jnp.int32, (q, LANES), 1)
    lo_half = lane < P

    def cumsums(dt):
        dta = dt * a2_row
        cf = jnp.dot(tri3_ref[...], jnp.concatenate(_split3(dta), axis=0), preferred_element_type=f32)
        rb = cf[q - 1:q, :] - cf + dta
        return cf, rb

    def load_chunk(r0):
        dt = dt_s[pl.ds(r0, q), :]
        xs = xs_s[pl.ds(r0, q), :]
        bcv = bc_s[pl.ds(r0, q), :]
        bmat = [bcv[:, g * N:(g + 1) * N] for g in range(SSD_GROUPS)]
        cmat = [bcv[:, SSD_BC + g * N:SSD_BC + (g + 1) * N] for g in range(SSD_GROUPS)]
        return dt, xs, bmat, cmat

    def inter_chunk(h_s, sel_ref, decay, weight, xs, bmat, cmat, dec_idx):
        ew = jnp.dot(jnp.concatenate([_pack3(decay), _pack3(weight)], axis=0), sel_ref[...],
                     preferred_element_type=f32)
        e_x, w_x = ew[:q], ew[q:]
        hb_ = h_s[...].astype(bf16)
        yoff = jnp.concatenate(
            [jnp.dot(cmat[g], hb_[:, g * gw:(g + 1) * gw], preferred_element_type=f32)
             for g in range(SSD_GROUPS)], axis=1)
        xw = (xs * w_x).astype(bf16)
        dec_row = e_x[dec_idx:dec_idx + 1, :]
        for g in range(SSD_GROUPS):
            gs = slice(g * gw, (g + 1) * gw)
            bt = jnp.transpose(bmat[g].astype(f32)).astype(bf16)
            upd = jnp.dot(bt, xw[:, gs], preferred_element_type=f32)
            h_s[:, gs] = h_s[:, gs] * dec_row[:, gs] + upd
        return yoff * e_x

    def finish(r0, y, xs):
        yy = y + xs * dskip_ref[...]
        zz = z_ref[pl.ds(r0, q), :].astype(f32)
        gated = yy * _silu(zz)
        ms = jnp.mean(gated * gated, axis=-1, keepdims=True)
        out_ref[pl.ds(r0, q), :] = (gated * lax.rsqrt(ms + EPS) * nw_ref[...]).astype(out_ref.dtype)

    def fwd_chunk(c, second_half):
        r0 = pl.multiple_of(c * q, q)
        dt, xs, bmat, cmat = load_chunk(r0)
        cf, rb = cumsums(dt)
        pcol = jnp.where(lane < H, cf, rb)
        bcast = jnp.dot(_pack3(pcol), selbc_ref[...], preferred_element_type=f32)
        prow = jnp.transpose(pcol - jnp.log2(dt))
        cbm = [lax.dot_general(cmat[g], bmat[g], (((1,), (1,)), ((), ())), preferred_element_type=f32)
               for g in range(SSD_GROUPS)]
        xsb = xs.astype(bf16)
        zero_b = jnp.zeros((), bf16)
        ypairs = []
        for k in range(H // 2):
            mats = []
            for h in (2 * k, 2 * k + 1):
                g = h // (H // SSD_GROUPS)
                hb_ = H + h
                segf = bcast[:, h * q:(h + 1) * q] - prow[h:h + 1, :]
                segb = bcast[:, hb_ * q:(hb_ + 1) * q] - prow[hb_:hb_ + 1, :]
                df = jnp.exp2(jnp.where(lower, segf, NEG_INF))
                db = jnp.exp2(jnp.where(upper, segb, NEG_INF))
                mats.append((cbm[g] * (df + db)).astype(bf16))
            xp = xsb[:, k * LANES:(k + 1) * LANES]
            xbd = jnp.concatenate([jnp.where(lo_half, xp, zero_b), jnp.where(lo_half, zero_b, xp)], axis=0)
            ypairs.append(jnp.dot(jnp.concatenate(mats, axis=1), xbd, preferred_element_type=f32))
        y = jnp.concatenate(ypairs, axis=1)
        wfa = jnp.exp2(cf[q - 1:q, :] - cf) * dt
        y = y + inter_chunk(hf_s, self_ref, jnp.exp2(cf), wfa, xs, bmat, cmat, q - 1)
        if second_half:
            finish(r0, y_s[pl.ds(r0, q), :] + y, xs)
        else:
            y_s[pl.ds(r0, q), :] = y

    def bwd_chunk(c, second_half):
        r0 = pl.multiple_of(c * q, q)
        dt, xs, bmat, cmat = load_chunk(r0)
        _, rb = cumsums(dt)
        wba = jnp.exp2(rb[0:1, :] - rb) * dt
        y = inter_chunk(hb_s, selb_ref, jnp.exp2(rb), wba, xs, bmat, cmat, 0)
        if second_half:
            finish(r0, y_s[pl.ds(r0, q), :] + y, xs)
        else:
            y_s[pl.ds(r0, q), :] = y

    half = nc // 2
    conv_chunk(nc - 1)

    def first_half(i, carry):
        fwd_chunk(i, False)
        bwd_chunk(nc - 1 - i, False)
        conv_chunk(i + 1)
        conv_chunk(nc - 2 - i)
        return carry

    def second_half(i, carry):
        fwd_chunk(i, True)
        bwd_chunk(nc - 1 - i, True)
        return carry

    lax.fori_loop(0, half - 1, first_half, 0)
    fwd_chunk(jnp.int32(half - 1), False)
    bwd_chunk(jnp.int32(half), False)
    lax.fori_loop(half, nc, second_half, 0)
    hfin_ref[0] = hf_s[...]
    hfin_ref[1] = hb_s[...]


def _ssd(xbc, dt, z, conv_w, conv_b, dtb, alog, dskip, nw, h0, bsz, seq_len):
    assert seq_len % (2 * SSD_CHUNK) == 0, "the two recurrences meet in the middle: even chunk count"
    one = pl.Buffered(1)
    seq = lambda w: pl.BlockSpec((seq_len, w), lambda b: (b, 0), pipeline_mode=one)
    const = lambda r, w: pl.BlockSpec((r, w), lambda b: (0, 0))
    st = pl.BlockSpec((None, 2, SSD_STATE, SSD_INNER), lambda b: (b, 0, 0, 0))
    sels = _ssd_selectors()
    return pl.pallas_call(
        functools.partial(_ssd_kernel, seq_len),
        out_shape=[jax.ShapeDtypeStruct((bsz * seq_len, SSD_INNER), bf16),
                   jax.ShapeDtypeStruct((bsz, 2, SSD_STATE, SSD_INNER), f32)],
        grid=(bsz,),
        in_specs=[seq(SSD_XBC), seq(LANES), seq(SSD_INNER),
                  const(8, SSD_XBC), const(1, SSD_XBC), const(1, LANES), const(1, LANES),
                  const(1, SSD_INNER), const(1, SSD_INNER)]
                 + [const(*s.shape) for s in sels] + [st],
        out_specs=[seq(SSD_INNER), st],
        scratch_shapes=[pltpu.VMEM((seq_len, SSD_INNER), f32),
                        pltpu.VMEM((seq_len, 2 * SSD_BC), bf16),
                        pltpu.VMEM((seq_len, LANES), f32),
                        pltpu.VMEM((seq_len, SSD_INNER), f32),
                        pltpu.VMEM((SSD_STATE, SSD_INNER), f32),
                        pltpu.VMEM((SSD_STATE, SSD_INNER), f32),
                        pltpu.VMEM((SSD_CHUNK + 32, SSD_XBC), bf16)],
        compiler_params=_cparams(("parallel",)),
        name="ssd",
    )(xbc, dt, z, conv_w, conv_b, dtb, alog, dskip, nw, *sels, h0)


def _attn_kernel(n_blocks, nsub, local, q_ref, g_ref, k_ref, kc_ref, *rest):
    nv = nsub + 2 if local else 0
    v_refs = rest[:nv]
    vc_ref, sink_ref, o_ref, s_s = rest[nv:]
    t = ATT_BLOCK
    i0 = pl.program_id(1) * nsub
    rpk = ATT_HEADS // ATT_KV_HEADS
    lane = lax.broadcasted_iota(jnp.int32, (t, LANES), 1)
    lo_half = lane < ATT_HEAD_DIM
    zero_b = jnp.zeros((), bf16)
    kl = lax.broadcasted_iota(jnp.int32, (t, t), 0)
    ql = lax.broadcasted_iota(jnp.int32, (t, t), 1)
    for sub in range(nsub):
        i = i0 + sub
        qv = q_ref[sub * t:(sub + 1) * t, :]
        if local:
            p0 = pl.multiple_of(jnp.maximum(i - 1, 0) * t, t)
            c0 = pl.multiple_of(i * t, t)
            n0 = pl.multiple_of(jnp.minimum(i + 1, n_blocks - 1) * t, t)
            bias_prev = jnp.where((kl >= ql) & (i > 0), 0.0, NEG_INF)
            bias_next = jnp.where((kl <= ql) & (i < n_blocks - 1), 0.0, NEG_INF)
            bias_prev = jnp.concatenate([bias_prev] * rpk, axis=1)
            bias_next = jnp.concatenate([bias_next] * rpk, axis=1)
        for j in range(ATT_KV_HEADS):
            ls = slice(j * LANES, (j + 1) * LANES)
            if local:
                kk = jnp.concatenate([k_ref[pl.ds(p0, t), ls], k_ref[pl.ds(c0, t), ls],
                                      k_ref[pl.ds(n0, t), ls], kc_ref[:, ls]], axis=0)
            else:
                kk = kc_ref[:, ls]
            pieces = []
            for r in range(rpk):
                hq = j * rpk + r
                qp = qv[:, (hq // 2) * LANES:(hq // 2 + 1) * LANES]
                keep = lo_half if hq % 2 == 0 else jnp.logical_not(lo_half)
                pieces.append(jnp.where(keep, qp, zero_b))
            q4 = jnp.concatenate(pieces, axis=0)
            s = lax.dot_general(kk, q4, (((1,), (1,)), ((), ())), preferred_element_type=f32)
            if local:
                s = jnp.concatenate([s[:t] + bias_prev, s[t:2 * t], s[2 * t:3 * t] + bias_next, s[3 * t:]],
                                    axis=0)
            s_s[sub, j] = s
    for sub in range(nsub):
        outs = []
        for j in range(ATT_KV_HEADS):
            ls = slice(j * LANES, (j + 1) * LANES)
            if local:
                vvt = jnp.concatenate([v_refs[sub + k][ls, :] for k in range(3)] + [vc_ref[ls, :]], axis=1)
            else:
                vvt = vc_ref[ls, :]
            sk = jnp.concatenate([jnp.full((1, t), sink_ref[j * rpk + r] * LOG2E, f32) for r in range(rpk)],
                                 axis=1)
            s = s_s[sub, j]
            m = jnp.maximum(jnp.max(s, axis=0, keepdims=True), sk)
            p = jnp.exp2(s - m)
            den = jnp.sum(p, axis=0, keepdims=True) + jnp.exp2(sk - m)
            ot = jnp.dot(vvt, p.astype(bf16), preferred_element_type=f32) / den
            o4 = [jnp.transpose(ot[:, r * t:(r + 1) * t]) for r in range(rpk)]
            outs.append(jnp.where(lo_half, o4[0], o4[1]))
            outs.append(jnp.where(lo_half, o4[2], o4[3]))
        o = jnp.concatenate(outs, axis=1)
        gv = g_ref[sub * t:(sub + 1) * t, :].astype(f32)
        o_ref[sub * t:(sub + 1) * t, :] = (o * _silu(gv)).astype(o_ref.dtype)


def _attention(q, g, k, vt, kc, vct, sink, bsz, seq_len, local):
    t = ATT_BLOCK
    nb = seq_len // t
    nsub = max(s for s in range(1, ATT_SUB + 1) if nb % s == 0)
    n_ctx = kc.shape[0] // bsz
    kw = ATT_KV_HEADS * LANES
    blk = pl.BlockSpec((nsub * t, ATT_Q), lambda b, i: (b * (nb // nsub) + i, 0))
    full = lambda n: pl.BlockSpec((n, kw), lambda b, i: (b, 0))
    vblk = lambda off: pl.BlockSpec((kw, t), lambda b, i: (0, b * nb + jnp.clip(i * nsub + off, 0, nb - 1)))
    vspecs = [vblk(off) for off in range(-1, nsub + 1)] if local else []
    return pl.pallas_call(
        functools.partial(_attn_kernel, nb, nsub, local),
        out_shape=jax.ShapeDtypeStruct((bsz * seq_len, ATT_Q), bf16),
        grid=(bsz, nb // nsub),
        in_specs=[blk, blk, full(k.shape[0] // bsz), full(n_ctx)] + vspecs
                 + [pl.BlockSpec((kw, n_ctx), lambda b, i: (0, b)), pl.BlockSpec(memory_space=pltpu.SMEM)],
        out_specs=blk,
        scratch_shapes=[pltpu.VMEM((nsub, ATT_KV_HEADS, (3 * t if local else 0) + n_ctx,
                                    (ATT_HEADS // ATT_KV_HEADS) * t), f32)],
        compiler_params=_cparams(("parallel", "arbitrary")),
        name="attention",
    )(q, g, k, kc, *([vt] * len(vspecs)), vct, sink)


def _s5_disc_kernel(lre_ref, lim_ref, ls_ref, bre_ref, bim_ref, abre_ref, abim_ref, bbre_ref, bbim_ref):
    lam_re = lre_ref[...]
    lam_im = lim_ref[...]
    dt = jnp.exp(ls_ref[...])
    mag = jnp.exp(lam_re * dt)
    ab_re = mag * jnp.cos(lam_im * dt)
    ab_im = mag * jnp.sin(lam_im * dt)
    num_re, num_im = ab_re - 1.0, ab_im
    den = lam_re * lam_re + lam_im * lam_im
    coef_re = (num_re * lam_re + num_im * lam_im) / den
    coef_im = (num_im * lam_re - num_re * lam_im) / den
    b_re, b_im = bre_ref[...], bim_ref[...]
    abre_ref[...] = ab_re
    abim_ref[...] = ab_im
    bbre_ref[...] = coef_re * b_re - coef_im * b_im
    bbim_ref[...] = coef_re * b_im + coef_im * b_re


def _s5_discretise(lam_re, lam_im, log_step, b_re, b_im):
    g, n, cg = b_re.shape
    exp = lambda t: jnp.repeat(t.reshape(2 * g, n), cg, axis=1)
    ls = jnp.broadcast_to(log_step.reshape(2 * g, 1), (2 * g, n * cg))
    bb = lambda t: jnp.tile(t.reshape(g, n * cg), (2, 1))
    shp = jax.ShapeDtypeStruct((2 * g, n * cg), f32)
    ab_re, ab_im, bb_re, bb_im = pl.pallas_call(
        _s5_disc_kernel, out_shape=[shp] * 4, name="s5_disc",
    )(exp(lam_re), exp(lam_im), ls, bb(b_re), bb(b_im))
    first = lambda t: t.reshape(2, g, n, cg)[..., 0]
    full = lambda t: t.reshape(2, g, n, cg)
    return first(ab_re), first(ab_im), full(bb_re), full(bb_im)


def _cmul(ar, ai, br, bi):
    return ar * br - ai * bi, ar * bi + ai * br


def _s5_kernel(n_lat, n_ctx, *refs):
    tc = S5_TC
    xl, xc, arow_ref, acol_ref, bbd_ref, cbd_ref, y_ref, wyz_s, ws_s, sl_s, sc_s = refs
    hw = S5_HGRP * S5_STATE
    hl = LANES // 2
    xw = tc * hl
    nsl = hw // LANES
    nb = 8
    lo = lax.broadcasted_iota(jnp.int32, (1, LANES), 1) < hl

    def powers(re, im, n):
        out = [(jnp.ones_like(re), jnp.zeros_like(im))]
        for _ in range(n):
            out.append(_cmul(out[-1][0], out[-1][1], re, im))
        return out

    steps = lambda h, p: (2 * p, 2 * p + 1) if h == 0 else (2 * p + 1, 2 * p)

    prow = []
    for h in range(2):
        pr = [powers(arow_ref[h, 2 * d:2 * d + 1, :], arow_ref[h, 2 * d + 1:2 * d + 2, :], tc) for d in range(2)]
        prow.append(pr)
        for s in range(tc):
            for d, k in ((0, tc - 1 - s), (1, s)):
                wr, wi = _cmul(bbd_ref[2 * d, h], bbd_ref[2 * d + 1, h], *pr[d][k])
                ws_s[h, s * hl:(s + 1) * hl, d * 2 * hw:d * 2 * hw + hw] = wr.astype(bf16)
                ws_s[h, s * hl:(s + 1) * hl, d * 2 * hw + hw:(d + 1) * 2 * hw] = wi.astype(bf16)
        crhs = jnp.concatenate([cbd_ref[0, h], -cbd_ref[1, h]], axis=0).astype(bf16)
        kall = [jnp.dot(ws_s[h, :, d * 2 * hw:(d + 1) * 2 * hw], crhs, preferred_element_type=f32)
                for d in range(2)]
        kf = [kall[0][(tc - 1 - k) * hl:(tc - k) * hl] for k in range(tc)]
        kb = [kall[1][k * hl:(k + 1) * hl] for k in range(tc)]
        lag = lambda s, t: kf[t - s] if t > s else (kb[s - t] if t < s else kf[0] + kb[0])
        for s in range(tc):
            for p in range(tc // 2):
                t0, t1 = steps(h, p)
                wyz_s[h, s * hl:(s + 1) * hl, p * LANES:(p + 1) * LANES] = (
                    jnp.where(lo, lag(s, t0), lag(s, t1)).astype(bf16))
        for d in range(2):
            pc = powers(acol_ref[h, 2 * d], acol_ref[h, 2 * d + 1], tc)
            kk = (lambda t: t + 1) if d == 0 else (lambda t: tc - t)
            r0 = xw + d * 2 * hw
            for p in range(tc // 2):
                t0, t1 = steps(h, p)
                ar = jnp.where(lo, pc[kk(t0)][0], pc[kk(t1)][0])
                ai = jnp.where(lo, pc[kk(t0)][1], pc[kk(t1)][1])
                dre, dim_ = _cmul(cbd_ref[0, h], cbd_ref[1, h], ar, ai)
                wyz_s[h, r0:r0 + hw, p * LANES:(p + 1) * LANES] = dre.astype(bf16)
                wyz_s[h, r0 + hw:r0 + 2 * hw, p * LANES:(p + 1) * LANES] = (-dim_).astype(bf16)

    def rows_of(x_ref, b, n, h):
        return x_ref[b * n:(b + 1) * n, h * xw:(h + 1) * xw]

    def inject(x_ref, s_ref, n):
        for b in range(nb):
            for h in range(2):
                sb = jnp.dot(rows_of(x_ref, b, n, h), ws_s[h], preferred_element_type=f32)
                for k in range(4 * nsl):
                    s_ref[h * 4 * nsl + k, pl.ds(b, n, stride=nb), :] = sb[:, k * LANES:(k + 1) * LANES]

    at = [[[tuple(jnp.broadcast_to(p[:, k * LANES:(k + 1) * LANES], (nb, LANES)) for p in prow[h][d][tc])
            for k in range(nsl)] for d in range(2)] for h in range(2)]

    def scan(s_ref, n, init):
        def step(i, carry):
            new = []
            for h in range(2):
                for d in range(2):
                    idx = pl.ds(pl.multiple_of((i if d == 0 else n - 1 - i) * nb, nb), nb)
                    for k in range(nsl):
                        c0 = 2 * ((h * 2 + d) * nsl + k)
                        hr, hi = carry[c0], carry[c0 + 1]
                        kr = h * 4 * nsl + d * 2 * nsl + k
                        ki = kr + nsl
                        sr, si = s_ref[kr, idx, :], s_ref[ki, idx, :]
                        s_ref[kr, idx, :] = hr
                        s_ref[ki, idx, :] = hi
                        ar, ai = at[h][d][k]
                        new += [ar * hr - ai * hi + sr, ar * hi + ai * hr + si]
            return tuple(new)
        return lax.fori_loop(0, n, step, init, unroll=4)

    inject(xc, sc_s, n_ctx)
    h_ctx = scan(sc_s, n_ctx, tuple(jnp.zeros((nb, LANES), f32) for _ in range(8 * nsl)))
    inject(xl, sl_s, n_lat)
    scan(sl_s, n_lat, h_ctx)

    for b in range(nb):
        yh = []
        for h in range(2):
            hin = jnp.concatenate([sl_s[h * 4 * nsl + k, pl.ds(b, n_lat, stride=nb), :]
                                   for k in range(4 * nsl)], axis=1).astype(bf16)
            yh.append(jnp.dot(jnp.concatenate([rows_of(xl, b, n_lat, h), hin], axis=1), wyz_s[h],
                              preferred_element_type=f32))
        for p in range(tc // 2):
            y0, y1 = yh[0][:, p * LANES:(p + 1) * LANES], yh[1][:, p * LANES:(p + 1) * LANES]
            even = jnp.where(lo, y0, y1)
            odd = pltpu.roll(jnp.where(lo, y1, y0), hl, axis=1)
            y_ref[pl.ds(b * n_lat * tc + 2 * p, n_lat, stride=tc), :] = even
            y_ref[pl.ds(b * n_lat * tc + 2 * p + 1, n_lat, stride=tc), :] = odd


def _s5_mix(u, u_c, arow, acol, bbd, cbd, bsz, seq_len, n_ctx_tok):
    assert bsz == 8, "the chunk recurrence puts the batch on the 8 sublanes"
    tc = S5_TC
    n_lat, n_ctx = seq_len // tc, n_ctx_tok // tc
    nblk = S5_GROUPS // S5_GBLK
    hw = S5_HGRP * S5_STATE
    hl = LANES // 2
    nsl = hw // LANES
    one = pl.Buffered(1)
    xspec = lambda rows: pl.BlockSpec((None, rows, tc * LANES), lambda g: (g, 0, 0))
    return pl.pallas_call(
        functools.partial(_s5_kernel, n_lat, n_ctx),
        out_shape=jax.ShapeDtypeStruct((nblk, bsz * seq_len, LANES), f32),
        grid=(nblk,),
        in_specs=[xspec(bsz * n_lat), xspec(bsz * n_ctx),
                  pl.BlockSpec((2, 4, hw), lambda g: (g, 0, 0), pipeline_mode=one),
                  pl.BlockSpec((2, 4, hw, LANES), lambda g: (g, 0, 0, 0), pipeline_mode=one),
                  pl.BlockSpec((4, 2, hl, hw), lambda g: (0, g, 0, 0), pipeline_mode=one),
                  pl.BlockSpec((2, 2, hw, LANES), lambda g: (0, g, 0, 0), pipeline_mode=one)],
        out_specs=pl.BlockSpec((None, bsz * seq_len, LANES), lambda g: (g, 0, 0), pipeline_mode=one),
        scratch_shapes=[pltpu.VMEM((2, tc * hl + 4 * hw, tc * hl), bf16),
                        pltpu.VMEM((2, tc * hl, 4 * hw), bf16),
                        pltpu.VMEM((8 * nsl, bsz * n_lat, LANES), f32),
                        pltpu.VMEM((8 * nsl, bsz * n_ctx, LANES), f32)],
        compiler_params=_cparams(("arbitrary",)),
        name="s5_mix",
    )(u, u_c, arow, acol, bbd, cbd)


def _s5_block_params(ab_re, ab_im, bb_re, bb_im, c_re, c_im):
    g, n, cg = S5_GROUPS, S5_STATE, S5_GROUP_CH
    nh = g // S5_HGRP
    hw = S5_HGRP * n
    arow = jnp.stack([t[d].reshape(nh, hw) for d in range(2) for t in (ab_re, ab_im)], axis=1)
    acol = jnp.broadcast_to(arow[..., None], (nh, 4, hw, LANES))

    def blockdiag(t, rows_per, cols_per):
        tiled = jnp.concatenate([t] * S5_HGRP, axis=-1)
        r = np.arange(t.shape[-2])[:, None] // rows_per
        c = np.arange(S5_HGRP * cols_per)[None, :] // cols_per
        return jnp.where(jnp.asarray(r == c), tiled, 0.0)

    bb = jnp.stack([t[d] for d in range(2) for t in (bb_re, bb_im)], axis=0)
    bb = jnp.swapaxes(bb.reshape(4, nh, S5_HGRP, n, cg), -1, -2).reshape(4, nh, S5_HGRP * cg, n)
    bbd = blockdiag(bb, cg, n)
    cc = jnp.swapaxes(jnp.stack([c_re, c_im], axis=0).reshape(2, nh, S5_HGRP, cg, n), -1, -2)
    cbd = blockdiag(cc.reshape(2, nh, hw, cg), n, cg)
    return arow, acol, bbd, jnp.concatenate([cbd, cbd], axis=-1)


def _s5_out_kernel(y_ref, u_ref, g_ref, x_ref, dskip_ref, gw_ref, gb_ref, w_ref, gate_ref,
                   fw_ref, o_ref):
    y = jnp.concatenate([y_ref[j] for j in range(y_ref.shape[0])], axis=1)
    y = y + dskip_ref[...] * u_ref[...].astype(f32)
    y = jax.nn.gelu(y)
    glu = jnp.dot(y.astype(bf16), gw_ref[...], preferred_element_type=f32) + gb_ref[...]
    y = y * jax.nn.sigmoid(glu)
    y = y * _silu(g_ref[...].astype(f32))
    x = x_ref[...] + gate_ref[...] * jnp.dot(y.astype(bf16), w_ref[...], preferred_element_type=f32)
    ms = jnp.mean(x * x, axis=-1, keepdims=True)
    o_ref[...] = x * lax.rsqrt(ms + EPS) * fw_ref[...]


def _s5_out(y, u, g, x, dskip, gw, gb, w, gate, fw, rows_per_mod):
    m, d = x.shape
    tm = min(ROW_TILE, rows_per_mod)
    per = rows_per_mod // tm
    row = lambda: pl.BlockSpec((tm, d), lambda i: (i, 0))
    vec = lambda: pl.BlockSpec((1, d), lambda i: (0, 0))
    mat = lambda: pl.BlockSpec((d, d), lambda i: (0, 0))
    return pl.pallas_call(
        _s5_out_kernel,
        out_shape=jax.ShapeDtypeStruct((m, d), f32),
        grid=(m // tm,),
        in_specs=[pl.BlockSpec((y.shape[0], tm, LANES), lambda i: (0, i, 0)),
                  row(), row(), row(), vec(), mat(), vec(), mat(),
                  pl.BlockSpec((None, 1, d), lambda i: (i // per, 0, 0)), vec()],
        out_specs=row(),
        compiler_params=_cparams(("parallel",)),
        name="s5_out",
    )(y, u, g, x, dskip, gw, gb, w, gate, fw)


def _even_weights(w_in):
    o = 0
    z = w_in[:, o:o + SSD_INNER]; o += SSD_INNER
    xbc = w_in[:, o:o + SSD_XBC]; o += SSD_XBC
    dt = w_in[:, o:o + 2 * SSD_HEADS]; o += 2 * SSD_HEADS
    q = w_in[:, o:o + ATT_Q]; o += ATT_Q
    k = w_in[:, o:o + ATT_KVW]; o += ATT_KVW
    v = w_in[:, o:o + ATT_KVW]; o += ATT_KVW
    g = w_in[:, o:o + ATT_Q]
    d = w_in.shape[0]
    dup = lambda t: jnp.concatenate([t.reshape(d, ATT_KV_HEADS, 1, ATT_HEAD_DIM)] * 2, axis=2).reshape(d, -1)
    dtp = jnp.pad(dt, ((0, 0), (0, LANES - 2 * SSD_HEADS)))
    cast = lambda t: t.astype(bf16)
    return [cast(t) for t in (z, xbc, q, dup(k), g, dtp)], cast(dup(v).T)


def _even_segs(rope):
    scale = ATT_HEAD_DIM ** -0.5 * LOG2E
    widths = [(SSD_INNER, None, bf16), (SSD_XBC, None, bf16),
              (ATT_Q, scale, bf16),
              (ATT_KV_HEADS * LANES, 1.0 if rope else None, bf16),
              (ATT_Q, None, bf16), (LANES, None, f32)]
    return [(i, w, r, dtp, False) for i, (w, r, dtp) in enumerate(widths)]


def _pad_lanes(v, n=LANES):
    v = v.reshape(1, -1)
    return jnp.pad(v, ((0, 0), (0, n - v.shape[1])))


def kernel(x, c, ctx, c_ctx, e_norm_w, e_ada_w, e_ada_b, e_w_in, e_conv_w, e_conv_b, e_dt_bias,
           e_a_log, e_d_skip, e_ssd_norm_w, e_sink, e_w_out, o_norm_w, o_ada_w, o_ada_b, o_w_in,
           o_lam_re, o_lam_im, o_log_step, o_b_re, o_b_im, o_c_re, o_c_im, o_d_skip, o_glu_w,
           o_glu_b, o_w_out, final_norm_w):
    bsz, seq_len, d = x.shape
    n_ctx = ctx.shape[1]
    xf = x.reshape(bsz * seq_len, d)
    xcf = ctx.reshape(bsz * n_ctx, d)

    cvecs = jnp.concatenate([c, c_ctx[None, :], jnp.zeros((16 - bsz - 1, d), f32)], axis=0)

    def modulation(ada_w, ada_b):
        mod = _adaln(cvecs, ada_w.astype(bf16), ada_b.reshape(1, -1))
        parts = [mod[:, k * d:(k + 1) * d] for k in range(3)]
        lat = [p[:bsz].reshape(bsz, 1, d) for p in parts]
        cx = [p[bsz:bsz + 1].reshape(1, 1, d) for p in parts]
        return lat, cx

    (shift, scale, gate), (shift_c, scale_c, gate_c) = modulation(e_ada_w[0], e_ada_b[0])
    w_in, w_vt = _even_weights(e_w_in[0])
    nw = e_norm_w[0].reshape(1, d)
    tabs = _rope_tables(seq_len)
    kw = ATT_KV_HEADS * LANES
    vseg = [(0, kw, bf16)]
    z, xbc, q, k, g, dt, vt = _inproj(xf, shift, scale, nw, w_in, _even_segs(True), seq_len, tabs, w_vt, vseg)
    z_c, xbc_c, q_c, k_c, g_c, dt_c, vt_c = _inproj(xcf, shift_c, scale_c, nw, w_in, _even_segs(False), n_ctx,
                                                    None, w_vt, vseg)

    conv_w = jnp.pad(e_conv_w[0], ((0, 8 - SSD_CONV), (0, 0)))
    conv_b = e_conv_b[0].reshape(1, -1)
    dtb = _pad_lanes(e_dt_bias[0])
    alog = _pad_lanes(e_a_log[0])
    dskip = jnp.repeat(e_d_skip[0], SSD_HEAD_DIM).reshape(1, -1)
    snw = e_ssd_norm_w[0].reshape(1, -1)
    h0 = jnp.zeros((bsz, 2, SSD_STATE, SSD_INNER), f32)
    ssd_c, hfin = _ssd(xbc_c, dt_c, z_c, conv_w, conv_b, dtb, alog, dskip, snw, h0, bsz, n_ctx)
    ssd_o, _ = _ssd(xbc, dt, z, conv_w, conv_b, dtb, alog, dskip, snw, hfin, bsz, seq_len)

    sink = e_sink[0]
    att = _attention(q, g, k, vt, k_c, vt_c, sink, bsz, seq_len, True)
    att_c = _attention(q_c, g_c, k_c, vt_c, k_c, vt_c, sink, bsz, n_ctx, False)
    w_out = e_w_out[0].astype(bf16)
    even_gate, even_gate_c = gate, gate_c

    (shift, scale, gate), (shift_c, scale_c, _) = modulation(o_ada_w[0], o_ada_b[0])
    w_u = o_w_in[0][:, :S5_WIDTH].astype(bf16)
    w_g = o_w_in[0][:, S5_WIDTH:].astype(bf16)
    nw = o_norm_w[0].reshape(1, d)
    x1, u, u_ch, g2 = _inproj(xf, shift, scale, nw, [w_u, w_g],
                              [(0, S5_WIDTH, None, bf16, False), (0, S5_WIDTH, None, bf16, True),
                               (1, S5_WIDTH, None, bf16, False)], seq_len,
                              pre=(ssd_o, att, w_out, even_gate))
    _, uc_ch = _inproj(xcf, shift_c, scale_c, nw, [w_u], [(0, S5_WIDTH, None, bf16, True)], n_ctx,
                       pre=(ssd_c, att_c, w_out, even_gate_c))

    ab_re, ab_im, bb_re, bb_im = _s5_discretise(o_lam_re[0], o_lam_im[0], o_log_step[0], o_b_re[0], o_b_im[0])
    arow, acol, bbd, cbd = _s5_block_params(ab_re, ab_im, bb_re, bb_im, o_c_re[0], o_c_im[0])
    y = _s5_mix(u_ch, uc_ch, arow, acol, bbd, cbd, bsz, seq_len, n_ctx)
    out = _s5_out(y, u, g2, x1, o_d_skip[0].reshape(1, -1),
                  o_glu_w[0].astype(bf16), o_glu_b[0].reshape(1, -1), o_w_out[0].astype(bf16), gate,
                  final_norm_w.reshape(1, -1), seq_len)
    return out.reshape(bsz, seq_len, d)
```

```python
import functools
import math

import jax
import jax.numpy as jnp
import numpy as np
from jax import lax
from jax.experimental import pallas as pl
from jax.experimental.pallas import tpu as pltpu

f32 = jnp.float32
bf16 = jnp.bfloat16

D_MODEL = 1024
GRID_W = 64
EPS = 1e-6
NEG_INF = -1e30

SSD_HEADS = 16
SSD_HEAD_DIM = 64
SSD_GROUPS = 2
SSD_STATE = 128
SSD_CONV = 5
SSD_CHUNK = 128
SSD_INNER = SSD_HEADS * SSD_HEAD_DIM
SSD_BC = SSD_GROUPS * SSD_STATE
SSD_XBC = SSD_INNER + 2 * SSD_BC
ATT_HEADS = 16
ATT_KV_HEADS = 4
ATT_HEAD_DIM = 64
ATT_BLOCK = 128
ROPE_THETA = 10000.0
ATT_Q = ATT_HEADS * ATT_HEAD_DIM
ATT_KVW = ATT_KV_HEADS * ATT_HEAD_DIM
S5_WIDTH = 1024
S5_GROUP_CH = 16
S5_GROUPS = S5_WIDTH // S5_GROUP_CH
S5_STATE = 64

LOG2E = math.log2(math.e)
LANES = 128
ROW_TILE = 512
ATT_SUB = 4
S5_GBLK = 8
S5_TC = 8
S5_HGRP = 4
VMEM_LIMIT = 56 * 1024 * 1024


def _cparams(sem, flags=None):
    return pltpu.CompilerParams(dimension_semantics=sem, vmem_limit_bytes=VMEM_LIMIT, flags=flags)


def _silu(x):
    h = 0.5 * x
    return h + h * jnp.tanh(h)


def _adaln_kernel(c_ref, w_ref, b_ref, o_ref):
    c = c_ref[...]
    s = _silu(c).astype(bf16)
    o_ref[...] = jnp.dot(s, w_ref[...], preferred_element_type=f32) + b_ref[...]


def _adaln(cvecs, w, b):
    r, d = cvecs.shape
    n = w.shape[1]
    tn = 1024
    return pl.pallas_call(
        _adaln_kernel,
        out_shape=jax.ShapeDtypeStruct((r, n), f32),
        grid=(n // tn,),
        in_specs=[pl.BlockSpec((r, d), lambda j: (0, 0)),
                  pl.BlockSpec((d, tn), lambda j: (0, j)),
                  pl.BlockSpec((1, tn), lambda j: (0, j))],
        out_specs=pl.BlockSpec((r, tn), lambda j: (0, j)),
        compiler_params=_cparams(("arbitrary",)),
        name="adaln",
    )(cvecs, w, b)


CONV_HALO = 16


def _inproj_kernel(segs, tsegs, n_w, has_rope, has_pre, conv, x_ref, shift_ref, scale_ref, nw_ref, *rest):
    if has_pre:
        a_ref, b_ref, wo_ref, gate_ref = rest[:4]
        rest = rest[4:]
    if conv is not None:
        xp_ref, xn_ref, cw_ref, cb_ref = rest[:4]
        rest = rest[4:]
    w_refs, rest = rest[:n_w], rest[n_w:]
    if tsegs:
        wt_ref, rest = rest[0], rest[1:]
    if has_rope:
        cos_ref, sina_ref, sinb_ref = rest[:3]
        rest = rest[3:]
    if has_pre:
        x1_ref, rest = rest[0], rest[1:]
    outs = rest[:len(segs)]
    touts = rest[len(segs):len(segs) + len(tsegs)]
    rest = rest[len(segs) + len(tsegs):]
    if conv is not None:
        conv_s, rest = rest[0], rest[1:]
    slab_s = rest[0] if rest else None
    x = x_ref[...]
    tm = x.shape[0]
    if conv is not None:
        x = jnp.concatenate([xp_ref[...], x, xn_ref[...]], axis=0)
    if has_pre:
        ka = a_ref.shape[1]
        acc = jnp.dot(a_ref[...], wo_ref[:ka, :], preferred_element_type=f32)
        acc = acc + jnp.dot(b_ref[...], wo_ref[ka:, :], preferred_element_type=f32)
        x = x + gate_ref[...] * acc
        x1_ref[...] = x
    ms = jnp.mean(x * x, axis=-1, keepdims=True)
    h = (x * lax.rsqrt(ms + EPS)) * nw_ref[...]
    h = h * (1.0 + scale_ref[...]) + shift_ref[...]
    hb = h.astype(bf16)
    if conv is not None:
        cidx, per = conv
        halo = CONV_HALO
        hb_ext, hb = hb, hb[halo:halo + tm]
        widx, width = segs[cidx][0], segs[cidx][1]
        acc = jnp.dot(hb_ext, w_refs[widx][...], preferred_element_type=f32)
        it = pl.program_id(0) % per
        conv_s[0:halo, :] = jnp.where(it > 0, acc[0:halo], 0.0)
        conv_s[halo:halo + tm, :] = acc[halo:halo + tm]
        conv_s[halo + tm:, :] = jnp.where(it < per - 1, acc[halo + tm:], 0.0)
    for (start, width, _), o_ref in zip(tsegs, touts):
        acc_t = lax.dot_general(wt_ref[start:start + width, :], hb, (((1,), (1,)), ((), ())),
                                preferred_element_type=f32)
        o_ref[...] = acc_t.astype(o_ref.dtype)
    products = {}
    for si, ((widx, width, rope, _, chunked), o_ref) in enumerate(zip(segs, outs)):
        if conv is not None and si == conv[0]:
            continue
        if widx not in products:
            products[widx] = jnp.dot(hb, w_refs[widx][...], preferred_element_type=f32)
        acc = products[widx]
        if chunked:
            tm = acc.shape[0]
            hl = LANES // 2
            lo = lax.broadcasted_iota(jnp.int32, (1, LANES), 1) < hl
            for j in range(width // LANES):
                slab_s[j] = acc[:, j * LANES:(j + 1) * LANES]
            for j in range(width // LANES):
                rows = [slab_s[j, pl.ds(s, tm // S5_TC, stride=S5_TC), :] for s in range(S5_TC)]
                for p in range(S5_TC // 2):
                    a, b = rows[2 * p], rows[2 * p + 1]
                    h0 = jnp.where(lo, a, pltpu.roll(b, hl, axis=1))
                    h1 = jnp.where(lo, pltpu.roll(a, hl, axis=1), b)
                    o_ref[j, :, p * LANES:(p + 1) * LANES] = h0.astype(o_ref.dtype)
                    o_ref[j, :, S5_TC * hl + p * LANES:S5_TC * hl + (p + 1) * LANES] = h1.astype(o_ref.dtype)
            continue
        if rope is not None and not has_rope:
            acc = acc * rope
        elif rope is not None:
            rep = width // LANES
            cos = jnp.concatenate([cos_ref[...]] * rep, axis=1) * rope
            sina = jnp.concatenate([sina_ref[...]] * rep, axis=1) * rope
            sinb = jnp.concatenate([sinb_ref[...]] * rep, axis=1) * rope
            half = ATT_HEAD_DIM // 2
            up = pltpu.roll(acc, width - half, axis=1)
            dn = pltpu.roll(acc, half, axis=1)
            acc = acc * cos + up * sina + dn * sinb
        o_ref[...] = acc.astype(o_ref.dtype)
    if conv is not None:
        cidx = conv[0]
        width = segs[cidx][1]
        rb, cbk = SSD_CHUNK, 2 * LANES
        for r in range(tm // rb):
            for j in range(width // cbk):
                cs = slice(j * cbk, (j + 1) * cbk)
                out = cb_ref[:, cs]
                for k in range(SSD_CONV):
                    r0 = CONV_HALO + r * rb + k - SSD_CONV // 2
                    out = out + conv_s[r0:r0 + rb, cs] * cw_ref[k:k + 1, cs]
                outs[cidx][r * rb:(r + 1) * rb, cs] = _silu(out).astype(outs[cidx].dtype)


def _inproj(x, shift, scale, nw, w, segs, rows_per_mod, rope_tabs=None, wt=None, tsegs=(), pre=None,
            conv=None):
    m, d = x.shape
    tm = min(ROW_TILE, rows_per_mod)
    per = rows_per_mod // tm
    assert pre is None or conv is None
    nmod = shift.shape[0]
    mod_idx = (lambda i: (i // per, 0, 0)) if nmod > 1 else (lambda i: (0, 0, 0))
    in_specs = [pl.BlockSpec((tm, d), lambda i: (i, 0)),
                pl.BlockSpec((None, 1, d), mod_idx),
                pl.BlockSpec((None, 1, d), mod_idx),
                pl.BlockSpec((1, d), lambda i: (0, 0))]
    args = [x, shift, scale, nw]
    if pre is not None:
        a, b, w_out, gate = pre
        gate_idx = (lambda i: (i // per, 0, 0)) if gate.shape[0] > 1 else (lambda i: (0, 0, 0))
        in_specs += [pl.BlockSpec((tm, a.shape[1]), lambda i: (i, 0)),
                     pl.BlockSpec((tm, b.shape[1]), lambda i: (i, 0)),
                     pl.BlockSpec(w_out.shape, lambda i: (0, 0)),
                     pl.BlockSpec((None, 1, d), gate_idx)]
        args += [a, b, w_out, gate]
    if conv is not None:
        cidx, conv_w, conv_b = conv
        hb_per_tile, last = tm // CONV_HALO, m // CONV_HALO - 1
        in_specs += [pl.BlockSpec((CONV_HALO, d), lambda i: (jnp.maximum(i * hb_per_tile - 1, 0), 0)),
                     pl.BlockSpec((CONV_HALO, d), lambda i: (jnp.minimum((i + 1) * hb_per_tile, last), 0)),
                     pl.BlockSpec(conv_w.shape, lambda i: (0, 0)),
                     pl.BlockSpec(conv_b.shape, lambda i: (0, 0))]
        args += [x, x, conv_w, conv_b]
    in_specs += [pl.BlockSpec(wi.shape, lambda i: (0, 0)) for wi in w]
    args += list(w)
    if tsegs:
        in_specs.append(pl.BlockSpec(wt.shape, lambda i: (0, 0)))
        args.append(wt)
    if rope_tabs is not None:
        for t in rope_tabs:
            in_specs.append(pl.BlockSpec((tm, LANES), lambda i: (i % per, 0)))
            args.append(t)
    out_shape, out_specs = [], []
    if pre is not None:
        out_shape.append(jax.ShapeDtypeStruct((m, d), f32))
        out_specs.append(pl.BlockSpec((tm, d), lambda i: (i, 0)))
    for sg in segs:
        if sg[4]:
            out_shape.append(jax.ShapeDtypeStruct((sg[1] // LANES, m // S5_TC, S5_TC * LANES), sg[3]))
            out_specs.append(pl.BlockSpec((sg[1] // LANES, tm // S5_TC, S5_TC * LANES), lambda i: (0, i, 0)))
        else:
            out_shape.append(jax.ShapeDtypeStruct((m, sg[1]), sg[3]))
            out_specs.append(pl.BlockSpec((tm, sg[1]), lambda i: (i, 0)))
    for _, width, dtp in tsegs:
        out_shape.append(jax.ShapeDtypeStruct((width, m), dtp))
        out_specs.append(pl.BlockSpec((width, tm), lambda i: (0, i)))
    chunked_w = [sg[1] for sg in segs if sg[4]]
    scratch = [pltpu.VMEM((tm + 2 * CONV_HALO, segs[conv[0]][1]), f32)] if conv is not None else []
    scratch += [pltpu.VMEM((max(chunked_w) // LANES, tm, LANES), f32)] if chunked_w else []
    return pl.pallas_call(
        functools.partial(_inproj_kernel, tuple(segs), tuple(tsegs), len(w), rope_tabs is not None,
                          pre is not None, None if conv is None else (conv[0], per)),
        out_shape=out_shape,
        grid=(m // tm,),
        in_specs=in_specs,
        out_specs=out_specs,
        scratch_shapes=scratch,
        compiler_params=_cparams(("parallel",)),
        name="inproj",
    )(*args)


def _rope_tables(seq_len):
    rows = seq_len // GRID_W
    row = jnp.repeat(jnp.arange(rows, dtype=f32), GRID_W)
    col = jnp.tile(jnp.arange(GRID_W, dtype=f32), rows)
    n_freq = ATT_HEAD_DIM // 4
    inv = ROPE_THETA ** (-jnp.arange(n_freq, dtype=f32) / n_freq)
    ang = jnp.concatenate([row[:, None] * inv, col[:, None] * inv], axis=-1)
    cos, sin = jnp.cos(ang), jnp.sin(ang)
    zero = jnp.zeros_like(sin)
    cos_h = jnp.concatenate([cos, cos], axis=-1)
    sina_h = jnp.concatenate([-sin, zero], axis=-1)
    sinb_h = jnp.concatenate([zero, sin], axis=-1)
    two = lambda t: jnp.concatenate([t, t], axis=-1)
    return two(cos_h), two(sina_h), two(sinb_h)


SSD_PACK = 32


def _split3(x):
    hi = x.astype(bf16)
    r1 = x - hi.astype(f32)
    mid = r1.astype(bf16)
    lo = (r1 - mid.astype(f32)).astype(bf16)
    return hi, mid, lo


def _pack3(x):
    hi, mid, lo = _split3(x)
    lane = lax.broadcasted_iota(jnp.int32, x.shape, 1)
    mid_r = pltpu.roll(mid.astype(f32), SSD_PACK, axis=1)
    lo_r = pltpu.roll(lo.astype(f32), 2 * SSD_PACK, axis=1)
    packed = jnp.where(lane < SSD_PACK, hi.astype(f32),
                       jnp.where(lane < 2 * SSD_PACK, mid_r,
                                 jnp.where(lane < 3 * SSD_PACK, lo_r, 0.0)))
    return packed.astype(bf16)


def _ssd_selectors():
    k = np.arange(LANES)
    src = np.where(k < 3 * SSD_PACK, k % SSD_PACK, -1)
    col_blk = np.arange(SSD_PACK * SSD_CHUNK) // SSD_CHUNK
    sel_bc = (src[:, None] == col_blk[None, :])
    head = np.arange(SSD_INNER) // SSD_HEAD_DIM
    sel_f = (src[:, None] == head[None, :])
    sel_b = (src[:, None] == (head + SSD_HEADS)[None, :])
    tri3 = np.tile(np.tril(np.ones((SSD_CHUNK, SSD_CHUNK))), (1, 3))
    as_bf = lambda a: jnp.asarray(a.astype(np.float32), dtype=bf16)
    return as_bf(sel_bc), as_bf(sel_f), as_bf(sel_b), as_bf(tri3)


def _ssd_kernel(seq_len, xbc_ref, dt_ref, z_ref, dtb_ref, alog_ref, dskip_ref, nw_ref,
                selbc_ref, self_ref, selb_ref, tri3_ref, h0_ref, out_ref, hfin_ref, y_s, hf_s, hb_s):
    q = SSD_CHUNK
    nc = seq_len // q
    H, P, N = SSD_HEADS, SSD_HEAD_DIM, SSD_STATE
    gw = (H // SSD_GROUPS) * P
    a2_row = -jnp.exp(alog_ref[...]) * math.log2(math.e)

    hf_s[...] = h0_ref[0]
    hb_s[...] = h0_ref[1]

    ri = lax.broadcasted_iota(jnp.int32, (q, q), 0)
    ci = lax.broadcasted_iota(jnp.int32, (q, q), 1)
    lower = ri >= ci
    upper = ci >= ri
    lane = lax.broadcasted_iota(jnp.int32, (q, LANES), 1)
    lo_half = lane < P

    def cumsums(dt):
        dta = dt * a2_row
        cf = jnp.dot(tri3_ref[...], jnp.concatenate(_split3(dta), axis=0), preferred_element_type=f32)
        rb = cf[q - 1:q, :] - cf + dta
        return cf, rb

    def load_chunk(r0):
        dt = jax.nn.softplus(dt_ref[pl.ds(r0, q), :] + dtb_ref[...])
        xs = xbc_ref[pl.ds(r0, q), :SSD_INNER].astype(f32)
        bmat = [xbc_ref[pl.ds(r0, q), SSD_INNER + g * N:SSD_INNER + (g + 1) * N] for g in range(SSD_GROUPS)]
        cmat = [xbc_ref[pl.ds(r0, q), SSD_INNER + SSD_BC + g * N:SSD_INNER + SSD_BC + (g + 1) * N]
                for g in range(SSD_GROUPS)]
        return dt, xs, bmat, cmat

    def inter_chunk(h_s, sel_ref, decay, weight, xs, bmat, cmat, dec_idx):
        ew = jnp.dot(jnp.concatenate([_pack3(decay), _pack3(weight)], axis=0), sel_ref[...],
                     preferred_element_type=f32)
        e_x, w_x = ew[:q], ew[q:]
        hb_ = h_s[...].astype(bf16)
        yoff = jnp.concatenate(
            [jnp.dot(cmat[g], hb_[:, g * gw:(g + 1) * gw], preferred_element_type=f32)
             for g in range(SSD_GROUPS)], axis=1)
        xw = (xs * w_x).astype(bf16)
        dec_row = e_x[dec_idx:dec_idx + 1, :]
        for g in range(SSD_GROUPS):
            gs = slice(g * gw, (g + 1) * gw)
            bt = jnp.transpose(bmat[g].astype(f32)).astype(bf16)
            upd = jnp.dot(bt, xw[:, gs], preferred_element_type=f32)
            h_s[:, gs] = h_s[:, gs] * dec_row[:, gs] + upd
        return yoff * e_x

    def finish(r0, y, xs):
        yy = y + xs * dskip_ref[...]
        zz = z_ref[pl.ds(r0, q), :].astype(f32)
        gated = yy * _silu(zz)
        ms = jnp.mean(gated * gated, axis=-1, keepdims=True)
        out_ref[pl.ds(r0, q), :] = (gated * lax.rsqrt(ms + EPS) * nw_ref[...]).astype(out_ref.dtype)

    def fwd_chunk(c, second_half):
        r0 = pl.multiple_of(c * q, q)
        dt, xs, bmat, cmat = load_chunk(r0)
        cf, rb = cumsums(dt)
        pcol = jnp.where(lane < H, cf, rb)
        bcast = jnp.dot(_pack3(pcol), selbc_ref[...], preferred_element_type=f32)
        prow = jnp.transpose(pcol - jnp.log2(dt))
        cbm = [lax.dot_general(cmat[g], bmat[g], (((1,), (1,)), ((), ())), preferred_element_type=f32)
               for g in range(SSD_GROUPS)]
        xsb = xs.astype(bf16)
        zero_b = jnp.zeros((), bf16)
        ypairs = []
        for k in range(H // 2):
            mats = []
            for h in (2 * k, 2 * k + 1):
                g = h // (H // SSD_GROUPS)
                hb_ = H + h
                segf = bcast[:, h * q:(h + 1) * q] - prow[h:h + 1, :]
                segb = bcast[:, hb_ * q:(hb_ + 1) * q] - prow[hb_:hb_ + 1, :]
                df = jnp.exp2(jnp.where(lower, segf, NEG_INF))
                db = jnp.exp2(jnp.where(upper, segb, NEG_INF))
                mats.append((cbm[g] * (df + db)).astype(bf16))
            xp = xsb[:, k * LANES:(k + 1) * LANES]
            xbd = jnp.concatenate([jnp.where(lo_half, xp, zero_b), jnp.where(lo_half, zero_b, xp)], axis=0)
            ypairs.append(jnp.dot(jnp.concatenate(mats, axis=1), xbd, preferred_element_type=f32))
        y = jnp.concatenate(ypairs, axis=1)
        wfa = jnp.exp2(cf[q - 1:q, :] - cf) * dt
        y = y + inter_chunk(hf_s, self_ref, jnp.exp2(cf), wfa, xs, bmat, cmat, q - 1)
        if second_half:
            finish(r0, y_s[pl.ds(r0, q), :] + y, xs)
        else:
            y_s[pl.ds(r0, q), :] = y

    def bwd_chunk(c, second_half):
        r0 = pl.multiple_of(c * q, q)
        dt, xs, bmat, cmat = load_chunk(r0)
        _, rb = cumsums(dt)
        wba = jnp.exp2(rb[0:1, :] - rb) * dt
        y = inter_chunk(hb_s, selb_ref, jnp.exp2(rb), wba, xs, bmat, cmat, 0)
        if second_half:
            finish(r0, y_s[pl.ds(r0, q), :] + y, xs)
        else:
            y_s[pl.ds(r0, q), :] = y

    half = nc // 2

    def first_half(i, carry):
        fwd_chunk(i, False)
        bwd_chunk(nc - 1 - i, False)
        return carry

    def second_half(i, carry):
        fwd_chunk(i, True)
        bwd_chunk(nc - 1 - i, True)
        return carry

    lax.fori_loop(0, half, first_half, 0)
    lax.fori_loop(half, nc, second_half, 0)
    hfin_ref[0] = hf_s[...]
    hfin_ref[1] = hb_s[...]


def _ssd(xbc, dt, z, dtb, alog, dskip, nw, h0, bsz, seq_len):
    assert seq_len % (2 * SSD_CHUNK) == 0, "the two recurrences meet in the middle: even chunk count"
    one = pl.Buffered(1)
    seq = lambda w: pl.BlockSpec((seq_len, w), lambda b: (b, 0), pipeline_mode=one)
    const = lambda r, w: pl.BlockSpec((r, w), lambda b: (0, 0))
    st = pl.BlockSpec((None, 2, SSD_STATE, SSD_INNER), lambda b: (b, 0, 0, 0))
    sels = _ssd_selectors()
    return pl.pallas_call(
        functools.partial(_ssd_kernel, seq_len),
        out_shape=[jax.ShapeDtypeStruct((bsz * seq_len, SSD_INNER), bf16),
                   jax.ShapeDtypeStruct((bsz, 2, SSD_STATE, SSD_INNER), f32)],
        grid=(bsz,),
        in_specs=[seq(SSD_XBC), seq(LANES), seq(SSD_INNER),
                  const(1, LANES), const(1, LANES), const(1, SSD_INNER), const(1, SSD_INNER)]
                 + [const(*s.shape) for s in sels] + [st],
        out_specs=[seq(SSD_INNER), st],
        scratch_shapes=[pltpu.VMEM((seq_len, SSD_INNER), f32),
                        pltpu.VMEM((SSD_STATE, SSD_INNER), f32),
                        pltpu.VMEM((SSD_STATE, SSD_INNER), f32)],
        compiler_params=_cparams(("parallel",)),
        name="ssd",
    )(xbc, dt, z, dtb, alog, dskip, nw, *sels, h0)


def _attn_kernel(n_blocks, nsub, local, q_ref, g_ref, k_ref, kc_ref, *rest):
    nv = nsub + 2 if local else 0
    v_refs = rest[:nv]
    vc_ref, sink_ref, o_ref, s_s = rest[nv:]
    t = ATT_BLOCK
    i0 = pl.program_id(1) * nsub
    rpk = ATT_HEADS // ATT_KV_HEADS
    lane = lax.broadcasted_iota(jnp.int32, (t, LANES), 1)
    lo_half = lane < ATT_HEAD_DIM
    zero_b = jnp.zeros((), bf16)
    kl = lax.broadcasted_iota(jnp.int32, (t, t), 0)
    ql = lax.broadcasted_iota(jnp.int32, (t, t), 1)
    for sub in range(nsub):
        i = i0 + sub
        qv = q_ref[sub * t:(sub + 1) * t, :]
        if local:
            p0 = pl.multiple_of(jnp.maximum(i - 1, 0) * t, t)
            c0 = pl.multiple_of(i * t, t)
            n0 = pl.multiple_of(jnp.minimum(i + 1, n_blocks - 1) * t, t)
            bias_prev = jnp.where((kl >= ql) & (i > 0), 0.0, NEG_INF)
            bias_next = jnp.where((kl <= ql) & (i < n_blocks - 1), 0.0, NEG_INF)
            bias_prev = jnp.concatenate([bias_prev] * rpk, axis=1)
            bias_next = jnp.concatenate([bias_next] * rpk, axis=1)
        for j in range(ATT_KV_HEADS):
            ls = slice(j * LANES, (j + 1) * LANES)
            if local:
                kk = jnp.concatenate([k_ref[pl.ds(p0, t), ls], k_ref[pl.ds(c0, t), ls],
                                      k_ref[pl.ds(n0, t), ls], kc_ref[:, ls]], axis=0)
            else:
                kk = kc_ref[:, ls]
            pieces = []
            for r in range(rpk):
                hq = j * rpk + r
                qp = qv[:, (hq // 2) * LANES:(hq // 2 + 1) * LANES]
                keep = lo_half if hq % 2 == 0 else jnp.logical_not(lo_half)
                pieces.append(jnp.where(keep, qp, zero_b))
            q4 = jnp.concatenate(pieces, axis=0)
            s = lax.dot_general(kk, q4, (((1,), (1,)), ((), ())), preferred_element_type=f32)
            if local:
                s = jnp.concatenate([s[:t] + bias_prev, s[t:2 * t], s[2 * t:3 * t] + bias_next, s[3 * t:]],
                                    axis=0)
            s_s[sub, j] = s
    for sub in range(nsub):
        outs = []
        for j in range(ATT_KV_HEADS):
            vs = slice(j * ATT_HEAD_DIM, (j + 1) * ATT_HEAD_DIM)
            if local:
                vvt = jnp.concatenate([v_refs[sub + k][vs, :] for k in range(3)] + [vc_ref[vs, :]], axis=1)
            else:
                vvt = vc_ref[vs, :]
            sk = jnp.concatenate([jnp.full((1, t), sink_ref[j * rpk + r] * LOG2E, f32) for r in range(rpk)],
                                 axis=1)
            s = s_s[sub, j]
            m = jnp.maximum(jnp.max(s, axis=0, keepdims=True), sk)
            p = jnp.exp2(s - m)
            den = jnp.sum(p, axis=0, keepdims=True) + jnp.exp2(sk - m)
            vvt = jnp.concatenate([vvt, vvt], axis=0)
            ot = jnp.dot(vvt, p.astype(bf16), preferred_element_type=f32) / den
            o4 = [jnp.transpose(ot[:, r * t:(r + 1) * t]) for r in range(rpk)]
            outs.append(jnp.where(lo_half, o4[0], o4[1]))
            outs.append(jnp.where(lo_half, o4[2], o4[3]))
        o = jnp.concatenate(outs, axis=1)
        gv = g_ref[sub * t:(sub + 1) * t, :].astype(f32)
        o_ref[sub * t:(sub + 1) * t, :] = (o * _silu(gv)).astype(o_ref.dtype)


def _attention(q, g, k, vt, kc, vct, sink, bsz, seq_len, local):
    t = ATT_BLOCK
    nb = seq_len // t
    nsub = max(s for s in range(1, ATT_SUB + 1) if nb % s == 0)
    n_ctx = kc.shape[0] // bsz
    kw = ATT_KV_HEADS * LANES
    vw = ATT_KVW
    blk = pl.BlockSpec((nsub * t, ATT_Q), lambda b, i: (b * (nb // nsub) + i, 0))
    full = lambda n: pl.BlockSpec((n, kw), lambda b, i: (b, 0))
    vblk = lambda off: pl.BlockSpec((vw, t), lambda b, i: (0, b * nb + jnp.clip(i * nsub + off, 0, nb - 1)))
    vspecs = [vblk(off) for off in range(-1, nsub + 1)] if local else []
    return pl.pallas_call(
        functools.partial(_attn_kernel, nb, nsub, local),
        out_shape=jax.ShapeDtypeStruct((bsz * seq_len, ATT_Q), bf16),
        grid=(bsz, nb // nsub),
        in_specs=[blk, blk, full(k.shape[0] // bsz), full(n_ctx)] + vspecs
                 + [pl.BlockSpec((vw, n_ctx), lambda b, i: (0, b)), pl.BlockSpec(memory_space=pltpu.SMEM)],
        out_specs=blk,
        scratch_shapes=[pltpu.VMEM((nsub, ATT_KV_HEADS, (3 * t if local else 0) + n_ctx,
                                    (ATT_HEADS // ATT_KV_HEADS) * t), f32)],
        compiler_params=_cparams(("parallel", "arbitrary")),
        name="attention",
    )(q, g, k, kc, *([vt] * len(vspecs)), vct, sink)


def _s5_disc_kernel(lre_ref, lim_ref, ls_ref, bre_ref, bim_ref, abre_ref, abim_ref, bbre_ref, bbim_ref):
    lam_re = lre_ref[...]
    lam_im = lim_ref[...]
    dt = jnp.exp(ls_ref[...])
    mag = jnp.exp(lam_re * dt)
    ab_re = mag * jnp.cos(lam_im * dt)
    ab_im = mag * jnp.sin(lam_im * dt)
    num_re, num_im = ab_re - 1.0, ab_im
    den = lam_re * lam_re + lam_im * lam_im
    coef_re = (num_re * lam_re + num_im * lam_im) / den
    coef_im = (num_im * lam_re - num_re * lam_im) / den
    b_re, b_im = bre_ref[...], bim_ref[...]
    abre_ref[...] = ab_re
    abim_ref[...] = ab_im
    bbre_ref[...] = coef_re * b_re - coef_im * b_im
    bbim_ref[...] = coef_re * b_im + coef_im * b_re


def _s5_discretise(lam_re, lam_im, log_step, b_re, b_im):
    g, n, cg = b_re.shape
    exp = lambda t: jnp.repeat(t.reshape(2 * g, n), cg, axis=1)
    ls = jnp.broadcast_to(log_step.reshape(2 * g, 1), (2 * g, n * cg))
    bb = lambda t: jnp.tile(t.reshape(g, n * cg), (2, 1))
    shp = jax.ShapeDtypeStruct((2 * g, n * cg), f32)
    ab_re, ab_im, bb_re, bb_im = pl.pallas_call(
        _s5_disc_kernel, out_shape=[shp] * 4, name="s5_disc",
    )(exp(lam_re), exp(lam_im), ls, bb(b_re), bb(b_im))
    first = lambda t: t.reshape(2, g, n, cg)[..., 0]
    full = lambda t: t.reshape(2, g, n, cg)
    return first(ab_re), first(ab_im), full(bb_re), full(bb_im)


def _cmul(ar, ai, br, bi):
    return ar * br - ai * bi, ar * bi + ai * br


def _s5_kernel(n_lat, n_ctx, *refs):
    tc = S5_TC
    xl, xc, arow_ref, acol_ref, bbd_ref, cbd_ref, y_ref, wyz_s, ws_s, sl_s, sc_s = refs
    hw = S5_HGRP * S5_STATE
    hl = LANES // 2
    xw = tc * hl
    nsl = hw // LANES
    nb = 8
    lo = lax.broadcasted_iota(jnp.int32, (1, LANES), 1) < hl

    def powers(re, im, n):
        out = [(jnp.ones_like(re), jnp.zeros_like(im))]
        for _ in range(n):
            out.append(_cmul(out[-1][0], out[-1][1], re, im))
        return out

    steps = lambda h, p: (2 * p, 2 * p + 1) if h == 0 else (2 * p + 1, 2 * p)

    prow = []
    for h in range(2):
        pr = [powers(arow_ref[h, 2 * d:2 * d + 1, :], arow_ref[h, 2 * d + 1:2 * d + 2, :], tc) for d in range(2)]
        prow.append(pr)
        for s in range(tc):
            for d, k in ((0, tc - 1 - s), (1, s)):
                wr, wi = _cmul(bbd_ref[2 * d, h], bbd_ref[2 * d + 1, h], *pr[d][k])
                ws_s[h, s * hl:(s + 1) * hl, d * 2 * hw:d * 2 * hw + hw] = wr.astype(bf16)
                ws_s[h, s * hl:(s + 1) * hl, d * 2 * hw + hw:(d + 1) * 2 * hw] = wi.astype(bf16)
        crhs = jnp.concatenate([cbd_ref[0, h], -cbd_ref[1, h]], axis=0).astype(bf16)
        kall = [jnp.dot(ws_s[h, :, d * 2 * hw:(d + 1) * 2 * hw], crhs, preferred_element_type=f32)
                for d in range(2)]
        kf = [kall[0][(tc - 1 - k) * hl:(tc - k) * hl] for k in range(tc)]
        kb = [kall[1][k * hl:(k + 1) * hl] for k in range(tc)]
        lag = lambda s, t: kf[t - s] if t > s else (kb[s - t] if t < s else kf[0] + kb[0])
        for s in range(tc):
            for p in range(tc // 2):
                t0, t1 = steps(h, p)
                wyz_s[h, s * hl:(s + 1) * hl, p * LANES:(p + 1) * LANES] = (
                    jnp.where(lo, lag(s, t0), lag(s, t1)).astype(bf16))
        for d in range(2):
            pc = powers(acol_ref[h, 2 * d], acol_ref[h, 2 * d + 1], tc)
            kk = (lambda t: t + 1) if d == 0 else (lambda t: tc - t)
            r0 = xw + d * 2 * hw
            for p in range(tc // 2):
                t0, t1 = steps(h, p)
                ar = jnp.where(lo, pc[kk(t0)][0], pc[kk(t1)][0])
                ai = jnp.where(lo, pc[kk(t0)][1], pc[kk(t1)][1])
                dre, dim_ = _cmul(cbd_ref[0, h], cbd_ref[1, h], ar, ai)
                wyz_s[h, r0:r0 + hw, p * LANES:(p + 1) * LANES] = dre.astype(bf16)
                wyz_s[h, r0 + hw:r0 + 2 * hw, p * LANES:(p + 1) * LANES] = (-dim_).astype(bf16)

    def rows_of(x_ref, b, n, h):
        return x_ref[b * n:(b + 1) * n, h * xw:(h + 1) * xw]

    def inject(x_ref, s_ref, n):
        for b in range(nb):
            for h in range(2):
                sb = jnp.dot(rows_of(x_ref, b, n, h), ws_s[h], preferred_element_type=f32)
                for k in range(4 * nsl):
                    s_ref[h * 4 * nsl + k, pl.ds(b, n, stride=nb), :] = sb[:, k * LANES:(k + 1) * LANES]

    at = [[[tuple(jnp.broadcast_to(p[:, k * LANES:(k + 1) * LANES], (nb, LANES)) for p in prow[h][d][tc])
            for k in range(nsl)] for d in range(2)] for h in range(2)]

    def scan(s_ref, n, init):
        def step(i, carry):
            idxs = [pl.ds(pl.multiple_of(i * nb, nb), nb), pl.ds(pl.multiple_of((n - 1 - i) * nb, nb), nb)]
            chains = [(h, d, k) for h in range(2) for d in range(2) for k in range(nsl)]
            slab = lambda h, d, k: h * 4 * nsl + d * 2 * nsl + k
            inj = [(s_ref[slab(h, d, k), idxs[d], :], s_ref[slab(h, d, k) + nsl, idxs[d], :])
                   for h, d, k in chains]
            new = []
            for c, (h, d, k) in enumerate(chains):
                hr, hi = carry[2 * c], carry[2 * c + 1]
                s_ref[slab(h, d, k), idxs[d], :] = hr
                s_ref[slab(h, d, k) + nsl, idxs[d], :] = hi
                ar, ai = at[h][d][k]
                new += [ar * hr - ai * hi + inj[c][0], ar * hi + ai * hr + inj[c][1]]
            return tuple(new)
        return lax.fori_loop(0, n, step, init, unroll=4)

    inject(xc, sc_s, n_ctx)
    h_ctx = scan(sc_s, n_ctx, tuple(jnp.zeros((nb, LANES), f32) for _ in range(8 * nsl)))
    inject(xl, sl_s, n_lat)
    scan(sl_s, n_lat, h_ctx)

    for b in range(nb):
        yh = []
        for h in range(2):
            hin = jnp.concatenate([sl_s[h * 4 * nsl + k, pl.ds(b, n_lat, stride=nb), :]
                                   for k in range(4 * nsl)], axis=1).astype(bf16)
            yh.append(jnp.dot(jnp.concatenate([rows_of(xl, b, n_lat, h), hin], axis=1), wyz_s[h],
                              preferred_element_type=f32))
        for p in range(tc // 2):
            y0, y1 = yh[0][:, p * LANES:(p + 1) * LANES], yh[1][:, p * LANES:(p + 1) * LANES]
            even = jnp.where(lo, y0, y1)
            odd = pltpu.roll(jnp.where(lo, y1, y0), hl, axis=1)
            y_ref[pl.ds(b * n_lat * tc + 2 * p, n_lat, stride=tc), :] = even
            y_ref[pl.ds(b * n_lat * tc + 2 * p + 1, n_lat, stride=tc), :] = odd


def _s5_mix(u, u_c, arow, acol, bbd, cbd, bsz, seq_len, n_ctx_tok):
    assert bsz == 8, "the chunk recurrence puts the batch on the 8 sublanes"
    tc = S5_TC
    n_lat, n_ctx = seq_len // tc, n_ctx_tok // tc
    nblk = S5_GROUPS // S5_GBLK
    hw = S5_HGRP * S5_STATE
    hl = LANES // 2
    nsl = hw // LANES
    one = pl.Buffered(1)
    xspec = lambda rows: pl.BlockSpec((None, rows, tc * LANES), lambda g: (g, 0, 0))
    return pl.pallas_call(
        functools.partial(_s5_kernel, n_lat, n_ctx),
        out_shape=jax.ShapeDtypeStruct((nblk, bsz * seq_len, LANES), f32),
        grid=(nblk,),
        in_specs=[xspec(bsz * n_lat), xspec(bsz * n_ctx),
                  pl.BlockSpec((2, 4, hw), lambda g: (g, 0, 0), pipeline_mode=one),
                  pl.BlockSpec((2, 4, hw, LANES), lambda g: (g, 0, 0, 0), pipeline_mode=one),
                  pl.BlockSpec((4, 2, hl, hw), lambda g: (0, g, 0, 0), pipeline_mode=one),
                  pl.BlockSpec((2, 2, hw, LANES), lambda g: (0, g, 0, 0), pipeline_mode=one)],
        out_specs=pl.BlockSpec((None, bsz * seq_len, LANES), lambda g: (g, 0, 0), pipeline_mode=one),
        scratch_shapes=[pltpu.VMEM((2, tc * hl + 4 * hw, tc * hl), bf16),
                        pltpu.VMEM((2, tc * hl, 4 * hw), bf16),
                        pltpu.VMEM((8 * nsl, bsz * n_lat, LANES), f32),
                        pltpu.VMEM((8 * nsl, bsz * n_ctx, LANES), f32)],
        compiler_params=_cparams(("arbitrary",)),
        name="s5_mix",
    )(u, u_c, arow, acol, bbd, cbd)


def _s5_block_params(ab_re, ab_im, bb_re, bb_im, c_re, c_im):
    g, n, cg = S5_GROUPS, S5_STATE, S5_GROUP_CH
    nh = g // S5_HGRP
    hw = S5_HGRP * n
    arow = jnp.stack([t[d].reshape(nh, hw) for d in range(2) for t in (ab_re, ab_im)], axis=1)
    acol = jnp.broadcast_to(arow[..., None], (nh, 4, hw, LANES))

    def blockdiag(t, rows_per, cols_per):
        tiled = jnp.concatenate([t] * S5_HGRP, axis=-1)
        r = np.arange(t.shape[-2])[:, None] // rows_per
        c = np.arange(S5_HGRP * cols_per)[None, :] // cols_per
        return jnp.where(jnp.asarray(r == c), tiled, 0.0)

    bb = jnp.stack([t[d] for d in range(2) for t in (bb_re, bb_im)], axis=0)
    bb = jnp.swapaxes(bb.reshape(4, nh, S5_HGRP, n, cg), -1, -2).reshape(4, nh, S5_HGRP * cg, n)
    bbd = blockdiag(bb, cg, n)
    cc = jnp.swapaxes(jnp.stack([c_re, c_im], axis=0).reshape(2, nh, S5_HGRP, cg, n), -1, -2)
    cbd = blockdiag(cc.reshape(2, nh, hw, cg), n, cg)
    return arow, acol, bbd, jnp.concatenate([cbd, cbd], axis=-1)


def _s5_out_kernel(y_ref, u_ref, g_ref, x_ref, dskip_ref, gw_ref, gb_ref, w_ref, gate_ref,
                   fw_ref, o_ref):
    y = jnp.concatenate([y_ref[j] for j in range(y_ref.shape[0])], axis=1)
    y = y + dskip_ref[...] * u_ref[...].astype(f32)
    y = jax.nn.gelu(y)
    glu = jnp.dot(y.astype(bf16), gw_ref[...], preferred_element_type=f32) + gb_ref[...]
    y = y * jax.nn.sigmoid(glu)
    y = y * _silu(g_ref[...].astype(f32))
    x = x_ref[...] + gate_ref[...] * jnp.dot(y.astype(bf16), w_ref[...], preferred_element_type=f32)
    ms = jnp.mean(x * x, axis=-1, keepdims=True)
    o_ref[...] = x * lax.rsqrt(ms + EPS) * fw_ref[...]


def _s5_out(y, u, g, x, dskip, gw, gb, w, gate, fw, rows_per_mod):
    m, d = x.shape
    tm = min(ROW_TILE, rows_per_mod)
    per = rows_per_mod // tm
    row = lambda: pl.BlockSpec((tm, d), lambda i: (i, 0))
    vec = lambda: pl.BlockSpec((1, d), lambda i: (0, 0))
    mat = lambda: pl.BlockSpec((d, d), lambda i: (0, 0))
    return pl.pallas_call(
        _s5_out_kernel,
        out_shape=jax.ShapeDtypeStruct((m, d), f32),
        grid=(m // tm,),
        in_specs=[pl.BlockSpec((y.shape[0], tm, LANES), lambda i: (0, i, 0)),
                  row(), row(), row(), vec(), mat(), vec(), mat(),
                  pl.BlockSpec((None, 1, d), lambda i: (i // per, 0, 0)), vec()],
        out_specs=row(),
        compiler_params=_cparams(("parallel",)),
        name="s5_out",
    )(y, u, g, x, dskip, gw, gb, w, gate, fw)


def _even_weights(w_in):
    o = 0
    z = w_in[:, o:o + SSD_INNER]; o += SSD_INNER
    xbc = w_in[:, o:o + SSD_XBC]; o += SSD_XBC
    dt = w_in[:, o:o + 2 * SSD_HEADS]; o += 2 * SSD_HEADS
    q = w_in[:, o:o + ATT_Q]; o += ATT_Q
    k = w_in[:, o:o + ATT_KVW]; o += ATT_KVW
    v = w_in[:, o:o + ATT_KVW]; o += ATT_KVW
    g = w_in[:, o:o + ATT_Q]
    d = w_in.shape[0]
    dup = lambda t: jnp.concatenate([t.reshape(d, ATT_KV_HEADS, 1, ATT_HEAD_DIM)] * 2, axis=2).reshape(d, -1)
    dtp = jnp.pad(dt, ((0, 0), (0, LANES - 2 * SSD_HEADS)))
    cast = lambda t: t.astype(bf16)
    return [cast(t) for t in (z, xbc, q, dup(k), g, dtp)], cast(v.T)


def _even_segs(rope):
    scale = ATT_HEAD_DIM ** -0.5 * LOG2E
    widths = [(SSD_INNER, None, bf16), (SSD_XBC, None, bf16),
              (ATT_Q, scale, bf16),
              (ATT_KV_HEADS * LANES, 1.0 if rope else None, bf16),
              (ATT_Q, None, bf16), (LANES, None, f32)]
    return [(i, w, r, dtp, False) for i, (w, r, dtp) in enumerate(widths)]


def _pad_lanes(v, n=LANES):
    v = v.reshape(1, -1)
    return jnp.pad(v, ((0, 0), (0, n - v.shape[1])))


def kernel(x, c, ctx, c_ctx, e_norm_w, e_ada_w, e_ada_b, e_w_in, e_conv_w, e_conv_b, e_dt_bias,
           e_a_log, e_d_skip, e_ssd_norm_w, e_sink, e_w_out, o_norm_w, o_ada_w, o_ada_b, o_w_in,
           o_lam_re, o_lam_im, o_log_step, o_b_re, o_b_im, o_c_re, o_c_im, o_d_skip, o_glu_w,
           o_glu_b, o_w_out, final_norm_w):
    bsz, seq_len, d = x.shape
    n_ctx = ctx.shape[1]
    xf = x.reshape(bsz * seq_len, d)
    xcf = ctx.reshape(bsz * n_ctx, d)

    cvecs = jnp.concatenate([c, c_ctx[None, :], jnp.zeros((16 - bsz - 1, d), f32)], axis=0)

    def modulation(ada_w, ada_b):
        mod = _adaln(cvecs, ada_w.astype(bf16), ada_b.reshape(1, -1))
        parts = [mod[:, k * d:(k + 1) * d] for k in range(3)]
        lat = [p[:bsz].reshape(bsz, 1, d) for p in parts]
        cx = [p[bsz:bsz + 1].reshape(1, 1, d) for p in parts]
        return lat, cx

    (shift, scale, gate), (shift_c, scale_c, gate_c) = modulation(e_ada_w[0], e_ada_b[0])
    w_in, w_vt = _even_weights(e_w_in[0])
    nw = e_norm_w[0].reshape(1, d)
    tabs = _rope_tables(seq_len)
    vseg = [(0, ATT_KVW, bf16)]
    conv_w = jnp.pad(e_conv_w[0], ((0, 8 - SSD_CONV), (0, 0)))
    conv_b = e_conv_b[0].reshape(1, -1)
    conv = (1, conv_w, conv_b)
    z, xbc, q, k, g, dt, vt = _inproj(xf, shift, scale, nw, w_in, _even_segs(True), seq_len, tabs, w_vt, vseg,
                                      conv=conv)
    z_c, xbc_c, q_c, k_c, g_c, dt_c, vt_c = _inproj(xcf, shift_c, scale_c, nw, w_in, _even_segs(False), n_ctx,
                                                    None, w_vt, vseg, conv=conv)

    dtb = _pad_lanes(e_dt_bias[0])
    alog = _pad_lanes(e_a_log[0])
    dskip = jnp.repeat(e_d_skip[0], SSD_HEAD_DIM).reshape(1, -1)
    snw = e_ssd_norm_w[0].reshape(1, -1)
    h0 = jnp.zeros((bsz, 2, SSD_STATE, SSD_INNER), f32)
    ssd_c, hfin = _ssd(xbc_c, dt_c, z_c, dtb, alog, dskip, snw, h0, bsz, n_ctx)
    ssd_o, _ = _ssd(xbc, dt, z, dtb, alog, dskip, snw, hfin, bsz, seq_len)

    sink = e_sink[0]
    att = _attention(q, g, k, vt, k_c, vt_c, sink, bsz, seq_len, True)
    att_c = _attention(q_c, g_c, k_c, vt_c, k_c, vt_c, sink, bsz, n_ctx, False)
    w_out = e_w_out[0].astype(bf16)
    even_gate, even_gate_c = gate, gate_c

    (shift, scale, gate), (shift_c, scale_c, _) = modulation(o_ada_w[0], o_ada_b[0])
    w_u = o_w_in[0][:, :S5_WIDTH].astype(bf16)
    w_g = o_w_in[0][:, S5_WIDTH:].astype(bf16)
    nw = o_norm_w[0].reshape(1, d)
    x1, u, u_ch, g2 = _inproj(xf, shift, scale, nw, [w_u, w_g],
                              [(0, S5_WIDTH, None, bf16, False), (0, S5_WIDTH, None, bf16, True),
                               (1, S5_WIDTH, None, bf16, False)], seq_len,
                              pre=(ssd_o, att, w_out, even_gate))
    _, uc_ch = _inproj(xcf, shift_c, scale_c, nw, [w_u], [(0, S5_WIDTH, None, bf16, True)], n_ctx,
                       pre=(ssd_c, att_c, w_out, even_gate_c))

    ab_re, ab_im, bb_re, bb_im = _s5_discretise(o_lam_re[0], o_lam_im[0], o_log_step[0], o_b_re[0], o_b_im[0])
    arow, acol, bbd, cbd = _s5_block_params(ab_re, ab_im, bb_re, bb_im, o_c_re[0], o_c_im[0])
    y = _s5_mix(u_ch, uc_ch, arow, acol, bbd, cbd, bsz, seq_len, n_ctx)
    out = _s5_out(y, u, g2, x1, o_d_skip[0].reshape(1, -1),
                  o_glu_w[0].astype(bf16), o_glu_b[0].reshape(1, -1), o_w_out[0].astype(bf16), gate,
                  final_norm_w.reshape(1, -1), seq_len)
    return out.reshape(bsz, seq_len, d)
```

```python
import functools
import math

import jax
import jax.numpy as jnp
import numpy as np
from jax import lax
from jax.experimental import pallas as pl
from jax.experimental.pallas import tpu as pltpu

f32 = jnp.float32
bf16 = jnp.bfloat16

D_MODEL = 1024
GRID_W = 64
EPS = 1e-6
NEG_INF = -1e30

SSD_HEADS = 16
SSD_HEAD_DIM = 64
SSD_GROUPS = 2
SSD_STATE = 128
SSD_CONV = 5
SSD_CHUNK = 128
SSD_INNER = SSD_HEADS * SSD_HEAD_DIM
SSD_BC = SSD_GROUPS * SSD_STATE
SSD_XBC = SSD_INNER + 2 * SSD_BC
ATT_HEADS = 16
ATT_KV_HEADS = 4
ATT_HEAD_DIM = 64
ATT_BLOCK = 128
ROPE_THETA = 10000.0
ATT_Q = ATT_HEADS * ATT_HEAD_DIM
ATT_KVW = ATT_KV_HEADS * ATT_HEAD_DIM
S5_WIDTH = 1024
S5_GROUP_CH = 16
S5_GROUPS = S5_WIDTH // S5_GROUP_CH
S5_STATE = 64

LOG2E = math.log2(math.e)
LANES = 128
ROW_TILE = 512
ATT_SUB = 4
S5_GBLK = 8
S5_TC = 8
S5_HGRP = 4
VMEM_LIMIT = 56 * 1024 * 1024


def _cparams(sem, flags=None):
    return pltpu.CompilerParams(dimension_semantics=sem, vmem_limit_bytes=VMEM_LIMIT, flags=flags)


def _silu(x):
    h = 0.5 * x
    return h + h * jnp.tanh(h)


def _adaln_kernel(c_ref, w_ref, b_ref, o_ref):
    c = c_ref[...]
    s = _silu(c).astype(bf16)
    o_ref[...] = jnp.dot(s, w_ref[...], preferred_element_type=f32) + b_ref[...]


def _adaln(cvecs, w, b):
    r, d = cvecs.shape
    n = w.shape[1]
    tn = 1024
    return pl.pallas_call(
        _adaln_kernel,
        out_shape=jax.ShapeDtypeStruct((r, n), f32),
        grid=(n // tn,),
        in_specs=[pl.BlockSpec((r, d), lambda j: (0, 0)),
                  pl.BlockSpec((d, tn), lambda j: (0, j)),
                  pl.BlockSpec((1, tn), lambda j: (0, j))],
        out_specs=pl.BlockSpec((r, tn), lambda j: (0, j)),
        compiler_params=_cparams(("arbitrary",)),
        name="adaln",
    )(cvecs, w, b)


def _inproj_kernel(segs, tsegs, n_w, has_rope, has_pre, x_ref, shift_ref, scale_ref, nw_ref, *rest):
    if has_pre:
        a_ref, b_ref, wo_ref, gate_ref = rest[:4]
        rest = rest[4:]
    w_refs, rest = rest[:n_w], rest[n_w:]
    if tsegs:
        wt_ref, rest = rest[0], rest[1:]
    if has_rope:
        cos_ref, sina_ref, sinb_ref = rest[:3]
        rest = rest[3:]
    if has_pre:
        x1_ref, rest = rest[0], rest[1:]
    outs = rest[:len(segs)]
    touts = rest[len(segs):len(segs) + len(tsegs)]
    rest = rest[len(segs) + len(tsegs):]
    slab_s = rest[0] if rest else None
    x = x_ref[...]
    if has_pre:
        ka = a_ref.shape[1]
        acc = jnp.dot(a_ref[...], wo_ref[:ka, :], preferred_element_type=f32)
        acc = acc + jnp.dot(b_ref[...], wo_ref[ka:, :], preferred_element_type=f32)
        x = x + gate_ref[...] * acc
        x1_ref[...] = x
    ms = jnp.mean(x * x, axis=-1, keepdims=True)
    h = (x * lax.rsqrt(ms + EPS)) * nw_ref[...]
    h = h * (1.0 + scale_ref[...]) + shift_ref[...]
    hb = h.astype(bf16)
    for (start, width, _), o_ref in zip(tsegs, touts):
        acc_t = lax.dot_general(wt_ref[start:start + width, :], hb, (((1,), (1,)), ((), ())),
                                preferred_element_type=f32)
        o_ref[...] = acc_t.astype(o_ref.dtype)
    products = {}
    for (widx, width, rope, _, chunked), o_ref in zip(segs, outs):
        if widx not in products:
            products[widx] = jnp.dot(hb, w_refs[widx][...], preferred_element_type=f32)
        acc = products[widx]
        if chunked:
            tm = acc.shape[0]
            hl = LANES // 2
            lo = lax.broadcasted_iota(jnp.int32, (1, LANES), 1) < hl
            for j in range(width // LANES):
                slab_s[j] = acc[:, j * LANES:(j + 1) * LANES]
            for j in range(width // LANES):
                rows = [slab_s[j, pl.ds(s, tm // S5_TC, stride=S5_TC), :] for s in range(S5_TC)]
                for p in range(S5_TC // 2):
                    a, b = rows[2 * p], rows[2 * p + 1]
                    h0 = jnp.where(lo, a, pltpu.roll(b, hl, axis=1))
                    h1 = jnp.where(lo, pltpu.roll(a, hl, axis=1), b)
                    o_ref[j, :, p * LANES:(p + 1) * LANES] = h0.astype(o_ref.dtype)
                    o_ref[j, :, S5_TC * hl + p * LANES:S5_TC * hl + (p + 1) * LANES] = h1.astype(o_ref.dtype)
            continue
        if rope is not None and not has_rope:
            acc = acc * rope
        elif rope is not None:
            rep = width // LANES
            cos = jnp.concatenate([cos_ref[...]] * rep, axis=1) * rope
            sina = jnp.concatenate([sina_ref[...]] * rep, axis=1) * rope
            sinb = jnp.concatenate([sinb_ref[...]] * rep, axis=1) * rope
            half = ATT_HEAD_DIM // 2
            up = pltpu.roll(acc, width - half, axis=1)
            dn = pltpu.roll(acc, half, axis=1)
            acc = acc * cos + up * sina + dn * sinb
        o_ref[...] = acc.astype(o_ref.dtype)


def _inproj(x, shift, scale, nw, w, segs, rows_per_mod, rope_tabs=None, wt=None, tsegs=(), pre=None):
    m, d = x.shape
    tm = min(ROW_TILE, rows_per_mod)
    per = rows_per_mod // tm
    nmod = shift.shape[0]
    mod_idx = (lambda i: (i // per, 0, 0)) if nmod > 1 else (lambda i: (0, 0, 0))
    in_specs = [pl.BlockSpec((tm, d), lambda i: (i, 0)),
                pl.BlockSpec((None, 1, d), mod_idx),
                pl.BlockSpec((None, 1, d), mod_idx),
                pl.BlockSpec((1, d), lambda i: (0, 0))]
    args = [x, shift, scale, nw]
    if pre is not None:
        a, b, w_out, gate = pre
        gate_idx = (lambda i: (i // per, 0, 0)) if gate.shape[0] > 1 else (lambda i: (0, 0, 0))
        in_specs += [pl.BlockSpec((tm, a.shape[1]), lambda i: (i, 0)),
                     pl.BlockSpec((tm, b.shape[1]), lambda i: (i, 0)),
                     pl.BlockSpec(w_out.shape, lambda i: (0, 0)),
                     pl.BlockSpec((None, 1, d), gate_idx)]
        args += [a, b, w_out, gate]
    in_specs += [pl.BlockSpec(wi.shape, lambda i: (0, 0)) for wi in w]
    args += list(w)
    if tsegs:
        in_specs.append(pl.BlockSpec(wt.shape, lambda i: (0, 0)))
        args.append(wt)
    if rope_tabs is not None:
        for t in rope_tabs:
            in_specs.append(pl.BlockSpec((tm, LANES), lambda i: (i % per, 0)))
            args.append(t)
    out_shape, out_specs = [], []
    if pre is not None:
        out_shape.append(jax.ShapeDtypeStruct((m, d), f32))
        out_specs.append(pl.BlockSpec((tm, d), lambda i: (i, 0)))
    for sg in segs:
        if sg[4]:
            out_shape.append(jax.ShapeDtypeStruct((sg[1] // LANES, m // S5_TC, S5_TC * LANES), sg[3]))
            out_specs.append(pl.BlockSpec((sg[1] // LANES, tm // S5_TC, S5_TC * LANES), lambda i: (0, i, 0)))
        else:
            out_shape.append(jax.ShapeDtypeStruct((m, sg[1]), sg[3]))
            out_specs.append(pl.BlockSpec((tm, sg[1]), lambda i: (i, 0)))
    for _, width, dtp in tsegs:
        out_shape.append(jax.ShapeDtypeStruct((width, m), dtp))
        out_specs.append(pl.BlockSpec((width, tm), lambda i: (0, i)))
    chunked_w = [sg[1] for sg in segs if sg[4]]
    scratch = [pltpu.VMEM((max(chunked_w) // LANES, tm, LANES), f32)] if chunked_w else []
    return pl.pallas_call(
        functools.partial(_inproj_kernel, tuple(segs), tuple(tsegs), len(w), rope_tabs is not None,
                          pre is not None),
        out_shape=out_shape,
        grid=(m // tm,),
        in_specs=in_specs,
        out_specs=out_specs,
        scratch_shapes=scratch,
        compiler_params=_cparams(("parallel",)),
        name="inproj",
    )(*args)


def _rope_tables(seq_len):
    rows = seq_len // GRID_W
    row = jnp.repeat(jnp.arange(rows, dtype=f32), GRID_W)
    col = jnp.tile(jnp.arange(GRID_W, dtype=f32), rows)
    n_freq = ATT_HEAD_DIM // 4
    inv = ROPE_THETA ** (-jnp.arange(n_freq, dtype=f32) / n_freq)
    ang = jnp.concatenate([row[:, None] * inv, col[:, None] * inv], axis=-1)
    cos, sin = jnp.cos(ang), jnp.sin(ang)
    zero = jnp.zeros_like(sin)
    cos_h = jnp.concatenate([cos, cos], axis=-1)
    sina_h = jnp.concatenate([-sin, zero], axis=-1)
    sinb_h = jnp.concatenate([zero, sin], axis=-1)
    two = lambda t: jnp.concatenate([t, t], axis=-1)
    return two(cos_h), two(sina_h), two(sinb_h)


SSD_PACK = 32


def _split3(x):
    hi = x.astype(bf16)
    r1 = x - hi.astype(f32)
    mid = r1.astype(bf16)
    lo = (r1 - mid.astype(f32)).astype(bf16)
    return hi, mid, lo


def _pack3(x):
    hi, mid, lo = _split3(x)
    lane = lax.broadcasted_iota(jnp.int32, x.shape, 1)
    mid_r = pltpu.roll(mid.astype(f32), SSD_PACK, axis=1)
    lo_r = pltpu.roll(lo.astype(f32), 2 * SSD_PACK, axis=1)
    packed = jnp.where(lane < SSD_PACK, hi.astype(f32),
                       jnp.where(lane < 2 * SSD_PACK, mid_r,
                                 jnp.where(lane < 3 * SSD_PACK, lo_r, 0.0)))
    return packed.astype(bf16)


def _ssd_selectors():
    k = np.arange(LANES)
    src = np.where(k < 3 * SSD_PACK, k % SSD_PACK, -1)
    col_blk = np.arange(SSD_PACK * SSD_CHUNK) // SSD_CHUNK
    sel_bc = (src[:, None] == col_blk[None, :])
    head = np.arange(SSD_INNER) // SSD_HEAD_DIM
    sel_f = (src[:, None] == head[None, :])
    sel_b = (src[:, None] == (head + SSD_HEADS)[None, :])
    tri3 = np.tile(np.tril(np.ones((SSD_CHUNK, SSD_CHUNK))), (1, 3))
    rows = np.arange(SSD_CHUNK)[:, None]
    cols = np.arange(SSD_CHUNK + 32)[None, :]
    half = SSD_CONV // 2
    shift = np.concatenate([cols == rows + 16 + d for d in range(-half, half + 1) if d != 0], axis=0)
    as_bf = lambda a: jnp.asarray(a.astype(np.float32), dtype=bf16)
    return as_bf(sel_bc), as_bf(sel_f), as_bf(sel_b), as_bf(tri3), as_bf(shift)


def _ssd_kernel(seq_len, xbc_ref, dt_ref, z_ref, cw_ref, cb_ref, dtb_ref, alog_ref, dskip_ref, nw_ref,
                selbc_ref, self_ref, selb_ref, tri3_ref, shift_ref, h0_ref, out_ref, hfin_ref,
                xs_s, bc_s, dt_s, y_s, hf_s, hb_s, win_s):
    q = SSD_CHUNK
    nc = seq_len // q
    halo = 16
    H, P, N = SSD_HEADS, SSD_HEAD_DIM, SSD_STATE
    gw = (H // SSD_GROUPS) * P
    a2_row = -jnp.exp(alog_ref[...]) * math.log2(math.e)

    def conv_chunk(c):
        r0 = pl.multiple_of(c * q, q)
        pstart = pl.multiple_of(jnp.maximum(r0 - halo, 0), halo)
        nstart = pl.multiple_of(jnp.minimum(r0 + q, seq_len - halo), halo)
        zero = jnp.zeros((), bf16)
        win_s[0:halo, :] = jnp.where(c > 0, xbc_ref[pl.ds(pstart, halo), :], zero)
        win_s[halo:halo + q, :] = xbc_ref[pl.ds(r0, q), :]
        win_s[halo + q:, :] = jnp.where(c < nc - 1, xbc_ref[pl.ds(nstart, halo), :], zero)
        taps = [k for k in range(SSD_CONV) if k != SSD_CONV // 2]
        cw = 2 * LANES
        for j in range(SSD_XBC // cw):
            cs = slice(j * cw, (j + 1) * cw)
            sh = jnp.dot(shift_ref[...], win_s[:, cs], preferred_element_type=f32)
            acc = cb_ref[:, cs] + win_s[halo:halo + q, cs].astype(f32) * cw_ref[SSD_CONV // 2:SSD_CONV // 2 + 1, cs]
            for n, k in enumerate(taps):
                acc = acc + sh[n * q:(n + 1) * q, :] * cw_ref[k:k + 1, cs]
            act = _silu(acc)
            if j < SSD_INNER // cw:
                xs_s[pl.ds(r0, q), cs] = act
            else:
                bc_s[pl.ds(r0, q), j * cw - SSD_INNER:(j + 1) * cw - SSD_INNER] = act.astype(bf16)
        dt_s[pl.ds(r0, q), :] = jax.nn.softplus(dt_ref[pl.ds(r0, q), :] + dtb_ref[...])

    conv_chunk(0)
    hf_s[...] = h0_ref[0]
    hb_s[...] = h0_ref[1]

    ri = lax.broadcasted_iota(jnp.int32, (q, q), 0)
    ci = lax.broadcasted_iota(jnp.int32, (q, q), 1)
    lower = ri >= ci
    upper = ci >= ri
    lane = lax.broadcasted_iota(jnp.int32, (q, LANES), 1)
    lo_half = lane < P

    def cumsums(dt):
        dta = dt * a2_row
        cf = jnp.dot(tri3_ref[...], jnp.concatenate(_split3(dta), axis=0), preferred_element_type=f32)
        rb = cf[q - 1:q, :] - cf + dta
        return cf, rb

    def load_chunk(r0):
        dt = dt_s[pl.ds(r0, q), :]
        xs = xs_s[pl.ds(r0, q), :]
        bcv = bc_s[pl.ds(r0, q), :]
        bmat = [bcv[:, g * N:(g + 1) * N] for g in range(SSD_GROUPS)]
        cmat = [bcv[:, SSD_BC + g * N:SSD_BC + (g + 1) * N] for g in range(SSD_GROUPS)]
        return dt, xs, bmat, cmat

    def inter_chunk(h_s, sel_ref, decay, weight, xs, bmat, cmat, dec_idx):
        ew = jnp.dot(jnp.concatenate([_pack3(decay), _pack3(weight)], axis=0), sel_ref[...],
                     preferred_element_type=f32)
        e_x, w_x = ew[:q], ew[q:]
        hb_ = h_s[...].astype(bf16)
        yoff = jnp.concatenate(
            [jnp.dot(cmat[g], hb_[:, g * gw:(g + 1) * gw], preferred_element_type=f32)
             for g in range(SSD_GROUPS)], axis=1)
        xw = (xs * w_x).astype(bf16)
        dec_row = e_x[dec_idx:dec_idx + 1, :]
        for g in range(SSD_GROUPS):
            gs = slice(g * gw, (g + 1) * gw)
            bt = jnp.transpose(bmat[g].astype(f32)).astype(bf16)
            upd = jnp.dot(bt, xw[:, gs], preferred_element_type=f32)
            h_s[:, gs] = h_s[:, gs] * dec_row[:, gs] + upd
        return yoff * e_x

    def finish(r0, y, xs):
        yy = y + xs * dskip_ref[...]
        zz = z_ref[pl.ds(r0, q), :].astype(f32)
        gated = yy * _silu(zz)
        ms = jnp.mean(gated * gated, axis=-1, keepdims=True)
        out_ref[pl.ds(r0, q), :] = (gated * lax.rsqrt(ms + EPS) * nw_ref[...]).astype(out_ref.dtype)

    def fwd_chunk(c, second_half):
        r0 = pl.multiple_of(c * q, q)
        dt, xs, bmat, cmat = load_chunk(r0)
        cf, rb = cumsums(dt)
        pcol = jnp.where(lane < H, cf, rb)
        bcast = jnp.dot(_pack3(pcol), selbc_ref[...], preferred_element_type=f32)
        prow = jnp.transpose(pcol - jnp.log2(dt))
        cbm = [lax.dot_general(cmat[g], bmat[g], (((1,), (1,)), ((), ())), preferred_element_type=f32)
               for g in range(SSD_GROUPS)]
        xsb = xs.astype(bf16)
        zero_b = jnp.zeros((), bf16)
        ypairs = []
        for k in range(H // 2):
            mats = []
            for h in (2 * k, 2 * k + 1):
                g = h // (H // SSD_GROUPS)
                hb_ = H + h
                segf = bcast[:, h * q:(h + 1) * q] - prow[h:h + 1, :]
                segb = bcast[:, hb_ * q:(hb_ + 1) * q] - prow[hb_:hb_ + 1, :]
                df = jnp.exp2(jnp.where(lower, segf, NEG_INF))
                db = jnp.exp2(jnp.where(upper, segb, NEG_INF))
                mats.append((cbm[g] * (df + db)).astype(bf16))
            xp = xsb[:, k * LANES:(k + 1) * LANES]
            xbd = jnp.concatenate([jnp.where(lo_half, xp, zero_b), jnp.where(lo_half, zero_b, xp)], axis=0)
            ypairs.append(jnp.dot(jnp.concatenate(mats, axis=1), xbd, preferred_element_type=f32))
        y = jnp.concatenate(ypairs, axis=1)
        wfa = jnp.exp2(cf[q - 1:q, :] - cf) * dt
        y = y + inter_chunk(hf_s, self_ref, jnp.exp2(cf), wfa, xs, bmat, cmat, q - 1)
        if second_half:
            finish(r0, y_s[pl.ds(r0, q), :] + y, xs)
        else:
            y_s[pl.ds(r0, q), :] = y

    def bwd_chunk(c, second_half):
        r0 = pl.multiple_of(c * q, q)
        dt, xs, bmat, cmat = load_chunk(r0)
        _, rb = cumsums(dt)
        wba = jnp.exp2(rb[0:1, :] - rb) * dt
        y = inter_chunk(hb_s, selb_ref, jnp.exp2(rb), wba, xs, bmat, cmat, 0)
        if second_half:
            finish(r0, y_s[pl.ds(r0, q), :] + y, xs)
        else:
            y_s[pl.ds(r0, q), :] = y

    half = nc // 2
    conv_chunk(nc - 1)

    def first_half(i, carry):
        fwd_chunk(i, False)
        bwd_chunk(nc - 1 - i, False)
        conv_chunk(i + 1)
        conv_chunk(nc - 2 - i)
        return carry

    def second_half(i, carry):
        fwd_chunk(i, True)
        bwd_chunk(nc - 1 - i, True)
        return carry

    lax.fori_loop(0, half - 1, first_half, 0)
    fwd_chunk(jnp.int32(half - 1), False)
    bwd_chunk(jnp.int32(half), False)
    lax.fori_loop(half, nc, second_half, 0)
    hfin_ref[0] = hf_s[...]
    hfin_ref[1] = hb_s[...]


def _ssd(xbc, dt, z, conv_w, conv_b, dtb, alog, dskip, nw, h0, bsz, seq_len):
    assert seq_len % (2 * SSD_CHUNK) == 0, "the two recurrences meet in the middle: even chunk count"
    one = pl.Buffered(1)
    seq = lambda w: pl.BlockSpec((seq_len, w), lambda b: (b, 0), pipeline_mode=one)
    const = lambda r, w: pl.BlockSpec((r, w), lambda b: (0, 0))
    st = pl.BlockSpec((None, 2, SSD_STATE, SSD_INNER), lambda b: (b, 0, 0, 0))
    sels = _ssd_selectors()
    return pl.pallas_call(
        functools.partial(_ssd_kernel, seq_len),
        out_shape=[jax.ShapeDtypeStruct((bsz * seq_len, SSD_INNER), bf16),
                   jax.ShapeDtypeStruct((bsz, 2, SSD_STATE, SSD_INNER), f32)],
        grid=(bsz,),
        in_specs=[seq(SSD_XBC), seq(LANES), seq(SSD_INNER),
                  const(8, SSD_XBC), const(1, SSD_XBC), const(1, LANES), const(1, LANES),
                  const(1, SSD_INNER), const(1, SSD_INNER)]
                 + [const(*s.shape) for s in sels] + [st],
        out_specs=[seq(SSD_INNER), st],
        scratch_shapes=[pltpu.VMEM((seq_len, SSD_INNER), f32),
                        pltpu.VMEM((seq_len, 2 * SSD_BC), bf16),
                        pltpu.VMEM((seq_len, LANES), f32),
                        pltpu.VMEM((seq_len, SSD_INNER), f32),
                        pltpu.VMEM((SSD_STATE, SSD_INNER), f32),
                        pltpu.VMEM((SSD_STATE, SSD_INNER), f32),
                        pltpu.VMEM((SSD_CHUNK + 32, SSD_XBC), bf16)],
        compiler_params=_cparams(("parallel",)),
        name="ssd",
    )(xbc, dt, z, conv_w, conv_b, dtb, alog, dskip, nw, *sels, h0)


def _attn_kernel(n_blocks, nsub, local, q_ref, g_ref, k_ref, kc_ref, *rest):
    nv = nsub + 2 if local else 0
    v_refs = rest[:nv]
    vc_ref, sink_ref, o_ref, s_s = rest[nv:]
    t = ATT_BLOCK
    i0 = pl.program_id(1) * nsub
    rpk = ATT_HEADS // ATT_KV_HEADS
    lane = lax.broadcasted_iota(jnp.int32, (t, LANES), 1)
    lo_half = lane < ATT_HEAD_DIM
    zero_b = jnp.zeros((), bf16)
    kl = lax.broadcasted_iota(jnp.int32, (t, t), 0)
    ql = lax.broadcasted_iota(jnp.int32, (t, t), 1)
    for sub in range(nsub):
        i = i0 + sub
        qv = q_ref[sub * t:(sub + 1) * t, :]
        if local:
            p0 = pl.multiple_of(jnp.maximum(i - 1, 0) * t, t)
            c0 = pl.multiple_of(i * t, t)
            n0 = pl.multiple_of(jnp.minimum(i + 1, n_blocks - 1) * t, t)
            bias_prev = jnp.where((kl >= ql) & (i > 0), 0.0, NEG_INF)
            bias_next = jnp.where((kl <= ql) & (i < n_blocks - 1), 0.0, NEG_INF)
            bias_prev = jnp.concatenate([bias_prev] * rpk, axis=1)
            bias_next = jnp.concatenate([bias_next] * rpk, axis=1)
        for j in range(ATT_KV_HEADS):
            ls = slice(j * LANES, (j + 1) * LANES)
            if local:
                kk = jnp.concatenate([k_ref[pl.ds(p0, t), ls], k_ref[pl.ds(c0, t), ls],
                                      k_ref[pl.ds(n0, t), ls], kc_ref[:, ls]], axis=0)
            else:
                kk = kc_ref[:, ls]
            pieces = []
            for r in range(rpk):
                hq = j * rpk + r
                qp = qv[:, (hq // 2) * LANES:(hq // 2 + 1) * LANES]
                keep = lo_half if hq % 2 == 0 else jnp.logical_not(lo_half)
                pieces.append(jnp.where(keep, qp, zero_b))
            q4 = jnp.concatenate(pieces, axis=0)
            s = lax.dot_general(kk, q4, (((1,), (1,)), ((), ())), preferred_element_type=f32)
            if local:
                s = jnp.concatenate([s[:t] + bias_prev, s[t:2 * t], s[2 * t:3 * t] + bias_next, s[3 * t:]],
                                    axis=0)
            s_s[sub, j] = s
    for sub in range(nsub):
        outs = []
        for j in range(ATT_KV_HEADS):
            vs = slice(j * ATT_HEAD_DIM, (j + 1) * ATT_HEAD_DIM)
            if local:
                vvt = jnp.concatenate([v_refs[sub + k][vs, :] for k in range(3)] + [vc_ref[vs, :]], axis=1)
            else:
                vvt = vc_ref[vs, :]
            sk = jnp.concatenate([jnp.full((1, t), sink_ref[j * rpk + r] * LOG2E, f32) for r in range(rpk)],
                                 axis=1)
            s = s_s[sub, j]
            m = jnp.maximum(jnp.max(s, axis=0, keepdims=True), sk)
            p = jnp.exp2(s - m)
            den = jnp.sum(p, axis=0, keepdims=True) + jnp.exp2(sk - m)
            vvt = jnp.concatenate([vvt, vvt], axis=0)
            ot = jnp.dot(vvt, p.astype(bf16), preferred_element_type=f32) / den
            o4 = [jnp.transpose(ot[:, r * t:(r + 1) * t]) for r in range(rpk)]
            outs.append(jnp.where(lo_half, o4[0], o4[1]))
            outs.append(jnp.where(lo_half, o4[2], o4[3]))
        o = jnp.concatenate(outs, axis=1)
        gv = g_ref[sub * t:(sub + 1) * t, :].astype(f32)
        o_ref[sub * t:(sub + 1) * t, :] = (o * _silu(gv)).astype(o_ref.dtype)


def _attention(q, g, k, vt, kc, vct, sink, bsz, seq_len, local):
    t = ATT_BLOCK
    nb = seq_len // t
    nsub = max(s for s in range(1, ATT_SUB + 1) if nb % s == 0)
    n_ctx = kc.shape[0] // bsz
    kw = ATT_KV_HEADS * LANES
    vw = ATT_KVW
    blk = pl.BlockSpec((nsub * t, ATT_Q), lambda b, i: (b * (nb // nsub) + i, 0))
    full = lambda n: pl.BlockSpec((n, kw), lambda b, i: (b, 0))
    vblk = lambda off: pl.BlockSpec((vw, t), lambda b, i: (0, b * nb + jnp.clip(i * nsub + off, 0, nb - 1)))
    vspecs = [vblk(off) for off in range(-1, nsub + 1)] if local else []
    return pl.pallas_call(
        functools.partial(_attn_kernel, nb, nsub, local),
        out_shape=jax.ShapeDtypeStruct((bsz * seq_len, ATT_Q), bf16),
        grid=(bsz, nb // nsub),
        in_specs=[blk, blk, full(k.shape[0] // bsz), full(n_ctx)] + vspecs
                 + [pl.BlockSpec((vw, n_ctx), lambda b, i: (0, b)), pl.BlockSpec(memory_space=pltpu.SMEM)],
        out_specs=blk,
        scratch_shapes=[pltpu.VMEM((nsub, ATT_KV_HEADS, (3 * t if local else 0) + n_ctx,
                                    (ATT_HEADS // ATT_KV_HEADS) * t), f32)],
        compiler_params=_cparams(("parallel", "arbitrary")),
        name="attention",
    )(q, g, k, kc, *([vt] * len(vspecs)), vct, sink)


def _s5_disc_kernel(lre_ref, lim_ref, ls_ref, bre_ref, bim_ref, abre_ref, abim_ref, bbre_ref, bbim_ref):
    lam_re = lre_ref[...]
    lam_im = lim_ref[...]
    dt = jnp.exp(ls_ref[...])
    mag = jnp.exp(lam_re * dt)
    ab_re = mag * jnp.cos(lam_im * dt)
    ab_im = mag * jnp.sin(lam_im * dt)
    num_re, num_im = ab_re - 1.0, ab_im
    den = lam_re * lam_re + lam_im * lam_im
    coef_re = (num_re * lam_re + num_im * lam_im) / den
    coef_im = (num_im * lam_re - num_re * lam_im) / den
    b_re, b_im = bre_ref[...], bim_ref[...]
    abre_ref[...] = ab_re
    abim_ref[...] = ab_im
    bbre_ref[...] = coef_re * b_re - coef_im * b_im
    bbim_ref[...] = coef_re * b_im + coef_im * b_re


def _s5_discretise(lam_re, lam_im, log_step, b_re, b_im):
    g, n, cg = b_re.shape
    exp = lambda t: jnp.repeat(t.reshape(2 * g, n), cg, axis=1)
    ls = jnp.broadcast_to(log_step.reshape(2 * g, 1), (2 * g, n * cg))
    bb = lambda t: jnp.tile(t.reshape(g, n * cg), (2, 1))
    shp = jax.ShapeDtypeStruct((2 * g, n * cg), f32)
    ab_re, ab_im, bb_re, bb_im = pl.pallas_call(
        _s5_disc_kernel, out_shape=[shp] * 4, name="s5_disc",
    )(exp(lam_re), exp(lam_im), ls, bb(b_re), bb(b_im))
    first = lambda t: t.reshape(2, g, n, cg)[..., 0]
    full = lambda t: t.reshape(2, g, n, cg)
    return first(ab_re), first(ab_im), full(bb_re), full(bb_im)


def _cmul(ar, ai, br, bi):
    return ar * br - ai * bi, ar * bi + ai * br


def _s5_kernel(n_lat, n_ctx, *refs):
    tc = S5_TC
    xl, xc, arow_ref, acol_ref, bbd_ref, cbd_ref, y_ref, wyz_s, ws_s, sl_s, sc_s = refs
    hw = S5_HGRP * S5_STATE
    hl = LANES // 2
    xw = tc * hl
    nsl = hw // LANES
    nb = 8
    lo = lax.broadcasted_iota(jnp.int32, (1, LANES), 1) < hl

    def powers(re, im, n):
        out = [(jnp.ones_like(re), jnp.zeros_like(im))]
        for _ in range(n):
            out.append(_cmul(out[-1][0], out[-1][1], re, im))
        return out

    steps = lambda h, p: (2 * p, 2 * p + 1) if h == 0 else (2 * p + 1, 2 * p)

    prow = []
    for h in range(2):
        pr = [powers(arow_ref[h, 2 * d:2 * d + 1, :], arow_ref[h, 2 * d + 1:2 * d + 2, :], tc) for d in range(2)]
        prow.append(pr)
        for s in range(tc):
            for d, k in ((0, tc - 1 - s), (1, s)):
                wr, wi = _cmul(bbd_ref[2 * d, h], bbd_ref[2 * d + 1, h], *pr[d][k])
                ws_s[h, s * hl:(s + 1) * hl, d * 2 * hw:d * 2 * hw + hw] = wr.astype(bf16)
                ws_s[h, s * hl:(s + 1) * hl, d * 2 * hw + hw:(d + 1) * 2 * hw] = wi.astype(bf16)
        crhs = jnp.concatenate([cbd_ref[0, h], -cbd_ref[1, h]], axis=0).astype(bf16)
        kall = [jnp.dot(ws_s[h, :, d * 2 * hw:(d + 1) * 2 * hw], crhs, preferred_element_type=f32)
                for d in range(2)]
        kf = [kall[0][(tc - 1 - k) * hl:(tc - k) * hl] for k in range(tc)]
        kb = [kall[1][k * hl:(k + 1) * hl] for k in range(tc)]
        lag = lambda s, t: kf[t - s] if t > s else (kb[s - t] if t < s else kf[0] + kb[0])
        for s in range(tc):
            for p in range(tc // 2):
                t0, t1 = steps(h, p)
                wyz_s[h, s * hl:(s + 1) * hl, p * LANES:(p + 1) * LANES] = (
                    jnp.where(lo, lag(s, t0), lag(s, t1)).astype(bf16))
        for d in range(2):
            pc = powers(acol_ref[h, 2 * d], acol_ref[h, 2 * d + 1], tc)
            kk = (lambda t: t + 1) if d == 0 else (lambda t: tc - t)
            r0 = xw + d * 2 * hw
            for p in range(tc // 2):
                t0, t1 = steps(h, p)
                ar = jnp.where(lo, pc[kk(t0)][0], pc[kk(t1)][0])
                ai = jnp.where(lo, pc[kk(t0)][1], pc[kk(t1)][1])
                dre, dim_ = _cmul(cbd_ref[0, h], cbd_ref[1, h], ar, ai)
                wyz_s[h, r0:r0 + hw, p * LANES:(p + 1) * LANES] = dre.astype(bf16)
                wyz_s[h, r0 + hw:r0 + 2 * hw, p * LANES:(p + 1) * LANES] = (-dim_).astype(bf16)

    def rows_of(x_ref, b, n, h):
        return x_ref[b * n:(b + 1) * n, h * xw:(h + 1) * xw]

    def inject(x_ref, s_ref, n):
        for b in range(nb):
            for h in range(2):
                sb = jnp.dot(rows_of(x_ref, b, n, h), ws_s[h], preferred_element_type=f32)
                for k in range(4 * nsl):
                    s_ref[h * 4 * nsl + k, pl.ds(b, n, stride=nb), :] = sb[:, k * LANES:(k + 1) * LANES]

    at = [[[tuple(jnp.broadcast_to(p[:, k * LANES:(k + 1) * LANES], (nb, LANES)) for p in prow[h][d][tc])
            for k in range(nsl)] for d in range(2)] for h in range(2)]

    def scan(s_ref, n, init):
        def step(i, carry):
            idxs = [pl.ds(pl.multiple_of(i * nb, nb), nb), pl.ds(pl.multiple_of((n - 1 - i) * nb, nb), nb)]
            chains = [(h, d, k) for h in range(2) for d in range(2) for k in range(nsl)]
            slab = lambda h, d, k: h * 4 * nsl + d * 2 * nsl + k
            inj = [(s_ref[slab(h, d, k), idxs[d], :], s_ref[slab(h, d, k) + nsl, idxs[d], :])
                   for h, d, k in chains]
            new = []
            for c, (h, d, k) in enumerate(chains):
                hr, hi = carry[2 * c], carry[2 * c + 1]
                s_ref[slab(h, d, k), idxs[d], :] = hr
                s_ref[slab(h, d, k) + nsl, idxs[d], :] = hi
                ar, ai = at[h][d][k]
                new += [ar * hr - ai * hi + inj[c][0], ar * hi + ai * hr + inj[c][1]]
            return tuple(new)
        return lax.fori_loop(0, n, step, init, unroll=4)

    inject(xc, sc_s, n_ctx)
    h_ctx = scan(sc_s, n_ctx, tuple(jnp.zeros((nb, LANES), f32) for _ in range(8 * nsl)))
    inject(xl, sl_s, n_lat)
    scan(sl_s, n_lat, h_ctx)

    for b in range(nb):
        yh = []
        for h in range(2):
            hin = jnp.concatenate([sl_s[h * 4 * nsl + k, pl.ds(b, n_lat, stride=nb), :]
                                   for k in range(4 * nsl)], axis=1).astype(bf16)
            yh.append(jnp.dot(jnp.concatenate([rows_of(xl, b, n_lat, h), hin], axis=1), wyz_s[h],
                              preferred_element_type=f32))
        for p in range(tc // 2):
            y0, y1 = yh[0][:, p * LANES:(p + 1) * LANES], yh[1][:, p * LANES:(p + 1) * LANES]
            even = jnp.where(lo, y0, y1)
            odd = pltpu.roll(jnp.where(lo, y1, y0), hl, axis=1)
            y_ref[pl.ds(b * n_lat * tc + 2 * p, n_lat, stride=tc), :] = even
            y_ref[pl.ds(b * n_lat * tc + 2 * p + 1, n_lat, stride=tc), :] = odd


def _s5_mix(u, u_c, arow, acol, bbd, cbd, bsz, seq_len, n_ctx_tok):
    assert bsz == 8, "the chunk recurrence puts the batch on the 8 sublanes"
    tc = S5_TC
    n_lat, n_ctx = seq_len // tc, n_ctx_tok // tc
    nblk = S5_GROUPS // S5_GBLK
    hw = S5_HGRP * S5_STATE
    hl = LANES // 2
    nsl = hw // LANES
    one = pl.Buffered(1)
    xspec = lambda rows: pl.BlockSpec((None, rows, tc * LANES), lambda g: (g, 0, 0))
    return pl.pallas_call(
        functools.partial(_s5_kernel, n_lat, n_ctx),
        out_shape=jax.ShapeDtypeStruct((nblk, bsz * seq_len, LANES), f32),
        grid=(nblk,),
        in_specs=[xspec(bsz * n_lat), xspec(bsz * n_ctx),
                  pl.BlockSpec((2, 4, hw), lambda g: (g, 0, 0), pipeline_mode=one),
                  pl.BlockSpec((2, 4, hw, LANES), lambda g: (g, 0, 0, 0), pipeline_mode=one),
                  pl.BlockSpec((4, 2, hl, hw), lambda g: (0, g, 0, 0), pipeline_mode=one),
                  pl.BlockSpec((2, 2, hw, LANES), lambda g: (0, g, 0, 0), pipeline_mode=one)],
        out_specs=pl.BlockSpec((None, bsz * seq_len, LANES), lambda g: (g, 0, 0), pipeline_mode=one),
        scratch_shapes=[pltpu.VMEM((2, tc * hl + 4 * hw, tc * hl), bf16),
                        pltpu.VMEM((2, tc * hl, 4 * hw), bf16),
                        pltpu.VMEM((8 * nsl, bsz * n_lat, LANES), f32),
                        pltpu.VMEM((8 * nsl, bsz * n_ctx, LANES), f32)],
        compiler_params=_cparams(("arbitrary",)),
        name="s5_mix",
    )(u, u_c, arow, acol, bbd, cbd)


def _s5_block_params(ab_re, ab_im, bb_re, bb_im, c_re, c_im):
    g, n, cg = S5_GROUPS, S5_STATE, S5_GROUP_CH
    nh = g // S5_HGRP
    hw = S5_HGRP * n
    arow = jnp.stack([t[d].reshape(nh, hw) for d in range(2) for t in (ab_re, ab_im)], axis=1)
    acol = jnp.broadcast_to(arow[..., None], (nh, 4, hw, LANES))

    def blockdiag(t, rows_per, cols_per):
        tiled = jnp.concatenate([t] * S5_HGRP, axis=-1)
        r = np.arange(t.shape[-2])[:, None] // rows_per
        c = np.arange(S5_HGRP * cols_per)[None, :] // cols_per
        return jnp.where(jnp.asarray(r == c), tiled, 0.0)

    bb = jnp.stack([t[d] for d in range(2) for t in (bb_re, bb_im)], axis=0)
    bb = jnp.swapaxes(bb.reshape(4, nh, S5_HGRP, n, cg), -1, -2).reshape(4, nh, S5_HGRP * cg, n)
    bbd = blockdiag(bb, cg, n)
    cc = jnp.swapaxes(jnp.stack([c_re, c_im], axis=0).reshape(2, nh, S5_HGRP, cg, n), -1, -2)
    cbd = blockdiag(cc.reshape(2, nh, hw, cg), n, cg)
    return arow, acol, bbd, jnp.concatenate([cbd, cbd], axis=-1)


def _s5_out_kernel(y_ref, u_ref, g_ref, x_ref, dskip_ref, gw_ref, gb_ref, w_ref, gate_ref,
                   fw_ref, o_ref):
    y = jnp.concatenate([y_ref[j] for j in range(y_ref.shape[0])], axis=1)
    y = y + dskip_ref[...] * u_ref[...].astype(f32)
    y = jax.nn.gelu(y)
    glu = jnp.dot(y.astype(bf16), gw_ref[...], preferred_element_type=f32) + gb_ref[...]
    y = y * jax.nn.sigmoid(glu)
    y = y * _silu(g_ref[...].astype(f32))
    x = x_ref[...] + gate_ref[...] * jnp.dot(y.astype(bf16), w_ref[...], preferred_element_type=f32)
    ms = jnp.mean(x * x, axis=-1, keepdims=True)
    o_ref[...] = x * lax.rsqrt(ms + EPS) * fw_ref[...]


def _s5_out(y, u, g, x, dskip, gw, gb, w, gate, fw, rows_per_mod):
    m, d = x.shape
    tm = min(ROW_TILE, rows_per_mod)
    per = rows_per_mod // tm
    row = lambda: pl.BlockSpec((tm, d), lambda i: (i, 0))
    vec = lambda: pl.BlockSpec((1, d), lambda i: (0, 0))
    mat = lambda: pl.BlockSpec((d, d), lambda i: (0, 0))
    return pl.pallas_call(
        _s5_out_kernel,
        out_shape=jax.ShapeDtypeStruct((m, d), f32),
        grid=(m // tm,),
        in_specs=[pl.BlockSpec((y.shape[0], tm, LANES), lambda i: (0, i, 0)),
                  row(), row(), row(), vec(), mat(), vec(), mat(),
                  pl.BlockSpec((None, 1, d), lambda i: (i // per, 0, 0)), vec()],
        out_specs=row(),
        compiler_params=_cparams(("parallel",)),
        name="s5_out",
    )(y, u, g, x, dskip, gw, gb, w, gate, fw)


def _even_weights(w_in):
    o = 0
    z = w_in[:, o:o + SSD_INNER]; o += SSD_INNER
    xbc = w_in[:, o:o + SSD_XBC]; o += SSD_XBC
    dt = w_in[:, o:o + 2 * SSD_HEADS]; o += 2 * SSD_HEADS
    q = w_in[:, o:o + ATT_Q]; o += ATT_Q
    k = w_in[:, o:o + ATT_KVW]; o += ATT_KVW
    v = w_in[:, o:o + ATT_KVW]; o += ATT_KVW
    g = w_in[:, o:o + ATT_Q]
    d = w_in.shape[0]
    dup = lambda t: jnp.concatenate([t.reshape(d, ATT_KV_HEADS, 1, ATT_HEAD_DIM)] * 2, axis=2).reshape(d, -1)
    dtp = jnp.pad(dt, ((0, 0), (0, LANES - 2 * SSD_HEADS)))
    cast = lambda t: t.astype(bf16)
    return [cast(t) for t in (z, xbc, q, dup(k), g, dtp)], cast(v.T)


def _even_segs(rope):
    scale = ATT_HEAD_DIM ** -0.5 * LOG2E
    widths = [(SSD_INNER, None, bf16), (SSD_XBC, None, bf16),
              (ATT_Q, scale, bf16),
              (ATT_KV_HEADS * LANES, 1.0 if rope else None, bf16),
              (ATT_Q, None, bf16), (LANES, None, f32)]
    return [(i, w, r, dtp, False) for i, (w, r, dtp) in enumerate(widths)]


def _pad_lanes(v, n=LANES):
    v = v.reshape(1, -1)
    return jnp.pad(v, ((0, 0), (0, n - v.shape[1])))


def kernel(x, c, ctx, c_ctx, e_norm_w, e_ada_w, e_ada_b, e_w_in, e_conv_w, e_conv_b, e_dt_bias,
           e_a_log, e_d_skip, e_ssd_norm_w, e_sink, e_w_out, o_norm_w, o_ada_w, o_ada_b, o_w_in,
           o_lam_re, o_lam_im, o_log_step, o_b_re, o_b_im, o_c_re, o_c_im, o_d_skip, o_glu_w,
           o_glu_b, o_w_out, final_norm_w):
    bsz, seq_len, d = x.shape
    n_ctx = ctx.shape[1]
    xf = x.reshape(bsz * seq_len, d)
    xcf = ctx.reshape(bsz * n_ctx, d)

    cvecs = jnp.concatenate([c, c_ctx[None, :], jnp.zeros((16 - bsz - 1, d), f32)], axis=0)

    def modulation(ada_w, ada_b):
        mod = _adaln(cvecs, ada_w.astype(bf16), ada_b.reshape(1, -1))
        parts = [mod[:, k * d:(k + 1) * d] for k in range(3)]
        lat = [p[:bsz].reshape(bsz, 1, d) for p in parts]
        cx = [p[bsz:bsz + 1].reshape(1, 1, d) for p in parts]
        return lat, cx

    (shift, scale, gate), (shift_c, scale_c, gate_c) = modulation(e_ada_w[0], e_ada_b[0])
    w_in, w_vt = _even_weights(e_w_in[0])
    nw = e_norm_w[0].reshape(1, d)
    tabs = _rope_tables(seq_len)
    vseg = [(0, ATT_KVW, bf16)]
    z, xbc, q, k, g, dt, vt = _inproj(xf, shift, scale, nw, w_in, _even_segs(True), seq_len, tabs, w_vt, vseg)
    z_c, xbc_c, q_c, k_c, g_c, dt_c, vt_c = _inproj(xcf, shift_c, scale_c, nw, w_in, _even_segs(False), n_ctx,
                                                    None, w_vt, vseg)

    conv_w = jnp.pad(e_conv_w[0], ((0, 8 - SSD_CONV), (0, 0)))
    conv_b = e_conv_b[0].reshape(1, -1)
    dtb = _pad_lanes(e_dt_bias[0])
    alog = _pad_lanes(e_a_log[0])
    dskip = jnp.repeat(e_d_skip[0], SSD_HEAD_DIM).reshape(1, -1)
    snw = e_ssd_norm_w[0].reshape(1, -1)
    h0 = jnp.zeros((bsz, 2, SSD_STATE, SSD_INNER), f32)
    ssd_c, hfin = _ssd(xbc_c, dt_c, z_c, conv_w, conv_b, dtb, alog, dskip, snw, h0, bsz, n_ctx)
    ssd_o, _ = _ssd(xbc, dt, z, conv_w, conv_b, dtb, alog, dskip, snw, hfin, bsz, seq_len)

    sink = e_sink[0]
    att = _attention(q, g, k, vt, k_c, vt_c, sink, bsz, seq_len, True)
    att_c = _attention(q_c, g_c, k_c, vt_c, k_c, vt_c, sink, bsz, n_ctx, False)
    w_out = e_w_out[0].astype(bf16)
    even_gate, even_gate_c = gate, gate_c

    (shift, scale, gate), (shift_c, scale_c, _) = modulation(o_ada_w[0], o_ada_b[0])
    w_u = o_w_in[0][:, :S5_WIDTH].astype(bf16)
    w_g = o_w_in[0][:, S5_WIDTH:].astype(bf16)
    nw = o_norm_w[0].reshape(1, d)
    x1, u, u_ch, g2 = _inproj(xf, shift, scale, nw, [w_u, w_g],
                              [(0, S5_WIDTH, None, bf16, False), (0, S5_WIDTH, None, bf16, True),
                               (1, S5_WIDTH, None, bf16, False)], seq_len,
                              pre=(ssd_o, att, w_out, even_gate))
    _, uc_ch = _inproj(xcf, shift_c, scale_c, nw, [w_u], [(0, S5_WIDTH, None, bf16, True)], n_ctx,
                       pre=(ssd_c, att_c, w_out, even_gate_c))

    ab_re, ab_im, bb_re, bb_im = _s5_discretise(o_lam_re[0], o_lam_im[0], o_log_step[0], o_b_re[0], o_b_im[0])
    arow, acol, bbd, cbd = _s5_block_params(ab_re, ab_im, bb_re, bb_im, o_c_re[0], o_c_im[0])
    y = _s5_mix(u_ch, uc_ch, arow, acol, bbd, cbd, bsz, seq_len, n_ctx)
    out = _s5_out(y, u, g2, x1, o_d_skip[0].reshape(1, -1),
                  o_glu_w[0].astype(bf16), o_glu_b[0].reshape(1, -1), o_w_out[0].astype(bf16), gate,
                  final_norm_w.reshape(1, -1), seq_len)
    return out.reshape(bsz, seq_len, d)
```

```python
import functools
import math

import jax
import jax.numpy as jnp
import numpy as np
from jax import lax
from jax.experimental import pallas as pl
from jax.experimental.pallas import tpu as pltpu

f32 = jnp.float32
bf16 = jnp.bfloat16

D_MODEL = 1024
GRID_W = 64
EPS = 1e-6
NEG_INF = -1e30

SSD_HEADS = 16
SSD_HEAD_DIM = 64
SSD_GROUPS = 2
SSD_STATE = 128
SSD_CONV = 5
SSD_CHUNK = 128
SSD_INNER = SSD_HEADS * SSD_HEAD_DIM
SSD_BC = SSD_GROUPS * SSD_STATE
SSD_XBC = SSD_INNER + 2 * SSD_BC
ATT_HEADS = 16
ATT_KV_HEADS = 4
ATT_HEAD_DIM = 64
ATT_BLOCK = 128
ROPE_THETA = 10000.0
ATT_Q = ATT_HEADS * ATT_HEAD_DIM
ATT_KVW = ATT_KV_HEADS * ATT_HEAD_DIM
S5_WIDTH = 1024
S5_GROUP_CH = 16
S5_GROUPS = S5_WIDTH // S5_GROUP_CH
S5_STATE = 64

LOG2E = math.log2(math.e)
LANES = 128
ROW_TILE = 512
ATT_SUB = 4
S5_GBLK = 8
S5_TC = 8
S5_HGRP = 4
VMEM_LIMIT = 56 * 1024 * 1024


def _cparams(sem, flags=None):
    return pltpu.CompilerParams(dimension_semantics=sem, vmem_limit_bytes=VMEM_LIMIT, flags=flags)


def _silu(x):
    h = 0.5 * x
    return h + h * jnp.tanh(h)


def _adaln_kernel(c_ref, w_ref, b_ref, o_ref):
    c = c_ref[...]
    s = _silu(c).astype(bf16)
    o_ref[...] = jnp.dot(s, w_ref[...], preferred_element_type=f32) + b_ref[...]


def _adaln(cvecs, w, b):
    r, d = cvecs.shape
    n = w.shape[1]
    tn = 1024
    return pl.pallas_call(
        _adaln_kernel,
        out_shape=jax.ShapeDtypeStruct((r, n), f32),
        grid=(n // tn,),
        in_specs=[pl.BlockSpec((r, d), lambda j: (0, 0)),
                  pl.BlockSpec((d, tn), lambda j: (0, j)),
                  pl.BlockSpec((1, tn), lambda j: (0, j))],
        out_specs=pl.BlockSpec((r, tn), lambda j: (0, j)),
        compiler_params=_cparams(("arbitrary",)),
        name="adaln",
    )(cvecs, w, b)


def _inproj_kernel(segs, tsegs, n_w, has_rope, has_pre, x_ref, shift_ref, scale_ref, nw_ref, *rest):
    if has_pre:
        a_ref, b_ref, wo_ref, gate_ref = rest[:4]
        rest = rest[4:]
    w_refs, rest = rest[:n_w], rest[n_w:]
    if tsegs:
        wt_ref, rest = rest[0], rest[1:]
    if has_rope:
        cos_ref, sina_ref, sinb_ref = rest[:3]
        rest = rest[3:]
    if has_pre:
        x1_ref, rest = rest[0], rest[1:]
    outs = rest[:len(segs)]
    touts = rest[len(segs):len(segs) + len(tsegs)]
    rest = rest[len(segs) + len(tsegs):]
    slab_s = rest[0] if rest else None
    x = x_ref[...]
    if has_pre:
        ka = a_ref.shape[1]
        acc = jnp.dot(a_ref[...], wo_ref[:ka, :], preferred_element_type=f32)
        acc = acc + jnp.dot(b_ref[...], wo_ref[ka:, :], preferred_element_type=f32)
        x = x + gate_ref[...] * acc
        x1_ref[...] = x
    ms = jnp.mean(x * x, axis=-1, keepdims=True)
    h = (x * lax.rsqrt(ms + EPS)) * nw_ref[...]
    h = h * (1.0 + scale_ref[...]) + shift_ref[...]
    hb = h.astype(bf16)
    for (start, width, _), o_ref in zip(tsegs, touts):
        acc_t = lax.dot_general(wt_ref[start:start + width, :], hb, (((1,), (1,)), ((), ())),
                                preferred_element_type=f32)
        o_ref[...] = acc_t.astype(o_ref.dtype)
    products = {}
    for (widx, width, rope, _, chunked), o_ref in zip(segs, outs):
        if widx not in products:
            products[widx] = jnp.dot(hb, w_refs[widx][...], preferred_element_type=f32)
        acc = products[widx]
        if chunked:
            tm = acc.shape[0]
            hl = LANES // 2
            lo = lax.broadcasted_iota(jnp.int32, (1, LANES), 1) < hl
            for j in range(width // LANES):
                slab_s[j] = acc[:, j * LANES:(j + 1) * LANES]
            for j in range(width // LANES):
                rows = [slab_s[j, pl.ds(s, tm // S5_TC, stride=S5_TC), :] for s in range(S5_TC)]
                for p in range(S5_TC // 2):
                    a, b = rows[2 * p], rows[2 * p + 1]
                    h0 = jnp.where(lo, a, pltpu.roll(b, hl, axis=1))
                    h1 = jnp.where(lo, pltpu.roll(a, hl, axis=1), b)
                    o_ref[j, :, p * LANES:(p + 1) * LANES] = h0.astype(o_ref.dtype)
                    o_ref[j, :, S5_TC * hl + p * LANES:S5_TC * hl + (p + 1) * LANES] = h1.astype(o_ref.dtype)
            continue
        if rope is not None and not has_rope:
            acc = acc * rope
        elif rope is not None:
            rep = width // LANES
            cos = jnp.concatenate([cos_ref[...]] * rep, axis=1) * rope
            sina = jnp.concatenate([sina_ref[...]] * rep, axis=1) * rope
            sinb = jnp.concatenate([sinb_ref[...]] * rep, axis=1) * rope
            half = ATT_HEAD_DIM // 2
            up = pltpu.roll(acc, width - half, axis=1)
            dn = pltpu.roll(acc, half, axis=1)
            acc = acc * cos + up * sina + dn * sinb
        o_ref[...] = acc.astype(o_ref.dtype)


def _inproj(x, shift, scale, nw, w, segs, rows_per_mod, rope_tabs=None, wt=None, tsegs=(), pre=None):
    m, d = x.shape
    tm = min(ROW_TILE, rows_per_mod)
    per = rows_per_mod // tm
    nmod = shift.shape[0]
    mod_idx = (lambda i: (i // per, 0, 0)) if nmod > 1 else (lambda i: (0, 0, 0))
    in_specs = [pl.BlockSpec((tm, d), lambda i: (i, 0)),
                pl.BlockSpec((None, 1, d), mod_idx),
                pl.BlockSpec((None, 1, d), mod_idx),
                pl.BlockSpec((1, d), lambda i: (0, 0))]
    args = [x, shift, scale, nw]
    if pre is not None:
        a, b, w_out, gate = pre
        gate_idx = (lambda i: (i // per, 0, 0)) if gate.shape[0] > 1 else (lambda i: (0, 0, 0))
        in_specs += [pl.BlockSpec((tm, a.shape[1]), lambda i: (i, 0)),
                     pl.BlockSpec((tm, b.shape[1]), lambda i: (i, 0)),
                     pl.BlockSpec(w_out.shape, lambda i: (0, 0)),
                     pl.BlockSpec((None, 1, d), gate_idx)]
        args += [a, b, w_out, gate]
    in_specs += [pl.BlockSpec(wi.shape, lambda i: (0, 0)) for wi in w]
    args += list(w)
    if tsegs:
        in_specs.append(pl.BlockSpec(wt.shape, lambda i: (0, 0)))
        args.append(wt)
    if rope_tabs is not None:
        for t in rope_tabs:
            in_specs.append(pl.BlockSpec((tm, LANES), lambda i: (i % per, 0)))
            args.append(t)
    out_shape, out_specs = [], []
    if pre is not None:
        out_shape.append(jax.ShapeDtypeStruct((m, d), f32))
        out_specs.append(pl.BlockSpec((tm, d), lambda i: (i, 0)))
    for sg in segs:
        if sg[4]:
            out_shape.append(jax.ShapeDtypeStruct((sg[1] // LANES, m // S5_TC, S5_TC * LANES), sg[3]))
            out_specs.append(pl.BlockSpec((sg[1] // LANES, tm // S5_TC, S5_TC * LANES), lambda i: (0, i, 0)))
        else:
            out_shape.append(jax.ShapeDtypeStruct((m, sg[1]), sg[3]))
            out_specs.append(pl.BlockSpec((tm, sg[1]), lambda i: (i, 0)))
    for _, width, dtp in tsegs:
        out_shape.append(jax.ShapeDtypeStruct((width, m), dtp))
        out_specs.append(pl.BlockSpec((width, tm), lambda i: (0, i)))
    chunked_w = [sg[1] for sg in segs if sg[4]]
    scratch = [pltpu.VMEM((max(chunked_w) // LANES, tm, LANES), f32)] if chunked_w else []
    return pl.pallas_call(
        functools.partial(_inproj_kernel, tuple(segs), tuple(tsegs), len(w), rope_tabs is not None,
                          pre is not None),
        out_shape=out_shape,
        grid=(m // tm,),
        in_specs=in_specs,
        out_specs=out_specs,
        scratch_shapes=scratch,
        compiler_params=_cparams(("parallel",)),
        name="inproj",
    )(*args)


def _rope_tables(seq_len):
    rows = seq_len // GRID_W
    row = jnp.repeat(jnp.arange(rows, dtype=f32), GRID_W)
    col = jnp.tile(jnp.arange(GRID_W, dtype=f32), rows)
    n_freq = ATT_HEAD_DIM // 4
    inv = ROPE_THETA ** (-jnp.arange(n_freq, dtype=f32) / n_freq)
    ang = jnp.concatenate([row[:, None] * inv, col[:, None] * inv], axis=-1)
    cos, sin = jnp.cos(ang), jnp.sin(ang)
    zero = jnp.zeros_like(sin)
    cos_h = jnp.concatenate([cos, cos], axis=-1)
    sina_h = jnp.concatenate([-sin, zero], axis=-1)
    sinb_h = jnp.concatenate([zero, sin], axis=-1)
    two = lambda t: jnp.concatenate([t, t], axis=-1)
    return two(cos_h), two(sina_h), two(sinb_h)


SSD_PACK = 32


def _split3(x):
    hi = x.astype(bf16)
    r1 = x - hi.astype(f32)
    mid = r1.astype(bf16)
    lo = (r1 - mid.astype(f32)).astype(bf16)
    return hi, mid, lo


def _pack3(x):
    hi, mid, lo = _split3(x)
    lane = lax.broadcasted_iota(jnp.int32, x.shape, 1)
    mid_r = pltpu.roll(mid.astype(f32), SSD_PACK, axis=1)
    lo_r = pltpu.roll(lo.astype(f32), 2 * SSD_PACK, axis=1)
    packed = jnp.where(lane < SSD_PACK, hi.astype(f32),
                       jnp.where(lane < 2 * SSD_PACK, mid_r,
                                 jnp.where(lane < 3 * SSD_PACK, lo_r, 0.0)))
    return packed.astype(bf16)


def _ssd_selectors():
    k = np.arange(LANES)
    src = np.where(k < 3 * SSD_PACK, k % SSD_PACK, -1)
    col_blk = np.arange(SSD_PACK * SSD_CHUNK) // SSD_CHUNK
    sel_bc = (src[:, None] == col_blk[None, :])
    head = np.arange(SSD_INNER) // SSD_HEAD_DIM
    sel_f = (src[:, None] == head[None, :])
    sel_b = (src[:, None] == (head + SSD_HEADS)[None, :])
    tri3 = np.tile(np.tril(np.ones((SSD_CHUNK, SSD_CHUNK))), (1, 3))
    rows = np.arange(SSD_CHUNK)[:, None]
    cols = np.arange(SSD_CHUNK + 32)[None, :]
    half = SSD_CONV // 2
    shift = np.concatenate([cols == rows + 16 + d for d in range(-half, half + 1) if d != 0], axis=0)
    as_bf = lambda a: jnp.asarray(a.astype(np.float32), dtype=bf16)
    return as_bf(sel_bc), as_bf(sel_f), as_bf(sel_b), as_bf(tri3), as_bf(shift)


def _ssd_kernel(seq_len, xbc_ref, dt_ref, z_ref, cw_ref, cb_ref, dtb_ref, alog_ref, dskip_ref, nw_ref,
                selbc_ref, self_ref, selb_ref, tri3_ref, shift_ref, h0_ref, out_ref, hfin_ref,
                xs_s, bc_s, dt_s, y_s, hf_s, hb_s, win_s):
    q = SSD_CHUNK
    nc = seq_len // q
    halo = 16
    H, P, N = SSD_HEADS, SSD_HEAD_DIM, SSD_STATE
    gw = (H // SSD_GROUPS) * P
    a2_row = -jnp.exp(alog_ref[...]) * math.log2(math.e)

    def conv_chunk(c):
        r0 = pl.multiple_of(c * q, q)
        pstart = pl.multiple_of(jnp.maximum(r0 - halo, 0), halo)
        nstart = pl.multiple_of(jnp.minimum(r0 + q, seq_len - halo), halo)
        zero = jnp.zeros((), bf16)
        win_s[0:halo, :] = jnp.where(c > 0, xbc_ref[pl.ds(pstart, halo), :], zero)
        win_s[halo:halo + q, :] = xbc_ref[pl.ds(r0, q), :]
        win_s[halo + q:, :] = jnp.where(c < nc - 1, xbc_ref[pl.ds(nstart, halo), :], zero)
        taps = [k for k in range(SSD_CONV) if k != SSD_CONV // 2]
        cw = 2 * LANES
        for j in range(SSD_XBC // cw):
            cs = slice(j * cw, (j + 1) * cw)
            sh = jnp.dot(shift_ref[...], win_s[:, cs], preferred_element_type=f32)
            acc = cb_ref[:, cs] + win_s[halo:halo + q, cs].astype(f32) * cw_ref[SSD_CONV // 2:SSD_CONV // 2 + 1, cs]
            for n, k in enumerate(taps):
                acc = acc + sh[n * q:(n + 1) * q, :] * cw_ref[k:k + 1, cs]
            act = _silu(acc)
            if j < SSD_INNER // cw:
                xs_s[pl.ds(r0, q), cs] = act
            else:
                bc_s[pl.ds(r0, q), j * cw - SSD_INNER:(j + 1) * cw - SSD_INNER] = act.astype(bf16)
        dt_s[pl.ds(r0, q), :] = jax.nn.softplus(dt_ref[pl.ds(r0, q), :] + dtb_ref[...])

    conv_chunk(0)
    hf_s[...] = h0_ref[0]
    hb_s[...] = h0_ref[1]

    ri = lax.broadcasted_iota(jnp.int32, (q, q), 0)
    ci = lax.broadcasted_iota(jnp.int32, (q, q), 1)
    lower = ri >= ci
    upper = ci >= ri
    lane = lax.broadcasted_iota(jnp.int32, (q, LANES), 1)
    lo_half = lane < P

    def cumsums(dt):
        dta = dt * a2_row
        cf = jnp.dot(tri3_ref[...], jnp.concatenate(_split3(dta), axis=0), preferred_element_type=f32)
        rb = cf[q - 1:q, :] - cf + dta
        return cf, rb

    def load_chunk(r0):
        dt = dt_s[pl.ds(r0, q), :]
        xs = xs_s[pl.ds(r0, q), :]
        bcv = bc_s[pl.ds(r0, q), :]
        bmat = [bcv[:, g * N:(g + 1) * N] for g in range(SSD_GROUPS)]
        cmat = [bcv[:, SSD_BC + g * N:SSD_BC + (g + 1) * N] for g in range(SSD_GROUPS)]
        return dt, xs, bmat, cmat

    def inter_chunk(h_s, sel_ref, decay, weight, xs, bmat, cmat, dec_idx):
        ew = jnp.dot(jnp.concatenate([_pack3(decay), _pack3(weight)], axis=0), sel_ref[...],
                     preferred_element_type=f32)
        e_x, w_x = ew[:q], ew[q:]
        hb_ = h_s[...].astype(bf16)
        yoff = jnp.concatenate(
            [jnp.dot(cmat[g], hb_[:, g * gw:(g + 1) * gw], preferred_element_type=f32)
             for g in range(SSD_GROUPS)], axis=1)
        xw = (xs * w_x).astype(bf16)
        dec_row = e_x[dec_idx:dec_idx + 1, :]
        for g in range(SSD_GROUPS):
            gs = slice(g * gw, (g + 1) * gw)
            bt = jnp.transpose(bmat[g].astype(f32)).astype(bf16)
            upd = jnp.dot(bt, xw[:, gs], preferred_element_type=f32)
            h_s[:, gs] = h_s[:, gs] * dec_row[:, gs] + upd
        return yoff * e_x

    def finish(r0, y, xs):
        yy = y + xs * dskip_ref[...]
        zz = z_ref[pl.ds(r0, q), :].astype(f32)
        gated = yy * _silu(zz)
        ms = jnp.mean(gated * gated, axis=-1, keepdims=True)
        out_ref[pl.ds(r0, q), :] = (gated * lax.rsqrt(ms + EPS) * nw_ref[...]).astype(out_ref.dtype)

    def fwd_chunk(c, second_half):
        r0 = pl.multiple_of(c * q, q)
        dt, xs, bmat, cmat = load_chunk(r0)
        cf, rb = cumsums(dt)
        pcol = jnp.where(lane < H, cf, rb)
        bcast = jnp.dot(_pack3(pcol), selbc_ref[...], preferred_element_type=f32)
        prow = jnp.transpose(pcol - jnp.log2(dt))
        cbm = [lax.dot_general(cmat[g], bmat[g], (((1,), (1,)), ((), ())), preferred_element_type=f32)
               for g in range(SSD_GROUPS)]
        xsb = xs.astype(bf16)
        zero_b = jnp.zeros((), bf16)
        ypairs = []
        for k in range(H // 2):
            mats = []
            for h in (2 * k, 2 * k + 1):
                g = h // (H // SSD_GROUPS)
                hb_ = H + h
                segf = bcast[:, h * q:(h + 1) * q] - prow[h:h + 1, :]
                segb = bcast[:, hb_ * q:(hb_ + 1) * q] - prow[hb_:hb_ + 1, :]
                df = jnp.exp2(jnp.where(lower, segf, NEG_INF))
                db = jnp.exp2(jnp.where(upper, segb, NEG_INF))
                mats.append((cbm[g] * (df + db)).astype(bf16))
            xp = xsb[:, k * LANES:(k + 1) * LANES]
            xbd = jnp.concatenate([jnp.where(lo_half, xp, zero_b), jnp.where(lo_half, zero_b, xp)], axis=0)
            ypairs.append(jnp.dot(jnp.concatenate(mats, axis=1), xbd, preferred_element_type=f32))
        y = jnp.concatenate(ypairs, axis=1)
        wfa = jnp.exp2(cf[q - 1:q, :] - cf) * dt
        y = y + inter_chunk(hf_s, self_ref, jnp.exp2(cf), wfa, xs, bmat, cmat, q - 1)
        if second_half:
            finish(r0, y_s[pl.ds(r0, q), :] + y, xs)
        else:
            y_s[pl.ds(r0, q), :] = y

    def bwd_chunk(c, second_half):
        r0 = pl.multiple_of(c * q, q)
        dt, xs, bmat, cmat = load_chunk(r0)
        _, rb = cumsums(dt)
        wba = jnp.exp2(rb[0:1, :] - rb) * dt
        y = inter_chunk(hb_s, selb_ref, jnp.exp2(rb), wba, xs, bmat, cmat, 0)
        if second_half:
            finish(r0, y_s[pl.ds(r0, q), :] + y, xs)
        else:
            y_s[pl.ds(r0, q), :] = y

    half = nc // 2
    conv_chunk(nc - 1)

    def first_half(i, carry):
        fwd_chunk(i, False)
        bwd_chunk(nc - 1 - i, False)
        conv_chunk(i + 1)
        conv_chunk(nc - 2 - i)
        return carry

    def second_half(i, carry):
        fwd_chunk(i, True)
        bwd_chunk(nc - 1 - i, True)
        return carry

    lax.fori_loop(0, half - 1, first_half, 0)
    fwd_chunk(jnp.int32(half - 1), False)
    bwd_chunk(jnp.int32(half), False)
    lax.fori_loop(half, nc, second_half, 0)
    hfin_ref[0] = hf_s[...]
    hfin_ref[1] = hb_s[...]


def _ssd(xbc, dt, z, conv_w, conv_b, dtb, alog, dskip, nw, h0, bsz, seq_len):
    assert seq_len % (2 * SSD_CHUNK) == 0, "the two recurrences meet in the middle: even chunk count"
    one = pl.Buffered(1)
    seq = lambda w: pl.BlockSpec((seq_len, w), lambda b: (b, 0), pipeline_mode=one)
    const = lambda r, w: pl.BlockSpec((r, w), lambda b: (0, 0))
    st = pl.BlockSpec((None, 2, SSD_STATE, SSD_INNER), lambda b: (b, 0, 0, 0))
    sels = _ssd_selectors()
    return pl.pallas_call(
        functools.partial(_ssd_kernel, seq_len),
        out_shape=[jax.ShapeDtypeStruct((bsz * seq_len, SSD_INNER), bf16),
                   jax.ShapeDtypeStruct((bsz, 2, SSD_STATE, SSD_INNER), f32)],
        grid=(bsz,),
        in_specs=[pl.BlockSpec((seq_len, SSD_XBC), lambda b: (b, 0)), seq(LANES),
                  pl.BlockSpec((seq_len, SSD_INNER), lambda b: (b, 0)),
                  const(8, SSD_XBC), const(1, SSD_XBC), const(1, LANES), const(1, LANES),
                  const(1, SSD_INNER), const(1, SSD_INNER)]
                 + [const(*s.shape) for s in sels] + [st],
        out_specs=[seq(SSD_INNER), st],
        scratch_shapes=[pltpu.VMEM((seq_len, SSD_INNER), f32),
                        pltpu.VMEM((seq_len, 2 * SSD_BC), bf16),
                        pltpu.VMEM((seq_len, LANES), f32),
                        pltpu.VMEM((seq_len, SSD_INNER), f32),
                        pltpu.VMEM((SSD_STATE, SSD_INNER), f32),
                        pltpu.VMEM((SSD_STATE, SSD_INNER), f32),
                        pltpu.VMEM((SSD_CHUNK + 32, SSD_XBC), bf16)],
        compiler_params=_cparams(("parallel",)),
        name="ssd",
    )(xbc, dt, z, conv_w, conv_b, dtb, alog, dskip, nw, *sels, h0)


def _attn_kernel(n_blocks, nsub, local, q_ref, g_ref, k_ref, kc_ref, *rest):
    nv = nsub + 2 if local else 0
    v_refs = rest[:nv]
    vc_ref, sink_ref, o_ref, s_s = rest[nv:]
    t = ATT_BLOCK
    i0 = pl.program_id(1) * nsub
    rpk = ATT_HEADS // ATT_KV_HEADS
    lane = lax.broadcasted_iota(jnp.int32, (t, LANES), 1)
    lo_half = lane < ATT_HEAD_DIM
    zero_b = jnp.zeros((), bf16)
    kl = lax.broadcasted_iota(jnp.int32, (t, t), 0)
    ql = lax.broadcasted_iota(jnp.int32, (t, t), 1)
    for sub in range(nsub):
        i = i0 + sub
        qv = q_ref[sub * t:(sub + 1) * t, :]
        if local:
            p0 = pl.multiple_of(jnp.maximum(i - 1, 0) * t, t)
            c0 = pl.multiple_of(i * t, t)
            n0 = pl.multiple_of(jnp.minimum(i + 1, n_blocks - 1) * t, t)
            bias_prev = jnp.where((kl >= ql) & (i > 0), 0.0, NEG_INF)
            bias_next = jnp.where((kl <= ql) & (i < n_blocks - 1), 0.0, NEG_INF)
            bias_prev = jnp.concatenate([bias_prev] * rpk, axis=1)
            bias_next = jnp.concatenate([bias_next] * rpk, axis=1)
        for j in range(ATT_KV_HEADS):
            ls = slice(j * LANES, (j + 1) * LANES)
            if local:
                kk = jnp.concatenate([k_ref[pl.ds(p0, t), ls], k_ref[pl.ds(c0, t), ls],
                                      k_ref[pl.ds(n0, t), ls], kc_ref[:, ls]], axis=0)
            else:
                kk = kc_ref[:, ls]
            pieces = []
            for r in range(rpk):
                hq = j * rpk + r
                qp = qv[:, (hq // 2) * LANES:(hq // 2 + 1) * LANES]
                keep = lo_half if hq % 2 == 0 else jnp.logical_not(lo_half)
                pieces.append(jnp.where(keep, qp, zero_b))
            q4 = jnp.concatenate(pieces, axis=0)
            s = lax.dot_general(kk, q4, (((1,), (1,)), ((), ())), preferred_element_type=f32)
            if local:
                s = jnp.concatenate([s[:t] + bias_prev, s[t:2 * t], s[2 * t:3 * t] + bias_next, s[3 * t:]],
                                    axis=0)
            s_s[sub, j] = s
    for sub in range(nsub):
        outs = []
        for j in range(ATT_KV_HEADS):
            vs = slice(j * ATT_HEAD_DIM, (j + 1) * ATT_HEAD_DIM)
            if local:
                vvt = jnp.concatenate([v_refs[sub + k][vs, :] for k in range(3)] + [vc_ref[vs, :]], axis=1)
            else:
                vvt = vc_ref[vs, :]
            sk = jnp.concatenate([jnp.full((1, t), sink_ref[j * rpk + r] * LOG2E, f32) for r in range(rpk)],
                                 axis=1)
            s = s_s[sub, j]
            m = jnp.maximum(jnp.max(s, axis=0, keepdims=True), sk)
            p = jnp.exp2(s - m)
            den = jnp.sum(p, axis=0, keepdims=True) + jnp.exp2(sk - m)
            vvt = jnp.concatenate([vvt, vvt], axis=0)
            ot = jnp.dot(vvt, p.astype(bf16), preferred_element_type=f32) / den
            o4 = [jnp.transpose(ot[:, r * t:(r + 1) * t]) for r in range(rpk)]
            outs.append(jnp.where(lo_half, o4[0], o4[1]))
            outs.append(jnp.where(lo_half, o4[2], o4[3]))
        o = jnp.concatenate(outs, axis=1)
        gv = g_ref[sub * t:(sub + 1) * t, :].astype(f32)
        o_ref[sub * t:(sub + 1) * t, :] = (o * _silu(gv)).astype(o_ref.dtype)


def _attention(q, g, k, vt, kc, vct, sink, bsz, seq_len, local):
    t = ATT_BLOCK
    nb = seq_len // t
    nsub = max(s for s in range(1, ATT_SUB + 1) if nb % s == 0)
    n_ctx = kc.shape[0] // bsz
    kw = ATT_KV_HEADS * LANES
    vw = ATT_KVW
    blk = pl.BlockSpec((nsub * t, ATT_Q), lambda b, i: (b * (nb // nsub) + i, 0))
    full = lambda n: pl.BlockSpec((n, kw), lambda b, i: (b, 0))
    vblk = lambda off: pl.BlockSpec((vw, t), lambda b, i: (0, b * nb + jnp.clip(i * nsub + off, 0, nb - 1)))
    vspecs = [vblk(off) for off in range(-1, nsub + 1)] if local else []
    return pl.pallas_call(
        functools.partial(_attn_kernel, nb, nsub, local),
        out_shape=jax.ShapeDtypeStruct((bsz * seq_len, ATT_Q), bf16),
        grid=(bsz, nb // nsub),
        in_specs=[blk, blk, full(k.shape[0] // bsz), full(n_ctx)] + vspecs
                 + [pl.BlockSpec((vw, n_ctx), lambda b, i: (0, b)), pl.BlockSpec(memory_space=pltpu.SMEM)],
        out_specs=blk,
        scratch_shapes=[pltpu.VMEM((nsub, ATT_KV_HEADS, (3 * t if local else 0) + n_ctx,
                                    (ATT_HEADS // ATT_KV_HEADS) * t), f32)],
        compiler_params=_cparams(("parallel", "arbitrary")),
        name="attention",
    )(q, g, k, kc, *([vt] * len(vspecs)), vct, sink)


def _s5_disc_kernel(lre_ref, lim_ref, ls_ref, bre_ref, bim_ref, abre_ref, abim_ref, bbre_ref, bbim_ref):
    lam_re = lre_ref[...]
    lam_im = lim_ref[...]
    dt = jnp.exp(ls_ref[...])
    mag = jnp.exp(lam_re * dt)
    ab_re = mag * jnp.cos(lam_im * dt)
    ab_im = mag * jnp.sin(lam_im * dt)
    num_re, num_im = ab_re - 1.0, ab_im
    den = lam_re * lam_re + lam_im * lam_im
    coef_re = (num_re * lam_re + num_im * lam_im) / den
    coef_im = (num_im * lam_re - num_re * lam_im) / den
    b_re, b_im = bre_ref[...], bim_ref[...]
    abre_ref[...] = ab_re
    abim_ref[...] = ab_im
    bbre_ref[...] = coef_re * b_re - coef_im * b_im
    bbim_ref[...] = coef_re * b_im + coef_im * b_re


def _s5_discretise(lam_re, lam_im, log_step, b_re, b_im):
    g, n, cg = b_re.shape
    exp = lambda t: jnp.repeat(t.reshape(2 * g, n), cg, axis=1)
    ls = jnp.broadcast_to(log_step.reshape(2 * g, 1), (2 * g, n * cg))
    bb = lambda t: jnp.tile(t.reshape(g, n * cg), (2, 1))
    shp = jax.ShapeDtypeStruct((2 * g, n * cg), f32)
    ab_re, ab_im, bb_re, bb_im = pl.pallas_call(
        _s5_disc_kernel, out_shape=[shp] * 4, name="s5_disc",
    )(exp(lam_re), exp(lam_im), ls, bb(b_re), bb(b_im))
    first = lambda t: t.reshape(2, g, n, cg)[..., 0]
    full = lambda t: t.reshape(2, g, n, cg)
    return first(ab_re), first(ab_im), full(bb_re), full(bb_im)


def _cmul(ar, ai, br, bi):
    return ar * br - ai * bi, ar * bi + ai * br


def _s5_kernel(n_lat, n_ctx, *refs):
    tc = S5_TC
    xl, xc, arow_ref, acol_ref, bbd_ref, cbd_ref, y_ref, wyz_s, ws_s, sl_s, sc_s = refs
    hw = S5_HGRP * S5_STATE
    hl = LANES // 2
    xw = tc * hl
    nsl = hw // LANES
    nb = 8
    lo = lax.broadcasted_iota(jnp.int32, (1, LANES), 1) < hl

    def powers(re, im, n):
        out = [(jnp.ones_like(re), jnp.zeros_like(im))]
        for _ in range(n):
            out.append(_cmul(out[-1][0], out[-1][1], re, im))
        return out

    steps = lambda h, p: (2 * p, 2 * p + 1) if h == 0 else (2 * p + 1, 2 * p)

    prow = []
    for h in range(2):
        pr = [powers(arow_ref[h, 2 * d:2 * d + 1, :], arow_ref[h, 2 * d + 1:2 * d + 2, :], tc) for d in range(2)]
        prow.append(pr)
        for s in range(tc):
            for d, k in ((0, tc - 1 - s), (1, s)):
                wr, wi = _cmul(bbd_ref[2 * d, h], bbd_ref[2 * d + 1, h], *pr[d][k])
                ws_s[h, s * hl:(s + 1) * hl, d * 2 * hw:d * 2 * hw + hw] = wr.astype(bf16)
                ws_s[h, s * hl:(s + 1) * hl, d * 2 * hw + hw:(d + 1) * 2 * hw] = wi.astype(bf16)
        crhs = jnp.concatenate([cbd_ref[0, h], -cbd_ref[1, h]], axis=0).astype(bf16)
        kall = [jnp.dot(ws_s[h, :, d * 2 * hw:(d + 1) * 2 * hw], crhs, preferred_element_type=f32)
                for d in range(2)]
        kf = [kall[0][(tc - 1 - k) * hl:(tc - k) * hl] for k in range(tc)]
        kb = [kall[1][k * hl:(k + 1) * hl] for k in range(tc)]
        lag = lambda s, t: kf[t - s] if t > s else (kb[s - t] if t < s else kf[0] + kb[0])
        for s in range(tc):
            for p in range(tc // 2):
                t0, t1 = steps(h, p)
                wyz_s[h, s * hl:(s + 1) * hl, p * LANES:(p + 1) * LANES] = (
                    jnp.where(lo, lag(s, t0), lag(s, t1)).astype(bf16))
        for d in range(2):
            pc = powers(acol_ref[h, 2 * d], acol_ref[h, 2 * d + 1], tc)
            kk = (lambda t: t + 1) if d == 0 else (lambda t: tc - t)
            r0 = xw + d * 2 * hw
            for p in range(tc // 2):
                t0, t1 = steps(h, p)
                ar = jnp.where(lo, pc[kk(t0)][0], pc[kk(t1)][0])
                ai = jnp.where(lo, pc[kk(t0)][1], pc[kk(t1)][1])
                dre, dim_ = _cmul(cbd_ref[0, h], cbd_ref[1, h], ar, ai)
                wyz_s[h, r0:r0 + hw, p * LANES:(p + 1) * LANES] = dre.astype(bf16)
                wyz_s[h, r0 + hw:r0 + 2 * hw, p * LANES:(p + 1) * LANES] = (-dim_).astype(bf16)

    def rows_of(x_ref, b, n, h):
        return x_ref[b * n:(b + 1) * n, h * xw:(h + 1) * xw]

    def inject(x_ref, s_ref, n):
        for b in range(nb):
            for h in range(2):
                sb = jnp.dot(rows_of(x_ref, b, n, h), ws_s[h], preferred_element_type=f32)
                for k in range(4 * nsl):
                    s_ref[h * 4 * nsl + k, pl.ds(b, n, stride=nb), :] = sb[:, k * LANES:(k + 1) * LANES]

    at = [[[tuple(jnp.broadcast_to(p[:, k * LANES:(k + 1) * LANES], (nb, LANES)) for p in prow[h][d][tc])
            for k in range(nsl)] for d in range(2)] for h in range(2)]

    def scan(s_ref, n, init):
        def step(i, carry):
            idxs = [pl.ds(pl.multiple_of(i * nb, nb), nb), pl.ds(pl.multiple_of((n - 1 - i) * nb, nb), nb)]
            chains = [(h, d, k) for h in range(2) for d in range(2) for k in range(nsl)]
            slab = lambda h, d, k: h * 4 * nsl + d * 2 * nsl + k
            inj = [(s_ref[slab(h, d, k), idxs[d], :], s_ref[slab(h, d, k) + nsl, idxs[d], :])
                   for h, d, k in chains]
            new = []
            for c, (h, d, k) in enumerate(chains):
                hr, hi = carry[2 * c], carry[2 * c + 1]
                s_ref[slab(h, d, k), idxs[d], :] = hr
                s_ref[slab(h, d, k) + nsl, idxs[d], :] = hi
                ar, ai = at[h][d][k]
                new += [ar * hr - ai * hi + inj[c][0], ar * hi + ai * hr + inj[c][1]]
            return tuple(new)
        return lax.fori_loop(0, n, step, init, unroll=4)

    inject(xc, sc_s, n_ctx)
    h_ctx = scan(sc_s, n_ctx, tuple(jnp.zeros((nb, LANES), f32) for _ in range(8 * nsl)))
    inject(xl, sl_s, n_lat)
    scan(sl_s, n_lat, h_ctx)

    for b in range(nb):
        yh = []
        for h in range(2):
            hin = jnp.concatenate([sl_s[h * 4 * nsl + k, pl.ds(b, n_lat, stride=nb), :]
                                   for k in range(4 * nsl)], axis=1).astype(bf16)
            yh.append(jnp.dot(jnp.concatenate([rows_of(xl, b, n_lat, h), hin], axis=1), wyz_s[h],
                              preferred_element_type=f32))
        for p in range(tc // 2):
            y0, y1 = yh[0][:, p * LANES:(p + 1) * LANES], yh[1][:, p * LANES:(p + 1) * LANES]
            even = jnp.where(lo, y0, y1)
            odd = pltpu.roll(jnp.where(lo, y1, y0), hl, axis=1)
            y_ref[pl.ds(b * n_lat * tc + 2 * p, n_lat, stride=tc), :] = even
            y_ref[pl.ds(b * n_lat * tc + 2 * p + 1, n_lat, stride=tc), :] = odd


def _s5_mix(u, u_c, arow, acol, bbd, cbd, bsz, seq_len, n_ctx_tok):
    assert bsz == 8, "the chunk recurrence puts the batch on the 8 sublanes"
    tc = S5_TC
    n_lat, n_ctx = seq_len // tc, n_ctx_tok // tc
    nblk = S5_GROUPS // S5_GBLK
    hw = S5_HGRP * S5_STATE
    hl = LANES // 2
    nsl = hw // LANES
    one = pl.Buffered(1)
    xspec = lambda rows: pl.BlockSpec((None, rows, tc * LANES), lambda g: (g, 0, 0))
    return pl.pallas_call(
        functools.partial(_s5_kernel, n_lat, n_ctx),
        out_shape=jax.ShapeDtypeStruct((nblk, bsz * seq_len, LANES), f32),
        grid=(nblk,),
        in_specs=[xspec(bsz * n_lat), xspec(bsz * n_ctx),
                  pl.BlockSpec((2, 4, hw), lambda g: (g, 0, 0), pipeline_mode=one),
                  pl.BlockSpec((2, 4, hw, LANES), lambda g: (g, 0, 0, 0), pipeline_mode=one),
                  pl.BlockSpec((4, 2, hl, hw), lambda g: (0, g, 0, 0), pipeline_mode=one),
                  pl.BlockSpec((2, 2, hw, LANES), lambda g: (0, g, 0, 0), pipeline_mode=one)],
        out_specs=pl.BlockSpec((None, bsz * seq_len, LANES), lambda g: (g, 0, 0)),
        scratch_shapes=[pltpu.VMEM((2, tc * hl + 4 * hw, tc * hl), bf16),
                        pltpu.VMEM((2, tc * hl, 4 * hw), bf16),
                        pltpu.VMEM((8 * nsl, bsz * n_lat, LANES), f32),
                        pltpu.VMEM((8 * nsl, bsz * n_ctx, LANES), f32)],
        compiler_params=_cparams(("arbitrary",)),
        name="s5_mix",
    )(u, u_c, arow, acol, bbd, cbd)


def _s5_block_params(ab_re, ab_im, bb_re, bb_im, c_re, c_im):
    g, n, cg = S5_GROUPS, S5_STATE, S5_GROUP_CH
    nh = g // S5_HGRP
    hw = S5_HGRP * n
    arow = jnp.stack([t[d].reshape(nh, hw) for d in range(2) for t in (ab_re, ab_im)], axis=1)
    acol = jnp.broadcast_to(arow[..., None], (nh, 4, hw, LANES))

    def blockdiag(t, rows_per, cols_per):
        tiled = jnp.concatenate([t] * S5_HGRP, axis=-1)
        r = np.arange(t.shape[-2])[:, None] // rows_per
        c = np.arange(S5_HGRP * cols_per)[None, :] // cols_per
        return jnp.where(jnp.asarray(r == c), tiled, 0.0)

    bb = jnp.stack([t[d] for d in range(2) for t in (bb_re, bb_im)], axis=0)
    bb = jnp.swapaxes(bb.reshape(4, nh, S5_HGRP, n, cg), -1, -2).reshape(4, nh, S5_HGRP * cg, n)
    bbd = blockdiag(bb, cg, n)
    cc = jnp.swapaxes(jnp.stack([c_re, c_im], axis=0).reshape(2, nh, S5_HGRP, cg, n), -1, -2)
    cbd = blockdiag(cc.reshape(2, nh, hw, cg), n, cg)
    return arow, acol, bbd, jnp.concatenate([cbd, cbd], axis=-1)


def _s5_out_kernel(y_ref, u_ref, g_ref, x_ref, dskip_ref, gw_ref, gb_ref, w_ref, gate_ref,
                   fw_ref, o_ref):
    y = jnp.concatenate([y_ref[j] for j in range(y_ref.shape[0])], axis=1)
    y = y + dskip_ref[...] * u_ref[...].astype(f32)
    y = jax.nn.gelu(y)
    glu = jnp.dot(y.astype(bf16), gw_ref[...], preferred_element_type=f32) + gb_ref[...]
    y = y * jax.nn.sigmoid(glu)
    y = y * _silu(g_ref[...].astype(f32))
    x = x_ref[...] + gate_ref[...] * jnp.dot(y.astype(bf16), w_ref[...], preferred_element_type=f32)
    ms = jnp.mean(x * x, axis=-1, keepdims=True)
    o_ref[...] = x * lax.rsqrt(ms + EPS) * fw_ref[...]


def _s5_out(y, u, g, x, dskip, gw, gb, w, gate, fw, rows_per_mod):
    m, d = x.shape
    tm = min(ROW_TILE, rows_per_mod)
    per = rows_per_mod // tm
    row = lambda: pl.BlockSpec((tm, d), lambda i: (i, 0))
    vec = lambda: pl.BlockSpec((1, d), lambda i: (0, 0))
    mat = lambda: pl.BlockSpec((d, d), lambda i: (0, 0))
    return pl.pallas_call(
        _s5_out_kernel,
        out_shape=jax.ShapeDtypeStruct((m, d), f32),
        grid=(m // tm,),
        in_specs=[pl.BlockSpec((y.shape[0], tm, LANES), lambda i: (0, i, 0)),
                  row(), row(), row(), vec(), mat(), vec(), mat(),
                  pl.BlockSpec((None, 1, d), lambda i: (i // per, 0, 0)), vec()],
        out_specs=row(),
        compiler_params=_cparams(("parallel",)),
        name="s5_out",
    )(y, u, g, x, dskip, gw, gb, w, gate, fw)


def _even_weights(w_in):
    o = 0
    z = w_in[:, o:o + SSD_INNER]; o += SSD_INNER
    xbc = w_in[:, o:o + SSD_XBC]; o += SSD_XBC
    dt = w_in[:, o:o + 2 * SSD_HEADS]; o += 2 * SSD_HEADS
    q = w_in[:, o:o + ATT_Q]; o += ATT_Q
    k = w_in[:, o:o + ATT_KVW]; o += ATT_KVW
    v = w_in[:, o:o + ATT_KVW]; o += ATT_KVW
    g = w_in[:, o:o + ATT_Q]
    d = w_in.shape[0]
    dup = lambda t: jnp.concatenate([t.reshape(d, ATT_KV_HEADS, 1, ATT_HEAD_DIM)] * 2, axis=2).reshape(d, -1)
    dtp = jnp.pad(dt, ((0, 0), (0, LANES - 2 * SSD_HEADS)))
    cast = lambda t: t.astype(bf16)
    return [cast(t) for t in (z, xbc, q, dup(k), g, dtp)], cast(v.T)


def _even_segs(rope):
    scale = ATT_HEAD_DIM ** -0.5 * LOG2E
    widths = [(SSD_INNER, None, bf16), (SSD_XBC, None, bf16),
              (ATT_Q, scale, bf16),
              (ATT_KV_HEADS * LANES, 1.0 if rope else None, bf16),
              (ATT_Q, None, bf16), (LANES, None, f32)]
    return [(i, w, r, dtp, False) for i, (w, r, dtp) in enumerate(widths)]


def _pad_lanes(v, n=LANES):
    v = v.reshape(1, -1)
    return jnp.pad(v, ((0, 0), (0, n - v.shape[1])))


def kernel(x, c, ctx, c_ctx, e_norm_w, e_ada_w, e_ada_b, e_w_in, e_conv_w, e_conv_b, e_dt_bias,
           e_a_log, e_d_skip, e_ssd_norm_w, e_sink, e_w_out, o_norm_w, o_ada_w, o_ada_b, o_w_in,
           o_lam_re, o_lam_im, o_log_step, o_b_re, o_b_im, o_c_re, o_c_im, o_d_skip, o_glu_w,
           o_glu_b, o_w_out, final_norm_w):
    bsz, seq_len, d = x.shape
    n_ctx = ctx.shape[1]
    xf = x.reshape(bsz * seq_len, d)
    xcf = ctx.reshape(bsz * n_ctx, d)

    cvecs = jnp.concatenate([c, c_ctx[None, :], jnp.zeros((16 - bsz - 1, d), f32)], axis=0)

    def modulation(ada_w, ada_b):
        mod = _adaln(cvecs, ada_w.astype(bf16), ada_b.reshape(1, -1))
        parts = [mod[:, k * d:(k + 1) * d] for k in range(3)]
        lat = [p[:bsz].reshape(bsz, 1, d) for p in parts]
        cx = [p[bsz:bsz + 1].reshape(1, 1, d) for p in parts]
        return lat, cx

    (shift, scale, gate), (shift_c, scale_c, gate_c) = modulation(e_ada_w[0], e_ada_b[0])
    w_in, w_vt = _even_weights(e_w_in[0])
    nw = e_norm_w[0].reshape(1, d)
    tabs = _rope_tables(seq_len)
    vseg = [(0, ATT_KVW, bf16)]
    z, xbc, q, k, g, dt, vt = _inproj(xf, shift, scale, nw, w_in, _even_segs(True), seq_len, tabs, w_vt, vseg)
    z_c, xbc_c, q_c, k_c, g_c, dt_c, vt_c = _inproj(xcf, shift_c, scale_c, nw, w_in, _even_segs(False), n_ctx,
                                                    None, w_vt, vseg)

    conv_w = jnp.pad(e_conv_w[0], ((0, 8 - SSD_CONV), (0, 0)))
    conv_b = e_conv_b[0].reshape(1, -1)
    dtb = _pad_lanes(e_dt_bias[0])
    alog = _pad_lanes(e_a_log[0])
    dskip = jnp.repeat(e_d_skip[0], SSD_HEAD_DIM).reshape(1, -1)
    snw = e_ssd_norm_w[0].reshape(1, -1)
    h0 = jnp.zeros((bsz, 2, SSD_STATE, SSD_INNER), f32)
    ssd_c, hfin = _ssd(xbc_c, dt_c, z_c, conv_w, conv_b, dtb, alog, dskip, snw, h0, bsz, n_ctx)
    ssd_o, _ = _ssd(xbc, dt, z, conv_w, conv_b, dtb, alog, dskip, snw, hfin, bsz, seq_len)

    sink = e_sink[0]
    att = _attention(q, g, k, vt, k_c, vt_c, sink, bsz, seq_len, True)
    att_c = _attention(q_c, g_c, k_c, vt_c, k_c, vt_c, sink, bsz, n_ctx, False)
    w_out = e_w_out[0].astype(bf16)
    even_gate, even_gate_c = gate, gate_c

    (shift, scale, gate), (shift_c, scale_c, _) = modulation(o_ada_w[0], o_ada_b[0])
    w_u = o_w_in[0][:, :S5_WIDTH].astype(bf16)
    w_g = o_w_in[0][:, S5_WIDTH:].astype(bf16)
    nw = o_norm_w[0].reshape(1, d)
    x1, u, u_ch, g2 = _inproj(xf, shift, scale, nw, [w_u, w_g],
                              [(0, S5_WIDTH, None, bf16, False), (0, S5_WIDTH, None, bf16, True),
                               (1, S5_WIDTH, None, bf16, False)], seq_len,
                              pre=(ssd_o, att, w_out, even_gate))
    _, uc_ch = _inproj(xcf, shift_c, scale_c, nw, [w_u], [(0, S5_WIDTH, None, bf16, True)], n_ctx,
                       pre=(ssd_c, att_c, w_out, even_gate_c))

    ab_re, ab_im, bb_re, bb_im = _s5_discretise(o_lam_re[0], o_lam_im[0], o_log_step[0], o_b_re[0], o_b_im[0])
    arow, acol, bbd, cbd = _s5_block_params(ab_re, ab_im, bb_re, bb_im, o_c_re[0], o_c_im[0])
    y = _s5_mix(u_ch, uc_ch, arow, acol, bbd, cbd, bsz, seq_len, n_ctx)
    out = _s5_out(y, u, g2, x1, o_d_skip[0].reshape(1, -1),
                  o_glu_w[0].astype(bf16), o_glu_b[0].reshape(1, -1), o_w_out[0].astype(bf16), gate,
                  final_norm_w.reshape(1, -1), seq_len)
    return out.reshape(bsz, seq_len, d)
```

```python
import functools
import math

import jax
import jax.numpy as jnp
import numpy as np
from jax import lax
from jax.experimental import pallas as pl
from jax.experimental.pallas import tpu as pltpu

f32 = jnp.float32
bf16 = jnp.bfloat16

D_MODEL = 1024
GRID_W = 64
EPS = 1e-6
NEG_INF = -1e30

SSD_HEADS = 16
SSD_HEAD_DIM = 64
SSD_GROUPS = 2
SSD_STATE = 128
SSD_CONV = 5
SSD_CHUNK = 128
SSD_INNER = SSD_HEADS * SSD_HEAD_DIM
SSD_BC = SSD_GROUPS * SSD_STATE
SSD_XBC = SSD_INNER + 2 * SSD_BC
ATT_HEADS = 16
ATT_KV_HEADS = 4
ATT_HEAD_DIM = 64
ATT_BLOCK = 128
ROPE_THETA = 10000.0
ATT_Q = ATT_HEADS * ATT_HEAD_DIM
ATT_KVW = ATT_KV_HEADS * ATT_HEAD_DIM
S5_WIDTH = 1024
S5_GROUP_CH = 16
S5_GROUPS = S5_WIDTH // S5_GROUP_CH
S5_STATE = 64

LOG2E = math.log2(math.e)
LANES = 128
ROW_TILE = 512
ATT_SUB = 4
S5_GBLK = 8
S5_TC = 8
S5_HGRP = 4
VMEM_LIMIT = 60 * 1024 * 1024


def _cparams(sem, flags=None):
    return pltpu.CompilerParams(dimension_semantics=sem, vmem_limit_bytes=VMEM_LIMIT, flags=flags)


def _silu(x):
    h = 0.5 * x
    return h + h * jnp.tanh(h)


def _adaln_kernel(c_ref, w_ref, b_ref, o_ref):
    c = c_ref[...]
    s = _silu(c).astype(bf16)
    o_ref[...] = jnp.dot(s, w_ref[...], preferred_element_type=f32) + b_ref[...]


def _adaln(cvecs, w, b):
    r, d = cvecs.shape
    n = w.shape[1]
    tn = 1024
    return pl.pallas_call(
        _adaln_kernel,
        out_shape=jax.ShapeDtypeStruct((r, n), f32),
        grid=(n // tn,),
        in_specs=[pl.BlockSpec((r, d), lambda j: (0, 0)),
                  pl.BlockSpec((d, tn), lambda j: (0, j)),
                  pl.BlockSpec((1, tn), lambda j: (0, j))],
        out_specs=pl.BlockSpec((r, tn), lambda j: (0, j)),
        compiler_params=_cparams(("arbitrary",)),
        name="adaln",
    )(cvecs, w, b)


def _inproj_kernel(segs, tsegs, n_w, has_rope, has_pre, x_ref, shift_ref, scale_ref, nw_ref, *rest):
    if has_pre:
        a_ref, b_ref, wo_ref, gate_ref = rest[:4]
        rest = rest[4:]
    w_refs, rest = rest[:n_w], rest[n_w:]
    if tsegs:
        wt_ref, rest = rest[0], rest[1:]
    if has_rope:
        cos_ref, sina_ref, sinb_ref = rest[:3]
        rest = rest[3:]
    if has_pre:
        x1_ref, rest = rest[0], rest[1:]
    outs = rest[:len(segs)]
    touts = rest[len(segs):len(segs) + len(tsegs)]
    rest = rest[len(segs) + len(tsegs):]
    slab_s = rest[0] if rest else None
    x = x_ref[...]
    if has_pre:
        ka = a_ref.shape[1]
        acc = jnp.dot(a_ref[...], wo_ref[:ka, :], preferred_element_type=f32)
        acc = acc + jnp.dot(b_ref[...], wo_ref[ka:, :], preferred_element_type=f32)
        x = x + gate_ref[...] * acc
        x1_ref[...] = x
    ms = jnp.mean(x * x, axis=-1, keepdims=True)
    h = (x * lax.rsqrt(ms + EPS)) * nw_ref[...]
    h = h * (1.0 + scale_ref[...]) + shift_ref[...]
    hb = h.astype(bf16)
    for (start, width, _), o_ref in zip(tsegs, touts):
        acc_t = lax.dot_general(wt_ref[start:start + width, :], hb, (((1,), (1,)), ((), ())),
                                preferred_element_type=f32)
        o_ref[...] = acc_t.astype(o_ref.dtype)
    products = {}
    for (widx, width, rope, _, chunked), o_ref in zip(segs, outs):
        if widx not in products:
            products[widx] = jnp.dot(hb, w_refs[widx][...], preferred_element_type=f32)
        acc = products[widx]
        if chunked:
            tm = acc.shape[0]
            hl = LANES // 2
            lo = lax.broadcasted_iota(jnp.int32, (1, LANES), 1) < hl
            for j in range(width // LANES):
                slab_s[j] = acc[:, j * LANES:(j + 1) * LANES]
            for j in range(width // LANES):
                rows = [slab_s[j, pl.ds(s, tm // S5_TC, stride=S5_TC), :] for s in range(S5_TC)]
                for p in range(S5_TC // 2):
                    a, b = rows[2 * p], rows[2 * p + 1]
                    h0 = jnp.where(lo, a, pltpu.roll(b, hl, axis=1))
                    h1 = jnp.where(lo, pltpu.roll(a, hl, axis=1), b)
                    o_ref[j, :, p * LANES:(p + 1) * LANES] = h0.astype(o_ref.dtype)
                    o_ref[j, :, S5_TC * hl + p * LANES:S5_TC * hl + (p + 1) * LANES] = h1.astype(o_ref.dtype)
            continue
        if rope is not None and not has_rope:
            acc = acc * rope
        elif rope is not None:
            rep = width // LANES
            cos = jnp.concatenate([cos_ref[...]] * rep, axis=1) * rope
            sina = jnp.concatenate([sina_ref[...]] * rep, axis=1) * rope
            sinb = jnp.concatenate([sinb_ref[...]] * rep, axis=1) * rope
            half = ATT_HEAD_DIM // 2
            up = pltpu.roll(acc, width - half, axis=1)
            dn = pltpu.roll(acc, half, axis=1)
            acc = acc * cos + up * sina + dn * sinb
        o_ref[...] = acc.astype(o_ref.dtype)


def _inproj(x, shift, scale, nw, w, segs, rows_per_mod, rope_tabs=None, wt=None, tsegs=(), pre=None):
    m, d = x.shape
    tm = min(ROW_TILE, rows_per_mod)
    per = rows_per_mod // tm
    nmod = shift.shape[0]
    mod_idx = (lambda i: (i // per, 0, 0)) if nmod > 1 else (lambda i: (0, 0, 0))
    in_specs = [pl.BlockSpec((tm, d), lambda i: (i, 0)),
                pl.BlockSpec((None, 1, d), mod_idx),
                pl.BlockSpec((None, 1, d), mod_idx),
                pl.BlockSpec((1, d), lambda i: (0, 0))]
    args = [x, shift, scale, nw]
    if pre is not None:
        a, b, w_out, gate = pre
        gate_idx = (lambda i: (i // per, 0, 0)) if gate.shape[0] > 1 else (lambda i: (0, 0, 0))
        in_specs += [pl.BlockSpec((tm, a.shape[1]), lambda i: (i, 0)),
                     pl.BlockSpec((tm, b.shape[1]), lambda i: (i, 0)),
                     pl.BlockSpec(w_out.shape, lambda i: (0, 0)),
                     pl.BlockSpec((None, 1, d), gate_idx)]
        args += [a, b, w_out, gate]
    in_specs += [pl.BlockSpec(wi.shape, lambda i: (0, 0)) for wi in w]
    args += list(w)
    if tsegs:
        in_specs.append(pl.BlockSpec(wt.shape, lambda i: (0, 0)))
        args.append(wt)
    if rope_tabs is not None:
        for t in rope_tabs:
            in_specs.append(pl.BlockSpec((tm, LANES), lambda i: (i % per, 0)))
            args.append(t)
    out_shape, out_specs = [], []
    if pre is not None:
        out_shape.append(jax.ShapeDtypeStruct((m, d), f32))
        out_specs.append(pl.BlockSpec((tm, d), lambda i: (i, 0)))
    for sg in segs:
        if sg[4]:
            out_shape.append(jax.ShapeDtypeStruct((sg[1] // LANES, m // S5_TC, S5_TC * LANES), sg[3]))
            out_specs.append(pl.BlockSpec((sg[1] // LANES, tm // S5_TC, S5_TC * LANES), lambda i: (0, i, 0)))
        else:
            out_shape.append(jax.ShapeDtypeStruct((m, sg[1]), sg[3]))
            out_specs.append(pl.BlockSpec((tm, sg[1]), lambda i: (i, 0)))
    for _, width, dtp in tsegs:
        out_shape.append(jax.ShapeDtypeStruct((width, m), dtp))
        out_specs.append(pl.BlockSpec((width, tm), lambda i: (0, i)))
    chunked_w = [sg[1] for sg in segs if sg[4]]
    scratch = [pltpu.VMEM((max(chunked_w) // LANES, tm, LANES), f32)] if chunked_w else []
    return pl.pallas_call(
        functools.partial(_inproj_kernel, tuple(segs), tuple(tsegs), len(w), rope_tabs is not None,
                          pre is not None),
        out_shape=out_shape,
        grid=(m // tm,),
        in_specs=in_specs,
        out_specs=out_specs,
        scratch_shapes=scratch,
        compiler_params=_cparams(("parallel",)),
        name="inproj",
    )(*args)


def _rope_tables(seq_len):
    rows = seq_len // GRID_W
    row = jnp.repeat(jnp.arange(rows, dtype=f32), GRID_W)
    col = jnp.tile(jnp.arange(GRID_W, dtype=f32), rows)
    n_freq = ATT_HEAD_DIM // 4
    inv = ROPE_THETA ** (-jnp.arange(n_freq, dtype=f32) / n_freq)
    ang = jnp.concatenate([row[:, None] * inv, col[:, None] * inv], axis=-1)
    cos, sin = jnp.cos(ang), jnp.sin(ang)
    zero = jnp.zeros_like(sin)
    cos_h = jnp.concatenate([cos, cos], axis=-1)
    sina_h = jnp.concatenate([-sin, zero], axis=-1)
    sinb_h = jnp.concatenate([zero, sin], axis=-1)
    two = lambda t: jnp.concatenate([t, t], axis=-1)
    return two(cos_h), two(sina_h), two(sinb_h)


SSD_PACK = 32


def _split3(x):
    hi = x.astype(bf16)
    r1 = x - hi.astype(f32)
    mid = r1.astype(bf16)
    lo = (r1 - mid.astype(f32)).astype(bf16)
    return hi, mid, lo


def _pack3(x):
    hi, mid, lo = _split3(x)
    lane = lax.broadcasted_iota(jnp.int32, x.shape, 1)
    mid_r = pltpu.roll(mid.astype(f32), SSD_PACK, axis=1)
    lo_r = pltpu.roll(lo.astype(f32), 2 * SSD_PACK, axis=1)
    packed = jnp.where(lane < SSD_PACK, hi.astype(f32),
                       jnp.where(lane < 2 * SSD_PACK, mid_r,
                                 jnp.where(lane < 3 * SSD_PACK, lo_r, 0.0)))
    return packed.astype(bf16)


def _ssd_selectors():
    k = np.arange(LANES)
    src = np.where(k < 3 * SSD_PACK, k % SSD_PACK, -1)
    col_blk = np.arange(SSD_PACK * SSD_CHUNK) // SSD_CHUNK
    sel_bc = (src[:, None] == col_blk[None, :])
    head = np.arange(SSD_INNER) // SSD_HEAD_DIM
    sel_f = (src[:, None] == head[None, :])
    sel_b = (src[:, None] == (head + SSD_HEADS)[None, :])
    tri3 = np.tile(np.tril(np.ones((SSD_CHUNK, SSD_CHUNK))), (1, 3))
    rows = np.arange(SSD_CHUNK)[:, None]
    cols = np.arange(SSD_CHUNK + 32)[None, :]
    half = SSD_CONV // 2
    shift = np.concatenate([cols == rows + 16 + d for d in range(-half, half + 1) if d != 0], axis=0)
    as_bf = lambda a: jnp.asarray(a.astype(np.float32), dtype=bf16)
    return as_bf(sel_bc), as_bf(sel_f), as_bf(sel_b), as_bf(tri3), as_bf(shift)


def _ssd_kernel(seq_len, xbc_ref, dt_ref, z_ref, cw_ref, cb_ref, dtb_ref, alog_ref, dskip_ref, nw_ref,
                selbc_ref, self_ref, selb_ref, tri3_ref, shift_ref, h0_ref, out_ref, hfin_ref,
                xs_s, bc_s, dt_s, y_s, hf_s, hb_s, win_s):
    q = SSD_CHUNK
    nc = seq_len // q
    halo = 16
    H, P, N = SSD_HEADS, SSD_HEAD_DIM, SSD_STATE
    gw = (H // SSD_GROUPS) * P
    a2_row = -jnp.exp(alog_ref[...]) * math.log2(math.e)

    def conv_chunk(c):
        r0 = pl.multiple_of(c * q, q)
        pstart = pl.multiple_of(jnp.maximum(r0 - halo, 0), halo)
        nstart = pl.multiple_of(jnp.minimum(r0 + q, seq_len - halo), halo)
        zero = jnp.zeros((), bf16)
        win_s[0:halo, :] = jnp.where(c > 0, xbc_ref[pl.ds(pstart, halo), :], zero)
        win_s[halo:halo + q, :] = xbc_ref[pl.ds(r0, q), :]
        win_s[halo + q:, :] = jnp.where(c < nc - 1, xbc_ref[pl.ds(nstart, halo), :], zero)
        taps = [k for k in range(SSD_CONV) if k != SSD_CONV // 2]
        cw = 2 * LANES
        for j in range(SSD_XBC // cw):
            cs = slice(j * cw, (j + 1) * cw)
            sh = jnp.dot(shift_ref[...], win_s[:, cs], preferred_element_type=f32)
            acc = cb_ref[:, cs] + win_s[halo:halo + q, cs].astype(f32) * cw_ref[SSD_CONV // 2:SSD_CONV // 2 + 1, cs]
            for n, k in enumerate(taps):
                acc = acc + sh[n * q:(n + 1) * q, :] * cw_ref[k:k + 1, cs]
            act = _silu(acc)
            if j < SSD_INNER // cw:
                xs_s[pl.ds(r0, q), cs] = act
            else:
                bc_s[pl.ds(r0, q), j * cw - SSD_INNER:(j + 1) * cw - SSD_INNER] = act.astype(bf16)
        dt_s[pl.ds(r0, q), :] = jax.nn.softplus(dt_ref[pl.ds(r0, q), :] + dtb_ref[...])

    conv_chunk(0)
    hf_s[...] = h0_ref[0]
    hb_s[...] = h0_ref[1]

    ri = lax.broadcasted_iota(jnp.int32, (q, q), 0)
    ci = lax.broadcasted_iota(jnp.int32, (q, q), 1)
    lower = ri >= ci
    upper = ci >= ri
    lane = lax.broadcasted_iota(jnp.int32, (q, LANES), 1)
    lo_half = lane < P

    def cumsums(dt):
        dta = dt * a2_row
        cf = jnp.dot(tri3_ref[...], jnp.concatenate(_split3(dta), axis=0), preferred_element_type=f32)
        rb = cf[q - 1:q, :] - cf + dta
        return cf, rb

    def load_chunk(r0):
        dt = dt_s[pl.ds(r0, q), :]
        xs = xs_s[pl.ds(r0, q), :]
        bcv = bc_s[pl.ds(r0, q), :]
        bmat = [bcv[:, g * N:(g + 1) * N] for g in range(SSD_GROUPS)]
        cmat = [bcv[:, SSD_BC + g * N:SSD_BC + (g + 1) * N] for g in range(SSD_GROUPS)]
        return dt, xs, bmat, cmat

    def inter_chunk(h_s, sel_ref, decay, weight, xs, bmat, cmat, dec_idx):
        ew = jnp.dot(jnp.concatenate([_pack3(decay), _pack3(weight)], axis=0), sel_ref[...],
                     preferred_element_type=f32)
        e_x, w_x = ew[:q], ew[q:]
        hb_ = h_s[...].astype(bf16)
        yoff = jnp.concatenate(
            [jnp.dot(cmat[g], hb_[:, g * gw:(g + 1) * gw], preferred_element_type=f32)
             for g in range(SSD_GROUPS)], axis=1)
        xw = (xs * w_x).astype(bf16)
        dec_row = e_x[dec_idx:dec_idx + 1, :]
        for g in range(SSD_GROUPS):
            gs = slice(g * gw, (g + 1) * gw)
            bt = jnp.transpose(bmat[g].astype(f32)).astype(bf16)
            upd = jnp.dot(bt, xw[:, gs], preferred_element_type=f32)
            h_s[:, gs] = h_s[:, gs] * dec_row[:, gs] + upd
        return yoff * e_x

    def finish(r0, y, xs):
        yy = y + xs * dskip_ref[...]
        zz = z_ref[pl.ds(r0, q), :].astype(f32)
        gated = yy * _silu(zz)
        ms = jnp.mean(gated * gated, axis=-1, keepdims=True)
        out_ref[pl.ds(r0, q), :] = (gated * lax.rsqrt(ms + EPS) * nw_ref[...]).astype(out_ref.dtype)

    def fwd_chunk(c, second_half):
        r0 = pl.multiple_of(c * q, q)
        dt, xs, bmat, cmat = load_chunk(r0)
        cf, rb = cumsums(dt)
        pcol = jnp.where(lane < H, cf, rb)
        bcast = jnp.dot(_pack3(pcol), selbc_ref[...], preferred_element_type=f32)
        prow = jnp.transpose(pcol - jnp.log2(dt))
        cbm = [lax.dot_general(cmat[g], bmat[g], (((1,), (1,)), ((), ())), preferred_element_type=f32)
               for g in range(SSD_GROUPS)]
        xsb = xs.astype(bf16)
        zero_b = jnp.zeros((), bf16)
        ypairs = []
        for k in range(H // 2):
            mats = []
            for h in (2 * k, 2 * k + 1):
                g = h // (H // SSD_GROUPS)
                hb_ = H + h
                segf = bcast[:, h * q:(h + 1) * q] - prow[h:h + 1, :]
                segb = bcast[:, hb_ * q:(hb_ + 1) * q] - prow[hb_:hb_ + 1, :]
                df = jnp.exp2(jnp.where(lower, segf, NEG_INF))
                db = jnp.exp2(jnp.where(upper, segb, NEG_INF))
                mats.append((cbm[g] * (df + db)).astype(bf16))
            xp = xsb[:, k * LANES:(k + 1) * LANES]
            xbd = jnp.concatenate([jnp.where(lo_half, xp, zero_b), jnp.where(lo_half, zero_b, xp)], axis=0)
            ypairs.append(jnp.dot(jnp.concatenate(mats, axis=1), xbd, preferred_element_type=f32))
        y = jnp.concatenate(ypairs, axis=1)
        wfa = jnp.exp2(cf[q - 1:q, :] - cf) * dt
        y = y + inter_chunk(hf_s, self_ref, jnp.exp2(cf), wfa, xs, bmat, cmat, q - 1)
        if second_half:
            finish(r0, y_s[pl.ds(r0, q), :] + y, xs)
        else:
            y_s[pl.ds(r0, q), :] = y

    def bwd_chunk(c, second_half):
        r0 = pl.multiple_of(c * q, q)
        dt, xs, bmat, cmat = load_chunk(r0)
        _, rb = cumsums(dt)
        wba = jnp.exp2(rb[0:1, :] - rb) * dt
        y = inter_chunk(hb_s, selb_ref, jnp.exp2(rb), wba, xs, bmat, cmat, 0)
        if second_half:
            finish(r0, y_s[pl.ds(r0, q), :] + y, xs)
        else:
            y_s[pl.ds(r0, q), :] = y

    half = nc // 2
    conv_chunk(nc - 1)

    def first_half(i, carry):
        fwd_chunk(i, False)
        bwd_chunk(nc - 1 - i, False)
        conv_chunk(i + 1)
        conv_chunk(nc - 2 - i)
        return carry

    def second_half(i, carry):
        fwd_chunk(i, True)
        bwd_chunk(nc - 1 - i, True)
        return carry

    lax.fori_loop(0, half - 1, first_half, 0)
    fwd_chunk(jnp.int32(half - 1), False)
    bwd_chunk(jnp.int32(half), False)
    lax.fori_loop(half, nc, second_half, 0)
    hfin_ref[0] = hf_s[...]
    hfin_ref[1] = hb_s[...]


def _ssd(xbc, dt, z, conv_w, conv_b, dtb, alog, dskip, nw, h0, bsz, seq_len):
    assert seq_len % (2 * SSD_CHUNK) == 0, "the two recurrences meet in the middle: even chunk count"
    one = pl.Buffered(1)
    seq = lambda w: pl.BlockSpec((seq_len, w), lambda b: (b, 0))
    const = lambda r, w: pl.BlockSpec((r, w), lambda b: (0, 0), pipeline_mode=one)
    st = pl.BlockSpec((None, 2, SSD_STATE, SSD_INNER), lambda b: (b, 0, 0, 0), pipeline_mode=one)
    sels = _ssd_selectors()
    return pl.pallas_call(
        functools.partial(_ssd_kernel, seq_len),
        out_shape=[jax.ShapeDtypeStruct((bsz * seq_len, SSD_INNER), bf16),
                   jax.ShapeDtypeStruct((bsz, 2, SSD_STATE, SSD_INNER), f32)],
        grid=(bsz,),
        in_specs=[seq(SSD_XBC), seq(LANES), seq(SSD_INNER),
                  const(8, SSD_XBC), const(1, SSD_XBC), const(1, LANES), const(1, LANES),
                  const(1, SSD_INNER), const(1, SSD_INNER)]
                 + [const(*s.shape) for s in sels] + [st],
        out_specs=[seq(SSD_INNER), st],
        scratch_shapes=[pltpu.VMEM((seq_len, SSD_INNER), f32),
                        pltpu.VMEM((seq_len, 2 * SSD_BC), bf16),
                        pltpu.VMEM((seq_len, LANES), f32),
                        pltpu.VMEM((seq_len, SSD_INNER), f32),
                        pltpu.VMEM((SSD_STATE, SSD_INNER), f32),
                        pltpu.VMEM((SSD_STATE, SSD_INNER), f32),
                        pltpu.VMEM((SSD_CHUNK + 32, SSD_XBC), bf16)],
        compiler_params=_cparams(("parallel",)),
        name="ssd",
    )(xbc, dt, z, conv_w, conv_b, dtb, alog, dskip, nw, *sels, h0)


def _attn_kernel(n_blocks, nsub, local, q_ref, g_ref, k_ref, kc_ref, *rest):
    nv = nsub + 2 if local else 0
    v_refs = rest[:nv]
    vc_ref, sink_ref, o_ref, s_s = rest[nv:]
    t = ATT_BLOCK
    i0 = pl.program_id(1) * nsub
    rpk = ATT_HEADS // ATT_KV_HEADS
    lane = lax.broadcasted_iota(jnp.int32, (t, LANES), 1)
    lo_half = lane < ATT_HEAD_DIM
    zero_b = jnp.zeros((), bf16)
    kl = lax.broadcasted_iota(jnp.int32, (t, t), 0)
    ql = lax.broadcasted_iota(jnp.int32, (t, t), 1)
    for sub in range(nsub):
        i = i0 + sub
        qv = q_ref[sub * t:(sub + 1) * t, :]
        if local:
            p0 = pl.multiple_of(jnp.maximum(i - 1, 0) * t, t)
            c0 = pl.multiple_of(i * t, t)
            n0 = pl.multiple_of(jnp.minimum(i + 1, n_blocks - 1) * t, t)
            bias_prev = jnp.where((kl >= ql) & (i > 0), 0.0, NEG_INF)
            bias_next = jnp.where((kl <= ql) & (i < n_blocks - 1), 0.0, NEG_INF)
            bias_prev = jnp.concatenate([bias_prev] * rpk, axis=1)
            bias_next = jnp.concatenate([bias_next] * rpk, axis=1)
        for j in range(ATT_KV_HEADS):
            ls = slice(j * LANES, (j + 1) * LANES)
            if local:
                kk = jnp.concatenate([k_ref[pl.ds(p0, t), ls], k_ref[pl.ds(c0, t), ls],
                                      k_ref[pl.ds(n0, t), ls], kc_ref[:, ls]], axis=0)
            else:
                kk = kc_ref[:, ls]
            pieces = []
            for r in range(rpk):
                hq = j * rpk + r
                qp = qv[:, (hq // 2) * LANES:(hq // 2 + 1) * LANES]
                keep = lo_half if hq % 2 == 0 else jnp.logical_not(lo_half)
                pieces.append(jnp.where(keep, qp, zero_b))
            q4 = jnp.concatenate(pieces, axis=0)
            s = lax.dot_general(kk, q4, (((1,), (1,)), ((), ())), preferred_element_type=f32)
            if local:
                s = jnp.concatenate([s[:t] + bias_prev, s[t:2 * t], s[2 * t:3 * t] + bias_next, s[3 * t:]],
                                    axis=0)
            s_s[sub, j] = s
    for sub in range(nsub):
        outs = []
        for j in range(ATT_KV_HEADS):
            vs = slice(j * ATT_HEAD_DIM, (j + 1) * ATT_HEAD_DIM)
            if local:
                vvt = jnp.concatenate([v_refs[sub + k][vs, :] for k in range(3)] + [vc_ref[vs, :]], axis=1)
            else:
                vvt = vc_ref[vs, :]
            sk = jnp.concatenate([jnp.full((1, t), sink_ref[j * rpk + r] * LOG2E, f32) for r in range(rpk)],
                                 axis=1)
            s = s_s[sub, j]
            m = jnp.maximum(jnp.max(s, axis=0, keepdims=True), sk)
            p = jnp.exp2(s - m)
            den = jnp.sum(p, axis=0, keepdims=True) + jnp.exp2(sk - m)
            vvt = jnp.concatenate([vvt, vvt], axis=0)
            ot = jnp.dot(vvt, p.astype(bf16), preferred_element_type=f32) / den
            o4 = [jnp.transpose(ot[:, r * t:(r + 1) * t]) for r in range(rpk)]
            outs.append(jnp.where(lo_half, o4[0], o4[1]))
            outs.append(jnp.where(lo_half, o4[2], o4[3]))
        o = jnp.concatenate(outs, axis=1)
        gv = g_ref[sub * t:(sub + 1) * t, :].astype(f32)
        o_ref[sub * t:(sub + 1) * t, :] = (o * _silu(gv)).astype(o_ref.dtype)


def _attention(q, g, k, vt, kc, vct, sink, bsz, seq_len, local):
    t = ATT_BLOCK
    nb = seq_len // t
    nsub = max(s for s in range(1, ATT_SUB + 1) if nb % s == 0)
    n_ctx = kc.shape[0] // bsz
    kw = ATT_KV_HEADS * LANES
    vw = ATT_KVW
    blk = pl.BlockSpec((nsub * t, ATT_Q), lambda b, i: (b * (nb // nsub) + i, 0))
    full = lambda n: pl.BlockSpec((n, kw), lambda b, i: (b, 0))
    vblk = lambda off: pl.BlockSpec((vw, t), lambda b, i: (0, b * nb + jnp.clip(i * nsub + off, 0, nb - 1)))
    vspecs = [vblk(off) for off in range(-1, nsub + 1)] if local else []
    return pl.pallas_call(
        functools.partial(_attn_kernel, nb, nsub, local),
        out_shape=jax.ShapeDtypeStruct((bsz * seq_len, ATT_Q), bf16),
        grid=(bsz, nb // nsub),
        in_specs=[blk, blk, full(k.shape[0] // bsz), full(n_ctx)] + vspecs
                 + [pl.BlockSpec((vw, n_ctx), lambda b, i: (0, b)), pl.BlockSpec(memory_space=pltpu.SMEM)],
        out_specs=blk,
        scratch_shapes=[pltpu.VMEM((nsub, ATT_KV_HEADS, (3 * t if local else 0) + n_ctx,
                                    (ATT_HEADS // ATT_KV_HEADS) * t), f32)],
        compiler_params=_cparams(("parallel", "arbitrary")),
        name="attention",
    )(q, g, k, kc, *([vt] * len(vspecs)), vct, sink)


def _s5_disc_kernel(lre_ref, lim_ref, ls_ref, bre_ref, bim_ref, abre_ref, abim_ref, bbre_ref, bbim_ref):
    lam_re = lre_ref[...]
    lam_im = lim_ref[...]
    dt = jnp.exp(ls_ref[...])
    mag = jnp.exp(lam_re * dt)
    ab_re = mag * jnp.cos(lam_im * dt)
    ab_im = mag * jnp.sin(lam_im * dt)
    num_re, num_im = ab_re - 1.0, ab_im
    den = lam_re * lam_re + lam_im * lam_im
    coef_re = (num_re * lam_re + num_im * lam_im) / den
    coef_im = (num_im * lam_re - num_re * lam_im) / den
    b_re, b_im = bre_ref[...], bim_ref[...]
    abre_ref[...] = ab_re
    abim_ref[...] = ab_im
    bbre_ref[...] = coef_re * b_re - coef_im * b_im
    bbim_ref[...] = coef_re * b_im + coef_im * b_re


def _s5_discretise(lam_re, lam_im, log_step, b_re, b_im):
    g, n, cg = b_re.shape
    exp = lambda t: jnp.repeat(t.reshape(2 * g, n), cg, axis=1)
    ls = jnp.broadcast_to(log_step.reshape(2 * g, 1), (2 * g, n * cg))
    bb = lambda t: jnp.tile(t.reshape(g, n * cg), (2, 1))
    shp = jax.ShapeDtypeStruct((2 * g, n * cg), f32)
    ab_re, ab_im, bb_re, bb_im = pl.pallas_call(
        _s5_disc_kernel, out_shape=[shp] * 4, name="s5_disc",
    )(exp(lam_re), exp(lam_im), ls, bb(b_re), bb(b_im))
    first = lambda t: t.reshape(2, g, n, cg)[..., 0]
    full = lambda t: t.reshape(2, g, n, cg)
    return first(ab_re), first(ab_im), full(bb_re), full(bb_im)


def _cmul(ar, ai, br, bi):
    return ar * br - ai * bi, ar * bi + ai * br


def _s5_kernel(n_lat, n_ctx, *refs):
    tc = S5_TC
    xl, xc, arow_ref, acol_ref, bbd_ref, cbd_ref, y_ref, wyz_s, ws_s, sl_s, sc_s = refs
    hw = S5_HGRP * S5_STATE
    hl = LANES // 2
    xw = tc * hl
    nsl = hw // LANES
    nb = 8
    lo = lax.broadcasted_iota(jnp.int32, (1, LANES), 1) < hl

    def powers(re, im, n):
        out = [(jnp.ones_like(re), jnp.zeros_like(im))]
        for _ in range(n):
            out.append(_cmul(out[-1][0], out[-1][1], re, im))
        return out

    steps = lambda h, p: (2 * p, 2 * p + 1) if h == 0 else (2 * p + 1, 2 * p)

    prow = []
    for h in range(2):
        pr = [powers(arow_ref[h, 2 * d:2 * d + 1, :], arow_ref[h, 2 * d + 1:2 * d + 2, :], tc) for d in range(2)]
        prow.append(pr)
        for s in range(tc):
            for d, k in ((0, tc - 1 - s), (1, s)):
                wr, wi = _cmul(bbd_ref[2 * d, h], bbd_ref[2 * d + 1, h], *pr[d][k])
                ws_s[h, s * hl:(s + 1) * hl, d * 2 * hw:d * 2 * hw + hw] = wr.astype(bf16)
                ws_s[h, s * hl:(s + 1) * hl, d * 2 * hw + hw:(d + 1) * 2 * hw] = wi.astype(bf16)
        crhs = jnp.concatenate([cbd_ref[0, h], -cbd_ref[1, h]], axis=0).astype(bf16)
        kall = [jnp.dot(ws_s[h, :, d * 2 * hw:(d + 1) * 2 * hw], crhs, preferred_element_type=f32)
                for d in range(2)]
        kf = [kall[0][(tc - 1 - k) * hl:(tc - k) * hl] for k in range(tc)]
        kb = [kall[1][k * hl:(k + 1) * hl] for k in range(tc)]
        lag = lambda s, t: kf[t - s] if t > s else (kb[s - t] if t < s else kf[0] + kb[0])
        for s in range(tc):
            for p in range(tc // 2):
                t0, t1 = steps(h, p)
                wyz_s[h, s * hl:(s + 1) * hl, p * LANES:(p + 1) * LANES] = (
                    jnp.where(lo, lag(s, t0), lag(s, t1)).astype(bf16))
        for d in range(2):
            pc = powers(acol_ref[h, 2 * d], acol_ref[h, 2 * d + 1], tc)
            kk = (lambda t: t + 1) if d == 0 else (lambda t: tc - t)
            r0 = xw + d * 2 * hw
            for p in range(tc // 2):
                t0, t1 = steps(h, p)
                ar = jnp.where(lo, pc[kk(t0)][0], pc[kk(t1)][0])
                ai = jnp.where(lo, pc[kk(t0)][1], pc[kk(t1)][1])
                dre, dim_ = _cmul(cbd_ref[0, h], cbd_ref[1, h], ar, ai)
                wyz_s[h, r0:r0 + hw, p * LANES:(p + 1) * LANES] = dre.astype(bf16)
                wyz_s[h, r0 + hw:r0 + 2 * hw, p * LANES:(p + 1) * LANES] = (-dim_).astype(bf16)

    def rows_of(x_ref, b, n, h):
        return x_ref[b * n:(b + 1) * n, h * xw:(h + 1) * xw]

    def inject(x_ref, s_ref, n):
        for b in range(nb):
            for h in range(2):
                sb = jnp.dot(rows_of(x_ref, b, n, h), ws_s[h], preferred_element_type=f32)
                for k in range(4 * nsl):
                    s_ref[h * 4 * nsl + k, pl.ds(b, n, stride=nb), :] = sb[:, k * LANES:(k + 1) * LANES]

    at = [[[tuple(jnp.broadcast_to(p[:, k * LANES:(k + 1) * LANES], (nb, LANES)) for p in prow[h][d][tc])
            for k in range(nsl)] for d in range(2)] for h in range(2)]

    def scan(s_ref, n, init):
        def step(i, carry):
            idxs = [pl.ds(pl.multiple_of(i * nb, nb), nb), pl.ds(pl.multiple_of((n - 1 - i) * nb, nb), nb)]
            chains = [(h, d, k) for h in range(2) for d in range(2) for k in range(nsl)]
            slab = lambda h, d, k: h * 4 * nsl + d * 2 * nsl + k
            inj = [(s_ref[slab(h, d, k), idxs[d], :], s_ref[slab(h, d, k) + nsl, idxs[d], :])
                   for h, d, k in chains]
            new = []
            for c, (h, d, k) in enumerate(chains):
                hr, hi = carry[2 * c], carry[2 * c + 1]
                s_ref[slab(h, d, k), idxs[d], :] = hr
                s_ref[slab(h, d, k) + nsl, idxs[d], :] = hi
                ar, ai = at[h][d][k]
                new += [ar * hr - ai * hi + inj[c][0], ar * hi + ai * hr + inj[c][1]]
            return tuple(new)
        return lax.fori_loop(0, n, step, init, unroll=4)

    inject(xc, sc_s, n_ctx)
    h_ctx = scan(sc_s, n_ctx, tuple(jnp.zeros((nb, LANES), f32) for _ in range(8 * nsl)))
    inject(xl, sl_s, n_lat)
    scan(sl_s, n_lat, h_ctx)

    for b in range(nb):
        yh = []
        for h in range(2):
            hin = jnp.concatenate([sl_s[h * 4 * nsl + k, pl.ds(b, n_lat, stride=nb), :]
                                   for k in range(4 * nsl)], axis=1).astype(bf16)
            yh.append(jnp.dot(jnp.concatenate([rows_of(xl, b, n_lat, h), hin], axis=1), wyz_s[h],
                              preferred_element_type=f32))
        for p in range(tc // 2):
            y0, y1 = yh[0][:, p * LANES:(p + 1) * LANES], yh[1][:, p * LANES:(p + 1) * LANES]
            even = jnp.where(lo, y0, y1)
            odd = pltpu.roll(jnp.where(lo, y1, y0), hl, axis=1)
            y_ref[pl.ds(b * n_lat * tc + 2 * p, n_lat, stride=tc), :] = even
            y_ref[pl.ds(b * n_lat * tc + 2 * p + 1, n_lat, stride=tc), :] = odd


def _s5_mix(u, u_c, arow, acol, bbd, cbd, bsz, seq_len, n_ctx_tok):
    assert bsz == 8, "the chunk recurrence puts the batch on the 8 sublanes"
    tc = S5_TC
    n_lat, n_ctx = seq_len // tc, n_ctx_tok // tc
    nblk = S5_GROUPS // S5_GBLK
    hw = S5_HGRP * S5_STATE
    hl = LANES // 2
    nsl = hw // LANES
    one = pl.Buffered(1)
    xspec = lambda rows: pl.BlockSpec((None, rows, tc * LANES), lambda g: (g, 0, 0))
    return pl.pallas_call(
        functools.partial(_s5_kernel, n_lat, n_ctx),
        out_shape=jax.ShapeDtypeStruct((nblk, bsz * seq_len, LANES), f32),
        grid=(nblk,),
        in_specs=[xspec(bsz * n_lat), xspec(bsz * n_ctx),
                  pl.BlockSpec((2, 4, hw), lambda g: (g, 0, 0), pipeline_mode=one),
                  pl.BlockSpec((2, 4, hw, LANES), lambda g: (g, 0, 0, 0), pipeline_mode=one),
                  pl.BlockSpec((4, 2, hl, hw), lambda g: (0, g, 0, 0), pipeline_mode=one),
                  pl.BlockSpec((2, 2, hw, LANES), lambda g: (0, g, 0, 0), pipeline_mode=one)],
        out_specs=pl.BlockSpec((None, bsz * seq_len, LANES), lambda g: (g, 0, 0)),
        scratch_shapes=[pltpu.VMEM((2, tc * hl + 4 * hw, tc * hl), bf16),
                        pltpu.VMEM((2, tc * hl, 4 * hw), bf16),
                        pltpu.VMEM((8 * nsl, bsz * n_lat, LANES), f32),
                        pltpu.VMEM((8 * nsl, bsz * n_ctx, LANES), f32)],
        compiler_params=_cparams(("arbitrary",)),
        name="s5_mix",
    )(u, u_c, arow, acol, bbd, cbd)


def _s5_block_params(ab_re, ab_im, bb_re, bb_im, c_re, c_im):
    g, n, cg = S5_GROUPS, S5_STATE, S5_GROUP_CH
    nh = g // S5_HGRP
    hw = S5_HGRP * n
    arow = jnp.stack([t[d].reshape(nh, hw) for d in range(2) for t in (ab_re, ab_im)], axis=1)
    acol = jnp.broadcast_to(arow[..., None], (nh, 4, hw, LANES))

    def blockdiag(t, rows_per, cols_per):
        tiled = jnp.concatenate([t] * S5_HGRP, axis=-1)
        r = np.arange(t.shape[-2])[:, None] // rows_per
        c = np.arange(S5_HGRP * cols_per)[None, :] // cols_per
        return jnp.where(jnp.asarray(r == c), tiled, 0.0)

    bb = jnp.stack([t[d] for d in range(2) for t in (bb_re, bb_im)], axis=0)
    bb = jnp.swapaxes(bb.reshape(4, nh, S5_HGRP, n, cg), -1, -2).reshape(4, nh, S5_HGRP * cg, n)
    bbd = blockdiag(bb, cg, n)
    cc = jnp.swapaxes(jnp.stack([c_re, c_im], axis=0).reshape(2, nh, S5_HGRP, cg, n), -1, -2)
    cbd = blockdiag(cc.reshape(2, nh, hw, cg), n, cg)
    return arow, acol, bbd, jnp.concatenate([cbd, cbd], axis=-1)


def _s5_out_kernel(y_ref, u_ref, g_ref, x_ref, dskip_ref, gw_ref, gb_ref, w_ref, gate_ref,
                   fw_ref, o_ref):
    y = jnp.concatenate([y_ref[j] for j in range(y_ref.shape[0])], axis=1)
    y = y + dskip_ref[...] * u_ref[...].astype(f32)
    y = jax.nn.gelu(y)
    glu = jnp.dot(y.astype(bf16), gw_ref[...], preferred_element_type=f32) + gb_ref[...]
    y = y * jax.nn.sigmoid(glu)
    y = y * _silu(g_ref[...].astype(f32))
    x = x_ref[...] + gate_ref[...] * jnp.dot(y.astype(bf16), w_ref[...], preferred_element_type=f32)
    ms = jnp.mean(x * x, axis=-1, keepdims=True)
    o_ref[...] = x * lax.rsqrt(ms + EPS) * fw_ref[...]


def _s5_out(y, u, g, x, dskip, gw, gb, w, gate, fw, rows_per_mod):
    m, d = x.shape
    tm = min(ROW_TILE, rows_per_mod)
    per = rows_per_mod // tm
    row = lambda: pl.BlockSpec((tm, d), lambda i: (i, 0))
    vec = lambda: pl.BlockSpec((1, d), lambda i: (0, 0))
    mat = lambda: pl.BlockSpec((d, d), lambda i: (0, 0))
    return pl.pallas_call(
        _s5_out_kernel,
        out_shape=jax.ShapeDtypeStruct((m, d), f32),
        grid=(m // tm,),
        in_specs=[pl.BlockSpec((y.shape[0], tm, LANES), lambda i: (0, i, 0)),
                  row(), row(), row(), vec(), mat(), vec(), mat(),
                  pl.BlockSpec((None, 1, d), lambda i: (i // per, 0, 0)), vec()],
        out_specs=row(),
        compiler_params=_cparams(("parallel",)),
        name="s5_out",
    )(y, u, g, x, dskip, gw, gb, w, gate, fw)


def _even_weights(w_in):
    o = 0
    z = w_in[:, o:o + SSD_INNER]; o += SSD_INNER
    xbc = w_in[:, o:o + SSD_XBC]; o += SSD_XBC
    dt = w_in[:, o:o + 2 * SSD_HEADS]; o += 2 * SSD_HEADS
    q = w_in[:, o:o + ATT_Q]; o += ATT_Q
    k = w_in[:, o:o + ATT_KVW]; o += ATT_KVW
    v = w_in[:, o:o + ATT_KVW]; o += ATT_KVW
    g = w_in[:, o:o + ATT_Q]
    d = w_in.shape[0]
    dup = lambda t: jnp.concatenate([t.reshape(d, ATT_KV_HEADS, 1, ATT_HEAD_DIM)] * 2, axis=2).reshape(d, -1)
    dtp = jnp.pad(dt, ((0, 0), (0, LANES - 2 * SSD_HEADS)))
    cast = lambda t: t.astype(bf16)
    return [cast(t) for t in (z, xbc, q, dup(k), g, dtp)], cast(v.T)


def _even_segs(rope):
    scale = ATT_HEAD_DIM ** -0.5 * LOG2E
    widths = [(SSD_INNER, None, bf16), (SSD_XBC, None, bf16),
              (ATT_Q, scale, bf16),
              (ATT_KV_HEADS * LANES, 1.0 if rope else None, bf16),
              (ATT_Q, None, bf16), (LANES, None, f32)]
    return [(i, w, r, dtp, False) for i, (w, r, dtp) in enumerate(widths)]


def _pad_lanes(v, n=LANES):
    v = v.reshape(1, -1)
    return jnp.pad(v, ((0, 0), (0, n - v.shape[1])))


def kernel(x, c, ctx, c_ctx, e_norm_w, e_ada_w, e_ada_b, e_w_in, e_conv_w, e_conv_b, e_dt_bias,
           e_a_log, e_d_skip, e_ssd_norm_w, e_sink, e_w_out, o_norm_w, o_ada_w, o_ada_b, o_w_in,
           o_lam_re, o_lam_im, o_log_step, o_b_re, o_b_im, o_c_re, o_c_im, o_d_skip, o_glu_w,
           o_glu_b, o_w_out, final_norm_w):
    bsz, seq_len, d = x.shape
    n_ctx = ctx.shape[1]
    xf = x.reshape(bsz * seq_len, d)
    xcf = ctx.reshape(bsz * n_ctx, d)

    cvecs = jnp.concatenate([c, c_ctx[None, :], jnp.zeros((16 - bsz - 1, d), f32)], axis=0)

    def modulation(ada_w, ada_b):
        mod = _adaln(cvecs, ada_w.astype(bf16), ada_b.reshape(1, -1))
        parts = [mod[:, k * d:(k + 1) * d] for k in range(3)]
        lat = [p[:bsz].reshape(bsz, 1, d) for p in parts]
        cx = [p[bsz:bsz + 1].reshape(1, 1, d) for p in parts]
        return lat, cx

    (shift, scale, gate), (shift_c, scale_c, gate_c) = modulation(e_ada_w[0], e_ada_b[0])
    w_in, w_vt = _even_weights(e_w_in[0])
    nw = e_norm_w[0].reshape(1, d)
    tabs = _rope_tables(seq_len)
    vseg = [(0, ATT_KVW, bf16)]
    z, xbc, q, k, g, dt, vt = _inproj(xf, shift, scale, nw, w_in, _even_segs(True), seq_len, tabs, w_vt, vseg)
    z_c, xbc_c, q_c, k_c, g_c, dt_c, vt_c = _inproj(xcf, shift_c, scale_c, nw, w_in, _even_segs(False), n_ctx,
                                                    None, w_vt, vseg)

    conv_w = jnp.pad(e_conv_w[0], ((0, 8 - SSD_CONV), (0, 0)))
    conv_b = e_conv_b[0].reshape(1, -1)
    dtb = _pad_lanes(e_dt_bias[0])
    alog = _pad_lanes(e_a_log[0])
    dskip = jnp.repeat(e_d_skip[0], SSD_HEAD_DIM).reshape(1, -1)
    snw = e_ssd_norm_w[0].reshape(1, -1)
    h0 = jnp.zeros((bsz, 2, SSD_STATE, SSD_INNER), f32)
    ssd_c, hfin = _ssd(xbc_c, dt_c, z_c, conv_w, conv_b, dtb, alog, dskip, snw, h0, bsz, n_ctx)
    ssd_o, _ = _ssd(xbc, dt, z, conv_w, conv_b, dtb, alog, dskip, snw, hfin, bsz, seq_len)

    sink = e_sink[0]
    att = _attention(q, g, k, vt, k_c, vt_c, sink, bsz, seq_len, True)
    att_c = _attention(q_c, g_c, k_c, vt_c, k_c, vt_c, sink, bsz, n_ctx, False)
    w_out = e_w_out[0].astype(bf16)
    even_gate, even_gate_c = gate, gate_c

    (shift, scale, gate), (shift_c, scale_c, _) = modulation(o_ada_w[0], o_ada_b[0])
    w_u = o_w_in[0][:, :S5_WIDTH].astype(bf16)
    w_g = o_w_in[0][:, S5_WIDTH:].astype(bf16)
    nw = o_norm_w[0].reshape(1, d)
    x1, u, u_ch, g2 = _inproj(xf, shift, scale, nw, [w_u, w_g],
                              [(0, S5_WIDTH, None, bf16, False), (0, S5_WIDTH, None, bf16, True),
                               (1, S5_WIDTH, None, bf16, False)], seq_len,
                              pre=(ssd_o, att, w_out, even_gate))
    _, uc_ch = _inproj(xcf, shift_c, scale_c, nw, [w_u], [(0, S5_WIDTH, None, bf16, True)], n_ctx,
                       pre=(ssd_c, att_c, w_out, even_gate_c))

    ab_re, ab_im, bb_re, bb_im = _s5_discretise(o_lam_re[0], o_lam_im[0], o_log_step[0], o_b_re[0], o_b_im[0])
    arow, acol, bbd, cbd = _s5_block_params(ab_re, ab_im, bb_re, bb_im, o_c_re[0], o_c_im[0])
    y = _s5_mix(u_ch, uc_ch, arow, acol, bbd, cbd, bsz, seq_len, n_ctx)
    out = _s5_out(y, u, g2, x1, o_d_skip[0].reshape(1, -1),
                  o_glu_w[0].astype(bf16), o_glu_b[0].reshape(1, -1), o_w_out[0].astype(bf16), gate,
                  final_norm_w.reshape(1, -1), seq_len)
    return out.reshape(bsz, seq_len, d)
```

```python
import functools
import math

import jax
import jax.numpy as jnp
import numpy as np
from jax import lax
from jax.experimental import pallas as pl
from jax.experimental.pallas import tpu as pltpu

f32 = jnp.float32
bf16 = jnp.bfloat16

GRID_W = 64
EPS = 1e-6
NEG_INF = -1e30

SSD_HEADS = 16
SSD_HEAD_DIM = 64
SSD_GROUPS = 2
SSD_STATE = 128
SSD_CONV = 5
SSD_CHUNK = 128
SSD_INNER = SSD_HEADS * SSD_HEAD_DIM
SSD_BC = SSD_GROUPS * SSD_STATE
SSD_XBC = SSD_INNER + 2 * SSD_BC
ATT_HEADS = 16
ATT_KV_HEADS = 4
ATT_HEAD_DIM = 64
ATT_BLOCK = 128
ROPE_THETA = 10000.0
ATT_Q = ATT_HEADS * ATT_HEAD_DIM
ATT_KVW = ATT_KV_HEADS * ATT_HEAD_DIM
S5_WIDTH = 1024
S5_GROUP_CH = 16
S5_GROUPS = S5_WIDTH // S5_GROUP_CH
S5_STATE = 64

LOG2E = math.log2(math.e)
LANES = 128
ROW_TILE = 1024
ATT_SUB = 4
S5_GBLK = 8
S5_TC = 8
S5_HGRP = 4
VMEM_LIMIT = 56 * 1024 * 1024


def _cparams(sem, flags=None):
    return pltpu.CompilerParams(dimension_semantics=sem, vmem_limit_bytes=VMEM_LIMIT, flags=flags)


def _silu(x):
    h = 0.5 * x
    return h + h * jnp.tanh(h)


def _adaln_kernel(c_ref, w_ref, b_ref, o_ref):
    c = c_ref[...]
    s = _silu(c).astype(bf16)
    o_ref[...] = jnp.dot(s, w_ref[...], preferred_element_type=f32) + b_ref[...]


def _adaln(cvecs, w, b):
    r, d = cvecs.shape
    n = w.shape[1]
    tn = 1024
    return pl.pallas_call(
        _adaln_kernel,
        out_shape=jax.ShapeDtypeStruct((r, n), f32),
        grid=(n // tn,),
        in_specs=[pl.BlockSpec((r, d), lambda j: (0, 0)),
                  pl.BlockSpec((d, tn), lambda j: (0, j)),
                  pl.BlockSpec((1, tn), lambda j: (0, j))],
        out_specs=pl.BlockSpec((r, tn), lambda j: (0, j)),
        compiler_params=_cparams(("arbitrary",)),
        name="adaln",
    )(cvecs, w, b)


def _inproj_kernel(segs, tsegs, n_w, has_rope, has_pre, x_ref, shift_ref, scale_ref, nw_ref, *rest):
    if has_pre:
        a_ref, b_ref, wo_ref, gate_ref = rest[:4]
        rest = rest[4:]
    w_refs, rest = rest[:n_w], rest[n_w:]
    if tsegs:
        wt_ref, rest = rest[0], rest[1:]
    if has_rope:
        cos_ref, sina_ref, sinb_ref = rest[:3]
        rest = rest[3:]
    if has_pre:
        x1_ref, rest = rest[0], rest[1:]
    outs = rest[:len(segs)]
    touts = rest[len(segs):len(segs) + len(tsegs)]
    rest = rest[len(segs) + len(tsegs):]
    slab_s = rest[0] if rest else None
    x = x_ref[...]
    if has_pre:
        ka = a_ref.shape[1]
        acc = jnp.dot(a_ref[...], wo_ref[:ka, :], preferred_element_type=f32)
        acc = acc + jnp.dot(b_ref[...], wo_ref[ka:, :], preferred_element_type=f32)
        x = x + gate_ref[...] * acc
        x1_ref[...] = x
    ms = jnp.mean(x * x, axis=-1, keepdims=True)
    h = (x * lax.rsqrt(ms + EPS)) * nw_ref[...]
    h = h * (1.0 + scale_ref[...]) + shift_ref[...]
    hb = h.astype(bf16)
    for (start, width, _), o_ref in zip(tsegs, touts):
        acc_t = lax.dot_general(wt_ref[start:start + width, :], hb, (((1,), (1,)), ((), ())),
                                preferred_element_type=f32)
        o_ref[...] = acc_t.astype(o_ref.dtype)
    products = {}
    for (widx, width, rope, _, chunked), o_ref in zip(segs, outs):
        if widx not in products:
            products[widx] = jnp.dot(hb, w_refs[widx][...], preferred_element_type=f32)
        acc = products[widx]
        if chunked:
            tm = acc.shape[0]
            hl = LANES // 2
            lo = lax.broadcasted_iota(jnp.int32, (1, LANES), 1) < hl
            for j in range(width // LANES):
                slab_s[j] = acc[:, j * LANES:(j + 1) * LANES]
            for j in range(width // LANES):
                rows = [slab_s[j, pl.ds(s, tm // S5_TC, stride=S5_TC), :] for s in range(S5_TC)]
                for p in range(S5_TC // 2):
                    a, b = rows[2 * p], rows[2 * p + 1]
                    h0 = jnp.where(lo, a, pltpu.roll(b, hl, axis=1))
                    h1 = jnp.where(lo, pltpu.roll(a, hl, axis=1), b)
                    o_ref[j, :, p * LANES:(p + 1) * LANES] = h0.astype(o_ref.dtype)
                    o_ref[j, :, S5_TC * hl + p * LANES:S5_TC * hl + (p + 1) * LANES] = h1.astype(o_ref.dtype)
            continue
        if rope is not None and not has_rope:
            acc = acc * rope
        elif rope is not None:
            rep = width // LANES
            cos = jnp.concatenate([cos_ref[...]] * rep, axis=1) * rope
            sina = jnp.concatenate([sina_ref[...]] * rep, axis=1) * rope
            sinb = jnp.concatenate([sinb_ref[...]] * rep, axis=1) * rope
            half = ATT_HEAD_DIM // 2
            up = pltpu.roll(acc, width - half, axis=1)
            dn = pltpu.roll(acc, half, axis=1)
            acc = acc * cos + up * sina + dn * sinb
        o_ref[...] = acc.astype(o_ref.dtype)


def _inproj(x, shift, scale, nw, w, segs, rows_per_mod, rope_tabs=None, wt=None, tsegs=(), pre=None):
    m, d = x.shape
    tm = min(ROW_TILE, rows_per_mod)
    per = rows_per_mod // tm
    nmod = shift.shape[0]
    mod_idx = (lambda i: (i // per, 0, 0)) if nmod > 1 else (lambda i: (0, 0, 0))
    in_specs = [pl.BlockSpec((tm, d), lambda i: (i, 0)),
                pl.BlockSpec((None, 1, d), mod_idx),
                pl.BlockSpec((None, 1, d), mod_idx),
                pl.BlockSpec((1, d), lambda i: (0, 0))]
    args = [x, shift, scale, nw]
    if pre is not None:
        a, b, w_out, gate = pre
        gate_idx = (lambda i: (i // per, 0, 0)) if gate.shape[0] > 1 else (lambda i: (0, 0, 0))
        in_specs += [pl.BlockSpec((tm, a.shape[1]), lambda i: (i, 0)),
                     pl.BlockSpec((tm, b.shape[1]), lambda i: (i, 0)),
                     pl.BlockSpec(w_out.shape, lambda i: (0, 0), pipeline_mode=pl.Buffered(1)),
                     pl.BlockSpec((None, 1, d), gate_idx)]
        args += [a, b, w_out, gate]
    in_specs += [pl.BlockSpec(wi.shape, lambda i: (0, 0), pipeline_mode=pl.Buffered(1)) for wi in w]
    args += list(w)
    if tsegs:
        in_specs.append(pl.BlockSpec(wt.shape, lambda i: (0, 0), pipeline_mode=pl.Buffered(1)))
        args.append(wt)
    if rope_tabs is not None:
        for t in rope_tabs:
            in_specs.append(pl.BlockSpec((tm, LANES), lambda i: (i % per, 0)))
            args.append(t)
    out_shape, out_specs = [], []
    if pre is not None:
        out_shape.append(jax.ShapeDtypeStruct((m, d), f32))
        out_specs.append(pl.BlockSpec((tm, d), lambda i: (i, 0)))
    for sg in segs:
        if sg[4]:
            out_shape.append(jax.ShapeDtypeStruct((sg[1] // LANES, m // S5_TC, S5_TC * LANES), sg[3]))
            out_specs.append(pl.BlockSpec((sg[1] // LANES, tm // S5_TC, S5_TC * LANES), lambda i: (0, i, 0)))
        else:
            out_shape.append(jax.ShapeDtypeStruct((m, sg[1]), sg[3]))
            out_specs.append(pl.BlockSpec((tm, sg[1]), lambda i: (i, 0)))
    for _, width, dtp in tsegs:
        out_shape.append(jax.ShapeDtypeStruct((width, m), dtp))
        out_specs.append(pl.BlockSpec((width, tm), lambda i: (0, i)))
    chunked_w = [sg[1] for sg in segs if sg[4]]
    scratch = [pltpu.VMEM((max(chunked_w) // LANES, tm, LANES), f32)] if chunked_w else []
    return pl.pallas_call(
        functools.partial(_inproj_kernel, tuple(segs), tuple(tsegs), len(w), rope_tabs is not None,
                          pre is not None),
        out_shape=out_shape,
        grid=(m // tm,),
        in_specs=in_specs,
        out_specs=out_specs,
        scratch_shapes=scratch,
        compiler_params=_cparams(("parallel",)),
        name="inproj",
    )(*args)


def _rope_tables(seq_len):
    rows = seq_len // GRID_W
    row = jnp.repeat(jnp.arange(rows, dtype=f32), GRID_W)
    col = jnp.tile(jnp.arange(GRID_W, dtype=f32), rows)
    n_freq = ATT_HEAD_DIM // 4
    inv = ROPE_THETA ** (-jnp.arange(n_freq, dtype=f32) / n_freq)
    ang = jnp.concatenate([row[:, None] * inv, col[:, None] * inv], axis=-1)
    cos, sin = jnp.cos(ang), jnp.sin(ang)
    zero = jnp.zeros_like(sin)
    cos_h = jnp.concatenate([cos, cos], axis=-1)
    sina_h = jnp.concatenate([-sin, zero], axis=-1)
    sinb_h = jnp.concatenate([zero, sin], axis=-1)
    two = lambda t: jnp.concatenate([t, t], axis=-1)
    return two(cos_h), two(sina_h), two(sinb_h)


SSD_PACK = 32


def _split3(x):
    hi = x.astype(bf16)
    r1 = x - hi.astype(f32)
    mid = r1.astype(bf16)
    lo = (r1 - mid.astype(f32)).astype(bf16)
    return hi, mid, lo


def _pack3(x):
    hi, mid, lo = _split3(x)
    lane = lax.broadcasted_iota(jnp.int32, x.shape, 1)
    mid_r = pltpu.roll(mid.astype(f32), SSD_PACK, axis=1)
    lo_r = pltpu.roll(lo.astype(f32), 2 * SSD_PACK, axis=1)
    packed = jnp.where(lane < SSD_PACK, hi.astype(f32),
                       jnp.where(lane < 2 * SSD_PACK, mid_r,
                                 jnp.where(lane < 3 * SSD_PACK, lo_r, 0.0)))
    return packed.astype(bf16)


def _ssd_selectors():
    k = np.arange(LANES)
    src = np.where(k < 3 * SSD_PACK, k % SSD_PACK, -1)
    col_blk = np.arange(SSD_PACK * SSD_CHUNK) // SSD_CHUNK
    sel_bc = (src[:, None] == col_blk[None, :])
    head = np.arange(SSD_INNER) // SSD_HEAD_DIM
    sel_f = (src[:, None] == head[None, :])
    sel_b = (src[:, None] == (head + SSD_HEADS)[None, :])
    tri3 = np.tile(np.tril(np.ones((SSD_CHUNK, SSD_CHUNK))), (1, 3))
    rows = np.arange(SSD_CHUNK)[:, None]
    cols = np.arange(SSD_CHUNK + 32)[None, :]
    half = SSD_CONV // 2
    shift = np.concatenate([cols == rows + 16 + d for d in range(-half, half + 1) if d != 0], axis=0)
    as_bf = lambda a: jnp.asarray(a.astype(np.float32), dtype=bf16)
    return as_bf(sel_bc), as_bf(sel_f), as_bf(sel_b), as_bf(tri3), as_bf(shift)


def _ssd_kernel(seq_len, xbc_ref, dt_ref, z_ref, cw_ref, cb_ref, dtb_ref, alog_ref, dskip_ref, nw_ref,
                selbc_ref, self_ref, selb_ref, tri3_ref, shift_ref, h0_ref, out_ref, hfin_ref,
                xs_s, bc_s, dt_s, y_s, hf_s, hb_s, win_s):
    q = SSD_CHUNK
    nc = seq_len // q
    halo = 16
    H, P, N = SSD_HEADS, SSD_HEAD_DIM, SSD_STATE
    gw = (H // SSD_GROUPS) * P
    a2_row = -jnp.exp(alog_ref[...]) * math.log2(math.e)

    def conv_chunk(c):
        r0 = pl.multiple_of(c * q, q)
        pstart = pl.multiple_of(jnp.maximum(r0 - halo, 0), halo)
        nstart = pl.multiple_of(jnp.minimum(r0 + q, seq_len - halo), halo)
        zero = jnp.zeros((), bf16)
        win_s[0:halo, :] = jnp.where(c > 0, xbc_ref[pl.ds(pstart, halo), :], zero)
        win_s[halo:halo + q, :] = xbc_ref[pl.ds(r0, q), :]
        win_s[halo + q:, :] = jnp.where(c < nc - 1, xbc_ref[pl.ds(nstart, halo), :], zero)
        taps = [k for k in range(SSD_CONV) if k != SSD_CONV // 2]
        cw = 2 * LANES
        for j in range(SSD_XBC // cw):
            cs = slice(j * cw, (j + 1) * cw)
            sh = jnp.dot(shift_ref[...], win_s[:, cs], preferred_element_type=f32)
            acc = cb_ref[:, cs] + win_s[halo:halo + q, cs].astype(f32) * cw_ref[SSD_CONV // 2:SSD_CONV // 2 + 1, cs]
            for n, k in enumerate(taps):
                acc = acc + sh[n * q:(n + 1) * q, :] * cw_ref[k:k + 1, cs]
            act = _silu(acc)
            if j < SSD_INNER // cw:
                xs_s[pl.ds(r0, q), cs] = act
            else:
                bc_s[pl.ds(r0, q), j * cw - SSD_INNER:(j + 1) * cw - SSD_INNER] = act.astype(bf16)
        dt_s[pl.ds(r0, q), :] = jax.nn.softplus(dt_ref[pl.ds(r0, q), :] + dtb_ref[...])

    conv_chunk(0)
    hf_s[...] = h0_ref[0]
    hb_s[...] = h0_ref[1]

    ri = lax.broadcasted_iota(jnp.int32, (q, q), 0)
    ci = lax.broadcasted_iota(jnp.int32, (q, q), 1)
    lower = ri >= ci
    upper = ci >= ri
    lane = lax.broadcasted_iota(jnp.int32, (q, LANES), 1)
    lo_half = lane < P

    def cumsums(dt):
        dta = dt * a2_row
        cf = jnp.dot(tri3_ref[...], jnp.concatenate(_split3(dta), axis=0), preferred_element_type=f32)
        rb = cf[q - 1:q, :] - cf + dta
        return cf, rb

    def load_chunk(r0):
        dt = dt_s[pl.ds(r0, q), :]
        xs = xs_s[pl.ds(r0, q), :]
        bcv = bc_s[pl.ds(r0, q), :]
        bmat = [bcv[:, g * N:(g + 1) * N] for g in range(SSD_GROUPS)]
        cmat = [bcv[:, SSD_BC + g * N:SSD_BC + (g + 1) * N] for g in range(SSD_GROUPS)]
        return dt, xs, bmat, cmat

    def inter_chunk(h_s, sel_ref, decay, weight, xs, bmat, cmat, dec_idx):
        ew = jnp.dot(jnp.concatenate([_pack3(decay), _pack3(weight)], axis=0), sel_ref[...],
                     preferred_element_type=f32)
        e_x, w_x = ew[:q], ew[q:]
        hb_ = h_s[...].astype(bf16)
        yoff = jnp.concatenate(
            [jnp.dot(cmat[g], hb_[:, g * gw:(g + 1) * gw], preferred_element_type=f32)
             for g in range(SSD_GROUPS)], axis=1)
        xw = (xs * w_x).astype(bf16)
        dec_row = e_x[dec_idx:dec_idx + 1, :]
        for g in range(SSD_GROUPS):
            gs = slice(g * gw, (g + 1) * gw)
            bt = jnp.transpose(bmat[g].astype(f32)).astype(bf16)
            upd = jnp.dot(bt, xw[:, gs], preferred_element_type=f32)
            h_s[:, gs] = h_s[:, gs] * dec_row[:, gs] + upd
        return yoff * e_x

    def finish(r0, y, xs):
        yy = y + xs * dskip_ref[...]
        zz = z_ref[pl.ds(r0, q), :].astype(f32)
        gated = yy * _silu(zz)
        ms = jnp.mean(gated * gated, axis=-1, keepdims=True)
        out_ref[pl.ds(r0, q), :] = (gated * lax.rsqrt(ms + EPS) * nw_ref[...]).astype(out_ref.dtype)

    def fwd_chunk(c, second_half):
        r0 = pl.multiple_of(c * q, q)
        dt, xs, bmat, cmat = load_chunk(r0)
        cf, rb = cumsums(dt)
        pcol = jnp.where(lane < H, cf, rb)
        bcast = jnp.dot(_pack3(pcol), selbc_ref[...], preferred_element_type=f32)
        prow = jnp.transpose(pcol - jnp.log2(dt))
        cbm = [lax.dot_general(cmat[g], bmat[g], (((1,), (1,)), ((), ())), preferred_element_type=f32)
               for g in range(SSD_GROUPS)]
        xsb = xs.astype(bf16)
        zero_b = jnp.zeros((), bf16)
        ypairs = []
        for k in range(H // 2):
            mats = []
            for h in (2 * k, 2 * k + 1):
                g = h // (H // SSD_GROUPS)
                hb_ = H + h
                segf = bcast[:, h * q:(h + 1) * q] - prow[h:h + 1, :]
                segb = bcast[:, hb_ * q:(hb_ + 1) * q] - prow[hb_:hb_ + 1, :]
                df = jnp.exp2(jnp.where(lower, segf, NEG_INF))
                db = jnp.exp2(jnp.where(upper, segb, NEG_INF))
                mats.append((cbm[g] * (df + db)).astype(bf16))
            xp = xsb[:, k * LANES:(k + 1) * LANES]
            xbd = jnp.concatenate([jnp.where(lo_half, xp, zero_b), jnp.where(lo_half, zero_b, xp)], axis=0)
            ypairs.append(jnp.dot(jnp.concatenate(mats, axis=1), xbd, preferred_element_type=f32))
        y = jnp.concatenate(ypairs, axis=1)
        wfa = jnp.exp2(cf[q - 1:q, :] - cf) * dt
        y = y + inter_chunk(hf_s, self_ref, jnp.exp2(cf), wfa, xs, bmat, cmat, q - 1)
        if second_half:
            finish(r0, y_s[pl.ds(r0, q), :] + y, xs)
        else:
            y_s[pl.ds(r0, q), :] = y

    def bwd_chunk(c, second_half):
        r0 = pl.multiple_of(c * q, q)
        dt, xs, bmat, cmat = load_chunk(r0)
        _, rb = cumsums(dt)
        wba = jnp.exp2(rb[0:1, :] - rb) * dt
        y = inter_chunk(hb_s, selb_ref, jnp.exp2(rb), wba, xs, bmat, cmat, 0)
        if second_half:
            finish(r0, y_s[pl.ds(r0, q), :] + y, xs)
        else:
            y_s[pl.ds(r0, q), :] = y

    half = nc // 2
    conv_chunk(nc - 1)

    def first_half(i, carry):
        fwd_chunk(i, False)
        bwd_chunk(nc - 1 - i, False)
        conv_chunk(i + 1)
        conv_chunk(nc - 2 - i)
        return carry

    def second_half(i, carry):
        fwd_chunk(i, True)
        bwd_chunk(nc - 1 - i, True)
        return carry

    lax.fori_loop(0, half - 1, first_half, 0)
    fwd_chunk(jnp.int32(half - 1), False)
    bwd_chunk(jnp.int32(half), False)
    lax.fori_loop(half, nc, second_half, 0)
    hfin_ref[0] = hf_s[...]
    hfin_ref[1] = hb_s[...]


def _ssd(xbc, dt, z, conv_w, conv_b, dtb, alog, dskip, nw, h0, bsz, seq_len):
    assert seq_len % (2 * SSD_CHUNK) == 0, "the two recurrences meet in the middle: even chunk count"
    one = pl.Buffered(1)
    seq = lambda w: pl.BlockSpec((seq_len, w), lambda b: (b, 0), pipeline_mode=one)
    const = lambda r, w: pl.BlockSpec((r, w), lambda b: (0, 0))
    st = pl.BlockSpec((None, 2, SSD_STATE, SSD_INNER), lambda b: (b, 0, 0, 0))
    sels = _ssd_selectors()
    return pl.pallas_call(
        functools.partial(_ssd_kernel, seq_len),
        out_shape=[jax.ShapeDtypeStruct((bsz * seq_len, SSD_INNER), bf16),
                   jax.ShapeDtypeStruct((bsz, 2, SSD_STATE, SSD_INNER), f32)],
        grid=(bsz,),
        in_specs=[pl.BlockSpec((seq_len, SSD_XBC), lambda b: (b, 0)), seq(LANES),
                  pl.BlockSpec((seq_len, SSD_INNER), lambda b: (b, 0)),
                  const(8, SSD_XBC), const(1, SSD_XBC), const(1, LANES), const(1, LANES),
                  const(1, SSD_INNER), const(1, SSD_INNER)]
                 + [const(*s.shape) for s in sels] + [st],
        out_specs=[seq(SSD_INNER), st],
        scratch_shapes=[pltpu.VMEM((seq_len, SSD_INNER), f32),
                        pltpu.VMEM((seq_len, 2 * SSD_BC), bf16),
                        pltpu.VMEM((seq_len, LANES), f32),
                        pltpu.VMEM((seq_len, SSD_INNER), f32),
                        pltpu.VMEM((SSD_STATE, SSD_INNER), f32),
                        pltpu.VMEM((SSD_STATE, SSD_INNER), f32),
                        pltpu.VMEM((SSD_CHUNK + 32, SSD_XBC), bf16)],
        compiler_params=_cparams(("parallel",)),
        name="ssd",
    )(xbc, dt, z, conv_w, conv_b, dtb, alog, dskip, nw, *sels, h0)


def _attn_kernel(n_blocks, nsub, local, q_ref, g_ref, k_ref, kc_ref, *rest):
    nv = nsub + 2 if local else 0
    v_refs = rest[:nv]
    vc_ref, sink_ref, o_ref, s_s = rest[nv:]
    t = ATT_BLOCK
    i0 = pl.program_id(1) * nsub
    rpk = ATT_HEADS // ATT_KV_HEADS
    lane = lax.broadcasted_iota(jnp.int32, (t, LANES), 1)
    lo_half = lane < ATT_HEAD_DIM
    zero_b = jnp.zeros((), bf16)
    kl = lax.broadcasted_iota(jnp.int32, (t, t), 0)
    ql = lax.broadcasted_iota(jnp.int32, (t, t), 1)
    for sub in range(nsub):
        i = i0 + sub
        qv = q_ref[sub * t:(sub + 1) * t, :]
        if local:
            p0 = pl.multiple_of(jnp.maximum(i - 1, 0) * t, t)
            c0 = pl.multiple_of(i * t, t)
            n0 = pl.multiple_of(jnp.minimum(i + 1, n_blocks - 1) * t, t)
            bias_prev = jnp.where((kl >= ql) & (i > 0), 0.0, NEG_INF)
            bias_next = jnp.where((kl <= ql) & (i < n_blocks - 1), 0.0, NEG_INF)
            bias_prev = jnp.concatenate([bias_prev] * rpk, axis=1)
            bias_next = jnp.concatenate([bias_next] * rpk, axis=1)
        for j in range(ATT_KV_HEADS):
            ls = slice(j * LANES, (j + 1) * LANES)
            if local:
                kk = jnp.concatenate([k_ref[pl.ds(p0, t), ls], k_ref[pl.ds(c0, t), ls],
                                      k_ref[pl.ds(n0, t), ls], kc_ref[:, ls]], axis=0)
            else:
                kk = kc_ref[:, ls]
            pieces = []
            for r in range(rpk):
                hq = j * rpk + r
                qp = qv[:, (hq // 2) * LANES:(hq // 2 + 1) * LANES]
                keep = lo_half if hq % 2 == 0 else jnp.logical_not(lo_half)
                pieces.append(jnp.where(keep, qp, zero_b))
            q4 = jnp.concatenate(pieces, axis=0)
            s = lax.dot_general(kk, q4, (((1,), (1,)), ((), ())), preferred_element_type=f32)
            if local:
                s = jnp.concatenate([s[:t] + bias_prev, s[t:2 * t], s[2 * t:3 * t] + bias_next, s[3 * t:]],
                                    axis=0)
            s_s[sub, j] = s
    for sub in range(nsub):
        outs = []
        for j in range(ATT_KV_HEADS):
            vs = slice(j * ATT_HEAD_DIM, (j + 1) * ATT_HEAD_DIM)
            if local:
                vvt = jnp.concatenate([v_refs[sub + k][vs, :] for k in range(3)] + [vc_ref[vs, :]], axis=1)
            else:
                vvt = vc_ref[vs, :]
            sk = jnp.concatenate([jnp.full((1, t), sink_ref[j * rpk + r] * LOG2E, f32) for r in range(rpk)],
                                 axis=1)
            s = s_s[sub, j]
            m = jnp.maximum(jnp.max(s, axis=0, keepdims=True), sk)
            p = jnp.exp2(s - m)
            den = jnp.sum(p, axis=0, keepdims=True) + jnp.exp2(sk - m)
            vvt = jnp.concatenate([vvt, vvt], axis=0)
            ot = jnp.dot(vvt, p.astype(bf16), preferred_element_type=f32) / den
            o4 = [jnp.transpose(ot[:, r * t:(r + 1) * t]) for r in range(rpk)]
            outs.append(jnp.where(lo_half, o4[0], o4[1]))
            outs.append(jnp.where(lo_half, o4[2], o4[3]))
        o = jnp.concatenate(outs, axis=1)
        gv = g_ref[sub * t:(sub + 1) * t, :].astype(f32)
        o_ref[sub * t:(sub + 1) * t, :] = (o * _silu(gv)).astype(o_ref.dtype)


def _attention(q, g, k, vt, kc, vct, sink, bsz, seq_len, local):
    t = ATT_BLOCK
    nb = seq_len // t
    nsub = max(s for s in range(1, ATT_SUB + 1) if nb % s == 0)
    n_ctx = kc.shape[0] // bsz
    kw = ATT_KV_HEADS * LANES
    vw = ATT_KVW
    blk = pl.BlockSpec((nsub * t, ATT_Q), lambda b, i: (b * (nb // nsub) + i, 0))
    full = lambda n: pl.BlockSpec((n, kw), lambda b, i: (b, 0))
    vblk = lambda off: pl.BlockSpec((vw, t), lambda b, i: (0, b * nb + jnp.clip(i * nsub + off, 0, nb - 1)))
    vspecs = [vblk(off) for off in range(-1, nsub + 1)] if local else []
    return pl.pallas_call(
        functools.partial(_attn_kernel, nb, nsub, local),
        out_shape=jax.ShapeDtypeStruct((bsz * seq_len, ATT_Q), bf16),
        grid=(bsz, nb // nsub),
        in_specs=[blk, blk, full(k.shape[0] // bsz), full(n_ctx)] + vspecs
                 + [pl.BlockSpec((vw, n_ctx), lambda b, i: (0, b)), pl.BlockSpec(memory_space=pltpu.SMEM)],
        out_specs=blk,
        scratch_shapes=[pltpu.VMEM((nsub, ATT_KV_HEADS, (3 * t if local else 0) + n_ctx,
                                    (ATT_HEADS // ATT_KV_HEADS) * t), f32)],
        compiler_params=_cparams(("parallel", "arbitrary")),
        name="attention",
    )(q, g, k, kc, *([vt] * len(vspecs)), vct, sink)


def _s5_disc_kernel(lre_ref, lim_ref, ls_ref, bre_ref, bim_ref, abre_ref, abim_ref, bbre_ref, bbim_ref):
    lam_re = lre_ref[...]
    lam_im = lim_ref[...]
    dt = jnp.exp(ls_ref[...])
    mag = jnp.exp(lam_re * dt)
    ab_re = mag * jnp.cos(lam_im * dt)
    ab_im = mag * jnp.sin(lam_im * dt)
    num_re, num_im = ab_re - 1.0, ab_im
    den = lam_re * lam_re + lam_im * lam_im
    coef_re = (num_re * lam_re + num_im * lam_im) / den
    coef_im = (num_im * lam_re - num_re * lam_im) / den
    b_re, b_im = bre_ref[...], bim_ref[...]
    abre_ref[...] = ab_re
    abim_ref[...] = ab_im
    bbre_ref[...] = coef_re * b_re - coef_im * b_im
    bbim_ref[...] = coef_re * b_im + coef_im * b_re


def _s5_discretise(lam_re, lam_im, log_step, b_re, b_im):
    g, n, cg = b_re.shape
    exp = lambda t: jnp.repeat(t.reshape(2 * g, n), cg, axis=1)
    ls = jnp.broadcast_to(log_step.reshape(2 * g, 1), (2 * g, n * cg))
    bb = lambda t: jnp.tile(t.reshape(g, n * cg), (2, 1))
    shp = jax.ShapeDtypeStruct((2 * g, n * cg), f32)
    ab_re, ab_im, bb_re, bb_im = pl.pallas_call(
        _s5_disc_kernel, out_shape=[shp] * 4, name="s5_disc",
    )(exp(lam_re), exp(lam_im), ls, bb(b_re), bb(b_im))
    first = lambda t: t.reshape(2, g, n, cg)[..., 0]
    full = lambda t: t.reshape(2, g, n, cg)
    return first(ab_re), first(ab_im), full(bb_re), full(bb_im)


def _cmul(ar, ai, br, bi):
    return ar * br - ai * bi, ar * bi + ai * br


def _s5_kernel(n_lat, n_ctx, *refs):
    tc = S5_TC
    xl, xc, arow_ref, acol_ref, bbd_ref, cbd_ref, y_ref, wyz_s, ws_s, sl_s, sc_s = refs
    hw = S5_HGRP * S5_STATE
    hl = LANES // 2
    xw = tc * hl
    nsl = hw // LANES
    nb = 8
    lo = lax.broadcasted_iota(jnp.int32, (1, LANES), 1) < hl

    def powers(re, im, n):
        out = [(jnp.ones_like(re), jnp.zeros_like(im))]
        for _ in range(n):
            out.append(_cmul(out[-1][0], out[-1][1], re, im))
        return out

    steps = lambda h, p: (2 * p, 2 * p + 1) if h == 0 else (2 * p + 1, 2 * p)

    prow = []
    for h in range(2):
        pr = [powers(arow_ref[h, 2 * d:2 * d + 1, :], arow_ref[h, 2 * d + 1:2 * d + 2, :], tc) for d in range(2)]
        prow.append(pr)
        for s in range(tc):
            for d, k in ((0, tc - 1 - s), (1, s)):
                wr, wi = _cmul(bbd_ref[2 * d, h], bbd_ref[2 * d + 1, h], *pr[d][k])
                ws_s[h, s * hl:(s + 1) * hl, d * 2 * hw:d * 2 * hw + hw] = wr.astype(bf16)
                ws_s[h, s * hl:(s + 1) * hl, d * 2 * hw + hw:(d + 1) * 2 * hw] = wi.astype(bf16)
        crhs = jnp.concatenate([cbd_ref[0, h], -cbd_ref[1, h]], axis=0).astype(bf16)
        kall = [jnp.dot(ws_s[h, :, d * 2 * hw:(d + 1) * 2 * hw], crhs, preferred_element_type=f32)
                for d in range(2)]
        kf = [kall[0][(tc - 1 - k) * hl:(tc - k) * hl] for k in range(tc)]
        kb = [kall[1][k * hl:(k + 1) * hl] for k in range(tc)]
        lag = lambda s, t: kf[t - s] if t > s else (kb[s - t] if t < s else kf[0] + kb[0])
        for s in range(tc):
            for p in range(tc // 2):
                t0, t1 = steps(h, p)
                wyz_s[h, s * hl:(s + 1) * hl, p * LANES:(p + 1) * LANES] = (
                    jnp.where(lo, lag(s, t0), lag(s, t1)).astype(bf16))
        for d in range(2):
            pc = powers(acol_ref[h, 2 * d], acol_ref[h, 2 * d + 1], tc)
            kk = (lambda t: t + 1) if d == 0 else (lambda t: tc - t)
            r0 = xw + d * 2 * hw
            for p in range(tc // 2):
                t0, t1 = steps(h, p)
                ar = jnp.where(lo, pc[kk(t0)][0], pc[kk(t1)][0])
                ai = jnp.where(lo, pc[kk(t0)][1], pc[kk(t1)][1])
                dre, dim_ = _cmul(cbd_ref[0, h], cbd_ref[1, h], ar, ai)
                wyz_s[h, r0:r0 + hw, p * LANES:(p + 1) * LANES] = dre.astype(bf16)
                wyz_s[h, r0 + hw:r0 + 2 * hw, p * LANES:(p + 1) * LANES] = (-dim_).astype(bf16)

    def rows_of(x_ref, b, n, h):
        return x_ref[b * n:(b + 1) * n, h * xw:(h + 1) * xw]

    def inject(x_ref, s_ref, n):
        for b in range(nb):
            for h in range(2):
                sb = jnp.dot(rows_of(x_ref, b, n, h), ws_s[h], preferred_element_type=f32)
                for k in range(4 * nsl):
                    s_ref[h * 4 * nsl + k, pl.ds(b, n, stride=nb), :] = sb[:, k * LANES:(k + 1) * LANES]

    at = [[[tuple(jnp.broadcast_to(p[:, k * LANES:(k + 1) * LANES], (nb, LANES)) for p in prow[h][d][tc])
            for k in range(nsl)] for d in range(2)] for h in range(2)]

    def scan(s_ref, n, init):
        def step(i, carry):
            idxs = [pl.ds(pl.multiple_of(i * nb, nb), nb), pl.ds(pl.multiple_of((n - 1 - i) * nb, nb), nb)]
            chains = [(h, d, k) for h in range(2) for d in range(2) for k in range(nsl)]
            slab = lambda h, d, k: h * 4 * nsl + d * 2 * nsl + k
            inj = [(s_ref[slab(h, d, k), idxs[d], :], s_ref[slab(h, d, k) + nsl, idxs[d], :])
                   for h, d, k in chains]
            new = []
            for c, (h, d, k) in enumerate(chains):
                hr, hi = carry[2 * c], carry[2 * c + 1]
                s_ref[slab(h, d, k), idxs[d], :] = hr
                s_ref[slab(h, d, k) + nsl, idxs[d], :] = hi
                ar, ai = at[h][d][k]
                new += [ar * hr - ai * hi + inj[c][0], ar * hi + ai * hr + inj[c][1]]
            return tuple(new)
        return lax.fori_loop(0, n, step, init, unroll=4)

    inject(xc, sc_s, n_ctx)
    h_ctx = scan(sc_s, n_ctx, tuple(jnp.zeros((nb, LANES), f32) for _ in range(8 * nsl)))
    inject(xl, sl_s, n_lat)
    scan(sl_s, n_lat, h_ctx)

    for b in range(nb):
        yh = []
        for h in range(2):
            hin = jnp.concatenate([sl_s[h * 4 * nsl + k, pl.ds(b, n_lat, stride=nb), :]
                                   for k in range(4 * nsl)], axis=1).astype(bf16)
            yh.append(jnp.dot(jnp.concatenate([rows_of(xl, b, n_lat, h), hin], axis=1), wyz_s[h],
                              preferred_element_type=f32))
        for p in range(tc // 2):
            y0, y1 = yh[0][:, p * LANES:(p + 1) * LANES], yh[1][:, p * LANES:(p + 1) * LANES]
            even = jnp.where(lo, y0, y1)
            odd = pltpu.roll(jnp.where(lo, y1, y0), hl, axis=1)
            y_ref[pl.ds(b * n_lat * tc + 2 * p, n_lat, stride=tc), :] = even
            y_ref[pl.ds(b * n_lat * tc + 2 * p + 1, n_lat, stride=tc), :] = odd


def _s5_mix(u, u_c, arow, acol, bbd, cbd, bsz, seq_len, n_ctx_tok):
    assert bsz == 8, "the chunk recurrence puts the batch on the 8 sublanes"
    tc = S5_TC
    n_lat, n_ctx = seq_len // tc, n_ctx_tok // tc
    nblk = S5_GROUPS // S5_GBLK
    hw = S5_HGRP * S5_STATE
    hl = LANES // 2
    nsl = hw // LANES
    one = pl.Buffered(1)
    xspec = lambda rows: pl.BlockSpec((None, rows, tc * LANES), lambda g: (g, 0, 0))
    return pl.pallas_call(
        functools.partial(_s5_kernel, n_lat, n_ctx),
        out_shape=jax.ShapeDtypeStruct((nblk, bsz * seq_len, LANES), f32),
        grid=(nblk,),
        in_specs=[xspec(bsz * n_lat), xspec(bsz * n_ctx),
                  pl.BlockSpec((2, 4, hw), lambda g: (g, 0, 0), pipeline_mode=one),
                  pl.BlockSpec((2, 4, hw, LANES), lambda g: (g, 0, 0, 0), pipeline_mode=one),
                  pl.BlockSpec((4, 2, hl, hw), lambda g: (0, g, 0, 0), pipeline_mode=one),
                  pl.BlockSpec((2, 2, hw, LANES), lambda g: (0, g, 0, 0), pipeline_mode=one)],
        out_specs=pl.BlockSpec((None, bsz * seq_len, LANES), lambda g: (g, 0, 0)),
        scratch_shapes=[pltpu.VMEM((2, tc * hl + 4 * hw, tc * hl), bf16),
                        pltpu.VMEM((2, tc * hl, 4 * hw), bf16),
                        pltpu.VMEM((8 * nsl, bsz * n_lat, LANES), f32),
                        pltpu.VMEM((8 * nsl, bsz * n_ctx, LANES), f32)],
        compiler_params=_cparams(("arbitrary",)),
        name="s5_mix",
    )(u, u_c, arow, acol, bbd, cbd)


def _s5_block_params(ab_re, ab_im, bb_re, bb_im, c_re, c_im):
    g, n, cg = S5_GROUPS, S5_STATE, S5_GROUP_CH
    nh = g // S5_HGRP
    hw = S5_HGRP * n
    arow = jnp.stack([t[d].reshape(nh, hw) for d in range(2) for t in (ab_re, ab_im)], axis=1)
    acol = jnp.broadcast_to(arow[..., None], (nh, 4, hw, LANES))

    def blockdiag(t, rows_per, cols_per):
        tiled = jnp.concatenate([t] * S5_HGRP, axis=-1)
        r = np.arange(t.shape[-2])[:, None] // rows_per
        c = np.arange(S5_HGRP * cols_per)[None, :] // cols_per
        return jnp.where(jnp.asarray(r == c), tiled, 0.0)

    bb = jnp.stack([t[d] for d in range(2) for t in (bb_re, bb_im)], axis=0)
    bb = jnp.swapaxes(bb.reshape(4, nh, S5_HGRP, n, cg), -1, -2).reshape(4, nh, S5_HGRP * cg, n)
    bbd = blockdiag(bb, cg, n)
    cc = jnp.swapaxes(jnp.stack([c_re, c_im], axis=0).reshape(2, nh, S5_HGRP, cg, n), -1, -2)
    cbd = blockdiag(cc.reshape(2, nh, hw, cg), n, cg)
    return arow, acol, bbd, jnp.concatenate([cbd, cbd], axis=-1)


def _s5_out_kernel(y_ref, u_ref, g_ref, x_ref, dskip_ref, gw_ref, gb_ref, w_ref, gate_ref,
                   fw_ref, o_ref):
    y = jnp.concatenate([y_ref[j] for j in range(y_ref.shape[0])], axis=1)
    y = y + dskip_ref[...] * u_ref[...].astype(f32)
    y = jax.nn.gelu(y)
    glu = jnp.dot(y.astype(bf16), gw_ref[...], preferred_element_type=f32) + gb_ref[...]
    y = y * jax.nn.sigmoid(glu)
    y = y * _silu(g_ref[...].astype(f32))
    x = x_ref[...] + gate_ref[...] * jnp.dot(y.astype(bf16), w_ref[...], preferred_element_type=f32)
    ms = jnp.mean(x * x, axis=-1, keepdims=True)
    o_ref[...] = x * lax.rsqrt(ms + EPS) * fw_ref[...]


def _s5_out(y, u, g, x, dskip, gw, gb, w, gate, fw, rows_per_mod):
    m, d = x.shape
    tm = min(ROW_TILE, rows_per_mod)
    per = rows_per_mod // tm
    row = lambda: pl.BlockSpec((tm, d), lambda i: (i, 0))
    vec = lambda: pl.BlockSpec((1, d), lambda i: (0, 0))
    mat = lambda: pl.BlockSpec((d, d), lambda i: (0, 0), pipeline_mode=pl.Buffered(1))
    return pl.pallas_call(
        _s5_out_kernel,
        out_shape=jax.ShapeDtypeStruct((m, d), f32),
        grid=(m // tm,),
        in_specs=[pl.BlockSpec((y.shape[0], tm, LANES), lambda i: (0, i, 0)),
                  row(), row(), row(), vec(), mat(), vec(), mat(),
                  pl.BlockSpec((None, 1, d), lambda i: (i // per, 0, 0)), vec()],
        out_specs=row(),
        compiler_params=_cparams(("parallel",)),
        name="s5_out",
    )(y, u, g, x, dskip, gw, gb, w, gate, fw)


def _even_weights(w_in):
    o = 0
    z = w_in[:, o:o + SSD_INNER]; o += SSD_INNER
    xbc = w_in[:, o:o + SSD_XBC]; o += SSD_XBC
    dt = w_in[:, o:o + 2 * SSD_HEADS]; o += 2 * SSD_HEADS
    q = w_in[:, o:o + ATT_Q]; o += ATT_Q
    k = w_in[:, o:o + ATT_KVW]; o += ATT_KVW
    v = w_in[:, o:o + ATT_KVW]; o += ATT_KVW
    g = w_in[:, o:o + ATT_Q]
    d = w_in.shape[0]
    dup = lambda t: jnp.concatenate([t.reshape(d, ATT_KV_HEADS, 1, ATT_HEAD_DIM)] * 2, axis=2).reshape(d, -1)
    dtp = jnp.pad(dt, ((0, 0), (0, LANES - 2 * SSD_HEADS)))
    cast = lambda t: t.astype(bf16)
    return [cast(t) for t in (z, xbc, q, dup(k), g, dtp)], cast(v.T)


def _even_segs(rope):
    scale = ATT_HEAD_DIM ** -0.5 * LOG2E
    widths = [(SSD_INNER, None, bf16), (SSD_XBC, None, bf16),
              (ATT_Q, scale, bf16),
              (ATT_KV_HEADS * LANES, 1.0 if rope else None, bf16),
              (ATT_Q, None, bf16), (LANES, None, f32)]
    return [(i, w, r, dtp, False) for i, (w, r, dtp) in enumerate(widths)]


def _pad_lanes(v, n=LANES):
    v = v.reshape(1, -1)
    return jnp.pad(v, ((0, 0), (0, n - v.shape[1])))


def kernel(x, c, ctx, c_ctx, e_norm_w, e_ada_w, e_ada_b, e_w_in, e_conv_w, e_conv_b, e_dt_bias,
           e_a_log, e_d_skip, e_ssd_norm_w, e_sink, e_w_out, o_norm_w, o_ada_w, o_ada_b, o_w_in,
           o_lam_re, o_lam_im, o_log_step, o_b_re, o_b_im, o_c_re, o_c_im, o_d_skip, o_glu_w,
           o_glu_b, o_w_out, final_norm_w):
    bsz, seq_len, d = x.shape
    n_ctx = ctx.shape[1]
    assert e_w_in.shape[0] == 1 and o_w_in.shape[0] == 1, "one even (SSD + attention) and one odd (S5) layer"
    xf = x.reshape(bsz * seq_len, d)
    xcf = ctx.reshape(bsz * n_ctx, d)

    cvecs = jnp.concatenate([c, c_ctx[None, :], jnp.zeros((16 - bsz - 1, d), f32)], axis=0)

    def modulation(ada_w, ada_b):
        mod = _adaln(cvecs, ada_w.astype(bf16), ada_b.reshape(1, -1))
        parts = [mod[:, k * d:(k + 1) * d] for k in range(3)]
        lat = [p[:bsz].reshape(bsz, 1, d) for p in parts]
        cx = [p[bsz:bsz + 1].reshape(1, 1, d) for p in parts]
        return lat, cx

    (shift, scale, gate), (shift_c, scale_c, gate_c) = modulation(e_ada_w[0], e_ada_b[0])
    w_in, w_vt = _even_weights(e_w_in[0])
    nw = e_norm_w[0].reshape(1, d)
    tabs = _rope_tables(seq_len)
    vseg = [(0, ATT_KVW, bf16)]
    z, xbc, q, k, g, dt, vt = _inproj(xf, shift, scale, nw, w_in, _even_segs(True), seq_len, tabs, w_vt, vseg)
    z_c, xbc_c, q_c, k_c, g_c, dt_c, vt_c = _inproj(xcf, shift_c, scale_c, nw, w_in, _even_segs(False), n_ctx,
                                                    None, w_vt, vseg)

    conv_w = jnp.pad(e_conv_w[0], ((0, 8 - SSD_CONV), (0, 0)))
    conv_b = e_conv_b[0].reshape(1, -1)
    dtb = _pad_lanes(e_dt_bias[0])
    alog = _pad_lanes(e_a_log[0])
    dskip = jnp.repeat(e_d_skip[0], SSD_HEAD_DIM).reshape(1, -1)
    snw = e_ssd_norm_w[0].reshape(1, -1)
    h0 = jnp.zeros((bsz, 2, SSD_STATE, SSD_INNER), f32)
    ssd_c, hfin = _ssd(xbc_c, dt_c, z_c, conv_w, conv_b, dtb, alog, dskip, snw, h0, bsz, n_ctx)
    ssd_o, _ = _ssd(xbc, dt, z, conv_w, conv_b, dtb, alog, dskip, snw, hfin, bsz, seq_len)

    sink = e_sink[0]
    att = _attention(q, g, k, vt, k_c, vt_c, sink, bsz, seq_len, True)
    att_c = _attention(q_c, g_c, k_c, vt_c, k_c, vt_c, sink, bsz, n_ctx, False)
    w_out = e_w_out[0].astype(bf16)
    even_gate, even_gate_c = gate, gate_c

    (shift, scale, gate), (shift_c, scale_c, _) = modulation(o_ada_w[0], o_ada_b[0])
    w_u = o_w_in[0][:, :S5_WIDTH].astype(bf16)
    w_g = o_w_in[0][:, S5_WIDTH:].astype(bf16)
    nw = o_norm_w[0].reshape(1, d)
    x1, u, u_ch, g2 = _inproj(xf, shift, scale, nw, [w_u, w_g],
                              [(0, S5_WIDTH, None, bf16, False), (0, S5_WIDTH, None, bf16, True),
                               (1, S5_WIDTH, None, bf16, False)], seq_len,
                              pre=(ssd_o, att, w_out, even_gate))
    _, uc_ch = _inproj(xcf, shift_c, scale_c, nw, [w_u], [(0, S5_WIDTH, None, bf16, True)], n_ctx,
                       pre=(ssd_c, att_c, w_out, even_gate_c))

    ab_re, ab_im, bb_re, bb_im = _s5_discretise(o_lam_re[0], o_lam_im[0], o_log_step[0], o_b_re[0], o_b_im[0])
    arow, acol, bbd, cbd = _s5_block_params(ab_re, ab_im, bb_re, bb_im, o_c_re[0], o_c_im[0])
    y = _s5_mix(u_ch, uc_ch, arow, acol, bbd, cbd, bsz, seq_len, n_ctx)
    out = _s5_out(y, u, g2, x1, o_d_skip[0].reshape(1, -1),
                  o_glu_w[0].astype(bf16), o_glu_b[0].reshape(1, -1), o_w_out[0].astype(bf16), gate,
                  final_norm_w.reshape(1, -1), seq_len)
    return out.reshape(bsz, seq_len, d)
```

```python
import functools
import math

import jax
import jax.numpy as jnp
import numpy as np
from jax import lax
from jax.experimental import pallas as pl
from jax.experimental.pallas import tpu as pltpu

f32 = jnp.float32
bf16 = jnp.bfloat16

GRID_W = 64
EPS = 1e-6
NEG_INF = -1e30

SSD_HEADS = 16
SSD_HEAD_DIM = 64
SSD_GROUPS = 2
SSD_STATE = 128
SSD_CONV = 5
SSD_CHUNK = 128
SSD_INNER = SSD_HEADS * SSD_HEAD_DIM
SSD_BC = SSD_GROUPS * SSD_STATE
SSD_XBC = SSD_INNER + 2 * SSD_BC
ATT_HEADS = 16
ATT_KV_HEADS = 4
ATT_HEAD_DIM = 64
ATT_BLOCK = 128
ROPE_THETA = 10000.0
ATT_Q = ATT_HEADS * ATT_HEAD_DIM
ATT_KVW = ATT_KV_HEADS * ATT_HEAD_DIM
S5_WIDTH = 1024
S5_GROUP_CH = 16
S5_GROUPS = S5_WIDTH // S5_GROUP_CH
S5_STATE = 64

LOG2E = math.log2(math.e)
LANES = 128
ROW_TILE = 1024
ATT_SUB = 4
S5_GBLK = 8
S5_TC = 8
S5_HGRP = 4
VMEM_LIMIT = 56 * 1024 * 1024


def _cparams(sem, flags=None):
    return pltpu.CompilerParams(dimension_semantics=sem, vmem_limit_bytes=VMEM_LIMIT, flags=flags)


def _silu(x):
    h = 0.5 * x
    return h + h * jnp.tanh(h)


def _adaln_kernel(c_ref, w_ref, b_ref, o_ref):
    c = c_ref[...]
    s = _silu(c).astype(bf16)
    o_ref[...] = jnp.dot(s, w_ref[...].astype(bf16), preferred_element_type=f32) + b_ref[...]


def _adaln(cvecs, w, b):
    r, d = cvecs.shape
    n = w.shape[1]
    tn = 1024
    return pl.pallas_call(
        _adaln_kernel,
        out_shape=jax.ShapeDtypeStruct((r, n), f32),
        grid=(n // tn,),
        in_specs=[pl.BlockSpec((r, d), lambda j: (0, 0)),
                  pl.BlockSpec((d, tn), lambda j: (0, j)),
                  pl.BlockSpec((1, tn), lambda j: (0, j))],
        out_specs=pl.BlockSpec((r, tn), lambda j: (0, j)),
        compiler_params=_cparams(("arbitrary",)),
        name="adaln",
    )(cvecs, w, b)


def _inproj_kernel(segs, tsegs, n_w, has_rope, has_pre, x_ref, shift_ref, scale_ref, nw_ref, *rest):
    if has_pre:
        a_ref, b_ref, wo_ref, gate_ref = rest[:4]
        rest = rest[4:]
    w_refs, rest = rest[:n_w], rest[n_w:]
    if tsegs:
        wt_ref, rest = rest[0], rest[1:]
    if has_rope:
        cos_ref, sina_ref, sinb_ref = rest[:3]
        rest = rest[3:]
    if has_pre:
        x1_ref, rest = rest[0], rest[1:]
    outs = rest[:len(segs)]
    touts = rest[len(segs):len(segs) + len(tsegs)]
    rest = rest[len(segs) + len(tsegs):]
    slab_s = rest[0] if rest else None
    x = x_ref[...]
    if has_pre:
        ka = a_ref.shape[1]
        acc = jnp.dot(a_ref[...], wo_ref[:ka, :], preferred_element_type=f32)
        acc = acc + jnp.dot(b_ref[...], wo_ref[ka:, :], preferred_element_type=f32)
        x = x + gate_ref[...] * acc
        x1_ref[...] = x
    ms = jnp.mean(x * x, axis=-1, keepdims=True)
    h = (x * lax.rsqrt(ms + EPS)) * nw_ref[...]
    h = h * (1.0 + scale_ref[...]) + shift_ref[...]
    hb = h.astype(bf16)
    for (start, width, _), o_ref in zip(tsegs, touts):
        acc_t = lax.dot_general(wt_ref[start:start + width, :], hb, (((1,), (1,)), ((), ())),
                                preferred_element_type=f32)
        o_ref[...] = acc_t.astype(o_ref.dtype)
    products = {}
    for (widx, width, rope, _, chunked), o_ref in zip(segs, outs):
        if widx not in products:
            products[widx] = jnp.dot(hb, w_refs[widx][...], preferred_element_type=f32)
        acc = products[widx]
        if chunked:
            tm = acc.shape[0]
            hl = LANES // 2
            lo = lax.broadcasted_iota(jnp.int32, (1, LANES), 1) < hl
            for j in range(width // LANES):
                slab_s[j] = acc[:, j * LANES:(j + 1) * LANES]
            for j in range(width // LANES):
                rows = [slab_s[j, pl.ds(s, tm // S5_TC, stride=S5_TC), :] for s in range(S5_TC)]
                for p in range(S5_TC // 2):
                    a, b = rows[2 * p], rows[2 * p + 1]
                    h0 = jnp.where(lo, a, pltpu.roll(b, hl, axis=1))
                    h1 = jnp.where(lo, pltpu.roll(a, hl, axis=1), b)
                    o_ref[j, :, p * LANES:(p + 1) * LANES] = h0.astype(o_ref.dtype)
                    o_ref[j, :, S5_TC * hl + p * LANES:S5_TC * hl + (p + 1) * LANES] = h1.astype(o_ref.dtype)
            continue
        if rope is not None and not has_rope:
            acc = acc * rope
        elif rope is not None:
            rep = width // LANES
            cos = jnp.concatenate([cos_ref[...]] * rep, axis=1) * rope
            sina = jnp.concatenate([sina_ref[...]] * rep, axis=1) * rope
            sinb = jnp.concatenate([sinb_ref[...]] * rep, axis=1) * rope
            half = ATT_HEAD_DIM // 2
            up = pltpu.roll(acc, width - half, axis=1)
            dn = pltpu.roll(acc, half, axis=1)
            acc = acc * cos + up * sina + dn * sinb
        o_ref[...] = acc.astype(o_ref.dtype)


def _inproj(x, shift, scale, nw, w, segs, rows_per_mod, rope_tabs=None, wt=None, tsegs=(), pre=None):
    m, d = x.shape
    tm = min(ROW_TILE, rows_per_mod)
    per = rows_per_mod // tm
    nmod = shift.shape[0]
    mod_idx = (lambda i: (i // per, 0, 0)) if nmod > 1 else (lambda i: (0, 0, 0))
    in_specs = [pl.BlockSpec((tm, d), lambda i: (i, 0)),
                pl.BlockSpec((None, 1, d), mod_idx),
                pl.BlockSpec((None, 1, d), mod_idx),
                pl.BlockSpec((1, d), lambda i: (0, 0))]
    args = [x, shift, scale, nw]
    if pre is not None:
        a, b, w_out, gate = pre
        gate_idx = (lambda i: (i // per, 0, 0)) if gate.shape[0] > 1 else (lambda i: (0, 0, 0))
        in_specs += [pl.BlockSpec((tm, a.shape[1]), lambda i: (i, 0)),
                     pl.BlockSpec((tm, b.shape[1]), lambda i: (i, 0)),
                     pl.BlockSpec(w_out.shape, lambda i: (0, 0), pipeline_mode=pl.Buffered(1)),
                     pl.BlockSpec((None, 1, d), gate_idx)]
        args += [a, b, w_out, gate]
    in_specs += [pl.BlockSpec(wi.shape, lambda i: (0, 0), pipeline_mode=pl.Buffered(1)) for wi in w]
    args += list(w)
    if tsegs:
        in_specs.append(pl.BlockSpec(wt.shape, lambda i: (0, 0), pipeline_mode=pl.Buffered(1)))
        args.append(wt)
    if rope_tabs is not None:
        for t in rope_tabs:
            in_specs.append(pl.BlockSpec((tm, LANES), lambda i: (i % per, 0)))
            args.append(t)
    out_shape, out_specs = [], []
    if pre is not None:
        out_shape.append(jax.ShapeDtypeStruct((m, d), f32))
        out_specs.append(pl.BlockSpec((tm, d), lambda i: (i, 0)))
    for sg in segs:
        if sg[4]:
            out_shape.append(jax.ShapeDtypeStruct((sg[1] // LANES, m // S5_TC, S5_TC * LANES), sg[3]))
            out_specs.append(pl.BlockSpec((sg[1] // LANES, tm // S5_TC, S5_TC * LANES), lambda i: (0, i, 0)))
        else:
            out_shape.append(jax.ShapeDtypeStruct((m, sg[1]), sg[3]))
            out_specs.append(pl.BlockSpec((tm, sg[1]), lambda i: (i, 0)))
    for _, width, dtp in tsegs:
        out_shape.append(jax.ShapeDtypeStruct((width, m), dtp))
        out_specs.append(pl.BlockSpec((width, tm), lambda i: (0, i)))
    chunked_w = [sg[1] for sg in segs if sg[4]]
    scratch = [pltpu.VMEM((max(chunked_w) // LANES, tm, LANES), f32)] if chunked_w else []
    return pl.pallas_call(
        functools.partial(_inproj_kernel, tuple(segs), tuple(tsegs), len(w), rope_tabs is not None,
                          pre is not None),
        out_shape=out_shape,
        grid=(m // tm,),
        in_specs=in_specs,
        out_specs=out_specs,
        scratch_shapes=scratch,
        compiler_params=_cparams(("parallel",)),
        name="inproj",
    )(*args)


def _rope_tables(seq_len):
    rows = seq_len // GRID_W
    row = jnp.repeat(jnp.arange(rows, dtype=f32), GRID_W)
    col = jnp.tile(jnp.arange(GRID_W, dtype=f32), rows)
    n_freq = ATT_HEAD_DIM // 4
    inv = ROPE_THETA ** (-jnp.arange(n_freq, dtype=f32) / n_freq)
    ang = jnp.concatenate([row[:, None] * inv, col[:, None] * inv], axis=-1)
    cos, sin = jnp.cos(ang), jnp.sin(ang)
    zero = jnp.zeros_like(sin)
    cos_h = jnp.concatenate([cos, cos], axis=-1)
    sina_h = jnp.concatenate([-sin, zero], axis=-1)
    sinb_h = jnp.concatenate([zero, sin], axis=-1)
    two = lambda t: jnp.concatenate([t, t], axis=-1)
    return two(cos_h), two(sina_h), two(sinb_h)


SSD_PACK = 32


def _split3(x):
    hi = x.astype(bf16)
    r1 = x - hi.astype(f32)
    mid = r1.astype(bf16)
    lo = (r1 - mid.astype(f32)).astype(bf16)
    return hi, mid, lo


def _pack3(x):
    hi, mid, lo = _split3(x)
    lane = lax.broadcasted_iota(jnp.int32, x.shape, 1)
    mid_r = pltpu.roll(mid.astype(f32), SSD_PACK, axis=1)
    lo_r = pltpu.roll(lo.astype(f32), 2 * SSD_PACK, axis=1)
    packed = jnp.where(lane < SSD_PACK, hi.astype(f32),
                       jnp.where(lane < 2 * SSD_PACK, mid_r,
                                 jnp.where(lane < 3 * SSD_PACK, lo_r, 0.0)))
    return packed.astype(bf16)


def _ssd_selectors():
    k = np.arange(LANES)
    src = np.where(k < 3 * SSD_PACK, k % SSD_PACK, -1)
    col_blk = np.arange(SSD_PACK * SSD_CHUNK) // SSD_CHUNK
    sel_bc = (src[:, None] == col_blk[None, :])
    head = np.arange(SSD_INNER) // SSD_HEAD_DIM
    sel_f = (src[:, None] == head[None, :])
    sel_b = (src[:, None] == (head + SSD_HEADS)[None, :])
    tri3 = np.tile(np.tril(np.ones((SSD_CHUNK, SSD_CHUNK))), (1, 3))
    rows = np.arange(SSD_CHUNK)[:, None]
    cols = np.arange(SSD_CHUNK + 32)[None, :]
    half = SSD_CONV // 2
    shift = np.concatenate([cols == rows + 16 + d for d in range(-half, half + 1) if d != 0], axis=0)
    as_bf = lambda a: jnp.asarray(a.astype(np.float32), dtype=bf16)
    return as_bf(sel_bc), as_bf(sel_f), as_bf(sel_b), as_bf(tri3), as_bf(shift)


def _ssd_kernel(seq_len, xbc_ref, dt_ref, z_ref, cw_ref, cb_ref, dtb_ref, alog_ref, dskip_ref, nw_ref,
                selbc_ref, self_ref, selb_ref, tri3_ref, shift_ref, h0_ref, out_ref, hfin_ref,
                xs_s, bc_s, dt_s, y_s, hf_s, hb_s, win_s):
    q = SSD_CHUNK
    nc = seq_len // q
    halo = 16
    H, P, N = SSD_HEADS, SSD_HEAD_DIM, SSD_STATE
    gw = (H // SSD_GROUPS) * P
    a2_row = -jnp.exp(alog_ref[...]) * math.log2(math.e)

    def conv_chunk(c):
        r0 = pl.multiple_of(c * q, q)
        pstart = pl.multiple_of(jnp.maximum(r0 - halo, 0), halo)
        nstart = pl.multiple_of(jnp.minimum(r0 + q, seq_len - halo), halo)
        zero = jnp.zeros((), bf16)
        win_s[0:halo, :] = jnp.where(c > 0, xbc_ref[pl.ds(pstart, halo), :], zero)
        win_s[halo:halo + q, :] = xbc_ref[pl.ds(r0, q), :]
        win_s[halo + q:, :] = jnp.where(c < nc - 1, xbc_ref[pl.ds(nstart, halo), :], zero)
        taps = [k for k in range(SSD_CONV) if k != SSD_CONV // 2]
        cw = 2 * LANES
        for j in range(SSD_XBC // cw):
            cs = slice(j * cw, (j + 1) * cw)
            sh = jnp.dot(shift_ref[...], win_s[:, cs], preferred_element_type=f32)
            acc = cb_ref[:, cs] + win_s[halo:halo + q, cs].astype(f32) * cw_ref[SSD_CONV // 2:SSD_CONV // 2 + 1, cs]
            for n, k in enumerate(taps):
                acc = acc + sh[n * q:(n + 1) * q, :] * cw_ref[k:k + 1, cs]
            act = _silu(acc)
            if j < SSD_INNER // cw:
                xs_s[pl.ds(r0, q), cs] = act
            else:
                bc_s[pl.ds(r0, q), j * cw - SSD_INNER:(j + 1) * cw - SSD_INNER] = act.astype(bf16)
        dt_s[pl.ds(r0, q), :] = jax.nn.softplus(dt_ref[pl.ds(r0, q), :] + dtb_ref[...])

    conv_chunk(0)
    hf_s[...] = h0_ref[0]
    hb_s[...] = h0_ref[1]

    ri = lax.broadcasted_iota(jnp.int32, (q, q), 0)
    ci = lax.broadcasted_iota(jnp.int32, (q, q), 1)
    lower = ri >= ci
    upper = ci >= ri
    lane = lax.broadcasted_iota(jnp.int32, (q, LANES), 1)
    lo_half = lane < P

    def cumsums(dt):
        dta = dt * a2_row
        cf = jnp.dot(tri3_ref[...], jnp.concatenate(_split3(dta), axis=0), preferred_element_type=f32)
        rb = cf[q - 1:q, :] - cf + dta
        return cf, rb

    def load_chunk(r0):
        dt = dt_s[pl.ds(r0, q), :]
        xs = xs_s[pl.ds(r0, q), :]
        bcv = bc_s[pl.ds(r0, q), :]
        bmat = [bcv[:, g * N:(g + 1) * N] for g in range(SSD_GROUPS)]
        cmat = [bcv[:, SSD_BC + g * N:SSD_BC + (g + 1) * N] for g in range(SSD_GROUPS)]
        return dt, xs, bmat, cmat

    def inter_chunk(h_s, sel_ref, decay, weight, xs, bmat, cmat, dec_idx):
        ew = jnp.dot(jnp.concatenate([_pack3(decay), _pack3(weight)], axis=0), sel_ref[...],
                     preferred_element_type=f32)
        e_x, w_x = ew[:q], ew[q:]
        hb_ = h_s[...].astype(bf16)
        yoff = jnp.concatenate(
            [jnp.dot(cmat[g], hb_[:, g * gw:(g + 1) * gw], preferred_element_type=f32)
             for g in range(SSD_GROUPS)], axis=1)
        xw = (xs * w_x).astype(bf16)
        dec_row = e_x[dec_idx:dec_idx + 1, :]
        for g in range(SSD_GROUPS):
            gs = slice(g * gw, (g + 1) * gw)
            bt = jnp.transpose(bmat[g].astype(f32)).astype(bf16)
            upd = jnp.dot(bt, xw[:, gs], preferred_element_type=f32)
            h_s[:, gs] = h_s[:, gs] * dec_row[:, gs] + upd
        return yoff * e_x

    def finish(r0, y, xs):
        yy = y + xs * dskip_ref[...]
        zz = z_ref[pl.ds(r0, q), :].astype(f32)
        gated = yy * _silu(zz)
        ms = jnp.mean(gated * gated, axis=-1, keepdims=True)
        out_ref[pl.ds(r0, q), :] = (gated * lax.rsqrt(ms + EPS) * nw_ref[...]).astype(out_ref.dtype)

    def fwd_chunk(c, second_half):
        r0 = pl.multiple_of(c * q, q)
        dt, xs, bmat, cmat = load_chunk(r0)
        cf, rb = cumsums(dt)
        pcol = jnp.where(lane < H, cf, rb)
        bcast = jnp.dot(_pack3(pcol), selbc_ref[...], preferred_element_type=f32)
        prow = jnp.transpose(pcol - jnp.log2(dt))
        cbm = [lax.dot_general(cmat[g], bmat[g], (((1,), (1,)), ((), ())), preferred_element_type=f32)
               for g in range(SSD_GROUPS)]
        xsb = xs.astype(bf16)
        zero_b = jnp.zeros((), bf16)
        ypairs = []
        for k in range(H // 2):
            mats = []
            for h in (2 * k, 2 * k + 1):
                g = h // (H // SSD_GROUPS)
                hb_ = H + h
                segf = bcast[:, h * q:(h + 1) * q] - prow[h:h + 1, :]
                segb = bcast[:, hb_ * q:(hb_ + 1) * q] - prow[hb_:hb_ + 1, :]
                df = jnp.exp2(jnp.where(lower, segf, NEG_INF))
                db = jnp.exp2(jnp.where(upper, segb, NEG_INF))
                mats.append((cbm[g] * (df + db)).astype(bf16))
            xp = xsb[:, k * LANES:(k + 1) * LANES]
            xbd = jnp.concatenate([jnp.where(lo_half, xp, zero_b), jnp.where(lo_half, zero_b, xp)], axis=0)
            ypairs.append(jnp.dot(jnp.concatenate(mats, axis=1), xbd, preferred_element_type=f32))
        y = jnp.concatenate(ypairs, axis=1)
        wfa = jnp.exp2(cf[q - 1:q, :] - cf) * dt
        y = y + inter_chunk(hf_s, self_ref, jnp.exp2(cf), wfa, xs, bmat, cmat, q - 1)
        if second_half:
            finish(r0, y_s[pl.ds(r0, q), :] + y, xs)
        else:
            y_s[pl.ds(r0, q), :] = y

    def bwd_chunk(c, second_half):
        r0 = pl.multiple_of(c * q, q)
        dt, xs, bmat, cmat = load_chunk(r0)
        _, rb = cumsums(dt)
        wba = jnp.exp2(rb[0:1, :] - rb) * dt
        y = inter_chunk(hb_s, selb_ref, jnp.exp2(rb), wba, xs, bmat, cmat, 0)
        if second_half:
            finish(r0, y_s[pl.ds(r0, q), :] + y, xs)
        else:
            y_s[pl.ds(r0, q), :] = y

    half = nc // 2
    conv_chunk(nc - 1)

    def first_half(i, carry):
        fwd_chunk(i, False)
        bwd_chunk(nc - 1 - i, False)
        conv_chunk(i + 1)
        conv_chunk(nc - 2 - i)
        return carry

    def second_half(i, carry):
        fwd_chunk(i, True)
        bwd_chunk(nc - 1 - i, True)
        return carry

    lax.fori_loop(0, half - 1, first_half, 0)
    fwd_chunk(jnp.int32(half - 1), False)
    bwd_chunk(jnp.int32(half), False)
    lax.fori_loop(half, nc, second_half, 0)
    hfin_ref[0] = hf_s[...]
    hfin_ref[1] = hb_s[...]


def _ssd(xbc, dt, z, conv_w, conv_b, dtb, alog, dskip, nw, h0, bsz, seq_len):
    assert seq_len % (2 * SSD_CHUNK) == 0, "the two recurrences meet in the middle: even chunk count"
    one = pl.Buffered(1)
    seq = lambda w: pl.BlockSpec((seq_len, w), lambda b: (b, 0), pipeline_mode=one)
    const = lambda r, w: pl.BlockSpec((r, w), lambda b: (0, 0))
    st = pl.BlockSpec((None, 2, SSD_STATE, SSD_INNER), lambda b: (b, 0, 0, 0))
    sels = _ssd_selectors()
    return pl.pallas_call(
        functools.partial(_ssd_kernel, seq_len),
        out_shape=[jax.ShapeDtypeStruct((bsz * seq_len, SSD_INNER), bf16),
                   jax.ShapeDtypeStruct((bsz, 2, SSD_STATE, SSD_INNER), f32)],
        grid=(bsz,),
        in_specs=[pl.BlockSpec((seq_len, SSD_XBC), lambda b: (b, 0)), seq(LANES),
                  pl.BlockSpec((seq_len, SSD_INNER), lambda b: (b, 0)),
                  const(8, SSD_XBC), const(1, SSD_XBC), const(1, LANES), const(1, LANES),
                  const(1, SSD_INNER), const(1, SSD_INNER)]
                 + [const(*s.shape) for s in sels] + [st],
        out_specs=[seq(SSD_INNER), st],
        scratch_shapes=[pltpu.VMEM((seq_len, SSD_INNER), f32),
                        pltpu.VMEM((seq_len, 2 * SSD_BC), bf16),
                        pltpu.VMEM((seq_len, LANES), f32),
                        pltpu.VMEM((seq_len, SSD_INNER), f32),
                        pltpu.VMEM((SSD_STATE, SSD_INNER), f32),
                        pltpu.VMEM((SSD_STATE, SSD_INNER), f32),
                        pltpu.VMEM((SSD_CHUNK + 32, SSD_XBC), bf16)],
        compiler_params=_cparams(("parallel",)),
        name="ssd",
    )(xbc, dt, z, conv_w, conv_b, dtb, alog, dskip, nw, *sels, h0)


def _attn_kernel(n_blocks, nsub, local, q_ref, g_ref, k_ref, kc_ref, *rest):
    nv = nsub + 2 if local else 0
    v_refs = rest[:nv]
    vc_ref, sink_ref, o_ref, s_s = rest[nv:]
    t = ATT_BLOCK
    i0 = pl.program_id(1) * nsub
    rpk = ATT_HEADS // ATT_KV_HEADS
    lane = lax.broadcasted_iota(jnp.int32, (t, LANES), 1)
    lo_half = lane < ATT_HEAD_DIM
    zero_b = jnp.zeros((), bf16)
    kl = lax.broadcasted_iota(jnp.int32, (t, t), 0)
    ql = lax.broadcasted_iota(jnp.int32, (t, t), 1)
    for sub in range(nsub):
        i = i0 + sub
        qv = q_ref[sub * t:(sub + 1) * t, :]
        if local:
            p0 = pl.multiple_of(jnp.maximum(i - 1, 0) * t, t)
            c0 = pl.multiple_of(i * t, t)
            n0 = pl.multiple_of(jnp.minimum(i + 1, n_blocks - 1) * t, t)
            bias_prev = jnp.where((kl >= ql) & (i > 0), 0.0, NEG_INF)
            bias_next = jnp.where((kl <= ql) & (i < n_blocks - 1), 0.0, NEG_INF)
            bias_prev = jnp.concatenate([bias_prev] * rpk, axis=1)
            bias_next = jnp.concatenate([bias_next] * rpk, axis=1)
        for j in range(ATT_KV_HEADS):
            ls = slice(j * LANES, (j + 1) * LANES)
            if local:
                kk = jnp.concatenate([k_ref[pl.ds(p0, t), ls], k_ref[pl.ds(c0, t), ls],
                                      k_ref[pl.ds(n0, t), ls], kc_ref[:, ls]], axis=0)
            else:
                kk = kc_ref[:, ls]
            pieces = []
            for r in range(rpk):
                hq = j * rpk + r
                qp = qv[:, (hq // 2) * LANES:(hq // 2 + 1) * LANES]
                keep = lo_half if hq % 2 == 0 else jnp.logical_not(lo_half)
                pieces.append(jnp.where(keep, qp, zero_b))
            q4 = jnp.concatenate(pieces, axis=0)
            s = lax.dot_general(kk, q4, (((1,), (1,)), ((), ())), preferred_element_type=f32)
            if local:
                s = jnp.concatenate([s[:t] + bias_prev, s[t:2 * t], s[2 * t:3 * t] + bias_next, s[3 * t:]],
                                    axis=0)
            s_s[sub, j] = s
    for sub in range(nsub):
        outs = []
        for j in range(ATT_KV_HEADS):
            vs = slice(j * ATT_HEAD_DIM, (j + 1) * ATT_HEAD_DIM)
            if local:
                vvt = jnp.concatenate([v_refs[sub + k][vs, :] for k in range(3)] + [vc_ref[vs, :]], axis=1)
            else:
                vvt = vc_ref[vs, :]
            sk = jnp.concatenate([jnp.full((1, t), sink_ref[j * rpk + r] * LOG2E, f32) for r in range(rpk)],
                                 axis=1)
            s = s_s[sub, j]
            m = jnp.maximum(jnp.max(s, axis=0, keepdims=True), sk)
            p = jnp.exp2(s - m)
            den = jnp.sum(p, axis=0, keepdims=True) + jnp.exp2(sk - m)
            vvt = jnp.concatenate([vvt, vvt], axis=0)
            ot = jnp.dot(vvt, p.astype(bf16), preferred_element_type=f32) / den
            o4 = [jnp.transpose(ot[:, r * t:(r + 1) * t]) for r in range(rpk)]
            outs.append(jnp.where(lo_half, o4[0], o4[1]))
            outs.append(jnp.where(lo_half, o4[2], o4[3]))
        o = jnp.concatenate(outs, axis=1)
        gv = g_ref[sub * t:(sub + 1) * t, :].astype(f32)
        o_ref[sub * t:(sub + 1) * t, :] = (o * _silu(gv)).astype(o_ref.dtype)


def _attention(q, g, k, vt, kc, vct, sink, bsz, seq_len, local):
    t = ATT_BLOCK
    nb = seq_len // t
    nsub = max(s for s in range(1, ATT_SUB + 1) if nb % s == 0)
    n_ctx = kc.shape[0] // bsz
    kw = ATT_KV_HEADS * LANES
    vw = ATT_KVW
    blk = pl.BlockSpec((nsub * t, ATT_Q), lambda b, i: (b * (nb // nsub) + i, 0))
    full = lambda n: pl.BlockSpec((n, kw), lambda b, i: (b, 0))
    vblk = lambda off: pl.BlockSpec((vw, t), lambda b, i: (0, b * nb + jnp.clip(i * nsub + off, 0, nb - 1)))
    vspecs = [vblk(off) for off in range(-1, nsub + 1)] if local else []
    return pl.pallas_call(
        functools.partial(_attn_kernel, nb, nsub, local),
        out_shape=jax.ShapeDtypeStruct((bsz * seq_len, ATT_Q), bf16),
        grid=(bsz, nb // nsub),
        in_specs=[blk, blk, full(k.shape[0] // bsz), full(n_ctx)] + vspecs
                 + [pl.BlockSpec((vw, n_ctx), lambda b, i: (0, b)), pl.BlockSpec(memory_space=pltpu.SMEM)],
        out_specs=blk,
        scratch_shapes=[pltpu.VMEM((nsub, ATT_KV_HEADS, (3 * t if local else 0) + n_ctx,
                                    (ATT_HEADS // ATT_KV_HEADS) * t), f32)],
        compiler_params=_cparams(("parallel", "arbitrary")),
        name="attention",
    )(q, g, k, kc, *([vt] * len(vspecs)), vct, sink)


def _s5_disc_kernel(lre_ref, lim_ref, ls_ref, bre_ref, bim_ref, abre_ref, abim_ref, bbre_ref, bbim_ref):
    lam_re = lre_ref[...]
    lam_im = lim_ref[...]
    dt = jnp.exp(ls_ref[...])
    mag = jnp.exp(lam_re * dt)
    ab_re = mag * jnp.cos(lam_im * dt)
    ab_im = mag * jnp.sin(lam_im * dt)
    num_re, num_im = ab_re - 1.0, ab_im
    den = lam_re * lam_re + lam_im * lam_im
    coef_re = (num_re * lam_re + num_im * lam_im) / den
    coef_im = (num_im * lam_re - num_re * lam_im) / den
    b_re, b_im = bre_ref[...], bim_ref[...]
    abre_ref[...] = ab_re
    abim_ref[...] = ab_im
    bbre_ref[...] = coef_re * b_re - coef_im * b_im
    bbim_ref[...] = coef_re * b_im + coef_im * b_re


def _s5_discretise(lam_re, lam_im, log_step, b_re, b_im):
    g, n, cg = b_re.shape
    exp = lambda t: jnp.repeat(t.reshape(2 * g, n), cg, axis=1)
    ls = jnp.broadcast_to(log_step.reshape(2 * g, 1), (2 * g, n * cg))
    bb = lambda t: jnp.tile(t.reshape(g, n * cg), (2, 1))
    shp = jax.ShapeDtypeStruct((2 * g, n * cg), f32)
    ab_re, ab_im, bb_re, bb_im = pl.pallas_call(
        _s5_disc_kernel, out_shape=[shp] * 4, name="s5_disc",
    )(exp(lam_re), exp(lam_im), ls, bb(b_re), bb(b_im))
    first = lambda t: t.reshape(2, g, n, cg)[..., 0]
    full = lambda t: t.reshape(2, g, n, cg)
    return first(ab_re), first(ab_im), full(bb_re), full(bb_im)


def _cmul(ar, ai, br, bi):
    return ar * br - ai * bi, ar * bi + ai * br


def _s5_kernel(n_lat, n_ctx, *refs):
    tc = S5_TC
    xl, xc, arow_ref, acol_ref, bbd_ref, cbd_ref, y_ref, wyz_s, ws_s, sl_s, sc_s = refs
    hw = S5_HGRP * S5_STATE
    hl = LANES // 2
    xw = tc * hl
    nsl = hw // LANES
    nb = 8
    lo = lax.broadcasted_iota(jnp.int32, (1, LANES), 1) < hl

    def powers(re, im, n):
        out = [(jnp.ones_like(re), jnp.zeros_like(im))]
        for _ in range(n):
            out.append(_cmul(out[-1][0], out[-1][1], re, im))
        return out

    steps = lambda h, p: (2 * p, 2 * p + 1) if h == 0 else (2 * p + 1, 2 * p)

    prow = []
    for h in range(2):
        pr = [powers(arow_ref[h, 2 * d:2 * d + 1, :], arow_ref[h, 2 * d + 1:2 * d + 2, :], tc) for d in range(2)]
        prow.append(pr)
        for s in range(tc):
            for d, k in ((0, tc - 1 - s), (1, s)):
                wr, wi = _cmul(bbd_ref[2 * d, h], bbd_ref[2 * d + 1, h], *pr[d][k])
                ws_s[h, s * hl:(s + 1) * hl, d * 2 * hw:d * 2 * hw + hw] = wr.astype(bf16)
                ws_s[h, s * hl:(s + 1) * hl, d * 2 * hw + hw:(d + 1) * 2 * hw] = wi.astype(bf16)
        crhs = jnp.concatenate([cbd_ref[0, h], -cbd_ref[1, h]], axis=0).astype(bf16)
        kall = [jnp.dot(ws_s[h, :, d * 2 * hw:(d + 1) * 2 * hw], crhs, preferred_element_type=f32)
                for d in range(2)]
        kf = [kall[0][(tc - 1 - k) * hl:(tc - k) * hl] for k in range(tc)]
        kb = [kall[1][k * hl:(k + 1) * hl] for k in range(tc)]
        lag = lambda s, t: kf[t - s] if t > s else (kb[s - t] if t < s else kf[0] + kb[0])
        for s in range(tc):
            for p in range(tc // 2):
                t0, t1 = steps(h, p)
                wyz_s[h, s * hl:(s + 1) * hl, p * LANES:(p + 1) * LANES] = (
                    jnp.where(lo, lag(s, t0), lag(s, t1)).astype(bf16))
        for d in range(2):
            pc = powers(acol_ref[h, 2 * d], acol_ref[h, 2 * d + 1], tc)
            kk = (lambda t: t + 1) if d == 0 else (lambda t: tc - t)
            r0 = xw + d * 2 * hw
            for p in range(tc // 2):
                t0, t1 = steps(h, p)
                ar = jnp.where(lo, pc[kk(t0)][0], pc[kk(t1)][0])
                ai = jnp.where(lo, pc[kk(t0)][1], pc[kk(t1)][1])
                dre, dim_ = _cmul(cbd_ref[0, h], cbd_ref[1, h], ar, ai)
                wyz_s[h, r0:r0 + hw, p * LANES:(p + 1) * LANES] = dre.astype(bf16)
                wyz_s[h, r0 + hw:r0 + 2 * hw, p * LANES:(p + 1) * LANES] = (-dim_).astype(bf16)

    def rows_of(x_ref, b, n, h):
        return x_ref[b * n:(b + 1) * n, h * xw:(h + 1) * xw]

    def inject(x_ref, s_ref, n):
        for b in range(nb):
            for h in range(2):
                sb = jnp.dot(rows_of(x_ref, b, n, h), ws_s[h], preferred_element_type=f32)
                for k in range(4 * nsl):
                    s_ref[h * 4 * nsl + k, pl.ds(b, n, stride=nb), :] = sb[:, k * LANES:(k + 1) * LANES]

    at = [[[tuple(jnp.broadcast_to(p[:, k * LANES:(k + 1) * LANES], (nb, LANES)) for p in prow[h][d][tc])
            for k in range(nsl)] for d in range(2)] for h in range(2)]

    def scan(s_ref, n, init):
        def step(i, carry):
            idxs = [pl.ds(pl.multiple_of(i * nb, nb), nb), pl.ds(pl.multiple_of((n - 1 - i) * nb, nb), nb)]
            chains = [(h, d, k) for h in range(2) for d in range(2) for k in range(nsl)]
            slab = lambda h, d, k: h * 4 * nsl + d * 2 * nsl + k
            inj = [(s_ref[slab(h, d, k), idxs[d], :], s_ref[slab(h, d, k) + nsl, idxs[d], :])
                   for h, d, k in chains]
            new = []
            for c, (h, d, k) in enumerate(chains):
                hr, hi = carry[2 * c], carry[2 * c + 1]
                s_ref[slab(h, d, k), idxs[d], :] = hr
                s_ref[slab(h, d, k) + nsl, idxs[d], :] = hi
                ar, ai = at[h][d][k]
                new += [ar * hr - ai * hi + inj[c][0], ar * hi + ai * hr + inj[c][1]]
            return tuple(new)
        return lax.fori_loop(0, n, step, init, unroll=4)

    inject(xc, sc_s, n_ctx)
    h_ctx = scan(sc_s, n_ctx, tuple(jnp.zeros((nb, LANES), f32) for _ in range(8 * nsl)))
    inject(xl, sl_s, n_lat)
    scan(sl_s, n_lat, h_ctx)

    for b in range(nb):
        yh = []
        for h in range(2):
            hin = jnp.concatenate([sl_s[h * 4 * nsl + k, pl.ds(b, n_lat, stride=nb), :]
                                   for k in range(4 * nsl)], axis=1).astype(bf16)
            yh.append(jnp.dot(jnp.concatenate([rows_of(xl, b, n_lat, h), hin], axis=1), wyz_s[h],
                              preferred_element_type=f32))
        for p in range(tc // 2):
            y0, y1 = yh[0][:, p * LANES:(p + 1) * LANES], yh[1][:, p * LANES:(p + 1) * LANES]
            even = jnp.where(lo, y0, y1)
            odd = pltpu.roll(jnp.where(lo, y1, y0), hl, axis=1)
            y_ref[pl.ds(b * n_lat * tc + 2 * p, n_lat, stride=tc), :] = even
            y_ref[pl.ds(b * n_lat * tc + 2 * p + 1, n_lat, stride=tc), :] = odd


def _s5_mix(u, u_c, arow, acol, bbd, cbd, bsz, seq_len, n_ctx_tok):
    assert bsz == 8, "the chunk recurrence puts the batch on the 8 sublanes"
    tc = S5_TC
    n_lat, n_ctx = seq_len // tc, n_ctx_tok // tc
    nblk = S5_GROUPS // S5_GBLK
    hw = S5_HGRP * S5_STATE
    hl = LANES // 2
    nsl = hw // LANES
    one = pl.Buffered(1)
    xspec = lambda rows: pl.BlockSpec((None, rows, tc * LANES), lambda g: (g, 0, 0))
    return pl.pallas_call(
        functools.partial(_s5_kernel, n_lat, n_ctx),
        out_shape=jax.ShapeDtypeStruct((nblk, bsz * seq_len, LANES), f32),
        grid=(nblk,),
        in_specs=[xspec(bsz * n_lat), xspec(bsz * n_ctx),
                  pl.BlockSpec((2, 4, hw), lambda g: (g, 0, 0), pipeline_mode=one),
                  pl.BlockSpec((2, 4, hw, LANES), lambda g: (g, 0, 0, 0), pipeline_mode=one),
                  pl.BlockSpec((4, 2, hl, hw), lambda g: (0, g, 0, 0), pipeline_mode=one),
                  pl.BlockSpec((2, 2, hw, LANES), lambda g: (0, g, 0, 0), pipeline_mode=one)],
        out_specs=pl.BlockSpec((None, bsz * seq_len, LANES), lambda g: (g, 0, 0)),
        scratch_shapes=[pltpu.VMEM((2, tc * hl + 4 * hw, tc * hl), bf16),
                        pltpu.VMEM((2, tc * hl, 4 * hw), bf16),
                        pltpu.VMEM((8 * nsl, bsz * n_lat, LANES), f32),
                        pltpu.VMEM((8 * nsl, bsz * n_ctx, LANES), f32)],
        compiler_params=_cparams(("arbitrary",)),
        name="s5_mix",
    )(u, u_c, arow, acol, bbd, cbd)


def _s5_block_params(ab_re, ab_im, bb_re, bb_im, c_re, c_im):
    g, n, cg = S5_GROUPS, S5_STATE, S5_GROUP_CH
    nh = g // S5_HGRP
    hw = S5_HGRP * n
    arow = jnp.stack([t[d].reshape(nh, hw) for d in range(2) for t in (ab_re, ab_im)], axis=1)
    acol = jnp.broadcast_to(arow[..., None], (nh, 4, hw, LANES))

    def blockdiag(t, rows_per, cols_per):
        tiled = jnp.concatenate([t] * S5_HGRP, axis=-1)
        r = np.arange(t.shape[-2])[:, None] // rows_per
        c = np.arange(S5_HGRP * cols_per)[None, :] // cols_per
        return jnp.where(jnp.asarray(r == c), tiled, 0.0)

    bb = jnp.stack([t[d] for d in range(2) for t in (bb_re, bb_im)], axis=0)
    bb = jnp.swapaxes(bb.reshape(4, nh, S5_HGRP, n, cg), -1, -2).reshape(4, nh, S5_HGRP * cg, n)
    bbd = blockdiag(bb, cg, n)
    cc = jnp.swapaxes(jnp.stack([c_re, c_im], axis=0).reshape(2, nh, S5_HGRP, cg, n), -1, -2)
    cbd = blockdiag(cc.reshape(2, nh, hw, cg), n, cg)
    return arow, acol, bbd, jnp.concatenate([cbd, cbd], axis=-1)


def _s5_out_kernel(y_ref, u_ref, g_ref, x_ref, dskip_ref, gw_ref, gb_ref, w_ref, gate_ref,
                   fw_ref, o_ref):
    y = jnp.concatenate([y_ref[j] for j in range(y_ref.shape[0])], axis=1)
    y = y + dskip_ref[...] * u_ref[...].astype(f32)
    y = jax.nn.gelu(y)
    glu = jnp.dot(y.astype(bf16), gw_ref[...], preferred_element_type=f32) + gb_ref[...]
    y = y * jax.nn.sigmoid(glu)
    y = y * _silu(g_ref[...].astype(f32))
    x = x_ref[...] + gate_ref[...] * jnp.dot(y.astype(bf16), w_ref[...], preferred_element_type=f32)
    ms = jnp.mean(x * x, axis=-1, keepdims=True)
    o_ref[...] = x * lax.rsqrt(ms + EPS) * fw_ref[...]


def _s5_out(y, u, g, x, dskip, gw, gb, w, gate, fw, rows_per_mod):
    m, d = x.shape
    tm = min(ROW_TILE, rows_per_mod)
    per = rows_per_mod // tm
    row = lambda: pl.BlockSpec((tm, d), lambda i: (i, 0))
    vec = lambda: pl.BlockSpec((1, d), lambda i: (0, 0))
    mat = lambda: pl.BlockSpec((d, d), lambda i: (0, 0), pipeline_mode=pl.Buffered(1))
    return pl.pallas_call(
        _s5_out_kernel,
        out_shape=jax.ShapeDtypeStruct((m, d), f32),
        grid=(m // tm,),
        in_specs=[pl.BlockSpec((y.shape[0], tm, LANES), lambda i: (0, i, 0)),
                  row(), row(), row(), vec(), mat(), vec(), mat(),
                  pl.BlockSpec((None, 1, d), lambda i: (i // per, 0, 0)), vec()],
        out_specs=row(),
        compiler_params=_cparams(("parallel",)),
        name="s5_out",
    )(y, u, g, x, dskip, gw, gb, w, gate, fw)


def _even_weights(w_in):
    o = 0
    z = w_in[:, o:o + SSD_INNER]; o += SSD_INNER
    xbc = w_in[:, o:o + SSD_XBC]; o += SSD_XBC
    dt = w_in[:, o:o + 2 * SSD_HEADS]; o += 2 * SSD_HEADS
    q = w_in[:, o:o + ATT_Q]; o += ATT_Q
    k = w_in[:, o:o + ATT_KVW]; o += ATT_KVW
    v = w_in[:, o:o + ATT_KVW]; o += ATT_KVW
    g = w_in[:, o:o + ATT_Q]
    d = w_in.shape[0]
    dup = lambda t: jnp.concatenate([t.reshape(d, ATT_KV_HEADS, 1, ATT_HEAD_DIM)] * 2, axis=2).reshape(d, -1)
    dtp = jnp.pad(dt, ((0, 0), (0, LANES - 2 * SSD_HEADS)))
    cast = lambda t: t.astype(bf16)
    return [cast(t) for t in (z, xbc, q, dup(k), g, dtp)], cast(v.T)


def _even_segs(rope):
    scale = ATT_HEAD_DIM ** -0.5 * LOG2E
    widths = [(SSD_INNER, None, bf16), (SSD_XBC, None, bf16),
              (ATT_Q, scale, bf16),
              (ATT_KV_HEADS * LANES, 1.0 if rope else None, bf16),
              (ATT_Q, None, bf16), (LANES, None, f32)]
    return [(i, w, r, dtp, False) for i, (w, r, dtp) in enumerate(widths)]


def _pad_lanes(v, n=LANES):
    v = v.reshape(1, -1)
    return jnp.pad(v, ((0, 0), (0, n - v.shape[1])))


def kernel(x, c, ctx, c_ctx, e_norm_w, e_ada_w, e_ada_b, e_w_in, e_conv_w, e_conv_b, e_dt_bias,
           e_a_log, e_d_skip, e_ssd_norm_w, e_sink, e_w_out, o_norm_w, o_ada_w, o_ada_b, o_w_in,
           o_lam_re, o_lam_im, o_log_step, o_b_re, o_b_im, o_c_re, o_c_im, o_d_skip, o_glu_w,
           o_glu_b, o_w_out, final_norm_w):
    bsz, seq_len, d = x.shape
    n_ctx = ctx.shape[1]
    assert e_w_in.shape[0] == 1 and o_w_in.shape[0] == 1, "one even (SSD + attention) and one odd (S5) layer"
    xf = x.reshape(bsz * seq_len, d)
    xcf = ctx.reshape(bsz * n_ctx, d)

    cvecs = jnp.concatenate([c, c_ctx[None, :], jnp.zeros((16 - bsz - 1, d), f32)], axis=0)

    def modulation(ada_w, ada_b):
        mod = _adaln(cvecs, ada_w, ada_b.reshape(1, -1))
        parts = [mod[:, k * d:(k + 1) * d] for k in range(3)]
        lat = [p[:bsz].reshape(bsz, 1, d) for p in parts]
        cx = [p[bsz:bsz + 1].reshape(1, 1, d) for p in parts]
        return lat, cx

    (shift, scale, gate), (shift_c, scale_c, gate_c) = modulation(e_ada_w[0], e_ada_b[0])
    w_in, w_vt = _even_weights(e_w_in[0])
    nw = e_norm_w[0].reshape(1, d)
    tabs = _rope_tables(seq_len)
    vseg = [(0, ATT_KVW, bf16)]
    z, xbc, q, k, g, dt, vt = _inproj(xf, shift, scale, nw, w_in, _even_segs(True), seq_len, tabs, w_vt, vseg)
    z_c, xbc_c, q_c, k_c, g_c, dt_c, vt_c = _inproj(xcf, shift_c, scale_c, nw, w_in, _even_segs(False), n_ctx,
                                                    None, w_vt, vseg)

    conv_w = jnp.pad(e_conv_w[0], ((0, 8 - SSD_CONV), (0, 0)))
    conv_b = e_conv_b[0].reshape(1, -1)
    dtb = _pad_lanes(e_dt_bias[0])
    alog = _pad_lanes(e_a_log[0])
    dskip = jnp.repeat(e_d_skip[0], SSD_HEAD_DIM).reshape(1, -1)
    snw = e_ssd_norm_w[0].reshape(1, -1)
    h0 = jnp.zeros((bsz, 2, SSD_STATE, SSD_INNER), f32)
    ssd_c, hfin = _ssd(xbc_c, dt_c, z_c, conv_w, conv_b, dtb, alog, dskip, snw, h0, bsz, n_ctx)
    ssd_o, _ = _ssd(xbc, dt, z, conv_w, conv_b, dtb, alog, dskip, snw, hfin, bsz, seq_len)

    sink = e_sink[0]
    att = _attention(q, g, k, vt, k_c, vt_c, sink, bsz, seq_len, True)
    att_c = _attention(q_c, g_c, k_c, vt_c, k_c, vt_c, sink, bsz, n_ctx, False)
    w_out = e_w_out[0].astype(bf16)
    even_gate, even_gate_c = gate, gate_c

    (shift, scale, gate), (shift_c, scale_c, _) = modulation(o_ada_w[0], o_ada_b[0])
    w_u = o_w_in[0][:, :S5_WIDTH].astype(bf16)
    w_g = o_w_in[0][:, S5_WIDTH:].astype(bf16)
    nw = o_norm_w[0].reshape(1, d)
    x1, u, u_ch, g2 = _inproj(xf, shift, scale, nw, [w_u, w_g],
                              [(0, S5_WIDTH, None, bf16, False), (0, S5_WIDTH, None, bf16, True),
                               (1, S5_WIDTH, None, bf16, False)], seq_len,
                              pre=(ssd_o, att, w_out, even_gate))
    _, uc_ch = _inproj(xcf, shift_c, scale_c, nw, [w_u], [(0, S5_WIDTH, None, bf16, True)], n_ctx,
                       pre=(ssd_c, att_c, w_out, even_gate_c))

    ab_re, ab_im, bb_re, bb_im = _s5_discretise(o_lam_re[0], o_lam_im[0], o_log_step[0], o_b_re[0], o_b_im[0])
    arow, acol, bbd, cbd = _s5_block_params(ab_re, ab_im, bb_re, bb_im, o_c_re[0], o_c_im[0])
    y = _s5_mix(u_ch, uc_ch, arow, acol, bbd, cbd, bsz, seq_len, n_ctx)
    out = _s5_out(y, u, g2, x1, o_d_skip[0].reshape(1, -1),
                  o_glu_w[0].astype(bf16), o_glu_b[0].reshape(1, -1), o_w_out[0].astype(bf16), gate,
                  final_norm_w.reshape(1, -1), seq_len)
    return out.reshape(bsz, seq_len, d)
```

```python
import functools
import math

import jax
import jax.numpy as jnp
import numpy as np
from jax import lax
from jax.experimental import pallas as pl
from jax.experimental.pallas import tpu as pltpu

f32 = jnp.float32
bf16 = jnp.bfloat16

GRID_W = 64
EPS = 1e-6
NEG_INF = -1e30

SSD_HEADS = 16
SSD_HEAD_DIM = 64
SSD_GROUPS = 2
SSD_STATE = 128
SSD_CONV = 5
SSD_CHUNK = 128
SSD_INNER = SSD_HEADS * SSD_HEAD_DIM
SSD_BC = SSD_GROUPS * SSD_STATE
SSD_XBC = SSD_INNER + 2 * SSD_BC
ATT_HEADS = 16
ATT_KV_HEADS = 4
ATT_HEAD_DIM = 64
ATT_BLOCK = 128
ROPE_THETA = 10000.0
ATT_Q = ATT_HEADS * ATT_HEAD_DIM
ATT_KVW = ATT_KV_HEADS * ATT_HEAD_DIM
S5_WIDTH = 1024
S5_GROUP_CH = 16
S5_GROUPS = S5_WIDTH // S5_GROUP_CH
S5_STATE = 64

LOG2E = math.log2(math.e)
LANES = 128
ROW_TILE = 1024
ATT_SUB = 4
S5_GBLK = 8
S5_TC = 8
S5_HGRP = 2
VMEM_LIMIT = 56 * 1024 * 1024


def _cparams(sem, flags=None):
    return pltpu.CompilerParams(dimension_semantics=sem, vmem_limit_bytes=VMEM_LIMIT, flags=flags)


def _silu(x):
    h = 0.5 * x
    return h + h * jnp.tanh(h)


def _adaln_kernel(c_ref, w_ref, b_ref, o_ref):
    c = c_ref[...]
    s = _silu(c).astype(bf16)
    o_ref[...] = jnp.dot(s, w_ref[...].astype(bf16), preferred_element_type=f32) + b_ref[...]


def _adaln(cvecs, w, b):
    r, d = cvecs.shape
    n = w.shape[1]
    tn = 1024
    return pl.pallas_call(
        _adaln_kernel,
        out_shape=jax.ShapeDtypeStruct((r, n), f32),
        grid=(n // tn,),
        in_specs=[pl.BlockSpec((r, d), lambda j: (0, 0)),
                  pl.BlockSpec((d, tn), lambda j: (0, j)),
                  pl.BlockSpec((1, tn), lambda j: (0, j))],
        out_specs=pl.BlockSpec((r, tn), lambda j: (0, j)),
        compiler_params=_cparams(("arbitrary",)),
        name="adaln",
    )(cvecs, w, b)


def _inproj_kernel(segs, tsegs, n_w, has_rope, has_pre, x_ref, shift_ref, scale_ref, nw_ref, *rest):
    if has_pre:
        a_ref, b_ref, wo_ref, gate_ref = rest[:4]
        rest = rest[4:]
    w_refs, rest = rest[:n_w], rest[n_w:]
    if tsegs:
        wt_ref, rest = rest[0], rest[1:]
    if has_rope:
        cos_ref, sina_ref, sinb_ref = rest[:3]
        rest = rest[3:]
    if has_pre:
        x1_ref, rest = rest[0], rest[1:]
    outs = rest[:len(segs)]
    touts = rest[len(segs):len(segs) + len(tsegs)]
    rest = rest[len(segs) + len(tsegs):]
    slab_s = rest[0] if rest else None
    x = x_ref[...]
    if has_pre:
        ka = a_ref.shape[1]
        acc = jnp.dot(a_ref[...], wo_ref[:ka, :], preferred_element_type=f32)
        acc = acc + jnp.dot(b_ref[...], wo_ref[ka:, :], preferred_element_type=f32)
        x = x + gate_ref[...] * acc
        x1_ref[...] = x
    ms = jnp.mean(x * x, axis=-1, keepdims=True)
    h = (x * lax.rsqrt(ms + EPS)) * nw_ref[...]
    h = h * (1.0 + scale_ref[...]) + shift_ref[...]
    hb = h.astype(bf16)
    for (start, width, _), o_ref in zip(tsegs, touts):
        acc_t = lax.dot_general(wt_ref[start:start + width, :], hb, (((1,), (1,)), ((), ())),
                                preferred_element_type=f32)
        o_ref[...] = acc_t.astype(o_ref.dtype)
    products = {}
    for (widx, width, rope, _, chunked), o_ref in zip(segs, outs):
        if widx not in products:
            products[widx] = jnp.dot(hb, w_refs[widx][...], preferred_element_type=f32)
        acc = products[widx]
        if chunked:
            tm = acc.shape[0]
            hl = S5_HGRP * S5_GROUP_CH
            per = LANES // hl
            pos = lax.broadcasted_iota(jnp.int32, (1, LANES), 1) // hl
            for j in range(width // LANES):
                slab_s[j] = acc[:, j * LANES:(j + 1) * LANES]
            for j in range(width // LANES):
                rows = [slab_s[j, pl.ds(s, tm // S5_TC, stride=S5_TC), :] for s in range(S5_TC)]
                for p in range(S5_TC // per):
                    for h in range(per):
                        out = None
                        for i in range(per):
                            piece = rows[p * per + i]
                            if i != h:
                                piece = pltpu.roll(piece, ((i - h) * hl) % LANES, axis=1)
                            out = piece if out is None else jnp.where(pos == i, piece, out)
                        c0 = h * S5_TC * hl + p * LANES
                        o_ref[j, :, c0:c0 + LANES] = out.astype(o_ref.dtype)
            continue
        if rope is not None and not has_rope:
            acc = acc * rope
        elif rope is not None:
            rep = width // LANES
            cos = jnp.concatenate([cos_ref[...]] * rep, axis=1) * rope
            sina = jnp.concatenate([sina_ref[...]] * rep, axis=1) * rope
            sinb = jnp.concatenate([sinb_ref[...]] * rep, axis=1) * rope
            half = ATT_HEAD_DIM // 2
            up = pltpu.roll(acc, width - half, axis=1)
            dn = pltpu.roll(acc, half, axis=1)
            acc = acc * cos + up * sina + dn * sinb
        o_ref[...] = acc.astype(o_ref.dtype)


def _inproj(x, shift, scale, nw, w, segs, rows_per_mod, rope_tabs=None, wt=None, tsegs=(), pre=None):
    m, d = x.shape
    tm = min(ROW_TILE, rows_per_mod)
    per = rows_per_mod // tm
    nmod = shift.shape[0]
    mod_idx = (lambda i: (i // per, 0, 0)) if nmod > 1 else (lambda i: (0, 0, 0))
    in_specs = [pl.BlockSpec((tm, d), lambda i: (i, 0)),
                pl.BlockSpec((None, 1, d), mod_idx),
                pl.BlockSpec((None, 1, d), mod_idx),
                pl.BlockSpec((1, d), lambda i: (0, 0))]
    args = [x, shift, scale, nw]
    if pre is not None:
        a, b, w_out, gate = pre
        gate_idx = (lambda i: (i // per, 0, 0)) if gate.shape[0] > 1 else (lambda i: (0, 0, 0))
        in_specs += [pl.BlockSpec((tm, a.shape[1]), lambda i: (i, 0)),
                     pl.BlockSpec((tm, b.shape[1]), lambda i: (i, 0)),
                     pl.BlockSpec(w_out.shape, lambda i: (0, 0), pipeline_mode=pl.Buffered(1)),
                     pl.BlockSpec((None, 1, d), gate_idx)]
        args += [a, b, w_out, gate]
    in_specs += [pl.BlockSpec(wi.shape, lambda i: (0, 0), pipeline_mode=pl.Buffered(1)) for wi in w]
    args += list(w)
    if tsegs:
        in_specs.append(pl.BlockSpec(wt.shape, lambda i: (0, 0), pipeline_mode=pl.Buffered(1)))
        args.append(wt)
    if rope_tabs is not None:
        for t in rope_tabs:
            in_specs.append(pl.BlockSpec((tm, LANES), lambda i: (i % per, 0)))
            args.append(t)
    out_shape, out_specs = [], []
    if pre is not None:
        out_shape.append(jax.ShapeDtypeStruct((m, d), f32))
        out_specs.append(pl.BlockSpec((tm, d), lambda i: (i, 0)))
    for sg in segs:
        if sg[4]:
            out_shape.append(jax.ShapeDtypeStruct((sg[1] // LANES, m // S5_TC, S5_TC * LANES), sg[3]))
            out_specs.append(pl.BlockSpec((sg[1] // LANES, tm // S5_TC, S5_TC * LANES), lambda i: (0, i, 0)))
        else:
            out_shape.append(jax.ShapeDtypeStruct((m, sg[1]), sg[3]))
            out_specs.append(pl.BlockSpec((tm, sg[1]), lambda i: (i, 0)))
    for _, width, dtp in tsegs:
        out_shape.append(jax.ShapeDtypeStruct((width, m), dtp))
        out_specs.append(pl.BlockSpec((width, tm), lambda i: (0, i)))
    chunked_w = [sg[1] for sg in segs if sg[4]]
    scratch = [pltpu.VMEM((max(chunked_w) // LANES, tm, LANES), f32)] if chunked_w else []
    return pl.pallas_call(
        functools.partial(_inproj_kernel, tuple(segs), tuple(tsegs), len(w), rope_tabs is not None,
                          pre is not None),
        out_shape=out_shape,
        grid=(m // tm,),
        in_specs=in_specs,
        out_specs=out_specs,
        scratch_shapes=scratch,
        compiler_params=_cparams(("parallel",)),
        name="inproj",
    )(*args)


def _rope_tables(seq_len):
    rows = seq_len // GRID_W
    row = jnp.repeat(jnp.arange(rows, dtype=f32), GRID_W)
    col = jnp.tile(jnp.arange(GRID_W, dtype=f32), rows)
    n_freq = ATT_HEAD_DIM // 4
    inv = ROPE_THETA ** (-jnp.arange(n_freq, dtype=f32) / n_freq)
    ang = jnp.concatenate([row[:, None] * inv, col[:, None] * inv], axis=-1)
    cos, sin = jnp.cos(ang), jnp.sin(ang)
    zero = jnp.zeros_like(sin)
    cos_h = jnp.concatenate([cos, cos], axis=-1)
    sina_h = jnp.concatenate([-sin, zero], axis=-1)
    sinb_h = jnp.concatenate([zero, sin], axis=-1)
    two = lambda t: jnp.concatenate([t, t], axis=-1)
    return two(cos_h), two(sina_h), two(sinb_h)


SSD_PACK = 32


def _split3(x):
    hi = x.astype(bf16)
    r1 = x - hi.astype(f32)
    mid = r1.astype(bf16)
    lo = (r1 - mid.astype(f32)).astype(bf16)
    return hi, mid, lo


def _pack3(x):
    hi, mid, lo = _split3(x)
    lane = lax.broadcasted_iota(jnp.int32, x.shape, 1)
    mid_r = pltpu.roll(mid.astype(f32), SSD_PACK, axis=1)
    lo_r = pltpu.roll(lo.astype(f32), 2 * SSD_PACK, axis=1)
    packed = jnp.where(lane < SSD_PACK, hi.astype(f32),
                       jnp.where(lane < 2 * SSD_PACK, mid_r,
                                 jnp.where(lane < 3 * SSD_PACK, lo_r, 0.0)))
    return packed.astype(bf16)


def _ssd_selectors():
    k = np.arange(LANES)
    src = np.where(k < 3 * SSD_PACK, k % SSD_PACK, -1)
    col_blk = np.arange(SSD_PACK * SSD_CHUNK) // SSD_CHUNK
    sel_bc = (src[:, None] == col_blk[None, :])
    head = np.arange(SSD_INNER) // SSD_HEAD_DIM
    sel_f = (src[:, None] == head[None, :])
    sel_b = (src[:, None] == (head + SSD_HEADS)[None, :])
    tri3 = np.tile(np.tril(np.ones((SSD_CHUNK, SSD_CHUNK))), (1, 3))
    rows = np.arange(SSD_CHUNK)[:, None]
    cols = np.arange(SSD_CHUNK + 32)[None, :]
    half = SSD_CONV // 2
    shift = np.concatenate([cols == rows + 16 + d for d in range(-half, half + 1) if d != 0], axis=0)
    as_bf = lambda a: jnp.asarray(a.astype(np.float32), dtype=bf16)
    return as_bf(sel_bc), as_bf(sel_f), as_bf(sel_b), as_bf(tri3), as_bf(shift)


def _ssd_kernel(seq_len, xbc_ref, dt_ref, z_ref, cw_ref, cb_ref, dtb_ref, alog_ref, dskip_ref, nw_ref,
                selbc_ref, self_ref, selb_ref, tri3_ref, shift_ref, h0_ref, out_ref, hfin_ref,
                xs_s, bc_s, dt_s, y_s, hf_s, hb_s, win_s):
    q = SSD_CHUNK
    nc = seq_len // q
    halo = 16
    H, P, N = SSD_HEADS, SSD_HEAD_DIM, SSD_STATE
    gw = (H // SSD_GROUPS) * P
    a2_row = -jnp.exp(alog_ref[...]) * math.log2(math.e)

    def conv_chunk(c):
        r0 = pl.multiple_of(c * q, q)
        pstart = pl.multiple_of(jnp.maximum(r0 - halo, 0), halo)
        nstart = pl.multiple_of(jnp.minimum(r0 + q, seq_len - halo), halo)
        zero = jnp.zeros((), bf16)
        win_s[0:halo, :] = jnp.where(c > 0, xbc_ref[pl.ds(pstart, halo), :], zero)
        win_s[halo:halo + q, :] = xbc_ref[pl.ds(r0, q), :]
        win_s[halo + q:, :] = jnp.where(c < nc - 1, xbc_ref[pl.ds(nstart, halo), :], zero)
        taps = [k for k in range(SSD_CONV) if k != SSD_CONV // 2]
        cw = 2 * LANES
        for j in range(SSD_XBC // cw):
            cs = slice(j * cw, (j + 1) * cw)
            sh = jnp.dot(shift_ref[...], win_s[:, cs], preferred_element_type=f32)
            acc = cb_ref[:, cs] + win_s[halo:halo + q, cs].astype(f32) * cw_ref[SSD_CONV // 2:SSD_CONV // 2 + 1, cs]
            for n, k in enumerate(taps):
                acc = acc + sh[n * q:(n + 1) * q, :] * cw_ref[k:k + 1, cs]
            act = _silu(acc)
            if j < SSD_INNER // cw:
                xs_s[pl.ds(r0, q), cs] = act
            else:
                bc_s[pl.ds(r0, q), j * cw - SSD_INNER:(j + 1) * cw - SSD_INNER] = act.astype(bf16)
        dt_s[pl.ds(r0, q), :] = jax.nn.softplus(dt_ref[pl.ds(r0, q), :] + dtb_ref[...])

    conv_chunk(0)
    hf_s[...] = h0_ref[0]
    hb_s[...] = h0_ref[1]

    ri = lax.broadcasted_iota(jnp.int32, (q, q), 0)
    ci = lax.broadcasted_iota(jnp.int32, (q, q), 1)
    lower = ri >= ci
    upper = ci >= ri
    lane = lax.broadcasted_iota(jnp.int32, (q, LANES), 1)
    lo_half = lane < P

    def cumsums(dt):
        dta = dt * a2_row
        cf = jnp.dot(tri3_ref[...], jnp.concatenate(_split3(dta), axis=0), preferred_element_type=f32)
        rb = cf[q - 1:q, :] - cf + dta
        return cf, rb

    def load_chunk(r0):
        dt = dt_s[pl.ds(r0, q), :]
        xs = xs_s[pl.ds(r0, q), :]
        bcv = bc_s[pl.ds(r0, q), :]
        bmat = [bcv[:, g * N:(g + 1) * N] for g in range(SSD_GROUPS)]
        cmat = [bcv[:, SSD_BC + g * N:SSD_BC + (g + 1) * N] for g in range(SSD_GROUPS)]
        return dt, xs, bmat, cmat

    def inter_chunk(h_s, sel_ref, decay, weight, xs, bmat, cmat, dec_idx):
        ew = jnp.dot(jnp.concatenate([_pack3(decay), _pack3(weight)], axis=0), sel_ref[...],
                     preferred_element_type=f32)
        e_x, w_x = ew[:q], ew[q:]
        hb_ = h_s[...].astype(bf16)
        yoff = jnp.concatenate(
            [jnp.dot(cmat[g], hb_[:, g * gw:(g + 1) * gw], preferred_element_type=f32)
             for g in range(SSD_GROUPS)], axis=1)
        xw = (xs * w_x).astype(bf16)
        dec_row = e_x[dec_idx:dec_idx + 1, :]
        for g in range(SSD_GROUPS):
            gs = slice(g * gw, (g + 1) * gw)
            bt = jnp.transpose(bmat[g].astype(f32)).astype(bf16)
            upd = jnp.dot(bt, xw[:, gs], preferred_element_type=f32)
            h_s[:, gs] = h_s[:, gs] * dec_row[:, gs] + upd
        return yoff * e_x

    def finish(r0, y, xs):
        yy = y + xs * dskip_ref[...]
        zz = z_ref[pl.ds(r0, q), :].astype(f32)
        gated = yy * _silu(zz)
        ms = jnp.mean(gated * gated, axis=-1, keepdims=True)
        out_ref[pl.ds(r0, q), :] = (gated * lax.rsqrt(ms + EPS) * nw_ref[...]).astype(out_ref.dtype)

    def fwd_chunk(c, second_half):
        r0 = pl.multiple_of(c * q, q)
        dt, xs, bmat, cmat = load_chunk(r0)
        cf, rb = cumsums(dt)
        pcol = jnp.where(lane < H, cf, rb)
        bcast = jnp.dot(_pack3(pcol), selbc_ref[...], preferred_element_type=f32)
        prow = jnp.transpose(pcol - jnp.log2(dt))
        cbm = [lax.dot_general(cmat[g], bmat[g], (((1,), (1,)), ((), ())), preferred_element_type=f32)
               for g in range(SSD_GROUPS)]
        xsb = xs.astype(bf16)
        zero_b = jnp.zeros((), bf16)
        ypairs = []
        for k in range(H // 2):
            mats = []
            for h in (2 * k, 2 * k + 1):
                g = h // (H // SSD_GROUPS)
                hb_ = H + h
                segf = bcast[:, h * q:(h + 1) * q] - prow[h:h + 1, :]
                segb = bcast[:, hb_ * q:(hb_ + 1) * q] - prow[hb_:hb_ + 1, :]
                df = jnp.exp2(jnp.where(lower, segf, NEG_INF))
                db = jnp.exp2(jnp.where(upper, segb, NEG_INF))
                mats.append((cbm[g] * (df + db)).astype(bf16))
            xp = xsb[:, k * LANES:(k + 1) * LANES]
            xbd = jnp.concatenate([jnp.where(lo_half, xp, zero_b), jnp.where(lo_half, zero_b, xp)], axis=0)
            ypairs.append(jnp.dot(jnp.concatenate(mats, axis=1), xbd, preferred_element_type=f32))
        y = jnp.concatenate(ypairs, axis=1)
        wfa = jnp.exp2(cf[q - 1:q, :] - cf) * dt
        y = y + inter_chunk(hf_s, self_ref, jnp.exp2(cf), wfa, xs, bmat, cmat, q - 1)
        if second_half:
            finish(r0, y_s[pl.ds(r0, q), :] + y, xs)
        else:
            y_s[pl.ds(r0, q), :] = y

    def bwd_chunk(c, second_half):
        r0 = pl.multiple_of(c * q, q)
        dt, xs, bmat, cmat = load_chunk(r0)
        _, rb = cumsums(dt)
        wba = jnp.exp2(rb[0:1, :] - rb) * dt
        y = inter_chunk(hb_s, selb_ref, jnp.exp2(rb), wba, xs, bmat, cmat, 0)
        if second_half:
            finish(r0, y_s[pl.ds(r0, q), :] + y, xs)
        else:
            y_s[pl.ds(r0, q), :] = y

    half = nc // 2
    conv_chunk(nc - 1)

    def first_half(i, carry):
        fwd_chunk(i, False)
        bwd_chunk(nc - 1 - i, False)
        conv_chunk(i + 1)
        conv_chunk(nc - 2 - i)
        return carry

    def second_half(i, carry):
        fwd_chunk(i, True)
        bwd_chunk(nc - 1 - i, True)
        return carry

    lax.fori_loop(0, half - 1, first_half, 0)
    fwd_chunk(jnp.int32(half - 1), False)
    bwd_chunk(jnp.int32(half), False)
    lax.fori_loop(half, nc, second_half, 0)
    hfin_ref[0] = hf_s[...]
    hfin_ref[1] = hb_s[...]


def _ssd(xbc, dt, z, conv_w, conv_b, dtb, alog, dskip, nw, h0, bsz, seq_len):
    assert seq_len % (2 * SSD_CHUNK) == 0, "the two recurrences meet in the middle: even chunk count"
    one = pl.Buffered(1)
    seq = lambda w: pl.BlockSpec((seq_len, w), lambda b: (b, 0), pipeline_mode=one)
    const = lambda r, w: pl.BlockSpec((r, w), lambda b: (0, 0))
    st = pl.BlockSpec((None, 2, SSD_STATE, SSD_INNER), lambda b: (b, 0, 0, 0))
    sels = _ssd_selectors()
    return pl.pallas_call(
        functools.partial(_ssd_kernel, seq_len),
        out_shape=[jax.ShapeDtypeStruct((bsz * seq_len, SSD_INNER), bf16),
                   jax.ShapeDtypeStruct((bsz, 2, SSD_STATE, SSD_INNER), f32)],
        grid=(bsz,),
        in_specs=[pl.BlockSpec((seq_len, SSD_XBC), lambda b: (b, 0)), seq(LANES),
                  pl.BlockSpec((seq_len, SSD_INNER), lambda b: (b, 0)),
                  const(8, SSD_XBC), const(1, SSD_XBC), const(1, LANES), const(1, LANES),
                  const(1, SSD_INNER), const(1, SSD_INNER)]
                 + [const(*s.shape) for s in sels] + [st],
        out_specs=[seq(SSD_INNER), st],
        scratch_shapes=[pltpu.VMEM((seq_len, SSD_INNER), f32),
                        pltpu.VMEM((seq_len, 2 * SSD_BC), bf16),
                        pltpu.VMEM((seq_len, LANES), f32),
                        pltpu.VMEM((seq_len, SSD_INNER), f32),
                        pltpu.VMEM((SSD_STATE, SSD_INNER), f32),
                        pltpu.VMEM((SSD_STATE, SSD_INNER), f32),
                        pltpu.VMEM((SSD_CHUNK + 32, SSD_XBC), bf16)],
        compiler_params=_cparams(("parallel",)),
        name="ssd",
    )(xbc, dt, z, conv_w, conv_b, dtb, alog, dskip, nw, *sels, h0)


def _attn_kernel(n_blocks, nsub, local, q_ref, g_ref, k_ref, kc_ref, *rest):
    nv = nsub + 2 if local else 0
    v_refs = rest[:nv]
    vc_ref, sink_ref, o_ref, s_s = rest[nv:]
    t = ATT_BLOCK
    i0 = pl.program_id(1) * nsub
    rpk = ATT_HEADS // ATT_KV_HEADS
    lane = lax.broadcasted_iota(jnp.int32, (t, LANES), 1)
    lo_half = lane < ATT_HEAD_DIM
    zero_b = jnp.zeros((), bf16)
    kl = lax.broadcasted_iota(jnp.int32, (t, t), 0)
    ql = lax.broadcasted_iota(jnp.int32, (t, t), 1)
    for sub in range(nsub):
        i = i0 + sub
        qv = q_ref[sub * t:(sub + 1) * t, :]
        if local:
            p0 = pl.multiple_of(jnp.maximum(i - 1, 0) * t, t)
            c0 = pl.multiple_of(i * t, t)
            n0 = pl.multiple_of(jnp.minimum(i + 1, n_blocks - 1) * t, t)
            bias_prev = jnp.where((kl >= ql) & (i > 0), 0.0, NEG_INF)
            bias_next = jnp.where((kl <= ql) & (i < n_blocks - 1), 0.0, NEG_INF)
            bias_prev = jnp.concatenate([bias_prev] * rpk, axis=1)
            bias_next = jnp.concatenate([bias_next] * rpk, axis=1)
        for j in range(ATT_KV_HEADS):
            ls = slice(j * LANES, (j + 1) * LANES)
            if local:
                kk = jnp.concatenate([k_ref[pl.ds(p0, t), ls], k_ref[pl.ds(c0, t), ls],
                                      k_ref[pl.ds(n0, t), ls], kc_ref[:, ls]], axis=0)
            else:
                kk = kc_ref[:, ls]
            pieces = []
            for r in range(rpk):
                hq = j * rpk + r
                qp = qv[:, (hq // 2) * LANES:(hq // 2 + 1) * LANES]
                keep = lo_half if hq % 2 == 0 else jnp.logical_not(lo_half)
                pieces.append(jnp.where(keep, qp, zero_b))
            q4 = jnp.concatenate(pieces, axis=0)
            s = lax.dot_general(kk, q4, (((1,), (1,)), ((), ())), preferred_element_type=f32)
            if local:
                s = jnp.concatenate([s[:t] + bias_prev, s[t:2 * t], s[2 * t:3 * t] + bias_next, s[3 * t:]],
                                    axis=0)
            s_s[sub, j] = s
    for sub in range(nsub):
        outs = []
        for j in range(ATT_KV_HEADS):
            vs = slice(j * ATT_HEAD_DIM, (j + 1) * ATT_HEAD_DIM)
            if local:
                vvt = jnp.concatenate([v_refs[sub + k][vs, :] for k in range(3)] + [vc_ref[vs, :]], axis=1)
            else:
                vvt = vc_ref[vs, :]
            sk = jnp.concatenate([jnp.full((1, t), sink_ref[j * rpk + r] * LOG2E, f32) for r in range(rpk)],
                                 axis=1)
            s = s_s[sub, j]
            m = jnp.maximum(jnp.max(s, axis=0, keepdims=True), sk)
            p = jnp.exp2(s - m)
            den = jnp.sum(p, axis=0, keepdims=True) + jnp.exp2(sk - m)
            vvt = jnp.concatenate([vvt, vvt], axis=0)
            ot = jnp.dot(vvt, p.astype(bf16), preferred_element_type=f32) / den
            o4 = [jnp.transpose(ot[:, r * t:(r + 1) * t]) for r in range(rpk)]
            outs.append(jnp.where(lo_half, o4[0], o4[1]))
            outs.append(jnp.where(lo_half, o4[2], o4[3]))
        o = jnp.concatenate(outs, axis=1)
        gv = g_ref[sub * t:(sub + 1) * t, :].astype(f32)
        o_ref[sub * t:(sub + 1) * t, :] = (o * _silu(gv)).astype(o_ref.dtype)


def _attention(q, g, k, vt, kc, vct, sink, bsz, seq_len, local):
    t = ATT_BLOCK
    nb = seq_len // t
    nsub = max(s for s in range(1, ATT_SUB + 1) if nb % s == 0)
    n_ctx = kc.shape[0] // bsz
    kw = ATT_KV_HEADS * LANES
    vw = ATT_KVW
    blk = pl.BlockSpec((nsub * t, ATT_Q), lambda b, i: (b * (nb // nsub) + i, 0))
    full = lambda n: pl.BlockSpec((n, kw), lambda b, i: (b, 0))
    vblk = lambda off: pl.BlockSpec((vw, t), lambda b, i: (0, b * nb + jnp.clip(i * nsub + off, 0, nb - 1)))
    vspecs = [vblk(off) for off in range(-1, nsub + 1)] if local else []
    return pl.pallas_call(
        functools.partial(_attn_kernel, nb, nsub, local),
        out_shape=jax.ShapeDtypeStruct((bsz * seq_len, ATT_Q), bf16),
        grid=(bsz, nb // nsub),
        in_specs=[blk, blk, full(k.shape[0] // bsz), full(n_ctx)] + vspecs
                 + [pl.BlockSpec((vw, n_ctx), lambda b, i: (0, b)), pl.BlockSpec(memory_space=pltpu.SMEM)],
        out_specs=blk,
        scratch_shapes=[pltpu.VMEM((nsub, ATT_KV_HEADS, (3 * t if local else 0) + n_ctx,
                                    (ATT_HEADS // ATT_KV_HEADS) * t), f32)],
        compiler_params=_cparams(("parallel", "arbitrary")),
        name="attention",
    )(q, g, k, kc, *([vt] * len(vspecs)), vct, sink)


def _s5_disc_kernel(lre_ref, lim_ref, ls_ref, bre_ref, bim_ref, abre_ref, abim_ref, bbre_ref, bbim_ref):
    lam_re = lre_ref[...]
    lam_im = lim_ref[...]
    dt = jnp.exp(ls_ref[...])
    mag = jnp.exp(lam_re * dt)
    ab_re = mag * jnp.cos(lam_im * dt)
    ab_im = mag * jnp.sin(lam_im * dt)
    num_re, num_im = ab_re - 1.0, ab_im
    den = lam_re * lam_re + lam_im * lam_im
    coef_re = (num_re * lam_re + num_im * lam_im) / den
    coef_im = (num_im * lam_re - num_re * lam_im) / den
    b_re, b_im = bre_ref[...], bim_ref[...]
    abre_ref[...] = ab_re
    abim_ref[...] = ab_im
    bbre_ref[...] = coef_re * b_re - coef_im * b_im
    bbim_ref[...] = coef_re * b_im + coef_im * b_re


def _s5_discretise(lam_re, lam_im, log_step, b_re, b_im):
    g, n, cg = b_re.shape
    exp = lambda t: jnp.repeat(t.reshape(2 * g, n), cg, axis=1)
    ls = jnp.broadcast_to(log_step.reshape(2 * g, 1), (2 * g, n * cg))
    bb = lambda t: jnp.tile(t.reshape(g, n * cg), (2, 1))
    shp = jax.ShapeDtypeStruct((2 * g, n * cg), f32)
    ab_re, ab_im, bb_re, bb_im = pl.pallas_call(
        _s5_disc_kernel, out_shape=[shp] * 4, name="s5_disc",
    )(exp(lam_re), exp(lam_im), ls, bb(b_re), bb(b_im))
    first = lambda t: t.reshape(2, g, n, cg)[..., 0]
    full = lambda t: t.reshape(2, g, n, cg)
    return first(ab_re), first(ab_im), full(bb_re), full(bb_im)


def _cmul(ar, ai, br, bi):
    return ar * br - ai * bi, ar * bi + ai * br


def _s5_kernel(n_lat, n_ctx, *refs):
    tc = S5_TC
    xl, xc, arow_ref, acol_ref, bbd_ref, cbd_ref, y_ref, wyz_s, ws_s, sl_s, sc_s = refs
    hw = S5_HGRP * S5_STATE
    hl = S5_HGRP * S5_GROUP_CH
    per = LANES // hl
    xw = tc * hl
    nsl = hw // LANES
    nb = 8
    pos = lax.broadcasted_iota(jnp.int32, (1, LANES), 1) // hl

    def pick(arrs):
        out = arrs[0]
        for i in range(1, per):
            out = jnp.where(pos == i, arrs[i], out)
        return out

    def powers(re, im, n):
        out = [(jnp.ones_like(re), jnp.zeros_like(im))]
        for _ in range(n):
            out.append(_cmul(out[-1][0], out[-1][1], re, im))
        return out

    steps = lambda h, p: [p * per + (i - h) % per for i in range(per)]

    prow = []
    for h in range(per):
        pr = [powers(arow_ref[h, 2 * d:2 * d + 1, :], arow_ref[h, 2 * d + 1:2 * d + 2, :], tc) for d in range(2)]
        prow.append(pr)
        for s in range(tc):
            for d, k in ((0, tc - 1 - s), (1, s)):
                wr, wi = _cmul(bbd_ref[2 * d, h], bbd_ref[2 * d + 1, h], *pr[d][k])
                ws_s[h, s * hl:(s + 1) * hl, d * 2 * hw:d * 2 * hw + hw] = wr.astype(bf16)
                ws_s[h, s * hl:(s + 1) * hl, d * 2 * hw + hw:(d + 1) * 2 * hw] = wi.astype(bf16)
        crhs = jnp.concatenate([cbd_ref[0, h], -cbd_ref[1, h]], axis=0).astype(bf16)
        kall = [jnp.dot(ws_s[h, :, d * 2 * hw:(d + 1) * 2 * hw], crhs, preferred_element_type=f32)
                for d in range(2)]
        kf = [kall[0][(tc - 1 - k) * hl:(tc - k) * hl] for k in range(tc)]
        kb = [kall[1][k * hl:(k + 1) * hl] for k in range(tc)]
        lag = lambda s, t: kf[t - s] if t > s else (kb[s - t] if t < s else kf[0] + kb[0])
        for s in range(tc):
            for p in range(tc // per):
                wyz_s[h, s * hl:(s + 1) * hl, p * LANES:(p + 1) * LANES] = (
                    pick([lag(s, t) for t in steps(h, p)]).astype(bf16))
        for d in range(2):
            pc = powers(acol_ref[h, 2 * d], acol_ref[h, 2 * d + 1], tc)
            kk = (lambda t: t + 1) if d == 0 else (lambda t: tc - t)
            r0 = xw + d * 2 * hw
            for p in range(tc // per):
                ar = pick([pc[kk(t)][0] for t in steps(h, p)])
                ai = pick([pc[kk(t)][1] for t in steps(h, p)])
                dre, dim_ = _cmul(cbd_ref[0, h], cbd_ref[1, h], ar, ai)
                wyz_s[h, r0:r0 + hw, p * LANES:(p + 1) * LANES] = dre.astype(bf16)
                wyz_s[h, r0 + hw:r0 + 2 * hw, p * LANES:(p + 1) * LANES] = (-dim_).astype(bf16)

    def rows_of(x_ref, b, n, h):
        return x_ref[b * n:(b + 1) * n, h * xw:(h + 1) * xw]

    def inject(x_ref, s_ref, n):
        for b in range(nb):
            for h in range(per):
                sb = jnp.dot(rows_of(x_ref, b, n, h), ws_s[h], preferred_element_type=f32)
                for k in range(4 * nsl):
                    s_ref[h * 4 * nsl + k, pl.ds(b, n, stride=nb), :] = sb[:, k * LANES:(k + 1) * LANES]

    at = [[[tuple(jnp.broadcast_to(p[:, k * LANES:(k + 1) * LANES], (nb, LANES)) for p in prow[h][d][tc])
            for k in range(nsl)] for d in range(2)] for h in range(per)]

    def scan(s_ref, n, init):
        def step(i, carry):
            idxs = [pl.ds(pl.multiple_of(i * nb, nb), nb), pl.ds(pl.multiple_of((n - 1 - i) * nb, nb), nb)]
            chains = [(h, d, k) for h in range(per) for d in range(2) for k in range(nsl)]
            slab = lambda h, d, k: h * 4 * nsl + d * 2 * nsl + k
            inj = [(s_ref[slab(h, d, k), idxs[d], :], s_ref[slab(h, d, k) + nsl, idxs[d], :])
                   for h, d, k in chains]
            new = []
            for c, (h, d, k) in enumerate(chains):
                hr, hi = carry[2 * c], carry[2 * c + 1]
                s_ref[slab(h, d, k), idxs[d], :] = hr
                s_ref[slab(h, d, k) + nsl, idxs[d], :] = hi
                ar, ai = at[h][d][k]
                new += [ar * hr - ai * hi + inj[c][0], ar * hi + ai * hr + inj[c][1]]
            return tuple(new)
        return lax.fori_loop(0, n, step, init, unroll=4)

    inject(xc, sc_s, n_ctx)
    h_ctx = scan(sc_s, n_ctx, tuple(jnp.zeros((nb, LANES), f32) for _ in range(per * 4 * nsl)))
    inject(xl, sl_s, n_lat)
    scan(sl_s, n_lat, h_ctx)

    for b in range(nb):
        yh = []
        for h in range(per):
            hin = jnp.concatenate([sl_s[h * 4 * nsl + k, pl.ds(b, n_lat, stride=nb), :]
                                   for k in range(4 * nsl)], axis=1).astype(bf16)
            yh.append(jnp.dot(jnp.concatenate([rows_of(xl, b, n_lat, h), hin], axis=1), wyz_s[h],
                              preferred_element_type=f32))
        for p in range(tc // per):
            grp = [yh[h][:, p * LANES:(p + 1) * LANES] for h in range(per)]
            for r in range(per):
                row = pick([grp[(i - r) % per] for i in range(per)])
                if r:
                    row = pltpu.roll(row, (per - r) * hl, axis=1)
                y_ref[pl.ds(b * n_lat * tc + p * per + r, n_lat, stride=tc), :] = row


def _s5_mix(u, u_c, arow, acol, bbd, cbd, bsz, seq_len, n_ctx_tok):
    assert bsz == 8, "the chunk recurrence puts the batch on the 8 sublanes"
    tc = S5_TC
    n_lat, n_ctx = seq_len // tc, n_ctx_tok // tc
    nblk = S5_GROUPS // S5_GBLK
    hw = S5_HGRP * S5_STATE
    hl = S5_HGRP * S5_GROUP_CH
    per = LANES // hl
    nsl = hw // LANES
    one = pl.Buffered(1)
    xspec = lambda rows: pl.BlockSpec((None, rows, tc * LANES), lambda g: (g, 0, 0))
    return pl.pallas_call(
        functools.partial(_s5_kernel, n_lat, n_ctx),
        out_shape=jax.ShapeDtypeStruct((nblk, bsz * seq_len, LANES), f32),
        grid=(nblk,),
        in_specs=[xspec(bsz * n_lat), xspec(bsz * n_ctx),
                  pl.BlockSpec((per, 4, hw), lambda g: (g, 0, 0), pipeline_mode=one),
                  pl.BlockSpec((per, 4, hw, LANES), lambda g: (g, 0, 0, 0), pipeline_mode=one),
                  pl.BlockSpec((4, per, hl, hw), lambda g: (0, g, 0, 0), pipeline_mode=one),
                  pl.BlockSpec((2, per, hw, LANES), lambda g: (0, g, 0, 0), pipeline_mode=one)],
        out_specs=pl.BlockSpec((None, bsz * seq_len, LANES), lambda g: (g, 0, 0)),
        scratch_shapes=[pltpu.VMEM((per, tc * hl + 4 * hw, tc * hl), bf16),
                        pltpu.VMEM((per, tc * hl, 4 * hw), bf16),
                        pltpu.VMEM((per * 4 * nsl, bsz * n_lat, LANES), f32),
                        pltpu.VMEM((per * 4 * nsl, bsz * n_ctx, LANES), f32)],
        compiler_params=_cparams(("arbitrary",)),
        name="s5_mix",
    )(u, u_c, arow, acol, bbd, cbd)


def _s5_block_params(ab_re, ab_im, bb_re, bb_im, c_re, c_im):
    g, n, cg = S5_GROUPS, S5_STATE, S5_GROUP_CH
    nh = g // S5_HGRP
    hw = S5_HGRP * n
    arow = jnp.stack([t[d].reshape(nh, hw) for d in range(2) for t in (ab_re, ab_im)], axis=1)
    acol = jnp.broadcast_to(arow[..., None], (nh, 4, hw, LANES))

    def blockdiag(t, rows_per, cols_per):
        tiled = jnp.concatenate([t] * S5_HGRP, axis=-1)
        r = np.arange(t.shape[-2])[:, None] // rows_per
        c = np.arange(S5_HGRP * cols_per)[None, :] // cols_per
        return jnp.where(jnp.asarray(r == c), tiled, 0.0)

    bb = jnp.stack([t[d] for d in range(2) for t in (bb_re, bb_im)], axis=0)
    bb = jnp.swapaxes(bb.reshape(4, nh, S5_HGRP, n, cg), -1, -2).reshape(4, nh, S5_HGRP * cg, n)
    bbd = blockdiag(bb, cg, n)
    cc = jnp.swapaxes(jnp.stack([c_re, c_im], axis=0).reshape(2, nh, S5_HGRP, cg, n), -1, -2)
    cbd = blockdiag(cc.reshape(2, nh, hw, cg), n, cg)
    return arow, acol, bbd, jnp.concatenate([cbd] * (LANES // (S5_HGRP * cg)), axis=-1)


def _s5_out_kernel(y_ref, u_ref, g_ref, x_ref, dskip_ref, gw_ref, gb_ref, w_ref, gate_ref,
                   fw_ref, o_ref):
    y = jnp.concatenate([y_ref[j] for j in range(y_ref.shape[0])], axis=1)
    y = y + dskip_ref[...] * u_ref[...].astype(f32)
    y = jax.nn.gelu(y)
    glu = jnp.dot(y.astype(bf16), gw_ref[...], preferred_element_type=f32) + gb_ref[...]
    y = y * jax.nn.sigmoid(glu)
    y = y * _silu(g_ref[...].astype(f32))
    x = x_ref[...] + gate_ref[...] * jnp.dot(y.astype(bf16), w_ref[...], preferred_element_type=f32)
    ms = jnp.mean(x * x, axis=-1, keepdims=True)
    o_ref[...] = x * lax.rsqrt(ms + EPS) * fw_ref[...]


def _s5_out(y, u, g, x, dskip, gw, gb, w, gate, fw, rows_per_mod):
    m, d = x.shape
    tm = min(ROW_TILE, rows_per_mod)
    per = rows_per_mod // tm
    row = lambda: pl.BlockSpec((tm, d), lambda i: (i, 0))
    vec = lambda: pl.BlockSpec((1, d), lambda i: (0, 0))
    mat = lambda: pl.BlockSpec((d, d), lambda i: (0, 0), pipeline_mode=pl.Buffered(1))
    return pl.pallas_call(
        _s5_out_kernel,
        out_shape=jax.ShapeDtypeStruct((m, d), f32),
        grid=(m // tm,),
        in_specs=[pl.BlockSpec((y.shape[0], tm, LANES), lambda i: (0, i, 0)),
                  row(), row(), row(), vec(), mat(), vec(), mat(),
                  pl.BlockSpec((None, 1, d), lambda i: (i // per, 0, 0)), vec()],
        out_specs=row(),
        compiler_params=_cparams(("parallel",)),
        name="s5_out",
    )(y, u, g, x, dskip, gw, gb, w, gate, fw)


def _even_weights(w_in):
    o = 0
    z = w_in[:, o:o + SSD_INNER]; o += SSD_INNER
    xbc = w_in[:, o:o + SSD_XBC]; o += SSD_XBC
    dt = w_in[:, o:o + 2 * SSD_HEADS]; o += 2 * SSD_HEADS
    q = w_in[:, o:o + ATT_Q]; o += ATT_Q
    k = w_in[:, o:o + ATT_KVW]; o += ATT_KVW
    v = w_in[:, o:o + ATT_KVW]; o += ATT_KVW
    g = w_in[:, o:o + ATT_Q]
    d = w_in.shape[0]
    dup = lambda t: jnp.concatenate([t.reshape(d, ATT_KV_HEADS, 1, ATT_HEAD_DIM)] * 2, axis=2).reshape(d, -1)
    dtp = jnp.pad(dt, ((0, 0), (0, LANES - 2 * SSD_HEADS)))
    cast = lambda t: t.astype(bf16)
    return [cast(t) for t in (z, xbc, q, dup(k), g, dtp)], cast(v.T)


def _even_segs(rope):
    scale = ATT_HEAD_DIM ** -0.5 * LOG2E
    widths = [(SSD_INNER, None, bf16), (SSD_XBC, None, bf16),
              (ATT_Q, scale, bf16),
              (ATT_KV_HEADS * LANES, 1.0 if rope else None, bf16),
              (ATT_Q, None, bf16), (LANES, None, f32)]
    return [(i, w, r, dtp, False) for i, (w, r, dtp) in enumerate(widths)]


def _pad_lanes(v, n=LANES):
    v = v.reshape(1, -1)
    return jnp.pad(v, ((0, 0), (0, n - v.shape[1])))


def kernel(x, c, ctx, c_ctx, e_norm_w, e_ada_w, e_ada_b, e_w_in, e_conv_w, e_conv_b, e_dt_bias,
           e_a_log, e_d_skip, e_ssd_norm_w, e_sink, e_w_out, o_norm_w, o_ada_w, o_ada_b, o_w_in,
           o_lam_re, o_lam_im, o_log_step, o_b_re, o_b_im, o_c_re, o_c_im, o_d_skip, o_glu_w,
           o_glu_b, o_w_out, final_norm_w):
    bsz, seq_len, d = x.shape
    n_ctx = ctx.shape[1]
    assert e_w_in.shape[0] == 1 and o_w_in.shape[0] == 1, "one even (SSD + attention) and one odd (S5) layer"
    xf = x.reshape(bsz * seq_len, d)
    xcf = ctx.reshape(bsz * n_ctx, d)

    cvecs = jnp.concatenate([c, c_ctx[None, :], jnp.zeros((16 - bsz - 1, d), f32)], axis=0)

    def modulation(ada_w, ada_b):
        mod = _adaln(cvecs, ada_w, ada_b.reshape(1, -1))
        parts = [mod[:, k * d:(k + 1) * d] for k in range(3)]
        lat = [p[:bsz].reshape(bsz, 1, d) for p in parts]
        cx = [p[bsz:bsz + 1].reshape(1, 1, d) for p in parts]
        return lat, cx

    (shift, scale, gate), (shift_c, scale_c, gate_c) = modulation(e_ada_w[0], e_ada_b[0])
    w_in, w_vt = _even_weights(e_w_in[0])
    nw = e_norm_w[0].reshape(1, d)
    tabs = _rope_tables(seq_len)
    vseg = [(0, ATT_KVW, bf16)]
    z, xbc, q, k, g, dt, vt = _inproj(xf, shift, scale, nw, w_in, _even_segs(True), seq_len, tabs, w_vt, vseg)
    z_c, xbc_c, q_c, k_c, g_c, dt_c, vt_c = _inproj(xcf, shift_c, scale_c, nw, w_in, _even_segs(False), n_ctx,
                                                    None, w_vt, vseg)

    conv_w = jnp.pad(e_conv_w[0], ((0, 8 - SSD_CONV), (0, 0)))
    conv_b = e_conv_b[0].reshape(1, -1)
    dtb = _pad_lanes(e_dt_bias[0])
    alog = _pad_lanes(e_a_log[0])
    dskip = jnp.repeat(e_d_skip[0], SSD_HEAD_DIM).reshape(1, -1)
    snw = e_ssd_norm_w[0].reshape(1, -1)
    h0 = jnp.zeros((bsz, 2, SSD_STATE, SSD_INNER), f32)
    ssd_c, hfin = _ssd(xbc_c, dt_c, z_c, conv_w, conv_b, dtb, alog, dskip, snw, h0, bsz, n_ctx)
    ssd_o, _ = _ssd(xbc, dt, z, conv_w, conv_b, dtb, alog, dskip, snw, hfin, bsz, seq_len)

    sink = e_sink[0]
    att = _attention(q, g, k, vt, k_c, vt_c, sink, bsz, seq_len, True)
    att_c = _attention(q_c, g_c, k_c, vt_c, k_c, vt_c, sink, bsz, n_ctx, False)
    w_out = e_w_out[0].astype(bf16)
    even_gate, even_gate_c = gate, gate_c

    (shift, scale, gate), (shift_c, scale_c, _) = modulation(o_ada_w[0], o_ada_b[0])
    w_u = o_w_in[0][:, :S5_WIDTH].astype(bf16)
    w_g = o_w_in[0][:, S5_WIDTH:].astype(bf16)
    nw = o_norm_w[0].reshape(1, d)
    x1, u, u_ch, g2 = _inproj(xf, shift, scale, nw, [w_u, w_g],
                              [(0, S5_WIDTH, None, bf16, False), (0, S5_WIDTH, None, bf16, True),
                               (1, S5_WIDTH, None, bf16, False)], seq_len,
                              pre=(ssd_o, att, w_out, even_gate))
    _, uc_ch = _inproj(xcf, shift_c, scale_c, nw, [w_u], [(0, S5_WIDTH, None, bf16, True)], n_ctx,
                       pre=(ssd_c, att_c, w_out, even_gate_c))

    ab_re, ab_im, bb_re, bb_im = _s5_discretise(o_lam_re[0], o_lam_im[0], o_log_step[0], o_b_re[0], o_b_im[0])
    arow, acol, bbd, cbd = _s5_block_params(ab_re, ab_im, bb_re, bb_im, o_c_re[0], o_c_im[0])
    y = _s5_mix(u_ch, uc_ch, arow, acol, bbd, cbd, bsz, seq_len, n_ctx)
    out = _s5_out(y, u, g2, x1, o_d_skip[0].reshape(1, -1),
                  o_glu_w[0].astype(bf16), o_glu_b[0].reshape(1, -1), o_w_out[0].astype(bf16), gate,
                  final_norm_w.reshape(1, -1), seq_len)
    return out.reshape(bsz, seq_len, d)
```

```python
import functools
import math

import jax
import jax.numpy as jnp
import numpy as np
from jax import lax
from jax.experimental import pallas as pl
from jax.experimental.pallas import tpu as pltpu

f32 = jnp.float32
bf16 = jnp.bfloat16

GRID_W = 64
EPS = 1e-6
NEG_INF = -1e30

SSD_HEADS = 16
SSD_HEAD_DIM = 64
SSD_GROUPS = 2
SSD_STATE = 128
SSD_CONV = 5
SSD_CHUNK = 128
SSD_INNER = SSD_HEADS * SSD_HEAD_DIM
SSD_BC = SSD_GROUPS * SSD_STATE
SSD_XBC = SSD_INNER + 2 * SSD_BC
ATT_HEADS = 16
ATT_KV_HEADS = 4
ATT_HEAD_DIM = 64
ATT_BLOCK = 128
ROPE_THETA = 10000.0
ATT_Q = ATT_HEADS * ATT_HEAD_DIM
ATT_KVW = ATT_KV_HEADS * ATT_HEAD_DIM
S5_WIDTH = 1024
S5_GROUP_CH = 16
S5_GROUPS = S5_WIDTH // S5_GROUP_CH
S5_STATE = 64

LOG2E = math.log2(math.e)
LANES = 128
ROW_TILE = 1024
ATT_SUB = 4
S5_GBLK = 8
S5_TC = 8
S5_HGRP = 2
VMEM_LIMIT = 56 * 1024 * 1024


def _cparams(sem, flags=None):
    return pltpu.CompilerParams(dimension_semantics=sem, vmem_limit_bytes=VMEM_LIMIT, flags=flags)


def _silu(x):
    h = 0.5 * x
    return h + h * jnp.tanh(h)


def _adaln_kernel(c_ref, w_ref, b_ref, o_ref):
    c = c_ref[...]
    s = _silu(c).astype(bf16)
    o_ref[...] = jnp.dot(s, w_ref[...].astype(bf16), preferred_element_type=f32) + b_ref[...]


def _adaln(cvecs, w, b):
    r, d = cvecs.shape
    n = w.shape[1]
    tn = 1024
    return pl.pallas_call(
        _adaln_kernel,
        out_shape=jax.ShapeDtypeStruct((r, n), f32),
        grid=(n // tn,),
        in_specs=[pl.BlockSpec((r, d), lambda j: (0, 0)),
                  pl.BlockSpec((d, tn), lambda j: (0, j)),
                  pl.BlockSpec((1, tn), lambda j: (0, j))],
        out_specs=pl.BlockSpec((r, tn), lambda j: (0, j)),
        compiler_params=_cparams(("arbitrary",)),
        name="adaln",
    )(cvecs, w, b)


def _inproj_kernel(segs, tsegs, n_w, has_rope, has_pre, x_ref, shift_ref, scale_ref, nw_ref, *rest):
    if has_pre:
        a_ref, b_ref, wo_ref, gate_ref = rest[:4]
        rest = rest[4:]
    w_refs, rest = rest[:n_w], rest[n_w:]
    if tsegs:
        wt_ref, rest = rest[0], rest[1:]
    if has_rope:
        cos_ref, sina_ref, sinb_ref = rest[:3]
        rest = rest[3:]
    if has_pre:
        x1_ref, rest = rest[0], rest[1:]
    outs = rest[:len(segs)]
    touts = rest[len(segs):len(segs) + len(tsegs)]
    rest = rest[len(segs) + len(tsegs):]
    slab_s = rest[0] if rest else None
    x = x_ref[...]
    if has_pre:
        ka = a_ref.shape[1]
        acc = jnp.dot(a_ref[...], wo_ref[:ka, :], preferred_element_type=f32)
        acc = acc + jnp.dot(b_ref[...], wo_ref[ka:, :], preferred_element_type=f32)
        x = x + gate_ref[...] * acc
        x1_ref[...] = x
    ms = jnp.mean(x * x, axis=-1, keepdims=True)
    h = (x * lax.rsqrt(ms + EPS)) * nw_ref[...]
    h = h * (1.0 + scale_ref[...]) + shift_ref[...]
    hb = h.astype(bf16)
    for (start, width, _), o_ref in zip(tsegs, touts):
        acc_t = lax.dot_general(wt_ref[start:start + width, :], hb, (((1,), (1,)), ((), ())),
                                preferred_element_type=f32)
        o_ref[...] = acc_t.astype(o_ref.dtype)
    products = {}
    for (widx, width, rope, _, chunked), o_ref in zip(segs, outs):
        if widx not in products:
            products[widx] = jnp.dot(hb, w_refs[widx][...], preferred_element_type=f32)
        acc = products[widx]
        if chunked:
            tm = acc.shape[0]
            hl = S5_HGRP * S5_GROUP_CH
            per = LANES // hl
            pos = lax.broadcasted_iota(jnp.int32, (1, LANES), 1) // hl
            for j in range(width // LANES):
                slab_s[j] = acc[:, j * LANES:(j + 1) * LANES]
            for j in range(width // LANES):
                rows = [slab_s[j, pl.ds(s, tm // S5_TC, stride=S5_TC), :] for s in range(S5_TC)]
                for p in range(S5_TC // per):
                    for h in range(per):
                        out = None
                        for i in range(per):
                            piece = rows[p * per + i]
                            if i != h:
                                piece = pltpu.roll(piece, ((i - h) * hl) % LANES, axis=1)
                            out = piece if out is None else jnp.where(pos == i, piece, out)
                        c0 = h * S5_TC * hl + p * LANES
                        o_ref[j, :, c0:c0 + LANES] = out.astype(o_ref.dtype)
            continue
        if rope is not None and not has_rope:
            acc = acc * rope
        elif rope is not None:
            rep = width // LANES
            cos = jnp.concatenate([cos_ref[...]] * rep, axis=1) * rope
            sina = jnp.concatenate([sina_ref[...]] * rep, axis=1) * rope
            sinb = jnp.concatenate([sinb_ref[...]] * rep, axis=1) * rope
            half = ATT_HEAD_DIM // 2
            up = pltpu.roll(acc, width - half, axis=1)
            dn = pltpu.roll(acc, half, axis=1)
            acc = acc * cos + up * sina + dn * sinb
        o_ref[...] = acc.astype(o_ref.dtype)


def _inproj(x, shift, scale, nw, w, segs, rows_per_mod, rope_tabs=None, wt=None, tsegs=(), pre=None):
    m, d = x.shape
    tm = min(ROW_TILE, rows_per_mod)
    per = rows_per_mod // tm
    nmod = shift.shape[0]
    mod_idx = (lambda i: (i // per, 0, 0)) if nmod > 1 else (lambda i: (0, 0, 0))
    in_specs = [pl.BlockSpec((tm, d), lambda i: (i, 0)),
                pl.BlockSpec((None, 1, d), mod_idx),
                pl.BlockSpec((None, 1, d), mod_idx),
                pl.BlockSpec((1, d), lambda i: (0, 0))]
    args = [x, shift, scale, nw]
    if pre is not None:
        a, b, w_out, gate = pre
        gate_idx = (lambda i: (i // per, 0, 0)) if gate.shape[0] > 1 else (lambda i: (0, 0, 0))
        in_specs += [pl.BlockSpec((tm, a.shape[1]), lambda i: (i, 0)),
                     pl.BlockSpec((tm, b.shape[1]), lambda i: (i, 0)),
                     pl.BlockSpec(w_out.shape, lambda i: (0, 0), pipeline_mode=pl.Buffered(1)),
                     pl.BlockSpec((None, 1, d), gate_idx)]
        args += [a, b, w_out, gate]
    in_specs += [pl.BlockSpec(wi.shape, lambda i: (0, 0), pipeline_mode=pl.Buffered(1)) for wi in w]
    args += list(w)
    if tsegs:
        in_specs.append(pl.BlockSpec(wt.shape, lambda i: (0, 0), pipeline_mode=pl.Buffered(1)))
        args.append(wt)
    if rope_tabs is not None:
        for t in rope_tabs:
            in_specs.append(pl.BlockSpec((tm, LANES), lambda i: (i % per, 0)))
            args.append(t)
    out_shape, out_specs = [], []
    if pre is not None:
        out_shape.append(jax.ShapeDtypeStruct((m, d), f32))
        out_specs.append(pl.BlockSpec((tm, d), lambda i: (i, 0)))
    for sg in segs:
        if sg[4]:
            out_shape.append(jax.ShapeDtypeStruct((sg[1] // LANES, m // S5_TC, S5_TC * LANES), sg[3]))
            out_specs.append(pl.BlockSpec((sg[1] // LANES, tm // S5_TC, S5_TC * LANES), lambda i: (0, i, 0)))
        else:
            out_shape.append(jax.ShapeDtypeStruct((m, sg[1]), sg[3]))
            out_specs.append(pl.BlockSpec((tm, sg[1]), lambda i: (i, 0)))
    for _, width, dtp in tsegs:
        out_shape.append(jax.ShapeDtypeStruct((width, m), dtp))
        out_specs.append(pl.BlockSpec((width, tm), lambda i: (0, i)))
    chunked_w = [sg[1] for sg in segs if sg[4]]
    scratch = [pltpu.VMEM((max(chunked_w) // LANES, tm, LANES), f32)] if chunked_w else []
    return pl.pallas_call(
        functools.partial(_inproj_kernel, tuple(segs), tuple(tsegs), len(w), rope_tabs is not None,
                          pre is not None),
        out_shape=out_shape,
        grid=(m // tm,),
        in_specs=in_specs,
        out_specs=out_specs,
        scratch_shapes=scratch,
        compiler_params=_cparams(("parallel",)),
        name="inproj",
    )(*args)


def _rope_tables(seq_len):
    rows = seq_len // GRID_W
    row = jnp.repeat(jnp.arange(rows, dtype=f32), GRID_W)
    col = jnp.tile(jnp.arange(GRID_W, dtype=f32), rows)
    n_freq = ATT_HEAD_DIM // 4
    inv = ROPE_THETA ** (-jnp.arange(n_freq, dtype=f32) / n_freq)
    ang = jnp.concatenate([row[:, None] * inv, col[:, None] * inv], axis=-1)
    cos, sin = jnp.cos(ang), jnp.sin(ang)
    zero = jnp.zeros_like(sin)
    cos_h = jnp.concatenate([cos, cos], axis=-1)
    sina_h = jnp.concatenate([-sin, zero], axis=-1)
    sinb_h = jnp.concatenate([zero, sin], axis=-1)
    two = lambda t: jnp.concatenate([t, t], axis=-1)
    return two(cos_h), two(sina_h), two(sinb_h)


SSD_PACK = 32


def _split3(x):
    hi = x.astype(bf16)
    r1 = x - hi.astype(f32)
    mid = r1.astype(bf16)
    lo = (r1 - mid.astype(f32)).astype(bf16)
    return hi, mid, lo


def _pack3(x):
    hi, mid, lo = _split3(x)
    lane = lax.broadcasted_iota(jnp.int32, x.shape, 1)
    mid_r = pltpu.roll(mid.astype(f32), SSD_PACK, axis=1)
    lo_r = pltpu.roll(lo.astype(f32), 2 * SSD_PACK, axis=1)
    packed = jnp.where(lane < SSD_PACK, hi.astype(f32),
                       jnp.where(lane < 2 * SSD_PACK, mid_r,
                                 jnp.where(lane < 3 * SSD_PACK, lo_r, 0.0)))
    return packed.astype(bf16)


def _ssd_selectors():
    k = np.arange(LANES)
    src = np.where(k < 3 * SSD_PACK, k % SSD_PACK, -1)
    col_blk = np.arange(SSD_PACK * SSD_CHUNK) // SSD_CHUNK
    sel_bc = (src[:, None] == col_blk[None, :])
    head = np.arange(SSD_INNER) // SSD_HEAD_DIM
    sel_f = (src[:, None] == head[None, :])
    sel_b = (src[:, None] == (head + SSD_HEADS)[None, :])
    tri3 = np.tile(np.tril(np.ones((SSD_CHUNK, SSD_CHUNK))), (1, 3))
    rows = np.arange(SSD_CHUNK)[:, None]
    cols = np.arange(SSD_CHUNK + 32)[None, :]
    half = SSD_CONV // 2
    shift = np.concatenate([cols == rows + 16 + d for d in range(-half, half + 1) if d != 0], axis=0)
    as_bf = lambda a: jnp.asarray(a.astype(np.float32), dtype=bf16)
    return as_bf(sel_bc), as_bf(sel_f), as_bf(sel_b), as_bf(tri3), as_bf(shift)


def _ssd_kernel(seq_len, xbc_ref, dt_ref, z_ref, cw_ref, cb_ref, dtb_ref, alog_ref, dskip_ref, nw_ref,
                selbc_ref, self_ref, selb_ref, tri3_ref, shift_ref, h0_ref, out_ref, hfin_ref,
                xs_s, bc_s, dt_s, y_s, hf_s, hb_s, win_s):
    q = SSD_CHUNK
    nc = seq_len // q
    halo = 16
    H, P, N = SSD_HEADS, SSD_HEAD_DIM, SSD_STATE
    gw = (H // SSD_GROUPS) * P
    a2_row = -jnp.exp(alog_ref[...]) * math.log2(math.e)

    def conv_chunk(c):
        r0 = pl.multiple_of(c * q, q)
        pstart = pl.multiple_of(jnp.maximum(r0 - halo, 0), halo)
        nstart = pl.multiple_of(jnp.minimum(r0 + q, seq_len - halo), halo)
        zero = jnp.zeros((), bf16)
        win_s[0:halo, :] = jnp.where(c > 0, xbc_ref[pl.ds(pstart, halo), :], zero)
        win_s[halo:halo + q, :] = xbc_ref[pl.ds(r0, q), :]
        win_s[halo + q:, :] = jnp.where(c < nc - 1, xbc_ref[pl.ds(nstart, halo), :], zero)
        taps = [k for k in range(SSD_CONV) if k != SSD_CONV // 2]
        cw = 2 * LANES
        for j in range(SSD_XBC // cw):
            cs = slice(j * cw, (j + 1) * cw)
            sh = jnp.dot(shift_ref[...], win_s[:, cs], preferred_element_type=f32)
            acc = cb_ref[:, cs] + win_s[halo:halo + q, cs].astype(f32) * cw_ref[SSD_CONV // 2:SSD_CONV // 2 + 1, cs]
            for n, k in enumerate(taps):
                acc = acc + sh[n * q:(n + 1) * q, :] * cw_ref[k:k + 1, cs]
            act = _silu(acc)
            if j < SSD_INNER // cw:
                xs_s[pl.ds(r0, q), cs] = act
            else:
                bc_s[pl.ds(r0, q), j * cw - SSD_INNER:(j + 1) * cw - SSD_INNER] = act.astype(bf16)
        dt_s[pl.ds(r0, q), :] = jax.nn.softplus(dt_ref[pl.ds(r0, q), :] + dtb_ref[...])

    conv_chunk(0)
    hf_s[...] = h0_ref[0]
    hb_s[...] = h0_ref[1]

    ri = lax.broadcasted_iota(jnp.int32, (q, q), 0)
    ci = lax.broadcasted_iota(jnp.int32, (q, q), 1)
    lower = ri >= ci
    upper = ci >= ri
    lane = lax.broadcasted_iota(jnp.int32, (q, LANES), 1)
    lo_half = lane < P

    def cumsums(dt):
        dta = dt * a2_row
        cf = jnp.dot(tri3_ref[...], jnp.concatenate(_split3(dta), axis=0), preferred_element_type=f32)
        rb = cf[q - 1:q, :] - cf + dta
        return cf, rb

    def load_chunk(r0):
        dt = dt_s[pl.ds(r0, q), :]
        xs = xs_s[pl.ds(r0, q), :]
        bcv = bc_s[pl.ds(r0, q), :]
        bmat = [bcv[:, g * N:(g + 1) * N] for g in range(SSD_GROUPS)]
        cmat = [bcv[:, SSD_BC + g * N:SSD_BC + (g + 1) * N] for g in range(SSD_GROUPS)]
        return dt, xs, bmat, cmat

    def inter_chunk(h_s, sel_ref, decay, weight, xs, bmat, cmat, dec_idx):
        ew = jnp.dot(jnp.concatenate([_pack3(decay), _pack3(weight)], axis=0), sel_ref[...],
                     preferred_element_type=f32)
        e_x, w_x = ew[:q], ew[q:]
        hb_ = h_s[...].astype(bf16)
        yoff = jnp.concatenate(
            [jnp.dot(cmat[g], hb_[:, g * gw:(g + 1) * gw], preferred_element_type=f32)
             for g in range(SSD_GROUPS)], axis=1)
        xw = (xs * w_x).astype(bf16)
        dec_row = e_x[dec_idx:dec_idx + 1, :]
        for g in range(SSD_GROUPS):
            gs = slice(g * gw, (g + 1) * gw)
            bt = jnp.transpose(bmat[g].astype(f32)).astype(bf16)
            upd = jnp.dot(bt, xw[:, gs], preferred_element_type=f32)
            h_s[:, gs] = h_s[:, gs] * dec_row[:, gs] + upd
        return yoff * e_x

    def finish(r0, y, xs):
        yy = y + xs * dskip_ref[...]
        zz = z_ref[pl.ds(r0, q), :].astype(f32)
        gated = yy * _silu(zz)
        ms = jnp.mean(gated * gated, axis=-1, keepdims=True)
        out_ref[pl.ds(r0, q), :] = (gated * lax.rsqrt(ms + EPS) * nw_ref[...]).astype(out_ref.dtype)

    def fwd_chunk(c, second_half):
        r0 = pl.multiple_of(c * q, q)
        dt, xs, bmat, cmat = load_chunk(r0)
        cf, rb = cumsums(dt)
        pcol = jnp.where(lane < H, cf, rb)
        bcast = jnp.dot(_pack3(pcol), selbc_ref[...], preferred_element_type=f32)
        prow = jnp.transpose(pcol - jnp.log2(dt))
        cbm = [lax.dot_general(cmat[g], bmat[g], (((1,), (1,)), ((), ())), preferred_element_type=f32)
               for g in range(SSD_GROUPS)]
        xsb = xs.astype(bf16)
        zero_b = jnp.zeros((), bf16)
        ypairs = []
        for k in range(H // 2):
            mats = []
            for h in (2 * k, 2 * k + 1):
                g = h // (H // SSD_GROUPS)
                hb_ = H + h
                segf = bcast[:, h * q:(h + 1) * q] - prow[h:h + 1, :]
                segb = bcast[:, hb_ * q:(hb_ + 1) * q] - prow[hb_:hb_ + 1, :]
                df = jnp.exp2(jnp.where(lower, segf, NEG_INF))
                db = jnp.exp2(jnp.where(upper, segb, NEG_INF))
                mats.append((cbm[g] * (df + db)).astype(bf16))
            xp = xsb[:, k * LANES:(k + 1) * LANES]
            xbd = jnp.concatenate([jnp.where(lo_half, xp, zero_b), jnp.where(lo_half, zero_b, xp)], axis=0)
            ypairs.append(jnp.dot(jnp.concatenate(mats, axis=1), xbd, preferred_element_type=f32))
        y = jnp.concatenate(ypairs, axis=1)
        wfa = jnp.exp2(cf[q - 1:q, :] - cf) * dt
        y = y + inter_chunk(hf_s, self_ref, jnp.exp2(cf), wfa, xs, bmat, cmat, q - 1)
        if second_half:
            finish(r0, y_s[pl.ds(r0, q), :] + y, xs)
        else:
            y_s[pl.ds(r0, q), :] = y

    def bwd_chunk(c, second_half):
        r0 = pl.multiple_of(c * q, q)
        dt, xs, bmat, cmat = load_chunk(r0)
        _, rb = cumsums(dt)
        wba = jnp.exp2(rb[0:1, :] - rb) * dt
        y = inter_chunk(hb_s, selb_ref, jnp.exp2(rb), wba, xs, bmat, cmat, 0)
        if second_half:
            finish(r0, y_s[pl.ds(r0, q), :] + y, xs)
        else:
            y_s[pl.ds(r0, q), :] = y

    half = nc // 2
    conv_chunk(nc - 1)

    def first_half(i, carry):
        fwd_chunk(i, False)
        bwd_chunk(nc - 1 - i, False)
        conv_chunk(i + 1)
        conv_chunk(nc - 2 - i)
        return carry

    def second_half(i, carry):
        fwd_chunk(i, True)
        bwd_chunk(nc - 1 - i, True)
        return carry

    lax.fori_loop(0, half - 1, first_half, 0)
    fwd_chunk(jnp.int32(half - 1), False)
    bwd_chunk(jnp.int32(half), False)
    lax.fori_loop(half, nc, second_half, 0)
    hfin_ref[0] = hf_s[...]
    hfin_ref[1] = hb_s[...]


def _ssd(xbc, dt, z, conv_w, conv_b, dtb, alog, dskip, nw, h0, bsz, seq_len):
    assert seq_len % (2 * SSD_CHUNK) == 0, "the two recurrences meet in the middle: even chunk count"
    one = pl.Buffered(1)
    seq = lambda w: pl.BlockSpec((seq_len, w), lambda b: (b, 0), pipeline_mode=one)
    const = lambda r, w: pl.BlockSpec((r, w), lambda b: (0, 0))
    st = pl.BlockSpec((None, 2, SSD_STATE, SSD_INNER), lambda b: (b, 0, 0, 0))
    sels = _ssd_selectors()
    return pl.pallas_call(
        functools.partial(_ssd_kernel, seq_len),
        out_shape=[jax.ShapeDtypeStruct((bsz * seq_len, SSD_INNER), bf16),
                   jax.ShapeDtypeStruct((bsz, 2, SSD_STATE, SSD_INNER), f32)],
        grid=(bsz,),
        in_specs=[pl.BlockSpec((seq_len, SSD_XBC), lambda b: (b, 0)), seq(LANES),
                  pl.BlockSpec((seq_len, SSD_INNER), lambda b: (b, 0)),
                  const(8, SSD_XBC), const(1, SSD_XBC), const(1, LANES), const(1, LANES),
                  const(1, SSD_INNER), const(1, SSD_INNER)]
                 + [const(*s.shape) for s in sels] + [st],
        out_specs=[seq(SSD_INNER), st],
        scratch_shapes=[pltpu.VMEM((seq_len, SSD_INNER), f32),
                        pltpu.VMEM((seq_len, 2 * SSD_BC), bf16),
                        pltpu.VMEM((seq_len, LANES), f32),
                        pltpu.VMEM((seq_len, SSD_INNER), f32),
                        pltpu.VMEM((SSD_STATE, SSD_INNER), f32),
                        pltpu.VMEM((SSD_STATE, SSD_INNER), f32),
                        pltpu.VMEM((SSD_CHUNK + 32, SSD_XBC), bf16)],
        compiler_params=_cparams(("parallel",)),
        name="ssd",
    )(xbc, dt, z, conv_w, conv_b, dtb, alog, dskip, nw, *sels, h0)


def _attn_kernel(n_blocks, nsub, local, q_ref, g_ref, k_ref, kc_ref, *rest):
    nv = nsub + 2 if local else 0
    v_refs = rest[:nv]
    vc_ref, sink_ref, o_ref, s_s = rest[nv:]
    t = ATT_BLOCK
    i0 = pl.program_id(1) * nsub
    rpk = ATT_HEADS // ATT_KV_HEADS
    lane = lax.broadcasted_iota(jnp.int32, (t, LANES), 1)
    lo_half = lane < ATT_HEAD_DIM
    zero_b = jnp.zeros((), bf16)
    kl = lax.broadcasted_iota(jnp.int32, (t, t), 0)
    ql = lax.broadcasted_iota(jnp.int32, (t, t), 1)
    for sub in range(nsub):
        i = i0 + sub
        qv = q_ref[sub * t:(sub + 1) * t, :]
        if local:
            p0 = pl.multiple_of(jnp.maximum(i - 1, 0) * t, t)
            c0 = pl.multiple_of(i * t, t)
            n0 = pl.multiple_of(jnp.minimum(i + 1, n_blocks - 1) * t, t)
            bias_prev = jnp.where((kl >= ql) & (i > 0), 0.0, NEG_INF)
            bias_next = jnp.where((kl <= ql) & (i < n_blocks - 1), 0.0, NEG_INF)
            bias_prev = jnp.concatenate([bias_prev] * rpk, axis=1)
            bias_next = jnp.concatenate([bias_next] * rpk, axis=1)
        for j in range(ATT_KV_HEADS):
            ls = slice(j * LANES, (j + 1) * LANES)
            if local:
                kk = jnp.concatenate([k_ref[pl.ds(p0, t), ls], k_ref[pl.ds(c0, t), ls],
                                      k_ref[pl.ds(n0, t), ls], kc_ref[:, ls]], axis=0)
            else:
                kk = kc_ref[:, ls]
            pieces = []
            for r in range(rpk):
                hq = j * rpk + r
                qp = qv[:, (hq // 2) * LANES:(hq // 2 + 1) * LANES]
                keep = lo_half if hq % 2 == 0 else jnp.logical_not(lo_half)
                pieces.append(jnp.where(keep, qp, zero_b))
            q4 = jnp.concatenate(pieces, axis=0)
            s = lax.dot_general(kk, q4, (((1,), (1,)), ((), ())), preferred_element_type=f32)
            if local:
                s = jnp.concatenate([s[:t] + bias_prev, s[t:2 * t], s[2 * t:3 * t] + bias_next, s[3 * t:]],
                                    axis=0)
            s_s[sub, j] = s
    for sub in range(nsub):
        outs = []
        for j in range(ATT_KV_HEADS):
            vs = slice(j * ATT_HEAD_DIM, (j + 1) * ATT_HEAD_DIM)
            if local:
                vvt = jnp.concatenate([v_refs[sub + k][vs, :] for k in range(3)] + [vc_ref[vs, :]], axis=1)
            else:
                vvt = vc_ref[vs, :]
            sk = jnp.concatenate([jnp.full((1, t), sink_ref[j * rpk + r] * LOG2E, f32) for r in range(rpk)],
                                 axis=1)
            s = s_s[sub, j]
            m = jnp.maximum(jnp.max(s, axis=0, keepdims=True), sk)
            p = jnp.exp2(s - m)
            den = jnp.sum(p, axis=0, keepdims=True) + jnp.exp2(sk - m)
            vvt = jnp.concatenate([vvt, vvt], axis=0)
            ot = jnp.dot(vvt, p.astype(bf16), preferred_element_type=f32) / den
            o4 = [jnp.transpose(ot[:, r * t:(r + 1) * t]) for r in range(rpk)]
            outs.append(jnp.where(lo_half, o4[0], o4[1]))
            outs.append(jnp.where(lo_half, o4[2], o4[3]))
        o = jnp.concatenate(outs, axis=1)
        gv = g_ref[sub * t:(sub + 1) * t, :].astype(f32)
        o_ref[sub * t:(sub + 1) * t, :] = (o * _silu(gv)).astype(o_ref.dtype)


def _attention(q, g, k, vt, kc, vct, sink, bsz, seq_len, local):
    t = ATT_BLOCK
    nb = seq_len // t
    nsub = max(s for s in range(1, ATT_SUB + 1) if nb % s == 0)
    n_ctx = kc.shape[0] // bsz
    kw = ATT_KV_HEADS * LANES
    vw = ATT_KVW
    blk = pl.BlockSpec((nsub * t, ATT_Q), lambda b, i: (b * (nb // nsub) + i, 0))
    full = lambda n: pl.BlockSpec((n, kw), lambda b, i: (b, 0))
    vblk = lambda off: pl.BlockSpec((vw, t), lambda b, i: (0, b * nb + jnp.clip(i * nsub + off, 0, nb - 1)))
    vspecs = [vblk(off) for off in range(-1, nsub + 1)] if local else []
    return pl.pallas_call(
        functools.partial(_attn_kernel, nb, nsub, local),
        out_shape=jax.ShapeDtypeStruct((bsz * seq_len, ATT_Q), bf16),
        grid=(bsz, nb // nsub),
        in_specs=[blk, blk, full(k.shape[0] // bsz), full(n_ctx)] + vspecs
                 + [pl.BlockSpec((vw, n_ctx), lambda b, i: (0, b)), pl.BlockSpec(memory_space=pltpu.SMEM)],
        out_specs=blk,
        scratch_shapes=[pltpu.VMEM((nsub, ATT_KV_HEADS, (3 * t if local else 0) + n_ctx,
                                    (ATT_HEADS // ATT_KV_HEADS) * t), f32)],
        compiler_params=_cparams(("parallel", "arbitrary")),
        name="attention",
    )(q, g, k, kc, *([vt] * len(vspecs)), vct, sink)


def _s5_disc_kernel(lre_ref, lim_ref, ls_ref, bre_ref, bim_ref, abre_ref, abim_ref, bbre_ref, bbim_ref):
    lam_re = lre_ref[...]
    lam_im = lim_ref[...]
    dt = jnp.exp(ls_ref[...])
    mag = jnp.exp(lam_re * dt)
    ab_re = mag * jnp.cos(lam_im * dt)
    ab_im = mag * jnp.sin(lam_im * dt)
    num_re, num_im = ab_re - 1.0, ab_im
    den = lam_re * lam_re + lam_im * lam_im
    coef_re = (num_re * lam_re + num_im * lam_im) / den
    coef_im = (num_im * lam_re - num_re * lam_im) / den
    b_re, b_im = bre_ref[...], bim_ref[...]
    abre_ref[...] = ab_re
    abim_ref[...] = ab_im
    bbre_ref[...] = coef_re * b_re - coef_im * b_im
    bbim_ref[...] = coef_re * b_im + coef_im * b_re


def _s5_discretise(lam_re, lam_im, log_step, b_re, b_im):
    g, n, cg = b_re.shape
    exp = lambda t: jnp.repeat(t.reshape(2 * g, n), cg, axis=1)
    ls = jnp.broadcast_to(log_step.reshape(2 * g, 1), (2 * g, n * cg))
    bb = lambda t: jnp.tile(t.reshape(g, n * cg), (2, 1))
    shp = jax.ShapeDtypeStruct((2 * g, n * cg), f32)
    ab_re, ab_im, bb_re, bb_im = pl.pallas_call(
        _s5_disc_kernel, out_shape=[shp] * 4, name="s5_disc",
    )(exp(lam_re), exp(lam_im), ls, bb(b_re), bb(b_im))
    first = lambda t: t.reshape(2, g, n, cg)[..., 0]
    full = lambda t: t.reshape(2, g, n, cg)
    return first(ab_re), first(ab_im), full(bb_re), full(bb_im)


def _cmul(ar, ai, br, bi):
    return ar * br - ai * bi, ar * bi + ai * br


def _s5_kernel(n_lat, n_ctx, *refs):
    tc = S5_TC
    xl, xc, arow_ref, acol_ref, bbd_ref, cbd_ref, y_ref, wyz_s, ws_s, sl_s, sc_s = refs
    hw = S5_HGRP * S5_STATE
    hl = S5_HGRP * S5_GROUP_CH
    per = LANES // hl
    xw = tc * hl
    nsl = hw // LANES
    nb = 8
    pos = lax.broadcasted_iota(jnp.int32, (1, LANES), 1) // hl

    def pick(arrs):
        out = arrs[0]
        for i in range(1, per):
            out = jnp.where(pos == i, arrs[i], out)
        return out

    def powers(re, im, n):
        out = [(jnp.ones_like(re), jnp.zeros_like(im))]
        for _ in range(n):
            out.append(_cmul(out[-1][0], out[-1][1], re, im))
        return out

    steps = lambda h, p: [p * per + (i - h) % per for i in range(per)]

    prow = []
    for h in range(per):
        pr = [powers(arow_ref[h, 2 * d:2 * d + 1, :], arow_ref[h, 2 * d + 1:2 * d + 2, :], tc) for d in range(2)]
        prow.append(pr)
        for s in range(tc):
            for d, k in ((0, tc - 1 - s), (1, s)):
                wr, wi = _cmul(bbd_ref[2 * d, h], bbd_ref[2 * d + 1, h], *pr[d][k])
                ws_s[h, s * hl:(s + 1) * hl, d * 2 * hw:d * 2 * hw + hw] = wr.astype(bf16)
                ws_s[h, s * hl:(s + 1) * hl, d * 2 * hw + hw:(d + 1) * 2 * hw] = wi.astype(bf16)
        crhs = jnp.concatenate([cbd_ref[0, h], -cbd_ref[1, h]], axis=0).astype(bf16)
        kall = [jnp.dot(ws_s[h, :, d * 2 * hw:(d + 1) * 2 * hw], crhs, preferred_element_type=f32)
                for d in range(2)]
        kf = [kall[0][(tc - 1 - k) * hl:(tc - k) * hl] for k in range(tc)]
        kb = [kall[1][k * hl:(k + 1) * hl] for k in range(tc)]
        lag = lambda s, t: kf[t - s] if t > s else (kb[s - t] if t < s else kf[0] + kb[0])
        for s in range(tc):
            for p in range(tc // per):
                wyz_s[h, s * hl:(s + 1) * hl, p * LANES:(p + 1) * LANES] = (
                    pick([lag(s, t) for t in steps(h, p)]).astype(bf16))
        for d in range(2):
            pc = powers(acol_ref[h, 2 * d], acol_ref[h, 2 * d + 1], tc)
            kk = (lambda t: t + 1) if d == 0 else (lambda t: tc - t)
            r0 = xw + d * 2 * hw
            for p in range(tc // per):
                ar = pick([pc[kk(t)][0] for t in steps(h, p)])
                ai = pick([pc[kk(t)][1] for t in steps(h, p)])
                dre, dim_ = _cmul(cbd_ref[0, h], cbd_ref[1, h], ar, ai)
                wyz_s[h, r0:r0 + hw, p * LANES:(p + 1) * LANES] = dre.astype(bf16)
                wyz_s[h, r0 + hw:r0 + 2 * hw, p * LANES:(p + 1) * LANES] = (-dim_).astype(bf16)

    def rows_of(x_ref, b, n, h):
        return x_ref[b * n:(b + 1) * n, h * xw:(h + 1) * xw]

    def inject(x_ref, s_ref, n):
        for b in range(nb):
            for h in range(per):
                sb = jnp.dot(rows_of(x_ref, b, n, h), ws_s[h], preferred_element_type=f32)
                for k in range(4 * nsl):
                    s_ref[h * 4 * nsl + k, pl.ds(b, n, stride=nb), :] = sb[:, k * LANES:(k + 1) * LANES]

    at = [[[tuple(jnp.broadcast_to(p[:, k * LANES:(k + 1) * LANES], (nb, LANES)) for p in prow[h][d][tc])
            for k in range(nsl)] for d in range(2)] for h in range(per)]

    def scan(s_ref, n, init):
        def step(i, carry):
            idxs = [pl.ds(pl.multiple_of(i * nb, nb), nb), pl.ds(pl.multiple_of((n - 1 - i) * nb, nb), nb)]
            chains = [(h, d, k) for h in range(per) for d in range(2) for k in range(nsl)]
            slab = lambda h, d, k: h * 4 * nsl + d * 2 * nsl + k
            inj = [(s_ref[slab(h, d, k), idxs[d], :], s_ref[slab(h, d, k) + nsl, idxs[d], :])
                   for h, d, k in chains]
            new = []
            for c, (h, d, k) in enumerate(chains):
                hr, hi = carry[2 * c], carry[2 * c + 1]
                s_ref[slab(h, d, k), idxs[d], :] = hr
                s_ref[slab(h, d, k) + nsl, idxs[d], :] = hi
                ar, ai = at[h][d][k]
                new += [ar * hr - ai * hi + inj[c][0], ar * hi + ai * hr + inj[c][1]]
            return tuple(new)
        return lax.fori_loop(0, n, step, init, unroll=4)

    inject(xc, sc_s, n_ctx)
    h_ctx = scan(sc_s, n_ctx, tuple(jnp.zeros((nb, LANES), f32) for _ in range(per * 4 * nsl)))
    inject(xl, sl_s, n_lat)
    scan(sl_s, n_lat, h_ctx)

    for b in range(nb):
        yh = []
        for h in range(per):
            hin = jnp.concatenate([sl_s[h * 4 * nsl + k, pl.ds(b, n_lat, stride=nb), :]
                                   for k in range(4 * nsl)], axis=1).astype(bf16)
            yh.append(jnp.dot(jnp.concatenate([rows_of(xl, b, n_lat, h), hin], axis=1), wyz_s[h],
                              preferred_element_type=f32))
        for p in range(tc // per):
            grp = [yh[h][:, p * LANES:(p + 1) * LANES] for h in range(per)]
            for r in range(per):
                row = pick([grp[(i - r) % per] for i in range(per)])
                if r:
                    row = pltpu.roll(row, (per - r) * hl, axis=1)
                y_ref[pl.ds(b * n_lat * tc + p * per + r, n_lat, stride=tc), :] = row


def _s5_mix(u, u_c, arow, acol, bbd, cbd, bsz, seq_len, n_ctx_tok):
    assert bsz == 8, "the chunk recurrence puts the batch on the 8 sublanes"
    tc = S5_TC
    n_lat, n_ctx = seq_len // tc, n_ctx_tok // tc
    nblk = S5_GROUPS // S5_GBLK
    hw = S5_HGRP * S5_STATE
    hl = S5_HGRP * S5_GROUP_CH
    per = LANES // hl
    nsl = hw // LANES
    one = pl.Buffered(1)
    xspec = lambda rows: pl.BlockSpec((None, rows, tc * LANES), lambda g: (g, 0, 0))
    return pl.pallas_call(
        functools.partial(_s5_kernel, n_lat, n_ctx),
        out_shape=jax.ShapeDtypeStruct((nblk, bsz * seq_len, LANES), f32),
        grid=(nblk,),
        in_specs=[xspec(bsz * n_lat), xspec(bsz * n_ctx),
                  pl.BlockSpec((per, 4, hw), lambda g: (g, 0, 0), pipeline_mode=one),
                  pl.BlockSpec((per, 4, hw, LANES), lambda g: (g, 0, 0, 0), pipeline_mode=one),
                  pl.BlockSpec((4, per, hl, hw), lambda g: (0, g, 0, 0), pipeline_mode=one),
                  pl.BlockSpec((2, per, hw, LANES), lambda g: (0, g, 0, 0), pipeline_mode=one)],
        out_specs=pl.BlockSpec((None, bsz * seq_len, LANES), lambda g: (g, 0, 0)),
        scratch_shapes=[pltpu.VMEM((per, tc * hl + 4 * hw, tc * hl), bf16),
                        pltpu.VMEM((per, tc * hl, 4 * hw), bf16),
                        pltpu.VMEM((per * 4 * nsl, bsz * n_lat, LANES), f32),
                        pltpu.VMEM((per * 4 * nsl, bsz * n_ctx, LANES), f32)],
        compiler_params=_cparams(("arbitrary",)),
        name="s5_mix",
    )(u, u_c, arow, acol, bbd, cbd)


def _s5_block_params(ab_re, ab_im, bb_re, bb_im, c_re, c_im):
    g, n, cg = S5_GROUPS, S5_STATE, S5_GROUP_CH
    nh = g // S5_HGRP
    hw = S5_HGRP * n
    arow = jnp.stack([t[d].reshape(nh, hw) for d in range(2) for t in (ab_re, ab_im)], axis=1)
    acol = jnp.broadcast_to(arow[..., None], (nh, 4, hw, LANES))

    def blockdiag(t, rows_per, cols_per):
        tiled = jnp.concatenate([t] * S5_HGRP, axis=-1)
        r = np.arange(t.shape[-2])[:, None] // rows_per
        c = np.arange(S5_HGRP * cols_per)[None, :] // cols_per
        return jnp.where(jnp.asarray(r == c), tiled, 0.0)

    bb = jnp.stack([t[d] for d in range(2) for t in (bb_re, bb_im)], axis=0)
    bb = jnp.swapaxes(bb.reshape(4, nh, S5_HGRP, n, cg), -1, -2).reshape(4, nh, S5_HGRP * cg, n)
    bbd = blockdiag(bb, cg, n)
    cc = jnp.swapaxes(jnp.stack([c_re, c_im], axis=0).reshape(2, nh, S5_HGRP, cg, n), -1, -2)
    cbd = blockdiag(cc.reshape(2, nh, hw, cg), n, cg)
    return arow, acol, bbd, jnp.concatenate([cbd] * (LANES // (S5_HGRP * cg)), axis=-1)


def _s5_out_kernel(y_ref, u_ref, g_ref, x_ref, dskip_ref, gw_ref, gb_ref, w_ref, gate_ref,
                   fw_ref, o_ref):
    y = jnp.concatenate([y_ref[j] for j in range(y_ref.shape[0])], axis=1)
    y = y + dskip_ref[...] * u_ref[...].astype(f32)
    y = jax.nn.gelu(y)
    glu = jnp.dot(y.astype(bf16), gw_ref[...], preferred_element_type=f32) + gb_ref[...]
    y = y * jax.nn.sigmoid(glu)
    y = y * _silu(g_ref[...].astype(f32))
    x = x_ref[...] + gate_ref[...] * jnp.dot(y.astype(bf16), w_ref[...], preferred_element_type=f32)
    ms = jnp.mean(x * x, axis=-1, keepdims=True)
    o_ref[...] = x * lax.rsqrt(ms + EPS) * fw_ref[...]


def _s5_out(y, u, g, x, dskip, gw, gb, w, gate, fw, rows_per_mod):
    m, d = x.shape
    tm = min(ROW_TILE, rows_per_mod)
    per = rows_per_mod // tm
    row = lambda: pl.BlockSpec((tm, d), lambda i: (i, 0))
    vec = lambda: pl.BlockSpec((1, d), lambda i: (0, 0))
    mat = lambda: pl.BlockSpec((d, d), lambda i: (0, 0), pipeline_mode=pl.Buffered(1))
    return pl.pallas_call(
        _s5_out_kernel,
        out_shape=jax.ShapeDtypeStruct((m, d), f32),
        grid=(m // tm,),
        in_specs=[pl.BlockSpec((y.shape[0], tm, LANES), lambda i: (0, i, 0)),
                  row(), row(), row(), vec(), mat(), vec(), mat(),
                  pl.BlockSpec((None, 1, d), lambda i: (i // per, 0, 0)), vec()],
        out_specs=row(),
        compiler_params=_cparams(("parallel",)),
        name="s5_out",
    )(y, u, g, x, dskip, gw, gb, w, gate, fw)


def _even_weights(w_in):
    o = 0
    z = w_in[:, o:o + SSD_INNER]; o += SSD_INNER
    xbc = w_in[:, o:o + SSD_XBC]; o += SSD_XBC
    dt = w_in[:, o:o + 2 * SSD_HEADS]; o += 2 * SSD_HEADS
    q = w_in[:, o:o + ATT_Q]; o += ATT_Q
    k = w_in[:, o:o + ATT_KVW]; o += ATT_KVW
    v = w_in[:, o:o + ATT_KVW]; o += ATT_KVW
    g = w_in[:, o:o + ATT_Q]
    d = w_in.shape[0]
    dup = lambda t: jnp.concatenate([t.reshape(d, ATT_KV_HEADS, 1, ATT_HEAD_DIM)] * 2, axis=2).reshape(d, -1)
    dtp = jnp.pad(dt, ((0, 0), (0, LANES - 2 * SSD_HEADS)))
    cast = lambda t: t.astype(bf16)
    return [cast(t) for t in (z, xbc, q, dup(k), g, dtp)], cast(v.T)


def _even_segs(rope):
    scale = ATT_HEAD_DIM ** -0.5 * LOG2E
    widths = [(SSD_INNER, None, bf16), (SSD_XBC, None, bf16),
              (ATT_Q, scale, bf16),
              (ATT_KV_HEADS * LANES, 1.0 if rope else None, bf16),
              (ATT_Q, None, bf16), (LANES, None, f32)]
    return [(i, w, r, dtp, False) for i, (w, r, dtp) in enumerate(widths)]


def _pad_lanes(v, n=LANES):
    v = v.reshape(1, -1)
    return jnp.pad(v, ((0, 0), (0, n - v.shape[1])))


def kernel(x, c, ctx, c_ctx, e_norm_w, e_ada_w, e_ada_b, e_w_in, e_conv_w, e_conv_b, e_dt_bias,
           e_a_log, e_d_skip, e_ssd_norm_w, e_sink, e_w_out, o_norm_w, o_ada_w, o_ada_b, o_w_in,
           o_lam_re, o_lam_im, o_log_step, o_b_re, o_b_im, o_c_re, o_c_im, o_d_skip, o_glu_w,
           o_glu_b, o_w_out, final_norm_w):
    bsz, seq_len, d = x.shape
    n_ctx = ctx.shape[1]
    assert e_w_in.shape[0] == 1 and o_w_in.shape[0] == 1, "one even (SSD + attention) and one odd (S5) layer"
    xf = x.reshape(bsz * seq_len, d)
    xcf = ctx.reshape(bsz * n_ctx, d)

    cvecs = jnp.concatenate([c, c_ctx[None, :], jnp.zeros((16 - bsz - 1, d), f32)], axis=0)

    def modulation(ada_w, ada_b):
        mod = _adaln(cvecs, ada_w, ada_b.reshape(1, -1))
        parts = [mod[:, k * d:(k + 1) * d] for k in range(3)]
        lat = [p[:bsz].reshape(bsz, 1, d) for p in parts]
        cx = [p[bsz:bsz + 1].reshape(1, 1, d) for p in parts]
        return lat, cx

    (shift, scale, gate), (shift_c, scale_c, gate_c) = modulation(e_ada_w[0], e_ada_b[0])
    w_in, w_vt = _even_weights(e_w_in[0])
    nw = e_norm_w[0].reshape(1, d)
    tabs = _rope_tables(seq_len)
    vseg = [(0, ATT_KVW, bf16)]
    z, xbc, q, k, g, dt, vt = _inproj(xf, shift, scale, nw, w_in, _even_segs(True), seq_len, tabs, w_vt, vseg)
    z_c, xbc_c, q_c, k_c, g_c, dt_c, vt_c = _inproj(xcf, shift_c, scale_c, nw, w_in, _even_segs(False), bsz * n_ctx,
                                                    None, w_vt, vseg)

    conv_w = jnp.pad(e_conv_w[0], ((0, 8 - SSD_CONV), (0, 0)))
    conv_b = e_conv_b[0].reshape(1, -1)
    dtb = _pad_lanes(e_dt_bias[0])
    alog = _pad_lanes(e_a_log[0])
    dskip = jnp.repeat(e_d_skip[0], SSD_HEAD_DIM).reshape(1, -1)
    snw = e_ssd_norm_w[0].reshape(1, -1)
    h0 = jnp.zeros((bsz, 2, SSD_STATE, SSD_INNER), f32)
    ssd_c, hfin = _ssd(xbc_c, dt_c, z_c, conv_w, conv_b, dtb, alog, dskip, snw, h0, bsz, n_ctx)
    ssd_o, _ = _ssd(xbc, dt, z, conv_w, conv_b, dtb, alog, dskip, snw, hfin, bsz, seq_len)

    sink = e_sink[0]
    att = _attention(q, g, k, vt, k_c, vt_c, sink, bsz, seq_len, True)
    att_c = _attention(q_c, g_c, k_c, vt_c, k_c, vt_c, sink, bsz, n_ctx, False)
    w_out = e_w_out[0].astype(bf16)
    even_gate, even_gate_c = gate, gate_c

    (shift, scale, gate), (shift_c, scale_c, _) = modulation(o_ada_w[0], o_ada_b[0])
    w_u = o_w_in[0][:, :S5_WIDTH].astype(bf16)
    w_g = o_w_in[0][:, S5_WIDTH:].astype(bf16)
    nw = o_norm_w[0].reshape(1, d)
    x1, u, u_ch, g2 = _inproj(xf, shift, scale, nw, [w_u, w_g],
                              [(0, S5_WIDTH, None, bf16, False), (0, S5_WIDTH, None, bf16, True),
                               (1, S5_WIDTH, None, bf16, False)], seq_len,
                              pre=(ssd_o, att, w_out, even_gate))
    _, uc_ch = _inproj(xcf, shift_c, scale_c, nw, [w_u], [(0, S5_WIDTH, None, bf16, True)], bsz * n_ctx,
                       pre=(ssd_c, att_c, w_out, even_gate_c))

    ab_re, ab_im, bb_re, bb_im = _s5_discretise(o_lam_re[0], o_lam_im[0], o_log_step[0], o_b_re[0], o_b_im[0])
    arow, acol, bbd, cbd = _s5_block_params(ab_re, ab_im, bb_re, bb_im, o_c_re[0], o_c_im[0])
    y = _s5_mix(u_ch, uc_ch, arow, acol, bbd, cbd, bsz, seq_len, n_ctx)
    out = _s5_out(y, u, g2, x1, o_d_skip[0].reshape(1, -1),
                  o_glu_w[0].astype(bf16), o_glu_b[0].reshape(1, -1), o_w_out[0].astype(bf16), gate,
                  final_norm_w.reshape(1, -1), seq_len)
    return out.reshape(bsz, seq_len, d)
```

```python
import functools
import math

import jax
import jax.numpy as jnp
import numpy as np
from jax import lax
from jax.experimental import pallas as pl
from jax.experimental.pallas import tpu as pltpu

f32 = jnp.float32
bf16 = jnp.bfloat16

GRID_W = 64
EPS = 1e-6
NEG_INF = -1e30

SSD_HEADS = 16
SSD_HEAD_DIM = 64
SSD_GROUPS = 2
SSD_STATE = 128
SSD_CONV = 5
SSD_CHUNK = 128
SSD_INNER = SSD_HEADS * SSD_HEAD_DIM
SSD_BC = SSD_GROUPS * SSD_STATE
SSD_XBC = SSD_INNER + 2 * SSD_BC
ATT_HEADS = 16
ATT_KV_HEADS = 4
ATT_HEAD_DIM = 64
ATT_BLOCK = 128
ROPE_THETA = 10000.0
ATT_Q = ATT_HEADS * ATT_HEAD_DIM
ATT_KVW = ATT_KV_HEADS * ATT_HEAD_DIM
S5_WIDTH = 1024
S5_GROUP_CH = 16
S5_GROUPS = S5_WIDTH // S5_GROUP_CH
S5_STATE = 64

LOG2E = math.log2(math.e)
LANES = 128
ROW_TILE = 1024
ATT_SUB = 4
S5_GBLK = 8
S5_TC = 8
S5_HGRP = 2
VMEM_LIMIT = 56 * 1024 * 1024


def _cparams(sem, flags=None):
    return pltpu.CompilerParams(dimension_semantics=sem, vmem_limit_bytes=VMEM_LIMIT, flags=flags)


def _silu(x):
    h = 0.5 * x
    return h + h * jnp.tanh(h)


def _adaln_kernel(c_ref, w_ref, b_ref, o_ref):
    c = c_ref[...]
    s = _silu(c).astype(bf16)
    o_ref[...] = jnp.dot(s, w_ref[...].astype(bf16), preferred_element_type=f32) + b_ref[...]


def _adaln(cvecs, w, b):
    r, d = cvecs.shape
    n = w.shape[1]
    tn = 1024
    return pl.pallas_call(
        _adaln_kernel,
        out_shape=jax.ShapeDtypeStruct((r, n), f32),
        grid=(n // tn,),
        in_specs=[pl.BlockSpec((r, d), lambda j: (0, 0)),
                  pl.BlockSpec((d, tn), lambda j: (0, j)),
                  pl.BlockSpec((1, tn), lambda j: (0, j))],
        out_specs=pl.BlockSpec((r, tn), lambda j: (0, j)),
        compiler_params=_cparams(("arbitrary",)),
        name="adaln",
    )(cvecs, w, b)


def _inproj_kernel(segs, tsegs, n_w, has_rope, has_pre, x_ref, shift_ref, scale_ref, nw_ref, *rest):
    if has_pre:
        a_ref, b_ref, wo_ref, gate_ref = rest[:4]
        rest = rest[4:]
    w_refs, rest = rest[:n_w], rest[n_w:]
    if tsegs:
        wt_ref, rest = rest[0], rest[1:]
    if has_rope:
        cos_ref, sina_ref, sinb_ref = rest[:3]
        rest = rest[3:]
    if has_pre:
        x1_ref, rest = rest[0], rest[1:]
    outs = rest[:len(segs)]
    touts = rest[len(segs):len(segs) + len(tsegs)]
    rest = rest[len(segs) + len(tsegs):]
    slab_s = rest[0] if rest else None
    x = x_ref[...]
    if has_pre:
        ka = a_ref.shape[1]
        acc = jnp.dot(a_ref[...], wo_ref[:ka, :], preferred_element_type=f32)
        acc = acc + jnp.dot(b_ref[...], wo_ref[ka:, :], preferred_element_type=f32)
        x = x + gate_ref[...] * acc
        x1_ref[...] = x
    ms = jnp.mean(x * x, axis=-1, keepdims=True)
    h = (x * lax.rsqrt(ms + EPS)) * nw_ref[...]
    h = h * (1.0 + scale_ref[...]) + shift_ref[...]
    hb = h.astype(bf16)
    for (start, width, _), o_ref in zip(tsegs, touts):
        acc_t = lax.dot_general(wt_ref[start:start + width, :], hb, (((1,), (1,)), ((), ())),
                                preferred_element_type=f32)
        o_ref[...] = acc_t.astype(o_ref.dtype)
    products = {}
    for (widx, width, rope, _, chunked), o_ref in zip(segs, outs):
        if widx not in products:
            products[widx] = jnp.dot(hb, w_refs[widx][...], preferred_element_type=f32)
        acc = products[widx]
        if chunked:
            tm = acc.shape[0]
            hl = S5_HGRP * S5_GROUP_CH
            per = LANES // hl
            pos = lax.broadcasted_iota(jnp.int32, (1, LANES), 1) // hl
            for j in range(width // LANES):
                slab_s[j] = acc[:, j * LANES:(j + 1) * LANES]
            for j in range(width // LANES):
                rows = [slab_s[j, pl.ds(s, tm // S5_TC, stride=S5_TC), :] for s in range(S5_TC)]
                for p in range(S5_TC // per):
                    for h in range(per):
                        out = None
                        for i in range(per):
                            piece = rows[p * per + i]
                            if i != h:
                                piece = pltpu.roll(piece, ((i - h) * hl) % LANES, axis=1)
                            out = piece if out is None else jnp.where(pos == i, piece, out)
                        c0 = h * S5_TC * hl + p * LANES
                        o_ref[j, :, c0:c0 + LANES] = out.astype(o_ref.dtype)
            continue
        if rope is not None and not has_rope:
            acc = acc * rope
        elif rope is not None:
            rep = width // LANES
            cos = jnp.concatenate([cos_ref[...]] * rep, axis=1) * rope
            sina = jnp.concatenate([sina_ref[...]] * rep, axis=1) * rope
            sinb = jnp.concatenate([sinb_ref[...]] * rep, axis=1) * rope
            half = ATT_HEAD_DIM // 2
            up = pltpu.roll(acc, width - half, axis=1)
            dn = pltpu.roll(acc, half, axis=1)
            acc = acc * cos + up * sina + dn * sinb
        o_ref[...] = acc.astype(o_ref.dtype)


def _inproj(x, shift, scale, nw, w, segs, rows_per_mod, rope_tabs=None, wt=None, tsegs=(), pre=None):
    m, d = x.shape
    tm = min(ROW_TILE, rows_per_mod)
    per = rows_per_mod // tm
    nmod = shift.shape[0]
    mod_idx = (lambda i: (i // per, 0, 0)) if nmod > 1 else (lambda i: (0, 0, 0))
    in_specs = [pl.BlockSpec((tm, d), lambda i: (i, 0)),
                pl.BlockSpec((None, 1, d), mod_idx),
                pl.BlockSpec((None, 1, d), mod_idx),
                pl.BlockSpec((1, d), lambda i: (0, 0))]
    args = [x, shift, scale, nw]
    if pre is not None:
        a, b, w_out, gate = pre
        gate_idx = (lambda i: (i // per, 0, 0)) if gate.shape[0] > 1 else (lambda i: (0, 0, 0))
        in_specs += [pl.BlockSpec((tm, a.shape[1]), lambda i: (i, 0)),
                     pl.BlockSpec((tm, b.shape[1]), lambda i: (i, 0)),
                     pl.BlockSpec(w_out.shape, lambda i: (0, 0), pipeline_mode=pl.Buffered(1)),
                     pl.BlockSpec((None, 1, d), gate_idx)]
        args += [a, b, w_out, gate]
    in_specs += [pl.BlockSpec(wi.shape, lambda i: (0, 0), pipeline_mode=pl.Buffered(1)) for wi in w]
    args += list(w)
    if tsegs:
        in_specs.append(pl.BlockSpec(wt.shape, lambda i: (0, 0), pipeline_mode=pl.Buffered(1)))
        args.append(wt)
    if rope_tabs is not None:
        for t in rope_tabs:
            in_specs.append(pl.BlockSpec((tm, LANES), lambda i: (i % per, 0)))
            args.append(t)
    out_shape, out_specs = [], []
    if pre is not None:
        out_shape.append(jax.ShapeDtypeStruct((m, d), f32))
        out_specs.append(pl.BlockSpec((tm, d), lambda i: (i, 0)))
    for sg in segs:
        if sg[4]:
            out_shape.append(jax.ShapeDtypeStruct((sg[1] // LANES, m // S5_TC, S5_TC * LANES), sg[3]))
            out_specs.append(pl.BlockSpec((sg[1] // LANES, tm // S5_TC, S5_TC * LANES), lambda i: (0, i, 0)))
        else:
            out_shape.append(jax.ShapeDtypeStruct((m, sg[1]), sg[3]))
            out_specs.append(pl.BlockSpec((tm, sg[1]), lambda i: (i, 0)))
    for _, width, dtp in tsegs:
        out_shape.append(jax.ShapeDtypeStruct((width, m), dtp))
        out_specs.append(pl.BlockSpec((width, tm), lambda i: (0, i)))
    chunked_w = [sg[1] for sg in segs if sg[4]]
    scratch = [pltpu.VMEM((max(chunked_w) // LANES, tm, LANES), f32)] if chunked_w else []
    return pl.pallas_call(
        functools.partial(_inproj_kernel, tuple(segs), tuple(tsegs), len(w), rope_tabs is not None,
                          pre is not None),
        out_shape=out_shape,
        grid=(m // tm,),
        in_specs=in_specs,
        out_specs=out_specs,
        scratch_shapes=scratch,
        compiler_params=_cparams(("parallel",)),
        name="inproj",
    )(*args)


def _rope_tables(seq_len):
    rows = seq_len // GRID_W
    row = jnp.repeat(jnp.arange(rows, dtype=f32), GRID_W)
    col = jnp.tile(jnp.arange(GRID_W, dtype=f32), rows)
    n_freq = ATT_HEAD_DIM // 4
    inv = ROPE_THETA ** (-jnp.arange(n_freq, dtype=f32) / n_freq)
    ang = jnp.concatenate([row[:, None] * inv, col[:, None] * inv], axis=-1)
    cos, sin = jnp.cos(ang), jnp.sin(ang)
    zero = jnp.zeros_like(sin)
    cos_h = jnp.concatenate([cos, cos], axis=-1)
    sina_h = jnp.concatenate([-sin, zero], axis=-1)
    sinb_h = jnp.concatenate([zero, sin], axis=-1)
    two = lambda t: jnp.concatenate([t, t], axis=-1)
    return two(cos_h), two(sina_h), two(sinb_h)


SSD_PACK = 32


def _split3(x):
    hi = x.astype(bf16)
    r1 = x - hi.astype(f32)
    mid = r1.astype(bf16)
    lo = (r1 - mid.astype(f32)).astype(bf16)
    return hi, mid, lo


def _pack3(x):
    hi, mid, lo = _split3(x)
    lane = lax.broadcasted_iota(jnp.int32, x.shape, 1)
    mid_r = pltpu.roll(mid.astype(f32), SSD_PACK, axis=1)
    lo_r = pltpu.roll(lo.astype(f32), 2 * SSD_PACK, axis=1)
    packed = jnp.where(lane < SSD_PACK, hi.astype(f32),
                       jnp.where(lane < 2 * SSD_PACK, mid_r,
                                 jnp.where(lane < 3 * SSD_PACK, lo_r, 0.0)))
    return packed.astype(bf16)


def _ssd_selectors():
    k = np.arange(LANES)
    src = np.where(k < 3 * SSD_PACK, k % SSD_PACK, -1)
    col_blk = np.arange(SSD_PACK * SSD_CHUNK) // SSD_CHUNK
    sel_bc = (src[:, None] == col_blk[None, :])
    head = np.arange(SSD_INNER) // SSD_HEAD_DIM
    sel_f = (src[:, None] == head[None, :])
    sel_b = (src[:, None] == (head + SSD_HEADS)[None, :])
    tri3 = np.tile(np.tril(np.ones((SSD_CHUNK, SSD_CHUNK))), (1, 3))
    rows = np.arange(SSD_CHUNK)[:, None]
    cols = np.arange(SSD_CHUNK + 32)[None, :]
    half = SSD_CONV // 2
    shift = np.concatenate([cols == rows + 16 + d for d in range(-half, half + 1) if d != 0], axis=0)
    as_bf = lambda a: jnp.asarray(a.astype(np.float32), dtype=bf16)
    return as_bf(sel_bc), as_bf(sel_f), as_bf(sel_b), as_bf(tri3), as_bf(shift)


def _ssd_kernel(seq_len, xbc_ref, dt_ref, z_ref, cw_ref, cb_ref, dtb_ref, alog_ref, dskip_ref, nw_ref,
                selbc_ref, self_ref, selb_ref, tri3_ref, shift_ref, h0_ref, out_ref, hfin_ref,
                xs_s, bc_s, dt_s, y_s, hf_s, hb_s, win_s):
    q = SSD_CHUNK
    nc = seq_len // q
    halo = 16
    H, P, N = SSD_HEADS, SSD_HEAD_DIM, SSD_STATE
    gw = (H // SSD_GROUPS) * P
    a2_row = -jnp.exp(alog_ref[...]) * math.log2(math.e)

    def conv_chunk(c):
        r0 = pl.multiple_of(c * q, q)
        pstart = pl.multiple_of(jnp.maximum(r0 - halo, 0), halo)
        nstart = pl.multiple_of(jnp.minimum(r0 + q, seq_len - halo), halo)
        zero = jnp.zeros((), bf16)
        win_s[0:halo, :] = jnp.where(c > 0, xbc_ref[pl.ds(pstart, halo), :], zero)
        win_s[halo:halo + q, :] = xbc_ref[pl.ds(r0, q), :]
        win_s[halo + q:, :] = jnp.where(c < nc - 1, xbc_ref[pl.ds(nstart, halo), :], zero)
        taps = [k for k in range(SSD_CONV) if k != SSD_CONV // 2]
        cw = 2 * LANES
        for j in range(SSD_XBC // cw):
            cs = slice(j * cw, (j + 1) * cw)
            sh = jnp.dot(shift_ref[...], win_s[:, cs], preferred_element_type=f32)
            acc = cb_ref[:, cs] + win_s[halo:halo + q, cs].astype(f32) * cw_ref[SSD_CONV // 2:SSD_CONV // 2 + 1, cs]
            for n, k in enumerate(taps):
                acc = acc + sh[n * q:(n + 1) * q, :] * cw_ref[k:k + 1, cs]
            act = _silu(acc)
            if j < SSD_INNER // cw:
                xs_s[pl.ds(r0, q), cs] = act
            else:
                bc_s[pl.ds(r0, q), j * cw - SSD_INNER:(j + 1) * cw - SSD_INNER] = act.astype(bf16)
        dt_s[pl.ds(r0, q), :] = jax.nn.softplus(dt_ref[pl.ds(r0, q), :] + dtb_ref[...])

    conv_chunk(0)
    hf_s[...] = h0_ref[0]
    hb_s[...] = h0_ref[1]

    ri = lax.broadcasted_iota(jnp.int32, (q, q), 0)
    ci = lax.broadcasted_iota(jnp.int32, (q, q), 1)
    lower = ri >= ci
    upper = ci >= ri
    lane = lax.broadcasted_iota(jnp.int32, (q, LANES), 1)
    lo_half = lane < P

    def cumsums(dt):
        dta = dt * a2_row
        cf = jnp.dot(tri3_ref[...], jnp.concatenate(_split3(dta), axis=0), preferred_element_type=f32)
        rb = cf[q - 1:q, :] - cf + dta
        return cf, rb

    def load_chunk(r0):
        dt = dt_s[pl.ds(r0, q), :]
        xs = xs_s[pl.ds(r0, q), :]
        bcv = bc_s[pl.ds(r0, q), :]
        bmat = [bcv[:, g * N:(g + 1) * N] for g in range(SSD_GROUPS)]
        cmat = [bcv[:, SSD_BC + g * N:SSD_BC + (g + 1) * N] for g in range(SSD_GROUPS)]
        return dt, xs, bmat, cmat

    def inter_chunk(h_s, sel_ref, decay, weight, xs, bmat, cmat, dec_idx):
        ew = jnp.dot(jnp.concatenate([_pack3(decay), _pack3(weight)], axis=0), sel_ref[...],
                     preferred_element_type=f32)
        e_x, w_x = ew[:q], ew[q:]
        hb_ = h_s[...].astype(bf16)
        yoff = jnp.concatenate(
            [jnp.dot(cmat[g], hb_[:, g * gw:(g + 1) * gw], preferred_element_type=f32)
             for g in range(SSD_GROUPS)], axis=1)
        xw = (xs * w_x).astype(bf16)
        dec_row = e_x[dec_idx:dec_idx + 1, :]
        for g in range(SSD_GROUPS):
            gs = slice(g * gw, (g + 1) * gw)
            bt = jnp.transpose(bmat[g].astype(f32)).astype(bf16)
            upd = jnp.dot(bt, xw[:, gs], preferred_element_type=f32)
            h_s[:, gs] = h_s[:, gs] * dec_row[:, gs] + upd
        return yoff * e_x

    def finish(r0, y, xs):
        yy = y + xs * dskip_ref[...]
        zz = z_ref[pl.ds(r0, q), :].astype(f32)
        gated = yy * _silu(zz)
        ms = jnp.mean(gated * gated, axis=-1, keepdims=True)
        out_ref[pl.ds(r0, q), :] = (gated * lax.rsqrt(ms + EPS) * nw_ref[...]).astype(out_ref.dtype)

    def fwd_chunk(c, second_half):
        r0 = pl.multiple_of(c * q, q)
        dt, xs, bmat, cmat = load_chunk(r0)
        cf, rb = cumsums(dt)
        pcol = jnp.where(lane < H, cf, rb)
        bcast = jnp.dot(_pack3(pcol), selbc_ref[...], preferred_element_type=f32)
        prow = jnp.transpose(pcol - jnp.log2(dt))
        cbm = [lax.dot_general(cmat[g], bmat[g], (((1,), (1,)), ((), ())), preferred_element_type=f32)
               for g in range(SSD_GROUPS)]
        xsb = xs.astype(bf16)
        zero_b = jnp.zeros((), bf16)
        ypairs = []
        for k in range(H // 2):
            mats = []
            for h in (2 * k, 2 * k + 1):
                g = h // (H // SSD_GROUPS)
                hb_ = H + h
                segf = bcast[:, h * q:(h + 1) * q] - prow[h:h + 1, :]
                segb = bcast[:, hb_ * q:(hb_ + 1) * q] - prow[hb_:hb_ + 1, :]
                df = jnp.exp2(jnp.where(lower, segf, NEG_INF))
                db = jnp.exp2(jnp.where(upper, segb, NEG_INF))
                mats.append((cbm[g] * (df + db)).astype(bf16))
            xp = xsb[:, k * LANES:(k + 1) * LANES]
            xbd = jnp.concatenate([jnp.where(lo_half, xp, zero_b), jnp.where(lo_half, zero_b, xp)], axis=0)
            ypairs.append(jnp.dot(jnp.concatenate(mats, axis=1), xbd, preferred_element_type=f32))
        y = jnp.concatenate(ypairs, axis=1)
        wfa = jnp.exp2(cf[q - 1:q, :] - cf) * dt
        y = y + inter_chunk(hf_s, self_ref, jnp.exp2(cf), wfa, xs, bmat, cmat, q - 1)
        if second_half:
            finish(r0, y_s[pl.ds(r0, q), :] + y, xs)
        else:
            y_s[pl.ds(r0, q), :] = y

    def bwd_chunk(c, second_half):
        r0 = pl.multiple_of(c * q, q)
        dt, xs, bmat, cmat = load_chunk(r0)
        _, rb = cumsums(dt)
        wba = jnp.exp2(rb[0:1, :] - rb) * dt
        y = inter_chunk(hb_s, selb_ref, jnp.exp2(rb), wba, xs, bmat, cmat, 0)
        if second_half:
            finish(r0, y_s[pl.ds(r0, q), :] + y, xs)
        else:
            y_s[pl.ds(r0, q), :] = y

    half = nc // 2
    conv_chunk(nc - 1)

    def first_half(i, carry):
        fwd_chunk(i, False)
        bwd_chunk(nc - 1 - i, False)
        conv_chunk(i + 1)
        conv_chunk(nc - 2 - i)
        return carry

    def second_half(i, carry):
        fwd_chunk(i, True)
        bwd_chunk(nc - 1 - i, True)
        return carry

    lax.fori_loop(0, half - 1, first_half, 0)
    fwd_chunk(jnp.int32(half - 1), False)
    bwd_chunk(jnp.int32(half), False)
    lax.fori_loop(half, nc, second_half, 0)
    hfin_ref[0] = hf_s[...]
    hfin_ref[1] = hb_s[...]


def _ssd(xbc, dt, z, conv_w, conv_b, dtb, alog, dskip, nw, h0, bsz, seq_len):
    assert seq_len % (2 * SSD_CHUNK) == 0, "the two recurrences meet in the middle: even chunk count"
    one = pl.Buffered(1)
    seq = lambda w: pl.BlockSpec((seq_len, w), lambda b: (b, 0), pipeline_mode=one)
    const = lambda r, w: pl.BlockSpec((r, w), lambda b: (0, 0))
    st = pl.BlockSpec((None, 2, SSD_STATE, SSD_INNER), lambda b: (b, 0, 0, 0))
    sels = _ssd_selectors()
    return pl.pallas_call(
        functools.partial(_ssd_kernel, seq_len),
        out_shape=[jax.ShapeDtypeStruct((bsz * seq_len, SSD_INNER), bf16),
                   jax.ShapeDtypeStruct((bsz, 2, SSD_STATE, SSD_INNER), f32)],
        grid=(bsz,),
        in_specs=[pl.BlockSpec((seq_len, SSD_XBC), lambda b: (b, 0)), seq(LANES),
                  pl.BlockSpec((seq_len, SSD_INNER), lambda b: (b, 0)),
                  const(8, SSD_XBC), const(1, SSD_XBC), const(1, LANES), const(1, LANES),
                  const(1, SSD_INNER), const(1, SSD_INNER)]
                 + [const(*s.shape) for s in sels] + [st],
        out_specs=[seq(SSD_INNER), st],
        scratch_shapes=[pltpu.VMEM((seq_len, SSD_INNER), f32),
                        pltpu.VMEM((seq_len, 2 * SSD_BC), bf16),
                        pltpu.VMEM((seq_len, LANES), f32),
                        pltpu.VMEM((seq_len, SSD_INNER), f32),
                        pltpu.VMEM((SSD_STATE, SSD_INNER), f32),
                        pltpu.VMEM((SSD_STATE, SSD_INNER), f32),
                        pltpu.VMEM((SSD_CHUNK + 32, SSD_XBC), bf16)],
        compiler_params=_cparams(("parallel",)),
        name="ssd",
    )(xbc, dt, z, conv_w, conv_b, dtb, alog, dskip, nw, *sels, h0)


def _attn_kernel(n_blocks, nsub, local, q_ref, g_ref, k_ref, kc_ref, *rest):
    nv = nsub + 2 if local else 0
    v_refs = rest[:nv]
    vc_ref, sink_ref, o_ref, s_s = rest[nv:]
    t = ATT_BLOCK
    i0 = pl.program_id(1) * nsub
    rpk = ATT_HEADS // ATT_KV_HEADS
    lane = lax.broadcasted_iota(jnp.int32, (t, LANES), 1)
    lo_half = lane < ATT_HEAD_DIM
    zero_b = jnp.zeros((), bf16)
    kl = lax.broadcasted_iota(jnp.int32, (t, t), 0)
    ql = lax.broadcasted_iota(jnp.int32, (t, t), 1)
    for sub in range(nsub):
        i = i0 + sub
        qv = q_ref[sub * t:(sub + 1) * t, :]
        if local:
            p0 = pl.multiple_of(jnp.maximum(i - 1, 0) * t, t)
            c0 = pl.multiple_of(i * t, t)
            n0 = pl.multiple_of(jnp.minimum(i + 1, n_blocks - 1) * t, t)
            bias_prev = jnp.where((kl >= ql) & (i > 0), 0.0, NEG_INF)
            bias_next = jnp.where((kl <= ql) & (i < n_blocks - 1), 0.0, NEG_INF)
            bias_prev = jnp.concatenate([bias_prev] * rpk, axis=1)
            bias_next = jnp.concatenate([bias_next] * rpk, axis=1)
        for j in range(ATT_KV_HEADS):
            ls = slice(j * LANES, (j + 1) * LANES)
            if local:
                kk = jnp.concatenate([k_ref[pl.ds(p0, t), ls], k_ref[pl.ds(c0, t), ls],
                                      k_ref[pl.ds(n0, t), ls], kc_ref[:, ls]], axis=0)
            else:
                kk = kc_ref[:, ls]
            pieces = []
            for r in range(rpk):
                hq = j * rpk + r
                qp = qv[:, (hq // 2) * LANES:(hq // 2 + 1) * LANES]
                keep = lo_half if hq % 2 == 0 else jnp.logical_not(lo_half)
                pieces.append(jnp.where(keep, qp, zero_b))
            q4 = jnp.concatenate(pieces, axis=0)
            s = lax.dot_general(kk, q4, (((1,), (1,)), ((), ())), preferred_element_type=f32)
            if local:
                s = jnp.concatenate([s[:t] + bias_prev, s[t:2 * t], s[2 * t:3 * t] + bias_next, s[3 * t:]],
                                    axis=0)
            s_s[sub, j] = s
    for sub in range(nsub):
        outs = []
        for j in range(ATT_KV_HEADS):
            vs = slice(j * ATT_HEAD_DIM, (j + 1) * ATT_HEAD_DIM)
            if local:
                vvt = jnp.concatenate([v_refs[sub + k][vs, :] for k in range(3)] + [vc_ref[vs, :]], axis=1)
            else:
                vvt = vc_ref[vs, :]
            sk = jnp.concatenate([jnp.full((1, t), sink_ref[j * rpk + r] * LOG2E, f32) for r in range(rpk)],
                                 axis=1)
            s = s_s[sub, j]
            m = jnp.maximum(jnp.max(s, axis=0, keepdims=True), sk)
            p = jnp.exp2(s - m)
            den = jnp.sum(p, axis=0, keepdims=True) + jnp.exp2(sk - m)
            vvt = jnp.concatenate([vvt, vvt], axis=0)
            ot = jnp.dot(vvt, p.astype(bf16), preferred_element_type=f32) / den
            o4 = [jnp.transpose(ot[:, r * t:(r + 1) * t]) for r in range(rpk)]
            outs.append(jnp.where(lo_half, o4[0], o4[1]))
            outs.append(jnp.where(lo_half, o4[2], o4[3]))
        o = jnp.concatenate(outs, axis=1)
        gv = g_ref[sub * t:(sub + 1) * t, :].astype(f32)
        o_ref[sub * t:(sub + 1) * t, :] = (o * _silu(gv)).astype(o_ref.dtype)


def _attention(q, g, k, vt, kc, vct, sink, bsz, seq_len, local):
    t = ATT_BLOCK
    nb = seq_len // t
    nsub = max(s for s in range(1, ATT_SUB + 1) if nb % s == 0)
    n_ctx = kc.shape[0] // bsz
    kw = ATT_KV_HEADS * LANES
    vw = ATT_KVW
    blk = pl.BlockSpec((nsub * t, ATT_Q), lambda b, i: (b * (nb // nsub) + i, 0))
    full = lambda n: pl.BlockSpec((n, kw), lambda b, i: (b, 0))
    vblk = lambda off: pl.BlockSpec((vw, t), lambda b, i: (0, b * nb + jnp.clip(i * nsub + off, 0, nb - 1)))
    vspecs = [vblk(off) for off in range(-1, nsub + 1)] if local else []
    return pl.pallas_call(
        functools.partial(_attn_kernel, nb, nsub, local),
        out_shape=jax.ShapeDtypeStruct((bsz * seq_len, ATT_Q), bf16),
        grid=(bsz, nb // nsub),
        in_specs=[blk, blk, full(k.shape[0] // bsz), full(n_ctx)] + vspecs
                 + [pl.BlockSpec((vw, n_ctx), lambda b, i: (0, b)), pl.BlockSpec(memory_space=pltpu.SMEM)],
        out_specs=blk,
        scratch_shapes=[pltpu.VMEM((nsub, ATT_KV_HEADS, (3 * t if local else 0) + n_ctx,
                                    (ATT_HEADS // ATT_KV_HEADS) * t), f32)],
        compiler_params=_cparams(("parallel", "arbitrary")),
        name="attention",
    )(q, g, k, kc, *([vt] * len(vspecs)), vct, sink)


def _s5_disc_kernel(lre_ref, lim_ref, ls_ref, bre_ref, bim_ref, abre_ref, abim_ref, bbre_ref, bbim_ref):
    lam_re = lre_ref[...]
    lam_im = lim_ref[...]
    dt = jnp.exp(ls_ref[...])
    mag = jnp.exp(lam_re * dt)
    ab_re = mag * jnp.cos(lam_im * dt)
    ab_im = mag * jnp.sin(lam_im * dt)
    num_re, num_im = ab_re - 1.0, ab_im
    den = lam_re * lam_re + lam_im * lam_im
    coef_re = (num_re * lam_re + num_im * lam_im) / den
    coef_im = (num_im * lam_re - num_re * lam_im) / den
    b_re, b_im = bre_ref[...], bim_ref[...]
    abre_ref[...] = ab_re
    abim_ref[...] = ab_im
    bbre_ref[...] = coef_re * b_re - coef_im * b_im
    bbim_ref[...] = coef_re * b_im + coef_im * b_re


def _s5_discretise(lam_re, lam_im, log_step, b_re, b_im):
    g, n, cg = b_re.shape
    exp = lambda t: jnp.repeat(t.reshape(2 * g, n), cg, axis=1)
    ls = jnp.broadcast_to(log_step.reshape(2 * g, 1), (2 * g, n * cg))
    bb = lambda t: jnp.tile(t.reshape(g, n * cg), (2, 1))
    shp = jax.ShapeDtypeStruct((2 * g, n * cg), f32)
    ab_re, ab_im, bb_re, bb_im = pl.pallas_call(
        _s5_disc_kernel, out_shape=[shp] * 4, name="s5_disc",
    )(exp(lam_re), exp(lam_im), ls, bb(b_re), bb(b_im))
    first = lambda t: t.reshape(2, g, n, cg)[..., 0]
    full = lambda t: t.reshape(2, g, n, cg)
    return first(ab_re), first(ab_im), full(bb_re), full(bb_im)


def _cmul(ar, ai, br, bi):
    return ar * br - ai * bi, ar * bi + ai * br


def _s5_kernel(n_lat, n_ctx, *refs):
    tc = S5_TC
    xl, xc, arow_ref, acol_ref, bbd_ref, cbd_ref, y_ref, wyz_s, ws_s, sl_s, sc_s = refs
    hw = S5_HGRP * S5_STATE
    hl = S5_HGRP * S5_GROUP_CH
    per = LANES // hl
    xw = tc * hl
    nsl = hw // LANES
    nb = 8
    pos = lax.broadcasted_iota(jnp.int32, (1, LANES), 1) // hl

    def pick(arrs):
        out = arrs[0]
        for i in range(1, per):
            out = jnp.where(pos == i, arrs[i], out)
        return out

    def powers(re, im, n):
        out = [(jnp.ones_like(re), jnp.zeros_like(im))]
        for _ in range(n):
            out.append(_cmul(out[-1][0], out[-1][1], re, im))
        return out

    steps = lambda h, p: [p * per + (i - h) % per for i in range(per)]

    prow = []
    for h in range(per):
        pr = [powers(arow_ref[h, 2 * d:2 * d + 1, :], arow_ref[h, 2 * d + 1:2 * d + 2, :], tc) for d in range(2)]
        prow.append(pr)
        for s in range(tc):
            for d, k in ((0, tc - 1 - s), (1, s)):
                wr, wi = _cmul(bbd_ref[2 * d, h], bbd_ref[2 * d + 1, h], *pr[d][k])
                ws_s[h, s * hl:(s + 1) * hl, d * 2 * hw:d * 2 * hw + hw] = wr.astype(bf16)
                ws_s[h, s * hl:(s + 1) * hl, d * 2 * hw + hw:(d + 1) * 2 * hw] = wi.astype(bf16)
        crhs = jnp.concatenate([cbd_ref[0, h], -cbd_ref[1, h]], axis=0).astype(bf16)
        kall = [jnp.dot(ws_s[h, :, d * 2 * hw:(d + 1) * 2 * hw], crhs, preferred_element_type=f32)
                for d in range(2)]
        kf = [kall[0][(tc - 1 - k) * hl:(tc - k) * hl] for k in range(tc)]
        kb = [kall[1][k * hl:(k + 1) * hl] for k in range(tc)]
        lag = lambda s, t: kf[t - s] if t > s else (kb[s - t] if t < s else kf[0] + kb[0])
        for s in range(tc):
            for p in range(tc // per):
                wyz_s[h, s * hl:(s + 1) * hl, p * LANES:(p + 1) * LANES] = (
                    pick([lag(s, t) for t in steps(h, p)]).astype(bf16))
        for d in range(2):
            pc = powers(acol_ref[h, 2 * d], acol_ref[h, 2 * d + 1], tc)
            kk = (lambda t: t + 1) if d == 0 else (lambda t: tc - t)
            r0 = xw + d * 2 * hw
            for p in range(tc // per):
                ar = pick([pc[kk(t)][0] for t in steps(h, p)])
                ai = pick([pc[kk(t)][1] for t in steps(h, p)])
                dre, dim_ = _cmul(cbd_ref[0, h], cbd_ref[1, h], ar, ai)
                wyz_s[h, r0:r0 + hw, p * LANES:(p + 1) * LANES] = dre.astype(bf16)
                wyz_s[h, r0 + hw:r0 + 2 * hw, p * LANES:(p + 1) * LANES] = (-dim_).astype(bf16)

    def rows_of(x_ref, b, n, h):
        return x_ref[b * n:(b + 1) * n, h * xw:(h + 1) * xw]

    def inject(x_ref, s_ref, n):
        for h in range(per):
            sall = jnp.dot(x_ref[:, h * xw:(h + 1) * xw], ws_s[h], preferred_element_type=f32)
            for b in range(nb):
                for k in range(4 * nsl):
                    s_ref[h * 4 * nsl + k, pl.ds(b, n, stride=nb), :] = sall[b * n:(b + 1) * n,
                                                                             k * LANES:(k + 1) * LANES]

    at = [[[tuple(jnp.broadcast_to(p[:, k * LANES:(k + 1) * LANES], (nb, LANES)) for p in prow[h][d][tc])
            for k in range(nsl)] for d in range(2)] for h in range(per)]

    def scan(s_ref, n, init):
        def step(i, carry):
            idxs = [pl.ds(pl.multiple_of(i * nb, nb), nb), pl.ds(pl.multiple_of((n - 1 - i) * nb, nb), nb)]
            chains = [(h, d, k) for h in range(per) for d in range(2) for k in range(nsl)]
            slab = lambda h, d, k: h * 4 * nsl + d * 2 * nsl + k
            inj = [(s_ref[slab(h, d, k), idxs[d], :], s_ref[slab(h, d, k) + nsl, idxs[d], :])
                   for h, d, k in chains]
            new = []
            for c, (h, d, k) in enumerate(chains):
                hr, hi = carry[2 * c], carry[2 * c + 1]
                s_ref[slab(h, d, k), idxs[d], :] = hr
                s_ref[slab(h, d, k) + nsl, idxs[d], :] = hi
                ar, ai = at[h][d][k]
                new += [ar * hr - ai * hi + inj[c][0], ar * hi + ai * hr + inj[c][1]]
            return tuple(new)
        return lax.fori_loop(0, n, step, init, unroll=4)

    inject(xc, sc_s, n_ctx)
    h_ctx = scan(sc_s, n_ctx, tuple(jnp.zeros((nb, LANES), f32) for _ in range(per * 4 * nsl)))
    inject(xl, sl_s, n_lat)
    scan(sl_s, n_lat, h_ctx)

    for b in range(nb):
        yh = []
        for h in range(per):
            hin = jnp.concatenate([sl_s[h * 4 * nsl + k, pl.ds(b, n_lat, stride=nb), :]
                                   for k in range(4 * nsl)], axis=1).astype(bf16)
            yh.append(jnp.dot(jnp.concatenate([rows_of(xl, b, n_lat, h), hin], axis=1), wyz_s[h],
                              preferred_element_type=f32))
        for p in range(tc // per):
            grp = [yh[h][:, p * LANES:(p + 1) * LANES] for h in range(per)]
            for r in range(per):
                row = pick([grp[(i - r) % per] for i in range(per)])
                if r:
                    row = pltpu.roll(row, (per - r) * hl, axis=1)
                y_ref[pl.ds(b * n_lat * tc + p * per + r, n_lat, stride=tc), :] = row


def _s5_mix(u, u_c, arow, acol, bbd, cbd, bsz, seq_len, n_ctx_tok):
    assert bsz == 8, "the chunk recurrence puts the batch on the 8 sublanes"
    tc = S5_TC
    n_lat, n_ctx = seq_len // tc, n_ctx_tok // tc
    nblk = S5_GROUPS // S5_GBLK
    hw = S5_HGRP * S5_STATE
    hl = S5_HGRP * S5_GROUP_CH
    per = LANES // hl
    nsl = hw // LANES
    one = pl.Buffered(1)
    xspec = lambda rows: pl.BlockSpec((None, rows, tc * LANES), lambda g: (g, 0, 0))
    return pl.pallas_call(
        functools.partial(_s5_kernel, n_lat, n_ctx),
        out_shape=jax.ShapeDtypeStruct((nblk, bsz * seq_len, LANES), f32),
        grid=(nblk,),
        in_specs=[xspec(bsz * n_lat), xspec(bsz * n_ctx),
                  pl.BlockSpec((per, 4, hw), lambda g: (g, 0, 0), pipeline_mode=one),
                  pl.BlockSpec((per, 4, hw, LANES), lambda g: (g, 0, 0, 0), pipeline_mode=one),
                  pl.BlockSpec((4, per, hl, hw), lambda g: (0, g, 0, 0), pipeline_mode=one),
                  pl.BlockSpec((2, per, hw, LANES), lambda g: (0, g, 0, 0), pipeline_mode=one)],
        out_specs=pl.BlockSpec((None, bsz * seq_len, LANES), lambda g: (g, 0, 0)),
        scratch_shapes=[pltpu.VMEM((per, tc * hl + 4 * hw, tc * hl), bf16),
                        pltpu.VMEM((per, tc * hl, 4 * hw), bf16),
                        pltpu.VMEM((per * 4 * nsl, bsz * n_lat, LANES), f32),
                        pltpu.VMEM((per * 4 * nsl, bsz * n_ctx, LANES), f32)],
        compiler_params=_cparams(("arbitrary",)),
        name="s5_mix",
    )(u, u_c, arow, acol, bbd, cbd)


def _s5_block_params(ab_re, ab_im, bb_re, bb_im, c_re, c_im):
    g, n, cg = S5_GROUPS, S5_STATE, S5_GROUP_CH
    nh = g // S5_HGRP
    hw = S5_HGRP * n
    arow = jnp.stack([t[d].reshape(nh, hw) for d in range(2) for t in (ab_re, ab_im)], axis=1)
    acol = jnp.broadcast_to(arow[..., None], (nh, 4, hw, LANES))

    def blockdiag(t, rows_per, cols_per):
        tiled = jnp.concatenate([t] * S5_HGRP, axis=-1)
        r = np.arange(t.shape[-2])[:, None] // rows_per
        c = np.arange(S5_HGRP * cols_per)[None, :] // cols_per
        return jnp.where(jnp.asarray(r == c), tiled, 0.0)

    bb = jnp.stack([t[d] for d in range(2) for t in (bb_re, bb_im)], axis=0)
    bb = jnp.swapaxes(bb.reshape(4, nh, S5_HGRP, n, cg), -1, -2).reshape(4, nh, S5_HGRP * cg, n)
    bbd = blockdiag(bb, cg, n)
    cc = jnp.swapaxes(jnp.stack([c_re, c_im], axis=0).reshape(2, nh, S5_HGRP, cg, n), -1, -2)
    cbd = blockdiag(cc.reshape(2, nh, hw, cg), n, cg)
    return arow, acol, bbd, jnp.concatenate([cbd] * (LANES // (S5_HGRP * cg)), axis=-1)


def _s5_out_kernel(y_ref, u_ref, g_ref, x_ref, dskip_ref, gw_ref, gb_ref, w_ref, gate_ref,
                   fw_ref, o_ref):
    y = jnp.concatenate([y_ref[j] for j in range(y_ref.shape[0])], axis=1)
    y = y + dskip_ref[...] * u_ref[...].astype(f32)
    y = jax.nn.gelu(y)
    glu = jnp.dot(y.astype(bf16), gw_ref[...], preferred_element_type=f32) + gb_ref[...]
    y = y * jax.nn.sigmoid(glu)
    y = y * _silu(g_ref[...].astype(f32))
    x = x_ref[...] + gate_ref[...] * jnp.dot(y.astype(bf16), w_ref[...], preferred_element_type=f32)
    ms = jnp.mean(x * x, axis=-1, keepdims=True)
    o_ref[...] = x * lax.rsqrt(ms + EPS) * fw_ref[...]


def _s5_out(y, u, g, x, dskip, gw, gb, w, gate, fw, rows_per_mod):
    m, d = x.shape
    tm = min(ROW_TILE, rows_per_mod)
    per = rows_per_mod // tm
    row = lambda: pl.BlockSpec((tm, d), lambda i: (i, 0))
    vec = lambda: pl.BlockSpec((1, d), lambda i: (0, 0))
    mat = lambda: pl.BlockSpec((d, d), lambda i: (0, 0), pipeline_mode=pl.Buffered(1))
    return pl.pallas_call(
        _s5_out_kernel,
        out_shape=jax.ShapeDtypeStruct((m, d), f32),
        grid=(m // tm,),
        in_specs=[pl.BlockSpec((y.shape[0], tm, LANES), lambda i: (0, i, 0)),
                  row(), row(), row(), vec(), mat(), vec(), mat(),
                  pl.BlockSpec((None, 1, d), lambda i: (i // per, 0, 0)), vec()],
        out_specs=row(),
        compiler_params=_cparams(("parallel",)),
        name="s5_out",
    )(y, u, g, x, dskip, gw, gb, w, gate, fw)


def _even_weights(w_in):
    o = 0
    z = w_in[:, o:o + SSD_INNER]; o += SSD_INNER
    xbc = w_in[:, o:o + SSD_XBC]; o += SSD_XBC
    dt = w_in[:, o:o + 2 * SSD_HEADS]; o += 2 * SSD_HEADS
    q = w_in[:, o:o + ATT_Q]; o += ATT_Q
    k = w_in[:, o:o + ATT_KVW]; o += ATT_KVW
    v = w_in[:, o:o + ATT_KVW]; o += ATT_KVW
    g = w_in[:, o:o + ATT_Q]
    d = w_in.shape[0]
    dup = lambda t: jnp.concatenate([t.reshape(d, ATT_KV_HEADS, 1, ATT_HEAD_DIM)] * 2, axis=2).reshape(d, -1)
    dtp = jnp.pad(dt, ((0, 0), (0, LANES - 2 * SSD_HEADS)))
    cast = lambda t: t.astype(bf16)
    return [cast(t) for t in (z, xbc, q, dup(k), g, dtp)], cast(v.T)


def _even_segs(rope):
    scale = ATT_HEAD_DIM ** -0.5 * LOG2E
    widths = [(SSD_INNER, None, bf16), (SSD_XBC, None, bf16),
              (ATT_Q, scale, bf16),
              (ATT_KV_HEADS * LANES, 1.0 if rope else None, bf16),
              (ATT_Q, None, bf16), (LANES, None, f32)]
    return [(i, w, r, dtp, False) for i, (w, r, dtp) in enumerate(widths)]


def _pad_lanes(v, n=LANES):
    v = v.reshape(1, -1)
    return jnp.pad(v, ((0, 0), (0, n - v.shape[1])))


def kernel(x, c, ctx, c_ctx, e_norm_w, e_ada_w, e_ada_b, e_w_in, e_conv_w, e_conv_b, e_dt_bias,
           e_a_log, e_d_skip, e_ssd_norm_w, e_sink, e_w_out, o_norm_w, o_ada_w, o_ada_b, o_w_in,
           o_lam_re, o_lam_im, o_log_step, o_b_re, o_b_im, o_c_re, o_c_im, o_d_skip, o_glu_w,
           o_glu_b, o_w_out, final_norm_w):
    bsz, seq_len, d = x.shape
    n_ctx = ctx.shape[1]
    assert e_w_in.shape[0] == 1 and o_w_in.shape[0] == 1, "one even (SSD + attention) and one odd (S5) layer"
    xf = x.reshape(bsz * seq_len, d)
    xcf = ctx.reshape(bsz * n_ctx, d)

    cvecs = jnp.concatenate([c, c_ctx[None, :], jnp.zeros((16 - bsz - 1, d), f32)], axis=0)

    def modulation(ada_w, ada_b):
        mod = _adaln(cvecs, ada_w, ada_b.reshape(1, -1))
        parts = [mod[:, k * d:(k + 1) * d] for k in range(3)]
        lat = [p[:bsz].reshape(bsz, 1, d) for p in parts]
        cx = [p[bsz:bsz + 1].reshape(1, 1, d) for p in parts]
        return lat, cx

    (shift, scale, gate), (shift_c, scale_c, gate_c) = modulation(e_ada_w[0], e_ada_b[0])
    w_in, w_vt = _even_weights(e_w_in[0])
    nw = e_norm_w[0].reshape(1, d)
    tabs = _rope_tables(seq_len)
    vseg = [(0, ATT_KVW, bf16)]
    z, xbc, q, k, g, dt, vt = _inproj(xf, shift, scale, nw, w_in, _even_segs(True), seq_len, tabs, w_vt, vseg)
    z_c, xbc_c, q_c, k_c, g_c, dt_c, vt_c = _inproj(xcf, shift_c, scale_c, nw, w_in, _even_segs(False), bsz * n_ctx,
                                                    None, w_vt, vseg)

    conv_w = jnp.pad(e_conv_w[0], ((0, 8 - SSD_CONV), (0, 0)))
    conv_b = e_conv_b[0].reshape(1, -1)
    dtb = _pad_lanes(e_dt_bias[0])
    alog = _pad_lanes(e_a_log[0])
    dskip = jnp.repeat(e_d_skip[0], SSD_HEAD_DIM).reshape(1, -1)
    snw = e_ssd_norm_w[0].reshape(1, -1)
    h0 = jnp.zeros((bsz, 2, SSD_STATE, SSD_INNER), f32)
    ssd_c, hfin = _ssd(xbc_c, dt_c, z_c, conv_w, conv_b, dtb, alog, dskip, snw, h0, bsz, n_ctx)
    ssd_o, _ = _ssd(xbc, dt, z, conv_w, conv_b, dtb, alog, dskip, snw, hfin, bsz, seq_len)

    sink = e_sink[0]
    att = _attention(q, g, k, vt, k_c, vt_c, sink, bsz, seq_len, True)
    att_c = _attention(q_c, g_c, k_c, vt_c, k_c, vt_c, sink, bsz, n_ctx, False)
    w_out = e_w_out[0].astype(bf16)
    even_gate, even_gate_c = gate, gate_c

    (shift, scale, gate), (shift_c, scale_c, _) = modulation(o_ada_w[0], o_ada_b[0])
    w_u = o_w_in[0][:, :S5_WIDTH].astype(bf16)
    w_g = o_w_in[0][:, S5_WIDTH:].astype(bf16)
    nw = o_norm_w[0].reshape(1, d)
    x1, u, u_ch, g2 = _inproj(xf, shift, scale, nw, [w_u, w_g],
                              [(0, S5_WIDTH, None, bf16, False), (0, S5_WIDTH, None, bf16, True),
                               (1, S5_WIDTH, None, bf16, False)], seq_len,
                              pre=(ssd_o, att, w_out, even_gate))
    _, uc_ch = _inproj(xcf, shift_c, scale_c, nw, [w_u], [(0, S5_WIDTH, None, bf16, True)], bsz * n_ctx,
                       pre=(ssd_c, att_c, w_out, even_gate_c))

    ab_re, ab_im, bb_re, bb_im = _s5_discretise(o_lam_re[0], o_lam_im[0], o_log_step[0], o_b_re[0], o_b_im[0])
    arow, acol, bbd, cbd = _s5_block_params(ab_re, ab_im, bb_re, bb_im, o_c_re[0], o_c_im[0])
    y = _s5_mix(u_ch, uc_ch, arow, acol, bbd, cbd, bsz, seq_len, n_ctx)
    out = _s5_out(y, u, g2, x1, o_d_skip[0].reshape(1, -1),
                  o_glu_w[0].astype(bf16), o_glu_b[0].reshape(1, -1), o_w_out[0].astype(bf16), gate,
                  final_norm_w.reshape(1, -1), seq_len)
    return out.reshape(bsz, seq_len, d)
```

```python
import functools
import math

import jax
import jax.numpy as jnp
import numpy as np
from jax import lax
from jax.experimental import pallas as pl
from jax.experimental.pallas import tpu as pltpu

f32 = jnp.float32
bf16 = jnp.bfloat16

GRID_W = 64
EPS = 1e-6
NEG_INF = -1e30

SSD_HEADS = 16
SSD_HEAD_DIM = 64
SSD_GROUPS = 2
SSD_STATE = 128
SSD_CONV = 5
SSD_CHUNK = 128
SSD_INNER = SSD_HEADS * SSD_HEAD_DIM
SSD_BC = SSD_GROUPS * SSD_STATE
SSD_XBC = SSD_INNER + 2 * SSD_BC
ATT_HEADS = 16
ATT_KV_HEADS = 4
ATT_HEAD_DIM = 64
ATT_BLOCK = 128
ROPE_THETA = 10000.0
ATT_Q = ATT_HEADS * ATT_HEAD_DIM
ATT_KVW = ATT_KV_HEADS * ATT_HEAD_DIM
S5_WIDTH = 1024
S5_GROUP_CH = 16
S5_GROUPS = S5_WIDTH // S5_GROUP_CH
S5_STATE = 64

LOG2E = math.log2(math.e)
LANES = 128
ROW_TILE = 1024
ATT_SUB = 4
S5_GBLK = 8
S5_TC = 8
S5_HGRP = 2
VMEM_LIMIT = 56 * 1024 * 1024


def _cparams(sem, flags=None):
    return pltpu.CompilerParams(dimension_semantics=sem, vmem_limit_bytes=VMEM_LIMIT, flags=flags)


def _silu(x):
    h = 0.5 * x
    return h + h * jnp.tanh(h)


def _adaln_kernel(c_ref, w_ref, b_ref, o_ref):
    c = c_ref[...]
    s = _silu(c).astype(bf16)
    o_ref[...] = jnp.dot(s, w_ref[...].astype(bf16), preferred_element_type=f32) + b_ref[...]


def _adaln(cvecs, w, b):
    r, d = cvecs.shape
    n = w.shape[1]
    tn = 1024
    return pl.pallas_call(
        _adaln_kernel,
        out_shape=jax.ShapeDtypeStruct((r, n), f32),
        grid=(n // tn,),
        in_specs=[pl.BlockSpec((r, d), lambda j: (0, 0)),
                  pl.BlockSpec((d, tn), lambda j: (0, j)),
                  pl.BlockSpec((1, tn), lambda j: (0, j))],
        out_specs=pl.BlockSpec((r, tn), lambda j: (0, j)),
        compiler_params=_cparams(("arbitrary",)),
        name="adaln",
    )(cvecs, w, b)


def _inproj_kernel(segs, tsegs, n_w, has_rope, has_pre, x_ref, shift_ref, scale_ref, nw_ref, *rest):
    if has_pre:
        a_ref, b_ref, wo_ref, gate_ref = rest[:4]
        rest = rest[4:]
    w_refs, rest = rest[:n_w], rest[n_w:]
    if tsegs:
        wt_ref, rest = rest[0], rest[1:]
    if has_rope:
        cos_ref, sina_ref, sinb_ref = rest[:3]
        rest = rest[3:]
    if has_pre:
        x1_ref, rest = rest[0], rest[1:]
    outs = rest[:len(segs)]
    touts = rest[len(segs):len(segs) + len(tsegs)]
    rest = rest[len(segs) + len(tsegs):]
    slab_s = rest[0] if rest else None
    x = x_ref[...]
    if has_pre:
        ka = a_ref.shape[1]
        acc = jnp.dot(a_ref[...], wo_ref[:ka, :], preferred_element_type=f32)
        acc = acc + jnp.dot(b_ref[...], wo_ref[ka:, :], preferred_element_type=f32)
        x = x + gate_ref[...] * acc
        x1_ref[...] = x
    ms = jnp.mean(x * x, axis=-1, keepdims=True)
    h = (x * lax.rsqrt(ms + EPS)) * nw_ref[...]
    h = h * (1.0 + scale_ref[...]) + shift_ref[...]
    hb = h.astype(bf16)
    for (start, width, _), o_ref in zip(tsegs, touts):
        acc_t = lax.dot_general(wt_ref[start:start + width, :], hb, (((1,), (1,)), ((), ())),
                                preferred_element_type=f32)
        o_ref[...] = acc_t.astype(o_ref.dtype)
    products = {}
    for (widx, width, rope, _, chunked), o_ref in zip(segs, outs):
        if widx not in products:
            products[widx] = jnp.dot(hb, w_refs[widx][...], preferred_element_type=f32)
        acc = products[widx]
        if chunked:
            tm = acc.shape[0]
            hl = S5_HGRP * S5_GROUP_CH
            per = LANES // hl
            pos = lax.broadcasted_iota(jnp.int32, (1, LANES), 1) // hl
            for j in range(width // LANES):
                slab_s[j] = acc[:, j * LANES:(j + 1) * LANES]
            for j in range(width // LANES):
                rows = [slab_s[j, pl.ds(s, tm // S5_TC, stride=S5_TC), :] for s in range(S5_TC)]
                for p in range(S5_TC // per):
                    for h in range(per):
                        out = None
                        for i in range(per):
                            piece = rows[p * per + i]
                            if i != h:
                                piece = pltpu.roll(piece, ((i - h) * hl) % LANES, axis=1)
                            out = piece if out is None else jnp.where(pos == i, piece, out)
                        c0 = h * S5_TC * hl + p * LANES
                        o_ref[j, :, c0:c0 + LANES] = out.astype(o_ref.dtype)
            continue
        if rope is not None and not has_rope:
            acc = acc * rope
        elif rope is not None:
            rep = width // LANES
            cos = jnp.concatenate([cos_ref[...]] * rep, axis=1) * rope
            sina = jnp.concatenate([sina_ref[...]] * rep, axis=1) * rope
            sinb = jnp.concatenate([sinb_ref[...]] * rep, axis=1) * rope
            half = ATT_HEAD_DIM // 2
            up = pltpu.roll(acc, width - half, axis=1)
            dn = pltpu.roll(acc, half, axis=1)
            acc = acc * cos + up * sina + dn * sinb
        o_ref[...] = acc.astype(o_ref.dtype)


def _inproj(x, shift, scale, nw, w, segs, rows_per_mod, rope_tabs=None, wt=None, tsegs=(), pre=None):
    m, d = x.shape
    tm = min(ROW_TILE, rows_per_mod)
    per = rows_per_mod // tm
    nmod = shift.shape[0]
    mod_idx = (lambda i: (i // per, 0, 0)) if nmod > 1 else (lambda i: (0, 0, 0))
    in_specs = [pl.BlockSpec((tm, d), lambda i: (i, 0)),
                pl.BlockSpec((None, 1, d), mod_idx),
                pl.BlockSpec((None, 1, d), mod_idx),
                pl.BlockSpec((1, d), lambda i: (0, 0))]
    args = [x, shift, scale, nw]
    if pre is not None:
        a, b, w_out, gate = pre
        gate_idx = (lambda i: (i // per, 0, 0)) if gate.shape[0] > 1 else (lambda i: (0, 0, 0))
        in_specs += [pl.BlockSpec((tm, a.shape[1]), lambda i: (i, 0)),
                     pl.BlockSpec((tm, b.shape[1]), lambda i: (i, 0)),
                     pl.BlockSpec(w_out.shape, lambda i: (0, 0), pipeline_mode=pl.Buffered(1)),
                     pl.BlockSpec((None, 1, d), gate_idx)]
        args += [a, b, w_out, gate]
    in_specs += [pl.BlockSpec(wi.shape, lambda i: (0, 0), pipeline_mode=pl.Buffered(1)) for wi in w]
    args += list(w)
    if tsegs:
        in_specs.append(pl.BlockSpec(wt.shape, lambda i: (0, 0), pipeline_mode=pl.Buffered(1)))
        args.append(wt)
    if rope_tabs is not None:
        for t in rope_tabs:
            in_specs.append(pl.BlockSpec((tm, LANES), lambda i: (i % per, 0)))
            args.append(t)
    out_shape, out_specs = [], []
    if pre is not None:
        out_shape.append(jax.ShapeDtypeStruct((m, d), f32))
        out_specs.append(pl.BlockSpec((tm, d), lambda i: (i, 0)))
    for sg in segs:
        if sg[4]:
            out_shape.append(jax.ShapeDtypeStruct((sg[1] // LANES, m // S5_TC, S5_TC * LANES), sg[3]))
            out_specs.append(pl.BlockSpec((sg[1] // LANES, tm // S5_TC, S5_TC * LANES), lambda i: (0, i, 0)))
        else:
            out_shape.append(jax.ShapeDtypeStruct((m, sg[1]), sg[3]))
            out_specs.append(pl.BlockSpec((tm, sg[1]), lambda i: (i, 0)))
    for _, width, dtp in tsegs:
        out_shape.append(jax.ShapeDtypeStruct((width, m), dtp))
        out_specs.append(pl.BlockSpec((width, tm), lambda i: (0, i)))
    chunked_w = [sg[1] for sg in segs if sg[4]]
    scratch = [pltpu.VMEM((max(chunked_w) // LANES, tm, LANES), f32)] if chunked_w else []
    return pl.pallas_call(
        functools.partial(_inproj_kernel, tuple(segs), tuple(tsegs), len(w), rope_tabs is not None,
                          pre is not None),
        out_shape=out_shape,
        grid=(m // tm,),
        in_specs=in_specs,
        out_specs=out_specs,
        scratch_shapes=scratch,
        compiler_params=_cparams(("parallel",)),
        name="inproj",
    )(*args)


def _rope_tables(seq_len):
    rows = seq_len // GRID_W
    row = jnp.repeat(jnp.arange(rows, dtype=f32), GRID_W)
    col = jnp.tile(jnp.arange(GRID_W, dtype=f32), rows)
    n_freq = ATT_HEAD_DIM // 4
    inv = ROPE_THETA ** (-jnp.arange(n_freq, dtype=f32) / n_freq)
    ang = jnp.concatenate([row[:, None] * inv, col[:, None] * inv], axis=-1)
    cos, sin = jnp.cos(ang), jnp.sin(ang)
    zero = jnp.zeros_like(sin)
    cos_h = jnp.concatenate([cos, cos], axis=-1)
    sina_h = jnp.concatenate([-sin, zero], axis=-1)
    sinb_h = jnp.concatenate([zero, sin], axis=-1)
    two = lambda t: jnp.concatenate([t, t], axis=-1)
    return two(cos_h), two(sina_h), two(sinb_h)


SSD_PACK = 32


def _split3(x):
    hi = x.astype(bf16)
    r1 = x - hi.astype(f32)
    mid = r1.astype(bf16)
    lo = (r1 - mid.astype(f32)).astype(bf16)
    return hi, mid, lo


def _pack3(x):
    hi, mid, lo = _split3(x)
    lane = lax.broadcasted_iota(jnp.int32, x.shape, 1)
    mid_r = pltpu.roll(mid.astype(f32), SSD_PACK, axis=1)
    lo_r = pltpu.roll(lo.astype(f32), 2 * SSD_PACK, axis=1)
    packed = jnp.where(lane < SSD_PACK, hi.astype(f32),
                       jnp.where(lane < 2 * SSD_PACK, mid_r,
                                 jnp.where(lane < 3 * SSD_PACK, lo_r, 0.0)))
    return packed.astype(bf16)


def _ssd_selectors():
    k = np.arange(LANES)
    src = np.where(k < 3 * SSD_PACK, k % SSD_PACK, -1)
    col_blk = np.arange(SSD_PACK * SSD_CHUNK) // SSD_CHUNK
    sel_bc = (src[:, None] == col_blk[None, :])
    head = np.arange(SSD_INNER) // SSD_HEAD_DIM
    sel_f = (src[:, None] == head[None, :])
    sel_b = (src[:, None] == (head + SSD_HEADS)[None, :])
    tri3 = np.tile(np.tril(np.ones((SSD_CHUNK, SSD_CHUNK))), (1, 3))
    rows = np.arange(SSD_CHUNK)[:, None]
    cols = np.arange(SSD_CHUNK + 32)[None, :]
    half = SSD_CONV // 2
    shift = np.concatenate([cols == rows + 16 + d for d in range(-half, half + 1) if d != 0], axis=0)
    as_bf = lambda a: jnp.asarray(a.astype(np.float32), dtype=bf16)
    return as_bf(sel_bc), as_bf(sel_f), as_bf(sel_b), as_bf(tri3), as_bf(shift)


def _ssd_kernel(seq_len, xbc_ref, dt_ref, z_ref, cw_ref, cb_ref, dtb_ref, alog_ref, dskip_ref, nw_ref,
                selbc_ref, self_ref, selb_ref, tri3_ref, shift_ref, h0_ref, out_ref, hfin_ref,
                xs_s, bc_s, dt_s, y_s, hf_s, hb_s, win_s):
    q = SSD_CHUNK
    nc = seq_len // q
    halo = 16
    H, P, N = SSD_HEADS, SSD_HEAD_DIM, SSD_STATE
    gw = (H // SSD_GROUPS) * P
    a2_row = -jnp.exp(alog_ref[...]) * math.log2(math.e)

    def conv_chunk(c):
        r0 = pl.multiple_of(c * q, q)
        pstart = pl.multiple_of(jnp.maximum(r0 - halo, 0), halo)
        nstart = pl.multiple_of(jnp.minimum(r0 + q, seq_len - halo), halo)
        zero = jnp.zeros((), bf16)
        win_s[0:halo, :] = jnp.where(c > 0, xbc_ref[pl.ds(pstart, halo), :], zero)
        win_s[halo:halo + q, :] = xbc_ref[pl.ds(r0, q), :]
        win_s[halo + q:, :] = jnp.where(c < nc - 1, xbc_ref[pl.ds(nstart, halo), :], zero)
        taps = [k for k in range(SSD_CONV) if k != SSD_CONV // 2]
        cw = 2 * LANES
        for j in range(SSD_XBC // cw):
            cs = slice(j * cw, (j + 1) * cw)
            sh = jnp.dot(shift_ref[...], win_s[:, cs], preferred_element_type=f32)
            acc = cb_ref[:, cs] + win_s[halo:halo + q, cs].astype(f32) * cw_ref[SSD_CONV // 2:SSD_CONV // 2 + 1, cs]
            for n, k in enumerate(taps):
                acc = acc + sh[n * q:(n + 1) * q, :] * cw_ref[k:k + 1, cs]
            act = _silu(acc)
            if j < SSD_INNER // cw:
                xs_s[pl.ds(r0, q), cs] = act
            else:
                bc_s[pl.ds(r0, q), j * cw - SSD_INNER:(j + 1) * cw - SSD_INNER] = act.astype(bf16)
        dt_s[pl.ds(r0, q), :] = jax.nn.softplus(dt_ref[pl.ds(r0, q), :] + dtb_ref[...])

    conv_chunk(0)
    hf_s[...] = h0_ref[0]
    hb_s[...] = h0_ref[1]

    ri = lax.broadcasted_iota(jnp.int32, (q, q), 0)
    ci = lax.broadcasted_iota(jnp.int32, (q, q), 1)
    lower = ri >= ci
    upper = ci >= ri
    lane = lax.broadcasted_iota(jnp.int32, (q, LANES), 1)
    lo_half = lane < P

    def cumsums(dt):
        dta = dt * a2_row
        cf = jnp.dot(tri3_ref[...], jnp.concatenate(_split3(dta), axis=0), preferred_element_type=f32)
        rb = cf[q - 1:q, :] - cf + dta
        return cf, rb

    def load_chunk(r0):
        dt = dt_s[pl.ds(r0, q), :]
        xs = xs_s[pl.ds(r0, q), :]
        bcv = bc_s[pl.ds(r0, q), :]
        bmat = [bcv[:, g * N:(g + 1) * N] for g in range(SSD_GROUPS)]
        cmat = [bcv[:, SSD_BC + g * N:SSD_BC + (g + 1) * N] for g in range(SSD_GROUPS)]
        return dt, xs, bmat, cmat

    def inter_chunk(h_s, sel_ref, decay, weight, xs, bmat, cmat, dec_idx):
        ew = jnp.dot(jnp.concatenate([_pack3(decay), _pack3(weight)], axis=0), sel_ref[...],
                     preferred_element_type=f32)
        e_x, w_x = ew[:q], ew[q:]
        hb_ = h_s[...].astype(bf16)
        yoff = jnp.concatenate(
            [jnp.dot(cmat[g], hb_[:, g * gw:(g + 1) * gw], preferred_element_type=f32)
             for g in range(SSD_GROUPS)], axis=1)
        xw = (xs * w_x).astype(bf16)
        dec_row = e_x[dec_idx:dec_idx + 1, :]
        for g in range(SSD_GROUPS):
            gs = slice(g * gw, (g + 1) * gw)
            bt = jnp.transpose(bmat[g].astype(f32)).astype(bf16)
            upd = jnp.dot(bt, xw[:, gs], preferred_element_type=f32)
            h_s[:, gs] = h_s[:, gs] * dec_row[:, gs] + upd
        return yoff * e_x

    def finish(r0, y, xs):
        yy = y + xs * dskip_ref[...]
        zz = z_ref[pl.ds(r0, q), :].astype(f32)
        gated = yy * _silu(zz)
        ms = jnp.mean(gated * gated, axis=-1, keepdims=True)
        out_ref[pl.ds(r0, q), :] = (gated * lax.rsqrt(ms + EPS) * nw_ref[...]).astype(out_ref.dtype)

    def fwd_chunk(c, second_half):
        r0 = pl.multiple_of(c * q, q)
        dt, xs, bmat, cmat = load_chunk(r0)
        cf, rb = cumsums(dt)
        pcol = jnp.where(lane < H, cf, rb)
        bcast = jnp.dot(_pack3(pcol), selbc_ref[...], preferred_element_type=f32)
        prow = jnp.transpose(pcol - jnp.log2(dt))
        cbm = [lax.dot_general(cmat[g], bmat[g], (((1,), (1,)), ((), ())), preferred_element_type=f32)
               for g in range(SSD_GROUPS)]
        xsb = xs.astype(bf16)
        zero_b = jnp.zeros((), bf16)
        ypairs = []
        for k in range(H // 2):
            mats = []
            for h in (2 * k, 2 * k + 1):
                g = h // (H // SSD_GROUPS)
                hb_ = H + h
                segf = bcast[:, h * q:(h + 1) * q] - prow[h:h + 1, :]
                segb = bcast[:, hb_ * q:(hb_ + 1) * q] - prow[hb_:hb_ + 1, :]
                df = jnp.exp2(jnp.where(lower, segf, NEG_INF))
                db = jnp.exp2(jnp.where(upper, segb, NEG_INF))
                mats.append((cbm[g] * (df + db)).astype(bf16))
            xp = xsb[:, k * LANES:(k + 1) * LANES]
            xbd = jnp.concatenate([jnp.where(lo_half, xp, zero_b), jnp.where(lo_half, zero_b, xp)], axis=0)
            ypairs.append(jnp.dot(jnp.concatenate(mats, axis=1), xbd, preferred_element_type=f32))
        y = jnp.concatenate(ypairs, axis=1)
        wfa = jnp.exp2(cf[q - 1:q, :] - cf) * dt
        y = y + inter_chunk(hf_s, self_ref, jnp.exp2(cf), wfa, xs, bmat, cmat, q - 1)
        if second_half:
            finish(r0, y_s[pl.ds(r0, q), :] + y, xs)
        else:
            y_s[pl.ds(r0, q), :] = y

    def bwd_chunk(c, second_half):
        r0 = pl.multiple_of(c * q, q)
        dt, xs, bmat, cmat = load_chunk(r0)
        _, rb = cumsums(dt)
        wba = jnp.exp2(rb[0:1, :] - rb) * dt
        y = inter_chunk(hb_s, selb_ref, jnp.exp2(rb), wba, xs, bmat, cmat, 0)
        if second_half:
            finish(r0, y_s[pl.ds(r0, q), :] + y, xs)
        else:
            y_s[pl.ds(r0, q), :] = y

    half = nc // 2
    conv_chunk(nc - 1)

    def first_half(i, carry):
        fwd_chunk(i, False)
        bwd_chunk(nc - 1 - i, False)
        conv_chunk(i + 1)
        conv_chunk(nc - 2 - i)
        return carry

    def second_half(i, carry):
        fwd_chunk(i, True)
        bwd_chunk(nc - 1 - i, True)
        return carry

    lax.fori_loop(0, half - 1, first_half, 0)
    fwd_chunk(jnp.int32(half - 1), False)
    bwd_chunk(jnp.int32(half), False)
    lax.fori_loop(half, nc, second_half, 0)
    hfin_ref[0] = hf_s[...]
    hfin_ref[1] = hb_s[...]


def _ssd(xbc, dt, z, conv_w, conv_b, dtb, alog, dskip, nw, h0, bsz, seq_len):
    assert seq_len % (2 * SSD_CHUNK) == 0, "the two recurrences meet in the middle: even chunk count"
    one = pl.Buffered(1)
    seq = lambda w: pl.BlockSpec((seq_len, w), lambda b: (b, 0), pipeline_mode=one)
    const = lambda r, w: pl.BlockSpec((r, w), lambda b: (0, 0))
    st = pl.BlockSpec((None, 2, SSD_STATE, SSD_INNER), lambda b: (b, 0, 0, 0))
    sels = _ssd_selectors()
    return pl.pallas_call(
        functools.partial(_ssd_kernel, seq_len),
        out_shape=[jax.ShapeDtypeStruct((bsz * seq_len, SSD_INNER), bf16),
                   jax.ShapeDtypeStruct((bsz, 2, SSD_STATE, SSD_INNER), f32)],
        grid=(bsz,),
        in_specs=[pl.BlockSpec((seq_len, SSD_XBC), lambda b: (b, 0)), seq(LANES),
                  pl.BlockSpec((seq_len, SSD_INNER), lambda b: (b, 0)),
                  const(8, SSD_XBC), const(1, SSD_XBC), const(1, LANES), const(1, LANES),
                  const(1, SSD_INNER), const(1, SSD_INNER)]
                 + [const(*s.shape) for s in sels] + [st],
        out_specs=[seq(SSD_INNER), st],
        scratch_shapes=[pltpu.VMEM((seq_len, SSD_INNER), f32),
                        pltpu.VMEM((seq_len, 2 * SSD_BC), bf16),
                        pltpu.VMEM((seq_len, LANES), f32),
                        pltpu.VMEM((seq_len, SSD_INNER), f32),
                        pltpu.VMEM((SSD_STATE, SSD_INNER), f32),
                        pltpu.VMEM((SSD_STATE, SSD_INNER), f32),
                        pltpu.VMEM((SSD_CHUNK + 32, SSD_XBC), bf16)],
        compiler_params=_cparams(("parallel",)),
        name="ssd",
    )(xbc, dt, z, conv_w, conv_b, dtb, alog, dskip, nw, *sels, h0)


def _attn_kernel(n_blocks, nsub, local, q_ref, g_ref, k_ref, kc_ref, *rest):
    nv = nsub + 2 if local else 0
    v_refs = rest[:nv]
    vc_ref, sink_ref, o_ref, s_s = rest[nv:]
    t = ATT_BLOCK
    i0 = pl.program_id(1) * nsub
    rpk = ATT_HEADS // ATT_KV_HEADS
    lane = lax.broadcasted_iota(jnp.int32, (t, LANES), 1)
    lo_half = lane < ATT_HEAD_DIM
    zero_b = jnp.zeros((), bf16)
    kl = lax.broadcasted_iota(jnp.int32, (t, t), 0)
    ql = lax.broadcasted_iota(jnp.int32, (t, t), 1)
    for sub in range(nsub):
        i = i0 + sub
        qv = q_ref[sub * t:(sub + 1) * t, :]
        if local:
            p0 = pl.multiple_of(jnp.maximum(i - 1, 0) * t, t)
            c0 = pl.multiple_of(i * t, t)
            n0 = pl.multiple_of(jnp.minimum(i + 1, n_blocks - 1) * t, t)
            bias_prev = jnp.where((kl >= ql) & (i > 0), 0.0, NEG_INF)
            bias_next = jnp.where((kl <= ql) & (i < n_blocks - 1), 0.0, NEG_INF)
            bias_prev = jnp.concatenate([bias_prev] * rpk, axis=1)
            bias_next = jnp.concatenate([bias_next] * rpk, axis=1)
        for j in range(ATT_KV_HEADS):
            ls = slice(j * LANES, (j + 1) * LANES)
            if local:
                kk = jnp.concatenate([k_ref[pl.ds(p0, t), ls], k_ref[pl.ds(c0, t), ls],
                                      k_ref[pl.ds(n0, t), ls], kc_ref[:, ls]], axis=0)
            else:
                kk = kc_ref[:, ls]
            pieces = []
            for r in range(rpk):
                hq = j * rpk + r
                qp = qv[:, (hq // 2) * LANES:(hq // 2 + 1) * LANES]
                keep = lo_half if hq % 2 == 0 else jnp.logical_not(lo_half)
                pieces.append(jnp.where(keep, qp, zero_b))
            q4 = jnp.concatenate(pieces, axis=0)
            s = lax.dot_general(kk, q4, (((1,), (1,)), ((), ())), preferred_element_type=f32)
            if local:
                s = jnp.concatenate([s[:t] + bias_prev, s[t:2 * t], s[2 * t:3 * t] + bias_next, s[3 * t:]],
                                    axis=0)
            s_s[sub, j] = s
    for sub in range(nsub):
        outs = []
        for j in range(ATT_KV_HEADS):
            vs = slice(j * ATT_HEAD_DIM, (j + 1) * ATT_HEAD_DIM)
            if local:
                vvt = jnp.concatenate([v_refs[sub + k][vs, :] for k in range(3)] + [vc_ref[vs, :]], axis=1)
            else:
                vvt = vc_ref[vs, :]
            sk = jnp.concatenate([jnp.full((1, t), sink_ref[j * rpk + r] * LOG2E, f32) for r in range(rpk)],
                                 axis=1)
            s = s_s[sub, j]
            m = jnp.maximum(jnp.max(s, axis=0, keepdims=True), sk)
            p = jnp.exp2(s - m)
            den = jnp.sum(p, axis=0, keepdims=True) + jnp.exp2(sk - m)
            vvt = jnp.concatenate([vvt, vvt], axis=0)
            ot = jnp.dot(vvt, p.astype(bf16), preferred_element_type=f32) / den
            o4 = [jnp.transpose(ot[:, r * t:(r + 1) * t]) for r in range(rpk)]
            outs.append(jnp.where(lo_half, o4[0], o4[1]))
            outs.append(jnp.where(lo_half, o4[2], o4[3]))
        o = jnp.concatenate(outs, axis=1)
        gv = g_ref[sub * t:(sub + 1) * t, :].astype(f32)
        o_ref[sub * t:(sub + 1) * t, :] = (o * _silu(gv)).astype(o_ref.dtype)


def _attention(q, g, k, vt, kc, vct, sink, bsz, seq_len, local):
    t = ATT_BLOCK
    nb = seq_len // t
    nsub = max(s for s in range(1, ATT_SUB + 1) if nb % s == 0)
    n_ctx = kc.shape[0] // bsz
    kw = ATT_KV_HEADS * LANES
    vw = ATT_KVW
    blk = pl.BlockSpec((nsub * t, ATT_Q), lambda b, i: (b * (nb // nsub) + i, 0))
    full = lambda n: pl.BlockSpec((n, kw), lambda b, i: (b, 0))
    vblk = lambda off: pl.BlockSpec((vw, t), lambda b, i: (0, b * nb + jnp.clip(i * nsub + off, 0, nb - 1)))
    vspecs = [vblk(off) for off in range(-1, nsub + 1)] if local else []
    return pl.pallas_call(
        functools.partial(_attn_kernel, nb, nsub, local),
        out_shape=jax.ShapeDtypeStruct((bsz * seq_len, ATT_Q), bf16),
        grid=(bsz, nb // nsub),
        in_specs=[blk, blk, full(k.shape[0] // bsz), full(n_ctx)] + vspecs
                 + [pl.BlockSpec((vw, n_ctx), lambda b, i: (0, b)), pl.BlockSpec(memory_space=pltpu.SMEM)],
        out_specs=blk,
        scratch_shapes=[pltpu.VMEM((nsub, ATT_KV_HEADS, (3 * t if local else 0) + n_ctx,
                                    (ATT_HEADS // ATT_KV_HEADS) * t), f32)],
        compiler_params=_cparams(("parallel", "arbitrary")),
        name="attention",
    )(q, g, k, kc, *([vt] * len(vspecs)), vct, sink)


def _s5_disc_kernel(lre_ref, lim_ref, ls_ref, bre_ref, bim_ref, abre_ref, abim_ref, bbre_ref, bbim_ref):
    lam_re = lre_ref[...]
    lam_im = lim_ref[...]
    dt = jnp.exp(ls_ref[...])
    mag = jnp.exp(lam_re * dt)
    ab_re = mag * jnp.cos(lam_im * dt)
    ab_im = mag * jnp.sin(lam_im * dt)
    num_re, num_im = ab_re - 1.0, ab_im
    den = lam_re * lam_re + lam_im * lam_im
    coef_re = (num_re * lam_re + num_im * lam_im) / den
    coef_im = (num_im * lam_re - num_re * lam_im) / den
    b_re, b_im = bre_ref[...], bim_ref[...]
    abre_ref[...] = ab_re
    abim_ref[...] = ab_im
    bbre_ref[...] = coef_re * b_re - coef_im * b_im
    bbim_ref[...] = coef_re * b_im + coef_im * b_re


def _s5_discretise(lam_re, lam_im, log_step, b_re, b_im):
    g, n, cg = b_re.shape
    exp = lambda t: jnp.repeat(t.reshape(2 * g, n), cg, axis=1)
    ls = jnp.broadcast_to(log_step.reshape(2 * g, 1), (2 * g, n * cg))
    bb = lambda t: jnp.tile(t.reshape(g, n * cg), (2, 1))
    shp = jax.ShapeDtypeStruct((2 * g, n * cg), f32)
    ab_re, ab_im, bb_re, bb_im = pl.pallas_call(
        _s5_disc_kernel, out_shape=[shp] * 4, name="s5_disc",
    )(exp(lam_re), exp(lam_im), ls, bb(b_re), bb(b_im))
    first = lambda t: t.reshape(2, g, n, cg)[..., 0]
    full = lambda t: t.reshape(2, g, n, cg)
    return first(ab_re), first(ab_im), full(bb_re), full(bb_im)


def _cmul(ar, ai, br, bi):
    return ar * br - ai * bi, ar * bi + ai * br


def _s5_kernel(n_lat, n_ctx, *refs):
    tc = S5_TC
    xl, xc, arow_ref, acol_ref, bbd_ref, cbd_ref, y_ref, wyz_s, ws_s, sl_s, sc_s = refs
    hw = S5_HGRP * S5_STATE
    hl = S5_HGRP * S5_GROUP_CH
    per = LANES // hl
    xw = tc * hl
    nsl = hw // LANES
    nb = 8
    pos = lax.broadcasted_iota(jnp.int32, (1, LANES), 1) // hl

    def pick(arrs):
        out = arrs[0]
        for i in range(1, per):
            out = jnp.where(pos == i, arrs[i], out)
        return out

    def powers(re, im, n):
        out = [(jnp.ones_like(re), jnp.zeros_like(im))]
        for _ in range(n):
            out.append(_cmul(out[-1][0], out[-1][1], re, im))
        return out

    steps = lambda h, p: [p * per + (i - h) % per for i in range(per)]

    prow = []
    for h in range(per):
        pr = [powers(arow_ref[h, 2 * d:2 * d + 1, :], arow_ref[h, 2 * d + 1:2 * d + 2, :], tc) for d in range(2)]
        prow.append(pr)
        for s in range(tc):
            for d, k in ((0, tc - 1 - s), (1, s)):
                wr, wi = _cmul(bbd_ref[2 * d, h], bbd_ref[2 * d + 1, h], *pr[d][k])
                ws_s[h, s * hl:(s + 1) * hl, d * 2 * hw:d * 2 * hw + hw] = wr.astype(bf16)
                ws_s[h, s * hl:(s + 1) * hl, d * 2 * hw + hw:(d + 1) * 2 * hw] = wi.astype(bf16)
        crhs = jnp.concatenate([cbd_ref[0, h], -cbd_ref[1, h]], axis=0).astype(bf16)
        kall = [jnp.dot(ws_s[h, :, d * 2 * hw:(d + 1) * 2 * hw], crhs, preferred_element_type=f32)
                for d in range(2)]
        kf = [kall[0][(tc - 1 - k) * hl:(tc - k) * hl] for k in range(tc)]
        kb = [kall[1][k * hl:(k + 1) * hl] for k in range(tc)]
        lag = lambda s, t: kf[t - s] if t > s else (kb[s - t] if t < s else kf[0] + kb[0])
        for s in range(tc):
            for p in range(tc // per):
                wyz_s[h, s * hl:(s + 1) * hl, p * LANES:(p + 1) * LANES] = (
                    pick([lag(s, t) for t in steps(h, p)]).astype(bf16))
        for d in range(2):
            pc = powers(acol_ref[h, 2 * d], acol_ref[h, 2 * d + 1], tc)
            kk = (lambda t: t + 1) if d == 0 else (lambda t: tc - t)
            r0 = xw + d * 2 * hw
            for p in range(tc // per):
                ar = pick([pc[kk(t)][0] for t in steps(h, p)])
                ai = pick([pc[kk(t)][1] for t in steps(h, p)])
                dre, dim_ = _cmul(cbd_ref[0, h], cbd_ref[1, h], ar, ai)
                wyz_s[h, r0:r0 + hw, p * LANES:(p + 1) * LANES] = dre.astype(bf16)
                wyz_s[h, r0 + hw:r0 + 2 * hw, p * LANES:(p + 1) * LANES] = (-dim_).astype(bf16)

    def rows_of(x_ref, b, n, h):
        return x_ref[b * n:(b + 1) * n, h * xw:(h + 1) * xw]

    def inject(x_ref, s_ref, n):
        for h in range(per):
            sall = jnp.dot(x_ref[:, h * xw:(h + 1) * xw], ws_s[h], preferred_element_type=f32)
            for b in range(nb):
                for k in range(4 * nsl):
                    s_ref[h * 4 * nsl + k, pl.ds(b, n, stride=nb), :] = sall[b * n:(b + 1) * n,
                                                                             k * LANES:(k + 1) * LANES]

    at = [[[tuple(jnp.broadcast_to(p[:, k * LANES:(k + 1) * LANES], (nb, LANES)) for p in prow[h][d][tc])
            for k in range(nsl)] for d in range(2)] for h in range(per)]

    def scan(s_ref, n, init):
        def step(i, carry):
            idxs = [pl.ds(pl.multiple_of(i * nb, nb), nb), pl.ds(pl.multiple_of((n - 1 - i) * nb, nb), nb)]
            chains = [(h, d, k) for h in range(per) for d in range(2) for k in range(nsl)]
            slab = lambda h, d, k: h * 4 * nsl + d * 2 * nsl + k
            inj = [(s_ref[slab(h, d, k), idxs[d], :], s_ref[slab(h, d, k) + nsl, idxs[d], :])
                   for h, d, k in chains]
            new = []
            for c, (h, d, k) in enumerate(chains):
                hr, hi = carry[2 * c], carry[2 * c + 1]
                s_ref[slab(h, d, k), idxs[d], :] = hr
                s_ref[slab(h, d, k) + nsl, idxs[d], :] = hi
                ar, ai = at[h][d][k]
                new += [ar * hr - ai * hi + inj[c][0], ar * hi + ai * hr + inj[c][1]]
            return tuple(new)
        return lax.fori_loop(0, n, step, init, unroll=4)

    inject(xc, sc_s, n_ctx)
    h_ctx = scan(sc_s, n_ctx, tuple(jnp.zeros((nb, LANES), f32) for _ in range(per * 4 * nsl)))
    inject(xl, sl_s, n_lat)
    scan(sl_s, n_lat, h_ctx)

    for b in range(nb):
        yh = []
        for h in range(per):
            hin = jnp.concatenate([sl_s[h * 4 * nsl + k, pl.ds(b, n_lat, stride=nb), :]
                                   for k in range(4 * nsl)], axis=1).astype(bf16)
            yh.append(jnp.dot(jnp.concatenate([rows_of(xl, b, n_lat, h), hin], axis=1), wyz_s[h],
                              preferred_element_type=f32))
        for p in range(tc // per):
            grp = [yh[h][:, p * LANES:(p + 1) * LANES] for h in range(per)]
            for r in range(per):
                row = pick([grp[(i - r) % per] for i in range(per)])
                if r:
                    row = pltpu.roll(row, (per - r) * hl, axis=1)
                y_ref[pl.ds(b * n_lat * tc + p * per + r, n_lat, stride=tc), :] = row


def _s5_mix(u, u_c, arow, acol, bbd, cbd, bsz, seq_len, n_ctx_tok):
    assert bsz == 8, "the chunk recurrence puts the batch on the 8 sublanes"
    tc = S5_TC
    n_lat, n_ctx = seq_len // tc, n_ctx_tok // tc
    nblk = S5_GROUPS // S5_GBLK
    hw = S5_HGRP * S5_STATE
    hl = S5_HGRP * S5_GROUP_CH
    per = LANES // hl
    nsl = hw // LANES
    one = pl.Buffered(1)
    xspec = lambda rows: pl.BlockSpec((None, rows, tc * LANES), lambda g: (g, 0, 0))
    return pl.pallas_call(
        functools.partial(_s5_kernel, n_lat, n_ctx),
        out_shape=jax.ShapeDtypeStruct((nblk, bsz * seq_len, LANES), f32),
        grid=(nblk,),
        in_specs=[xspec(bsz * n_lat), xspec(bsz * n_ctx),
                  pl.BlockSpec((per, 4, hw), lambda g: (g, 0, 0), pipeline_mode=one),
                  pl.BlockSpec((per, 4, hw, LANES), lambda g: (g, 0, 0, 0), pipeline_mode=one),
                  pl.BlockSpec((4, per, hl, hw), lambda g: (0, g, 0, 0), pipeline_mode=one),
                  pl.BlockSpec((2, per, hw, LANES), lambda g: (0, g, 0, 0), pipeline_mode=one)],
        out_specs=pl.BlockSpec((None, bsz * seq_len, LANES), lambda g: (g, 0, 0)),
        scratch_shapes=[pltpu.VMEM((per, tc * hl + 4 * hw, tc * hl), bf16),
                        pltpu.VMEM((per, tc * hl, 4 * hw), bf16),
                        pltpu.VMEM((per * 4 * nsl, bsz * n_lat, LANES), f32),
                        pltpu.VMEM((per * 4 * nsl, bsz * n_ctx, LANES), f32)],
        compiler_params=_cparams(("arbitrary",)),
        name="s5_mix",
    )(u, u_c, arow, acol, bbd, cbd)


def _s5_block_params(ab_re, ab_im, bb_re, bb_im, c_re, c_im):
    g, n, cg = S5_GROUPS, S5_STATE, S5_GROUP_CH
    nh = g // S5_HGRP
    hw = S5_HGRP * n
    arow = jnp.stack([t[d].reshape(nh, hw) for d in range(2) for t in (ab_re, ab_im)], axis=1)
    acol = jnp.broadcast_to(arow[..., None], (nh, 4, hw, LANES))

    def blockdiag(t, rows_per, cols_per):
        tiled = jnp.concatenate([t] * S5_HGRP, axis=-1)
        r = np.arange(t.shape[-2])[:, None] // rows_per
        c = np.arange(S5_HGRP * cols_per)[None, :] // cols_per
        return jnp.where(jnp.asarray(r == c), tiled, 0.0)

    bb = jnp.stack([t[d] for d in range(2) for t in (bb_re, bb_im)], axis=0)
    bb = jnp.swapaxes(bb.reshape(4, nh, S5_HGRP, n, cg), -1, -2).reshape(4, nh, S5_HGRP * cg, n)
    bbd = blockdiag(bb, cg, n)
    cc = jnp.swapaxes(jnp.stack([c_re, c_im], axis=0).reshape(2, nh, S5_HGRP, cg, n), -1, -2)
    cbd = blockdiag(cc.reshape(2, nh, hw, cg), n, cg)
    return arow, acol, bbd, jnp.concatenate([cbd] * (LANES // (S5_HGRP * cg)), axis=-1)


def _s5_out_kernel(y_ref, u_ref, g_ref, x_ref, dskip_ref, gw_ref, gb_ref, w_ref, gate_ref,
                   fw_ref, o_ref):
    y = jnp.concatenate([y_ref[j] for j in range(y_ref.shape[0])], axis=1)
    y = y + dskip_ref[...] * u_ref[...].astype(f32)
    y = jax.nn.gelu(y)
    glu = jnp.dot(y.astype(bf16), gw_ref[...], preferred_element_type=f32) + gb_ref[...]
    y = y * jax.nn.sigmoid(glu)
    y = y * _silu(g_ref[...].astype(f32))
    x = x_ref[...] + gate_ref[...] * jnp.dot(y.astype(bf16), w_ref[...], preferred_element_type=f32)
    ms = jnp.mean(x * x, axis=-1, keepdims=True)
    o_ref[...] = x * lax.rsqrt(ms + EPS) * fw_ref[...]


def _s5_out(y, u, g, x, dskip, gw, gb, w, gate, fw, rows_per_mod):
    m, d = x.shape
    tm = min(ROW_TILE, rows_per_mod)
    per = rows_per_mod // tm
    deep = pl.Buffered(3)
    rin = lambda: pl.BlockSpec((tm, d), lambda i: (i, 0), pipeline_mode=deep)
    inner_in = [pl.BlockSpec((y.shape[0], tm, LANES), lambda i: (0, i, 0), pipeline_mode=deep),
                rin(), rin(), rin(), pl.BlockSpec((None, 1, d), lambda i: (i // per, 0, 0))]
    inner_out = [pl.BlockSpec((tm, d), lambda i: (i, 0))]

    def outer(y_hbm, u_hbm, g_hbm, x_hbm, gate_hbm, dskip_ref, gw_ref, gb_ref, w_ref, fw_ref, o_hbm):
        def body(y_blk, u_blk, g_blk, x_blk, gate_blk, o_blk):
            _s5_out_kernel(y_blk, u_blk, g_blk, x_blk, dskip_ref, gw_ref, gb_ref, w_ref, gate_blk, fw_ref, o_blk)
        pltpu.emit_pipeline(body, grid=(m // tm,), in_specs=inner_in, out_specs=inner_out)(
            y_hbm, u_hbm, g_hbm, x_hbm, gate_hbm, o_hbm)

    hbm = lambda: pl.BlockSpec(memory_space=pl.ANY)
    vmem = lambda: pl.BlockSpec(memory_space=pltpu.VMEM)
    return pl.pallas_call(
        outer,
        out_shape=jax.ShapeDtypeStruct((m, d), f32),
        in_specs=[hbm(), hbm(), hbm(), hbm(), hbm(), vmem(), vmem(), vmem(), vmem(), vmem()],
        out_specs=hbm(),
        compiler_params=pltpu.CompilerParams(vmem_limit_bytes=VMEM_LIMIT),
        name="s5_out",
    )(y, u, g, x, gate, dskip, gw, gb, w, fw)


def _even_weights(w_in):
    o = 0
    z = w_in[:, o:o + SSD_INNER]; o += SSD_INNER
    xbc = w_in[:, o:o + SSD_XBC]; o += SSD_XBC
    dt = w_in[:, o:o + 2 * SSD_HEADS]; o += 2 * SSD_HEADS
    q = w_in[:, o:o + ATT_Q]; o += ATT_Q
    k = w_in[:, o:o + ATT_KVW]; o += ATT_KVW
    v = w_in[:, o:o + ATT_KVW]; o += ATT_KVW
    g = w_in[:, o:o + ATT_Q]
    d = w_in.shape[0]
    dup = lambda t: jnp.concatenate([t.reshape(d, ATT_KV_HEADS, 1, ATT_HEAD_DIM)] * 2, axis=2).reshape(d, -1)
    dtp = jnp.pad(dt, ((0, 0), (0, LANES - 2 * SSD_HEADS)))
    cast = lambda t: t.astype(bf16)
    return [cast(t) for t in (z, xbc, q, dup(k), g, dtp)], cast(v.T)


def _even_segs(rope):
    scale = ATT_HEAD_DIM ** -0.5 * LOG2E
    widths = [(SSD_INNER, None, bf16), (SSD_XBC, None, bf16),
              (ATT_Q, scale, bf16),
              (ATT_KV_HEADS * LANES, 1.0 if rope else None, bf16),
              (ATT_Q, None, bf16), (LANES, None, f32)]
    return [(i, w, r, dtp, False) for i, (w, r, dtp) in enumerate(widths)]


def _pad_lanes(v, n=LANES):
    v = v.reshape(1, -1)
    return jnp.pad(v, ((0, 0), (0, n - v.shape[1])))


def kernel(x, c, ctx, c_ctx, e_norm_w, e_ada_w, e_ada_b, e_w_in, e_conv_w, e_conv_b, e_dt_bias,
           e_a_log, e_d_skip, e_ssd_norm_w, e_sink, e_w_out, o_norm_w, o_ada_w, o_ada_b, o_w_in,
           o_lam_re, o_lam_im, o_log_step, o_b_re, o_b_im, o_c_re, o_c_im, o_d_skip, o_glu_w,
           o_glu_b, o_w_out, final_norm_w):
    bsz, seq_len, d = x.shape
    n_ctx = ctx.shape[1]
    assert e_w_in.shape[0] == 1 and o_w_in.shape[0] == 1, "one even (SSD + attention) and one odd (S5) layer"
    xf = x.reshape(bsz * seq_len, d)
    xcf = ctx.reshape(bsz * n_ctx, d)

    cvecs = jnp.concatenate([c, c_ctx[None, :], jnp.zeros((16 - bsz - 1, d), f32)], axis=0)

    def modulation(ada_w, ada_b):
        mod = _adaln(cvecs, ada_w, ada_b.reshape(1, -1))
        parts = [mod[:, k * d:(k + 1) * d] for k in range(3)]
        lat = [p[:bsz].reshape(bsz, 1, d) for p in parts]
        cx = [p[bsz:bsz + 1].reshape(1, 1, d) for p in parts]
        return lat, cx

    (shift, scale, gate), (shift_c, scale_c, gate_c) = modulation(e_ada_w[0], e_ada_b[0])
    w_in, w_vt = _even_weights(e_w_in[0])
    nw = e_norm_w[0].reshape(1, d)
    tabs = _rope_tables(seq_len)
    vseg = [(0, ATT_KVW, bf16)]
    z, xbc, q, k, g, dt, vt = _inproj(xf, shift, scale, nw, w_in, _even_segs(True), seq_len, tabs, w_vt, vseg)
    z_c, xbc_c, q_c, k_c, g_c, dt_c, vt_c = _inproj(xcf, shift_c, scale_c, nw, w_in, _even_segs(False), bsz * n_ctx,
                                                    None, w_vt, vseg)

    conv_w = jnp.pad(e_conv_w[0], ((0, 8 - SSD_CONV), (0, 0)))
    conv_b = e_conv_b[0].reshape(1, -1)
    dtb = _pad_lanes(e_dt_bias[0])
    alog = _pad_lanes(e_a_log[0])
    dskip = jnp.repeat(e_d_skip[0], SSD_HEAD_DIM).reshape(1, -1)
    snw = e_ssd_norm_w[0].reshape(1, -1)
    h0 = jnp.zeros((bsz, 2, SSD_STATE, SSD_INNER), f32)
    ssd_c, hfin = _ssd(xbc_c, dt_c, z_c, conv_w, conv_b, dtb, alog, dskip, snw, h0, bsz, n_ctx)
    ssd_o, _ = _ssd(xbc, dt, z, conv_w, conv_b, dtb, alog, dskip, snw, hfin, bsz, seq_len)

    sink = e_sink[0]
    att = _attention(q, g, k, vt, k_c, vt_c, sink, bsz, seq_len, True)
    att_c = _attention(q_c, g_c, k_c, vt_c, k_c, vt_c, sink, bsz, n_ctx, False)
    w_out = e_w_out[0].astype(bf16)
    even_gate, even_gate_c = gate, gate_c

    (shift, scale, gate), (shift_c, scale_c, _) = modulation(o_ada_w[0], o_ada_b[0])
    w_u = o_w_in[0][:, :S5_WIDTH].astype(bf16)
    w_g = o_w_in[0][:, S5_WIDTH:].astype(bf16)
    nw = o_norm_w[0].reshape(1, d)
    x1, u, u_ch, g2 = _inproj(xf, shift, scale, nw, [w_u, w_g],
                              [(0, S5_WIDTH, None, bf16, False), (0, S5_WIDTH, None, bf16, True),
                               (1, S5_WIDTH, None, bf16, False)], seq_len,
                              pre=(ssd_o, att, w_out, even_gate))
    _, uc_ch = _inproj(xcf, shift_c, scale_c, nw, [w_u], [(0, S5_WIDTH, None, bf16, True)], bsz * n_ctx,
                       pre=(ssd_c, att_c, w_out, even_gate_c))

    ab_re, ab_im, bb_re, bb_im = _s5_discretise(o_lam_re[0], o_lam_im[0], o_log_step[0], o_b_re[0], o_b_im[0])
    arow, acol, bbd, cbd = _s5_block_params(ab_re, ab_im, bb_re, bb_im, o_c_re[0], o_c_im[0])
    y = _s5_mix(u_ch, uc_ch, arow, acol, bbd, cbd, bsz, seq_len, n_ctx)
    out = _s5_out(y, u, g2, x1, o_d_skip[0].reshape(1, -1),
                  o_glu_w[0].astype(bf16), o_glu_b[0].reshape(1, -1), o_w_out[0].astype(bf16), gate,
                  final_norm_w.reshape(1, -1), seq_len)
    return out.reshape(bsz, seq_len, d)
```
